```python
import jax, jax.numpy as jnp
from jax import lax
import numpy as np

D_MODEL = 1024
BATCH = 8
SEQ = 2048
DEPTH = 2

PLE_DIM = 256
BRANCH_WIDTH = D_MODEL // 2
N_BRANCH = 3
FOX_HEAD_DIM = 64
FOX_HEADS = BRANCH_WIDTH // FOX_HEAD_DIM
FOX_WIDTH = FOX_HEADS * FOX_HEAD_DIM
FOX_BLOCK = 128
SC_WIDTH = BRANCH_WIDTH
SC_KERNEL = 3
DN_HEAD_DIM = 128
DN_HEADS = BRANCH_WIDTH // DN_HEAD_DIM
DN_WIDTH = DN_HEADS * DN_HEAD_DIM
DN_CONV = 4
DN_CHUNK = 64
D_FF = 128 * ((8 * D_MODEL // 3 + 127) // 128)
FFN_CONV = 3
EPS = 1e-6

IN_SIZES = (3 * FOX_WIDTH, FOX_HEADS, 3 * SC_WIDTH, 3 * DN_WIDTH, DN_HEADS, DN_HEADS, DN_WIDTH, N_BRANCH * D_MODEL)
IN_WIDTH = sum(IN_SIZES)

kernel_name = 'hybrid_fox_shortconv_gdn_parallel_block'


def split_cols(t, sizes):
    offs = []
    acc = 0
    for s in sizes[:-1]:
        acc += s
        offs.append(acc)
    return jnp.split(t, offs, axis=-1)


def rmsnorm(x, gain):
    xf = x.astype(jnp.float32)
    y = xf * lax.rsqrt(jnp.mean(xf * xf, axis=-1, keepdims=True) + EPS)
    return (y * gain.astype(jnp.float32)).astype(x.dtype)


def l2norm(x):
    xf = x.astype(jnp.float32)
    return xf * lax.rsqrt(jnp.sum(xf * xf, axis=-1, keepdims=True) + EPS)


def causal_dwconv(x, w):
    k_width, chans = w.shape
    return lax.conv_general_dilated(x, w[:, None, :].astype(x.dtype), window_strides=(1,),
                                    padding=[(k_width - 1, 0)],
                                    dimension_numbers=('NWC', 'WIO', 'NWC'),
                                    feature_group_count=chans)


def forgetting_attention(q, k, v, f_logit, b_f, q_gain, k_gain):
    seq = q.shape[1]
    dh = q.shape[-1]
    q = rmsnorm(q, q_gain).transpose(0, 2, 1, 3)
    k = rmsnorm(k, k_gain).transpose(0, 2, 1, 3)
    v = v.transpose(0, 2, 1, 3)
    log_f = jax.nn.log_sigmoid(f_logit.astype(jnp.float32) + b_f.astype(jnp.float32))
    cum_f = jnp.cumsum(log_f, axis=1).transpose(0, 2, 1)
    scale = dh ** -0.5
    outs = []
    for start in range(0, seq, FOX_BLOCK):
        end = start + FOX_BLOCK
        s = jnp.einsum('bhqd,bhkd->bhqk', q[:, :, start:end], k[:, :, :end],
                       preferred_element_type=jnp.float32) * scale
        s = s + cum_f[:, :, start:end, None] - cum_f[:, :, None, :end]
        causal = jnp.arange(start, end)[:, None] >= jnp.arange(end)[None, :]
        s = jnp.where(causal, s, -jnp.inf)
        pr = jax.nn.softmax(s, axis=-1).astype(v.dtype)
        outs.append(jnp.einsum('bhqk,bhkd->bqhd', pr, v[:, :, :end]))
    return jnp.concatenate(outs, axis=1)


def gated_delta_rule(q, k, v, g, beta):
    bsz, seq, heads, dk = q.shape
    dv = v.shape[-1]
    c = DN_CHUNK
    n_chunks = seq // c

    def to_chunks(t):
        return t.astype(jnp.float32).reshape(bsz, n_chunks, c, heads, -1).transpose(1, 0, 3, 2, 4)

    qc = to_chunks(q) * dk ** -0.5
    kc = to_chunks(k)
    vc = to_chunks(v)
    gc = to_chunks(g[..., None])[..., 0]
    bc = to_chunks(beta[..., None])[..., 0]
    gcum = jnp.cumsum(gc, axis=-1)
    incl = jnp.tril(jnp.ones((c, c), dtype=bool))
    strict = jnp.tril(jnp.ones((c, c), dtype=bool), k=-1)
    decay = jnp.exp(jnp.where(incl, gcum[..., :, None] - gcum[..., None, :], -jnp.inf))
    kb = kc * bc[..., None]
    a_mat = jnp.where(strict, jnp.einsum('nbhid,nbhjd->nbhij', kb, kc) * decay, 0.0) \
        + jnp.eye(c, dtype=jnp.float32)
    rhs = jnp.concatenate([vc * bc[..., None], kb * jnp.exp(gcum)[..., None]], axis=-1)
    sol = lax.linalg.triangular_solve(a_mat, rhs, left_side=True, lower=True, unit_diagonal=True)
    u_val, k_cum = sol[..., :dv], sol[..., dv:]
    qk = jnp.where(incl, jnp.einsum('nbhid,nbhjd->nbhij', qc, kc) * decay, 0.0)
    q_dec = qc * jnp.exp(gcum)[..., None]
    k_dec = kc * jnp.exp(gcum[..., -1:] - gcum)[..., None]
    g_tot = jnp.exp(gcum[..., -1])

    def step(state, xs):
        u_i, kcum_i, qk_i, qdec_i, kdec_i, gtot_i = xs
        v_new = u_i - jnp.einsum('bhck,bhkv->bhcv', kcum_i, state)
        out = jnp.einsum('bhck,bhkv->bhcv', qdec_i, state) + jnp.einsum('bhij,bhjv->bhiv', qk_i, v_new)
        state = state * gtot_i[..., None, None] + jnp.einsum('bhck,bhcv->bhkv', kdec_i, v_new)
        return state, out

    state0 = jnp.zeros((bsz, heads, dk, dv), jnp.float32)
    _, out = lax.scan(step, state0, (u_val, k_cum, qk, q_dec, k_dec, g_tot))
    return out.transpose(1, 0, 3, 2, 4).reshape(bsz, seq, heads, dv)


def hybrid_layer(x, p_i, g_mix, w_in, b_fox_f, fox_q_gain, fox_k_gain, sc_conv_w, dn_conv_w,
                 dn_a_log, dn_dt_bias, dn_norm_gain, w_branch, w_o, g_ffn, w_up, ffn_conv_w,
                 w_down, g_ple, w_ple_gate, w_ple):
    bsz, seq, _ = x.shape
    h = rmsnorm(x, g_mix)
    proj = h @ w_in
    fox_qkv, fox_f, sc_bcv, dn_qkv, dn_b, dn_a, dn_z, br_gate = split_cols(proj, IN_SIZES)

    fq, fk, fv = [t.reshape(bsz, seq, FOX_HEADS, FOX_HEAD_DIM) for t in jnp.split(fox_qkv, 3, axis=-1)]
    y_fox = forgetting_attention(fq, fk, fv, fox_f, b_fox_f, fox_q_gain, fox_k_gain)
    y_fox = y_fox.reshape(bsz, seq, FOX_WIDTH)

    sb, sc, sv = jnp.split(sc_bcv, 3, axis=-1)
    y_sc = sb * causal_dwconv(sc * sv, sc_conv_w)

    dn_qkv = jax.nn.silu(causal_dwconv(dn_qkv, dn_conv_w))
    dq, dk_, dv_ = [t.reshape(bsz, seq, DN_HEADS, DN_HEAD_DIM) for t in jnp.split(dn_qkv, 3, axis=-1)]
    beta = jax.nn.sigmoid(dn_b.astype(jnp.float32))
    g = -jnp.exp(dn_a_log.astype(jnp.float32)) * jax.nn.softplus(dn_a.astype(jnp.float32) + dn_dt_bias.astype(jnp.float32))
    o_dn = gated_delta_rule(l2norm(dq), l2norm(dk_), dv_, g, beta).astype(x.dtype)
    z = dn_z.reshape(bsz, seq, DN_HEADS, DN_HEAD_DIM)
    y_dn = (rmsnorm(o_dn, dn_norm_gain) * jax.nn.silu(z)).reshape(bsz, seq, DN_WIDTH)

    ys = jnp.stack([y_fox, y_sc, y_dn], axis=2)
    gates = jax.nn.sigmoid(br_gate).reshape(bsz, seq, N_BRANCH, D_MODEL)
    merged = jnp.sum(jnp.einsum('bsnc,ncd->bsnd', ys, w_branch) * gates, axis=2)
    x = x + merged @ w_o

    u = causal_dwconv(rmsnorm(x, g_ffn) @ w_up, ffn_conv_w)
    u_gate, u_val = jnp.split(u, 2, axis=-1)
    x = x + (jax.nn.silu(u_gate) * u_val) @ w_down

    x = x + jax.nn.sigmoid(rmsnorm(x, g_ple) @ w_ple_gate) * (p_i.astype(x.dtype) @ w_ple)
    return x


def _fwd_setup_inputs(seed: int = 0) -> dict:
    key = jax.random.key(seed)
    ks = jax.random.split(key, 24)
    f32 = jnp.float32

    def nrm(k, shape, scale):
        return jax.random.normal(k, shape, f32) * scale

    def gain(k, shape):
        return 1.0 + 0.02 * jax.random.normal(k, shape, f32)

    x = nrm(ks[0], (BATCH, SEQ, D_MODEL), 1.0)
    p = nrm(ks[1], (DEPTH, BATCH, SEQ, PLE_DIM), 1.0)
    g_mix = gain(ks[2], (DEPTH, D_MODEL))
    w_in = nrm(ks[3], (DEPTH, D_MODEL, IN_WIDTH), D_MODEL ** -0.5)
    b_fox_f = jnp.linspace(1.0, 5.0, FOX_HEADS, dtype=f32)[None, :] + nrm(ks[4], (DEPTH, FOX_HEADS), 0.1)
    fox_q_gain = gain(ks[5], (DEPTH, FOX_HEAD_DIM))
    fox_k_gain = gain(ks[6], (DEPTH, FOX_HEAD_DIM))
    sc_conv_w = nrm(ks[7], (DEPTH, SC_KERNEL, SC_WIDTH), SC_KERNEL ** -0.5)
    dn_conv_w = nrm(ks[8], (DEPTH, DN_CONV, 3 * DN_WIDTH), DN_CONV ** -0.5)
    dn_a_log = jnp.log(jax.random.uniform(ks[9], (DEPTH, DN_HEADS), f32, 1.0, 16.0))
    dt = jnp.exp(jax.random.uniform(ks[10], (DEPTH, DN_HEADS), f32, float(np.log(1e-3)), float(np.log(1e-1))))
    dn_dt_bias = dt + jnp.log(-jnp.expm1(-dt))
    dn_norm_gain = gain(ks[11], (DEPTH, DN_HEAD_DIM))
    w_branch = nrm(ks[12], (DEPTH, N_BRANCH, BRANCH_WIDTH, D_MODEL), BRANCH_WIDTH ** -0.5)
    w_o = nrm(ks[13], (DEPTH, D_MODEL, D_MODEL), D_MODEL ** -0.5)
    g_ffn = gain(ks[14], (DEPTH, D_MODEL))
    w_up = nrm(ks[15], (DEPTH, D_MODEL, 2 * D_FF), D_MODEL ** -0.5)
    ffn_conv_w = nrm(ks[16], (DEPTH, FFN_CONV, 2 * D_FF), FFN_CONV ** -0.5)
    w_down = nrm(ks[17], (DEPTH, D_FF, D_MODEL), D_FF ** -0.5)
    g_ple = gain(ks[18], (DEPTH, D_MODEL))
    w_ple_gate = nrm(ks[19], (DEPTH, D_MODEL, D_MODEL), D_MODEL ** -0.5)
    w_ple = nrm(ks[20], (DEPTH, PLE_DIM, D_MODEL), PLE_DIM ** -0.5)
    return {'x': x, 'p': p, 'g_mix': g_mix, 'w_in': w_in, 'b_fox_f': b_fox_f,
            'fox_q_gain': fox_q_gain, 'fox_k_gain': fox_k_gain, 'sc_conv_w': sc_conv_w,
            'dn_conv_w': dn_conv_w, 'dn_a_log': dn_a_log, 'dn_dt_bias': dn_dt_bias,
            'dn_norm_gain': dn_norm_gain, 'w_branch': w_branch, 'w_o': w_o, 'g_ffn': g_ffn,
            'w_up': w_up, 'ffn_conv_w': ffn_conv_w, 'w_down': w_down, 'g_ple': g_ple,
            'w_ple_gate': w_ple_gate, 'w_ple': w_ple}


def _fwd_reference(x, p, g_mix, w_in, b_fox_f, fox_q_gain, fox_k_gain, sc_conv_w, dn_conv_w,
              dn_a_log, dn_dt_bias, dn_norm_gain, w_branch, w_o, g_ffn, w_up, ffn_conv_w,
              w_down, g_ple, w_ple_gate, w_ple):
    for i in range(DEPTH):
        x = hybrid_layer(x, p[i], g_mix[i], w_in[i], b_fox_f[i], fox_q_gain[i], fox_k_gain[i],
                         sc_conv_w[i], dn_conv_w[i], dn_a_log[i], dn_dt_bias[i], dn_norm_gain[i],
                         w_branch[i], w_o[i], g_ffn[i], w_up[i], ffn_conv_w[i], w_down[i],
                         g_ple[i], w_ple_gate[i], w_ple[i])
    return x


import jax as _jax
import jax.numpy as _jnp

TWIN_FORMAT = 'train_step'
FWD_PARAMS = ['x', 'p', 'g_mix', 'w_in', 'b_fox_f', 'fox_q_gain', 'fox_k_gain', 'sc_conv_w', 'dn_conv_w', 'dn_a_log', 'dn_dt_bias', 'dn_norm_gain', 'w_branch', 'w_o', 'g_ffn', 'w_up', 'ffn_conv_w', 'w_down', 'g_ple', 'w_ple_gate', 'w_ple']
TWIN_WEIGHTS = ['g_mix', 'w_in', 'b_fox_f', 'fox_q_gain', 'fox_k_gain', 'sc_conv_w', 'dn_conv_w', 'dn_a_log', 'dn_dt_bias', 'dn_norm_gain', 'w_branch', 'w_o', 'g_ffn', 'w_up', 'ffn_conv_w', 'w_down', 'g_ple', 'w_ple_gate', 'w_ple']
TWIN_DIFF_INPUT = 'x'
TWIN_INPUTS = ['x', 'p', 'g_mix', 'w_in', 'b_fox_f', 'fox_q_gain', 'fox_k_gain', 'sc_conv_w', 'dn_conv_w', 'dn_a_log', 'dn_dt_bias', 'dn_norm_gain', 'w_branch', 'w_o', 'g_ffn', 'w_up', 'ffn_conv_w', 'w_down', 'g_ple', 'w_ple_gate', 'w_ple', 'loss_target', 'm_g_mix', 'm_w_in', 'm_b_fox_f', 'm_fox_q_gain', 'm_fox_k_gain', 'm_sc_conv_w', 'm_dn_conv_w', 'm_dn_a_log', 'm_dn_dt_bias', 'm_dn_norm_gain', 'm_w_branch', 'm_w_o', 'm_g_ffn', 'm_w_up', 'm_ffn_conv_w', 'm_w_down', 'm_g_ple', 'm_w_ple_gate', 'm_w_ple', 'v_g_mix', 'v_w_in', 'v_b_fox_f', 'v_fox_q_gain', 'v_fox_k_gain', 'v_sc_conv_w', 'v_dn_conv_w', 'v_dn_a_log', 'v_dn_dt_bias', 'v_dn_norm_gain', 'v_w_branch', 'v_w_o', 'v_g_ffn', 'v_w_up', 'v_ffn_conv_w', 'v_w_down', 'v_g_ple', 'v_w_ple_gate', 'v_w_ple']
TWIN_OUTPUTS = ['loss', 'grad_x', 'grad_g_mix', 'grad_w_in', 'grad_b_fox_f', 'grad_fox_q_gain', 'grad_fox_k_gain', 'grad_sc_conv_w', 'grad_dn_conv_w', 'grad_dn_a_log', 'grad_dn_dt_bias', 'grad_dn_norm_gain', 'grad_w_branch', 'grad_w_o', 'grad_g_ffn', 'grad_w_up', 'grad_ffn_conv_w', 'grad_w_down', 'grad_g_ple', 'grad_w_ple_gate', 'grad_w_ple', 'delta_g_mix', 'delta_w_in', 'delta_b_fox_f', 'delta_fox_q_gain', 'delta_fox_k_gain', 'delta_sc_conv_w', 'delta_dn_conv_w', 'delta_dn_a_log', 'delta_dn_dt_bias', 'delta_dn_norm_gain', 'delta_w_branch', 'delta_w_o', 'delta_g_ffn', 'delta_w_up', 'delta_ffn_conv_w', 'delta_w_down', 'delta_g_ple', 'delta_w_ple_gate', 'delta_w_ple', 'new_m_g_mix', 'new_m_w_in', 'new_m_b_fox_f', 'new_m_fox_q_gain', 'new_m_fox_k_gain', 'new_m_sc_conv_w', 'new_m_dn_conv_w', 'new_m_dn_a_log', 'new_m_dn_dt_bias', 'new_m_dn_norm_gain', 'new_m_w_branch', 'new_m_w_o', 'new_m_g_ffn', 'new_m_w_up', 'new_m_ffn_conv_w', 'new_m_w_down', 'new_m_g_ple', 'new_m_w_ple_gate', 'new_m_w_ple', 'new_v_g_mix', 'new_v_w_in', 'new_v_b_fox_f', 'new_v_fox_q_gain', 'new_v_fox_k_gain', 'new_v_sc_conv_w', 'new_v_dn_conv_w', 'new_v_dn_a_log', 'new_v_dn_dt_bias', 'new_v_dn_norm_gain', 'new_v_w_branch', 'new_v_w_o', 'new_v_g_ffn', 'new_v_w_up', 'new_v_ffn_conv_w', 'new_v_w_down', 'new_v_g_ple', 'new_v_w_ple_gate', 'new_v_w_ple']
TWIN_LEAF_KINDS = {'loss': 'loss', 'grad_x': 'grad_x', 'grad_g_mix': 'grad_w', 'grad_w_in': 'grad_w', 'grad_b_fox_f': 'grad_w', 'grad_fox_q_gain': 'grad_w', 'grad_fox_k_gain': 'grad_w', 'grad_sc_conv_w': 'grad_w', 'grad_dn_conv_w': 'grad_w', 'grad_dn_a_log': 'grad_w', 'grad_dn_dt_bias': 'grad_w', 'grad_dn_norm_gain': 'grad_w', 'grad_w_branch': 'grad_w', 'grad_w_o': 'grad_w', 'grad_g_ffn': 'grad_w', 'grad_w_up': 'grad_w', 'grad_ffn_conv_w': 'grad_w', 'grad_w_down': 'grad_w', 'grad_g_ple': 'grad_w', 'grad_w_ple_gate': 'grad_w', 'grad_w_ple': 'grad_w', 'delta_g_mix': 'delta_w', 'delta_w_in': 'delta_w', 'delta_b_fox_f': 'delta_w', 'delta_fox_q_gain': 'delta_w', 'delta_fox_k_gain': 'delta_w', 'delta_sc_conv_w': 'delta_w', 'delta_dn_conv_w': 'delta_w', 'delta_dn_a_log': 'delta_w', 'delta_dn_dt_bias': 'delta_w', 'delta_dn_norm_gain': 'delta_w', 'delta_w_branch': 'delta_w', 'delta_w_o': 'delta_w', 'delta_g_ffn': 'delta_w', 'delta_w_up': 'delta_w', 'delta_ffn_conv_w': 'delta_w', 'delta_w_down': 'delta_w', 'delta_g_ple': 'delta_w', 'delta_w_ple_gate': 'delta_w', 'delta_w_ple': 'delta_w', 'new_m_g_mix': 'new_m', 'new_m_w_in': 'new_m', 'new_m_b_fox_f': 'new_m', 'new_m_fox_q_gain': 'new_m', 'new_m_fox_k_gain': 'new_m', 'new_m_sc_conv_w': 'new_m', 'new_m_dn_conv_w': 'new_m', 'new_m_dn_a_log': 'new_m', 'new_m_dn_dt_bias': 'new_m', 'new_m_dn_norm_gain': 'new_m', 'new_m_w_branch': 'new_m', 'new_m_w_o': 'new_m', 'new_m_g_ffn': 'new_m', 'new_m_w_up': 'new_m', 'new_m_ffn_conv_w': 'new_m', 'new_m_w_down': 'new_m', 'new_m_g_ple': 'new_m', 'new_m_w_ple_gate': 'new_m', 'new_m_w_ple': 'new_m', 'new_v_g_mix': 'new_v', 'new_v_w_in': 'new_v', 'new_v_b_fox_f': 'new_v', 'new_v_fox_q_gain': 'new_v', 'new_v_fox_k_gain': 'new_v', 'new_v_sc_conv_w': 'new_v', 'new_v_dn_conv_w': 'new_v', 'new_v_dn_a_log': 'new_v', 'new_v_dn_dt_bias': 'new_v', 'new_v_dn_norm_gain': 'new_v', 'new_v_w_branch': 'new_v', 'new_v_w_o': 'new_v', 'new_v_g_ffn': 'new_v', 'new_v_w_up': 'new_v', 'new_v_ffn_conv_w': 'new_v', 'new_v_w_down': 'new_v', 'new_v_g_ple': 'new_v', 'new_v_w_ple_gate': 'new_v', 'new_v_w_ple': 'new_v'}


def _forward(args):
    return _fwd_reference(*[args[k] for k in FWD_PARAMS])


def _output_shape():
    out = _jax.eval_shape(lambda: _forward(_fwd_setup_inputs(0)))
    return out.shape, out.dtype

N_MICROBATCH = 1
ADAM_LR = 0.001
ADAM_B1 = 0.9
ADAM_B2 = 0.999
ADAM_EPS = 1e-08
ADAM_WD = 0.01
ADAM_STEP = 10
PER_EXAMPLE_BATCH_AXIS = {'x': 0, 'p': 1, 'loss_target': 0}
SHARED_INPUTS = []
_WEIGHT_DTYPES = {'g_mix': _jnp.float32, 'w_in': _jnp.float32, 'b_fox_f': _jnp.float32, 'fox_q_gain': _jnp.float32, 'fox_k_gain': _jnp.float32, 'sc_conv_w': _jnp.float32, 'dn_conv_w': _jnp.float32, 'dn_a_log': _jnp.float32, 'dn_dt_bias': _jnp.float32, 'dn_norm_gain': _jnp.float32, 'w_branch': _jnp.float32, 'w_o': _jnp.float32, 'g_ffn': _jnp.float32, 'w_up': _jnp.float32, 'ffn_conv_w': _jnp.float32, 'w_down': _jnp.float32, 'g_ple': _jnp.float32, 'w_ple_gate': _jnp.float32, 'w_ple': _jnp.float32}
MOMENT_SCALE = {'g_mix': 1.734248e+01, 'w_in': 2.693949e-01, 'b_fox_f': 2.435434e+01, 'fox_q_gain': 3.077211e+00, 'fox_k_gain': 3.070592e+00, 'sc_conv_w': 5.308162e+00, 'dn_conv_w': 2.796232e-01, 'dn_a_log': 8.981867e+00, 'dn_dt_bias': 8.575606e+00, 'dn_norm_gain': 1.399306e+01, 'w_branch': 3.025828e-01, 'w_o': 5.008729e-01, 'g_ffn': 1.276586e+01, 'w_up': 2.182628e-01, 'ffn_conv_w': 1.761125e+00, 'w_down': 2.901097e-01, 'g_ple': 4.662265e-01, 'w_ple_gate': 1.050458e-01, 'w_ple': 3.037274e-01}


def _to_microbatches(a, axis):
    t = _jnp.moveaxis(a, axis, 0)
    t = t.reshape((N_MICROBATCH, t.shape[0] // N_MICROBATCH) + t.shape[1:])
    return _jnp.moveaxis(t, 1, axis + 1)


def setup_inputs(seed: int = 0) -> dict:
    inp = _fwd_setup_inputs(seed)
    key = _jax.random.fold_in(_jax.random.key(seed), 7919)
    shape, _ = _output_shape()
    out = dict(inp)
    out["loss_target"] = _jax.random.normal(_jax.random.fold_in(key, 0), shape, _jnp.float32)
    for i, name in enumerate(TWIN_WEIGHTS):
        w = inp[name].astype(_jnp.float32)
        if MOMENT_SCALE is None:
            s = _jnp.sqrt(_jnp.mean(_jnp.square(w)) + 1e-30)
        else:
            s = MOMENT_SCALE[name]
        km, kv = _jax.random.split(_jax.random.fold_in(key, i + 1))
        out[name] = w
        out["m_" + name] = s * _jax.random.normal(km, w.shape, _jnp.float32)
        out["v_" + name] = (s * s) * _jax.random.uniform(kv, w.shape, _jnp.float32, 0.5, 1.5)
    if N_MICROBATCH > 1:
        for name, axis in PER_EXAMPLE_BATCH_AXIS.items():
            out[name] = _to_microbatches(out[name], axis)
    return {'x': out['x'], 'p': out['p'], 'g_mix': out['g_mix'], 'w_in': out['w_in'], 'b_fox_f': out['b_fox_f'], 'fox_q_gain': out['fox_q_gain'], 'fox_k_gain': out['fox_k_gain'], 'sc_conv_w': out['sc_conv_w'], 'dn_conv_w': out['dn_conv_w'], 'dn_a_log': out['dn_a_log'], 'dn_dt_bias': out['dn_dt_bias'], 'dn_norm_gain': out['dn_norm_gain'], 'w_branch': out['w_branch'], 'w_o': out['w_o'], 'g_ffn': out['g_ffn'], 'w_up': out['w_up'], 'ffn_conv_w': out['ffn_conv_w'], 'w_down': out['w_down'], 'g_ple': out['g_ple'], 'w_ple_gate': out['w_ple_gate'], 'w_ple': out['w_ple'], 'loss_target': out['loss_target'], 'm_g_mix': out['m_g_mix'], 'm_w_in': out['m_w_in'], 'm_b_fox_f': out['m_b_fox_f'], 'm_fox_q_gain': out['m_fox_q_gain'], 'm_fox_k_gain': out['m_fox_k_gain'], 'm_sc_conv_w': out['m_sc_conv_w'], 'm_dn_conv_w': out['m_dn_conv_w'], 'm_dn_a_log': out['m_dn_a_log'], 'm_dn_dt_bias': out['m_dn_dt_bias'], 'm_dn_norm_gain': out['m_dn_norm_gain'], 'm_w_branch': out['m_w_branch'], 'm_w_o': out['m_w_o'], 'm_g_ffn': out['m_g_ffn'], 'm_w_up': out['m_w_up'], 'm_ffn_conv_w': out['m_ffn_conv_w'], 'm_w_down': out['m_w_down'], 'm_g_ple': out['m_g_ple'], 'm_w_ple_gate': out['m_w_ple_gate'], 'm_w_ple': out['m_w_ple'], 'v_g_mix': out['v_g_mix'], 'v_w_in': out['v_w_in'], 'v_b_fox_f': out['v_b_fox_f'], 'v_fox_q_gain': out['v_fox_q_gain'], 'v_fox_k_gain': out['v_fox_k_gain'], 'v_sc_conv_w': out['v_sc_conv_w'], 'v_dn_conv_w': out['v_dn_conv_w'], 'v_dn_a_log': out['v_dn_a_log'], 'v_dn_dt_bias': out['v_dn_dt_bias'], 'v_dn_norm_gain': out['v_dn_norm_gain'], 'v_w_branch': out['v_w_branch'], 'v_w_o': out['v_w_o'], 'v_g_ffn': out['v_g_ffn'], 'v_w_up': out['v_w_up'], 'v_ffn_conv_w': out['v_ffn_conv_w'], 'v_w_down': out['v_w_down'], 'v_g_ple': out['v_g_ple'], 'v_w_ple_gate': out['v_w_ple_gate'], 'v_w_ple': out['v_w_ple']}


def _loss(weights, diff, rest, loss_target):
    with _jax.named_scope("forward"):
        args = {**rest, TWIN_DIFF_INPUT: diff, **{k: w.astype(_WEIGHT_DTYPES[k]) for k, w in weights.items()}}
        y = _forward(args)
    with _jax.named_scope("loss_head"):
        err = _jnp.square(y.astype(_jnp.float32) - loss_target)
        return 0.5 * _jnp.sum(_jnp.mean(err, axis=-1)) if err.ndim else 0.5 * err


def _adamw(w, g, m, v):
    m = ADAM_B1 * m + (1.0 - ADAM_B1) * g
    v = ADAM_B2 * v + (1.0 - ADAM_B2) * _jnp.square(g)
    m_hat = m / (1.0 - ADAM_B1 ** ADAM_STEP)
    v_hat = v / (1.0 - ADAM_B2 ** ADAM_STEP)
    delta = -ADAM_LR * (m_hat / (_jnp.sqrt(v_hat) + ADAM_EPS) + ADAM_WD * w)
    return delta, m, v


def reference(x, p, g_mix, w_in, b_fox_f, fox_q_gain, fox_k_gain, sc_conv_w, dn_conv_w, dn_a_log, dn_dt_bias, dn_norm_gain, w_branch, w_o, g_ffn, w_up, ffn_conv_w, w_down, g_ple, w_ple_gate, w_ple, loss_target, m_g_mix, m_w_in, m_b_fox_f, m_fox_q_gain, m_fox_k_gain, m_sc_conv_w, m_dn_conv_w, m_dn_a_log, m_dn_dt_bias, m_dn_norm_gain, m_w_branch, m_w_o, m_g_ffn, m_w_up, m_ffn_conv_w, m_w_down, m_g_ple, m_w_ple_gate, m_w_ple, v_g_mix, v_w_in, v_b_fox_f, v_fox_q_gain, v_fox_k_gain, v_sc_conv_w, v_dn_conv_w, v_dn_a_log, v_dn_dt_bias, v_dn_norm_gain, v_w_branch, v_w_o, v_g_ffn, v_w_up, v_ffn_conv_w, v_w_down, v_g_ple, v_w_ple_gate, v_w_ple):
    given = dict(x=x, p=p, g_mix=g_mix, w_in=w_in, b_fox_f=b_fox_f, fox_q_gain=fox_q_gain, fox_k_gain=fox_k_gain, sc_conv_w=sc_conv_w, dn_conv_w=dn_conv_w, dn_a_log=dn_a_log, dn_dt_bias=dn_dt_bias, dn_norm_gain=dn_norm_gain, w_branch=w_branch, w_o=w_o, g_ffn=g_ffn, w_up=w_up, ffn_conv_w=ffn_conv_w, w_down=w_down, g_ple=g_ple, w_ple_gate=w_ple_gate, w_ple=w_ple, loss_target=loss_target, m_g_mix=m_g_mix, m_w_in=m_w_in, m_b_fox_f=m_b_fox_f, m_fox_q_gain=m_fox_q_gain, m_fox_k_gain=m_fox_k_gain, m_sc_conv_w=m_sc_conv_w, m_dn_conv_w=m_dn_conv_w, m_dn_a_log=m_dn_a_log, m_dn_dt_bias=m_dn_dt_bias, m_dn_norm_gain=m_dn_norm_gain, m_w_branch=m_w_branch, m_w_o=m_w_o, m_g_ffn=m_g_ffn, m_w_up=m_w_up, m_ffn_conv_w=m_ffn_conv_w, m_w_down=m_w_down, m_g_ple=m_g_ple, m_w_ple_gate=m_w_ple_gate, m_w_ple=m_w_ple, v_g_mix=v_g_mix, v_w_in=v_w_in, v_b_fox_f=v_b_fox_f, v_fox_q_gain=v_fox_q_gain, v_fox_k_gain=v_fox_k_gain, v_sc_conv_w=v_sc_conv_w, v_dn_conv_w=v_dn_conv_w, v_dn_a_log=v_dn_a_log, v_dn_dt_bias=v_dn_dt_bias, v_dn_norm_gain=v_dn_norm_gain, v_w_branch=v_w_branch, v_w_o=v_w_o, v_g_ffn=v_g_ffn, v_w_up=v_w_up, v_ffn_conv_w=v_ffn_conv_w, v_w_down=v_w_down, v_g_ple=v_g_ple, v_w_ple_gate=v_w_ple_gate, v_w_ple=v_w_ple)
    weights = {n: given[n] for n in TWIN_WEIGHTS}
    shared = {n: given[n] for n in SHARED_INPUTS}
    per_example = {n: given[n] for n in ['x', 'p']}
    grad_fn = _jax.value_and_grad(_loss, argnums=(0, 1))

    def one_microbatch(ex, loss_target):
        ex = dict(ex)
        diff = ex.pop(TWIN_DIFF_INPUT)
        return grad_fn(weights, diff, {**shared, **ex}, loss_target)

    if N_MICROBATCH == 1:
        loss, (grad_w, grad_x) = one_microbatch(per_example, given["loss_target"])
    else:
        def body(carry, xs):
            loss_sum, grad_sum = carry
            l_k, (gw_k, gx_k) = one_microbatch(xs[0], xs[1])
            with _jax.named_scope("update"):
                return (loss_sum + l_k, _jax.tree.map(_jnp.add, grad_sum, gw_k)), gx_k

        init = (_jnp.zeros((), _jnp.float32), _jax.tree.map(_jnp.zeros_like, weights))
        (loss, grad_w), grad_x = _jax.lax.scan(body, init, (per_example, given["loss_target"]))
    with _jax.named_scope("update"):
        delta_w, new_m, new_v = {}, {}, {}
        for n in TWIN_WEIGHTS:
            delta_w[n], new_m[n], new_v[n] = _adamw(weights[n], grad_w[n], given["m_" + n], given["v_" + n])
    return (loss, grad_x, *[grad_w[n] for n in TWIN_WEIGHTS], *[delta_w[n] for n in TWIN_WEIGHTS],
            *[new_m[n] for n in TWIN_WEIGHTS], *[new_v[n] for n in TWIN_WEIGHTS])
```

```python
import functools

import jax
import jax.numpy as jnp
from jax import lax
from jax.experimental import pallas as pl
from jax.experimental.pallas import tpu as pltpu

F32 = jnp.float32
BF16 = jnp.bfloat16
HI = lax.Precision.HIGHEST

D_MODEL = 1024
DEPTH = 2
N_DEV = 8
PLE_DIM = 256
BW = 512
FOX_HEADS, FOX_DH = 8, 64
DN_HEADS, DN_DH = 4, 128
DN_CHUNK = 64
D_FF = 2816
EPS = 1e-6
NEG = -1e30

ADAM_LR, ADAM_B1, ADAM_B2, ADAM_EPS, ADAM_WD, ADAM_STEP = 0.001, 0.9, 0.999, 1e-08, 0.01, 10

C_FQ, C_FK, C_FV = 0, 512, 1024
C_SB, C_SC, C_SV = 1536, 2048, 2560
C_DQ, C_DK, C_DV, C_DZ = 3072, 3584, 4096, 4608
C_GATE = 5120
C_SMALL = 8192
IN_P = 8320
IN_ORIG = 8208

WEIGHTS = ['g_mix', 'w_in', 'b_fox_f', 'fox_q_gain', 'fox_k_gain', 'sc_conv_w', 'dn_conv_w', 'dn_a_log',
           'dn_dt_bias', 'dn_norm_gain', 'w_branch', 'w_o', 'g_ffn', 'w_up', 'ffn_conv_w', 'w_down', 'g_ple',
           'w_ple_gate', 'w_ple']
BIG = {'w_in': 1, 'w_branch': 2, 'w_o': 0, 'w_up': 1, 'w_down': 0, 'w_ple_gate': 0, 'w_ple': 1}
CONVW = {'sc_conv_w': 1, 'dn_conv_w': 1, 'ffn_conv_w': 1}
SHARDED = {**BIG, **CONVW}
SMALL = [n for n in WEIGHTS if n not in SHARDED]
MESH = pl.DeviceIdType.MESH


def _sigmoid(x):
    return 0.5 * (jnp.tanh(0.5 * x) + 1.0)


def _silu(x):
    return x * _sigmoid(x)


def _log1pexp_negabs(z):
    return jnp.log(1.0 + jnp.exp(-jnp.abs(z)))


def _log_sigmoid(z):
    return jnp.minimum(z, 0.0) - _log1pexp_negabs(z)


def _softplus(z):
    return jnp.maximum(z, 0.0) + _log1pexp_negabs(z)


def _rms(x, g):
    return x * lax.rsqrt(jnp.mean(x * x, axis=-1, keepdims=True) + EPS) * g


def _l2(x):
    return x * lax.rsqrt(jnp.sum(x * x, axis=-1, keepdims=True) + EPS)


def _dot(a, b, dims, precision=None):
    return lax.dot_general(a, b, (dims, ((), ())), preferred_element_type=F32, precision=precision)


NN = ((1,), (0,))
NT = ((1,), (1,))
TN = ((0,), (0,))


def _shift_down(x, s):
    if s == 0:
        return x
    t = lax.broadcasted_iota(jnp.int32, x.shape, 0)
    return jnp.where(t >= s, pltpu.roll(x, s, 0), 0.0)


def _shift_up(x, s):
    if s == 0:
        return x
    n = x.shape[0]
    t = lax.broadcasted_iota(jnp.int32, x.shape, 0)
    return jnp.where(t < n - s, pltpu.roll(x, n - s, 0), 0.0)


def _conv(x, w):
    k = w.shape[0]
    y = w[k - 1:k] * x
    for j in range(k - 1):
        y = y + w[j:j + 1] * _shift_down(x, k - 1 - j)
    return y


def _conv_bwd(x, w, dy):
    k = w.shape[0]
    dx = w[k - 1:k] * dy
    dws = []
    for j in range(k - 1):
        dx = dx + w[j:j + 1] * _shift_up(dy, k - 1 - j)
        dws.append(jnp.sum(dy * _shift_down(x, k - 1 - j), axis=0, keepdims=True))
    dws.append(jnp.sum(dy * x, axis=0, keepdims=True))
    return dx, dws


def _mm(a, b, mode, outs, *, epi=None, epi_args=(), tm=512, tn=512, name):
    if mode == 'nn':
        (m, k), (k2, n) = a.shape, b.shape
    elif mode == 'nt':
        (m, k), (n, k2) = a.shape, b.shape
    else:
        (k, m), (k2, n) = a.shape, b.shape
    assert k == k2, (a.shape, b.shape, mode)
    tm, tn = min(tm, m), min(tn, n)
    assert m % tm == 0 and n % tn == 0, (m, n, tm, tn)
    dims = {'nn': NN, 'nt': NT, 'tn': TN}[mode]
    a_spec = pl.BlockSpec((k, tm), lambda i, j: (0, i)) if mode == 'tn' else pl.BlockSpec((tm, k), lambda i, j: (i, 0))
    b_spec = pl.BlockSpec((tn, k), lambda i, j: (j, 0)) if mode == 'nt' else pl.BlockSpec((k, tn), lambda i, j: (0, j))
    e_specs = [pl.BlockSpec((1, tn), lambda i, j: (0, j)) if e.shape[0] == 1 else pl.BlockSpec((tm, tn), lambda i, j: (i, j))
               for e in epi_args]
    ne, no = len(epi_args), len(outs)
    cast_a = a.dtype != BF16

    def body(a_ref, b_ref, *rest):
        if cast_a:
            a_sc = rest[-1]

            @pl.when(pl.program_id(1) == 0)
            def _():
                a_sc[...] = a_ref[...].astype(BF16)
            av = a_sc[...]
        else:
            av = a_ref[...]
        acc = _dot(av, b_ref[...].astype(BF16), dims)
        vals = epi(acc, *[e[...] for e in rest[:ne]]) if epi is not None else (acc,)
        for o_ref, v in zip(rest[ne:ne + no], vals):
            o_ref[...] = v.astype(o_ref.dtype)

    res = pl.pallas_call(
        body, grid=(m // tm, n // tn),
        in_specs=[a_spec, b_spec] + e_specs,
        out_specs=[pl.BlockSpec((tm, tn), lambda i, j: (i, j)) for _ in outs],
        out_shape=[jax.ShapeDtypeStruct((m, n), dt) for dt in outs],
        scratch_shapes=[pltpu.VMEM(a_spec.block_shape, BF16)] if cast_a else [],
        name=name,
    )(a, b, *epi_args)
    return res[0] if no == 1 else res


def _ew(fn, tiled, bcast, outs, reds=(), *, tm=256, name):
    secs = [(t, 0, t.shape[1]) if not isinstance(t, tuple) else t for t in tiled]
    m = secs[0][0].shape[0]
    tm = min(tm, m)
    assert m % tm == 0
    in_specs = []
    for arr, off, w in secs:
        assert off % w == 0
        in_specs.append(pl.BlockSpec((tm, w), functools.partial(lambda i, c: (i, c), c=off // w)))
    in_specs += [pl.BlockSpec(b.shape, lambda i: (0, 0)) for b in bcast]
    nin, no = len(in_specs), len(outs)

    def body(*refs):
        vals = fn(*[r[...] for r in refs[:nin]])
        for r, v in zip(refs[nin:nin + no], vals[:no]):
            r[...] = v.astype(r.dtype)
        i = pl.program_id(0)
        for r, v in zip(refs[nin + no:], vals[no:]):
            @pl.when(i == 0)
            def _():
                r[...] = v

            @pl.when(i > 0)
            def _():
                r[...] += v

    res = pl.pallas_call(
        body, grid=(m // tm,), in_specs=in_specs,
        out_specs=[pl.BlockSpec((tm, c), lambda i: (i, 0)) for c, _ in outs] + [pl.BlockSpec(s, lambda i: (0, 0)) for s in reds],
        out_shape=[jax.ShapeDtypeStruct((m, c), dt) for c, dt in outs] + [jax.ShapeDtypeStruct(s, F32) for s in reds],
        name=name,
    )(*[s[0] for s in secs], *bcast)
    return res


def _cb(fn, cols, params, outs, pouts, *, tc, nblk, name):
    t = cols[0][0].shape[0]
    in_specs = []
    for arr, off in cols:
        assert off % tc == 0
        in_specs.append(pl.BlockSpec((t, tc), functools.partial(lambda c, o: (0, o + c), o=off // tc)))
    for arr, off in params:
        in_specs.append(pl.BlockSpec((arr.shape[0], tc), functools.partial(lambda c, o: (0, o + c), o=off // tc)))
    nin, no = len(in_specs), len(outs)

    def body(*refs):
        vals = fn(*[r[...] for r in refs[:nin]])
        for r, v in zip(refs[nin:], vals):
            r[...] = v.astype(r.dtype)

    return pl.pallas_call(
        body, grid=(nblk,), in_specs=in_specs,
        out_specs=[pl.BlockSpec((t, tc), lambda c: (0, c)) for _ in outs] + [pl.BlockSpec((k, tc), lambda c: (0, c)) for k in pouts],
        out_shape=[jax.ShapeDtypeStruct((t, nblk * tc), dt) for dt in outs] + [jax.ShapeDtypeStruct((k, nblk * tc), F32) for k in pouts],
        name=name,
    )(*[c[0] for c in cols], *[p[0] for p in params])


CUM_BLK = 256


def _prep_point(s, bias, alog):
    lane = lax.broadcasted_iota(jnp.int32, s.shape, 1)
    z = s + bias
    return jnp.where(lane < 8, _log_sigmoid(z),
                     jnp.where(lane < 12, _sigmoid(s),
                               jnp.where(lane < 16, -jnp.exp(alog) * _softplus(z), 0.0)))


def _tri(n, upper):
    r = lax.broadcasted_iota(jnp.int32, (n, n), 0)
    c = lax.broadcasted_iota(jnp.int32, (n, n), 1)
    return (r <= c if upper else r >= c).astype(F32)


def _prep_fwd(proj, bias_row, alog_row, name):
    t = proj.shape[0]
    nb = t // CUM_BLK

    def body(s_ref, b_ref, a_ref, o_ref):
        pre = _prep_point(s_ref[...], b_ref[...], a_ref[...])
        lane = lax.broadcasted_iota(jnp.int32, (CUM_BLK, 128), 1)
        tri = _tri(CUM_BLK, False)
        carry = jnp.zeros((1, 128), F32)
        for blk in range(nb):
            xb = pre[blk * CUM_BLK:(blk + 1) * CUM_BLK]
            cb = _dot(tri, xb, NN, HI) + carry
            carry = cb[CUM_BLK - 1:CUM_BLK]
            o_ref[blk * CUM_BLK:(blk + 1) * CUM_BLK, :] = jnp.where(lane < 8, cb, xb)

    return pl.pallas_call(
        body, grid=(1,),
        in_specs=[pl.BlockSpec((t, 128), lambda i: (0, C_SMALL // 128)), pl.BlockSpec((1, 128), lambda i: (0, 0)),
                  pl.BlockSpec((1, 128), lambda i: (0, 0))],
        out_specs=pl.BlockSpec((t, 128), lambda i: (0, 0)),
        out_shape=jax.ShapeDtypeStruct((t, 128), F32), name=name,
    )(proj, bias_row, alog_row)


def _prep_bwd(proj, bias_row, alog_row, daux, name):
    t = proj.shape[0]
    nb = t // CUM_BLK

    def body(s_ref, b_ref, a_ref, d_ref, ds_ref, db_ref, da_ref, dpre_sc):
        lane = lax.broadcasted_iota(jnp.int32, (CUM_BLK, 128), 1)
        tri = _tri(CUM_BLK, True)
        carry = jnp.zeros((1, 128), F32)
        for blk in reversed(range(nb)):
            db = d_ref[blk * CUM_BLK:(blk + 1) * CUM_BLK, :]
            cb = _dot(tri, db, NN, HI) + carry
            carry = cb[0:1]
            dpre_sc[blk * CUM_BLK:(blk + 1) * CUM_BLK, :] = jnp.where(lane < 8, cb, db)
        _, vjp = jax.vjp(_prep_point, s_ref[...], b_ref[...], a_ref[...])
        ds, dbias, dalog = vjp(dpre_sc[...])
        ds_ref[...] = ds.astype(ds_ref.dtype)
        db_ref[...] = dbias
        da_ref[...] = dalog

    return pl.pallas_call(
        body, grid=(1,),
        in_specs=[pl.BlockSpec((t, 128), lambda i: (0, C_SMALL // 128)), pl.BlockSpec((1, 128), lambda i: (0, 0)),
                  pl.BlockSpec((1, 128), lambda i: (0, 0)), pl.BlockSpec((t, 128), lambda i: (0, 0))],
        out_specs=[pl.BlockSpec((t, 128), lambda i: (0, 0)), pl.BlockSpec((1, 128), lambda i: (0, 0)),
                   pl.BlockSpec((1, 128), lambda i: (0, 0))],
        out_shape=[jax.ShapeDtypeStruct((t, 128), BF16), jax.ShapeDtypeStruct((1, 128), F32), jax.ShapeDtypeStruct((1, 128), F32)],
        scratch_shapes=[pltpu.VMEM((t, 128), F32)], name=name,
    )(proj, bias_row, alog_row, daux)


ATT_TQ = 256
FOX_SCALE = FOX_DH ** -0.5


def _qnorm(q, g):
    return _rms(q, g) * FOX_SCALE


def _att_scores(qn_blk, kn, cfc_blk, cfr, qi, tq, kend):
    s = _dot(qn_blk.astype(BF16), kn[:kend].astype(BF16), NT) + cfc_blk - cfr[:, :kend]
    row = lax.broadcasted_iota(jnp.int32, (tq, kend), 0) + qi * tq
    col = lax.broadcasted_iota(jnp.int32, (tq, kend), 1)
    return s, row >= col


def _att_fwd(qh, kh, vh, qg, kg, cfc, cfr, name):
    h, t, dh = qh.shape
    tq = min(ATT_TQ, t)

    def body(q_ref, k_ref, v_ref, qg_ref, kg_ref, cfc_ref, cfr_ref, o_ref, lse_ref):
        qn = _qnorm(q_ref[0], qg_ref[...])
        kn = _rms(k_ref[0], kg_ref[...])
        v = v_ref[0].astype(BF16)
        cfr = cfr_ref[0]
        for qi in range(t // tq):
            kend = (qi + 1) * tq
            rows = slice(qi * tq, kend)
            s, mask = _att_scores(qn[rows], kn, cfc_ref[0, rows, :], cfr, qi, tq, kend)
            s = jnp.where(mask, s, NEG)
            m = jnp.max(s, axis=1, keepdims=True)
            p = jnp.exp(s - m)
            l = jnp.sum(p, axis=1, keepdims=True)
            o_ref[0, rows, :] = _dot(p.astype(BF16), v[:kend], NN) / l
            lse_ref[0, rows, :] = m + jnp.log(l)

    hd = lambda i: (i, 0, 0)
    return pl.pallas_call(
        body, grid=(h,),
        in_specs=[pl.BlockSpec((1, t, dh), hd)] * 3 + [pl.BlockSpec((1, dh), lambda i: (0, 0))] * 2
        + [pl.BlockSpec((1, t, 1), hd), pl.BlockSpec((1, 1, t), hd)],
        out_specs=[pl.BlockSpec((1, t, dh), hd), pl.BlockSpec((1, t, 1), hd)],
        out_shape=[jax.ShapeDtypeStruct((h, t, dh), F32), jax.ShapeDtypeStruct((h, t, 1), F32)], name=name,
    )(qh, kh, vh, qg, kg, cfc, cfr)


def _att_bwd(qh, kh, vh, qg, kg, cfc, cfr, lse, oh, doh, name):
    h, t, dh = qh.shape
    tq = min(ATT_TQ, t)

    def body(q_ref, k_ref, v_ref, qg_ref, kg_ref, cfc_ref, cfr_ref, lse_ref, o_ref, do_ref,
             dq_ref, dk_ref, dv_ref, dcfc_ref, dcfr_ref, dqg_ref, dkg_ref, dqn_sc, dkn_sc, dv_sc, dcfr_sc):
        qn, vjp_q = jax.vjp(_qnorm, q_ref[0], qg_ref[...])
        kn, vjp_k = jax.vjp(_rms, k_ref[0], kg_ref[...])
        v = v_ref[0].astype(BF16)
        cfr = cfr_ref[0]
        do = do_ref[0]
        delta = jnp.sum(do * o_ref[0], axis=1, keepdims=True)
        dkn_sc[...] = jnp.zeros_like(dkn_sc)
        dv_sc[...] = jnp.zeros_like(dv_sc)
        dcfr_sc[...] = jnp.zeros_like(dcfr_sc)
        for qi in range(t // tq):
            kend = (qi + 1) * tq
            rows = slice(qi * tq, kend)
            s, mask = _att_scores(qn[rows], kn, cfc_ref[0, rows, :], cfr, qi, tq, kend)
            p = jnp.where(mask, jnp.exp(jnp.where(mask, s, NEG) - lse_ref[0, rows, :]), 0.0)
            do_b = do[rows].astype(BF16)
            dv_sc[0:kend, :] += _dot(p.astype(BF16), do_b, TN)
            dp = _dot(do_b, v[:kend], NT)
            ds = p * (dp - delta[rows])
            ds_b = ds.astype(BF16)
            dqn_sc[rows, :] = _dot(ds_b, kn[:kend].astype(BF16), NN)
            dkn_sc[0:kend, :] += _dot(ds_b, qn[rows].astype(BF16), TN)
            dcfc_ref[0, rows, :] = jnp.sum(ds, axis=1, keepdims=True)
            dcfr_sc[:, 0:kend] -= jnp.sum(ds, axis=0, keepdims=True)
        dq, dqg = vjp_q(dqn_sc[...])
        dk, dkg = vjp_k(dkn_sc[...])
        dq_ref[0] = dq
        dk_ref[0] = dk
        dv_ref[0] = dv_sc[...]
        dcfr_ref[0] = dcfr_sc[...]
        dqg_ref[0] = dqg
        dkg_ref[0] = dkg

    hd = lambda i: (i, 0, 0)
    big = pl.BlockSpec((1, t, dh), hd)
    col = pl.BlockSpec((1, t, 1), hd)
    row = pl.BlockSpec((1, 1, t), hd)
    gsp = pl.BlockSpec((1, 1, dh), hd)
    return pl.pallas_call(
        body, grid=(h,),
        in_specs=[big] * 3 + [pl.BlockSpec((1, dh), lambda i: (0, 0))] * 2 + [col, row, col, big, big],
        out_specs=[big, big, big, col, row, gsp, gsp],
        out_shape=[jax.ShapeDtypeStruct((h, t, dh), F32)] * 3 + [jax.ShapeDtypeStruct((h, t, 1), F32), jax.ShapeDtypeStruct((h, 1, t), F32)]
        + [jax.ShapeDtypeStruct((h, 1, dh), F32)] * 2,
        scratch_shapes=[pltpu.VMEM((t, dh), F32)] * 3 + [pltpu.VMEM((1, t), F32)], name=name,
    )(qh, kh, vh, qg, kg, cfc, cfr, lse, oh, doh)


DN_SCALE = DN_DH ** -0.5


def _dn_chunk(qc, kc, vc, g, beta, state):
    c = qc.shape[0]
    ii = lax.broadcasted_iota(jnp.int32, (c, c), 0)
    jj = lax.broadcasted_iota(jnp.int32, (c, c), 1)
    incl, strict = ii >= jj, ii > jj
    lower = incl.astype(F32)
    dm = _dot(lower, jnp.where(strict, g, 0.0), NN, HI)
    decay = jnp.where(incl, jnp.exp(jnp.where(incl, dm, 0.0)), 0.0)
    gcum = _dot(lower, g * jnp.ones((1, DN_DH), F32), NN, HI)
    eg = jnp.exp(gcum)
    glast = gcum[c - 1:c]
    kb = kc * beta
    n1 = jnp.where(strict, _dot(kb, kc, NT, HI) * decay, 0.0)
    x = jnp.concatenate([vc * beta, kb * eg], axis=1)
    x = x - _dot(n1, x, NN, HI)
    pw = n1
    for _ in range(5):
        pw = _dot(pw, pw, NN, HI)
        x = x + _dot(pw, x, NN, HI)
    u, kcum = x[:, :DN_DH], x[:, DN_DH:]
    qk = jnp.where(incl, _dot(qc, kc, NT, HI) * decay, 0.0)
    v_new = u - _dot(kcum, state, NN, HI)
    out = _dot(qc * eg, state, NN, HI) + _dot(qk, v_new, NN, HI)
    new_state = state * jnp.exp(glast) + _dot(kc * jnp.exp(glast - gcum), v_new, TN, HI)
    return out, new_state


def _dn_pre_q(c):
    return _l2(_silu(c)) * DN_SCALE


def _dn_pre_k(c):
    return _l2(_silu(c))


def _dn_post(o, z, ng):
    return _rms(o, ng) * _silu(z)


def _dn_specs(t):
    cblk = lambda o: pl.BlockSpec((t, DN_DH), functools.partial(lambda h, o: (0, o + h), o=o // DN_DH))
    wblk = lambda o: pl.BlockSpec((4, DN_DH), functools.partial(lambda h, o: (0, o + h), o=o // DN_DH))
    proj_specs = [cblk(C_DQ), cblk(C_DK), cblk(C_DV), cblk(C_DZ)]
    w_specs = [wblk(0), wblk(BW), wblk(2 * BW)]
    gb_spec = pl.BlockSpec((1, t, 2), lambda h: (h, 0, 0))
    return proj_specs, w_specs, gb_spec


def _dn_fwd(proj, conv_w, gb, ng, name):
    t = proj.shape[0]
    nc = t // DN_CHUNK
    proj_specs, w_specs, gb_spec = _dn_specs(t)

    def body(q_ref, k_ref, v_ref, z_ref, wq_ref, wk_ref, wv_ref, gb_ref, ng_ref, y_ref, o_ref, st_ref, qn_sc, kn_sc, vv_sc):
        qn_sc[...] = _dn_pre_q(_conv(q_ref[...], wq_ref[...]))
        kn_sc[...] = _dn_pre_k(_conv(k_ref[...], wk_ref[...]))
        vv_sc[...] = _silu(_conv(v_ref[...], wv_ref[...]))

        def chunk(n, state):
            r = pl.ds(pl.multiple_of(n * DN_CHUNK, DN_CHUNK), DN_CHUNK)
            gbv = gb_ref[0, r, :]
            out, new_state = _dn_chunk(qn_sc[r, :], kn_sc[r, :], vv_sc[r, :], gbv[:, 0:1], gbv[:, 1:2], state)
            st_ref[0, n] = state
            o_ref[r, :] = out
            return new_state

        lax.fori_loop(0, nc, chunk, jnp.zeros((DN_DH, DN_DH), F32))
        y_ref[...] = _dn_post(o_ref[...], z_ref[...], ng_ref[...])

    hblk = pl.BlockSpec((t, DN_DH), lambda h: (0, h))
    return pl.pallas_call(
        body, grid=(DN_HEADS,),
        in_specs=proj_specs + w_specs + [gb_spec, pl.BlockSpec((1, DN_DH), lambda h: (0, 0))],
        out_specs=[hblk, hblk, pl.BlockSpec((1, nc, DN_DH, DN_DH), lambda h: (h, 0, 0, 0))],
        out_shape=[jax.ShapeDtypeStruct((t, BW), F32), jax.ShapeDtypeStruct((t, BW), F32),
                   jax.ShapeDtypeStruct((DN_HEADS, nc, DN_DH, DN_DH), F32)],
        scratch_shapes=[pltpu.VMEM((t, DN_DH), F32)] * 3, name=name,
    )(proj, proj, proj, proj, conv_w, conv_w, conv_w, gb, ng)


def _dn_bwd(proj, conv_w, gb, ng, o, states, dy, name):
    t = proj.shape[0]
    nc = t // DN_CHUNK
    proj_specs, w_specs, gb_spec = _dn_specs(t)

    def body(q_ref, k_ref, v_ref, z_ref, wq_ref, wk_ref, wv_ref, gb_ref, ng_ref, o_ref, st_ref, dy_ref,
             dq_ref, dk_ref, dv_ref, dz_ref, dwq_ref, dwk_ref, dwv_ref, dgb_ref, dng_ref,
             qn_sc, kn_sc, vv_sc, do_sc, dqn_sc, dkn_sc, dvv_sc):
        cq = _conv(q_ref[...], wq_ref[...])
        ck = _conv(k_ref[...], wk_ref[...])
        cv = _conv(v_ref[...], wv_ref[...])
        qn, vjp_q = jax.vjp(_dn_pre_q, cq)
        kn, vjp_k = jax.vjp(_dn_pre_k, ck)
        vv, vjp_v = jax.vjp(_silu, cv)
        qn_sc[...] = qn
        kn_sc[...] = kn
        vv_sc[...] = vv
        _, vjp_y = jax.vjp(_dn_post, o_ref[...], z_ref[...], ng_ref[...])
        do, dz, dng = vjp_y(dy_ref[...])
        do_sc[...] = do
        dz_ref[...] = dz.astype(dz_ref.dtype)
        dng_ref[0] = dng

        def chunk(i, dstate):
            n = nc - 1 - i
            r = pl.ds(pl.multiple_of(n * DN_CHUNK, DN_CHUNK), DN_CHUNK)
            gbv = gb_ref[0, r, :]
            _, vjp = jax.vjp(_dn_chunk, qn_sc[r, :], kn_sc[r, :], vv_sc[r, :], gbv[:, 0:1], gbv[:, 1:2], st_ref[0, n])
            dqc, dkc, dvc, dg, dbeta, dprev = vjp((do_sc[r, :], dstate))
            dqn_sc[r, :] = dqc
            dkn_sc[r, :] = dkc
            dvv_sc[r, :] = dvc
            dgb_ref[0, r, :] = jnp.concatenate([dg, dbeta], axis=1)
            return dprev

        lax.fori_loop(0, nc, chunk, jnp.zeros((DN_DH, DN_DH), F32))
        for x_ref, w_ref, vjp, d_sc, dx_ref, dw_ref in ((q_ref, wq_ref, vjp_q, dqn_sc, dq_ref, dwq_ref),
                                                       (k_ref, wk_ref, vjp_k, dkn_sc, dk_ref, dwk_ref),
                                                       (v_ref, wv_ref, vjp_v, dvv_sc, dv_ref, dwv_ref)):
            (dc,) = vjp(d_sc[...])
            dx, dws = _conv_bwd(x_ref[...], w_ref[...], dc)
            dx_ref[...] = dx.astype(dx_ref.dtype)
            for j, dw in enumerate(dws):
                dw_ref[j:j + 1, :] = dw

    hblk = pl.BlockSpec((t, DN_DH), lambda h: (0, h))
    wout = pl.BlockSpec((4, DN_DH), lambda h: (0, h))
    return pl.pallas_call(
        body, grid=(DN_HEADS,),
        in_specs=proj_specs + w_specs + [gb_spec, pl.BlockSpec((1, DN_DH), lambda h: (0, 0)), hblk,
                                         pl.BlockSpec((1, nc, DN_DH, DN_DH), lambda h: (h, 0, 0, 0)), hblk],
        out_specs=[hblk] * 4 + [wout] * 3 + [gb_spec, pl.BlockSpec((1, 1, DN_DH), lambda h: (h, 0, 0))],
        out_shape=[jax.ShapeDtypeStruct((t, BW), BF16)] * 4 + [jax.ShapeDtypeStruct((4, BW), F32)] * 3
        + [jax.ShapeDtypeStruct((DN_HEADS, t, 2), F32), jax.ShapeDtypeStruct((DN_HEADS, 1, DN_DH), F32)],
        scratch_shapes=[pltpu.VMEM((t, DN_DH), F32)] * 7, name=name,
    )(proj, proj, proj, proj, conv_w, conv_w, conv_w, gb, ng, o, states, dy)


MERGE_TM, MERGE_TN = 512, 512


def _merge_specs(t):
    tm, tn = min(MERGE_TM, t), MERGE_TN
    y_spec = pl.BlockSpec((tm, BW), lambda i, j: (i, 0))
    w_spec = pl.BlockSpec((3, BW, tn), lambda i, j: (0, 0, j))
    gate_specs = [pl.BlockSpec((tm, tn), functools.partial(lambda i, j, o: (i, o + j), o=(C_GATE + n * D_MODEL) // tn))
                  for n in range(3)]
    return tm, tn, [y_spec] * 3 + [w_spec] + gate_specs


def _merge_fwd(ys, wb, proj, name):
    t = proj.shape[0]
    tm, tn, in_specs = _merge_specs(t)

    def body(y0, y1, y2, w_ref, g0, g1, g2, o_ref):
        acc = jnp.zeros((tm, tn), F32)
        for n, (y, g) in enumerate(((y0, g0), (y1, g1), (y2, g2))):
            acc = acc + _dot(y[...].astype(BF16), w_ref[n], NN) * _sigmoid(g[...])
        o_ref[...] = acc.astype(o_ref.dtype)

    return pl.pallas_call(
        body, grid=(t // tm, D_MODEL // tn), in_specs=in_specs,
        out_specs=pl.BlockSpec((tm, tn), lambda i, j: (i, j)),
        out_shape=jax.ShapeDtypeStruct((t, D_MODEL), BF16), name=name,
    )(*ys, wb, proj, proj, proj)


def _merge_bwd(ys, wb, proj, dmerged, name):
    t = proj.shape[0]
    tm, tn, in_specs = _merge_specs(t)

    def body(y0, y1, y2, w_ref, g0, g1, g2, dm_ref, dg_ref, dt_ref):
        dm = dm_ref[...]
        for n, (y, g) in enumerate(((y0, g0), (y1, g1), (y2, g2))):
            tn_ = _dot(y[...].astype(BF16), w_ref[n], NN)
            sg = _sigmoid(g[...])
            dg_ref[n] = (dm * tn_ * sg * (1.0 - sg)).astype(dg_ref.dtype)
            dt_ref[n] = (dm * sg).astype(dt_ref.dtype)

    o3 = pl.BlockSpec((3, tm, tn), lambda i, j: (0, i, j))
    return pl.pallas_call(
        body, grid=(t // tm, D_MODEL // tn), in_specs=in_specs + [pl.BlockSpec((tm, tn), lambda i, j: (i, j))],
        out_specs=[o3, o3], out_shape=[jax.ShapeDtypeStruct((3, t, D_MODEL), BF16)] * 2, name=name,
    )(*ys, wb, proj, proj, proj, dmerged)


def _heads(a, off):
    t = a.shape[0]
    return a[:, off:off + BW].reshape(t, FOX_HEADS, FOX_DH).transpose(1, 0, 2)


def _unheads(a):
    h, t, d = a.shape
    return a.transpose(1, 0, 2).reshape(t, h * d)


def _prep_rows(sp):
    z4, z112 = jnp.zeros((1, 4), F32), jnp.zeros((1, 112), F32)
    bias_row = jnp.concatenate([sp['b_fox_f'], z4, sp['dn_dt_bias'], z112], axis=1)
    alog_row = jnp.concatenate([jnp.zeros((1, 12), F32), sp['dn_a_log'], z112], axis=1)
    return bias_row, alog_row


def _sc_fwd_tile(sb, sc, sv, w):
    return (sb * _conv(sc * sv, w),)


def _ffn_act(ug, uv):
    return _silu(ug) * uv


def _perm_w_in(wi):
    z = jnp.zeros((wi.shape[0], IN_P - IN_ORIG), wi.dtype)
    return jnp.concatenate([wi[:, 0:1536], wi[:, 1544:3080], wi[:, 3080:4616], wi[:, 4624:5136], wi[:, 5136:8208],
                            wi[:, 1536:1544], wi[:, 4616:4624], z], axis=1)


def _unperm_w_in(gp):
    return jnp.concatenate([gp[:, 0:1536], gp[:, 8192:8200], gp[:, 1536:3072], gp[:, 3072:4608], gp[:, 8200:8208],
                            gp[:, 4608:5120], gp[:, 5120:8192]], axis=1)


def _layer_fwd(x, p_i, w, sp, tag):
    t = x.shape[0]
    sv = {'x': x}
    (hn,) = _ew(lambda a, g: (_rms(a, g),), [x], [sp['g_mix']], [(D_MODEL, BF16)], name=f'rms_mix_{tag}')
    proj = _mm(hn, w['w_in'], 'nn', [F32], tn=640, name=f'in_proj_{tag}')
    bias_row, alog_row = _prep_rows(sp)
    aux = _prep_fwd(proj, bias_row, alog_row, f'prep_{tag}')
    qh, kh, vh = _heads(proj, C_FQ), _heads(proj, C_FK), _heads(proj, C_FV)
    cf = aux[:, :FOX_HEADS].T
    cfc, cfr = cf[:, :, None], cf[:, None, :]
    oh, lse = _att_fwd(qh, kh, vh, sp['fox_q_gain'], sp['fox_k_gain'], cfc, cfr, f'fox_fwd_{tag}')
    y_fox = _unheads(oh)
    (y_sc,) = _cb(_sc_fwd_tile, [(proj, C_SB), (proj, C_SC), (proj, C_SV)], [(w['sc_conv_w'], 0)], [F32], [],
                  tc=256, nblk=2, name=f'sc_fwd_{tag}')
    gb = jnp.stack([aux[:, 12:16].T, aux[:, 8:12].T], axis=-1)
    y_dn, o_dn, states = _dn_fwd(proj, w['dn_conv_w'], gb, sp['dn_norm_gain'], f'dn_fwd_{tag}')
    ys = (y_fox, y_sc, y_dn)
    merged = _merge_fwd(ys, w['w_branch'], proj, f'merge_fwd_{tag}')
    x1 = _mm(merged, w['w_o'], 'nn', [F32], epi=lambda acc, r: (acc + r,), epi_args=(x,), name=f'o_proj_{tag}')
    (hf,) = _ew(lambda a, g: (_rms(a, g),), [x1], [sp['g_ffn']], [(D_MODEL, BF16)], name=f'rms_ffn_{tag}')
    up = _mm(hf, w['w_up'], 'nn', [F32], name=f'up_proj_{tag}')
    (act,) = _cb(lambda ug, uv, wg, wv: (_ffn_act(_conv(ug, wg), _conv(uv, wv)),), [(up, 0), (up, D_FF)],
                 [(w['ffn_conv_w'], 0), (w['ffn_conv_w'], D_FF)], [BF16], [], tc=256, nblk=D_FF // 256, name=f'ffn_act_{tag}')
    x2 = _mm(act, w['w_down'], 'nn', [F32], epi=lambda acc, r: (acc + r,), epi_args=(x1,), name=f'down_proj_{tag}')
    (hp,) = _ew(lambda a, g: (_rms(a, g),), [x2], [sp['g_ple']], [(D_MODEL, BF16)], name=f'rms_ple_{tag}')
    gp = _mm(hp, w['w_ple_gate'], 'nn', [F32], name=f'ple_gate_{tag}')
    x3 = _mm(p_i, w['w_ple'], 'nn', [F32], epi=lambda acc, g, r: (r + _sigmoid(g) * acc,), epi_args=(gp, x2), name=f'ple_{tag}')
    sv.update(hn=hn, proj=proj, aux=aux, qh=qh, kh=kh, vh=vh, cfc=cfc, cfr=cfr, oh=oh, lse=lse, ys=ys, gb=gb, o_dn=o_dn,
              states=states, merged=merged, x1=x1, hf=hf, up=up, act=act, x2=x2, hp=hp, gp=gp, p=p_i,
              bias_row=bias_row, alog_row=alog_row)
    return x3, sv


def _rms_bwd(x, g, dh, dres, name):
    def fn(xv, dhv, dr, gv):
        _, vjp = jax.vjp(_rms, xv, gv)
        dx, dg = vjp(dhv)
        return dr + dx, dg
    return _ew(fn, [x, dh, dres], [g], [(D_MODEL, F32)], [(1, D_MODEL)], name=name)


def _layer_bwd(dx3, sv, w, sp, tag):
    t = dx3.shape[0]
    g = {}
    def ple_epi(acc, gpv, d):
        s = _sigmoid(gpv)
        return d * acc * s * (1.0 - s), d * s
    dgp, de = _mm(sv['p'], w['w_ple'], 'nn', [BF16, BF16], epi=ple_epi, epi_args=(sv['gp'], dx3), name=f'ple_bwd_{tag}')
    g['w_ple'] = _mm(sv['p'], de, 'tn', [F32], name=f'd_w_ple_{tag}')
    g['w_ple_gate'] = _mm(sv['hp'], dgp, 'tn', [F32], name=f'd_w_ple_gate_{tag}')
    dhp = _mm(dgp, w['w_ple_gate'], 'nt', [F32], name=f'd_hp_{tag}')
    dx2, g['g_ple'] = _rms_bwd(sv['x2'], sp['g_ple'], dhp, dx3, f'rms_ple_bwd_{tag}')
    dact = _mm(dx2, w['w_down'], 'nt', [F32], tn=256, name=f'd_act_{tag}')
    g['w_down'] = _mm(sv['act'], dx2, 'tn', [F32], tm=256, name=f'd_w_down_{tag}')

    def ffn_bwd_tile(ug, uv, da, wg, wv):
        cg, cv = _conv(ug, wg), _conv(uv, wv)
        _, vjp = jax.vjp(_ffn_act, cg, cv)
        dcg, dcv = vjp(da)
        dug, dwg = _conv_bwd(ug, wg, dcg)
        duv, dwv = _conv_bwd(uv, wv, dcv)
        return dug, duv, jnp.concatenate(dwg, axis=0), jnp.concatenate(dwv, axis=0)
    dupg, dupv, dwg, dwv = _cb(ffn_bwd_tile, [(sv['up'], 0), (sv['up'], D_FF), (dact, 0)],
                               [(w['ffn_conv_w'], 0), (w['ffn_conv_w'], D_FF)], [BF16, BF16], [3, 3], tc=256,
                               nblk=D_FF // 256, name=f'ffn_act_bwd_{tag}')
    dup = jnp.concatenate([dupg, dupv], axis=1)
    g['ffn_conv_w'] = jnp.concatenate([dwg, dwv], axis=1)
    g['w_up'] = _mm(sv['hf'], dup, 'tn', [F32], name=f'd_w_up_{tag}')
    dhf = _mm(dup, w['w_up'], 'nt', [F32], name=f'd_hf_{tag}')
    dx1, g['g_ffn'] = _rms_bwd(sv['x1'], sp['g_ffn'], dhf, dx2, f'rms_ffn_bwd_{tag}')
    dmerged = _mm(dx1, w['w_o'], 'nt', [F32], name=f'd_merged_{tag}')
    g['w_o'] = _mm(sv['merged'], dx1, 'tn', [F32], name=f'd_w_o_{tag}')
    dgate, dtn = _merge_bwd(sv['ys'], w['w_branch'], sv['proj'], dmerged, f'merge_bwd_{tag}')
    dys, dwb = [], []
    for n in range(3):
        dys.append(_mm(dtn[n], w['w_branch'][n], 'nt', [F32], name=f'd_y{n}_{tag}'))
        dwb.append(_mm(sv['ys'][n], dtn[n], 'tn', [F32], name=f'd_w_branch{n}_{tag}'))
    g['w_branch'] = jnp.stack(dwb)
    ddq, ddk, ddv, ddz, dwq, dwk, dwv_, dgb, dng = _dn_bwd(sv['proj'], w['dn_conv_w'], sv['gb'], sp['dn_norm_gain'],
                                                           sv['o_dn'], sv['states'], dys[2], f'dn_bwd_{tag}')
    g['dn_conv_w'] = jnp.concatenate([dwq, dwk, dwv_], axis=1)
    g['dn_norm_gain'] = jnp.sum(dng, axis=0)
    def sc_bwd_tile(sb, sc, svv, dy, wv):
        u = sc * svv
        dsb = dy * _conv(u, wv)
        du, dws = _conv_bwd(u, wv, dy * sb)
        return dsb, du * svv, du * sc, jnp.concatenate(dws, axis=0)
    dsb, dsc, dsv, g['sc_conv_w'] = _cb(sc_bwd_tile, [(sv['proj'], C_SB), (sv['proj'], C_SC), (sv['proj'], C_SV), (dys[1], 0)],
                                        [(w['sc_conv_w'], 0)], [BF16, BF16, BF16], [3], tc=256, nblk=2, name=f'sc_bwd_{tag}')
    doh = dys[0].reshape(t, FOX_HEADS, FOX_DH).transpose(1, 0, 2)
    dqh, dkh, dvh, dcfc, dcfr, dqg, dkg = _att_bwd(sv['qh'], sv['kh'], sv['vh'], sp['fox_q_gain'], sp['fox_k_gain'],
                                                   sv['cfc'], sv['cfr'], sv['lse'], sv['oh'], doh, f'fox_bwd_{tag}')
    g['fox_q_gain'] = jnp.sum(dqg, axis=0)
    g['fox_k_gain'] = jnp.sum(dkg, axis=0)
    dcf = (dcfc[:, :, 0] + dcfr[:, 0, :]).T
    daux = jnp.concatenate([dcf, dgb[:, :, 1].T, dgb[:, :, 0].T, jnp.zeros((t, 112), F32)], axis=1)
    dsmall, dbias, dalog = _prep_bwd(sv['proj'], sv['bias_row'], sv['alog_row'], daux, f'prep_bwd_{tag}')
    g['b_fox_f'] = dbias[:, 0:8]
    g['dn_dt_bias'] = dbias[:, 12:16]
    g['dn_a_log'] = dalog[:, 12:16]
    dproj = jnp.concatenate([_unheads(dqh).astype(BF16), _unheads(dkh).astype(BF16), _unheads(dvh).astype(BF16),
                             dsb, dsc, dsv, ddq, ddk, ddv, ddz, dgate[0], dgate[1], dgate[2], dsmall], axis=1)
    g['w_in'] = _mm(sv['hn'], dproj, 'tn', [F32], tn=640, name=f'd_w_in_{tag}')
    dhn = _mm(dproj, w['w_in'], 'nt', [F32], name=f'd_hn_{tag}')
    dx, g['g_mix'] = _rms_bwd(sv['x'], sp['g_mix'], dhn, dx1, f'rms_mix_bwd_{tag}')
    return dx, g


def _loss_bwd(y, target, name):
    inv = 1.0 / y.shape[1]

    def fn(yv, tv):
        err = yv - tv
        return err * inv, jnp.zeros((8, 128), F32) + 0.5 * inv * jnp.sum(err * err)
    return _ew(fn, [y, target], [], [(y.shape[1], F32)], [(8, 128)], name=name)


PACK_W = 1024
FULL_SHAPE = {'w_in': (D_MODEL, IN_ORIG), 'w_branch': (3, BW, D_MODEL), 'w_o': (D_MODEL, D_MODEL), 'w_up': (D_MODEL, 2 * D_FF),
              'w_down': (D_FF, D_MODEL), 'w_ple_gate': (D_MODEL, D_MODEL), 'w_ple': (PLE_DIM, D_MODEL),
              'sc_conv_w': (3, BW), 'dn_conv_w': (4, 3 * BW), 'ffn_conv_w': (3, 2 * D_FF)}
SMALL_SHAPE = {'g_mix': D_MODEL, 'b_fox_f': FOX_HEADS, 'fox_q_gain': FOX_DH, 'fox_k_gain': FOX_DH, 'dn_a_log': DN_HEADS,
               'dn_dt_bias': DN_HEADS, 'dn_norm_gain': DN_DH, 'g_ffn': D_MODEL, 'g_ple': D_MODEL}


def _shard_shape(name):
    s = list(FULL_SHAPE[name])
    s[SHARDED[name]] //= N_DEV
    return tuple(s)


def _segments(names):
    segs, r = [], 0
    for layer in range(DEPTH):
        for name in names:
            n = 1
            for d in _shard_shape(name):
                n *= d
            rows = -(-n // (16 * PACK_W)) * 16
            segs.append((layer, name, r, rows, n))
            r += rows
    return segs, r


SEGS, PACK_ROWS = _segments(list(SHARDED))
CONV_SEGS, CONV_ROWS = _segments(list(CONVW))


def _pack(shards, segs, dtype):
    parts = []
    for layer, name, _, rows, n in segs:
        flat = shards[name][layer].reshape(-1).astype(dtype)
        parts.append(jnp.pad(flat, (0, rows * PACK_W - n)).reshape(rows, PACK_W))
    return jnp.concatenate(parts, axis=0)


def _unpack(pack, segs):
    out = {}
    for layer, name, r0, rows, n in segs:
        out.setdefault(name, []).append(pack[r0:r0 + rows].reshape(-1)[:n].reshape(_shard_shape(name)))
    return {k: jnp.stack(v) for k, v in out.items()}


def _unpack_gathered(gathered, segs, layer_sel):
    out = {}
    for layer, name, r0, rows, n in segs:
        if layer != layer_sel:
            continue
        sh = _shard_shape(name)
        blocks = gathered[:, r0:r0 + rows].reshape(N_DEV, -1)[:, :n].reshape((N_DEV,) + sh)
        ax = SHARDED[name]
        blocks = jnp.moveaxis(blocks, 0, ax)
        out[name] = blocks.reshape(sh[:ax] + (N_DEV * sh[ax],) + sh[ax + 1:])
    return out


def _pack_by_dest(grads, segs):
    parts = []
    for layer, name, _, rows, n in segs:
        full = grads[layer][name]
        sh, ax = _shard_shape(name), SHARDED[name]
        blocks = jnp.moveaxis(full.reshape(sh[:ax] + (N_DEV, sh[ax]) + sh[ax + 1:]), ax, 0).reshape(N_DEV, n)
        parts.append(jnp.pad(blocks, ((0, 0), (0, rows * PACK_W - n))).reshape(N_DEV, rows, PACK_W))
    return jnp.concatenate(parts, axis=1)


def _small_layout():
    lay, off = {}, 0
    for layer in range(DEPTH):
        for name in SMALL:
            lay[(layer, name)] = off
            off += SMALL_SHAPE[name]
    return lay, off


SMALL_LAY, SMALL_LOSS_AT = _small_layout()
SMALL_ROWS = 8
assert SMALL_LOSS_AT < SMALL_ROWS * PACK_W


def _pack_small(vals, loss=None):
    parts = [vals[name][layer].reshape(-1) for layer in range(DEPTH) for name in SMALL]
    parts.append(jnp.zeros((1,), F32) if loss is None else loss.reshape(1))
    flat = jnp.concatenate(parts)
    return jnp.pad(flat, (0, SMALL_ROWS * PACK_W - flat.shape[0])).reshape(SMALL_ROWS, PACK_W)


def _unpack_small(pack):
    flat = pack.reshape(-1)
    return {name: jnp.stack([flat[SMALL_LAY[(layer, name)]:SMALL_LAY[(layer, name)] + SMALL_SHAPE[name]] for layer in range(DEPTH)])
            for name in SMALL}


HBM_SPEC = pl.BlockSpec(memory_space=pltpu.HBM)


def _place():
    x, y, c = lax.axis_index("x"), lax.axis_index("y"), lax.axis_index("c")
    return x, y, c, [(1 - x, y), (x, 1 - y), (1 - x, 1 - y)]


def _all_gather(arrs, name):
    n = len(arrs)

    def body(*refs):
        ins, outs = refs[:n], refs[n:2 * n]
        send_sems, recv_sems, local_sems = refs[2 * n:]
        x, y, c, chips = _place()
        me, sibling = (x, y, c), (x, y, 1 - c)

        def block(a, p):
            return outs[a].at[4 * p[0] + 2 * p[1] + p[2]]

        def copy(a, k, blk, to, src=None):
            return pltpu.make_async_remote_copy(src_ref=block(a, blk) if src is None else src, dst_ref=block(a, blk),
                                                send_sem=send_sems.at[a, k], recv_sem=recv_sems.at[a, k],
                                                device_id=to, device_id_type=MESH)

        mine = [pltpu.make_async_copy(ins[a], block(a, me), local_sems.at[a]) for a in range(n)]
        first, passed = [], []
        for a in range(n):
            mine[a].start()
            first.append(copy(a, 0, me, sibling, src=ins[a]))
            first += [copy(a, 1 + j, me, (*chip, c), src=ins[a]) for j, chip in enumerate(chips)]
        for cp in first:
            cp.start()
        for j, chip in enumerate(chips):
            for a in range(n):
                copy(a, 1 + j, (*chip, c), me).wait_recv()
                fwd = copy(a, 4 + j, (*chip, c), sibling)
                fwd.start()
                passed.append(fwd)
        for a in range(n):
            copy(a, 0, sibling, me).wait_recv()
            for j, chip in enumerate(chips):
                copy(a, 4 + j, (*chip, 1 - c), me).wait_recv()
        for cp in first + passed:
            cp.wait_send()
        for a in range(n):
            mine[a].wait()

    return pl.pallas_call(
        body, in_specs=[HBM_SPEC] * n, out_specs=[HBM_SPEC] * n,
        out_shape=[jax.ShapeDtypeStruct((N_DEV,) + a.shape, a.dtype) for a in arrs],
        scratch_shapes=[pltpu.SemaphoreType.DMA((n, 7)), pltpu.SemaphoreType.DMA((n, 7)), pltpu.SemaphoreType.DMA((n,))],
        name=name,
    )(*arrs)


def _sibling_swap(a, name):
    def body(a_ref, o_ref, send_sem, recv_sem):
        x, y, c, _ = _place()
        cp = pltpu.make_async_remote_copy(src_ref=a_ref, dst_ref=o_ref, send_sem=send_sem, recv_sem=recv_sem,
                                          device_id=(x, y, 1 - c), device_id_type=MESH)
        cp.start()
        cp.wait()

    return pl.pallas_call(body, in_specs=[HBM_SPEC], out_specs=HBM_SPEC, out_shape=jax.ShapeDtypeStruct(a.shape, a.dtype),
                          scratch_shapes=[pltpu.SemaphoreType.DMA, pltpu.SemaphoreType.DMA], name=name)(a)


def _chip_exchange(a, name):
    def body(a_ref, o_ref, send_sems, recv_sems):
        x, y, c, chips = _place()
        cps = [pltpu.make_async_remote_copy(src_ref=a_ref.at[2 * cx + cy], dst_ref=o_ref.at[j], send_sem=send_sems.at[j],
                                            recv_sem=recv_sems.at[j], device_id=(cx, cy, c), device_id_type=MESH)
               for j, (cx, cy) in enumerate(chips)]
        for cp in cps:
            cp.start()
        for cp in cps:
            cp.wait()

    return pl.pallas_call(body, in_specs=[HBM_SPEC], out_specs=HBM_SPEC,
                          out_shape=jax.ShapeDtypeStruct((3,) + a.shape[1:], a.dtype),
                          scratch_shapes=[pltpu.SemaphoreType.DMA((3,)), pltpu.SemaphoreType.DMA((3,))], name=name)(a)


def _adamw(w, g, m, v):
    m = ADAM_B1 * m + (1.0 - ADAM_B1) * g
    v = ADAM_B2 * v + (1.0 - ADAM_B2) * jnp.square(g)
    m_hat = m / (1.0 - ADAM_B1 ** ADAM_STEP)
    v_hat = v / (1.0 - ADAM_B2 ** ADAM_STEP)
    delta = -ADAM_LR * (m_hat / (jnp.sqrt(v_hat) + ADAM_EPS) + ADAM_WD * w)
    return delta, m, v


def _adamw_call(w, m, v, gparts, name):
    def fn(wv, mv, vv, *gs):
        g = gs[0].astype(F32)
        for gp in gs[1:]:
            g = g + gp.astype(F32)
        delta, m2, v2 = _adamw(wv, g, mv, vv)
        return g, delta, m2, v2
    c = w.shape[1]
    return _ew(fn, [w, m, v] + list(gparts), [], [(c, F32)] * 4, tm=128, name=name)


def kernel(x, p, g_mix, w_in, b_fox_f, fox_q_gain, fox_k_gain, sc_conv_w, dn_conv_w, dn_a_log, dn_dt_bias,
           dn_norm_gain, w_branch, w_o, g_ffn, w_up, ffn_conv_w, w_down, g_ple, w_ple_gate, w_ple, loss_target,
           m_g_mix, m_w_in, m_b_fox_f, m_fox_q_gain, m_fox_k_gain, m_sc_conv_w, m_dn_conv_w, m_dn_a_log,
           m_dn_dt_bias, m_dn_norm_gain, m_w_branch, m_w_o, m_g_ffn, m_w_up, m_ffn_conv_w, m_w_down, m_g_ple,
           m_w_ple_gate, m_w_ple, v_g_mix, v_w_in, v_b_fox_f, v_fox_q_gain, v_fox_k_gain, v_sc_conv_w, v_dn_conv_w,
           v_dn_a_log, v_dn_dt_bias, v_dn_norm_gain, v_w_branch, v_w_o, v_g_ffn, v_w_up, v_ffn_conv_w, v_w_down,
           v_g_ple, v_w_ple_gate, v_w_ple):
    return _step(x, p, g_mix, w_in, b_fox_f, fox_q_gain, fox_k_gain, sc_conv_w, dn_conv_w, dn_a_log, dn_dt_bias,
                 dn_norm_gain, w_branch, w_o, g_ffn, w_up, ffn_conv_w, w_down, g_ple, w_ple_gate, w_ple, loss_target,
                 m_g_mix, m_w_in, m_b_fox_f, m_fox_q_gain, m_fox_k_gain, m_sc_conv_w, m_dn_conv_w, m_dn_a_log,
                 m_dn_dt_bias, m_dn_norm_gain, m_w_branch, m_w_o, m_g_ffn, m_w_up, m_ffn_conv_w, m_w_down, m_g_ple,
                 m_w_ple_gate, m_w_ple, v_g_mix, v_w_in, v_b_fox_f, v_fox_q_gain, v_fox_k_gain, v_sc_conv_w,
                 v_dn_conv_w, v_dn_a_log, v_dn_dt_bias, v_dn_norm_gain, v_w_branch, v_w_o, v_g_ffn, v_w_up,
                 v_ffn_conv_w, v_w_down, v_g_ple, v_w_ple_gate, v_w_ple)


def _step(*args):
    names = ['x', 'p'] + WEIGHTS + ['loss_target'] + ['m_' + n for n in WEIGHTS] + ['v_' + n for n in WEIGHTS]
    assert len(args) == len(names)
    a = dict(zip(names, args))
    x, target = a['x'][0], a['loss_target'][0]
    p = a['p'][:, 0]
    _, _, c, _ = _place()
    chip = 2 * lax.axis_index("x") + lax.axis_index("y")

    big_pack = _pack({n: a[n] for n in SHARDED}, SEGS, BF16)
    conv_pack = _pack({n: a[n] for n in CONVW}, CONV_SEGS, F32)
    big_all, conv_all = _all_gather([big_pack, conv_pack], 'gather_weights')
    ws, sps = [], []
    for layer in range(DEPTH):
        w = _unpack_gathered(big_all, SEGS, layer)
        w.update(_unpack_gathered(conv_all, CONV_SEGS, layer))
        w['w_in'] = _perm_w_in(w['w_in'])
        ws.append(w)
        sps.append({n: a[n][layer][None, :] for n in SMALL})

    h, saved = x, []
    for layer in range(DEPTH):
        h, sv = _layer_fwd(h, p[layer], ws[layer], sps[layer], f'l{layer}')
        saved.append(sv)
    dh, loss_part = _loss_bwd(h, target, 'loss')
    grads = [None] * DEPTH
    for layer in reversed(range(DEPTH)):
        dh, g = _layer_bwd(dh, saved[layer], ws[layer], sps[layer], f'l{layer}')
        g['w_in'] = _unperm_w_in(g['w_in'])
        grads[layer] = g
    grad_x = dh[None]

    by_dest = _pack_by_dest(grads, SEGS).reshape(4, 2, PACK_ROWS, PACK_W)
    keep = lax.dynamic_index_in_dim(by_dest, c, axis=1, keepdims=False)
    give = lax.dynamic_index_in_dim(by_dest, 1 - c, axis=1, keepdims=False).astype(BF16)
    got = _sibling_swap(give.reshape(4 * PACK_ROWS, PACK_W), 'grad_sibling_swap')
    part, part_b = _ew(lambda k, r: (k + r.astype(F32),) * 2, [keep.reshape(4 * PACK_ROWS, PACK_W), got], [],
                       [(PACK_W, F32), (PACK_W, BF16)], tm=128, name='grad_chip_sum')
    others = _chip_exchange(part_b.reshape(4, PACK_ROWS, PACK_W), 'grad_chip_exchange')
    own = lax.dynamic_index_in_dim(part.reshape(4, PACK_ROWS, PACK_W), chip, axis=0, keepdims=False)
    w_pack = _pack({n: a[n] for n in SHARDED}, SEGS, F32)
    m_pack = _pack({n: a['m_' + n] for n in SHARDED}, SEGS, F32)
    v_pack = _pack({n: a['v_' + n] for n in SHARDED}, SEGS, F32)
    res = _adamw_call(w_pack, m_pack, v_pack, [own, others[0], others[1], others[2]], 'adamw_sharded')
    big_out = [_unpack(r, SEGS) for r in res]

    small_g = {n: jnp.stack([grads[layer][n].reshape(-1) for layer in range(DEPTH)]) for n in SMALL}
    (small_all,) = _all_gather([_pack_small(small_g, loss_part[0, 0])], 'gather_small_grads')
    res = _adamw_call(_pack_small({n: a[n] for n in SMALL}), _pack_small({n: a['m_' + n] for n in SMALL}),
                      _pack_small({n: a['v_' + n] for n in SMALL}), [small_all[d] for d in range(N_DEV)], 'adamw_replicated')
    loss = res[0].reshape(-1)[SMALL_LOSS_AT]
    small_out = [_unpack_small(r) for r in res]

    outs = [loss, grad_x]
    for k in range(4):
        outs += [big_out[k][n] if n in SHARDED else small_out[k][n] for n in WEIGHTS]
    return tuple(outs)
```

```python
import functools

import jax
import jax.numpy as jnp
from jax import lax
from jax.experimental import pallas as pl
from jax.experimental.pallas import tpu as pltpu

F32 = jnp.float32
BF16 = jnp.bfloat16
HI = lax.Precision.HIGHEST

D_MODEL = 1024
DEPTH = 2
N_DEV = 8
PLE_DIM = 256
BW = 512
FOX_HEADS, FOX_DH = 8, 64
DN_HEADS, DN_DH = 4, 128
DN_CHUNK = 64
D_FF = 2816
EPS = 1e-6
NEG = -1e30

ADAM_LR, ADAM_B1, ADAM_B2, ADAM_EPS, ADAM_WD, ADAM_STEP = 0.001, 0.9, 0.999, 1e-08, 0.01, 10

C_FQ, C_FK, C_FV = 0, 512, 1024
C_SB, C_SC, C_SV = 1536, 2048, 2560
C_DQ, C_DK, C_DV, C_DZ = 3072, 3584, 4096, 4608
C_GATE = 5120
C_SMALL = 8192
IN_P = 8320
IN_ORIG = 8208

WEIGHTS = ['g_mix', 'w_in', 'b_fox_f', 'fox_q_gain', 'fox_k_gain', 'sc_conv_w', 'dn_conv_w', 'dn_a_log',
           'dn_dt_bias', 'dn_norm_gain', 'w_branch', 'w_o', 'g_ffn', 'w_up', 'ffn_conv_w', 'w_down', 'g_ple',
           'w_ple_gate', 'w_ple']
BIG = {'w_in': 1, 'w_branch': 2, 'w_o': 0, 'w_up': 1, 'w_down': 0, 'w_ple_gate': 0, 'w_ple': 1}
CONVW = {'sc_conv_w': 1, 'dn_conv_w': 1, 'ffn_conv_w': 1}
SHARDED = {**BIG, **CONVW}
SMALL = [n for n in WEIGHTS if n not in SHARDED]
MESH = pl.DeviceIdType.MESH


def _sigmoid(x):
    return 0.5 * (jnp.tanh(0.5 * x) + 1.0)


def _silu(x):
    return x * _sigmoid(x)


def _log1pexp_negabs(z):
    return jnp.log(1.0 + jnp.exp(-jnp.abs(z)))


def _log_sigmoid(z):
    return jnp.minimum(z, 0.0) - _log1pexp_negabs(z)


def _softplus(z):
    return jnp.maximum(z, 0.0) + _log1pexp_negabs(z)


def _rms(x, g):
    return x * lax.rsqrt(jnp.mean(x * x, axis=-1, keepdims=True) + EPS) * g


def _l2(x):
    return x * lax.rsqrt(jnp.sum(x * x, axis=-1, keepdims=True) + EPS)


def _dot(a, b, dims, precision=None):
    return lax.dot_general(a, b, (dims, ((), ())), preferred_element_type=F32, precision=precision)


NN = ((1,), (0,))
NT = ((1,), (1,))
TN = ((0,), (0,))


def _shift_down(x, s):
    if s == 0:
        return x
    t = lax.broadcasted_iota(jnp.int32, x.shape, 0)
    return jnp.where(t >= s, pltpu.roll(x, s, 0), 0.0)


def _shift_up(x, s):
    if s == 0:
        return x
    n = x.shape[0]
    t = lax.broadcasted_iota(jnp.int32, x.shape, 0)
    return jnp.where(t < n - s, pltpu.roll(x, n - s, 0), 0.0)


def _conv(x, w):
    k = w.shape[0]
    y = w[k - 1:k] * x
    for j in range(k - 1):
        y = y + w[j:j + 1] * _shift_down(x, k - 1 - j)
    return y


def _conv_bwd(x, w, dy):
    k = w.shape[0]
    dx = w[k - 1:k] * dy
    dws = []
    for j in range(k - 1):
        dx = dx + w[j:j + 1] * _shift_up(dy, k - 1 - j)
        dws.append(jnp.sum(dy * _shift_down(x, k - 1 - j), axis=0, keepdims=True))
    dws.append(jnp.sum(dy * x, axis=0, keepdims=True))
    return dx, dws


def _mm(a, b, mode, outs, *, epi=None, epi_args=(), tm=512, tn=512, name):
    if mode == 'nn':
        (m, k), (k2, n) = a.shape, b.shape
    elif mode == 'nt':
        (m, k), (n, k2) = a.shape, b.shape
    else:
        (k, m), (k2, n) = a.shape, b.shape
    assert k == k2, (a.shape, b.shape, mode)
    tm, tn = min(tm, m), min(tn, n)
    assert m % tm == 0 and n % tn == 0, (m, n, tm, tn)
    dims = {'nn': NN, 'nt': NT, 'tn': TN}[mode]
    a_spec = pl.BlockSpec((k, tm), lambda i, j: (0, i)) if mode == 'tn' else pl.BlockSpec((tm, k), lambda i, j: (i, 0))
    b_spec = pl.BlockSpec((tn, k), lambda i, j: (j, 0)) if mode == 'nt' else pl.BlockSpec((k, tn), lambda i, j: (0, j))
    e_specs = [pl.BlockSpec((1, tn), lambda i, j: (0, j)) if e.shape[0] == 1 else pl.BlockSpec((tm, tn), lambda i, j: (i, j))
               for e in epi_args]
    ne, no = len(epi_args), len(outs)
    cast_a = a.dtype != BF16

    def body(a_ref, b_ref, *rest):
        if cast_a:
            a_sc = rest[-1]

            @pl.when(pl.program_id(1) == 0)
            def _():
                a_sc[...] = a_ref[...].astype(BF16)
            av = a_sc[...]
        else:
            av = a_ref[...]
        acc = _dot(av, b_ref[...].astype(BF16), dims)
        vals = epi(acc, *[e[...] for e in rest[:ne]]) if epi is not None else (acc,)
        for o_ref, v in zip(rest[ne:ne + no], vals):
            o_ref[...] = v.astype(o_ref.dtype)

    res = pl.pallas_call(
        body, grid=(m // tm, n // tn),
        in_specs=[a_spec, b_spec] + e_specs,
        out_specs=[pl.BlockSpec((tm, tn), lambda i, j: (i, j)) for _ in outs],
        out_shape=[jax.ShapeDtypeStruct((m, n), dt) for dt in outs],
        scratch_shapes=[pltpu.VMEM(a_spec.block_shape, BF16)] if cast_a else [],
        name=name,
    )(a, b, *epi_args)
    return res[0] if no == 1 else res


def _ew(fn, tiled, bcast, outs, reds=(), *, tm=256, name):
    secs = [(t, 0, t.shape[1]) if not isinstance(t, tuple) else t for t in tiled]
    m = secs[0][0].shape[0]
    tm = min(tm, m)
    assert m % tm == 0
    in_specs = []
    for arr, off, w in secs:
        assert off % w == 0
        in_specs.append(pl.BlockSpec((tm, w), functools.partial(lambda i, c: (i, c), c=off // w)))
    in_specs += [pl.BlockSpec(b.shape, lambda i: (0, 0)) for b in bcast]
    nin, no = len(in_specs), len(outs)

    def body(*refs):
        vals = fn(*[r[...] for r in refs[:nin]])
        for r, v in zip(refs[nin:nin + no], vals[:no]):
            r[...] = v.astype(r.dtype)
        i = pl.program_id(0)
        for r, v in zip(refs[nin + no:], vals[no:]):
            @pl.when(i == 0)
            def _():
                r[...] = v

            @pl.when(i > 0)
            def _():
                r[...] += v

    res = pl.pallas_call(
        body, grid=(m // tm,), in_specs=in_specs,
        out_specs=[pl.BlockSpec((tm, c), lambda i: (i, 0)) for c, _ in outs] + [pl.BlockSpec(s, lambda i: (0, 0)) for s in reds],
        out_shape=[jax.ShapeDtypeStruct((m, c), dt) for c, dt in outs] + [jax.ShapeDtypeStruct(s, F32) for s in reds],
        name=name,
    )(*[s[0] for s in secs], *bcast)
    return res


def _cb(fn, cols, params, outs, pouts, *, tc, nblk, name):
    t = cols[0][0].shape[0]
    in_specs = []
    for arr, off in cols:
        assert off % tc == 0
        in_specs.append(pl.BlockSpec((t, tc), functools.partial(lambda c, o: (0, o + c), o=off // tc)))
    for arr, off in params:
        in_specs.append(pl.BlockSpec((arr.shape[0], tc), functools.partial(lambda c, o: (0, o + c), o=off // tc)))
    nin, no = len(in_specs), len(outs)

    def body(*refs):
        vals = fn(*[r[...] for r in refs[:nin]])
        for r, v in zip(refs[nin:], vals):
            r[...] = v.astype(r.dtype)

    return pl.pallas_call(
        body, grid=(nblk,), in_specs=in_specs,
        out_specs=[pl.BlockSpec((t, tc), lambda c: (0, c)) for _ in outs] + [pl.BlockSpec((k, tc), lambda c: (0, c)) for k in pouts],
        out_shape=[jax.ShapeDtypeStruct((t, nblk * tc), dt) for dt in outs] + [jax.ShapeDtypeStruct((k, nblk * tc), F32) for k in pouts],
        name=name,
    )(*[c[0] for c in cols], *[p[0] for p in params])


CUM_BLK = 256


def _prep_point(s, bias, alog):
    lane = lax.broadcasted_iota(jnp.int32, s.shape, 1)
    z = s + bias
    return jnp.where(lane < 8, _log_sigmoid(z),
                     jnp.where(lane < 12, _sigmoid(s),
                               jnp.where(lane < 16, -jnp.exp(alog) * _softplus(z), 0.0)))


def _tri(n, upper):
    r = lax.broadcasted_iota(jnp.int32, (n, n), 0)
    c = lax.broadcasted_iota(jnp.int32, (n, n), 1)
    return (r <= c if upper else r >= c).astype(F32)


def _prep_fwd(proj, bias_row, alog_row, name):
    t = proj.shape[0]
    nb = t // CUM_BLK

    def body(s_ref, b_ref, a_ref, o_ref):
        pre = _prep_point(s_ref[...], b_ref[...], a_ref[...])
        lane = lax.broadcasted_iota(jnp.int32, (CUM_BLK, 128), 1)
        tri = _tri(CUM_BLK, False)
        carry = jnp.zeros((1, 128), F32)
        for blk in range(nb):
            xb = pre[blk * CUM_BLK:(blk + 1) * CUM_BLK]
            cb = _dot(tri, xb, NN, HI) + carry
            carry = cb[CUM_BLK - 1:CUM_BLK]
            o_ref[blk * CUM_BLK:(blk + 1) * CUM_BLK, :] = jnp.where(lane < 8, cb, xb)

    return pl.pallas_call(
        body, grid=(1,),
        in_specs=[pl.BlockSpec((t, 128), lambda i: (0, C_SMALL // 128)), pl.BlockSpec((1, 128), lambda i: (0, 0)),
                  pl.BlockSpec((1, 128), lambda i: (0, 0))],
        out_specs=pl.BlockSpec((t, 128), lambda i: (0, 0)),
        out_shape=jax.ShapeDtypeStruct((t, 128), F32), name=name,
    )(proj, bias_row, alog_row)


def _prep_bwd(proj, bias_row, alog_row, daux, name):
    t = proj.shape[0]
    nb = t // CUM_BLK

    def body(s_ref, b_ref, a_ref, d_ref, ds_ref, db_ref, da_ref, dpre_sc):
        lane = lax.broadcasted_iota(jnp.int32, (CUM_BLK, 128), 1)
        tri = _tri(CUM_BLK, True)
        carry = jnp.zeros((1, 128), F32)
        for blk in reversed(range(nb)):
            db = d_ref[blk * CUM_BLK:(blk + 1) * CUM_BLK, :]
            cb = _dot(tri, db, NN, HI) + carry
            carry = cb[0:1]
            dpre_sc[blk * CUM_BLK:(blk + 1) * CUM_BLK, :] = jnp.where(lane < 8, cb, db)
        _, vjp = jax.vjp(_prep_point, s_ref[...], b_ref[...], a_ref[...])
        ds, dbias, dalog = vjp(dpre_sc[...])
        ds_ref[...] = ds.astype(ds_ref.dtype)
        db_ref[...] = dbias
        da_ref[...] = dalog

    return pl.pallas_call(
        body, grid=(1,),
        in_specs=[pl.BlockSpec((t, 128), lambda i: (0, C_SMALL // 128)), pl.BlockSpec((1, 128), lambda i: (0, 0)),
                  pl.BlockSpec((1, 128), lambda i: (0, 0)), pl.BlockSpec((t, 128), lambda i: (0, 0))],
        out_specs=[pl.BlockSpec((t, 128), lambda i: (0, 0)), pl.BlockSpec((1, 128), lambda i: (0, 0)),
                   pl.BlockSpec((1, 128), lambda i: (0, 0))],
        out_shape=[jax.ShapeDtypeStruct((t, 128), BF16), jax.ShapeDtypeStruct((1, 128), F32), jax.ShapeDtypeStruct((1, 128), F32)],
        scratch_shapes=[pltpu.VMEM((t, 128), F32)], name=name,
    )(proj, bias_row, alog_row, daux)


ATT_TQ = 256
FOX_SCALE = FOX_DH ** -0.5


def _qnorm(q, g):
    return _rms(q, g) * FOX_SCALE


def _att_scores(qn_blk, kn, cfc_blk, cfr, qi, tq, kend):
    s = _dot(qn_blk.astype(BF16), kn[:kend].astype(BF16), NT) + cfc_blk - cfr[:, :kend]
    row = lax.broadcasted_iota(jnp.int32, (tq, kend), 0) + qi * tq
    col = lax.broadcasted_iota(jnp.int32, (tq, kend), 1)
    return s, row >= col


def _att_fwd(qh, kh, vh, qg, kg, cfc, cfr, name):
    h, t, dh = qh.shape
    tq = min(ATT_TQ, t)

    def body(q_ref, k_ref, v_ref, qg_ref, kg_ref, cfc_ref, cfr_ref, o_ref, lse_ref):
        qn = _qnorm(q_ref[0], qg_ref[...])
        kn = _rms(k_ref[0], kg_ref[...])
        v = v_ref[0].astype(BF16)
        cfr = cfr_ref[0]
        for qi in range(t // tq):
            kend = (qi + 1) * tq
            rows = slice(qi * tq, kend)
            s, mask = _att_scores(qn[rows], kn, cfc_ref[0, rows, :], cfr, qi, tq, kend)
            s = jnp.where(mask, s, NEG)
            m = jnp.max(s, axis=1, keepdims=True)
            p = jnp.exp(s - m)
            l = jnp.sum(p, axis=1, keepdims=True)
            o_ref[0, rows, :] = _dot(p.astype(BF16), v[:kend], NN) / l
            lse_ref[0, rows, :] = m + jnp.log(l)

    hd = lambda i: (i, 0, 0)
    return pl.pallas_call(
        body, grid=(h,),
        in_specs=[pl.BlockSpec((1, t, dh), hd)] * 3 + [pl.BlockSpec((1, dh), lambda i: (0, 0))] * 2
        + [pl.BlockSpec((1, t, 1), hd), pl.BlockSpec((1, 1, t), hd)],
        out_specs=[pl.BlockSpec((1, t, dh), hd), pl.BlockSpec((1, t, 1), hd)],
        out_shape=[jax.ShapeDtypeStruct((h, t, dh), F32), jax.ShapeDtypeStruct((h, t, 1), F32)], name=name,
    )(qh, kh, vh, qg, kg, cfc, cfr)


def _att_bwd(qh, kh, vh, qg, kg, cfc, cfr, lse, oh, doh, name):
    h, t, dh = qh.shape
    tq = min(ATT_TQ, t)

    def body(q_ref, k_ref, v_ref, qg_ref, kg_ref, cfc_ref, cfr_ref, lse_ref, o_ref, do_ref,
             dq_ref, dk_ref, dv_ref, dcfc_ref, dcfr_ref, dqg_ref, dkg_ref, dqn_sc, dkn_sc, dv_sc, dcfr_sc):
        qn, vjp_q = jax.vjp(_qnorm, q_ref[0], qg_ref[...])
        kn, vjp_k = jax.vjp(_rms, k_ref[0], kg_ref[...])
        v = v_ref[0].astype(BF16)
        cfr = cfr_ref[0]
        do = do_ref[0]
        delta = jnp.sum(do * o_ref[0], axis=1, keepdims=True)
        dkn_sc[...] = jnp.zeros_like(dkn_sc)
        dv_sc[...] = jnp.zeros_like(dv_sc)
        dcfr_sc[...] = jnp.zeros_like(dcfr_sc)
        for qi in range(t // tq):
            kend = (qi + 1) * tq
            rows = slice(qi * tq, kend)
            s, mask = _att_scores(qn[rows], kn, cfc_ref[0, rows, :], cfr, qi, tq, kend)
            p = jnp.where(mask, jnp.exp(jnp.where(mask, s, NEG) - lse_ref[0, rows, :]), 0.0)
            do_b = do[rows].astype(BF16)
            dv_sc[0:kend, :] += _dot(p.astype(BF16), do_b, TN)
            dp = _dot(do_b, v[:kend], NT)
            ds = p * (dp - delta[rows])
            ds_b = ds.astype(BF16)
            dqn_sc[rows, :] = _dot(ds_b, kn[:kend].astype(BF16), NN)
            dkn_sc[0:kend, :] += _dot(ds_b, qn[rows].astype(BF16), TN)
            dcfc_ref[0, rows, :] = jnp.sum(ds, axis=1, keepdims=True)
            dcfr_sc[:, 0:kend] -= jnp.sum(ds, axis=0, keepdims=True)
        dq, dqg = vjp_q(dqn_sc[...])
        dk, dkg = vjp_k(dkn_sc[...])
        dq_ref[0] = dq
        dk_ref[0] = dk
        dv_ref[0] = dv_sc[...]
        dcfr_ref[0] = dcfr_sc[...]
        dqg_ref[0] = dqg
        dkg_ref[0] = dkg

    hd = lambda i: (i, 0, 0)
    big = pl.BlockSpec((1, t, dh), hd)
    col = pl.BlockSpec((1, t, 1), hd)
    row = pl.BlockSpec((1, 1, t), hd)
    gsp = pl.BlockSpec((1, 1, dh), hd)
    return pl.pallas_call(
        body, grid=(h,),
        in_specs=[big] * 3 + [pl.BlockSpec((1, dh), lambda i: (0, 0))] * 2 + [col, row, col, big, big],
        out_specs=[big, big, big, col, row, gsp, gsp],
        out_shape=[jax.ShapeDtypeStruct((h, t, dh), F32)] * 3 + [jax.ShapeDtypeStruct((h, t, 1), F32), jax.ShapeDtypeStruct((h, 1, t), F32)]
        + [jax.ShapeDtypeStruct((h, 1, dh), F32)] * 2,
        scratch_shapes=[pltpu.VMEM((t, dh), F32)] * 3 + [pltpu.VMEM((1, t), F32)], name=name,
    )(qh, kh, vh, qg, kg, cfc, cfr, lse, oh, doh)


DN_SCALE = DN_DH ** -0.5


def _dn_chunk(qc, kc, vc, g, beta, state):
    c = qc.shape[0]
    ii = lax.broadcasted_iota(jnp.int32, (c, c), 0)
    jj = lax.broadcasted_iota(jnp.int32, (c, c), 1)
    incl, strict = ii >= jj, ii > jj
    lower = incl.astype(F32)
    dm = _dot(lower, jnp.where(strict, g, 0.0), NN, HI)
    decay = jnp.where(incl, jnp.exp(jnp.where(incl, dm, 0.0)), 0.0)
    gcum = _dot(lower, g * jnp.ones((1, DN_DH), F32), NN, HI)
    eg = jnp.exp(gcum)
    glast = gcum[c - 1:c]
    kb = kc * beta
    n1 = jnp.where(strict, _dot(kb, kc, NT, HI) * decay, 0.0)
    x = jnp.concatenate([vc * beta, kb * eg], axis=1)
    x = x - _dot(n1, x, NN, HI)
    pw = n1
    for _ in range(5):
        pw = _dot(pw, pw, NN, HI)
        x = x + _dot(pw, x, NN, HI)
    u, kcum = x[:, :DN_DH], x[:, DN_DH:]
    qk = jnp.where(incl, _dot(qc, kc, NT, HI) * decay, 0.0)
    v_new = u - _dot(kcum, state, NN, HI)
    out = _dot(qc * eg, state, NN, HI) + _dot(qk, v_new, NN, HI)
    new_state = state * jnp.exp(glast) + _dot(kc * jnp.exp(glast - gcum), v_new, TN, HI)
    return out, new_state


def _dn_pre_q(c):
    return _l2(_silu(c)) * DN_SCALE


def _dn_pre_k(c):
    return _l2(_silu(c))


def _dn_post(o, z, ng):
    return _rms(o, ng) * _silu(z)


def _dn_specs(t):
    cblk = lambda o: pl.BlockSpec((t, DN_DH), functools.partial(lambda h, o: (0, o + h), o=o // DN_DH))
    wblk = lambda o: pl.BlockSpec((4, DN_DH), functools.partial(lambda h, o: (0, o + h), o=o // DN_DH))
    proj_specs = [cblk(C_DQ), cblk(C_DK), cblk(C_DV), cblk(C_DZ)]
    w_specs = [wblk(0), wblk(BW), wblk(2 * BW)]
    gb_spec = pl.BlockSpec((1, t, 2), lambda h: (h, 0, 0))
    return proj_specs, w_specs, gb_spec


def _dn_fwd(proj, conv_w, gb, ng, name):
    t = proj.shape[0]
    nc = t // DN_CHUNK
    proj_specs, w_specs, gb_spec = _dn_specs(t)

    def body(q_ref, k_ref, v_ref, z_ref, wq_ref, wk_ref, wv_ref, gb_ref, ng_ref, y_ref, o_ref, st_ref, qn_sc, kn_sc, vv_sc):
        qn_sc[...] = _dn_pre_q(_conv(q_ref[...], wq_ref[...]))
        kn_sc[...] = _dn_pre_k(_conv(k_ref[...], wk_ref[...]))
        vv_sc[...] = _silu(_conv(v_ref[...], wv_ref[...]))

        def chunk(n, state):
            r = pl.ds(pl.multiple_of(n * DN_CHUNK, DN_CHUNK), DN_CHUNK)
            gbv = gb_ref[0, r, :]
            out, new_state = _dn_chunk(qn_sc[r, :], kn_sc[r, :], vv_sc[r, :], gbv[:, 0:1], gbv[:, 1:2], state)
            st_ref[0, n] = state
            o_ref[r, :] = out
            return new_state

        lax.fori_loop(0, nc, chunk, jnp.zeros((DN_DH, DN_DH), F32))
        y_ref[...] = _dn_post(o_ref[...], z_ref[...], ng_ref[...])

    hblk = pl.BlockSpec((t, DN_DH), lambda h: (0, h))
    return pl.pallas_call(
        body, grid=(DN_HEADS,),
        in_specs=proj_specs + w_specs + [gb_spec, pl.BlockSpec((1, DN_DH), lambda h: (0, 0))],
        out_specs=[hblk, hblk, pl.BlockSpec((1, nc, DN_DH, DN_DH), lambda h: (h, 0, 0, 0))],
        out_shape=[jax.ShapeDtypeStruct((t, BW), F32), jax.ShapeDtypeStruct((t, BW), F32),
                   jax.ShapeDtypeStruct((DN_HEADS, nc, DN_DH, DN_DH), F32)],
        scratch_shapes=[pltpu.VMEM((t, DN_DH), F32)] * 3, name=name,
    )(proj, proj, proj, proj, conv_w, conv_w, conv_w, gb, ng)


def _dn_bwd(proj, conv_w, gb, ng, o, states, dy, name):
    t = proj.shape[0]
    nc = t // DN_CHUNK
    proj_specs, w_specs, gb_spec = _dn_specs(t)

    def body(q_ref, k_ref, v_ref, z_ref, wq_ref, wk_ref, wv_ref, gb_ref, ng_ref, o_ref, st_ref, dy_ref,
             dq_ref, dk_ref, dv_ref, dz_ref, dwq_ref, dwk_ref, dwv_ref, dgb_ref, dng_ref,
             qn_sc, kn_sc, vv_sc, do_sc, dqn_sc, dkn_sc, dvv_sc):
        cq = _conv(q_ref[...], wq_ref[...])
        ck = _conv(k_ref[...], wk_ref[...])
        cv = _conv(v_ref[...], wv_ref[...])
        qn, vjp_q = jax.vjp(_dn_pre_q, cq)
        kn, vjp_k = jax.vjp(_dn_pre_k, ck)
        vv, vjp_v = jax.vjp(_silu, cv)
        qn_sc[...] = qn
        kn_sc[...] = kn
        vv_sc[...] = vv
        _, vjp_y = jax.vjp(_dn_post, o_ref[...], z_ref[...], ng_ref[...])
        do, dz, dng = vjp_y(dy_ref[...])
        do_sc[...] = do
        dz_ref[...] = dz.astype(dz_ref.dtype)
        dng_ref[0] = dng

        def chunk(i, dstate):
            n = nc - 1 - i
            r = pl.ds(pl.multiple_of(n * DN_CHUNK, DN_CHUNK), DN_CHUNK)
            gbv = gb_ref[0, r, :]
            _, vjp = jax.vjp(_dn_chunk, qn_sc[r, :], kn_sc[r, :], vv_sc[r, :], gbv[:, 0:1], gbv[:, 1:2], st_ref[0, n])
            dqc, dkc, dvc, dg, dbeta, dprev = vjp((do_sc[r, :], dstate))
            dqn_sc[r, :] = dqc
            dkn_sc[r, :] = dkc
            dvv_sc[r, :] = dvc
            dgb_ref[0, r, :] = jnp.concatenate([dg, dbeta], axis=1)
            return dprev

        lax.fori_loop(0, nc, chunk, jnp.zeros((DN_DH, DN_DH), F32))
        for x_ref, w_ref, vjp, d_sc, dx_ref, dw_ref in ((q_ref, wq_ref, vjp_q, dqn_sc, dq_ref, dwq_ref),
                                                       (k_ref, wk_ref, vjp_k, dkn_sc, dk_ref, dwk_ref),
                                                       (v_ref, wv_ref, vjp_v, dvv_sc, dv_ref, dwv_ref)):
            (dc,) = vjp(d_sc[...])
            dx, dws = _conv_bwd(x_ref[...], w_ref[...], dc)
            dx_ref[...] = dx.astype(dx_ref.dtype)
            for j, dw in enumerate(dws):
                dw_ref[j:j + 1, :] = dw

    hblk = pl.BlockSpec((t, DN_DH), lambda h: (0, h))
    wout = pl.BlockSpec((4, DN_DH), lambda h: (0, h))
    return pl.pallas_call(
        body, grid=(DN_HEADS,),
        in_specs=proj_specs + w_specs + [gb_spec, pl.BlockSpec((1, DN_DH), lambda h: (0, 0)), hblk,
                                         pl.BlockSpec((1, nc, DN_DH, DN_DH), lambda h: (h, 0, 0, 0)), hblk],
        out_specs=[hblk] * 4 + [wout] * 3 + [gb_spec, pl.BlockSpec((1, 1, DN_DH), lambda h: (h, 0, 0))],
        out_shape=[jax.ShapeDtypeStruct((t, BW), BF16)] * 4 + [jax.ShapeDtypeStruct((4, BW), F32)] * 3
        + [jax.ShapeDtypeStruct((DN_HEADS, t, 2), F32), jax.ShapeDtypeStruct((DN_HEADS, 1, DN_DH), F32)],
        scratch_shapes=[pltpu.VMEM((t, DN_DH), F32)] * 7, name=name,
    )(proj, proj, proj, proj, conv_w, conv_w, conv_w, gb, ng, o, states, dy)


MERGE_TM, MERGE_TN = 512, 512


def _merge_specs(t):
    tm, tn = min(MERGE_TM, t), MERGE_TN
    y_spec = pl.BlockSpec((tm, BW), lambda i, j: (i, 0))
    w_spec = pl.BlockSpec((3, BW, tn), lambda i, j: (0, 0, j))
    gate_specs = [pl.BlockSpec((tm, tn), functools.partial(lambda i, j, o: (i, o + j), o=(C_GATE + n * D_MODEL) // tn))
                  for n in range(3)]
    return tm, tn, [y_spec] * 3 + [w_spec] + gate_specs


def _merge_fwd(ys, wb, proj, name):
    t = proj.shape[0]
    tm, tn, in_specs = _merge_specs(t)

    def body(y0, y1, y2, w_ref, g0, g1, g2, o_ref):
        acc = jnp.zeros((tm, tn), F32)
        for n, (y, g) in enumerate(((y0, g0), (y1, g1), (y2, g2))):
            acc = acc + _dot(y[...].astype(BF16), w_ref[n], NN) * _sigmoid(g[...])
        o_ref[...] = acc.astype(o_ref.dtype)

    return pl.pallas_call(
        body, grid=(t // tm, D_MODEL // tn), in_specs=in_specs,
        out_specs=pl.BlockSpec((tm, tn), lambda i, j: (i, j)),
        out_shape=jax.ShapeDtypeStruct((t, D_MODEL), BF16), name=name,
    )(*ys, wb, proj, proj, proj)


def _merge_bwd(ys, wb, proj, dmerged, name):
    t = proj.shape[0]
    tm, tn, in_specs = _merge_specs(t)

    def body(y0, y1, y2, w_ref, g0, g1, g2, dm_ref, dg_ref, dt_ref):
        dm = dm_ref[...]
        for n, (y, g) in enumerate(((y0, g0), (y1, g1), (y2, g2))):
            tn_ = _dot(y[...].astype(BF16), w_ref[n], NN)
            sg = _sigmoid(g[...])
            dg_ref[n] = (dm * tn_ * sg * (1.0 - sg)).astype(dg_ref.dtype)
            dt_ref[n] = (dm * sg).astype(dt_ref.dtype)

    o3 = pl.BlockSpec((3, tm, tn), lambda i, j: (0, i, j))
    return pl.pallas_call(
        body, grid=(t // tm, D_MODEL // tn), in_specs=in_specs + [pl.BlockSpec((tm, tn), lambda i, j: (i, j))],
        out_specs=[o3, o3], out_shape=[jax.ShapeDtypeStruct((3, t, D_MODEL), BF16)] * 2, name=name,
    )(*ys, wb, proj, proj, proj, dmerged)


def _heads(a, off):
    t = a.shape[0]
    return a[:, off:off + BW].reshape(t, FOX_HEADS, FOX_DH).transpose(1, 0, 2)


def _unheads(a):
    h, t, d = a.shape
    return a.transpose(1, 0, 2).reshape(t, h * d)


def _prep_rows(sp):
    z4, z112 = jnp.zeros((1, 4), F32), jnp.zeros((1, 112), F32)
    bias_row = jnp.concatenate([sp['b_fox_f'], z4, sp['dn_dt_bias'], z112], axis=1)
    alog_row = jnp.concatenate([jnp.zeros((1, 12), F32), sp['dn_a_log'], z112], axis=1)
    return bias_row, alog_row


def _sc_fwd_tile(sb, sc, sv, w):
    return (sb * _conv(sc * sv, w),)


def _ffn_act(ug, uv):
    return _silu(ug) * uv


def _perm_w_in(wi):
    z = jnp.zeros((wi.shape[0], IN_P - IN_ORIG), wi.dtype)
    return jnp.concatenate([wi[:, 0:1536], wi[:, 1544:3080], wi[:, 3080:4616], wi[:, 4624:5136], wi[:, 5136:8208],
                            wi[:, 1536:1544], wi[:, 4616:4624], z], axis=1)


def _unperm_w_in(gp):
    return jnp.concatenate([gp[:, 0:1536], gp[:, 8192:8200], gp[:, 1536:3072], gp[:, 3072:4608], gp[:, 8200:8208],
                            gp[:, 4608:5120], gp[:, 5120:8192]], axis=1)


def _layer_fwd(x, p_i, w, sp, tag):
    t = x.shape[0]
    sv = {'x': x}
    (hn,) = _ew(lambda a, g: (_rms(a, g),), [x], [sp['g_mix']], [(D_MODEL, BF16)], name=f'rms_mix_{tag}')
    proj = _mm(hn, w['w_in'], 'nn', [F32], tn=640, name=f'in_proj_{tag}')
    bias_row, alog_row = _prep_rows(sp)
    aux = _prep_fwd(proj, bias_row, alog_row, f'prep_{tag}')
    qh, kh, vh = _heads(proj, C_FQ), _heads(proj, C_FK), _heads(proj, C_FV)
    cf = aux[:, :FOX_HEADS].T
    cfc, cfr = cf[:, :, None], cf[:, None, :]
    oh, lse = _att_fwd(qh, kh, vh, sp['fox_q_gain'], sp['fox_k_gain'], cfc, cfr, f'fox_fwd_{tag}')
    y_fox = _unheads(oh)
    (y_sc,) = _cb(_sc_fwd_tile, [(proj, C_SB), (proj, C_SC), (proj, C_SV)], [(w['sc_conv_w'], 0)], [F32], [],
                  tc=256, nblk=2, name=f'sc_fwd_{tag}')
    gb = jnp.stack([aux[:, 12:16].T, aux[:, 8:12].T], axis=-1)
    y_dn, o_dn, states = _dn_fwd(proj, w['dn_conv_w'], gb, sp['dn_norm_gain'], f'dn_fwd_{tag}')
    ys = (y_fox, y_sc, y_dn)
    merged = _merge_fwd(ys, w['w_branch'], proj, f'merge_fwd_{tag}')
    x1 = _mm(merged, w['w_o'], 'nn', [F32], epi=lambda acc, r: (acc + r,), epi_args=(x,), name=f'o_proj_{tag}')
    (hf,) = _ew(lambda a, g: (_rms(a, g),), [x1], [sp['g_ffn']], [(D_MODEL, BF16)], name=f'rms_ffn_{tag}')
    up = _mm(hf, w['w_up'], 'nn', [F32], name=f'up_proj_{tag}')
    (act,) = _cb(lambda ug, uv, wg, wv: (_ffn_act(_conv(ug, wg), _conv(uv, wv)),), [(up, 0), (up, D_FF)],
                 [(w['ffn_conv_w'], 0), (w['ffn_conv_w'], D_FF)], [BF16], [], tc=256, nblk=D_FF // 256, name=f'ffn_act_{tag}')
    x2 = _mm(act, w['w_down'], 'nn', [F32], epi=lambda acc, r: (acc + r,), epi_args=(x1,), name=f'down_proj_{tag}')
    (hp,) = _ew(lambda a, g: (_rms(a, g),), [x2], [sp['g_ple']], [(D_MODEL, BF16)], name=f'rms_ple_{tag}')
    gp = _mm(hp, w['w_ple_gate'], 'nn', [F32], name=f'ple_gate_{tag}')
    x3 = _mm(p_i, w['w_ple'], 'nn', [F32], epi=lambda acc, g, r: (r + _sigmoid(g) * acc,), epi_args=(gp, x2), name=f'ple_{tag}')
    sv.update(hn=hn, proj=proj, aux=aux, qh=qh, kh=kh, vh=vh, cfc=cfc, cfr=cfr, oh=oh, lse=lse, ys=ys, gb=gb, o_dn=o_dn,
              states=states, merged=merged, x1=x1, hf=hf, up=up, act=act, x2=x2, hp=hp, gp=gp, p=p_i,
              bias_row=bias_row, alog_row=alog_row)
    return x3, sv


def _rms_bwd(x, g, dh, dres, name):
    def fn(xv, dhv, dr, gv):
        _, vjp = jax.vjp(_rms, xv, gv)
        dx, dg = vjp(dhv)
        return dr + dx, dg
    return _ew(fn, [x, dh, dres], [g], [(D_MODEL, F32)], [(1, D_MODEL)], name=name)


def _layer_bwd(dx3, sv, w, sp, tag):
    t = dx3.shape[0]
    g = {}
    def ple_epi(acc, gpv, d):
        s = _sigmoid(gpv)
        return d * acc * s * (1.0 - s), d * s
    dgp, de = _mm(sv['p'], w['w_ple'], 'nn', [BF16, BF16], epi=ple_epi, epi_args=(sv['gp'], dx3), name=f'ple_bwd_{tag}')
    g['w_ple'] = _mm(sv['p'], de, 'tn', [F32], name=f'd_w_ple_{tag}')
    g['w_ple_gate'] = _mm(sv['hp'], dgp, 'tn', [F32], name=f'd_w_ple_gate_{tag}')
    dhp = _mm(dgp, w['w_ple_gate'], 'nt', [F32], name=f'd_hp_{tag}')
    dx2, g['g_ple'] = _rms_bwd(sv['x2'], sp['g_ple'], dhp, dx3, f'rms_ple_bwd_{tag}')
    dact = _mm(dx2, w['w_down'], 'nt', [F32], tn=256, name=f'd_act_{tag}')
    g['w_down'] = _mm(sv['act'], dx2, 'tn', [F32], tm=256, name=f'd_w_down_{tag}')

    def ffn_bwd_tile(ug, uv, da, wg, wv):
        cg, cv = _conv(ug, wg), _conv(uv, wv)
        _, vjp = jax.vjp(_ffn_act, cg, cv)
        dcg, dcv = vjp(da)
        dug, dwg = _conv_bwd(ug, wg, dcg)
        duv, dwv = _conv_bwd(uv, wv, dcv)
        return dug, duv, jnp.concatenate(dwg, axis=0), jnp.concatenate(dwv, axis=0)
    dupg, dupv, dwg, dwv = _cb(ffn_bwd_tile, [(sv['up'], 0), (sv['up'], D_FF), (dact, 0)],
                               [(w['ffn_conv_w'], 0), (w['ffn_conv_w'], D_FF)], [BF16, BF16], [3, 3], tc=256,
                               nblk=D_FF // 256, name=f'ffn_act_bwd_{tag}')
    dup = jnp.concatenate([dupg, dupv], axis=1)
    g['ffn_conv_w'] = jnp.concatenate([dwg, dwv], axis=1)
    g['w_up'] = _mm(sv['hf'], dup, 'tn', [F32], name=f'd_w_up_{tag}')
    dhf = _mm(dup, w['w_up'], 'nt', [F32], name=f'd_hf_{tag}')
    dx1, g['g_ffn'] = _rms_bwd(sv['x1'], sp['g_ffn'], dhf, dx2, f'rms_ffn_bwd_{tag}')
    dmerged = _mm(dx1, w['w_o'], 'nt', [F32], name=f'd_merged_{tag}')
    g['w_o'] = _mm(sv['merged'], dx1, 'tn', [F32], name=f'd_w_o_{tag}')
    dgate, dtn = _merge_bwd(sv['ys'], w['w_branch'], sv['proj'], dmerged, f'merge_bwd_{tag}')
    dys, dwb = [], []
    for n in range(3):
        dys.append(_mm(dtn[n], w['w_branch'][n], 'nt', [F32], name=f'd_y{n}_{tag}'))
        dwb.append(_mm(sv['ys'][n], dtn[n], 'tn', [F32], name=f'd_w_branch{n}_{tag}'))
    g['w_branch'] = jnp.stack(dwb)
    ddq, ddk, ddv, ddz, dwq, dwk, dwv_, dgb, dng = _dn_bwd(sv['proj'], w['dn_conv_w'], sv['gb'], sp['dn_norm_gain'],
                                                           sv['o_dn'], sv['states'], dys[2], f'dn_bwd_{tag}')
    g['dn_conv_w'] = jnp.concatenate([dwq, dwk, dwv_], axis=1)
    g['dn_norm_gain'] = jnp.sum(dng, axis=0)
    def sc_bwd_tile(sb, sc, svv, dy, wv):
        u = sc * svv
        dsb = dy * _conv(u, wv)
        du, dws = _conv_bwd(u, wv, dy * sb)
        return dsb, du * svv, du * sc, jnp.concatenate(dws, axis=0)
    dsb, dsc, dsv, g['sc_conv_w'] = _cb(sc_bwd_tile, [(sv['proj'], C_SB), (sv['proj'], C_SC), (sv['proj'], C_SV), (dys[1], 0)],
                                        [(w['sc_conv_w'], 0)], [BF16, BF16, BF16], [3], tc=256, nblk=2, name=f'sc_bwd_{tag}')
    doh = dys[0].reshape(t, FOX_HEADS, FOX_DH).transpose(1, 0, 2)
    dqh, dkh, dvh, dcfc, dcfr, dqg, dkg = _att_bwd(sv['qh'], sv['kh'], sv['vh'], sp['fox_q_gain'], sp['fox_k_gain'],
                                                   sv['cfc'], sv['cfr'], sv['lse'], sv['oh'], doh, f'fox_bwd_{tag}')
    g['fox_q_gain'] = jnp.sum(dqg, axis=0)
    g['fox_k_gain'] = jnp.sum(dkg, axis=0)
    dcf = (dcfc[:, :, 0] + dcfr[:, 0, :]).T
    daux = jnp.concatenate([dcf, dgb[:, :, 1].T, dgb[:, :, 0].T, jnp.zeros((t, 112), F32)], axis=1)
    dsmall, dbias, dalog = _prep_bwd(sv['proj'], sv['bias_row'], sv['alog_row'], daux, f'prep_bwd_{tag}')
    g['b_fox_f'] = dbias[:, 0:8]
    g['dn_dt_bias'] = dbias[:, 12:16]
    g['dn_a_log'] = dalog[:, 12:16]
    dproj = jnp.concatenate([_unheads(dqh).astype(BF16), _unheads(dkh).astype(BF16), _unheads(dvh).astype(BF16),
                             dsb, dsc, dsv, ddq, ddk, ddv, ddz, dgate[0], dgate[1], dgate[2], dsmall], axis=1)
    g['w_in'] = _mm(sv['hn'], dproj, 'tn', [F32], tn=640, name=f'd_w_in_{tag}')
    dhn = _mm(dproj, w['w_in'], 'nt', [F32], name=f'd_hn_{tag}')
    dx, g['g_mix'] = _rms_bwd(sv['x'], sp['g_mix'], dhn, dx1, f'rms_mix_bwd_{tag}')
    return dx, g


def _loss_bwd(y, target, name):
    inv = 1.0 / y.shape[1]

    def fn(yv, tv):
        err = yv - tv
        return err * inv, jnp.zeros((8, 128), F32) + 0.5 * inv * jnp.sum(err * err)
    return _ew(fn, [y, target], [], [(y.shape[1], F32)], [(8, 128)], name=name)


PACK_W = 1024
FULL_SHAPE = {'w_in': (D_MODEL, IN_ORIG), 'w_branch': (3, BW, D_MODEL), 'w_o': (D_MODEL, D_MODEL), 'w_up': (D_MODEL, 2 * D_FF),
              'w_down': (D_FF, D_MODEL), 'w_ple_gate': (D_MODEL, D_MODEL), 'w_ple': (PLE_DIM, D_MODEL),
              'sc_conv_w': (3, BW), 'dn_conv_w': (4, 3 * BW), 'ffn_conv_w': (3, 2 * D_FF)}
SMALL_SHAPE = {'g_mix': D_MODEL, 'b_fox_f': FOX_HEADS, 'fox_q_gain': FOX_DH, 'fox_k_gain': FOX_DH, 'dn_a_log': DN_HEADS,
               'dn_dt_bias': DN_HEADS, 'dn_norm_gain': DN_DH, 'g_ffn': D_MODEL, 'g_ple': D_MODEL}


def _shard_shape(name):
    s = list(FULL_SHAPE[name])
    s[SHARDED[name]] //= N_DEV
    return tuple(s)


def _full_from_gathered(g, name, layer):
    sh, ax = _shard_shape(name), SHARDED[name]
    blocks = jnp.moveaxis(g[:, layer], 0, ax)
    return blocks.reshape(sh[:ax] + (N_DEV * sh[ax],) + sh[ax + 1:])


def _by_dest(full, name):
    sh, ax = _shard_shape(name), SHARDED[name]
    return jnp.moveaxis(full.reshape(sh[:ax] + (N_DEV, sh[ax]) + sh[ax + 1:]), ax, 0)


def _flat_pack(arrs):
    flat = jnp.concatenate([a.reshape(-1).astype(F32) for a in arrs])
    rows = -(-flat.shape[0] // (8 * PACK_W)) * 8
    return jnp.pad(flat, (0, rows * PACK_W - flat.shape[0])).reshape(rows, PACK_W)


def _flat_unpack(pack, shapes):
    flat, out, off = pack.reshape(-1), [], 0
    for s in shapes:
        n = 1
        for d in s:
            n *= d
        out.append(flat[off:off + n].reshape(s))
        off += n
    return out


SMALL_SHAPES = [(DEPTH, SMALL_SHAPE[n]) for n in SMALL]
SMALL_LOSS_AT = sum(DEPTH * SMALL_SHAPE[n] for n in SMALL)
CONV_SHARD_SHAPES = [(DEPTH,) + _shard_shape(n) for n in CONVW]
CONV_FULL_SHAPES = [(DEPTH,) + FULL_SHAPE[n] for n in CONVW]


def _rows2d(a):
    return a.reshape(-1, a.shape[-1])


def _pick_tm(m, width):
    best = None
    for tm in range(16, m + 1, 16):
        if m % tm == 0 and tm * width * 4 <= (1 << 20):
            best = tm
    return best if best is not None else m


HBM_SPEC = pl.BlockSpec(memory_space=pltpu.HBM)


def _place():
    x, y, c = lax.axis_index("x"), lax.axis_index("y"), lax.axis_index("c")
    return x, y, c, [(1 - x, y), (x, 1 - y), (1 - x, 1 - y)]


def _all_gather(arrs, name):
    n = len(arrs)

    def body(*refs):
        ins, outs = refs[:n], refs[n:2 * n]
        send_sems, recv_sems, local_sems = refs[2 * n:]
        x, y, c, chips = _place()
        me, sibling = (x, y, c), (x, y, 1 - c)

        def block(a, p):
            return outs[a].at[4 * p[0] + 2 * p[1] + p[2]]

        def copy(a, k, blk, to, src=None):
            return pltpu.make_async_remote_copy(src_ref=block(a, blk) if src is None else src, dst_ref=block(a, blk),
                                                send_sem=send_sems.at[a, k], recv_sem=recv_sems.at[a, k],
                                                device_id=to, device_id_type=MESH)

        mine = [pltpu.make_async_copy(ins[a], block(a, me), local_sems.at[a]) for a in range(n)]
        first, passed = [], []
        for a in range(n):
            mine[a].start()
            first.append(copy(a, 0, me, sibling, src=ins[a]))
            first += [copy(a, 1 + j, me, (*chip, c), src=ins[a]) for j, chip in enumerate(chips)]
        for cp in first:
            cp.start()
        for j, chip in enumerate(chips):
            for a in range(n):
                copy(a, 1 + j, (*chip, c), me).wait_recv()
                fwd = copy(a, 4 + j, (*chip, c), sibling)
                fwd.start()
                passed.append(fwd)
        for a in range(n):
            copy(a, 0, sibling, me).wait_recv()
            for j, chip in enumerate(chips):
                copy(a, 4 + j, (*chip, 1 - c), me).wait_recv()
        for cp in first + passed:
            cp.wait_send()
        for a in range(n):
            mine[a].wait()

    return pl.pallas_call(
        body, in_specs=[HBM_SPEC] * n, out_specs=[HBM_SPEC] * n,
        out_shape=[jax.ShapeDtypeStruct((N_DEV,) + a.shape, a.dtype) for a in arrs],
        scratch_shapes=[pltpu.SemaphoreType.DMA((n, 7)), pltpu.SemaphoreType.DMA((n, 7)), pltpu.SemaphoreType.DMA((n,))],
        name=name,
    )(*arrs)


def _sibling_swap(arrs, name):
    n = len(arrs)

    def body(*refs):
        send_sems, recv_sems = refs[2 * n:]
        x, y, c, _ = _place()
        cps = [pltpu.make_async_remote_copy(src_ref=refs[a], dst_ref=refs[n + a], send_sem=send_sems.at[a], recv_sem=recv_sems.at[a],
                                            device_id=(x, y, 1 - c), device_id_type=MESH) for a in range(n)]
        for cp in cps:
            cp.start()
        for cp in cps:
            cp.wait()

    return pl.pallas_call(body, in_specs=[HBM_SPEC] * n, out_specs=[HBM_SPEC] * n,
                          out_shape=[jax.ShapeDtypeStruct(a.shape, a.dtype) for a in arrs],
                          scratch_shapes=[pltpu.SemaphoreType.DMA((n,)), pltpu.SemaphoreType.DMA((n,))], name=name)(*arrs)


def _chip_exchange(arrs, name):
    n = len(arrs)

    def body(*refs):
        send_sems, recv_sems = refs[2 * n:]
        x, y, c, chips = _place()
        cps = [pltpu.make_async_remote_copy(src_ref=refs[a].at[2 * cx + cy], dst_ref=refs[n + a].at[j], send_sem=send_sems.at[a, j],
                                            recv_sem=recv_sems.at[a, j], device_id=(cx, cy, c), device_id_type=MESH)
               for a in range(n) for j, (cx, cy) in enumerate(chips)]
        for cp in cps:
            cp.start()
        for cp in cps:
            cp.wait()

    return pl.pallas_call(body, in_specs=[HBM_SPEC] * n, out_specs=[HBM_SPEC] * n,
                          out_shape=[jax.ShapeDtypeStruct((3,) + a.shape[1:], a.dtype) for a in arrs],
                          scratch_shapes=[pltpu.SemaphoreType.DMA((n, 3)), pltpu.SemaphoreType.DMA((n, 3))], name=name)(*arrs)


def _adamw(w, g, m, v):
    m = ADAM_B1 * m + (1.0 - ADAM_B1) * g
    v = ADAM_B2 * v + (1.0 - ADAM_B2) * jnp.square(g)
    m_hat = m / (1.0 - ADAM_B1 ** ADAM_STEP)
    v_hat = v / (1.0 - ADAM_B2 ** ADAM_STEP)
    delta = -ADAM_LR * (m_hat / (jnp.sqrt(v_hat) + ADAM_EPS) + ADAM_WD * w)
    return delta, m, v


def _adamw_call(w, m, v, gparts, name):
    def fn(wv, mv, vv, *gs):
        g = gs[0].astype(F32)
        for gp in gs[1:]:
            g = g + gp.astype(F32)
        delta, m2, v2 = _adamw(wv, g, mv, vv)
        return g, delta, m2, v2
    rows, c = w.shape
    return _ew(fn, [w, m, v] + list(gparts), [], [(c, F32)] * 4, tm=_pick_tm(rows, c), name=name)


def kernel(x, p, g_mix, w_in, b_fox_f, fox_q_gain, fox_k_gain, sc_conv_w, dn_conv_w, dn_a_log, dn_dt_bias,
           dn_norm_gain, w_branch, w_o, g_ffn, w_up, ffn_conv_w, w_down, g_ple, w_ple_gate, w_ple, loss_target,
           m_g_mix, m_w_in, m_b_fox_f, m_fox_q_gain, m_fox_k_gain, m_sc_conv_w, m_dn_conv_w, m_dn_a_log,
           m_dn_dt_bias, m_dn_norm_gain, m_w_branch, m_w_o, m_g_ffn, m_w_up, m_ffn_conv_w, m_w_down, m_g_ple,
           m_w_ple_gate, m_w_ple, v_g_mix, v_w_in, v_b_fox_f, v_fox_q_gain, v_fox_k_gain, v_sc_conv_w, v_dn_conv_w,
           v_dn_a_log, v_dn_dt_bias, v_dn_norm_gain, v_w_branch, v_w_o, v_g_ffn, v_w_up, v_ffn_conv_w, v_w_down,
           v_g_ple, v_w_ple_gate, v_w_ple):
    return _step(x, p, g_mix, w_in, b_fox_f, fox_q_gain, fox_k_gain, sc_conv_w, dn_conv_w, dn_a_log, dn_dt_bias,
                 dn_norm_gain, w_branch, w_o, g_ffn, w_up, ffn_conv_w, w_down, g_ple, w_ple_gate, w_ple, loss_target,
                 m_g_mix, m_w_in, m_b_fox_f, m_fox_q_gain, m_fox_k_gain, m_sc_conv_w, m_dn_conv_w, m_dn_a_log,
                 m_dn_dt_bias, m_dn_norm_gain, m_w_branch, m_w_o, m_g_ffn, m_w_up, m_ffn_conv_w, m_w_down, m_g_ple,
                 m_w_ple_gate, m_w_ple, v_g_mix, v_w_in, v_b_fox_f, v_fox_q_gain, v_fox_k_gain, v_sc_conv_w,
                 v_dn_conv_w, v_dn_a_log, v_dn_dt_bias, v_dn_norm_gain, v_w_branch, v_w_o, v_g_ffn, v_w_up,
                 v_ffn_conv_w, v_w_down, v_g_ple, v_w_ple_gate, v_w_ple)


def _step(*args):
    names = ['x', 'p'] + WEIGHTS + ['loss_target'] + ['m_' + n for n in WEIGHTS] + ['v_' + n for n in WEIGHTS]
    assert len(args) == len(names)
    a = dict(zip(names, args))
    x, target = a['x'][0], a['loss_target'][0]
    p = a['p'][:, 0]
    _, _, c, _ = _place()
    chip = 2 * lax.axis_index("x") + lax.axis_index("y")

    dev = 4 * lax.axis_index("x") + 2 * lax.axis_index("y") + c

    gathered = _all_gather([a[n].astype(BF16) for n in BIG] + [_flat_pack([a[n] for n in CONVW])], 'gather_weights')
    conv_by_dev = [_flat_unpack(gathered[-1][d], CONV_SHARD_SHAPES) for d in range(N_DEV)]
    conv_full = {n: jnp.concatenate([conv_by_dev[d][i] for d in range(N_DEV)], axis=2) for i, n in enumerate(CONVW)}
    ws, sps = [], []
    for layer in range(DEPTH):
        w = {n: _full_from_gathered(g, n, layer) for n, g in zip(BIG, gathered)}
        w.update({n: conv_full[n][layer] for n in CONVW})
        w['w_in'] = _perm_w_in(w['w_in'])
        ws.append(w)
        sps.append({n: a[n][layer][None, :] for n in SMALL})

    h, saved = x, []
    for layer in range(DEPTH):
        h, sv = _layer_fwd(h, p[layer], ws[layer], sps[layer], f'l{layer}')
        saved.append(sv)
    dh, loss_part = _loss_bwd(h, target, 'loss')
    grads = [None] * DEPTH
    for layer in reversed(range(DEPTH)):
        dh, g = _layer_bwd(dh, saved[layer], ws[layer], sps[layer], f'l{layer}')
        g['w_in'] = _unperm_w_in(g['w_in'])
        grads[layer] = g
    grad_x = dh[None]

    keeps, gives = [], []
    for n in BIG:
        by_dest = jnp.stack([_by_dest(grads[layer][n], n) for layer in range(DEPTH)], axis=1)
        by_dest = by_dest.reshape((4, 2) + by_dest.shape[1:])
        keeps.append(lax.dynamic_index_in_dim(by_dest, c, axis=1, keepdims=False))
        gives.append(lax.dynamic_index_in_dim(by_dest, 1 - c, axis=1, keepdims=False).astype(BF16))
    gots = _sibling_swap(gives, 'grad_sibling_swap')
    parts, parts_b = [], []
    for n, keep, got in zip(BIG, keeps, gots):
        k2, g2 = _rows2d(keep), _rows2d(got)
        part, part_b = _ew(lambda k, r: (k + r.astype(F32),) * 2, [k2, g2], [], [(k2.shape[1], F32), (k2.shape[1], BF16)],
                           tm=_pick_tm(*k2.shape), name=f'grad_chip_sum_{n}')
        parts.append(part.reshape(keep.shape))
        parts_b.append(part_b.reshape(keep.shape))
    others = _chip_exchange(parts_b, 'grad_chip_exchange')
    out = {}
    for n, part, other in zip(BIG, parts, others):
        own = lax.dynamic_index_in_dim(part, chip, axis=0, keepdims=False)
        res = _adamw_call(_rows2d(a[n]), _rows2d(a['m_' + n]), _rows2d(a['v_' + n]),
                          [_rows2d(own)] + [_rows2d(other[j]) for j in range(3)], f'adamw_{n}')
        out[n] = [r.reshape(a[n].shape) for r in res]

    small_pack = _flat_pack([jnp.stack([grads[layer][n].reshape(-1) for layer in range(DEPTH)]) for n in SMALL] + [loss_part[0, 0]])
    conv_pack = _flat_pack([jnp.stack([grads[layer][n] for layer in range(DEPTH)]) for n in CONVW])
    small_all, conv_all = _all_gather([small_pack, conv_pack], 'gather_small_grads')
    res = _adamw_call(_flat_pack([a[n] for n in SMALL]), _flat_pack([a['m_' + n] for n in SMALL]),
                      _flat_pack([a['v_' + n] for n in SMALL]), [small_all[d] for d in range(N_DEV)], 'adamw_replicated')
    loss = res[0].reshape(-1)[SMALL_LOSS_AT]
    for k, r in enumerate(res):
        for n, val in zip(SMALL, _flat_unpack(r, SMALL_SHAPES)):
            out.setdefault(n, [None] * 4)[k] = val
    (conv_sum,) = _ew(lambda *gs: (functools.reduce(lambda s, t: s + t, gs),), [conv_all[d] for d in range(N_DEV)], [],
                      [(PACK_W, F32)], tm=conv_pack.shape[0], name='conv_grad_sum')
    conv_own = [lax.dynamic_slice_in_dim(g, dev * (g.shape[2] // N_DEV), g.shape[2] // N_DEV, axis=2)
                for g in _flat_unpack(conv_sum, CONV_FULL_SHAPES)]
    res = _adamw_call(_flat_pack([a[n] for n in CONVW]), _flat_pack([a['m_' + n] for n in CONVW]),
                      _flat_pack([a['v_' + n] for n in CONVW]), [_flat_pack(conv_own)], 'adamw_conv')
    for k, r in enumerate(res):
        for n, val in zip(CONVW, _flat_unpack(r, CONV_SHARD_SHAPES)):
            out.setdefault(n, [None] * 4)[k] = val

    outs = [loss, grad_x]
    for k in range(4):
        outs += [out[n][k] for n in WEIGHTS]
    return tuple(outs)
```

```python
import functools

import jax
import jax.numpy as jnp
from jax import lax
from jax.experimental import pallas as pl
from jax.experimental.pallas import tpu as pltpu

F32 = jnp.float32
BF16 = jnp.bfloat16
HI = lax.Precision.HIGHEST

D_MODEL = 1024
DEPTH = 2
N_DEV = 8
PLE_DIM = 256
BW = 512
FOX_HEADS, FOX_DH = 8, 64
DN_HEADS, DN_DH = 4, 128
DN_CHUNK = 64
D_FF = 2816
EPS = 1e-6
NEG = -1e30

ADAM_LR, ADAM_B1, ADAM_B2, ADAM_EPS, ADAM_WD, ADAM_STEP = 0.001, 0.9, 0.999, 1e-08, 0.01, 10

C_FQ, C_FK, C_FV = 0, 512, 1024
C_SB, C_SC, C_SV = 1536, 2048, 2560
C_DQ, C_DK, C_DV, C_DZ = 3072, 3584, 4096, 4608
C_GATE = 5120
C_SMALL = 8192
IN_P = 8320
IN_ORIG = 8208

WEIGHTS = ['g_mix', 'w_in', 'b_fox_f', 'fox_q_gain', 'fox_k_gain', 'sc_conv_w', 'dn_conv_w', 'dn_a_log',
           'dn_dt_bias', 'dn_norm_gain', 'w_branch', 'w_o', 'g_ffn', 'w_up', 'ffn_conv_w', 'w_down', 'g_ple',
           'w_ple_gate', 'w_ple']
BIG = {'w_in': 1, 'w_branch': 2, 'w_o': 0, 'w_up': 1, 'w_down': 0, 'w_ple_gate': 0, 'w_ple': 1}
CONVW = {'sc_conv_w': 1, 'dn_conv_w': 1, 'ffn_conv_w': 1}
SHARDED = {**BIG, **CONVW}
SMALL = [n for n in WEIGHTS if n not in SHARDED]
MESH = pl.DeviceIdType.MESH


def _sigmoid(x):
    return 0.5 * (jnp.tanh(0.5 * x) + 1.0)


def _silu(x):
    return x * _sigmoid(x)


def _log1pexp_negabs(z):
    return jnp.log(1.0 + jnp.exp(-jnp.abs(z)))


def _log_sigmoid(z):
    return jnp.minimum(z, 0.0) - _log1pexp_negabs(z)


def _softplus(z):
    return jnp.maximum(z, 0.0) + _log1pexp_negabs(z)


def _rms(x, g):
    return x * lax.rsqrt(jnp.mean(x * x, axis=-1, keepdims=True) + EPS) * g


def _l2(x):
    return x * lax.rsqrt(jnp.sum(x * x, axis=-1, keepdims=True) + EPS)


def _dot(a, b, dims, precision=None):
    return lax.dot_general(a, b, (dims, ((), ())), preferred_element_type=F32, precision=precision)


NN = ((1,), (0,))
NT = ((1,), (1,))
TN = ((0,), (0,))


def _shift_down(x, s):
    if s == 0:
        return x
    t = lax.broadcasted_iota(jnp.int32, x.shape, 0)
    return jnp.where(t >= s, pltpu.roll(x, s, 0), 0.0)


def _shift_up(x, s):
    if s == 0:
        return x
    n = x.shape[0]
    t = lax.broadcasted_iota(jnp.int32, x.shape, 0)
    return jnp.where(t < n - s, pltpu.roll(x, n - s, 0), 0.0)


def _conv(x, w):
    k = w.shape[0]
    y = w[k - 1:k] * x
    for j in range(k - 1):
        y = y + w[j:j + 1] * _shift_down(x, k - 1 - j)
    return y


def _conv_bwd(x, w, dy):
    k = w.shape[0]
    dx = w[k - 1:k] * dy
    dws = []
    for j in range(k - 1):
        dx = dx + w[j:j + 1] * _shift_up(dy, k - 1 - j)
        dws.append(jnp.sum(dy * _shift_down(x, k - 1 - j), axis=0, keepdims=True))
    dws.append(jnp.sum(dy * x, axis=0, keepdims=True))
    return dx, dws


def _mm(a, b, mode, outs, *, epi=None, epi_args=(), tm=512, tn=512, name):
    if mode == 'nn':
        (m, k), (k2, n) = a.shape, b.shape
    elif mode == 'nt':
        (m, k), (n, k2) = a.shape, b.shape
    else:
        (k, m), (k2, n) = a.shape, b.shape
    assert k == k2, (a.shape, b.shape, mode)
    tm, tn = min(tm, m), min(tn, n)
    assert m % tm == 0 and n % tn == 0, (m, n, tm, tn)
    dims = {'nn': NN, 'nt': NT, 'tn': TN}[mode]
    a_spec = pl.BlockSpec((k, tm), lambda i, j: (0, i)) if mode == 'tn' else pl.BlockSpec((tm, k), lambda i, j: (i, 0))
    b_spec = pl.BlockSpec((tn, k), lambda i, j: (j, 0)) if mode == 'nt' else pl.BlockSpec((k, tn), lambda i, j: (0, j))
    e_specs = [pl.BlockSpec((1, tn), lambda i, j: (0, j)) if e.shape[0] == 1 else pl.BlockSpec((tm, tn), lambda i, j: (i, j))
               for e in epi_args]
    ne, no = len(epi_args), len(outs)
    cast_a = a.dtype != BF16

    def body(a_ref, b_ref, *rest):
        if cast_a:
            a_sc = rest[-1]

            @pl.when(pl.program_id(1) == 0)
            def _():
                a_sc[...] = a_ref[...].astype(BF16)
            av = a_sc[...]
        else:
            av = a_ref[...]
        acc = _dot(av, b_ref[...].astype(BF16), dims)
        vals = epi(acc, *[e[...] for e in rest[:ne]]) if epi is not None else (acc,)
        for o_ref, v in zip(rest[ne:ne + no], vals):
            o_ref[...] = v.astype(o_ref.dtype)

    res = pl.pallas_call(
        body, grid=(m // tm, n // tn),
        in_specs=[a_spec, b_spec] + e_specs,
        out_specs=[pl.BlockSpec((tm, tn), lambda i, j: (i, j)) for _ in outs],
        out_shape=[jax.ShapeDtypeStruct((m, n), dt) for dt in outs],
        scratch_shapes=[pltpu.VMEM(a_spec.block_shape, BF16)] if cast_a else [],
        name=name,
    )(a, b, *epi_args)
    return res[0] if no == 1 else res


def _ew(fn, tiled, bcast, outs, reds=(), *, tm=256, name):
    secs = [(t, 0, t.shape[1]) if not isinstance(t, tuple) else t for t in tiled]
    m = secs[0][0].shape[0]
    tm = min(tm, m)
    assert m % tm == 0
    in_specs = []
    for arr, off, w in secs:
        assert off % w == 0
        in_specs.append(pl.BlockSpec((tm, w), functools.partial(lambda i, c: (i, c), c=off // w)))
    in_specs += [pl.BlockSpec(b.shape, lambda i: (0, 0)) for b in bcast]
    nin, no = len(in_specs), len(outs)

    def body(*refs):
        vals = fn(*[r[...] for r in refs[:nin]])
        for r, v in zip(refs[nin:nin + no], vals[:no]):
            r[...] = v.astype(r.dtype)
        i = pl.program_id(0)
        for r, v in zip(refs[nin + no:], vals[no:]):
            @pl.when(i == 0)
            def _():
                r[...] = v

            @pl.when(i > 0)
            def _():
                r[...] += v

    res = pl.pallas_call(
        body, grid=(m // tm,), in_specs=in_specs,
        out_specs=[pl.BlockSpec((tm, c), lambda i: (i, 0)) for c, _ in outs] + [pl.BlockSpec(s, lambda i: (0, 0)) for s in reds],
        out_shape=[jax.ShapeDtypeStruct((m, c), dt) for c, dt in outs] + [jax.ShapeDtypeStruct(s, F32) for s in reds],
        name=name,
    )(*[s[0] for s in secs], *bcast)
    return res


def _cb(fn, cols, params, outs, pouts, *, tc, nblk, name):
    t = cols[0][0].shape[0]
    in_specs = []
    for arr, off in cols:
        assert off % tc == 0
        in_specs.append(pl.BlockSpec((t, tc), functools.partial(lambda c, o: (0, o + c), o=off // tc)))
    for arr, off in params:
        in_specs.append(pl.BlockSpec((arr.shape[0], tc), functools.partial(lambda c, o: (0, o + c), o=off // tc)))
    nin, no = len(in_specs), len(outs)

    def body(*refs):
        vals = fn(*[r[...] for r in refs[:nin]])
        for r, v in zip(refs[nin:], vals):
            r[...] = v.astype(r.dtype)

    return pl.pallas_call(
        body, grid=(nblk,), in_specs=in_specs,
        out_specs=[pl.BlockSpec((t, tc), lambda c: (0, c)) for _ in outs] + [pl.BlockSpec((k, tc), lambda c: (0, c)) for k in pouts],
        out_shape=[jax.ShapeDtypeStruct((t, nblk * tc), dt) for dt in outs] + [jax.ShapeDtypeStruct((k, nblk * tc), F32) for k in pouts],
        name=name,
    )(*[c[0] for c in cols], *[p[0] for p in params])


CUM_BLK = 256


def _prep_point(s, bias, alog):
    lane = lax.broadcasted_iota(jnp.int32, s.shape, 1)
    z = s + bias
    return jnp.where(lane < 8, _log_sigmoid(z),
                     jnp.where(lane < 12, _sigmoid(s),
                               jnp.where(lane < 16, -jnp.exp(alog) * _softplus(z), 0.0)))


def _tri(n, upper):
    r = lax.broadcasted_iota(jnp.int32, (n, n), 0)
    c = lax.broadcasted_iota(jnp.int32, (n, n), 1)
    return (r <= c if upper else r >= c).astype(F32)


def _prep_fwd(proj, bias_row, alog_row, name):
    t = proj.shape[0]
    nb = t // CUM_BLK

    def body(s_ref, b_ref, a_ref, o_ref):
        pre = _prep_point(s_ref[...], b_ref[...], a_ref[...])
        lane = lax.broadcasted_iota(jnp.int32, (CUM_BLK, 128), 1)
        tri = _tri(CUM_BLK, False)
        carry = jnp.zeros((1, 128), F32)
        for blk in range(nb):
            xb = pre[blk * CUM_BLK:(blk + 1) * CUM_BLK]
            cb = _dot(tri, xb, NN, HI) + carry
            carry = cb[CUM_BLK - 1:CUM_BLK]
            o_ref[blk * CUM_BLK:(blk + 1) * CUM_BLK, :] = jnp.where(lane < 8, cb, xb)

    return pl.pallas_call(
        body, grid=(1,),
        in_specs=[pl.BlockSpec((t, 128), lambda i: (0, C_SMALL // 128)), pl.BlockSpec((1, 128), lambda i: (0, 0)),
                  pl.BlockSpec((1, 128), lambda i: (0, 0))],
        out_specs=pl.BlockSpec((t, 128), lambda i: (0, 0)),
        out_shape=jax.ShapeDtypeStruct((t, 128), F32), name=name,
    )(proj, bias_row, alog_row)


def _prep_bwd(proj, bias_row, alog_row, daux, name):
    t = proj.shape[0]
    nb = t // CUM_BLK

    def body(s_ref, b_ref, a_ref, d_ref, ds_ref, db_ref, da_ref, dpre_sc):
        lane = lax.broadcasted_iota(jnp.int32, (CUM_BLK, 128), 1)
        tri = _tri(CUM_BLK, True)
        carry = jnp.zeros((1, 128), F32)
        for blk in reversed(range(nb)):
            db = d_ref[blk * CUM_BLK:(blk + 1) * CUM_BLK, :]
            cb = _dot(tri, db, NN, HI) + carry
            carry = cb[0:1]
            dpre_sc[blk * CUM_BLK:(blk + 1) * CUM_BLK, :] = jnp.where(lane < 8, cb, db)
        _, vjp = jax.vjp(_prep_point, s_ref[...], b_ref[...], a_ref[...])
        ds, dbias, dalog = vjp(dpre_sc[...])
        ds_ref[...] = ds.astype(ds_ref.dtype)
        db_ref[...] = dbias
        da_ref[...] = dalog

    return pl.pallas_call(
        body, grid=(1,),
        in_specs=[pl.BlockSpec((t, 128), lambda i: (0, C_SMALL // 128)), pl.BlockSpec((1, 128), lambda i: (0, 0)),
                  pl.BlockSpec((1, 128), lambda i: (0, 0)), pl.BlockSpec((t, 128), lambda i: (0, 0))],
        out_specs=[pl.BlockSpec((t, 128), lambda i: (0, 0)), pl.BlockSpec((1, 128), lambda i: (0, 0)),
                   pl.BlockSpec((1, 128), lambda i: (0, 0))],
        out_shape=[jax.ShapeDtypeStruct((t, 128), BF16), jax.ShapeDtypeStruct((1, 128), F32), jax.ShapeDtypeStruct((1, 128), F32)],
        scratch_shapes=[pltpu.VMEM((t, 128), F32)], name=name,
    )(proj, bias_row, alog_row, daux)


ATT_TQ = 256
FOX_SCALE = FOX_DH ** -0.5


def _qnorm(q, g):
    return _rms(q, g) * FOX_SCALE


def _att_scores(qn_blk, kn, cfc_blk, cfr, qi, tq, kend):
    s = _dot(qn_blk.astype(BF16), kn[:kend].astype(BF16), NT) + cfc_blk - cfr[:, :kend]
    row = lax.broadcasted_iota(jnp.int32, (tq, kend), 0) + qi * tq
    col = lax.broadcasted_iota(jnp.int32, (tq, kend), 1)
    return s, row >= col


ATT_PAIR = 128 // FOX_DH


def _att_specs(t):
    pair = lambda off: pl.BlockSpec((t, 128), functools.partial(lambda i, o: (0, o + i), o=off // 128))
    gain = pl.BlockSpec((1, FOX_DH), lambda i: (0, 0))
    hd = lambda i: (i, 0, 0)
    col, row = pl.BlockSpec((ATT_PAIR, t, 1), hd), pl.BlockSpec((ATT_PAIR, 1, t), hd)
    return [pair(C_FQ), pair(C_FK), pair(C_FV), gain, gain, col, row], pl.BlockSpec((t, 128), lambda i: (0, i)), col, row


def _att_fwd(proj, qg, kg, cfc, cfr, name):
    t = proj.shape[0]
    tq = min(ATT_TQ, t)
    in_specs, pair_out, col, _ = _att_specs(t)

    def body(q_ref, k_ref, v_ref, qg_ref, kg_ref, cfc_ref, cfr_ref, o_ref, lse_ref):
        for e in range(ATT_PAIR):
            lanes = slice(e * FOX_DH, (e + 1) * FOX_DH)
            qn = _qnorm(q_ref[:, lanes], qg_ref[...])
            kn = _rms(k_ref[:, lanes], kg_ref[...])
            v = v_ref[:, lanes].astype(BF16)
            cfr = cfr_ref[e]
            for qi in range(t // tq):
                kend = (qi + 1) * tq
                rows = slice(qi * tq, kend)
                s, mask = _att_scores(qn[rows], kn, cfc_ref[e, rows, :], cfr, qi, tq, kend)
                s = jnp.where(mask, s, NEG)
                m = jnp.max(s, axis=1, keepdims=True)
                p = jnp.exp(s - m)
                l = jnp.sum(p, axis=1, keepdims=True)
                o_ref[rows, lanes] = _dot(p.astype(BF16), v[:kend], NN) / l
                lse_ref[e, rows, :] = m + jnp.log(l)

    return pl.pallas_call(
        body, grid=(FOX_HEADS // ATT_PAIR,), in_specs=in_specs, out_specs=[pair_out, col],
        out_shape=[jax.ShapeDtypeStruct((t, BW), F32), jax.ShapeDtypeStruct((FOX_HEADS, t, 1), F32)], name=name,
    )(proj, proj, proj, qg, kg, cfc, cfr)


def _att_bwd(proj, qg, kg, cfc, cfr, lse, o, do, name):
    t = proj.shape[0]
    tq = min(ATT_TQ, t)
    in_specs, pair_out, col, row = _att_specs(t)

    def body(q_ref, k_ref, v_ref, qg_ref, kg_ref, cfc_ref, cfr_ref, lse_ref, o_ref, do_ref,
             dq_ref, dk_ref, dv_ref, dcfc_ref, dcfr_ref, dqg_ref, dkg_ref, dqn_sc, dkn_sc, dv_sc, dcfr_sc):
        for e in range(ATT_PAIR):
            lanes = slice(e * FOX_DH, (e + 1) * FOX_DH)
            qn, vjp_q = jax.vjp(_qnorm, q_ref[:, lanes], qg_ref[...])
            kn, vjp_k = jax.vjp(_rms, k_ref[:, lanes], kg_ref[...])
            v = v_ref[:, lanes].astype(BF16)
            cfr = cfr_ref[e]
            do_e = do_ref[:, lanes]
            delta = jnp.sum(do_e * o_ref[:, lanes], axis=1, keepdims=True)
            dkn_sc[...] = jnp.zeros_like(dkn_sc)
            dv_sc[...] = jnp.zeros_like(dv_sc)
            dcfr_sc[...] = jnp.zeros_like(dcfr_sc)
            for qi in range(t // tq):
                kend = (qi + 1) * tq
                rows = slice(qi * tq, kend)
                s, mask = _att_scores(qn[rows], kn, cfc_ref[e, rows, :], cfr, qi, tq, kend)
                p = jnp.where(mask, jnp.exp(jnp.where(mask, s, NEG) - lse_ref[e, rows, :]), 0.0)
                do_b = do_e[rows].astype(BF16)
                dv_sc[0:kend, :] += _dot(p.astype(BF16), do_b, TN)
                dp = _dot(do_b, v[:kend], NT)
                ds = p * (dp - delta[rows])
                ds_b = ds.astype(BF16)
                dqn_sc[rows, :] = _dot(ds_b, kn[:kend].astype(BF16), NN)
                dkn_sc[0:kend, :] += _dot(ds_b, qn[rows].astype(BF16), TN)
                dcfc_ref[e, rows, :] = jnp.sum(ds, axis=1, keepdims=True)
                dcfr_sc[:, 0:kend] -= jnp.sum(ds, axis=0, keepdims=True)
            dq, dqg = vjp_q(dqn_sc[...])
            dk, dkg = vjp_k(dkn_sc[...])
            dq_ref[:, lanes] = dq.astype(dq_ref.dtype)
            dk_ref[:, lanes] = dk.astype(dk_ref.dtype)
            dv_ref[:, lanes] = dv_sc[...].astype(dv_ref.dtype)
            dcfr_ref[e] = dcfr_sc[...]
            dqg_ref[e] = dqg
            dkg_ref[e] = dkg

    gsp = pl.BlockSpec((ATT_PAIR, 1, FOX_DH), lambda i: (i, 0, 0))
    return pl.pallas_call(
        body, grid=(FOX_HEADS // ATT_PAIR,),
        in_specs=in_specs + [col, pair_out, pair_out],
        out_specs=[pair_out] * 3 + [col, row, gsp, gsp],
        out_shape=[jax.ShapeDtypeStruct((t, BW), BF16)] * 3
        + [jax.ShapeDtypeStruct((FOX_HEADS, t, 1), F32), jax.ShapeDtypeStruct((FOX_HEADS, 1, t), F32)]
        + [jax.ShapeDtypeStruct((FOX_HEADS, 1, FOX_DH), F32)] * 2,
        scratch_shapes=[pltpu.VMEM((t, FOX_DH), F32)] * 3 + [pltpu.VMEM((1, t), F32)], name=name,
    )(proj, proj, proj, qg, kg, cfc, cfr, lse, o, do)


DN_SCALE = DN_DH ** -0.5


DN_BATCH = 8


def _mm3(a, b, dims, batch=False):
    if batch:
        dn = (((dims[0][0] + 1,), (dims[1][0] + 1,)), ((0,), (0,)))
        dot = lambda p, q: lax.dot_general(p, q, dn, preferred_element_type=F32)
    else:
        dot = lambda p, q: _dot(p, q, dims)
    ah, bh = a.astype(BF16), b.astype(BF16)
    al, bl = (a - ah.astype(F32)).astype(BF16), (b - bh.astype(F32)).astype(BF16)
    return dot(ah, bh) + (dot(ah, bl) + dot(al, bh))


def _dn_local(qc, kc, vc, g, beta):
    nb, c, _ = qc.shape
    ii = lax.broadcasted_iota(jnp.int32, (c, c), 0)
    jj = lax.broadcasted_iota(jnp.int32, (c, c), 1)
    incl, strict = ii >= jj, ii > jj
    lower = jnp.broadcast_to(incl.astype(F32), (nb, c, c))
    eye = (ii == jj).astype(F32)
    mm = functools.partial(_mm3, batch=True)
    dm = mm(lower, jnp.where(strict, g, 0.0), NN)
    decay = jnp.where(incl, jnp.exp(jnp.where(incl, dm, 0.0)), 0.0)
    gcum = mm(lower, g * jnp.ones((1, 1, DN_DH), F32), NN)
    eg = jnp.exp(gcum)
    glast = gcum[:, c - 1:c]
    kb = kc * beta
    n1 = jnp.where(strict, mm(kb, kc, NT) * decay, 0.0)
    inv = eye - n1
    pw = n1
    for _ in range(5):
        pw = mm(pw, pw, NN)
        inv = inv + mm(pw, inv, NN)
    sol = mm(inv, jnp.concatenate([vc * beta, kb * eg], axis=2), NN)
    qk = jnp.where(incl, mm(qc, kc, NT) * decay, 0.0)
    return sol[:, :, :DN_DH], sol[:, :, DN_DH:], qk, qc * eg, kc * jnp.exp(glast - gcum), jnp.exp(glast)


def _dn_state(u, kcum, qk, qdec, kdec, egl, state):
    v_new = u - _mm3(kcum, state, NN)
    out = _mm3(qdec, state, NN) + _mm3(qk, v_new, NN)
    return out, state * egl + _mm3(kdec, v_new, TN)


def _dn_pre_q(c):
    return _l2(_silu(c)) * DN_SCALE


def _dn_pre_k(c):
    return _l2(_silu(c))


def _dn_post(o, z, ng):
    return _rms(o, ng) * _silu(z)


def _dn_specs(t):
    cblk = lambda o: pl.BlockSpec((t, DN_DH), functools.partial(lambda h, o: (0, o + h), o=o // DN_DH))
    wblk = lambda o: pl.BlockSpec((4, DN_DH), functools.partial(lambda h, o: (0, o + h), o=o // DN_DH))
    proj_specs = [cblk(C_DQ), cblk(C_DK), cblk(C_DV), cblk(C_DZ)]
    w_specs = [wblk(0), wblk(BW), wblk(2 * BW)]
    gb_spec = pl.BlockSpec((1, t, 2), lambda h: (h, 0, 0))
    return proj_specs, w_specs, gb_spec


def _chunk_rows(n, count=1):
    return pl.ds(pl.multiple_of(n * DN_CHUNK, DN_CHUNK), count * DN_CHUNK)


def _egl_rows(n, count=1):
    return pl.ds(pl.multiple_of(n * 8, 8), count * 8)


def _dn_local_inputs(n, qn_sc, kn_sc, vv_sc, gb_ref):
    r = _chunk_rows(n, DN_BATCH)
    split = lambda v: v.reshape(DN_BATCH, DN_CHUNK, v.shape[-1])
    gbv = split(gb_ref[0, r, :])
    return split(qn_sc[r, :]), split(kn_sc[r, :]), split(vv_sc[r, :]), gbv[:, :, 0:1], gbv[:, :, 1:2]


def _dn_local_phase(nc, qn_sc, kn_sc, vv_sc, gb_ref, loc):
    def step(i, carry):
        n = i * DN_BATCH
        vals = _dn_local(*_dn_local_inputs(n, qn_sc, kn_sc, vv_sc, gb_ref))
        for sc, val in zip(loc[:5], vals[:5]):
            sc[_chunk_rows(n, DN_BATCH), :] = val.reshape(DN_BATCH * DN_CHUNK, val.shape[-1])
        loc[5][_egl_rows(n, DN_BATCH), :] = jnp.broadcast_to(vals[5], (DN_BATCH, 8, DN_DH)).reshape(DN_BATCH * 8, DN_DH)
        return carry

    lax.fori_loop(0, nc // DN_BATCH, step, 0)


def _dn_loc_scratch(t, nc):
    big = pltpu.VMEM((t, DN_DH), F32)
    return [big, big, pltpu.VMEM((t, DN_CHUNK), F32), big, big, pltpu.VMEM((nc * 8, DN_DH), F32)]


def _dn_fwd(proj, conv_w, gb, ng, name):
    t = proj.shape[0]
    nc = t // DN_CHUNK
    assert nc % DN_BATCH == 0
    proj_specs, w_specs, gb_spec = _dn_specs(t)

    def body(q_ref, k_ref, v_ref, z_ref, wq_ref, wk_ref, wv_ref, gb_ref, ng_ref, y_ref, o_ref, st_ref, qn_sc, kn_sc, vv_sc, *loc):
        qn_sc[...] = _dn_pre_q(_conv(q_ref[...], wq_ref[...]))
        kn_sc[...] = _dn_pre_k(_conv(k_ref[...], wk_ref[...]))
        vv_sc[...] = _silu(_conv(v_ref[...], wv_ref[...]))
        _dn_local_phase(nc, qn_sc, kn_sc, vv_sc, gb_ref, loc)
        u_sc, kcum_sc, qk_sc, qdec_sc, kdec_sc, egl_sc = loc

        def chunk(n, state):
            r = _chunk_rows(n)
            egl = egl_sc[_egl_rows(n), :][0:1]
            out, new_state = _dn_state(u_sc[r, :], kcum_sc[r, :], qk_sc[r, :], qdec_sc[r, :], kdec_sc[r, :], egl, state)
            st_ref[0, n] = state
            o_ref[r, :] = out
            return new_state

        lax.fori_loop(0, nc, chunk, jnp.zeros((DN_DH, DN_DH), F32))
        y_ref[...] = _dn_post(o_ref[...], z_ref[...], ng_ref[...])

    hblk = pl.BlockSpec((t, DN_DH), lambda h: (0, h))
    return pl.pallas_call(
        body, grid=(DN_HEADS,),
        in_specs=proj_specs + w_specs + [gb_spec, pl.BlockSpec((1, DN_DH), lambda h: (0, 0))],
        out_specs=[hblk, hblk, pl.BlockSpec((1, nc, DN_DH, DN_DH), lambda h: (h, 0, 0, 0))],
        out_shape=[jax.ShapeDtypeStruct((t, BW), F32), jax.ShapeDtypeStruct((t, BW), F32),
                   jax.ShapeDtypeStruct((DN_HEADS, nc, DN_DH, DN_DH), F32)],
        scratch_shapes=[pltpu.VMEM((t, DN_DH), F32)] * 3 + _dn_loc_scratch(t, nc), name=name,
    )(proj, proj, proj, proj, conv_w, conv_w, conv_w, gb, ng)


def _dn_bwd(proj, conv_w, gb, ng, o, states, dy, name):
    t = proj.shape[0]
    nc = t // DN_CHUNK
    proj_specs, w_specs, gb_spec = _dn_specs(t)
    nloc = 6

    def body(q_ref, k_ref, v_ref, z_ref, wq_ref, wk_ref, wv_ref, gb_ref, ng_ref, o_ref, st_ref, dy_ref,
             dq_ref, dk_ref, dv_ref, dz_ref, dwq_ref, dwk_ref, dwv_ref, dgb_ref, dng_ref,
             qn_sc, kn_sc, vv_sc, do_sc, *rest):
        loc, dloc = rest[:nloc], rest[nloc:]
        qn_sc[...] = _dn_pre_q(_conv(q_ref[...], wq_ref[...]))
        kn_sc[...] = _dn_pre_k(_conv(k_ref[...], wk_ref[...]))
        vv_sc[...] = _silu(_conv(v_ref[...], wv_ref[...]))
        _, vjp_y = jax.vjp(_dn_post, o_ref[...], z_ref[...], ng_ref[...])
        do, dz, dng = vjp_y(dy_ref[...])
        do_sc[...] = do
        dz_ref[...] = dz.astype(dz_ref.dtype)
        dng_ref[0] = dng
        _dn_local_phase(nc, qn_sc, kn_sc, vv_sc, gb_ref, loc)
        u_sc, kcum_sc, qk_sc, qdec_sc, kdec_sc, egl_sc = loc

        def state_bwd(i, dstate):
            n = nc - 1 - i
            r = _chunk_rows(n)
            r8 = _egl_rows(n)
            _, vjp = jax.vjp(_dn_state, u_sc[r, :], kcum_sc[r, :], qk_sc[r, :], qdec_sc[r, :], kdec_sc[r, :],
                             egl_sc[r8, :][0:1], st_ref[0, n])
            du, dkcum, dqk, dqdec, dkdec, degl, dprev = vjp((do_sc[r, :], dstate))
            for d_sc, val in zip(dloc[:5], (du, dkcum, dqk, dqdec, dkdec)):
                d_sc[r, :] = val
            dloc[5][r8, :] = jnp.broadcast_to(degl, (8, DN_DH))
            return dprev

        lax.fori_loop(0, nc, state_bwd, jnp.zeros((DN_DH, DN_DH), F32))

        def local_bwd(i, carry):
            n = i * DN_BATCH
            r = _chunk_rows(n, DN_BATCH)
            _, vjp = jax.vjp(_dn_local, *_dn_local_inputs(n, qn_sc, kn_sc, vv_sc, gb_ref))
            cts = tuple(d_sc[r, :].reshape(DN_BATCH, DN_CHUNK, d_sc.shape[-1]) for d_sc in dloc[:5])
            cts += (dloc[5][_egl_rows(n, DN_BATCH), :].reshape(DN_BATCH, 8, DN_DH)[:, 0:1],)
            dqc, dkc, dvc, dg, dbeta = vjp(cts)
            for d_sc, val in zip((dloc[0], dloc[1], dloc[3]), (dqc, dkc, dvc)):
                d_sc[r, :] = val.reshape(DN_BATCH * DN_CHUNK, DN_DH)
            dgb_ref[0, r, :] = jnp.concatenate([dg, dbeta], axis=2).reshape(DN_BATCH * DN_CHUNK, 2)
            return carry

        lax.fori_loop(0, nc // DN_BATCH, local_bwd, 0)
        for x_ref, w_ref, pre, d_sc, dx_ref, dw_ref in ((q_ref, wq_ref, _dn_pre_q, dloc[0], dq_ref, dwq_ref),
                                                       (k_ref, wk_ref, _dn_pre_k, dloc[1], dk_ref, dwk_ref),
                                                       (v_ref, wv_ref, _silu, dloc[3], dv_ref, dwv_ref)):
            _, vjp = jax.vjp(pre, _conv(x_ref[...], w_ref[...]))
            (dc,) = vjp(d_sc[...])
            dx, dws = _conv_bwd(x_ref[...], w_ref[...], dc)
            dx_ref[...] = dx.astype(dx_ref.dtype)
            for j, dw in enumerate(dws):
                dw_ref[j:j + 1, :] = dw

    hblk = pl.BlockSpec((t, DN_DH), lambda h: (0, h))
    wout = pl.BlockSpec((4, DN_DH), lambda h: (0, h))
    return pl.pallas_call(
        body, grid=(DN_HEADS,),
        in_specs=proj_specs + w_specs + [gb_spec, pl.BlockSpec((1, DN_DH), lambda h: (0, 0)), hblk,
                                         pl.BlockSpec((1, nc, DN_DH, DN_DH), lambda h: (h, 0, 0, 0)), hblk],
        out_specs=[hblk] * 4 + [wout] * 3 + [gb_spec, pl.BlockSpec((1, 1, DN_DH), lambda h: (h, 0, 0))],
        out_shape=[jax.ShapeDtypeStruct((t, BW), BF16)] * 4 + [jax.ShapeDtypeStruct((4, BW), F32)] * 3
        + [jax.ShapeDtypeStruct((DN_HEADS, t, 2), F32), jax.ShapeDtypeStruct((DN_HEADS, 1, DN_DH), F32)],
        scratch_shapes=[pltpu.VMEM((t, DN_DH), F32)] * 4 + _dn_loc_scratch(t, nc) * 2, name=name,
    )(proj, proj, proj, proj, conv_w, conv_w, conv_w, gb, ng, o, states, dy)


MERGE_TM, MERGE_TN = 512, 512


def _merge_specs(t):
    tm, tn = min(MERGE_TM, t), MERGE_TN
    y_spec = pl.BlockSpec((tm, BW), lambda i, j: (i, 0))
    w_spec = pl.BlockSpec((3, BW, tn), lambda i, j: (0, 0, j))
    gate_specs = [pl.BlockSpec((tm, tn), functools.partial(lambda i, j, o: (i, o + j), o=(C_GATE + n * D_MODEL) // tn))
                  for n in range(3)]
    return tm, tn, [y_spec] * 3 + [w_spec] + gate_specs


def _merge_fwd(ys, wb, proj, name):
    t = proj.shape[0]
    tm, tn, in_specs = _merge_specs(t)

    def body(y0, y1, y2, w_ref, g0, g1, g2, o_ref):
        acc = jnp.zeros((tm, tn), F32)
        for n, (y, g) in enumerate(((y0, g0), (y1, g1), (y2, g2))):
            acc = acc + _dot(y[...].astype(BF16), w_ref[n], NN) * _sigmoid(g[...])
        o_ref[...] = acc.astype(o_ref.dtype)

    return pl.pallas_call(
        body, grid=(t // tm, D_MODEL // tn), in_specs=in_specs,
        out_specs=pl.BlockSpec((tm, tn), lambda i, j: (i, j)),
        out_shape=jax.ShapeDtypeStruct((t, D_MODEL), BF16), name=name,
    )(*ys, wb, proj, proj, proj)


def _merge_bwd(ys, wb, proj, dmerged, name):
    t = proj.shape[0]
    tm, tn, in_specs = _merge_specs(t)

    def body(y0, y1, y2, w_ref, g0, g1, g2, dm_ref, dg_ref, dt_ref):
        dm = dm_ref[...]
        for n, (y, g) in enumerate(((y0, g0), (y1, g1), (y2, g2))):
            tn_ = _dot(y[...].astype(BF16), w_ref[n], NN)
            sg = _sigmoid(g[...])
            dg_ref[n] = (dm * tn_ * sg * (1.0 - sg)).astype(dg_ref.dtype)
            dt_ref[n] = (dm * sg).astype(dt_ref.dtype)

    o3 = pl.BlockSpec((3, tm, tn), lambda i, j: (0, i, j))
    return pl.pallas_call(
        body, grid=(t // tm, D_MODEL // tn), in_specs=in_specs + [pl.BlockSpec((tm, tn), lambda i, j: (i, j))],
        out_specs=[o3, o3], out_shape=[jax.ShapeDtypeStruct((3, t, D_MODEL), BF16)] * 2, name=name,
    )(*ys, wb, proj, proj, proj, dmerged)


def _prep_rows(sp):
    z4, z112 = jnp.zeros((1, 4), F32), jnp.zeros((1, 112), F32)
    bias_row = jnp.concatenate([sp['b_fox_f'], z4, sp['dn_dt_bias'], z112], axis=1)
    alog_row = jnp.concatenate([jnp.zeros((1, 12), F32), sp['dn_a_log'], z112], axis=1)
    return bias_row, alog_row


def _sc_fwd_tile(sb, sc, sv, w):
    return (sb * _conv(sc * sv, w),)


def _ffn_act(ug, uv):
    return _silu(ug) * uv


def _perm_w_in(wi):
    z = jnp.zeros((wi.shape[0], IN_P - IN_ORIG), wi.dtype)
    return jnp.concatenate([wi[:, 0:1536], wi[:, 1544:3080], wi[:, 3080:4616], wi[:, 4624:5136], wi[:, 5136:8208],
                            wi[:, 1536:1544], wi[:, 4616:4624], z], axis=1)


def _unperm_w_in(gp):
    return jnp.concatenate([gp[:, 0:1536], gp[:, 8192:8200], gp[:, 1536:3072], gp[:, 3072:4608], gp[:, 8200:8208],
                            gp[:, 4608:5120], gp[:, 5120:8192]], axis=1)


def _layer_fwd(x, p_i, w, sp, tag):
    t = x.shape[0]
    sv = {'x': x}
    (hn,) = _ew(lambda a, g: (_rms(a, g),), [x], [sp['g_mix']], [(D_MODEL, BF16)], name=f'rms_mix_{tag}')
    proj = _mm(hn, w['w_in'], 'nn', [F32], tn=640, name=f'in_proj_{tag}')
    bias_row, alog_row = _prep_rows(sp)
    aux = _prep_fwd(proj, bias_row, alog_row, f'prep_{tag}')
    cf = aux[:, :FOX_HEADS].T
    cfc, cfr = cf[:, :, None], cf[:, None, :]
    y_fox, lse = _att_fwd(proj, sp['fox_q_gain'], sp['fox_k_gain'], cfc, cfr, f'fox_fwd_{tag}')
    (y_sc,) = _cb(_sc_fwd_tile, [(proj, C_SB), (proj, C_SC), (proj, C_SV)], [(w['sc_conv_w'], 0)], [F32], [],
                  tc=256, nblk=2, name=f'sc_fwd_{tag}')
    gb = jnp.stack([aux[:, 12:16].T, aux[:, 8:12].T], axis=-1)
    y_dn, o_dn, states = _dn_fwd(proj, w['dn_conv_w'], gb, sp['dn_norm_gain'], f'dn_fwd_{tag}')
    ys = (y_fox, y_sc, y_dn)
    merged = _merge_fwd(ys, w['w_branch'], proj, f'merge_fwd_{tag}')
    x1 = _mm(merged, w['w_o'], 'nn', [F32], epi=lambda acc, r: (acc + r,), epi_args=(x,), name=f'o_proj_{tag}')
    (hf,) = _ew(lambda a, g: (_rms(a, g),), [x1], [sp['g_ffn']], [(D_MODEL, BF16)], name=f'rms_ffn_{tag}')
    up = _mm(hf, w['w_up'], 'nn', [F32], name=f'up_proj_{tag}')
    (act,) = _cb(lambda ug, uv, wg, wv: (_ffn_act(_conv(ug, wg), _conv(uv, wv)),), [(up, 0), (up, D_FF)],
                 [(w['ffn_conv_w'], 0), (w['ffn_conv_w'], D_FF)], [BF16], [], tc=256, nblk=D_FF // 256, name=f'ffn_act_{tag}')
    x2 = _mm(act, w['w_down'], 'nn', [F32], epi=lambda acc, r: (acc + r,), epi_args=(x1,), name=f'down_proj_{tag}')
    (hp,) = _ew(lambda a, g: (_rms(a, g),), [x2], [sp['g_ple']], [(D_MODEL, BF16)], name=f'rms_ple_{tag}')
    gp = _mm(hp, w['w_ple_gate'], 'nn', [F32], name=f'ple_gate_{tag}')
    x3 = _mm(p_i, w['w_ple'], 'nn', [F32], epi=lambda acc, g, r: (r + _sigmoid(g) * acc,), epi_args=(gp, x2), name=f'ple_{tag}')
    sv.update(hn=hn, proj=proj, aux=aux, cfc=cfc, cfr=cfr, lse=lse, ys=ys, gb=gb, o_dn=o_dn,
              states=states, merged=merged, x1=x1, hf=hf, up=up, act=act, x2=x2, hp=hp, gp=gp, p=p_i,
              bias_row=bias_row, alog_row=alog_row)
    return x3, sv


def _rms_bwd(x, g, dh, dres, name):
    def fn(xv, dhv, dr, gv):
        _, vjp = jax.vjp(_rms, xv, gv)
        dx, dg = vjp(dhv)
        return dr + dx, dg
    return _ew(fn, [x, dh, dres], [g], [(D_MODEL, F32)], [(1, D_MODEL)], name=name)


def _layer_bwd(dx3, sv, w, sp, tag):
    t = dx3.shape[0]
    g = {}
    def ple_epi(acc, gpv, d):
        s = _sigmoid(gpv)
        return d * acc * s * (1.0 - s), d * s
    dgp, de = _mm(sv['p'], w['w_ple'], 'nn', [BF16, BF16], epi=ple_epi, epi_args=(sv['gp'], dx3), name=f'ple_bwd_{tag}')
    g['w_ple'] = _mm(sv['p'], de, 'tn', [F32], name=f'd_w_ple_{tag}')
    g['w_ple_gate'] = _mm(sv['hp'], dgp, 'tn', [F32], name=f'd_w_ple_gate_{tag}')
    dhp = _mm(dgp, w['w_ple_gate'], 'nt', [F32], name=f'd_hp_{tag}')
    dx2, g['g_ple'] = _rms_bwd(sv['x2'], sp['g_ple'], dhp, dx3, f'rms_ple_bwd_{tag}')
    dact = _mm(dx2, w['w_down'], 'nt', [F32], tn=256, name=f'd_act_{tag}')
    g['w_down'] = _mm(sv['act'], dx2, 'tn', [F32], tm=256, name=f'd_w_down_{tag}')

    def ffn_bwd_tile(ug, uv, da, wg, wv):
        cg, cv = _conv(ug, wg), _conv(uv, wv)
        _, vjp = jax.vjp(_ffn_act, cg, cv)
        dcg, dcv = vjp(da)
        dug, dwg = _conv_bwd(ug, wg, dcg)
        duv, dwv = _conv_bwd(uv, wv, dcv)
        return dug, duv, jnp.concatenate(dwg, axis=0), jnp.concatenate(dwv, axis=0)
    dupg, dupv, dwg, dwv = _cb(ffn_bwd_tile, [(sv['up'], 0), (sv['up'], D_FF), (dact, 0)],
                               [(w['ffn_conv_w'], 0), (w['ffn_conv_w'], D_FF)], [BF16, BF16], [3, 3], tc=256,
                               nblk=D_FF // 256, name=f'ffn_act_bwd_{tag}')
    dup = jnp.concatenate([dupg, dupv], axis=1)
    g['ffn_conv_w'] = jnp.concatenate([dwg, dwv], axis=1)
    g['w_up'] = _mm(sv['hf'], dup, 'tn', [F32], name=f'd_w_up_{tag}')
    dhf = _mm(dup, w['w_up'], 'nt', [F32], name=f'd_hf_{tag}')
    dx1, g['g_ffn'] = _rms_bwd(sv['x1'], sp['g_ffn'], dhf, dx2, f'rms_ffn_bwd_{tag}')
    dmerged = _mm(dx1, w['w_o'], 'nt', [F32], name=f'd_merged_{tag}')
    g['w_o'] = _mm(sv['merged'], dx1, 'tn', [F32], name=f'd_w_o_{tag}')
    dgate, dtn = _merge_bwd(sv['ys'], w['w_branch'], sv['proj'], dmerged, f'merge_bwd_{tag}')
    dys, dwb = [], []
    for n in range(3):
        dys.append(_mm(dtn[n], w['w_branch'][n], 'nt', [F32], name=f'd_y{n}_{tag}'))
        dwb.append(_mm(sv['ys'][n], dtn[n], 'tn', [F32], name=f'd_w_branch{n}_{tag}'))
    g['w_branch'] = jnp.stack(dwb)
    ddq, ddk, ddv, ddz, dwq, dwk, dwv_, dgb, dng = _dn_bwd(sv['proj'], w['dn_conv_w'], sv['gb'], sp['dn_norm_gain'],
                                                           sv['o_dn'], sv['states'], dys[2], f'dn_bwd_{tag}')
    g['dn_conv_w'] = jnp.concatenate([dwq, dwk, dwv_], axis=1)
    g['dn_norm_gain'] = jnp.sum(dng, axis=0)
    def sc_bwd_tile(sb, sc, svv, dy, wv):
        u = sc * svv
        dsb = dy * _conv(u, wv)
        du, dws = _conv_bwd(u, wv, dy * sb)
        return dsb, du * svv, du * sc, jnp.concatenate(dws, axis=0)
    dsb, dsc, dsv, g['sc_conv_w'] = _cb(sc_bwd_tile, [(sv['proj'], C_SB), (sv['proj'], C_SC), (sv['proj'], C_SV), (dys[1], 0)],
                                        [(w['sc_conv_w'], 0)], [BF16, BF16, BF16], [3], tc=256, nblk=2, name=f'sc_bwd_{tag}')
    dfq, dfk, dfv, dcfc, dcfr, dqg, dkg = _att_bwd(sv['proj'], sp['fox_q_gain'], sp['fox_k_gain'], sv['cfc'], sv['cfr'],
                                                   sv['lse'], sv['ys'][0], dys[0], f'fox_bwd_{tag}')
    g['fox_q_gain'] = jnp.sum(dqg, axis=0)
    g['fox_k_gain'] = jnp.sum(dkg, axis=0)
    dcf = (dcfc[:, :, 0] + dcfr[:, 0, :]).T
    daux = jnp.concatenate([dcf, dgb[:, :, 1].T, dgb[:, :, 0].T, jnp.zeros((t, 112), F32)], axis=1)
    dsmall, dbias, dalog = _prep_bwd(sv['proj'], sv['bias_row'], sv['alog_row'], daux, f'prep_bwd_{tag}')
    g['b_fox_f'] = dbias[:, 0:8]
    g['dn_dt_bias'] = dbias[:, 12:16]
    g['dn_a_log'] = dalog[:, 12:16]
    dproj = jnp.concatenate([dfq, dfk, dfv, dsb, dsc, dsv, ddq, ddk, ddv, ddz, dgate[0], dgate[1], dgate[2], dsmall], axis=1)
    g['w_in'] = _mm(sv['hn'], dproj, 'tn', [F32], tn=640, name=f'd_w_in_{tag}')
    dhn = _mm(dproj, w['w_in'], 'nt', [F32], name=f'd_hn_{tag}')
    dx, g['g_mix'] = _rms_bwd(sv['x'], sp['g_mix'], dhn, dx1, f'rms_mix_bwd_{tag}')
    return dx, g


def _loss_bwd(y, target, name):
    inv = 1.0 / y.shape[1]

    def fn(yv, tv):
        err = yv - tv
        return err * inv, jnp.zeros((8, 128), F32) + 0.5 * inv * jnp.sum(err * err)
    return _ew(fn, [y, target], [], [(y.shape[1], F32)], [(8, 128)], name=name)


PACK_W = 1024
FULL_SHAPE = {'w_in': (D_MODEL, IN_ORIG), 'w_branch': (3, BW, D_MODEL), 'w_o': (D_MODEL, D_MODEL), 'w_up': (D_MODEL, 2 * D_FF),
              'w_down': (D_FF, D_MODEL), 'w_ple_gate': (D_MODEL, D_MODEL), 'w_ple': (PLE_DIM, D_MODEL),
              'sc_conv_w': (3, BW), 'dn_conv_w': (4, 3 * BW), 'ffn_conv_w': (3, 2 * D_FF)}
SMALL_SHAPE = {'g_mix': D_MODEL, 'b_fox_f': FOX_HEADS, 'fox_q_gain': FOX_DH, 'fox_k_gain': FOX_DH, 'dn_a_log': DN_HEADS,
               'dn_dt_bias': DN_HEADS, 'dn_norm_gain': DN_DH, 'g_ffn': D_MODEL, 'g_ple': D_MODEL}


def _shard_shape(name):
    s = list(FULL_SHAPE[name])
    s[SHARDED[name]] //= N_DEV
    return tuple(s)


def _full_from_gathered(g, name, layer):
    sh, ax = _shard_shape(name), SHARDED[name]
    blocks = jnp.moveaxis(g[:, layer], 0, ax)
    return blocks.reshape(sh[:ax] + (N_DEV * sh[ax],) + sh[ax + 1:])


def _by_dest(full, name):
    sh, ax = _shard_shape(name), SHARDED[name]
    return jnp.moveaxis(full.reshape(sh[:ax] + (N_DEV, sh[ax]) + sh[ax + 1:]), ax, 0)


def _flat_pack(arrs):
    flat = jnp.concatenate([a.reshape(-1).astype(F32) for a in arrs])
    rows = -(-flat.shape[0] // (8 * PACK_W)) * 8
    return jnp.pad(flat, (0, rows * PACK_W - flat.shape[0])).reshape(rows, PACK_W)


def _flat_unpack(pack, shapes):
    flat, out, off = pack.reshape(-1), [], 0
    for s in shapes:
        n = 1
        for d in s:
            n *= d
        out.append(flat[off:off + n].reshape(s))
        off += n
    return out


SMALL_SHAPES = [(DEPTH, SMALL_SHAPE[n]) for n in SMALL]
SMALL_LOSS_AT = sum(DEPTH * SMALL_SHAPE[n] for n in SMALL)
CONV_SHARD_SHAPES = [(DEPTH,) + _shard_shape(n) for n in CONVW]
CONV_FULL_SHAPES = [(DEPTH,) + FULL_SHAPE[n] for n in CONVW]


def _rows2d(a):
    return a.reshape(-1, a.shape[-1])


def _pick_tm(m, width):
    best = None
    for tm in range(16, m + 1, 16):
        if m % tm == 0 and tm * width * 4 <= (1 << 20):
            best = tm
    return best if best is not None else m


HBM_SPEC = pl.BlockSpec(memory_space=pltpu.HBM)


def _place():
    x, y, c = lax.axis_index("x"), lax.axis_index("y"), lax.axis_index("c")
    return x, y, c, [(1 - x, y), (x, 1 - y), (1 - x, 1 - y)]


def _all_gather(arrs, name):
    n = len(arrs)

    def body(*refs):
        ins, outs = refs[:n], refs[n:2 * n]
        send_sems, recv_sems, local_sems = refs[2 * n:]
        x, y, c, chips = _place()
        me, sibling = (x, y, c), (x, y, 1 - c)

        def block(a, p):
            return outs[a].at[4 * p[0] + 2 * p[1] + p[2]]

        def copy(a, k, blk, to, src=None):
            return pltpu.make_async_remote_copy(src_ref=block(a, blk) if src is None else src, dst_ref=block(a, blk),
                                                send_sem=send_sems.at[a, k], recv_sem=recv_sems.at[a, k],
                                                device_id=to, device_id_type=MESH)

        mine = [pltpu.make_async_copy(ins[a], block(a, me), local_sems.at[a]) for a in range(n)]
        first, passed = [], []
        for a in range(n):
            mine[a].start()
            first.append(copy(a, 0, me, sibling, src=ins[a]))
            first += [copy(a, 1 + j, me, (*chip, c), src=ins[a]) for j, chip in enumerate(chips)]
        for cp in first:
            cp.start()
        for j, chip in enumerate(chips):
            for a in range(n):
                copy(a, 1 + j, (*chip, c), me).wait_recv()
                fwd = copy(a, 4 + j, (*chip, c), sibling)
                fwd.start()
                passed.append(fwd)
        for a in range(n):
            copy(a, 0, sibling, me).wait_recv()
            for j, chip in enumerate(chips):
                copy(a, 4 + j, (*chip, 1 - c), me).wait_recv()
        for cp in first + passed:
            cp.wait_send()
        for a in range(n):
            mine[a].wait()

    return pl.pallas_call(
        body, in_specs=[HBM_SPEC] * n, out_specs=[HBM_SPEC] * n,
        out_shape=[jax.ShapeDtypeStruct((N_DEV,) + a.shape, a.dtype) for a in arrs],
        scratch_shapes=[pltpu.SemaphoreType.DMA((n, 7)), pltpu.SemaphoreType.DMA((n, 7)), pltpu.SemaphoreType.DMA((n,))],
        name=name,
    )(*arrs)


def _sibling_swap(arrs, name):
    n = len(arrs)

    def body(*refs):
        send_sems, recv_sems = refs[2 * n:]
        x, y, c, _ = _place()
        cps = [pltpu.make_async_remote_copy(src_ref=refs[a], dst_ref=refs[n + a], send_sem=send_sems.at[a], recv_sem=recv_sems.at[a],
                                            device_id=(x, y, 1 - c), device_id_type=MESH) for a in range(n)]
        for cp in cps:
            cp.start()
        for cp in cps:
            cp.wait()

    return pl.pallas_call(body, in_specs=[HBM_SPEC] * n, out_specs=[HBM_SPEC] * n,
                          out_shape=[jax.ShapeDtypeStruct(a.shape, a.dtype) for a in arrs],
                          scratch_shapes=[pltpu.SemaphoreType.DMA((n,)), pltpu.SemaphoreType.DMA((n,))], name=name)(*arrs)


def _chip_exchange(arrs, name):
    n = len(arrs)

    def body(*refs):
        send_sems, recv_sems = refs[2 * n:]
        x, y, c, chips = _place()
        cps = [pltpu.make_async_remote_copy(src_ref=refs[a].at[2 * cx + cy], dst_ref=refs[n + a].at[j], send_sem=send_sems.at[a, j],
                                            recv_sem=recv_sems.at[a, j], device_id=(cx, cy, c), device_id_type=MESH)
               for a in range(n) for j, (cx, cy) in enumerate(chips)]
        for cp in cps:
            cp.start()
        for cp in cps:
            cp.wait()

    return pl.pallas_call(body, in_specs=[HBM_SPEC] * n, out_specs=[HBM_SPEC] * n,
                          out_shape=[jax.ShapeDtypeStruct((3,) + a.shape[1:], a.dtype) for a in arrs],
                          scratch_shapes=[pltpu.SemaphoreType.DMA((n, 3)), pltpu.SemaphoreType.DMA((n, 3))], name=name)(*arrs)


def _adamw(w, g, m, v):
    m = ADAM_B1 * m + (1.0 - ADAM_B1) * g
    v = ADAM_B2 * v + (1.0 - ADAM_B2) * jnp.square(g)
    m_hat = m / (1.0 - ADAM_B1 ** ADAM_STEP)
    v_hat = v / (1.0 - ADAM_B2 ** ADAM_STEP)
    delta = -ADAM_LR * (m_hat / (jnp.sqrt(v_hat) + ADAM_EPS) + ADAM_WD * w)
    return delta, m, v


def _adamw_call(w, m, v, gparts, name):
    def fn(wv, mv, vv, *gs):
        g = gs[0].astype(F32)
        for gp in gs[1:]:
            g = g + gp.astype(F32)
        delta, m2, v2 = _adamw(wv, g, mv, vv)
        return g, delta, m2, v2
    rows, c = w.shape
    return _ew(fn, [w, m, v] + list(gparts), [], [(c, F32)] * 4, tm=_pick_tm(rows, c), name=name)


def kernel(x, p, g_mix, w_in, b_fox_f, fox_q_gain, fox_k_gain, sc_conv_w, dn_conv_w, dn_a_log, dn_dt_bias,
           dn_norm_gain, w_branch, w_o, g_ffn, w_up, ffn_conv_w, w_down, g_ple, w_ple_gate, w_ple, loss_target,
           m_g_mix, m_w_in, m_b_fox_f, m_fox_q_gain, m_fox_k_gain, m_sc_conv_w, m_dn_conv_w, m_dn_a_log,
           m_dn_dt_bias, m_dn_norm_gain, m_w_branch, m_w_o, m_g_ffn, m_w_up, m_ffn_conv_w, m_w_down, m_g_ple,
           m_w_ple_gate, m_w_ple, v_g_mix, v_w_in, v_b_fox_f, v_fox_q_gain, v_fox_k_gain, v_sc_conv_w, v_dn_conv_w,
           v_dn_a_log, v_dn_dt_bias, v_dn_norm_gain, v_w_branch, v_w_o, v_g_ffn, v_w_up, v_ffn_conv_w, v_w_down,
           v_g_ple, v_w_ple_gate, v_w_ple):
    return _step(x, p, g_mix, w_in, b_fox_f, fox_q_gain, fox_k_gain, sc_conv_w, dn_conv_w, dn_a_log, dn_dt_bias,
                 dn_norm_gain, w_branch, w_o, g_ffn, w_up, ffn_conv_w, w_down, g_ple, w_ple_gate, w_ple, loss_target,
                 m_g_mix, m_w_in, m_b_fox_f, m_fox_q_gain, m_fox_k_gain, m_sc_conv_w, m_dn_conv_w, m_dn_a_log,
                 m_dn_dt_bias, m_dn_norm_gain, m_w_branch, m_w_o, m_g_ffn, m_w_up, m_ffn_conv_w, m_w_down, m_g_ple,
                 m_w_ple_gate, m_w_ple, v_g_mix, v_w_in, v_b_fox_f, v_fox_q_gain, v_fox_k_gain, v_sc_conv_w,
                 v_dn_conv_w, v_dn_a_log, v_dn_dt_bias, v_dn_norm_gain, v_w_branch, v_w_o, v_g_ffn, v_w_up,
                 v_ffn_conv_w, v_w_down, v_g_ple, v_w_ple_gate, v_w_ple)


def _step(*args):
    names = ['x', 'p'] + WEIGHTS + ['loss_target'] + ['m_' + n for n in WEIGHTS] + ['v_' + n for n in WEIGHTS]
    assert len(args) == len(names)
    a = dict(zip(names, args))
    x, target = a['x'][0], a['loss_target'][0]
    p = a['p'][:, 0]
    _, _, c, _ = _place()
    chip = 2 * lax.axis_index("x") + lax.axis_index("y")

    dev = 4 * lax.axis_index("x") + 2 * lax.axis_index("y") + c

    gathered = _all_gather([a[n].astype(BF16) for n in BIG] + [_flat_pack([a[n] for n in CONVW])], 'gather_weights')
    conv_by_dev = [_flat_unpack(gathered[-1][d], CONV_SHARD_SHAPES) for d in range(N_DEV)]
    conv_full = {n: jnp.concatenate([conv_by_dev[d][i] for d in range(N_DEV)], axis=2) for i, n in enumerate(CONVW)}
    ws, sps = [], []
    for layer in range(DEPTH):
        w = {n: _full_from_gathered(g, n, layer) for n, g in zip(BIG, gathered)}
        w.update({n: conv_full[n][layer] for n in CONVW})
        w['w_in'] = _perm_w_in(w['w_in'])
        ws.append(w)
        sps.append({n: a[n][layer][None, :] for n in SMALL})

    h, saved = x, []
    for layer in range(DEPTH):
        h, sv = _layer_fwd(h, p[layer], ws[layer], sps[layer], f'l{layer}')
        saved.append(sv)
    dh, loss_part = _loss_bwd(h, target, 'loss')
    grads = [None] * DEPTH
    for layer in reversed(range(DEPTH)):
        dh, g = _layer_bwd(dh, saved[layer], ws[layer], sps[layer], f'l{layer}')
        g['w_in'] = _unperm_w_in(g['w_in'])
        grads[layer] = g
    grad_x = dh[None]

    keeps, gives = [], []
    for n in BIG:
        by_dest = jnp.stack([_by_dest(grads[layer][n], n) for layer in range(DEPTH)], axis=1)
        by_dest = by_dest.reshape((4, 2) + by_dest.shape[1:])
        keeps.append(lax.dynamic_index_in_dim(by_dest, c, axis=1, keepdims=False))
        gives.append(lax.dynamic_index_in_dim(by_dest, 1 - c, axis=1, keepdims=False).astype(BF16))
    gots = _sibling_swap(gives, 'grad_sibling_swap')
    parts, parts_b = [], []
    for n, keep, got in zip(BIG, keeps, gots):
        k2, g2 = _rows2d(keep), _rows2d(got)
        part, part_b = _ew(lambda k, r: (k + r.astype(F32),) * 2, [k2, g2], [], [(k2.shape[1], F32), (k2.shape[1], BF16)],
                           tm=_pick_tm(*k2.shape), name=f'grad_chip_sum_{n}')
        parts.append(part.reshape(keep.shape))
        parts_b.append(part_b.reshape(keep.shape))
    others = _chip_exchange(parts_b, 'grad_chip_exchange')
    out = {}
    for n, part, other in zip(BIG, parts, others):
        own = lax.dynamic_index_in_dim(part, chip, axis=0, keepdims=False)
        res = _adamw_call(_rows2d(a[n]), _rows2d(a['m_' + n]), _rows2d(a['v_' + n]),
                          [_rows2d(own)] + [_rows2d(other[j]) for j in range(3)], f'adamw_{n}')
        out[n] = [r.reshape(a[n].shape) for r in res]

    small_pack = _flat_pack([jnp.stack([grads[layer][n].reshape(-1) for layer in range(DEPTH)]) for n in SMALL] + [loss_part[0, 0]])
    conv_pack = _flat_pack([jnp.stack([grads[layer][n] for layer in range(DEPTH)]) for n in CONVW])
    small_all, conv_all = _all_gather([small_pack, conv_pack], 'gather_small_grads')
    res = _adamw_call(_flat_pack([a[n] for n in SMALL]), _flat_pack([a['m_' + n] for n in SMALL]),
                      _flat_pack([a['v_' + n] for n in SMALL]), [small_all[d] for d in range(N_DEV)], 'adamw_replicated')
    loss = res[0].reshape(-1)[SMALL_LOSS_AT]
    for k, r in enumerate(res):
        for n, val in zip(SMALL, _flat_unpack(r, SMALL_SHAPES)):
            out.setdefault(n, [None] * 4)[k] = val
    (conv_sum,) = _ew(lambda *gs: (functools.reduce(lambda s, t: s + t, gs),), [conv_all[d] for d in range(N_DEV)], [],
                      [(PACK_W, F32)], tm=conv_pack.shape[0], name='conv_grad_sum')
    conv_own = [lax.dynamic_slice_in_dim(g, dev * (g.shape[2] // N_DEV), g.shape[2] // N_DEV, axis=2)
                for g in _flat_unpack(conv_sum, CONV_FULL_SHAPES)]
    res = _adamw_call(_flat_pack([a[n] for n in CONVW]), _flat_pack([a['m_' + n] for n in CONVW]),
                      _flat_pack([a['v_' + n] for n in CONVW]), [_flat_pack(conv_own)], 'adamw_conv')
    for k, r in enumerate(res):
        for n, val in zip(CONVW, _flat_unpack(r, CONV_SHARD_SHAPES)):
            out.setdefault(n, [None] * 4)[k] = val

    outs = [loss, grad_x]
    for k in range(4):
        outs += [out[n][k] for n in WEIGHTS]
    return tuple(outs)
```

```python
import functools

import jax
import jax.numpy as jnp
from jax import lax
from jax.experimental import pallas as pl
from jax.experimental.pallas import tpu as pltpu

F32 = jnp.float32
BF16 = jnp.bfloat16
HI = lax.Precision.HIGHEST

D_MODEL = 1024
DEPTH = 2
N_DEV = 8
PLE_DIM = 256
BW = 512
FOX_HEADS, FOX_DH = 8, 64
DN_HEADS, DN_DH = 4, 128
DN_CHUNK = 64
D_FF = 2816
EPS = 1e-6
NEG = -1e30

ADAM_LR, ADAM_B1, ADAM_B2, ADAM_EPS, ADAM_WD, ADAM_STEP = 0.001, 0.9, 0.999, 1e-08, 0.01, 10

C_FQ, C_FK, C_FV = 0, 512, 1024
C_SB, C_SC, C_SV = 1536, 2048, 2560
C_DQ, C_DK, C_DV, C_DZ = 3072, 3584, 4096, 4608
C_GATE = 5120
C_SMALL = 8192
IN_P = 8320
IN_ORIG = 8208

WEIGHTS = ['g_mix', 'w_in', 'b_fox_f', 'fox_q_gain', 'fox_k_gain', 'sc_conv_w', 'dn_conv_w', 'dn_a_log',
           'dn_dt_bias', 'dn_norm_gain', 'w_branch', 'w_o', 'g_ffn', 'w_up', 'ffn_conv_w', 'w_down', 'g_ple',
           'w_ple_gate', 'w_ple']
BIG = {'w_in': 1, 'w_branch': 2, 'w_o': 0, 'w_up': 1, 'w_down': 0, 'w_ple_gate': 0, 'w_ple': 1}
CONVW = {'sc_conv_w': 1, 'dn_conv_w': 1, 'ffn_conv_w': 1}
SHARDED = {**BIG, **CONVW}
SMALL = [n for n in WEIGHTS if n not in SHARDED]
MESH = pl.DeviceIdType.MESH


def _sigmoid(x):
    return 0.5 * (jnp.tanh(0.5 * x) + 1.0)


def _silu(x):
    return x * _sigmoid(x)


def _log1pexp_negabs(z):
    return jnp.log(1.0 + jnp.exp(-jnp.abs(z)))


def _log_sigmoid(z):
    return jnp.minimum(z, 0.0) - _log1pexp_negabs(z)


def _softplus(z):
    return jnp.maximum(z, 0.0) + _log1pexp_negabs(z)


def _rms(x, g):
    return x * lax.rsqrt(jnp.mean(x * x, axis=-1, keepdims=True) + EPS) * g


def _l2(x):
    return x * lax.rsqrt(jnp.sum(x * x, axis=-1, keepdims=True) + EPS)


def _dot(a, b, dims, precision=None):
    return lax.dot_general(a, b, (dims, ((), ())), preferred_element_type=F32, precision=precision)


NN = ((1,), (0,))
NT = ((1,), (1,))
TN = ((0,), (0,))


def _shift_down(x, s):
    if s == 0:
        return x
    t = lax.broadcasted_iota(jnp.int32, x.shape, 0)
    return jnp.where(t >= s, pltpu.roll(x, s, 0), 0.0)


def _shift_up(x, s):
    if s == 0:
        return x
    n = x.shape[0]
    t = lax.broadcasted_iota(jnp.int32, x.shape, 0)
    return jnp.where(t < n - s, pltpu.roll(x, n - s, 0), 0.0)


def _conv(x, w):
    k = w.shape[0]
    y = w[k - 1:k] * x
    for j in range(k - 1):
        y = y + w[j:j + 1] * _shift_down(x, k - 1 - j)
    return y


def _conv_bwd(x, w, dy):
    k = w.shape[0]
    dx = w[k - 1:k] * dy
    dws = []
    for j in range(k - 1):
        dx = dx + w[j:j + 1] * _shift_up(dy, k - 1 - j)
        dws.append(jnp.sum(dy * _shift_down(x, k - 1 - j), axis=0, keepdims=True))
    dws.append(jnp.sum(dy * x, axis=0, keepdims=True))
    return dx, dws


MM_VMEM_BUDGET = 36 << 20
MM_STEP_BYTES = 1 << 20


def _mm_tiles(m, n, k, a_size, b_size, tile_size, cast_a):
    best = None
    for tm in [d for d in (2048, 1024, 512, 256, 128) if d <= m and m % d == 0] or [m]:
        for tn in [d for d in range(128, min(n, 2048) + 1, 128) if n % d == 0] or [n]:
            vmem = 2 * tm * k * a_size + (2 * tm * k if cast_a else 0) + 2 * k * tn * b_size + 2 * tm * tn * tile_size
            if vmem > MM_VMEM_BUDGET:
                continue
            steps = (m // tm) * (n // tn)
            cost = m * k * a_size + (m // tm) * k * n * b_size + m * n * tile_size + steps * MM_STEP_BYTES
            if best is None or cost < best[0]:
                best = (cost, tm, tn)
    assert best is not None, (m, n, k)
    return best[1], best[2]


def _mm(a, b, mode, outs, *, epi=None, epi_args=(), name):
    if mode == 'nn':
        (m, k), (k2, n) = a.shape, b.shape
    elif mode == 'nt':
        (m, k), (n, k2) = a.shape, b.shape
    else:
        (k, m), (k2, n) = a.shape, b.shape
    assert k == k2, (a.shape, b.shape, mode)
    tile_size = sum(jnp.dtype(dt).itemsize for dt in outs) + sum(e.dtype.itemsize for e in epi_args if e.shape[0] != 1)
    tm, tn = _mm_tiles(m, n, k, a.dtype.itemsize, b.dtype.itemsize, tile_size, a.dtype != BF16)
    dims = {'nn': NN, 'nt': NT, 'tn': TN}[mode]
    a_spec = pl.BlockSpec((k, tm), lambda i, j: (0, i)) if mode == 'tn' else pl.BlockSpec((tm, k), lambda i, j: (i, 0))
    b_spec = pl.BlockSpec((tn, k), lambda i, j: (j, 0)) if mode == 'nt' else pl.BlockSpec((k, tn), lambda i, j: (0, j))
    e_specs = [pl.BlockSpec((1, tn), lambda i, j: (0, j)) if e.shape[0] == 1 else pl.BlockSpec((tm, tn), lambda i, j: (i, j))
               for e in epi_args]
    ne, no = len(epi_args), len(outs)
    cast_a = a.dtype != BF16

    def body(a_ref, b_ref, *rest):
        if cast_a:
            a_sc = rest[-1]

            @pl.when(pl.program_id(1) == 0)
            def _():
                a_sc[...] = a_ref[...].astype(BF16)
            av = a_sc[...]
        else:
            av = a_ref[...]
        acc = _dot(av, b_ref[...].astype(BF16), dims)
        vals = epi(acc, *[e[...] for e in rest[:ne]]) if epi is not None else (acc,)
        for o_ref, v in zip(rest[ne:ne + no], vals):
            o_ref[...] = v.astype(o_ref.dtype)

    res = pl.pallas_call(
        body, grid=(m // tm, n // tn),
        in_specs=[a_spec, b_spec] + e_specs,
        out_specs=[pl.BlockSpec((tm, tn), lambda i, j: (i, j)) for _ in outs],
        out_shape=[jax.ShapeDtypeStruct((m, n), dt) for dt in outs],
        scratch_shapes=[pltpu.VMEM(a_spec.block_shape, BF16)] if cast_a else [],
        name=name,
    )(a, b, *epi_args)
    return res[0] if no == 1 else res


def _ew(fn, tiled, bcast, outs, reds=(), *, tm=256, name):
    secs = [(t, 0, t.shape[1]) if not isinstance(t, tuple) else t for t in tiled]
    m = secs[0][0].shape[0]
    tm = min(tm, m)
    assert m % tm == 0
    in_specs = []
    for arr, off, w in secs:
        assert off % w == 0
        in_specs.append(pl.BlockSpec((tm, w), functools.partial(lambda i, c: (i, c), c=off // w)))
    in_specs += [pl.BlockSpec(b.shape, lambda i: (0, 0)) for b in bcast]
    nin, no = len(in_specs), len(outs)

    def body(*refs):
        vals = fn(*[r[...] for r in refs[:nin]])
        for r, v in zip(refs[nin:nin + no], vals[:no]):
            r[...] = v.astype(r.dtype)
        i = pl.program_id(0)
        for r, v in zip(refs[nin + no:], vals[no:]):
            @pl.when(i == 0)
            def _():
                r[...] = v

            @pl.when(i > 0)
            def _():
                r[...] += v

    res = pl.pallas_call(
        body, grid=(m // tm,), in_specs=in_specs,
        out_specs=[pl.BlockSpec((tm, c), lambda i: (i, 0)) for c, _ in outs] + [pl.BlockSpec(s, lambda i: (0, 0)) for s in reds],
        out_shape=[jax.ShapeDtypeStruct((m, c), dt) for c, dt in outs] + [jax.ShapeDtypeStruct(s, F32) for s in reds],
        name=name,
    )(*[s[0] for s in secs], *bcast)
    return res


def _cb(fn, cols, params, outs, pouts, *, tc, nblk, name):
    t = cols[0][0].shape[0]
    in_specs = []
    for arr, off in cols:
        assert off % tc == 0
        in_specs.append(pl.BlockSpec((t, tc), functools.partial(lambda c, o: (0, o + c), o=off // tc)))
    for arr, off in params:
        in_specs.append(pl.BlockSpec((arr.shape[0], tc), functools.partial(lambda c, o: (0, o + c), o=off // tc)))
    nin, no = len(in_specs), len(outs)

    def body(*refs):
        vals = fn(*[r[...] for r in refs[:nin]])
        for r, v in zip(refs[nin:], vals):
            r[...] = v.astype(r.dtype)

    return pl.pallas_call(
        body, grid=(nblk,), in_specs=in_specs,
        out_specs=[pl.BlockSpec((t, tc), lambda c: (0, c)) for _ in outs] + [pl.BlockSpec((k, tc), lambda c: (0, c)) for k in pouts],
        out_shape=[jax.ShapeDtypeStruct((t, nblk * tc), dt) for dt in outs] + [jax.ShapeDtypeStruct((k, nblk * tc), F32) for k in pouts],
        name=name,
    )(*[c[0] for c in cols], *[p[0] for p in params])


CUM_BLK = 256


def _prep_point(s, bias, alog):
    lane = lax.broadcasted_iota(jnp.int32, s.shape, 1)
    z = s + bias
    return jnp.where(lane < 8, _log_sigmoid(z),
                     jnp.where(lane < 12, _sigmoid(s),
                               jnp.where(lane < 16, -jnp.exp(alog) * _softplus(z), 0.0)))


def _tri(n, upper):
    r = lax.broadcasted_iota(jnp.int32, (n, n), 0)
    c = lax.broadcasted_iota(jnp.int32, (n, n), 1)
    return (r <= c if upper else r >= c).astype(F32)


def _prep_fwd(proj, bias_row, alog_row, name):
    t = proj.shape[0]
    nb = t // CUM_BLK

    def body(s_ref, b_ref, a_ref, o_ref):
        pre = _prep_point(s_ref[...], b_ref[...], a_ref[...])
        lane = lax.broadcasted_iota(jnp.int32, (CUM_BLK, 128), 1)
        tri = _tri(CUM_BLK, False)
        carry = jnp.zeros((1, 128), F32)
        for blk in range(nb):
            xb = pre[blk * CUM_BLK:(blk + 1) * CUM_BLK]
            cb = _dot(tri, xb, NN, HI) + carry
            carry = cb[CUM_BLK - 1:CUM_BLK]
            o_ref[blk * CUM_BLK:(blk + 1) * CUM_BLK, :] = jnp.where(lane < 8, cb, xb)

    return pl.pallas_call(
        body, grid=(1,),
        in_specs=[pl.BlockSpec((t, 128), lambda i: (0, C_SMALL // 128)), pl.BlockSpec((1, 128), lambda i: (0, 0)),
                  pl.BlockSpec((1, 128), lambda i: (0, 0))],
        out_specs=pl.BlockSpec((t, 128), lambda i: (0, 0)),
        out_shape=jax.ShapeDtypeStruct((t, 128), F32), name=name,
    )(proj, bias_row, alog_row)


def _prep_bwd(proj, bias_row, alog_row, daux, name):
    t = proj.shape[0]
    nb = t // CUM_BLK

    def body(s_ref, b_ref, a_ref, d_ref, ds_ref, db_ref, da_ref, dpre_sc):
        lane = lax.broadcasted_iota(jnp.int32, (CUM_BLK, 128), 1)
        tri = _tri(CUM_BLK, True)
        carry = jnp.zeros((1, 128), F32)
        for blk in reversed(range(nb)):
            db = d_ref[blk * CUM_BLK:(blk + 1) * CUM_BLK, :]
            cb = _dot(tri, db, NN, HI) + carry
            carry = cb[0:1]
            dpre_sc[blk * CUM_BLK:(blk + 1) * CUM_BLK, :] = jnp.where(lane < 8, cb, db)
        _, vjp = jax.vjp(_prep_point, s_ref[...], b_ref[...], a_ref[...])
        ds, dbias, dalog = vjp(dpre_sc[...])
        ds_ref[...] = ds.astype(ds_ref.dtype)
        db_ref[...] = dbias
        da_ref[...] = dalog

    return pl.pallas_call(
        body, grid=(1,),
        in_specs=[pl.BlockSpec((t, 128), lambda i: (0, C_SMALL // 128)), pl.BlockSpec((1, 128), lambda i: (0, 0)),
                  pl.BlockSpec((1, 128), lambda i: (0, 0)), pl.BlockSpec((t, 128), lambda i: (0, 0))],
        out_specs=[pl.BlockSpec((t, 128), lambda i: (0, 0)), pl.BlockSpec((1, 128), lambda i: (0, 0)),
                   pl.BlockSpec((1, 128), lambda i: (0, 0))],
        out_shape=[jax.ShapeDtypeStruct((t, 128), BF16), jax.ShapeDtypeStruct((1, 128), F32), jax.ShapeDtypeStruct((1, 128), F32)],
        scratch_shapes=[pltpu.VMEM((t, 128), F32)], name=name,
    )(proj, bias_row, alog_row, daux)


ATT_TQ = 256
FOX_SCALE = FOX_DH ** -0.5


def _qnorm(q, g):
    return _rms(q, g) * FOX_SCALE


def _att_scores(qn_blk, kn, cfc_blk, cfr, qi, tq, kend):
    s = _dot(qn_blk.astype(BF16), kn[:kend].astype(BF16), NT) + cfc_blk - cfr[:, :kend]
    row = lax.broadcasted_iota(jnp.int32, (tq, kend), 0) + qi * tq
    col = lax.broadcasted_iota(jnp.int32, (tq, kend), 1)
    return s, row >= col


ATT_PAIR = 128 // FOX_DH


def _att_specs(t):
    pair = lambda off: pl.BlockSpec((t, 128), functools.partial(lambda i, o: (0, o + i), o=off // 128))
    gain = pl.BlockSpec((1, FOX_DH), lambda i: (0, 0))
    hd = lambda i: (i, 0, 0)
    col, row = pl.BlockSpec((ATT_PAIR, t, 1), hd), pl.BlockSpec((ATT_PAIR, 1, t), hd)
    return [pair(C_FQ), pair(C_FK), pair(C_FV), gain, gain, col, row], pl.BlockSpec((t, 128), lambda i: (0, i)), col, row


def _att_fwd(proj, qg, kg, cfc, cfr, name):
    t = proj.shape[0]
    tq = min(ATT_TQ, t)
    in_specs, pair_out, col, _ = _att_specs(t)

    def body(q_ref, k_ref, v_ref, qg_ref, kg_ref, cfc_ref, cfr_ref, o_ref, lse_ref):
        for e in range(ATT_PAIR):
            lanes = slice(e * FOX_DH, (e + 1) * FOX_DH)
            qn = _qnorm(q_ref[:, lanes], qg_ref[...])
            kn = _rms(k_ref[:, lanes], kg_ref[...])
            v = v_ref[:, lanes].astype(BF16)
            cfr = cfr_ref[e]
            for qi in range(t // tq):
                kend = (qi + 1) * tq
                rows = slice(qi * tq, kend)
                s, mask = _att_scores(qn[rows], kn, cfc_ref[e, rows, :], cfr, qi, tq, kend)
                s = jnp.where(mask, s, NEG)
                m = jnp.max(s, axis=1, keepdims=True)
                p = jnp.exp(s - m)
                l = jnp.sum(p, axis=1, keepdims=True)
                o_ref[rows, lanes] = _dot(p.astype(BF16), v[:kend], NN) / l
                lse_ref[e, rows, :] = m + jnp.log(l)

    return pl.pallas_call(
        body, grid=(FOX_HEADS // ATT_PAIR,), in_specs=in_specs, out_specs=[pair_out, col],
        out_shape=[jax.ShapeDtypeStruct((t, BW), F32), jax.ShapeDtypeStruct((FOX_HEADS, t, 1), F32)], name=name,
    )(proj, proj, proj, qg, kg, cfc, cfr)


def _att_bwd(proj, qg, kg, cfc, cfr, lse, o, do, name):
    t = proj.shape[0]
    tq = min(ATT_TQ, t)
    in_specs, pair_out, col, row = _att_specs(t)

    def body(q_ref, k_ref, v_ref, qg_ref, kg_ref, cfc_ref, cfr_ref, lse_ref, o_ref, do_ref,
             dq_ref, dk_ref, dv_ref, dcfc_ref, dcfr_ref, dqg_ref, dkg_ref, dqn_sc, dkn_sc, dv_sc, dcfr_sc):
        for e in range(ATT_PAIR):
            lanes = slice(e * FOX_DH, (e + 1) * FOX_DH)
            qn, vjp_q = jax.vjp(_qnorm, q_ref[:, lanes], qg_ref[...])
            kn, vjp_k = jax.vjp(_rms, k_ref[:, lanes], kg_ref[...])
            v = v_ref[:, lanes].astype(BF16)
            cfr = cfr_ref[e]
            do_e = do_ref[:, lanes]
            delta = jnp.sum(do_e * o_ref[:, lanes], axis=1, keepdims=True)
            dkn_sc[...] = jnp.zeros_like(dkn_sc)
            dv_sc[...] = jnp.zeros_like(dv_sc)
            dcfr_sc[...] = jnp.zeros_like(dcfr_sc)
            for qi in range(t // tq):
                kend = (qi + 1) * tq
                rows = slice(qi * tq, kend)
                s, mask = _att_scores(qn[rows], kn, cfc_ref[e, rows, :], cfr, qi, tq, kend)
                p = jnp.where(mask, jnp.exp(jnp.where(mask, s, NEG) - lse_ref[e, rows, :]), 0.0)
                do_b = do_e[rows].astype(BF16)
                dv_sc[0:kend, :] += _dot(p.astype(BF16), do_b, TN)
                dp = _dot(do_b, v[:kend], NT)
                ds = p * (dp - delta[rows])
                ds_b = ds.astype(BF16)
                dqn_sc[rows, :] = _dot(ds_b, kn[:kend].astype(BF16), NN)
                dkn_sc[0:kend, :] += _dot(ds_b, qn[rows].astype(BF16), TN)
                dcfc_ref[e, rows, :] = jnp.sum(ds, axis=1, keepdims=True)
                dcfr_sc[:, 0:kend] -= jnp.sum(ds, axis=0, keepdims=True)
            dq, dqg = vjp_q(dqn_sc[...])
            dk, dkg = vjp_k(dkn_sc[...])
            dq_ref[:, lanes] = dq.astype(dq_ref.dtype)
            dk_ref[:, lanes] = dk.astype(dk_ref.dtype)
            dv_ref[:, lanes] = dv_sc[...].astype(dv_ref.dtype)
            dcfr_ref[e] = dcfr_sc[...]
            dqg_ref[e] = dqg
            dkg_ref[e] = dkg

    gsp = pl.BlockSpec((ATT_PAIR, 1, FOX_DH), lambda i: (i, 0, 0))
    return pl.pallas_call(
        body, grid=(FOX_HEADS // ATT_PAIR,),
        in_specs=in_specs + [col, pair_out, pair_out],
        out_specs=[pair_out] * 3 + [col, row, gsp, gsp],
        out_shape=[jax.ShapeDtypeStruct((t, BW), BF16)] * 3
        + [jax.ShapeDtypeStruct((FOX_HEADS, t, 1), F32), jax.ShapeDtypeStruct((FOX_HEADS, 1, t), F32)]
        + [jax.ShapeDtypeStruct((FOX_HEADS, 1, FOX_DH), F32)] * 2,
        scratch_shapes=[pltpu.VMEM((t, FOX_DH), F32)] * 3 + [pltpu.VMEM((1, t), F32)], name=name,
    )(proj, proj, proj, qg, kg, cfc, cfr, lse, o, do)


DN_SCALE = DN_DH ** -0.5


DN_BATCH = 8


def _mm3(a, b, dims, batch=False):
    if batch:
        dn = (((dims[0][0] + 1,), (dims[1][0] + 1,)), ((0,), (0,)))
        dot = lambda p, q: lax.dot_general(p, q, dn, preferred_element_type=F32)
    else:
        dot = lambda p, q: _dot(p, q, dims)
    ah, bh = a.astype(BF16), b.astype(BF16)
    al, bl = (a - ah.astype(F32)).astype(BF16), (b - bh.astype(F32)).astype(BF16)
    return dot(ah, bh) + (dot(ah, bl) + dot(al, bh))


def _dn_local(qc, kc, vc, g, beta):
    nb, c, _ = qc.shape
    ii = lax.broadcasted_iota(jnp.int32, (c, c), 0)
    jj = lax.broadcasted_iota(jnp.int32, (c, c), 1)
    incl, strict = ii >= jj, ii > jj
    lower = jnp.broadcast_to(incl.astype(F32), (nb, c, c))
    eye = (ii == jj).astype(F32)
    mm = functools.partial(_mm3, batch=True)
    dm = mm(lower, jnp.where(strict, g, 0.0), NN)
    decay = jnp.where(incl, jnp.exp(jnp.where(incl, dm, 0.0)), 0.0)
    gcum = mm(lower, g * jnp.ones((1, 1, DN_DH), F32), NN)
    eg = jnp.exp(gcum)
    glast = gcum[:, c - 1:c]
    kb = kc * beta
    n1 = jnp.where(strict, mm(kb, kc, NT) * decay, 0.0)
    inv = eye - n1
    pw = n1
    for _ in range(5):
        pw = mm(pw, pw, NN)
        inv = inv + mm(pw, inv, NN)
    sol = mm(inv, jnp.concatenate([vc * beta, kb * eg], axis=2), NN)
    qk = jnp.where(incl, mm(qc, kc, NT) * decay, 0.0)
    return sol[:, :, :DN_DH], sol[:, :, DN_DH:], qk, qc * eg, kc * jnp.exp(glast - gcum), jnp.exp(glast)


def _dn_state(u, kcum, qk, qdec, kdec, egl, state):
    v_new = u - _mm3(kcum, state, NN)
    out = _mm3(qdec, state, NN) + _mm3(qk, v_new, NN)
    return out, state * egl + _mm3(kdec, v_new, TN)


def _dn_pre_q(c):
    return _l2(_silu(c)) * DN_SCALE


def _dn_pre_k(c):
    return _l2(_silu(c))


def _dn_post(o, z, ng):
    return _rms(o, ng) * _silu(z)


def _dn_specs(t):
    cblk = lambda o: pl.BlockSpec((t, DN_DH), functools.partial(lambda h, o: (0, o + h), o=o // DN_DH))
    wblk = lambda o: pl.BlockSpec((4, DN_DH), functools.partial(lambda h, o: (0, o + h), o=o // DN_DH))
    proj_specs = [cblk(C_DQ), cblk(C_DK), cblk(C_DV), cblk(C_DZ)]
    w_specs = [wblk(0), wblk(BW), wblk(2 * BW)]
    gb_spec = pl.BlockSpec((1, t, 2), lambda h: (h, 0, 0))
    return proj_specs, w_specs, gb_spec


def _chunk_rows(n, count=1):
    return pl.ds(pl.multiple_of(n * DN_CHUNK, DN_CHUNK), count * DN_CHUNK)


def _egl_rows(n, count=1):
    return pl.ds(pl.multiple_of(n * 8, 8), count * 8)


def _dn_local_inputs(n, qn_sc, kn_sc, vv_sc, gb_ref):
    r = _chunk_rows(n, DN_BATCH)
    split = lambda v: v.reshape(DN_BATCH, DN_CHUNK, v.shape[-1])
    gbv = split(gb_ref[0, r, :])
    return split(qn_sc[r, :]), split(kn_sc[r, :]), split(vv_sc[r, :]), gbv[:, :, 0:1], gbv[:, :, 1:2]


def _dn_local_phase(nc, qn_sc, kn_sc, vv_sc, gb_ref, loc):
    def step(i, carry):
        n = i * DN_BATCH
        vals = _dn_local(*_dn_local_inputs(n, qn_sc, kn_sc, vv_sc, gb_ref))
        for sc, val in zip(loc[:5], vals[:5]):
            sc[_chunk_rows(n, DN_BATCH), :] = val.reshape(DN_BATCH * DN_CHUNK, val.shape[-1])
        loc[5][_egl_rows(n, DN_BATCH), :] = jnp.broadcast_to(vals[5], (DN_BATCH, 8, DN_DH)).reshape(DN_BATCH * 8, DN_DH)
        return carry

    lax.fori_loop(0, nc // DN_BATCH, step, 0)


def _dn_loc_scratch(t, nc):
    big = pltpu.VMEM((t, DN_DH), F32)
    return [big, big, pltpu.VMEM((t, DN_CHUNK), F32), big, big, pltpu.VMEM((nc * 8, DN_DH), F32)]


def _dn_fwd(proj, conv_w, gb, ng, name):
    t = proj.shape[0]
    nc = t // DN_CHUNK
    assert nc % DN_BATCH == 0
    proj_specs, w_specs, gb_spec = _dn_specs(t)

    def body(q_ref, k_ref, v_ref, z_ref, wq_ref, wk_ref, wv_ref, gb_ref, ng_ref, y_ref, o_ref, st_ref, qn_sc, kn_sc, vv_sc, *loc):
        qn_sc[...] = _dn_pre_q(_conv(q_ref[...], wq_ref[...]))
        kn_sc[...] = _dn_pre_k(_conv(k_ref[...], wk_ref[...]))
        vv_sc[...] = _silu(_conv(v_ref[...], wv_ref[...]))
        _dn_local_phase(nc, qn_sc, kn_sc, vv_sc, gb_ref, loc)
        u_sc, kcum_sc, qk_sc, qdec_sc, kdec_sc, egl_sc = loc

        def chunk(n, state):
            r = _chunk_rows(n)
            egl = egl_sc[_egl_rows(n), :][0:1]
            out, new_state = _dn_state(u_sc[r, :], kcum_sc[r, :], qk_sc[r, :], qdec_sc[r, :], kdec_sc[r, :], egl, state)
            st_ref[0, n] = state
            o_ref[r, :] = out
            return new_state

        lax.fori_loop(0, nc, chunk, jnp.zeros((DN_DH, DN_DH), F32))
        y_ref[...] = _dn_post(o_ref[...], z_ref[...], ng_ref[...])

    hblk = pl.BlockSpec((t, DN_DH), lambda h: (0, h))
    return pl.pallas_call(
        body, grid=(DN_HEADS,),
        in_specs=proj_specs + w_specs + [gb_spec, pl.BlockSpec((1, DN_DH), lambda h: (0, 0))],
        out_specs=[hblk, hblk, pl.BlockSpec((1, nc, DN_DH, DN_DH), lambda h: (h, 0, 0, 0))],
        out_shape=[jax.ShapeDtypeStruct((t, BW), F32), jax.ShapeDtypeStruct((t, BW), F32),
                   jax.ShapeDtypeStruct((DN_HEADS, nc, DN_DH, DN_DH), F32)],
        scratch_shapes=[pltpu.VMEM((t, DN_DH), F32)] * 3 + _dn_loc_scratch(t, nc), name=name,
    )(proj, proj, proj, proj, conv_w, conv_w, conv_w, gb, ng)


def _dn_bwd(proj, conv_w, gb, ng, o, states, dy, name):
    t = proj.shape[0]
    nc = t // DN_CHUNK
    proj_specs, w_specs, gb_spec = _dn_specs(t)
    nloc = 6

    def body(q_ref, k_ref, v_ref, z_ref, wq_ref, wk_ref, wv_ref, gb_ref, ng_ref, o_ref, st_ref, dy_ref,
             dq_ref, dk_ref, dv_ref, dz_ref, dwq_ref, dwk_ref, dwv_ref, dgb_ref, dng_ref,
             qn_sc, kn_sc, vv_sc, do_sc, *rest):
        loc, dloc = rest[:nloc], rest[nloc:]
        qn_sc[...] = _dn_pre_q(_conv(q_ref[...], wq_ref[...]))
        kn_sc[...] = _dn_pre_k(_conv(k_ref[...], wk_ref[...]))
        vv_sc[...] = _silu(_conv(v_ref[...], wv_ref[...]))
        _, vjp_y = jax.vjp(_dn_post, o_ref[...], z_ref[...], ng_ref[...])
        do, dz, dng = vjp_y(dy_ref[...])
        do_sc[...] = do
        dz_ref[...] = dz.astype(dz_ref.dtype)
        dng_ref[0] = dng
        _dn_local_phase(nc, qn_sc, kn_sc, vv_sc, gb_ref, loc)
        u_sc, kcum_sc, qk_sc, qdec_sc, kdec_sc, egl_sc = loc

        def state_bwd(i, dstate):
            n = nc - 1 - i
            r = _chunk_rows(n)
            r8 = _egl_rows(n)
            _, vjp = jax.vjp(_dn_state, u_sc[r, :], kcum_sc[r, :], qk_sc[r, :], qdec_sc[r, :], kdec_sc[r, :],
                             egl_sc[r8, :][0:1], st_ref[0, n])
            du, dkcum, dqk, dqdec, dkdec, degl, dprev = vjp((do_sc[r, :], dstate))
            for d_sc, val in zip(dloc[:5], (du, dkcum, dqk, dqdec, dkdec)):
                d_sc[r, :] = val
            dloc[5][r8, :] = jnp.broadcast_to(degl, (8, DN_DH))
            return dprev

        lax.fori_loop(0, nc, state_bwd, jnp.zeros((DN_DH, DN_DH), F32))

        def local_bwd(i, carry):
            n = i * DN_BATCH
            r = _chunk_rows(n, DN_BATCH)
            _, vjp = jax.vjp(_dn_local, *_dn_local_inputs(n, qn_sc, kn_sc, vv_sc, gb_ref))
            cts = tuple(d_sc[r, :].reshape(DN_BATCH, DN_CHUNK, d_sc.shape[-1]) for d_sc in dloc[:5])
            cts += (dloc[5][_egl_rows(n, DN_BATCH), :].reshape(DN_BATCH, 8, DN_DH)[:, 0:1],)
            dqc, dkc, dvc, dg, dbeta = vjp(cts)
            for d_sc, val in zip((dloc[0], dloc[1], dloc[3]), (dqc, dkc, dvc)):
                d_sc[r, :] = val.reshape(DN_BATCH * DN_CHUNK, DN_DH)
            dgb_ref[0, r, :] = jnp.concatenate([dg, dbeta], axis=2).reshape(DN_BATCH * DN_CHUNK, 2)
            return carry

        lax.fori_loop(0, nc // DN_BATCH, local_bwd, 0)
        for x_ref, w_ref, pre, d_sc, dx_ref, dw_ref in ((q_ref, wq_ref, _dn_pre_q, dloc[0], dq_ref, dwq_ref),
                                                       (k_ref, wk_ref, _dn_pre_k, dloc[1], dk_ref, dwk_ref),
                                                       (v_ref, wv_ref, _silu, dloc[3], dv_ref, dwv_ref)):
            _, vjp = jax.vjp(pre, _conv(x_ref[...], w_ref[...]))
            (dc,) = vjp(d_sc[...])
            dx, dws = _conv_bwd(x_ref[...], w_ref[...], dc)
            dx_ref[...] = dx.astype(dx_ref.dtype)
            for j, dw in enumerate(dws):
                dw_ref[j:j + 1, :] = dw

    hblk = pl.BlockSpec((t, DN_DH), lambda h: (0, h))
    wout = pl.BlockSpec((4, DN_DH), lambda h: (0, h))
    return pl.pallas_call(
        body, grid=(DN_HEADS,),
        in_specs=proj_specs + w_specs + [gb_spec, pl.BlockSpec((1, DN_DH), lambda h: (0, 0)), hblk,
                                         pl.BlockSpec((1, nc, DN_DH, DN_DH), lambda h: (h, 0, 0, 0)), hblk],
        out_specs=[hblk] * 4 + [wout] * 3 + [gb_spec, pl.BlockSpec((1, 1, DN_DH), lambda h: (h, 0, 0))],
        out_shape=[jax.ShapeDtypeStruct((t, BW), BF16)] * 4 + [jax.ShapeDtypeStruct((4, BW), F32)] * 3
        + [jax.ShapeDtypeStruct((DN_HEADS, t, 2), F32), jax.ShapeDtypeStruct((DN_HEADS, 1, DN_DH), F32)],
        scratch_shapes=[pltpu.VMEM((t, DN_DH), F32)] * 4 + _dn_loc_scratch(t, nc) * 2, name=name,
    )(proj, proj, proj, proj, conv_w, conv_w, conv_w, gb, ng, o, states, dy)


MERGE_TM, MERGE_TN = 512, 512


def _merge_specs(t):
    tm, tn = min(MERGE_TM, t), MERGE_TN
    y_spec = pl.BlockSpec((tm, BW), lambda i, j: (i, 0))
    w_spec = pl.BlockSpec((3, BW, tn), lambda i, j: (0, 0, j))
    gate_specs = [pl.BlockSpec((tm, tn), functools.partial(lambda i, j, o: (i, o + j), o=(C_GATE + n * D_MODEL) // tn))
                  for n in range(3)]
    return tm, tn, [y_spec] * 3 + [w_spec] + gate_specs


def _merge_fwd(ys, wb, proj, name):
    t = proj.shape[0]
    tm, tn, in_specs = _merge_specs(t)

    def body(y0, y1, y2, w_ref, g0, g1, g2, o_ref):
        acc = jnp.zeros((tm, tn), F32)
        for n, (y, g) in enumerate(((y0, g0), (y1, g1), (y2, g2))):
            acc = acc + _dot(y[...].astype(BF16), w_ref[n], NN) * _sigmoid(g[...])
        o_ref[...] = acc.astype(o_ref.dtype)

    return pl.pallas_call(
        body, grid=(t // tm, D_MODEL // tn), in_specs=in_specs,
        out_specs=pl.BlockSpec((tm, tn), lambda i, j: (i, j)),
        out_shape=jax.ShapeDtypeStruct((t, D_MODEL), BF16), name=name,
    )(*ys, wb, proj, proj, proj)


def _merge_bwd(ys, wb, proj, dmerged, name):
    t = proj.shape[0]
    tm, tn, in_specs = _merge_specs(t)

    def body(y0, y1, y2, w_ref, g0, g1, g2, dm_ref, dg_ref, dt_ref):
        dm = dm_ref[...]
        for n, (y, g) in enumerate(((y0, g0), (y1, g1), (y2, g2))):
            tn_ = _dot(y[...].astype(BF16), w_ref[n], NN)
            sg = _sigmoid(g[...])
            dg_ref[n] = (dm * tn_ * sg * (1.0 - sg)).astype(dg_ref.dtype)
            dt_ref[n] = (dm * sg).astype(dt_ref.dtype)

    o3 = pl.BlockSpec((3, tm, tn), lambda i, j: (0, i, j))
    return pl.pallas_call(
        body, grid=(t // tm, D_MODEL // tn), in_specs=in_specs + [pl.BlockSpec((tm, tn), lambda i, j: (i, j))],
        out_specs=[o3, o3], out_shape=[jax.ShapeDtypeStruct((3, t, D_MODEL), BF16)] * 2, name=name,
    )(*ys, wb, proj, proj, proj, dmerged)


def _prep_rows(sp):
    z4, z112 = jnp.zeros((1, 4), F32), jnp.zeros((1, 112), F32)
    bias_row = jnp.concatenate([sp['b_fox_f'], z4, sp['dn_dt_bias'], z112], axis=1)
    alog_row = jnp.concatenate([jnp.zeros((1, 12), F32), sp['dn_a_log'], z112], axis=1)
    return bias_row, alog_row


def _sc_fwd_tile(sb, sc, sv, w):
    return (sb * _conv(sc * sv, w),)


def _ffn_act(ug, uv):
    return _silu(ug) * uv


W_IN_COLS = [(C_FQ, 0, 1536), (C_SB, 1544, 1536), (C_DQ, 3080, 1536), (C_DZ, 4624, 512), (C_GATE, 5136, 3072),
             (C_SMALL, 1536, 8), (C_SMALL + 8, 4616, 8)]
W_IN_SHARD = IN_ORIG // N_DEV


def _w_in_segments():
    out = []
    for d, o, w in W_IN_COLS:
        end = o + w
        while o < end:
            k = o // W_IN_SHARD
            n = min(end, (k + 1) * W_IN_SHARD) - o
            out.append((d, k, o - k * W_IN_SHARD, n))
            o, d = o + n, d + n
    return out


def _w_in_assemble(g, layer, name):
    tm = 256

    def body(g_ref, o_ref):
        for d, k, s, n in _w_in_segments():
            o_ref[:, d:d + n] = g_ref[k, :, s:s + n]
        o_ref[:, IN_ORIG:IN_P] = jnp.zeros((tm, IN_P - IN_ORIG), o_ref.dtype)

    return pl.pallas_call(
        body, grid=(D_MODEL // tm,),
        in_specs=[pl.BlockSpec((N_DEV, None, tm, W_IN_SHARD), lambda i: (0, layer, i, 0))],
        out_specs=pl.BlockSpec((tm, IN_P), lambda i: (i, 0)),
        out_shape=jax.ShapeDtypeStruct((D_MODEL, IN_P), g.dtype), name=name,
    )(g)


def _w_in_grad_split(g0, g1, name):
    tm = 128
    nt = D_MODEL // tm

    def scatter(g_ref, keep_ref, give_ref, core):
        for d, k, s, n in _w_in_segments():
            val = g_ref[:, d:d + n]
            if k % 2 == core:
                keep_ref[k // 2, :, s:s + n] = val
            else:
                give_ref[k // 2, :, s:s + n] = val.astype(give_ref.dtype)

    def body(g0_ref, g1_ref, keep_ref, give_ref):
        layer, c = pl.program_id(0), lax.axis_index("c")
        for which, g_ref in enumerate((g0_ref, g1_ref)):
            for core in range(2):
                pl.when(jnp.logical_and(layer == which, c == core))(
                    functools.partial(scatter, g_ref, keep_ref, give_ref, core))

    blk = pl.BlockSpec((4, None, tm, W_IN_SHARD), lambda l, i: (0, l, i, 0))
    shape = (4, DEPTH, D_MODEL, W_IN_SHARD)
    return pl.pallas_call(
        body, grid=(DEPTH, nt),
        in_specs=[pl.BlockSpec((tm, IN_P), lambda l, i: (i * (1 - l) + (nt - 1) * l, 0)),
                  pl.BlockSpec((tm, IN_P), lambda l, i: (i * l, 0))],
        out_specs=[blk, blk], out_shape=[jax.ShapeDtypeStruct(shape, F32), jax.ShapeDtypeStruct(shape, BF16)], name=name,
    )(g0, g1)


def _layer_fwd(x, p_i, w, sp, tag):
    t = x.shape[0]
    sv = {'x': x}
    (hn,) = _ew(lambda a, g: (_rms(a, g),), [x], [sp['g_mix']], [(D_MODEL, BF16)], name=f'rms_mix_{tag}')
    proj = _mm(hn, w['w_in'], 'nn', [F32], name=f'in_proj_{tag}')
    bias_row, alog_row = _prep_rows(sp)
    aux = _prep_fwd(proj, bias_row, alog_row, f'prep_{tag}')
    cf = aux[:, :FOX_HEADS].T
    cfc, cfr = cf[:, :, None], cf[:, None, :]
    y_fox, lse = _att_fwd(proj, sp['fox_q_gain'], sp['fox_k_gain'], cfc, cfr, f'fox_fwd_{tag}')
    (y_sc,) = _cb(_sc_fwd_tile, [(proj, C_SB), (proj, C_SC), (proj, C_SV)], [(w['sc_conv_w'], 0)], [F32], [],
                  tc=256, nblk=2, name=f'sc_fwd_{tag}')
    gb = jnp.stack([aux[:, 12:16].T, aux[:, 8:12].T], axis=-1)
    y_dn, o_dn, states = _dn_fwd(proj, w['dn_conv_w'], gb, sp['dn_norm_gain'], f'dn_fwd_{tag}')
    ys = (y_fox, y_sc, y_dn)
    merged = _merge_fwd(ys, w['w_branch'], proj, f'merge_fwd_{tag}')
    x1 = _mm(merged, w['w_o'], 'nn', [F32], epi=lambda acc, r: (acc + r,), epi_args=(x,), name=f'o_proj_{tag}')
    (hf,) = _ew(lambda a, g: (_rms(a, g),), [x1], [sp['g_ffn']], [(D_MODEL, BF16)], name=f'rms_ffn_{tag}')
    up = _mm(hf, w['w_up'], 'nn', [F32], name=f'up_proj_{tag}')
    (act,) = _cb(lambda ug, uv, wg, wv: (_ffn_act(_conv(ug, wg), _conv(uv, wv)),), [(up, 0), (up, D_FF)],
                 [(w['ffn_conv_w'], 0), (w['ffn_conv_w'], D_FF)], [BF16], [], tc=256, nblk=D_FF // 256, name=f'ffn_act_{tag}')
    x2 = _mm(act, w['w_down'], 'nn', [F32], epi=lambda acc, r: (acc + r,), epi_args=(x1,), name=f'down_proj_{tag}')
    (hp,) = _ew(lambda a, g: (_rms(a, g),), [x2], [sp['g_ple']], [(D_MODEL, BF16)], name=f'rms_ple_{tag}')
    gp = _mm(hp, w['w_ple_gate'], 'nn', [F32], name=f'ple_gate_{tag}')
    x3 = _mm(p_i, w['w_ple'], 'nn', [F32], epi=lambda acc, g, r: (r + _sigmoid(g) * acc,), epi_args=(gp, x2), name=f'ple_{tag}')
    sv.update(hn=hn, proj=proj, aux=aux, cfc=cfc, cfr=cfr, lse=lse, ys=ys, gb=gb, o_dn=o_dn,
              states=states, merged=merged, x1=x1, hf=hf, up=up, act=act, x2=x2, hp=hp, gp=gp, p=p_i,
              bias_row=bias_row, alog_row=alog_row)
    return x3, sv


def _rms_bwd(x, g, dh, dres, name):
    def fn(xv, dhv, dr, gv):
        _, vjp = jax.vjp(_rms, xv, gv)
        dx, dg = vjp(dhv)
        return dr + dx, dg
    return _ew(fn, [x, dh, dres], [g], [(D_MODEL, F32)], [(1, D_MODEL)], name=name)


def _layer_bwd(dx3, sv, w, sp, tag):
    t = dx3.shape[0]
    g = {}
    def ple_epi(acc, gpv, d):
        s = _sigmoid(gpv)
        return d * acc * s * (1.0 - s), d * s
    dgp, de = _mm(sv['p'], w['w_ple'], 'nn', [BF16, BF16], epi=ple_epi, epi_args=(sv['gp'], dx3), name=f'ple_bwd_{tag}')
    g['w_ple'] = _mm(sv['p'], de, 'tn', [F32], name=f'd_w_ple_{tag}')
    g['w_ple_gate'] = _mm(sv['hp'], dgp, 'tn', [F32], name=f'd_w_ple_gate_{tag}')
    dhp = _mm(dgp, w['w_ple_gate'], 'nt', [F32], name=f'd_hp_{tag}')
    dx2, g['g_ple'] = _rms_bwd(sv['x2'], sp['g_ple'], dhp, dx3, f'rms_ple_bwd_{tag}')
    dact = _mm(dx2, w['w_down'], 'nt', [F32], name=f'd_act_{tag}')
    g['w_down'] = _mm(sv['act'], dx2, 'tn', [F32], name=f'd_w_down_{tag}')

    def ffn_bwd_tile(ug, uv, da, wg, wv):
        cg, cv = _conv(ug, wg), _conv(uv, wv)
        _, vjp = jax.vjp(_ffn_act, cg, cv)
        dcg, dcv = vjp(da)
        dug, dwg = _conv_bwd(ug, wg, dcg)
        duv, dwv = _conv_bwd(uv, wv, dcv)
        return dug, duv, jnp.concatenate(dwg, axis=0), jnp.concatenate(dwv, axis=0)
    dupg, dupv, dwg, dwv = _cb(ffn_bwd_tile, [(sv['up'], 0), (sv['up'], D_FF), (dact, 0)],
                               [(w['ffn_conv_w'], 0), (w['ffn_conv_w'], D_FF)], [BF16, BF16], [3, 3], tc=256,
                               nblk=D_FF // 256, name=f'ffn_act_bwd_{tag}')
    dup = jnp.concatenate([dupg, dupv], axis=1)
    g['ffn_conv_w'] = jnp.concatenate([dwg, dwv], axis=1)
    g['w_up'] = _mm(sv['hf'], dup, 'tn', [F32], name=f'd_w_up_{tag}')
    dhf = _mm(dup, w['w_up'], 'nt', [F32], name=f'd_hf_{tag}')
    dx1, g['g_ffn'] = _rms_bwd(sv['x1'], sp['g_ffn'], dhf, dx2, f'rms_ffn_bwd_{tag}')
    dmerged = _mm(dx1, w['w_o'], 'nt', [F32], name=f'd_merged_{tag}')
    g['w_o'] = _mm(sv['merged'], dx1, 'tn', [F32], name=f'd_w_o_{tag}')
    dgate, dtn = _merge_bwd(sv['ys'], w['w_branch'], sv['proj'], dmerged, f'merge_bwd_{tag}')
    dys, dwb = [], []
    for n in range(3):
        dys.append(_mm(dtn[n], w['w_branch'][n], 'nt', [F32], name=f'd_y{n}_{tag}'))
        dwb.append(_mm(sv['ys'][n], dtn[n], 'tn', [F32], name=f'd_w_branch{n}_{tag}'))
    g['w_branch'] = jnp.stack(dwb)
    ddq, ddk, ddv, ddz, dwq, dwk, dwv_, dgb, dng = _dn_bwd(sv['proj'], w['dn_conv_w'], sv['gb'], sp['dn_norm_gain'],
                                                           sv['o_dn'], sv['states'], dys[2], f'dn_bwd_{tag}')
    g['dn_conv_w'] = jnp.concatenate([dwq, dwk, dwv_], axis=1)
    g['dn_norm_gain'] = jnp.sum(dng, axis=0)
    def sc_bwd_tile(sb, sc, svv, dy, wv):
        u = sc * svv
        dsb = dy * _conv(u, wv)
        du, dws = _conv_bwd(u, wv, dy * sb)
        return dsb, du * svv, du * sc, jnp.concatenate(dws, axis=0)
    dsb, dsc, dsv, g['sc_conv_w'] = _cb(sc_bwd_tile, [(sv['proj'], C_SB), (sv['proj'], C_SC), (sv['proj'], C_SV), (dys[1], 0)],
                                        [(w['sc_conv_w'], 0)], [BF16, BF16, BF16], [3], tc=256, nblk=2, name=f'sc_bwd_{tag}')
    dfq, dfk, dfv, dcfc, dcfr, dqg, dkg = _att_bwd(sv['proj'], sp['fox_q_gain'], sp['fox_k_gain'], sv['cfc'], sv['cfr'],
                                                   sv['lse'], sv['ys'][0], dys[0], f'fox_bwd_{tag}')
    g['fox_q_gain'] = jnp.sum(dqg, axis=0)
    g['fox_k_gain'] = jnp.sum(dkg, axis=0)
    dcf = (dcfc[:, :, 0] + dcfr[:, 0, :]).T
    daux = jnp.concatenate([dcf, dgb[:, :, 1].T, dgb[:, :, 0].T, jnp.zeros((t, 112), F32)], axis=1)
    dsmall, dbias, dalog = _prep_bwd(sv['proj'], sv['bias_row'], sv['alog_row'], daux, f'prep_bwd_{tag}')
    g['b_fox_f'] = dbias[:, 0:8]
    g['dn_dt_bias'] = dbias[:, 12:16]
    g['dn_a_log'] = dalog[:, 12:16]
    dproj = jnp.concatenate([dfq, dfk, dfv, dsb, dsc, dsv, ddq, ddk, ddv, ddz, dgate[0], dgate[1], dgate[2], dsmall], axis=1)
    g['w_in'] = _mm(sv['hn'], dproj, 'tn', [F32], name=f'd_w_in_{tag}')
    dhn = _mm(dproj, w['w_in'], 'nt', [F32], name=f'd_hn_{tag}')
    dx, g['g_mix'] = _rms_bwd(sv['x'], sp['g_mix'], dhn, dx1, f'rms_mix_bwd_{tag}')
    return dx, g


def _loss_bwd(y, target, name):
    inv = 1.0 / y.shape[1]

    def fn(yv, tv):
        err = yv - tv
        return err * inv, jnp.zeros((8, 128), F32) + 0.5 * inv * jnp.sum(err * err)
    return _ew(fn, [y, target], [], [(y.shape[1], F32)], [(8, 128)], name=name)


PACK_W = 1024
FULL_SHAPE = {'w_in': (D_MODEL, IN_ORIG), 'w_branch': (3, BW, D_MODEL), 'w_o': (D_MODEL, D_MODEL), 'w_up': (D_MODEL, 2 * D_FF),
              'w_down': (D_FF, D_MODEL), 'w_ple_gate': (D_MODEL, D_MODEL), 'w_ple': (PLE_DIM, D_MODEL),
              'sc_conv_w': (3, BW), 'dn_conv_w': (4, 3 * BW), 'ffn_conv_w': (3, 2 * D_FF)}
SMALL_SHAPE = {'g_mix': D_MODEL, 'b_fox_f': FOX_HEADS, 'fox_q_gain': FOX_DH, 'fox_k_gain': FOX_DH, 'dn_a_log': DN_HEADS,
               'dn_dt_bias': DN_HEADS, 'dn_norm_gain': DN_DH, 'g_ffn': D_MODEL, 'g_ple': D_MODEL}


def _shard_shape(name):
    s = list(FULL_SHAPE[name])
    s[SHARDED[name]] //= N_DEV
    return tuple(s)


def _full_from_gathered(g, name, layer):
    sh, ax = _shard_shape(name), SHARDED[name]
    blocks = jnp.moveaxis(g[:, layer], 0, ax)
    return blocks.reshape(sh[:ax] + (N_DEV * sh[ax],) + sh[ax + 1:])


def _by_dest(full, name):
    sh, ax = _shard_shape(name), SHARDED[name]
    return jnp.moveaxis(full.reshape(sh[:ax] + (N_DEV, sh[ax]) + sh[ax + 1:]), ax, 0)


def _flat_pack(arrs):
    flat = jnp.concatenate([a.reshape(-1).astype(F32) for a in arrs])
    rows = -(-flat.shape[0] // (8 * PACK_W)) * 8
    return jnp.pad(flat, (0, rows * PACK_W - flat.shape[0])).reshape(rows, PACK_W)


def _flat_unpack(pack, shapes):
    flat, out, off = pack.reshape(-1), [], 0
    for s in shapes:
        n = 1
        for d in s:
            n *= d
        out.append(flat[off:off + n].reshape(s))
        off += n
    return out


SMALL_SHAPES = [(DEPTH, SMALL_SHAPE[n]) for n in SMALL]
SMALL_LOSS_AT = sum(DEPTH * SMALL_SHAPE[n] for n in SMALL)
CONV_SHARD_SHAPES = [(DEPTH,) + _shard_shape(n) for n in CONVW]
CONV_FULL_SHAPES = [(DEPTH,) + FULL_SHAPE[n] for n in CONVW]


def _rows2d(a):
    return a.reshape(-1, a.shape[-1])


def _pick_tm(m, width):
    best = None
    for tm in range(16, m + 1, 16):
        if m % tm == 0 and tm * width * 4 <= (1 << 20):
            best = tm
    return best if best is not None else m


HBM_SPEC = pl.BlockSpec(memory_space=pltpu.HBM)


def _place():
    x, y, c = lax.axis_index("x"), lax.axis_index("y"), lax.axis_index("c")
    return x, y, c, [(1 - x, y), (x, 1 - y), (1 - x, 1 - y)]


def _all_gather(arrs, name):
    n = len(arrs)

    def body(*refs):
        ins, outs = refs[:n], refs[n:2 * n]
        send_sems, recv_sems, local_sems = refs[2 * n:]
        x, y, c, chips = _place()
        me, sibling = (x, y, c), (x, y, 1 - c)

        def block(a, p):
            return outs[a].at[4 * p[0] + 2 * p[1] + p[2]]

        def copy(a, k, blk, to, src=None):
            return pltpu.make_async_remote_copy(src_ref=block(a, blk) if src is None else src, dst_ref=block(a, blk),
                                                send_sem=send_sems.at[a, k], recv_sem=recv_sems.at[a, k],
                                                device_id=to, device_id_type=MESH)

        mine = [pltpu.make_async_copy(ins[a], block(a, me), local_sems.at[a]) for a in range(n)]
        first, passed = [], []
        for a in range(n):
            mine[a].start()
            first.append(copy(a, 0, me, sibling, src=ins[a]))
            first += [copy(a, 1 + j, me, (*chip, c), src=ins[a]) for j, chip in enumerate(chips)]
        for cp in first:
            cp.start()
        for j, chip in enumerate(chips):
            for a in range(n):
                copy(a, 1 + j, (*chip, c), me).wait_recv()
                fwd = copy(a, 4 + j, (*chip, c), sibling)
                fwd.start()
                passed.append(fwd)
        for a in range(n):
            copy(a, 0, sibling, me).wait_recv()
            for j, chip in enumerate(chips):
                copy(a, 4 + j, (*chip, 1 - c), me).wait_recv()
        for cp in first + passed:
            cp.wait_send()
        for a in range(n):
            mine[a].wait()

    return pl.pallas_call(
        body, in_specs=[HBM_SPEC] * n, out_specs=[HBM_SPEC] * n,
        out_shape=[jax.ShapeDtypeStruct((N_DEV,) + a.shape, a.dtype) for a in arrs],
        scratch_shapes=[pltpu.SemaphoreType.DMA((n, 7)), pltpu.SemaphoreType.DMA((n, 7)), pltpu.SemaphoreType.DMA((n,))],
        name=name,
    )(*arrs)


def _sibling_swap(arrs, name):
    n = len(arrs)

    def body(*refs):
        send_sems, recv_sems = refs[2 * n:]
        x, y, c, _ = _place()
        cps = [pltpu.make_async_remote_copy(src_ref=refs[a], dst_ref=refs[n + a], send_sem=send_sems.at[a], recv_sem=recv_sems.at[a],
                                            device_id=(x, y, 1 - c), device_id_type=MESH) for a in range(n)]
        for cp in cps:
            cp.start()
        for cp in cps:
            cp.wait()

    return pl.pallas_call(body, in_specs=[HBM_SPEC] * n, out_specs=[HBM_SPEC] * n,
                          out_shape=[jax.ShapeDtypeStruct(a.shape, a.dtype) for a in arrs],
                          scratch_shapes=[pltpu.SemaphoreType.DMA((n,)), pltpu.SemaphoreType.DMA((n,))], name=name)(*arrs)


def _chip_exchange(arrs, name):
    n = len(arrs)

    def body(*refs):
        send_sems, recv_sems = refs[2 * n:]
        x, y, c, chips = _place()
        cps = [pltpu.make_async_remote_copy(src_ref=refs[a].at[2 * cx + cy], dst_ref=refs[n + a].at[j], send_sem=send_sems.at[a, j],
                                            recv_sem=recv_sems.at[a, j], device_id=(cx, cy, c), device_id_type=MESH)
               for a in range(n) for j, (cx, cy) in enumerate(chips)]
        for cp in cps:
            cp.start()
        for cp in cps:
            cp.wait()

    return pl.pallas_call(body, in_specs=[HBM_SPEC] * n, out_specs=[HBM_SPEC] * n,
                          out_shape=[jax.ShapeDtypeStruct((3,) + a.shape[1:], a.dtype) for a in arrs],
                          scratch_shapes=[pltpu.SemaphoreType.DMA((n, 3)), pltpu.SemaphoreType.DMA((n, 3))], name=name)(*arrs)


def _adamw(w, g, m, v):
    m = ADAM_B1 * m + (1.0 - ADAM_B1) * g
    v = ADAM_B2 * v + (1.0 - ADAM_B2) * jnp.square(g)
    m_hat = m / (1.0 - ADAM_B1 ** ADAM_STEP)
    v_hat = v / (1.0 - ADAM_B2 ** ADAM_STEP)
    delta = -ADAM_LR * (m_hat / (jnp.sqrt(v_hat) + ADAM_EPS) + ADAM_WD * w)
    return delta, m, v


def _adamw_call(w, m, v, gparts, name):
    def fn(wv, mv, vv, *gs):
        g = gs[0].astype(F32)
        for gp in gs[1:]:
            g = g + gp.astype(F32)
        delta, m2, v2 = _adamw(wv, g, mv, vv)
        return g, delta, m2, v2
    rows, c = w.shape
    return _ew(fn, [w, m, v] + list(gparts), [], [(c, F32)] * 4, tm=_pick_tm(rows, c), name=name)


def kernel(x, p, g_mix, w_in, b_fox_f, fox_q_gain, fox_k_gain, sc_conv_w, dn_conv_w, dn_a_log, dn_dt_bias,
           dn_norm_gain, w_branch, w_o, g_ffn, w_up, ffn_conv_w, w_down, g_ple, w_ple_gate, w_ple, loss_target,
           m_g_mix, m_w_in, m_b_fox_f, m_fox_q_gain, m_fox_k_gain, m_sc_conv_w, m_dn_conv_w, m_dn_a_log,
           m_dn_dt_bias, m_dn_norm_gain, m_w_branch, m_w_o, m_g_ffn, m_w_up, m_ffn_conv_w, m_w_down, m_g_ple,
           m_w_ple_gate, m_w_ple, v_g_mix, v_w_in, v_b_fox_f, v_fox_q_gain, v_fox_k_gain, v_sc_conv_w, v_dn_conv_w,
           v_dn_a_log, v_dn_dt_bias, v_dn_norm_gain, v_w_branch, v_w_o, v_g_ffn, v_w_up, v_ffn_conv_w, v_w_down,
           v_g_ple, v_w_ple_gate, v_w_ple):
    return _step(x, p, g_mix, w_in, b_fox_f, fox_q_gain, fox_k_gain, sc_conv_w, dn_conv_w, dn_a_log, dn_dt_bias,
                 dn_norm_gain, w_branch, w_o, g_ffn, w_up, ffn_conv_w, w_down, g_ple, w_ple_gate, w_ple, loss_target,
                 m_g_mix, m_w_in, m_b_fox_f, m_fox_q_gain, m_fox_k_gain, m_sc_conv_w, m_dn_conv_w, m_dn_a_log,
                 m_dn_dt_bias, m_dn_norm_gain, m_w_branch, m_w_o, m_g_ffn, m_w_up, m_ffn_conv_w, m_w_down, m_g_ple,
                 m_w_ple_gate, m_w_ple, v_g_mix, v_w_in, v_b_fox_f, v_fox_q_gain, v_fox_k_gain, v_sc_conv_w,
                 v_dn_conv_w, v_dn_a_log, v_dn_dt_bias, v_dn_norm_gain, v_w_branch, v_w_o, v_g_ffn, v_w_up,
                 v_ffn_conv_w, v_w_down, v_g_ple, v_w_ple_gate, v_w_ple)


def _step(*args):
    names = ['x', 'p'] + WEIGHTS + ['loss_target'] + ['m_' + n for n in WEIGHTS] + ['v_' + n for n in WEIGHTS]
    assert len(args) == len(names)
    a = dict(zip(names, args))
    x, target = a['x'][0], a['loss_target'][0]
    p = a['p'][:, 0]
    _, _, c, _ = _place()
    chip = 2 * lax.axis_index("x") + lax.axis_index("y")

    dev = 4 * lax.axis_index("x") + 2 * lax.axis_index("y") + c

    gathered = _all_gather([a[n].astype(BF16) for n in BIG] + [_flat_pack([a[n] for n in CONVW])], 'gather_weights')
    conv_by_dev = [_flat_unpack(gathered[-1][d], CONV_SHARD_SHAPES) for d in range(N_DEV)]
    conv_full = {n: jnp.concatenate([conv_by_dev[d][i] for d in range(N_DEV)], axis=2) for i, n in enumerate(CONVW)}
    ws, sps = [], []
    for layer in range(DEPTH):
        w = {n: _full_from_gathered(g, n, layer) for n, g in zip(BIG, gathered) if n != 'w_in'}
        w.update({n: conv_full[n][layer] for n in CONVW})
        w['w_in'] = _w_in_assemble(gathered[list(BIG).index('w_in')], layer, f'w_in_assemble_l{layer}')
        ws.append(w)
        sps.append({n: a[n][layer][None, :] for n in SMALL})

    h, saved = x, []
    for layer in range(DEPTH):
        h, sv = _layer_fwd(h, p[layer], ws[layer], sps[layer], f'l{layer}')
        saved.append(sv)
    dh, loss_part = _loss_bwd(h, target, 'loss')
    grads = [None] * DEPTH
    for layer in reversed(range(DEPTH)):
        dh, grads[layer] = _layer_bwd(dh, saved[layer], ws[layer], sps[layer], f'l{layer}')
    grad_x = dh[None]

    keeps, gives = [], []
    for n in BIG:
        if n == 'w_in':
            keep, give = _w_in_grad_split(grads[0][n], grads[1][n], 'w_in_grad_split')
            keeps.append(keep)
            gives.append(give)
            continue
        by_dest = jnp.stack([_by_dest(grads[layer][n], n) for layer in range(DEPTH)], axis=1)
        by_dest = by_dest.reshape((4, 2) + by_dest.shape[1:])
        keeps.append(lax.dynamic_index_in_dim(by_dest, c, axis=1, keepdims=False))
        gives.append(lax.dynamic_index_in_dim(by_dest, 1 - c, axis=1, keepdims=False).astype(BF16))
    gots = _sibling_swap(gives, 'grad_sibling_swap')
    parts, parts_b = [], []
    for n, keep, got in zip(BIG, keeps, gots):
        k2, g2 = _rows2d(keep), _rows2d(got)
        part, part_b = _ew(lambda k, r: (k + r.astype(F32),) * 2, [k2, g2], [], [(k2.shape[1], F32), (k2.shape[1], BF16)],
                           tm=_pick_tm(*k2.shape), name=f'grad_chip_sum_{n}')
        parts.append(part.reshape(keep.shape))
        parts_b.append(part_b.reshape(keep.shape))
    others = _chip_exchange(parts_b, 'grad_chip_exchange')
    out = {}
    for n, part, other in zip(BIG, parts, others):
        own = lax.dynamic_index_in_dim(part, chip, axis=0, keepdims=False)
        res = _adamw_call(_rows2d(a[n]), _rows2d(a['m_' + n]), _rows2d(a['v_' + n]),
                          [_rows2d(own)] + [_rows2d(other[j]) for j in range(3)], f'adamw_{n}')
        out[n] = [r.reshape(a[n].shape) for r in res]

    small_pack = _flat_pack([jnp.stack([grads[layer][n].reshape(-1) for layer in range(DEPTH)]) for n in SMALL] + [loss_part[0, 0]])
    conv_pack = _flat_pack([jnp.stack([grads[layer][n] for layer in range(DEPTH)]) for n in CONVW])
    small_all, conv_all = _all_gather([small_pack, conv_pack], 'gather_small_grads')
    res = _adamw_call(_flat_pack([a[n] for n in SMALL]), _flat_pack([a['m_' + n] for n in SMALL]),
                      _flat_pack([a['v_' + n] for n in SMALL]), [small_all[d] for d in range(N_DEV)], 'adamw_replicated')
    loss = res[0].reshape(-1)[SMALL_LOSS_AT]
    for k, r in enumerate(res):
        for n, val in zip(SMALL, _flat_unpack(r, SMALL_SHAPES)):
            out.setdefault(n, [None] * 4)[k] = val
    (conv_sum,) = _ew(lambda *gs: (functools.reduce(lambda s, t: s + t, gs),), [conv_all[d] for d in range(N_DEV)], [],
                      [(PACK_W, F32)], tm=conv_pack.shape[0], name='conv_grad_sum')
    conv_own = [lax.dynamic_slice_in_dim(g, dev * (g.shape[2] // N_DEV), g.shape[2] // N_DEV, axis=2)
                for g in _flat_unpack(conv_sum, CONV_FULL_SHAPES)]
    res = _adamw_call(_flat_pack([a[n] for n in CONVW]), _flat_pack([a['m_' + n] for n in CONVW]),
                      _flat_pack([a['v_' + n] for n in CONVW]), [_flat_pack(conv_own)], 'adamw_conv')
    for k, r in enumerate(res):
        for n, val in zip(CONVW, _flat_unpack(r, CONV_SHARD_SHAPES)):
            out.setdefault(n, [None] * 4)[k] = val

    outs = [loss, grad_x]
    for k in range(4):
        outs += [out[n][k] for n in WEIGHTS]
    return tuple(outs)
```

```python
import functools

import jax
import jax.numpy as jnp
from jax import lax
from jax.experimental import pallas as pl
from jax.experimental.pallas import tpu as pltpu

F32 = jnp.float32
BF16 = jnp.bfloat16
HI = lax.Precision.HIGHEST

D_MODEL = 1024
DEPTH = 2
N_DEV = 8
PLE_DIM = 256
BW = 512
FOX_HEADS, FOX_DH = 8, 64
DN_HEADS, DN_DH = 4, 128
DN_CHUNK = 64
D_FF = 2816
EPS = 1e-6
NEG = -1e30

ADAM_LR, ADAM_B1, ADAM_B2, ADAM_EPS, ADAM_WD, ADAM_STEP = 0.001, 0.9, 0.999, 1e-08, 0.01, 10

C_FQ, C_FK, C_FV = 0, 512, 1024
C_SB, C_SC, C_SV = 1536, 2048, 2560
C_DQ, C_DK, C_DV, C_DZ = 3072, 3584, 4096, 4608
C_GATE = 5120
C_SMALL = 8192
IN_P = 8320
IN_ORIG = 8208

WEIGHTS = ['g_mix', 'w_in', 'b_fox_f', 'fox_q_gain', 'fox_k_gain', 'sc_conv_w', 'dn_conv_w', 'dn_a_log',
           'dn_dt_bias', 'dn_norm_gain', 'w_branch', 'w_o', 'g_ffn', 'w_up', 'ffn_conv_w', 'w_down', 'g_ple',
           'w_ple_gate', 'w_ple']
BIG = {'w_in': 1, 'w_branch': 2, 'w_o': 0, 'w_up': 1, 'w_down': 0, 'w_ple_gate': 0, 'w_ple': 1}
CONVW = {'sc_conv_w': 1, 'dn_conv_w': 1, 'ffn_conv_w': 1}
SHARDED = {**BIG, **CONVW}
SMALL = [n for n in WEIGHTS if n not in SHARDED]
MESH = pl.DeviceIdType.MESH


def _sigmoid(x):
    return 0.5 * (jnp.tanh(0.5 * x) + 1.0)


def _silu(x):
    return x * _sigmoid(x)


def _log1pexp_negabs(z):
    return jnp.log(1.0 + jnp.exp(-jnp.abs(z)))


def _log_sigmoid(z):
    return jnp.minimum(z, 0.0) - _log1pexp_negabs(z)


def _softplus(z):
    return jnp.maximum(z, 0.0) + _log1pexp_negabs(z)


def _rms(x, g):
    return x * lax.rsqrt(jnp.mean(x * x, axis=-1, keepdims=True) + EPS) * g


def _l2(x):
    return x * lax.rsqrt(jnp.sum(x * x, axis=-1, keepdims=True) + EPS)


def _dot(a, b, dims, precision=None):
    return lax.dot_general(a, b, (dims, ((), ())), preferred_element_type=F32, precision=precision)


NN = ((1,), (0,))
NT = ((1,), (1,))
TN = ((0,), (0,))


def _shift_down(x, s):
    if s == 0:
        return x
    t = lax.broadcasted_iota(jnp.int32, x.shape, 0)
    return jnp.where(t >= s, pltpu.roll(x, s, 0), 0.0)


def _shift_up(x, s):
    if s == 0:
        return x
    n = x.shape[0]
    t = lax.broadcasted_iota(jnp.int32, x.shape, 0)
    return jnp.where(t < n - s, pltpu.roll(x, n - s, 0), 0.0)


def _conv(x, w):
    k = w.shape[0]
    y = w[k - 1:k] * x
    for j in range(k - 1):
        y = y + w[j:j + 1] * _shift_down(x, k - 1 - j)
    return y


def _conv_bwd(x, w, dy):
    k = w.shape[0]
    dx = w[k - 1:k] * dy
    dws = []
    for j in range(k - 1):
        dx = dx + w[j:j + 1] * _shift_up(dy, k - 1 - j)
        dws.append(jnp.sum(dy * _shift_down(x, k - 1 - j), axis=0, keepdims=True))
    dws.append(jnp.sum(dy * x, axis=0, keepdims=True))
    return dx, dws


MM_VMEM_BUDGET = 36 << 20
MM_STEP_BYTES = 1 << 20


def _mm_tiles(m, n, k, a_size, b_size, tile_size, cast_a):
    best = None
    for tm in [d for d in (2048, 1024, 512, 256, 128) if d <= m and m % d == 0] or [m]:
        for tn in [d for d in range(128, min(n, 2048) + 1, 128) if n % d == 0] or [n]:
            vmem = 2 * tm * k * a_size + (2 * tm * k if cast_a else 0) + 2 * k * tn * b_size + 2 * tm * tn * tile_size
            if vmem > MM_VMEM_BUDGET:
                continue
            steps = (m // tm) * (n // tn)
            cost = m * k * a_size + (m // tm) * k * n * b_size + m * n * tile_size + steps * MM_STEP_BYTES
            if best is None or cost < best[0]:
                best = (cost, tm, tn)
    assert best is not None, (m, n, k)
    return best[1], best[2]


def _mm(a, b, mode, outs, *, epi=None, epi_args=(), name):
    if mode == 'nn':
        (m, k), (k2, n) = a.shape, b.shape
    elif mode == 'nt':
        (m, k), (n, k2) = a.shape, b.shape
    else:
        (k, m), (k2, n) = a.shape, b.shape
    assert k == k2, (a.shape, b.shape, mode)
    tile_size = sum(jnp.dtype(dt).itemsize for dt in outs) + sum(e.dtype.itemsize for e in epi_args if e.shape[0] != 1)
    tm, tn = _mm_tiles(m, n, k, a.dtype.itemsize, b.dtype.itemsize, tile_size, a.dtype != BF16)
    dims = {'nn': NN, 'nt': NT, 'tn': TN}[mode]
    a_spec = pl.BlockSpec((k, tm), lambda i, j: (0, i)) if mode == 'tn' else pl.BlockSpec((tm, k), lambda i, j: (i, 0))
    b_spec = pl.BlockSpec((tn, k), lambda i, j: (j, 0)) if mode == 'nt' else pl.BlockSpec((k, tn), lambda i, j: (0, j))
    e_specs = [pl.BlockSpec((1, tn), lambda i, j: (0, j)) if e.shape[0] == 1 else pl.BlockSpec((tm, tn), lambda i, j: (i, j))
               for e in epi_args]
    ne, no = len(epi_args), len(outs)
    cast_a = a.dtype != BF16

    def body(a_ref, b_ref, *rest):
        if cast_a:
            a_sc = rest[-1]

            @pl.when(pl.program_id(1) == 0)
            def _():
                a_sc[...] = a_ref[...].astype(BF16)
            av = a_sc[...]
        else:
            av = a_ref[...]
        acc = _dot(av, b_ref[...].astype(BF16), dims)
        vals = epi(acc, *[e[...] for e in rest[:ne]]) if epi is not None else (acc,)
        for o_ref, v in zip(rest[ne:ne + no], vals):
            o_ref[...] = v.astype(o_ref.dtype)

    res = pl.pallas_call(
        body, grid=(m // tm, n // tn),
        in_specs=[a_spec, b_spec] + e_specs,
        out_specs=[pl.BlockSpec((tm, tn), lambda i, j: (i, j)) for _ in outs],
        out_shape=[jax.ShapeDtypeStruct((m, n), dt) for dt in outs],
        scratch_shapes=[pltpu.VMEM(a_spec.block_shape, BF16)] if cast_a else [],
        name=name,
    )(a, b, *epi_args)
    return res[0] if no == 1 else res


def _ew(fn, tiled, bcast, outs, reds=(), *, tm=256, name):
    secs = [(t, 0, t.shape[1]) if not isinstance(t, tuple) else t for t in tiled]
    m = secs[0][0].shape[0]
    tm = min(tm, m)
    assert m % tm == 0
    in_specs = []
    for arr, off, w in secs:
        assert off % w == 0
        in_specs.append(pl.BlockSpec((tm, w), functools.partial(lambda i, c: (i, c), c=off // w)))
    in_specs += [pl.BlockSpec(b.shape, lambda i: (0, 0)) for b in bcast]
    nin, no = len(in_specs), len(outs)

    def body(*refs):
        vals = fn(*[r[...] for r in refs[:nin]])
        for r, v in zip(refs[nin:nin + no], vals[:no]):
            r[...] = v.astype(r.dtype)
        i = pl.program_id(0)
        for r, v in zip(refs[nin + no:], vals[no:]):
            @pl.when(i == 0)
            def _():
                r[...] = v

            @pl.when(i > 0)
            def _():
                r[...] += v

    res = pl.pallas_call(
        body, grid=(m // tm,), in_specs=in_specs,
        out_specs=[pl.BlockSpec((tm, c), lambda i: (i, 0)) for c, _ in outs] + [pl.BlockSpec(s, lambda i: (0, 0)) for s in reds],
        out_shape=[jax.ShapeDtypeStruct((m, c), dt) for c, dt in outs] + [jax.ShapeDtypeStruct(s, F32) for s in reds],
        name=name,
    )(*[s[0] for s in secs], *bcast)
    return res


def _cb(fn, cols, params, outs, pouts, *, tc, nblk, name):
    t = cols[0][0].shape[0]
    in_specs = []
    for arr, off in cols:
        assert off % tc == 0
        in_specs.append(pl.BlockSpec((t, tc), functools.partial(lambda c, o: (0, o + c), o=off // tc)))
    for arr, off in params:
        in_specs.append(pl.BlockSpec((arr.shape[0], tc), functools.partial(lambda c, o: (0, o + c), o=off // tc)))
    nin, no = len(in_specs), len(outs)

    def body(*refs):
        vals = fn(*[r[...] for r in refs[:nin]])
        for r, v in zip(refs[nin:], vals):
            r[...] = v.astype(r.dtype)

    return pl.pallas_call(
        body, grid=(nblk,), in_specs=in_specs,
        out_specs=[pl.BlockSpec((t, tc), lambda c: (0, c)) for _ in outs] + [pl.BlockSpec((k, tc), lambda c: (0, c)) for k in pouts],
        out_shape=[jax.ShapeDtypeStruct((t, nblk * tc), dt) for dt in outs] + [jax.ShapeDtypeStruct((k, nblk * tc), F32) for k in pouts],
        name=name,
    )(*[c[0] for c in cols], *[p[0] for p in params])


CUM_BLK = 256


def _prep_point(s, bias, alog):
    lane = lax.broadcasted_iota(jnp.int32, s.shape, 1)
    z = s + bias
    return jnp.where(lane < 8, _log_sigmoid(z),
                     jnp.where(lane < 12, _sigmoid(s),
                               jnp.where(lane < 16, -jnp.exp(alog) * _softplus(z), 0.0)))


def _tri(n, upper):
    r = lax.broadcasted_iota(jnp.int32, (n, n), 0)
    c = lax.broadcasted_iota(jnp.int32, (n, n), 1)
    return (r <= c if upper else r >= c).astype(F32)


def _prep_fwd(proj, bias_row, alog_row, name):
    t = proj.shape[0]
    nb = t // CUM_BLK

    def body(s_ref, b_ref, a_ref, o_ref):
        pre = _prep_point(s_ref[...], b_ref[...], a_ref[...])
        lane = lax.broadcasted_iota(jnp.int32, (CUM_BLK, 128), 1)
        tri = _tri(CUM_BLK, False)
        carry = jnp.zeros((1, 128), F32)
        for blk in range(nb):
            xb = pre[blk * CUM_BLK:(blk + 1) * CUM_BLK]
            cb = _dot(tri, xb, NN, HI) + carry
            carry = cb[CUM_BLK - 1:CUM_BLK]
            o_ref[blk * CUM_BLK:(blk + 1) * CUM_BLK, :] = jnp.where(lane < 8, cb, xb)

    return pl.pallas_call(
        body, grid=(1,),
        in_specs=[pl.BlockSpec((t, 128), lambda i: (0, C_SMALL // 128)), pl.BlockSpec((1, 128), lambda i: (0, 0)),
                  pl.BlockSpec((1, 128), lambda i: (0, 0))],
        out_specs=pl.BlockSpec((t, 128), lambda i: (0, 0)),
        out_shape=jax.ShapeDtypeStruct((t, 128), F32), name=name,
    )(proj, bias_row, alog_row)


def _prep_bwd(proj, bias_row, alog_row, daux, name):
    t = proj.shape[0]
    nb = t // CUM_BLK

    def body(s_ref, b_ref, a_ref, d_ref, ds_ref, db_ref, da_ref, dpre_sc):
        lane = lax.broadcasted_iota(jnp.int32, (CUM_BLK, 128), 1)
        tri = _tri(CUM_BLK, True)
        carry = jnp.zeros((1, 128), F32)
        for blk in reversed(range(nb)):
            db = d_ref[blk * CUM_BLK:(blk + 1) * CUM_BLK, :]
            cb = _dot(tri, db, NN, HI) + carry
            carry = cb[0:1]
            dpre_sc[blk * CUM_BLK:(blk + 1) * CUM_BLK, :] = jnp.where(lane < 8, cb, db)
        _, vjp = jax.vjp(_prep_point, s_ref[...], b_ref[...], a_ref[...])
        ds, dbias, dalog = vjp(dpre_sc[...])
        ds_ref[...] = ds.astype(ds_ref.dtype)
        db_ref[...] = dbias
        da_ref[...] = dalog

    return pl.pallas_call(
        body, grid=(1,),
        in_specs=[pl.BlockSpec((t, 128), lambda i: (0, C_SMALL // 128)), pl.BlockSpec((1, 128), lambda i: (0, 0)),
                  pl.BlockSpec((1, 128), lambda i: (0, 0)), pl.BlockSpec((t, 128), lambda i: (0, 0))],
        out_specs=[pl.BlockSpec((t, 128), lambda i: (0, 0)), pl.BlockSpec((1, 128), lambda i: (0, 0)),
                   pl.BlockSpec((1, 128), lambda i: (0, 0))],
        out_shape=[jax.ShapeDtypeStruct((t, 128), BF16), jax.ShapeDtypeStruct((1, 128), F32), jax.ShapeDtypeStruct((1, 128), F32)],
        scratch_shapes=[pltpu.VMEM((t, 128), F32)], name=name,
    )(proj, bias_row, alog_row, daux)


ATT_TQ = 256
FOX_SCALE = FOX_DH ** -0.5


def _qnorm(q, g):
    return _rms(q, g) * FOX_SCALE


def _att_scores(qn_blk, kn, cfc_blk, cfr, qi, tq, kend):
    s = _dot(qn_blk.astype(BF16), kn[:kend].astype(BF16), NT) + cfc_blk - cfr[:, :kend]
    row = lax.broadcasted_iota(jnp.int32, (tq, kend), 0) + qi * tq
    col = lax.broadcasted_iota(jnp.int32, (tq, kend), 1)
    return s, row >= col


ATT_PAIR = 128 // FOX_DH


def _att_specs(t):
    pair = lambda off: pl.BlockSpec((t, 128), functools.partial(lambda i, o: (0, o + i), o=off // 128))
    gain = pl.BlockSpec((1, FOX_DH), lambda i: (0, 0))
    hd = lambda i: (i, 0, 0)
    col, row = pl.BlockSpec((ATT_PAIR, t, 1), hd), pl.BlockSpec((ATT_PAIR, 1, t), hd)
    return [pair(C_FQ), pair(C_FK), pair(C_FV), gain, gain, col, row], pl.BlockSpec((t, 128), lambda i: (0, i)), col, row


def _att_fwd(proj, qg, kg, cfc, cfr, name):
    t = proj.shape[0]
    tq = min(ATT_TQ, t)
    in_specs, pair_out, col, _ = _att_specs(t)

    def body(q_ref, k_ref, v_ref, qg_ref, kg_ref, cfc_ref, cfr_ref, o_ref, lse_ref):
        for e in range(ATT_PAIR):
            lanes = slice(e * FOX_DH, (e + 1) * FOX_DH)
            qn = _qnorm(q_ref[:, lanes], qg_ref[...])
            kn = _rms(k_ref[:, lanes], kg_ref[...])
            v = v_ref[:, lanes].astype(BF16)
            cfr = cfr_ref[e]
            for qi in range(t // tq):
                kend = (qi + 1) * tq
                rows = slice(qi * tq, kend)
                s, mask = _att_scores(qn[rows], kn, cfc_ref[e, rows, :], cfr, qi, tq, kend)
                s = jnp.where(mask, s, NEG)
                m = jnp.max(s, axis=1, keepdims=True)
                p = jnp.exp(s - m)
                l = jnp.sum(p, axis=1, keepdims=True)
                o_ref[rows, lanes] = _dot(p.astype(BF16), v[:kend], NN) / l
                lse_ref[e, rows, :] = m + jnp.log(l)

    return pl.pallas_call(
        body, grid=(FOX_HEADS // ATT_PAIR,), in_specs=in_specs, out_specs=[pair_out, col],
        out_shape=[jax.ShapeDtypeStruct((t, BW), F32), jax.ShapeDtypeStruct((FOX_HEADS, t, 1), F32)], name=name,
    )(proj, proj, proj, qg, kg, cfc, cfr)


def _att_bwd(proj, qg, kg, cfc, cfr, lse, o, do, name):
    t = proj.shape[0]
    tq = min(ATT_TQ, t)
    in_specs, pair_out, col, row = _att_specs(t)

    def body(q_ref, k_ref, v_ref, qg_ref, kg_ref, cfc_ref, cfr_ref, lse_ref, o_ref, do_ref,
             dq_ref, dk_ref, dv_ref, dcfc_ref, dcfr_ref, dqg_ref, dkg_ref, dqn_sc, dkn_sc, dv_sc, dcfr_sc):
        for e in range(ATT_PAIR):
            lanes = slice(e * FOX_DH, (e + 1) * FOX_DH)
            qn, vjp_q = jax.vjp(_qnorm, q_ref[:, lanes], qg_ref[...])
            kn, vjp_k = jax.vjp(_rms, k_ref[:, lanes], kg_ref[...])
            v = v_ref[:, lanes].astype(BF16)
            cfr = cfr_ref[e]
            do_e = do_ref[:, lanes]
            delta = jnp.sum(do_e * o_ref[:, lanes], axis=1, keepdims=True)
            dkn_sc[...] = jnp.zeros_like(dkn_sc)
            dv_sc[...] = jnp.zeros_like(dv_sc)
            dcfr_sc[...] = jnp.zeros_like(dcfr_sc)
            for qi in range(t // tq):
                kend = (qi + 1) * tq
                rows = slice(qi * tq, kend)
                s, mask = _att_scores(qn[rows], kn, cfc_ref[e, rows, :], cfr, qi, tq, kend)
                p = jnp.where(mask, jnp.exp(jnp.where(mask, s, NEG) - lse_ref[e, rows, :]), 0.0)
                do_b = do_e[rows].astype(BF16)
                dv_sc[0:kend, :] += _dot(p.astype(BF16), do_b, TN)
                dp = _dot(do_b, v[:kend], NT)
                ds = p * (dp - delta[rows])
                ds_b = ds.astype(BF16)
                dqn_sc[rows, :] = _dot(ds_b, kn[:kend].astype(BF16), NN)
                dkn_sc[0:kend, :] += _dot(ds_b, qn[rows].astype(BF16), TN)
                dcfc_ref[e, rows, :] = jnp.sum(ds, axis=1, keepdims=True)
                dcfr_sc[:, 0:kend] -= jnp.sum(ds, axis=0, keepdims=True)
            dq, dqg = vjp_q(dqn_sc[...])
            dk, dkg = vjp_k(dkn_sc[...])
            dq_ref[:, lanes] = dq.astype(dq_ref.dtype)
            dk_ref[:, lanes] = dk.astype(dk_ref.dtype)
            dv_ref[:, lanes] = dv_sc[...].astype(dv_ref.dtype)
            dcfr_ref[e] = dcfr_sc[...]
            dqg_ref[e] = dqg
            dkg_ref[e] = dkg

    gsp = pl.BlockSpec((ATT_PAIR, 1, FOX_DH), lambda i: (i, 0, 0))
    return pl.pallas_call(
        body, grid=(FOX_HEADS // ATT_PAIR,),
        in_specs=in_specs + [col, pair_out, pair_out],
        out_specs=[pair_out] * 3 + [col, row, gsp, gsp],
        out_shape=[jax.ShapeDtypeStruct((t, BW), BF16)] * 3
        + [jax.ShapeDtypeStruct((FOX_HEADS, t, 1), F32), jax.ShapeDtypeStruct((FOX_HEADS, 1, t), F32)]
        + [jax.ShapeDtypeStruct((FOX_HEADS, 1, FOX_DH), F32)] * 2,
        scratch_shapes=[pltpu.VMEM((t, FOX_DH), F32)] * 3 + [pltpu.VMEM((1, t), F32)], name=name,
    )(proj, proj, proj, qg, kg, cfc, cfr, lse, o, do)


DN_SCALE = DN_DH ** -0.5


DN_BATCH = 8


def _mm3(a, b, dims, batch=False):
    if batch:
        dn = (((dims[0][0] + 1,), (dims[1][0] + 1,)), ((0,), (0,)))
        dot = lambda p, q: lax.dot_general(p, q, dn, preferred_element_type=F32)
    else:
        dot = lambda p, q: _dot(p, q, dims)
    ah, bh = a.astype(BF16), b.astype(BF16)
    al, bl = (a - ah.astype(F32)).astype(BF16), (b - bh.astype(F32)).astype(BF16)
    return dot(ah, bh) + (dot(ah, bl) + dot(al, bh))


def _dn_local(qc, kc, vc, g, beta):
    nb, c, _ = qc.shape
    ii = lax.broadcasted_iota(jnp.int32, (c, c), 0)
    jj = lax.broadcasted_iota(jnp.int32, (c, c), 1)
    incl, strict = ii >= jj, ii > jj
    lower = jnp.broadcast_to(incl.astype(F32), (nb, c, c))
    eye = (ii == jj).astype(F32)
    mm = functools.partial(_mm3, batch=True)
    dm = mm(lower, jnp.where(strict, g, 0.0), NN)
    decay = jnp.where(incl, jnp.exp(jnp.where(incl, dm, 0.0)), 0.0)
    gcum = mm(lower, g * jnp.ones((1, 1, DN_DH), F32), NN)
    eg = jnp.exp(gcum)
    glast = gcum[:, c - 1:c]
    kb = kc * beta
    n1 = jnp.where(strict, mm(kb, kc, NT) * decay, 0.0)
    inv = eye - n1
    pw = n1
    for _ in range(5):
        pw = mm(pw, pw, NN)
        inv = inv + mm(pw, inv, NN)
    sol = mm(inv, jnp.concatenate([vc * beta, kb * eg], axis=2), NN)
    qk = jnp.where(incl, mm(qc, kc, NT) * decay, 0.0)
    return sol[:, :, :DN_DH], sol[:, :, DN_DH:], qk, qc * eg, kc * jnp.exp(glast - gcum), jnp.exp(glast)


def _dn_state(u, kcum, qk, qdec, kdec, egl, state):
    v_new = u - _mm3(kcum, state, NN)
    out = _mm3(qdec, state, NN) + _mm3(qk, v_new, NN)
    return out, state * egl + _mm3(kdec, v_new, TN)


def _dn_pre_q(c):
    return _l2(_silu(c)) * DN_SCALE


def _dn_pre_k(c):
    return _l2(_silu(c))


def _dn_post(o, z, ng):
    return _rms(o, ng) * _silu(z)


def _dn_specs(t):
    cblk = lambda o: pl.BlockSpec((t, DN_DH), functools.partial(lambda h, o: (0, o + h), o=o // DN_DH))
    wblk = lambda o: pl.BlockSpec((4, DN_DH), functools.partial(lambda h, o: (0, o + h), o=o // DN_DH))
    proj_specs = [cblk(C_DQ), cblk(C_DK), cblk(C_DV), cblk(C_DZ)]
    w_specs = [wblk(0), wblk(BW), wblk(2 * BW)]
    gb_spec = pl.BlockSpec((1, t, 2), lambda h: (h, 0, 0))
    return proj_specs, w_specs, gb_spec


def _chunk_rows(n, count=1):
    return pl.ds(pl.multiple_of(n * DN_CHUNK, DN_CHUNK), count * DN_CHUNK)


def _egl_rows(n, count=1):
    return pl.ds(pl.multiple_of(n * 8, 8), count * 8)


def _dn_local_inputs(n, qn_sc, kn_sc, vv_sc, gb_ref):
    r = _chunk_rows(n, DN_BATCH)
    split = lambda v: v.reshape(DN_BATCH, DN_CHUNK, v.shape[-1])
    gbv = split(gb_ref[0, r, :])
    return split(qn_sc[r, :]), split(kn_sc[r, :]), split(vv_sc[r, :]), gbv[:, :, 0:1], gbv[:, :, 1:2]


def _dn_local_phase(nc, qn_sc, kn_sc, vv_sc, gb_ref, loc):
    def step(i, carry):
        n = i * DN_BATCH
        vals = _dn_local(*_dn_local_inputs(n, qn_sc, kn_sc, vv_sc, gb_ref))
        for sc, val in zip(loc[:5], vals[:5]):
            sc[_chunk_rows(n, DN_BATCH), :] = val.reshape(DN_BATCH * DN_CHUNK, val.shape[-1])
        loc[5][_egl_rows(n, DN_BATCH), :] = jnp.broadcast_to(vals[5], (DN_BATCH, 8, DN_DH)).reshape(DN_BATCH * 8, DN_DH)
        return carry

    lax.fori_loop(0, nc // DN_BATCH, step, 0)


def _dn_loc_scratch(t, nc):
    big = pltpu.VMEM((t, DN_DH), F32)
    return [big, big, pltpu.VMEM((t, DN_CHUNK), F32), big, big, pltpu.VMEM((nc * 8, DN_DH), F32)]


def _dn_fwd(proj, conv_w, gb, ng, name):
    t = proj.shape[0]
    nc = t // DN_CHUNK
    assert nc % DN_BATCH == 0
    proj_specs, w_specs, gb_spec = _dn_specs(t)

    def body(q_ref, k_ref, v_ref, z_ref, wq_ref, wk_ref, wv_ref, gb_ref, ng_ref, y_ref, o_ref, st_ref, qn_sc, kn_sc, vv_sc, *loc):
        qn_sc[...] = _dn_pre_q(_conv(q_ref[...], wq_ref[...]))
        kn_sc[...] = _dn_pre_k(_conv(k_ref[...], wk_ref[...]))
        vv_sc[...] = _silu(_conv(v_ref[...], wv_ref[...]))
        _dn_local_phase(nc, qn_sc, kn_sc, vv_sc, gb_ref, loc)
        u_sc, kcum_sc, qk_sc, qdec_sc, kdec_sc, egl_sc = loc

        def chunk(n, state):
            r = _chunk_rows(n)
            egl = egl_sc[_egl_rows(n), :][0:1]
            out, new_state = _dn_state(u_sc[r, :], kcum_sc[r, :], qk_sc[r, :], qdec_sc[r, :], kdec_sc[r, :], egl, state)
            st_ref[0, n] = state
            o_ref[r, :] = out
            return new_state

        lax.fori_loop(0, nc, chunk, jnp.zeros((DN_DH, DN_DH), F32))
        y_ref[...] = _dn_post(o_ref[...], z_ref[...], ng_ref[...])

    hblk = pl.BlockSpec((t, DN_DH), lambda h: (0, h))
    return pl.pallas_call(
        body, grid=(DN_HEADS,),
        in_specs=proj_specs + w_specs + [gb_spec, pl.BlockSpec((1, DN_DH), lambda h: (0, 0))],
        out_specs=[hblk, hblk, pl.BlockSpec((1, nc, DN_DH, DN_DH), lambda h: (h, 0, 0, 0))],
        out_shape=[jax.ShapeDtypeStruct((t, BW), F32), jax.ShapeDtypeStruct((t, BW), F32),
                   jax.ShapeDtypeStruct((DN_HEADS, nc, DN_DH, DN_DH), F32)],
        scratch_shapes=[pltpu.VMEM((t, DN_DH), F32)] * 3 + _dn_loc_scratch(t, nc), name=name,
    )(proj, proj, proj, proj, conv_w, conv_w, conv_w, gb, ng)


def _dn_bwd(proj, conv_w, gb, ng, o, states, dy, name):
    t = proj.shape[0]
    nc = t // DN_CHUNK
    proj_specs, w_specs, gb_spec = _dn_specs(t)
    nloc = 6

    def body(q_ref, k_ref, v_ref, z_ref, wq_ref, wk_ref, wv_ref, gb_ref, ng_ref, o_ref, st_ref, dy_ref,
             dq_ref, dk_ref, dv_ref, dz_ref, dwq_ref, dwk_ref, dwv_ref, dgb_ref, dng_ref,
             qn_sc, kn_sc, vv_sc, do_sc, *rest):
        loc, dloc = rest[:nloc], rest[nloc:]
        qn_sc[...] = _dn_pre_q(_conv(q_ref[...], wq_ref[...]))
        kn_sc[...] = _dn_pre_k(_conv(k_ref[...], wk_ref[...]))
        vv_sc[...] = _silu(_conv(v_ref[...], wv_ref[...]))
        _, vjp_y = jax.vjp(_dn_post, o_ref[...], z_ref[...], ng_ref[...])
        do, dz, dng = vjp_y(dy_ref[...])
        do_sc[...] = do
        dz_ref[...] = dz.astype(dz_ref.dtype)
        dng_ref[0] = dng
        _dn_local_phase(nc, qn_sc, kn_sc, vv_sc, gb_ref, loc)
        u_sc, kcum_sc, qk_sc, qdec_sc, kdec_sc, egl_sc = loc

        def state_bwd(i, dstate):
            n = nc - 1 - i
            r = _chunk_rows(n)
            r8 = _egl_rows(n)
            _, vjp = jax.vjp(_dn_state, u_sc[r, :], kcum_sc[r, :], qk_sc[r, :], qdec_sc[r, :], kdec_sc[r, :],
                             egl_sc[r8, :][0:1], st_ref[0, n])
            du, dkcum, dqk, dqdec, dkdec, degl, dprev = vjp((do_sc[r, :], dstate))
            for d_sc, val in zip(dloc[:5], (du, dkcum, dqk, dqdec, dkdec)):
                d_sc[r, :] = val
            dloc[5][r8, :] = jnp.broadcast_to(degl, (8, DN_DH))
            return dprev

        lax.fori_loop(0, nc, state_bwd, jnp.zeros((DN_DH, DN_DH), F32))

        def local_bwd(i, carry):
            n = i * DN_BATCH
            r = _chunk_rows(n, DN_BATCH)
            _, vjp = jax.vjp(_dn_local, *_dn_local_inputs(n, qn_sc, kn_sc, vv_sc, gb_ref))
            cts = tuple(d_sc[r, :].reshape(DN_BATCH, DN_CHUNK, d_sc.shape[-1]) for d_sc in dloc[:5])
            cts += (dloc[5][_egl_rows(n, DN_BATCH), :].reshape(DN_BATCH, 8, DN_DH)[:, 0:1],)
            dqc, dkc, dvc, dg, dbeta = vjp(cts)
            for d_sc, val in zip((dloc[0], dloc[1], dloc[3]), (dqc, dkc, dvc)):
                d_sc[r, :] = val.reshape(DN_BATCH * DN_CHUNK, DN_DH)
            dgb_ref[0, r, :] = jnp.concatenate([dg, dbeta], axis=2).reshape(DN_BATCH * DN_CHUNK, 2)
            return carry

        lax.fori_loop(0, nc // DN_BATCH, local_bwd, 0)
        for x_ref, w_ref, pre, d_sc, dx_ref, dw_ref in ((q_ref, wq_ref, _dn_pre_q, dloc[0], dq_ref, dwq_ref),
                                                       (k_ref, wk_ref, _dn_pre_k, dloc[1], dk_ref, dwk_ref),
                                                       (v_ref, wv_ref, _silu, dloc[3], dv_ref, dwv_ref)):
            _, vjp = jax.vjp(pre, _conv(x_ref[...], w_ref[...]))
            (dc,) = vjp(d_sc[...])
            dx, dws = _conv_bwd(x_ref[...], w_ref[...], dc)
            dx_ref[...] = dx.astype(dx_ref.dtype)
            for j, dw in enumerate(dws):
                dw_ref[j:j + 1, :] = dw

    hblk = pl.BlockSpec((t, DN_DH), lambda h: (0, h))
    wout = pl.BlockSpec((4, DN_DH), lambda h: (0, h))
    return pl.pallas_call(
        body, grid=(DN_HEADS,),
        in_specs=proj_specs + w_specs + [gb_spec, pl.BlockSpec((1, DN_DH), lambda h: (0, 0)), hblk,
                                         pl.BlockSpec((1, nc, DN_DH, DN_DH), lambda h: (h, 0, 0, 0)), hblk],
        out_specs=[hblk] * 4 + [wout] * 3 + [gb_spec, pl.BlockSpec((1, 1, DN_DH), lambda h: (h, 0, 0))],
        out_shape=[jax.ShapeDtypeStruct((t, BW), BF16)] * 4 + [jax.ShapeDtypeStruct((4, BW), F32)] * 3
        + [jax.ShapeDtypeStruct((DN_HEADS, t, 2), F32), jax.ShapeDtypeStruct((DN_HEADS, 1, DN_DH), F32)],
        scratch_shapes=[pltpu.VMEM((t, DN_DH), F32)] * 4 + _dn_loc_scratch(t, nc) * 2, name=name,
    )(proj, proj, proj, proj, conv_w, conv_w, conv_w, gb, ng, o, states, dy)


MERGE_TM, MERGE_TN = 512, 512


def _merge_specs(t):
    tm, tn = min(MERGE_TM, t), MERGE_TN
    y_spec = pl.BlockSpec((tm, BW), lambda i, j: (i, 0))
    w_spec = pl.BlockSpec((3, BW, tn), lambda i, j: (0, 0, j))
    gate_specs = [pl.BlockSpec((tm, tn), functools.partial(lambda i, j, o: (i, o + j), o=(C_GATE + n * D_MODEL) // tn))
                  for n in range(3)]
    return tm, tn, [y_spec] * 3 + [w_spec] + gate_specs


def _merge_fwd(ys, wb, proj, name):
    t = proj.shape[0]
    tm, tn, in_specs = _merge_specs(t)

    def body(y0, y1, y2, w_ref, g0, g1, g2, o_ref):
        acc = jnp.zeros((tm, tn), F32)
        for n, (y, g) in enumerate(((y0, g0), (y1, g1), (y2, g2))):
            acc = acc + _dot(y[...].astype(BF16), w_ref[n], NN) * _sigmoid(g[...])
        o_ref[...] = acc.astype(o_ref.dtype)

    return pl.pallas_call(
        body, grid=(t // tm, D_MODEL // tn), in_specs=in_specs,
        out_specs=pl.BlockSpec((tm, tn), lambda i, j: (i, j)),
        out_shape=jax.ShapeDtypeStruct((t, D_MODEL), BF16), name=name,
    )(*ys, wb, proj, proj, proj)


def _merge_bwd(ys, wb, proj, dmerged, name):
    t = proj.shape[0]
    tm, tn, in_specs = _merge_specs(t)

    def body(y0, y1, y2, w_ref, g0, g1, g2, dm_ref, dg_ref, dt_ref):
        dm = dm_ref[...]
        for n, (y, g) in enumerate(((y0, g0), (y1, g1), (y2, g2))):
            tn_ = _dot(y[...].astype(BF16), w_ref[n], NN)
            sg = _sigmoid(g[...])
            dg_ref[n] = (dm * tn_ * sg * (1.0 - sg)).astype(dg_ref.dtype)
            dt_ref[n] = (dm * sg).astype(dt_ref.dtype)

    o3 = pl.BlockSpec((3, tm, tn), lambda i, j: (0, i, j))
    return pl.pallas_call(
        body, grid=(t // tm, D_MODEL // tn), in_specs=in_specs + [pl.BlockSpec((tm, tn), lambda i, j: (i, j))],
        out_specs=[o3, o3], out_shape=[jax.ShapeDtypeStruct((3, t, D_MODEL), BF16)] * 2, name=name,
    )(*ys, wb, proj, proj, proj, dmerged)


def _prep_rows(sp):
    z4, z112 = jnp.zeros((1, 4), F32), jnp.zeros((1, 112), F32)
    bias_row = jnp.concatenate([sp['b_fox_f'], z4, sp['dn_dt_bias'], z112], axis=1)
    alog_row = jnp.concatenate([jnp.zeros((1, 12), F32), sp['dn_a_log'], z112], axis=1)
    return bias_row, alog_row


def _sc_fwd_tile(sb, sc, sv, w):
    return (sb * _conv(sc * sv, w),)


def _ffn_act(ug, uv):
    return _silu(ug) * uv


W_IN_COLS = [(C_FQ, 0, 1536), (C_SB, 1544, 1536), (C_DQ, 3080, 1536), (C_DZ, 4624, 512), (C_GATE, 5136, 3072),
             (C_SMALL, 1536, 8), (C_SMALL + 8, 4616, 8)]
W_IN_SHARD = IN_ORIG // N_DEV


def _w_in_segments():
    out = []
    for d, o, w in W_IN_COLS:
        end = o + w
        while o < end:
            k = o // W_IN_SHARD
            n = min(end, (k + 1) * W_IN_SHARD) - o
            out.append((d, k, o - k * W_IN_SHARD, n))
            o, d = o + n, d + n
    return out


def _w_in_assemble(g, name):
    tm = 256

    def body(g_ref, o_ref):
        for d, k, s, n in _w_in_segments():
            o_ref[:, d:d + n] = g_ref[k, :, s:s + n]
        o_ref[:, IN_ORIG:IN_P] = jnp.zeros((tm, IN_P - IN_ORIG), o_ref.dtype)

    return pl.pallas_call(
        body, grid=(D_MODEL // tm,),
        in_specs=[pl.BlockSpec((N_DEV, tm, W_IN_SHARD), lambda i: (0, i, 0))],
        out_specs=pl.BlockSpec((tm, IN_P), lambda i: (i, 0)),
        out_shape=jax.ShapeDtypeStruct((D_MODEL, IN_P), g.dtype), name=name,
    )(g)


W_IN_SPLIT_TM = 128


def _w_in_split_by_core(g, name):
    tm = W_IN_SPLIT_TM

    def scatter(g_ref, keep_ref, give_ref, core):
        for d, k, s, n in _w_in_segments():
            val = g_ref[:, d:d + n]
            if k % 2 == core:
                keep_ref[k // 2, :, s:s + n] = val
            else:
                give_ref[k // 2, :, s:s + n] = val.astype(give_ref.dtype)

    def body(g_ref, keep_ref, give_ref):
        c = lax.axis_index("c")
        for core in range(2):
            pl.when(c == core)(functools.partial(scatter, g_ref, keep_ref, give_ref, core))

    blk = pl.BlockSpec((4, tm, W_IN_SHARD), lambda i: (0, i, 0))
    shape = (4, D_MODEL, W_IN_SHARD)
    return pl.pallas_call(
        body, grid=(D_MODEL // tm,), in_specs=[pl.BlockSpec((tm, IN_P), lambda i: (i, 0))],
        out_specs=[blk, blk], out_shape=[jax.ShapeDtypeStruct(shape, F32), jax.ShapeDtypeStruct(shape, BF16)], name=name,
    )(g)


def _w_in_split_by_device(g, name):
    tm = W_IN_SPLIT_TM

    def body(g_ref, own_ref, send_ref):
        me = 4 * lax.axis_index("x") + 2 * lax.axis_index("y") + lax.axis_index("c")
        for d, k, s, n in _w_in_segments():
            val = g_ref[:, d:d + n]
            send_ref[k, :, s:s + n] = val.astype(send_ref.dtype)

            @pl.when(me == k)
            def _():
                own_ref[:, s:s + n] = val

    return pl.pallas_call(
        body, grid=(D_MODEL // tm,), in_specs=[pl.BlockSpec((tm, IN_P), lambda i: (i, 0))],
        out_specs=[pl.BlockSpec((tm, W_IN_SHARD), lambda i: (i, 0)), pl.BlockSpec((N_DEV, tm, W_IN_SHARD), lambda i: (0, i, 0))],
        out_shape=[jax.ShapeDtypeStruct((D_MODEL, W_IN_SHARD), F32), jax.ShapeDtypeStruct((N_DEV, D_MODEL, W_IN_SHARD), BF16)],
        name=name,
    )(g)


def _layer_fwd(x, p_i, w, sp, tag):
    t = x.shape[0]
    sv = {'x': x}
    (hn,) = _ew(lambda a, g: (_rms(a, g),), [x], [sp['g_mix']], [(D_MODEL, BF16)], name=f'rms_mix_{tag}')
    proj = _mm(hn, w['w_in'], 'nn', [F32], name=f'in_proj_{tag}')
    bias_row, alog_row = _prep_rows(sp)
    aux = _prep_fwd(proj, bias_row, alog_row, f'prep_{tag}')
    cf = aux[:, :FOX_HEADS].T
    cfc, cfr = cf[:, :, None], cf[:, None, :]
    y_fox, lse = _att_fwd(proj, sp['fox_q_gain'], sp['fox_k_gain'], cfc, cfr, f'fox_fwd_{tag}')
    (y_sc,) = _cb(_sc_fwd_tile, [(proj, C_SB), (proj, C_SC), (proj, C_SV)], [(w['sc_conv_w'], 0)], [F32], [],
                  tc=256, nblk=2, name=f'sc_fwd_{tag}')
    gb = jnp.stack([aux[:, 12:16].T, aux[:, 8:12].T], axis=-1)
    y_dn, o_dn, states = _dn_fwd(proj, w['dn_conv_w'], gb, sp['dn_norm_gain'], f'dn_fwd_{tag}')
    ys = (y_fox, y_sc, y_dn)
    merged = _merge_fwd(ys, w['w_branch'], proj, f'merge_fwd_{tag}')
    x1 = _mm(merged, w['w_o'], 'nn', [F32], epi=lambda acc, r: (acc + r,), epi_args=(x,), name=f'o_proj_{tag}')
    (hf,) = _ew(lambda a, g: (_rms(a, g),), [x1], [sp['g_ffn']], [(D_MODEL, BF16)], name=f'rms_ffn_{tag}')
    up = _mm(hf, w['w_up'], 'nn', [F32], name=f'up_proj_{tag}')
    (act,) = _cb(lambda ug, uv, wg, wv: (_ffn_act(_conv(ug, wg), _conv(uv, wv)),), [(up, 0), (up, D_FF)],
                 [(w['ffn_conv_w'], 0), (w['ffn_conv_w'], D_FF)], [BF16], [], tc=256, nblk=D_FF // 256, name=f'ffn_act_{tag}')
    x2 = _mm(act, w['w_down'], 'nn', [F32], epi=lambda acc, r: (acc + r,), epi_args=(x1,), name=f'down_proj_{tag}')
    (hp,) = _ew(lambda a, g: (_rms(a, g),), [x2], [sp['g_ple']], [(D_MODEL, BF16)], name=f'rms_ple_{tag}')
    gp = _mm(hp, w['w_ple_gate'], 'nn', [F32], name=f'ple_gate_{tag}')
    x3 = _mm(p_i, w['w_ple'], 'nn', [F32], epi=lambda acc, g, r: (r + _sigmoid(g) * acc,), epi_args=(gp, x2), name=f'ple_{tag}')
    sv.update(hn=hn, proj=proj, aux=aux, cfc=cfc, cfr=cfr, lse=lse, ys=ys, gb=gb, o_dn=o_dn,
              states=states, merged=merged, x1=x1, hf=hf, up=up, act=act, x2=x2, hp=hp, gp=gp, p=p_i,
              bias_row=bias_row, alog_row=alog_row)
    return x3, sv


def _rms_bwd(x, g, dh, dres, name):
    def fn(xv, dhv, dr, gv):
        _, vjp = jax.vjp(_rms, xv, gv)
        dx, dg = vjp(dhv)
        return dr + dx, dg
    return _ew(fn, [x, dh, dres], [g], [(D_MODEL, F32)], [(1, D_MODEL)], name=name)


def _layer_bwd(dx3, sv, w, sp, tag):
    t = dx3.shape[0]
    g = {}
    def ple_epi(acc, gpv, d):
        s = _sigmoid(gpv)
        return d * acc * s * (1.0 - s), d * s
    dgp, de = _mm(sv['p'], w['w_ple'], 'nn', [BF16, BF16], epi=ple_epi, epi_args=(sv['gp'], dx3), name=f'ple_bwd_{tag}')
    g['w_ple'] = _mm(sv['p'], de, 'tn', [F32], name=f'd_w_ple_{tag}')
    g['w_ple_gate'] = _mm(sv['hp'], dgp, 'tn', [F32], name=f'd_w_ple_gate_{tag}')
    dhp = _mm(dgp, w['w_ple_gate'], 'nt', [F32], name=f'd_hp_{tag}')
    dx2, g['g_ple'] = _rms_bwd(sv['x2'], sp['g_ple'], dhp, dx3, f'rms_ple_bwd_{tag}')
    dact = _mm(dx2, w['w_down'], 'nt', [F32], name=f'd_act_{tag}')
    g['w_down'] = _mm(sv['act'], dx2, 'tn', [F32], name=f'd_w_down_{tag}')

    def ffn_bwd_tile(ug, uv, da, wg, wv):
        cg, cv = _conv(ug, wg), _conv(uv, wv)
        _, vjp = jax.vjp(_ffn_act, cg, cv)
        dcg, dcv = vjp(da)
        dug, dwg = _conv_bwd(ug, wg, dcg)
        duv, dwv = _conv_bwd(uv, wv, dcv)
        return dug, duv, jnp.concatenate(dwg, axis=0), jnp.concatenate(dwv, axis=0)
    dupg, dupv, dwg, dwv = _cb(ffn_bwd_tile, [(sv['up'], 0), (sv['up'], D_FF), (dact, 0)],
                               [(w['ffn_conv_w'], 0), (w['ffn_conv_w'], D_FF)], [BF16, BF16], [3, 3], tc=256,
                               nblk=D_FF // 256, name=f'ffn_act_bwd_{tag}')
    dup = jnp.concatenate([dupg, dupv], axis=1)
    g['ffn_conv_w'] = jnp.concatenate([dwg, dwv], axis=1)
    g['w_up'] = _mm(sv['hf'], dup, 'tn', [F32], name=f'd_w_up_{tag}')
    dhf = _mm(dup, w['w_up'], 'nt', [F32], name=f'd_hf_{tag}')
    dx1, g['g_ffn'] = _rms_bwd(sv['x1'], sp['g_ffn'], dhf, dx2, f'rms_ffn_bwd_{tag}')
    dmerged = _mm(dx1, w['w_o'], 'nt', [F32], name=f'd_merged_{tag}')
    g['w_o'] = _mm(sv['merged'], dx1, 'tn', [F32], name=f'd_w_o_{tag}')
    dgate, dtn = _merge_bwd(sv['ys'], w['w_branch'], sv['proj'], dmerged, f'merge_bwd_{tag}')
    dys, dwb = [], []
    for n in range(3):
        dys.append(_mm(dtn[n], w['w_branch'][n], 'nt', [F32], name=f'd_y{n}_{tag}'))
        dwb.append(_mm(sv['ys'][n], dtn[n], 'tn', [F32], name=f'd_w_branch{n}_{tag}'))
    g['w_branch'] = jnp.stack(dwb)
    ddq, ddk, ddv, ddz, dwq, dwk, dwv_, dgb, dng = _dn_bwd(sv['proj'], w['dn_conv_w'], sv['gb'], sp['dn_norm_gain'],
                                                           sv['o_dn'], sv['states'], dys[2], f'dn_bwd_{tag}')
    g['dn_conv_w'] = jnp.concatenate([dwq, dwk, dwv_], axis=1)
    g['dn_norm_gain'] = jnp.sum(dng, axis=0)
    def sc_bwd_tile(sb, sc, svv, dy, wv):
        u = sc * svv
        dsb = dy * _conv(u, wv)
        du, dws = _conv_bwd(u, wv, dy * sb)
        return dsb, du * svv, du * sc, jnp.concatenate(dws, axis=0)
    dsb, dsc, dsv, g['sc_conv_w'] = _cb(sc_bwd_tile, [(sv['proj'], C_SB), (sv['proj'], C_SC), (sv['proj'], C_SV), (dys[1], 0)],
                                        [(w['sc_conv_w'], 0)], [BF16, BF16, BF16], [3], tc=256, nblk=2, name=f'sc_bwd_{tag}')
    dfq, dfk, dfv, dcfc, dcfr, dqg, dkg = _att_bwd(sv['proj'], sp['fox_q_gain'], sp['fox_k_gain'], sv['cfc'], sv['cfr'],
                                                   sv['lse'], sv['ys'][0], dys[0], f'fox_bwd_{tag}')
    g['fox_q_gain'] = jnp.sum(dqg, axis=0)
    g['fox_k_gain'] = jnp.sum(dkg, axis=0)
    dcf = (dcfc[:, :, 0] + dcfr[:, 0, :]).T
    daux = jnp.concatenate([dcf, dgb[:, :, 1].T, dgb[:, :, 0].T, jnp.zeros((t, 112), F32)], axis=1)
    dsmall, dbias, dalog = _prep_bwd(sv['proj'], sv['bias_row'], sv['alog_row'], daux, f'prep_bwd_{tag}')
    g['b_fox_f'] = dbias[:, 0:8]
    g['dn_dt_bias'] = dbias[:, 12:16]
    g['dn_a_log'] = dalog[:, 12:16]
    dproj = jnp.concatenate([dfq, dfk, dfv, dsb, dsc, dsv, ddq, ddk, ddv, ddz, dgate[0], dgate[1], dgate[2], dsmall], axis=1)
    g['w_in'] = _mm(sv['hn'], dproj, 'tn', [F32], name=f'd_w_in_{tag}')
    dhn = _mm(dproj, w['w_in'], 'nt', [F32], name=f'd_hn_{tag}')
    dx, g['g_mix'] = _rms_bwd(sv['x'], sp['g_mix'], dhn, dx1, f'rms_mix_bwd_{tag}')
    return dx, g


def _loss_bwd(y, target, name):
    inv = 1.0 / y.shape[1]

    def fn(yv, tv):
        err = yv - tv
        return err * inv, jnp.zeros((8, 128), F32) + 0.5 * inv * jnp.sum(err * err)
    return _ew(fn, [y, target], [], [(y.shape[1], F32)], [(8, 128)], name=name)


PACK_W = 1024
FULL_SHAPE = {'w_in': (D_MODEL, IN_ORIG), 'w_branch': (3, BW, D_MODEL), 'w_o': (D_MODEL, D_MODEL), 'w_up': (D_MODEL, 2 * D_FF),
              'w_down': (D_FF, D_MODEL), 'w_ple_gate': (D_MODEL, D_MODEL), 'w_ple': (PLE_DIM, D_MODEL),
              'sc_conv_w': (3, BW), 'dn_conv_w': (4, 3 * BW), 'ffn_conv_w': (3, 2 * D_FF)}
SMALL_SHAPE = {'g_mix': D_MODEL, 'b_fox_f': FOX_HEADS, 'fox_q_gain': FOX_DH, 'fox_k_gain': FOX_DH, 'dn_a_log': DN_HEADS,
               'dn_dt_bias': DN_HEADS, 'dn_norm_gain': DN_DH, 'g_ffn': D_MODEL, 'g_ple': D_MODEL}


def _shard_shape(name):
    s = list(FULL_SHAPE[name])
    s[SHARDED[name]] //= N_DEV
    return tuple(s)


def _full_from_gathered(g, name):
    sh, ax = _shard_shape(name), SHARDED[name]
    blocks = jnp.moveaxis(g, 0, ax)
    return blocks.reshape(sh[:ax] + (N_DEV * sh[ax],) + sh[ax + 1:])


def _by_dest(full, name):
    sh, ax = _shard_shape(name), SHARDED[name]
    return jnp.moveaxis(full.reshape(sh[:ax] + (N_DEV, sh[ax]) + sh[ax + 1:]), ax, 0)


def _flat_pack(arrs):
    flat = jnp.concatenate([a.reshape(-1).astype(F32) for a in arrs])
    rows = -(-flat.shape[0] // (8 * PACK_W)) * 8
    return jnp.pad(flat, (0, rows * PACK_W - flat.shape[0])).reshape(rows, PACK_W)


def _flat_unpack(pack, shapes):
    flat, out, off = pack.reshape(-1), [], 0
    for s in shapes:
        n = 1
        for d in s:
            n *= d
        out.append(flat[off:off + n].reshape(s))
        off += n
    return out


SMALL_SHAPES = [(DEPTH, SMALL_SHAPE[n]) for n in SMALL]
SMALL_LOSS_AT = sum(DEPTH * SMALL_SHAPE[n] for n in SMALL)
CONV_SHARD_SHAPES = [(DEPTH,) + _shard_shape(n) for n in CONVW]
CONV_FULL_SHAPES = [(DEPTH,) + FULL_SHAPE[n] for n in CONVW]


def _rows2d(a):
    return a.reshape(-1, a.shape[-1])


def _pick_tm(m, width):
    best = None
    for tm in range(16, m + 1, 16):
        if m % tm == 0 and tm * width * 4 <= (1 << 20):
            best = tm
    return best if best is not None else m


HBM_SPEC = pl.BlockSpec(memory_space=pltpu.HBM)


def _place():
    x, y, c = lax.axis_index("x"), lax.axis_index("y"), lax.axis_index("c")
    return x, y, c, [(1 - x, y), (x, 1 - y), (1 - x, 1 - y)]


def _all_gather(arrs, name):
    n = len(arrs)

    def body(*refs):
        ins, outs = refs[:n], refs[n:2 * n]
        send_sems, recv_sems, local_sems = refs[2 * n:]
        x, y, c, chips = _place()
        me, sibling = (x, y, c), (x, y, 1 - c)

        def block(a, p):
            return outs[a].at[4 * p[0] + 2 * p[1] + p[2]]

        def copy(a, k, blk, to, src=None):
            return pltpu.make_async_remote_copy(src_ref=block(a, blk) if src is None else src, dst_ref=block(a, blk),
                                                send_sem=send_sems.at[a, k], recv_sem=recv_sems.at[a, k],
                                                device_id=to, device_id_type=MESH)

        mine = [pltpu.make_async_copy(ins[a], block(a, me), local_sems.at[a]) for a in range(n)]
        first, passed = [], []
        for a in range(n):
            mine[a].start()
            first.append(copy(a, 0, me, sibling, src=ins[a]))
            first += [copy(a, 1 + j, me, (*chip, c), src=ins[a]) for j, chip in enumerate(chips)]
        for cp in first:
            cp.start()
        for j, chip in enumerate(chips):
            for a in range(n):
                copy(a, 1 + j, (*chip, c), me).wait_recv()
                fwd = copy(a, 4 + j, (*chip, c), sibling)
                fwd.start()
                passed.append(fwd)
        for a in range(n):
            copy(a, 0, sibling, me).wait_recv()
            for j, chip in enumerate(chips):
                copy(a, 4 + j, (*chip, 1 - c), me).wait_recv()
        for cp in first + passed:
            cp.wait_send()
        for a in range(n):
            mine[a].wait()

    return pl.pallas_call(
        body, in_specs=[HBM_SPEC] * n, out_specs=[HBM_SPEC] * n,
        out_shape=[jax.ShapeDtypeStruct((N_DEV,) + a.shape, a.dtype) for a in arrs],
        scratch_shapes=[pltpu.SemaphoreType.DMA((n, 7)), pltpu.SemaphoreType.DMA((n, 7)), pltpu.SemaphoreType.DMA((n,))],
        name=name,
    )(*arrs)


def _sibling_swap(arrs, name):
    n = len(arrs)

    def body(*refs):
        send_sems, recv_sems = refs[2 * n:]
        x, y, c, _ = _place()
        cps = [pltpu.make_async_remote_copy(src_ref=refs[a], dst_ref=refs[n + a], send_sem=send_sems.at[a], recv_sem=recv_sems.at[a],
                                            device_id=(x, y, 1 - c), device_id_type=MESH) for a in range(n)]
        for cp in cps:
            cp.start()
        for cp in cps:
            cp.wait()

    return pl.pallas_call(body, in_specs=[HBM_SPEC] * n, out_specs=[HBM_SPEC] * n,
                          out_shape=[jax.ShapeDtypeStruct(a.shape, a.dtype) for a in arrs],
                          scratch_shapes=[pltpu.SemaphoreType.DMA((n,)), pltpu.SemaphoreType.DMA((n,))], name=name)(*arrs)


def _chip_exchange(arrs, name):
    n = len(arrs)

    def body(*refs):
        send_sems, recv_sems = refs[2 * n:]
        x, y, c, chips = _place()
        cps = [pltpu.make_async_remote_copy(src_ref=refs[a].at[2 * cx + cy], dst_ref=refs[n + a].at[j], send_sem=send_sems.at[a, j],
                                            recv_sem=recv_sems.at[a, j], device_id=(cx, cy, c), device_id_type=MESH)
               for a in range(n) for j, (cx, cy) in enumerate(chips)]
        for cp in cps:
            cp.start()
        for cp in cps:
            cp.wait()

    return pl.pallas_call(body, in_specs=[HBM_SPEC] * n, out_specs=[HBM_SPEC] * n,
                          out_shape=[jax.ShapeDtypeStruct((3,) + a.shape[1:], a.dtype) for a in arrs],
                          scratch_shapes=[pltpu.SemaphoreType.DMA((n, 3)), pltpu.SemaphoreType.DMA((n, 3))], name=name)(*arrs)


ANY_SPEC = pl.BlockSpec(memory_space=pl.ANY)
SEM_SPEC = pl.BlockSpec(memory_space=pltpu.SEMAPHORE)
DATAFLOW = pltpu.SideEffectType.DATAFLOW_SIDE_EFFECTING
N_PEER = N_DEV - 1


def _direct_copies(src_refs, zone_refs, send_sems, recv_sems, per_dest):
    x, y, c, _ = _place()
    me = 4 * x + 2 * y + c
    cps = []
    for a, (src, zone) in enumerate(zip(src_refs, zone_refs)):
        for r in range(N_PEER):
            bits = r + 1
            px = 1 - x if bits & 4 else x
            py = 1 - y if bits & 2 else y
            pc = 1 - c if bits & 1 else c
            cps.append(pltpu.make_async_remote_copy(
                src_ref=src.at[4 * px + 2 * py + pc] if per_dest else src, dst_ref=zone.at[me],
                send_sem=send_sems.at[a * N_PEER + r], recv_sem=recv_sems.at[a * N_PEER + r],
                device_id=(px, py, pc), device_id_type=MESH))
    return cps


def _direct_start(srcs, per_dest, after, name):
    n = len(srcs)
    zones = [lax.empty((N_DEV,) + (s.shape[1:] if per_dest else s.shape), s.dtype) for s in srcs]

    def body(*refs):
        src_refs, zone_refs = refs[:n], refs[n:2 * n]
        send_sems, recv_sems = refs[2 * n + 1:2 * n + 3]
        for cp in _direct_copies(src_refs, zone_refs, send_sems, recv_sems, per_dest):
            cp.start()

    sems = pltpu.SemaphoreType.DMA((n * N_PEER,))
    res = pl.pallas_call(
        body, name=name, in_specs=[HBM_SPEC] * (2 * n) + [ANY_SPEC],
        out_shape=[sems, sems] + [pltpu.HBM(s.shape, s.dtype) for s in srcs] + [pltpu.HBM(z.shape, z.dtype) for z in zones],
        out_specs=[SEM_SPEC, SEM_SPEC] + [HBM_SPEC] * (2 * n),
        input_output_aliases={i: 2 + i for i in range(2 * n)},
        compiler_params=pltpu.CompilerParams(has_side_effects=DATAFLOW),
    )(*[pltpu.with_memory_space_constraint(s, pltpu.HBM) for s in srcs],
      *[pltpu.with_memory_space_constraint(z, pltpu.HBM) for z in zones], after)
    return res[0], res[1], list(res[2:2 + n]), list(res[2 + n:]), per_dest


def _direct_wait(started, after, name):
    send_sems, recv_sems, srcs, zones, per_dest = started
    n = len(srcs)

    def body(*refs):
        src_refs, zone_refs = refs[:n], refs[n:2 * n]
        for cp in _direct_copies(src_refs, zone_refs, refs[2 * n], refs[2 * n + 1], per_dest):
            cp.wait_send()
            cp.wait_recv()

    res = pl.pallas_call(
        body, name=name, in_specs=[HBM_SPEC] * (2 * n) + [SEM_SPEC, SEM_SPEC, ANY_SPEC],
        out_shape=[pltpu.HBM(s.shape, s.dtype) for s in srcs] + [pltpu.HBM(z.shape, z.dtype) for z in zones],
        out_specs=[HBM_SPEC] * (2 * n), input_output_aliases={i: i for i in range(2 * n)},
        compiler_params=pltpu.CompilerParams(has_side_effects=DATAFLOW),
    )(*srcs, *zones, send_sems, recv_sems, after)
    return list(res[n:])


def _adamw(w, g, m, v):
    m = ADAM_B1 * m + (1.0 - ADAM_B1) * g
    v = ADAM_B2 * v + (1.0 - ADAM_B2) * jnp.square(g)
    m_hat = m / (1.0 - ADAM_B1 ** ADAM_STEP)
    v_hat = v / (1.0 - ADAM_B2 ** ADAM_STEP)
    delta = -ADAM_LR * (m_hat / (jnp.sqrt(v_hat) + ADAM_EPS) + ADAM_WD * w)
    return delta, m, v


def _adamw_layers(w, m, v, own0, ex0, own1, zone1, name):
    depth, rows, c = w.shape
    tm = _pick_tm(rows, c)
    nt = rows // tm
    first = lambda l, i: i * (1 - l) + (nt - 1) * l
    second = lambda l, i: i * l

    def body(w_ref, m_ref, v_ref, own0_ref, ex0_ref, own1_ref, zone1_ref, g_ref, d_ref, m2_ref, v2_ref):
        def finish(g):
            delta, m2, v2 = _adamw(w_ref[...], g, m_ref[...], v_ref[...])
            g_ref[...], d_ref[...], m2_ref[...], v2_ref[...] = g, delta, m2, v2

        @pl.when(pl.program_id(0) == 0)
        def _():
            g = own0_ref[...]
            for j in range(ex0_ref.shape[0]):
                g = g + ex0_ref[j].astype(F32)
            finish(g)

        @pl.when(pl.program_id(0) == 1)
        def _():
            g = own1_ref[...]
            for d in range(zone1_ref.shape[0]):
                g = g + zone1_ref[d].astype(F32)
            finish(g)

    wspec = pl.BlockSpec((None, tm, c), lambda l, i: (l, i, 0))
    return pl.pallas_call(
        body, grid=(depth, nt),
        in_specs=[wspec] * 3 + [pl.BlockSpec((tm, c), lambda l, i: (first(l, i), 0)),
                                pl.BlockSpec((ex0.shape[0], tm, c), lambda l, i: (0, first(l, i), 0)),
                                pl.BlockSpec((tm, c), lambda l, i: (second(l, i), 0)),
                                pl.BlockSpec((zone1.shape[0], tm, c), lambda l, i: (0, second(l, i), 0))],
        out_specs=[wspec] * 4, out_shape=[jax.ShapeDtypeStruct(w.shape, F32)] * 4, name=name,
    )(w, m, v, own0, ex0, own1, zone1)


def _adamw_call(w, m, v, gparts, name):
    def fn(wv, mv, vv, *gs):
        g = gs[0].astype(F32)
        for gp in gs[1:]:
            g = g + gp.astype(F32)
        delta, m2, v2 = _adamw(wv, g, mv, vv)
        return g, delta, m2, v2
    rows, c = w.shape
    return _ew(fn, [w, m, v] + list(gparts), [], [(c, F32)] * 4, tm=_pick_tm(rows, c), name=name)


def kernel(x, p, g_mix, w_in, b_fox_f, fox_q_gain, fox_k_gain, sc_conv_w, dn_conv_w, dn_a_log, dn_dt_bias,
           dn_norm_gain, w_branch, w_o, g_ffn, w_up, ffn_conv_w, w_down, g_ple, w_ple_gate, w_ple, loss_target,
           m_g_mix, m_w_in, m_b_fox_f, m_fox_q_gain, m_fox_k_gain, m_sc_conv_w, m_dn_conv_w, m_dn_a_log,
           m_dn_dt_bias, m_dn_norm_gain, m_w_branch, m_w_o, m_g_ffn, m_w_up, m_ffn_conv_w, m_w_down, m_g_ple,
           m_w_ple_gate, m_w_ple, v_g_mix, v_w_in, v_b_fox_f, v_fox_q_gain, v_fox_k_gain, v_sc_conv_w, v_dn_conv_w,
           v_dn_a_log, v_dn_dt_bias, v_dn_norm_gain, v_w_branch, v_w_o, v_g_ffn, v_w_up, v_ffn_conv_w, v_w_down,
           v_g_ple, v_w_ple_gate, v_w_ple):
    return _step(x, p, g_mix, w_in, b_fox_f, fox_q_gain, fox_k_gain, sc_conv_w, dn_conv_w, dn_a_log, dn_dt_bias,
                 dn_norm_gain, w_branch, w_o, g_ffn, w_up, ffn_conv_w, w_down, g_ple, w_ple_gate, w_ple, loss_target,
                 m_g_mix, m_w_in, m_b_fox_f, m_fox_q_gain, m_fox_k_gain, m_sc_conv_w, m_dn_conv_w, m_dn_a_log,
                 m_dn_dt_bias, m_dn_norm_gain, m_w_branch, m_w_o, m_g_ffn, m_w_up, m_ffn_conv_w, m_w_down, m_g_ple,
                 m_w_ple_gate, m_w_ple, v_g_mix, v_w_in, v_b_fox_f, v_fox_q_gain, v_fox_k_gain, v_sc_conv_w,
                 v_dn_conv_w, v_dn_a_log, v_dn_dt_bias, v_dn_norm_gain, v_w_branch, v_w_o, v_g_ffn, v_w_up,
                 v_ffn_conv_w, v_w_down, v_g_ple, v_w_ple_gate, v_w_ple)


def _step(*args):
    names = ['x', 'p'] + WEIGHTS + ['loss_target'] + ['m_' + n for n in WEIGHTS] + ['v_' + n for n in WEIGHTS]
    assert len(args) == len(names)
    a = dict(zip(names, args))
    x, target = a['x'][0], a['loss_target'][0]
    p = a['p'][:, 0]
    _, _, c, _ = _place()
    chip = 2 * lax.axis_index("x") + lax.axis_index("y")

    dev = 4 * lax.axis_index("x") + 2 * lax.axis_index("y") + c

    def full_weights(gathered, conv_full, layer):
        w = {n: _full_from_gathered(g, n) for n, g in zip(BIG, gathered) if n != 'w_in'}
        w.update({n: conv_full[n][layer] for n in CONVW})
        w['w_in'] = _w_in_assemble(gathered[list(BIG).index('w_in')], f'w_in_assemble_l{layer}')
        return w

    shards = [[a[n][layer].astype(BF16) for n in BIG] for layer in range(DEPTH)]
    gathered0 = _all_gather(shards[0] + [_flat_pack([a[n] for n in CONVW])], 'gather_weights_l0')
    conv_by_dev = [_flat_unpack(gathered0[-1][d], CONV_SHARD_SHAPES) for d in range(N_DEV)]
    conv_full = {n: jnp.concatenate([conv_by_dev[d][i] for d in range(N_DEV)], axis=2) for i, n in enumerate(CONVW)}
    w1_started = _direct_start(shards[1], False, gathered0[0], 'weights_l1_start')
    sps = [{n: a[n][layer][None, :] for n in SMALL} for layer in range(DEPTH)]
    ws = [full_weights(gathered0[:-1], conv_full, 0), None]

    h, sv0 = _layer_fwd(x, p[0], ws[0], sps[0], 'l0')
    zones = _direct_wait(w1_started, h, 'weights_l1_wait')
    ws[1] = full_weights([lax.dynamic_update_index_in_dim(z, s, dev, 0) for z, s in zip(zones, shards[1])], conv_full, 1)
    h, sv1 = _layer_fwd(h, p[1], ws[1], sps[1], 'l1')
    dh, loss_part = _loss_bwd(h, target, 'loss')
    grads = [None] * DEPTH
    dh, grads[1] = _layer_bwd(dh, sv1, ws[1], sps[1], 'l1')
    owns1, sends1 = [], []
    for n in BIG:
        if n == 'w_in':
            own, send = _w_in_split_by_device(grads[1][n], 'w_in_grad_split_l1')
        else:
            by_dest = _by_dest(grads[1][n], n)
            own, send = lax.dynamic_index_in_dim(by_dest, dev, axis=0, keepdims=False), by_dest.astype(BF16)
        owns1.append(own)
        sends1.append(send)
    g1_started = _direct_start(sends1, True, dh, 'grads_l1_start')
    dh, grads[0] = _layer_bwd(dh, sv0, ws[0], sps[0], 'l0')
    grad_x = dh[None]

    keeps, gives = [], []
    for n in BIG:
        if n == 'w_in':
            keep, give = _w_in_split_by_core(grads[0][n], 'w_in_grad_split_l0')
        else:
            by_dest = _by_dest(grads[0][n], n)
            by_dest = by_dest.reshape((4, 2) + by_dest.shape[1:])
            keep = lax.dynamic_index_in_dim(by_dest, c, axis=1, keepdims=False)
            give = lax.dynamic_index_in_dim(by_dest, 1 - c, axis=1, keepdims=False).astype(BF16)
        keeps.append(keep)
        gives.append(give)
    gots = _sibling_swap(gives, 'grad_sibling_swap')
    parts, parts_b = [], []
    for n, keep, got in zip(BIG, keeps, gots):
        k2, g2 = _rows2d(keep), _rows2d(got)
        part, part_b = _ew(lambda k, r: (k + r.astype(F32),) * 2, [k2, g2], [], [(k2.shape[1], F32), (k2.shape[1], BF16)],
                           tm=_pick_tm(*k2.shape), name=f'grad_chip_sum_{n}')
        parts.append(part.reshape(keep.shape))
        parts_b.append(part_b.reshape(keep.shape))
    others = _chip_exchange(parts_b, 'grad_chip_exchange')
    zones = _direct_wait(g1_started, others[0], 'grads_l1_wait')
    out = {}
    for n, part, other, own1, zone in zip(BIG, parts, others, owns1, zones):
        rc = (-1, a[n].shape[-1])
        own0 = lax.dynamic_index_in_dim(part, chip, axis=0, keepdims=False).reshape(rc)
        zone = lax.dynamic_update_index_in_dim(zone, jnp.zeros(zone.shape[1:], zone.dtype), dev, 0)
        view = lambda t: t.reshape((DEPTH,) + own0.shape)
        res = _adamw_layers(view(a[n]), view(a['m_' + n]), view(a['v_' + n]), own0, other.reshape((3,) + own0.shape),
                            own1.reshape(rc), zone.reshape((N_DEV,) + own0.shape), f'adamw_{n}')
        out[n] = [r.reshape(a[n].shape) for r in res]

    small_pack = _flat_pack([jnp.stack([grads[layer][n].reshape(-1) for layer in range(DEPTH)]) for n in SMALL] + [loss_part[0, 0]])
    conv_pack = _flat_pack([jnp.stack([grads[layer][n] for layer in range(DEPTH)]) for n in CONVW])
    small_all, conv_all = _all_gather([small_pack, conv_pack], 'gather_small_grads')
    res = _adamw_call(_flat_pack([a[n] for n in SMALL]), _flat_pack([a['m_' + n] for n in SMALL]),
                      _flat_pack([a['v_' + n] for n in SMALL]), [small_all[d] for d in range(N_DEV)], 'adamw_replicated')
    loss = res[0].reshape(-1)[SMALL_LOSS_AT]
    for k, r in enumerate(res):
        for n, val in zip(SMALL, _flat_unpack(r, SMALL_SHAPES)):
            out.setdefault(n, [None] * 4)[k] = val
    (conv_sum,) = _ew(lambda *gs: (functools.reduce(lambda s, t: s + t, gs),), [conv_all[d] for d in range(N_DEV)], [],
                      [(PACK_W, F32)], tm=conv_pack.shape[0], name='conv_grad_sum')
    conv_own = [lax.dynamic_slice_in_dim(g, dev * (g.shape[2] // N_DEV), g.shape[2] // N_DEV, axis=2)
                for g in _flat_unpack(conv_sum, CONV_FULL_SHAPES)]
    res = _adamw_call(_flat_pack([a[n] for n in CONVW]), _flat_pack([a['m_' + n] for n in CONVW]),
                      _flat_pack([a['v_' + n] for n in CONVW]), [_flat_pack(conv_own)], 'adamw_conv')
    for k, r in enumerate(res):
        for n, val in zip(CONVW, _flat_unpack(r, CONV_SHARD_SHAPES)):
            out.setdefault(n, [None] * 4)[k] = val

    outs = [loss, grad_x]
    for k in range(4):
        outs += [out[n][k] for n in WEIGHTS]
    return tuple(outs)
```

```python
import functools

import jax
import jax.numpy as jnp
from jax import lax
from jax.experimental import pallas as pl
from jax.experimental.pallas import tpu as pltpu

F32 = jnp.float32
BF16 = jnp.bfloat16
HI = lax.Precision.HIGHEST

D_MODEL = 1024
DEPTH = 2
N_DEV = 8
PLE_DIM = 256
BW = 512
FOX_HEADS, FOX_DH = 8, 64
DN_HEADS, DN_DH = 4, 128
DN_CHUNK = 64
D_FF = 2816
EPS = 1e-6
NEG = -1e30

ADAM_LR, ADAM_B1, ADAM_B2, ADAM_EPS, ADAM_WD, ADAM_STEP = 0.001, 0.9, 0.999, 1e-08, 0.01, 10

C_FQ, C_FK, C_FV = 0, 512, 1024
C_SB, C_SC, C_SV = 1536, 2048, 2560
C_DQ, C_DK, C_DV, C_DZ = 3072, 3584, 4096, 4608
C_GATE = 5120
C_SMALL = 8192
IN_P = 8320
IN_ORIG = 8208

WEIGHTS = ['g_mix', 'w_in', 'b_fox_f', 'fox_q_gain', 'fox_k_gain', 'sc_conv_w', 'dn_conv_w', 'dn_a_log',
           'dn_dt_bias', 'dn_norm_gain', 'w_branch', 'w_o', 'g_ffn', 'w_up', 'ffn_conv_w', 'w_down', 'g_ple',
           'w_ple_gate', 'w_ple']
BIG = {'w_in': 1, 'w_branch': 2, 'w_o': 0, 'w_up': 1, 'w_down': 0, 'w_ple_gate': 0, 'w_ple': 1}
CONVW = {'sc_conv_w': 1, 'dn_conv_w': 1, 'ffn_conv_w': 1}
SHARDED = {**BIG, **CONVW}
SMALL = [n for n in WEIGHTS if n not in SHARDED]
MESH = pl.DeviceIdType.MESH


def _sigmoid(x):
    return 0.5 * (jnp.tanh(0.5 * x) + 1.0)


def _silu(x):
    return x * _sigmoid(x)


def _log1pexp_negabs(z):
    return jnp.log(1.0 + jnp.exp(-jnp.abs(z)))


def _log_sigmoid(z):
    return jnp.minimum(z, 0.0) - _log1pexp_negabs(z)


def _softplus(z):
    return jnp.maximum(z, 0.0) + _log1pexp_negabs(z)


def _rms(x, g):
    return x * lax.rsqrt(jnp.mean(x * x, axis=-1, keepdims=True) + EPS) * g


def _l2(x):
    return x * lax.rsqrt(jnp.sum(x * x, axis=-1, keepdims=True) + EPS)


def _dot(a, b, dims, precision=None):
    return lax.dot_general(a, b, (dims, ((), ())), preferred_element_type=F32, precision=precision)


NN = ((1,), (0,))
NT = ((1,), (1,))
TN = ((0,), (0,))


def _shift_down(x, s):
    if s == 0:
        return x
    t = lax.broadcasted_iota(jnp.int32, x.shape, 0)
    return jnp.where(t >= s, pltpu.roll(x, s, 0), 0.0)


def _shift_up(x, s):
    if s == 0:
        return x
    n = x.shape[0]
    t = lax.broadcasted_iota(jnp.int32, x.shape, 0)
    return jnp.where(t < n - s, pltpu.roll(x, n - s, 0), 0.0)


def _conv(x, w):
    k = w.shape[0]
    y = w[k - 1:k] * x
    for j in range(k - 1):
        y = y + w[j:j + 1] * _shift_down(x, k - 1 - j)
    return y


def _conv_bwd(x, w, dy):
    k = w.shape[0]
    dx = w[k - 1:k] * dy
    dws = []
    for j in range(k - 1):
        dx = dx + w[j:j + 1] * _shift_up(dy, k - 1 - j)
        dws.append(jnp.sum(dy * _shift_down(x, k - 1 - j), axis=0, keepdims=True))
    dws.append(jnp.sum(dy * x, axis=0, keepdims=True))
    return dx, dws


MM_VMEM_BUDGET = 36 << 20
MM_STEP_BYTES = 1 << 20


def _mm_tiles(m, n, k, a_size, b_size, tile_size, cast_a):
    best = None
    for tm in [d for d in (2048, 1024, 512, 256, 128) if d <= m and m % d == 0] or [m]:
        for tn in [d for d in range(128, min(n, 2048) + 1, 128) if n % d == 0] or [n]:
            vmem = 2 * tm * k * a_size + (2 * tm * k if cast_a else 0) + 2 * k * tn * b_size + 2 * tm * tn * tile_size
            if vmem > MM_VMEM_BUDGET:
                continue
            steps = (m // tm) * (n // tn)
            cost = m * k * a_size + (m // tm) * k * n * b_size + m * n * tile_size + steps * MM_STEP_BYTES
            if best is None or cost < best[0]:
                best = (cost, tm, tn)
    assert best is not None, (m, n, k)
    return best[1], best[2]


def _mm(a, b, mode, outs, *, epi=None, epi_args=(), name):
    if mode == 'nn':
        (m, k), (k2, n) = a.shape, b.shape
    elif mode == 'nt':
        (m, k), (n, k2) = a.shape, b.shape
    else:
        (k, m), (k2, n) = a.shape, b.shape
    assert k == k2, (a.shape, b.shape, mode)
    tile_size = sum(jnp.dtype(dt).itemsize for dt in outs) + sum(e.dtype.itemsize for e in epi_args if e.shape[0] != 1)
    tm, tn = _mm_tiles(m, n, k, a.dtype.itemsize, b.dtype.itemsize, tile_size, a.dtype != BF16)
    dims = {'nn': NN, 'nt': NT, 'tn': TN}[mode]
    a_spec = pl.BlockSpec((k, tm), lambda i, j: (0, i)) if mode == 'tn' else pl.BlockSpec((tm, k), lambda i, j: (i, 0))
    b_spec = pl.BlockSpec((tn, k), lambda i, j: (j, 0)) if mode == 'nt' else pl.BlockSpec((k, tn), lambda i, j: (0, j))
    e_specs = [pl.BlockSpec((1, tn), lambda i, j: (0, j)) if e.shape[0] == 1 else pl.BlockSpec((tm, tn), lambda i, j: (i, j))
               for e in epi_args]
    ne, no = len(epi_args), len(outs)
    cast_a = a.dtype != BF16

    def body(a_ref, b_ref, *rest):
        if cast_a:
            a_sc = rest[-1]

            @pl.when(pl.program_id(1) == 0)
            def _():
                a_sc[...] = a_ref[...].astype(BF16)
            av = a_sc[...]
        else:
            av = a_ref[...]
        acc = _dot(av, b_ref[...].astype(BF16), dims)
        vals = epi(acc, *[e[...] for e in rest[:ne]]) if epi is not None else (acc,)
        for o_ref, v in zip(rest[ne:ne + no], vals):
            o_ref[...] = v.astype(o_ref.dtype)

    res = pl.pallas_call(
        body, grid=(m // tm, n // tn),
        in_specs=[a_spec, b_spec] + e_specs,
        out_specs=[pl.BlockSpec((tm, tn), lambda i, j: (i, j)) for _ in outs],
        out_shape=[jax.ShapeDtypeStruct((m, n), dt) for dt in outs],
        scratch_shapes=[pltpu.VMEM(a_spec.block_shape, BF16)] if cast_a else [],
        name=name,
    )(a, b, *epi_args)
    return res[0] if no == 1 else res


def _ew(fn, tiled, bcast, outs, reds=(), *, tm=256, name):
    secs = [(t, 0, t.shape[1]) if not isinstance(t, tuple) else t for t in tiled]
    m = secs[0][0].shape[0]
    tm = min(tm, m)
    assert m % tm == 0
    in_specs = []
    for arr, off, w in secs:
        assert off % w == 0
        in_specs.append(pl.BlockSpec((tm, w), functools.partial(lambda i, c: (i, c), c=off // w)))
    in_specs += [pl.BlockSpec(b.shape, lambda i: (0, 0)) for b in bcast]
    nin, no = len(in_specs), len(outs)

    def body(*refs):
        vals = fn(*[r[...] for r in refs[:nin]])
        for r, v in zip(refs[nin:nin + no], vals[:no]):
            r[...] = v.astype(r.dtype)
        i = pl.program_id(0)
        for r, v in zip(refs[nin + no:], vals[no:]):
            @pl.when(i == 0)
            def _():
                r[...] = v

            @pl.when(i > 0)
            def _():
                r[...] += v

    res = pl.pallas_call(
        body, grid=(m // tm,), in_specs=in_specs,
        out_specs=[pl.BlockSpec((tm, c), lambda i: (i, 0)) for c, _ in outs] + [pl.BlockSpec(s, lambda i: (0, 0)) for s in reds],
        out_shape=[jax.ShapeDtypeStruct((m, c), dt) for c, dt in outs] + [jax.ShapeDtypeStruct(s, F32) for s in reds],
        name=name,
    )(*[s[0] for s in secs], *bcast)
    return res


def _cb(fn, cols, params, outs, pouts, *, tc, nblk, name):
    t = cols[0][0].shape[0]
    in_specs = []
    for arr, off in cols:
        assert off % tc == 0
        in_specs.append(pl.BlockSpec((t, tc), functools.partial(lambda c, o: (0, o + c), o=off // tc)))
    for arr, off in params:
        in_specs.append(pl.BlockSpec((arr.shape[0], tc), functools.partial(lambda c, o: (0, o + c), o=off // tc)))
    nin, no = len(in_specs), len(outs)

    def body(*refs):
        vals = fn(*[r[...] for r in refs[:nin]])
        for r, v in zip(refs[nin:], vals):
            r[...] = v.astype(r.dtype)

    return pl.pallas_call(
        body, grid=(nblk,), in_specs=in_specs,
        out_specs=[pl.BlockSpec((t, tc), lambda c: (0, c)) for _ in outs] + [pl.BlockSpec((k, tc), lambda c: (0, c)) for k in pouts],
        out_shape=[jax.ShapeDtypeStruct((t, nblk * tc), dt) for dt in outs] + [jax.ShapeDtypeStruct((k, nblk * tc), F32) for k in pouts],
        name=name,
    )(*[c[0] for c in cols], *[p[0] for p in params])


CUM_BLK = 256


def _prep_point(s, bias, alog):
    lane = lax.broadcasted_iota(jnp.int32, s.shape, 1)
    z = s + bias
    return jnp.where(lane < 8, _log_sigmoid(z),
                     jnp.where(lane < 12, _sigmoid(s),
                               jnp.where(lane < 16, -jnp.exp(alog) * _softplus(z), 0.0)))


def _tri(n, upper):
    r = lax.broadcasted_iota(jnp.int32, (n, n), 0)
    c = lax.broadcasted_iota(jnp.int32, (n, n), 1)
    return (r <= c if upper else r >= c).astype(F32)


def _prep_fwd(proj, bias_row, alog_row, name):
    t = proj.shape[0]
    nb = t // CUM_BLK

    def body(s_ref, b_ref, a_ref, o_ref):
        pre = _prep_point(s_ref[...], b_ref[...], a_ref[...])
        lane = lax.broadcasted_iota(jnp.int32, (CUM_BLK, 128), 1)
        tri = _tri(CUM_BLK, False)
        carry = jnp.zeros((1, 128), F32)
        for blk in range(nb):
            xb = pre[blk * CUM_BLK:(blk + 1) * CUM_BLK]
            cb = _dot(tri, xb, NN, HI) + carry
            carry = cb[CUM_BLK - 1:CUM_BLK]
            o_ref[blk * CUM_BLK:(blk + 1) * CUM_BLK, :] = jnp.where(lane < 8, cb, xb)

    return pl.pallas_call(
        body, grid=(1,),
        in_specs=[pl.BlockSpec((t, 128), lambda i: (0, C_SMALL // 128)), pl.BlockSpec((1, 128), lambda i: (0, 0)),
                  pl.BlockSpec((1, 128), lambda i: (0, 0))],
        out_specs=pl.BlockSpec((t, 128), lambda i: (0, 0)),
        out_shape=jax.ShapeDtypeStruct((t, 128), F32), name=name,
    )(proj, bias_row, alog_row)


def _prep_bwd(proj, bias_row, alog_row, daux, name):
    t = proj.shape[0]
    nb = t // CUM_BLK

    def body(s_ref, b_ref, a_ref, d_ref, ds_ref, db_ref, da_ref, dpre_sc):
        lane = lax.broadcasted_iota(jnp.int32, (CUM_BLK, 128), 1)
        tri = _tri(CUM_BLK, True)
        carry = jnp.zeros((1, 128), F32)
        for blk in reversed(range(nb)):
            db = d_ref[blk * CUM_BLK:(blk + 1) * CUM_BLK, :]
            cb = _dot(tri, db, NN, HI) + carry
            carry = cb[0:1]
            dpre_sc[blk * CUM_BLK:(blk + 1) * CUM_BLK, :] = jnp.where(lane < 8, cb, db)
        _, vjp = jax.vjp(_prep_point, s_ref[...], b_ref[...], a_ref[...])
        ds, dbias, dalog = vjp(dpre_sc[...])
        ds_ref[...] = ds.astype(ds_ref.dtype)
        db_ref[...] = dbias
        da_ref[...] = dalog

    return pl.pallas_call(
        body, grid=(1,),
        in_specs=[pl.BlockSpec((t, 128), lambda i: (0, C_SMALL // 128)), pl.BlockSpec((1, 128), lambda i: (0, 0)),
                  pl.BlockSpec((1, 128), lambda i: (0, 0)), pl.BlockSpec((t, 128), lambda i: (0, 0))],
        out_specs=[pl.BlockSpec((t, 128), lambda i: (0, 0)), pl.BlockSpec((1, 128), lambda i: (0, 0)),
                   pl.BlockSpec((1, 128), lambda i: (0, 0))],
        out_shape=[jax.ShapeDtypeStruct((t, 128), BF16), jax.ShapeDtypeStruct((1, 128), F32), jax.ShapeDtypeStruct((1, 128), F32)],
        scratch_shapes=[pltpu.VMEM((t, 128), F32)], name=name,
    )(proj, bias_row, alog_row, daux)


ATT_TQ = 256
FOX_SCALE = FOX_DH ** -0.5


def _qnorm(q, g):
    return _rms(q, g) * FOX_SCALE


def _att_scores(qn_blk, kn, cfc_blk, cfr, qi, tq, kend):
    s = _dot(qn_blk.astype(BF16), kn[:kend].astype(BF16), NT) + cfc_blk - cfr[:, :kend]
    row = lax.broadcasted_iota(jnp.int32, (tq, kend), 0) + qi * tq
    col = lax.broadcasted_iota(jnp.int32, (tq, kend), 1)
    return s, row >= col


ATT_PAIR = 128 // FOX_DH


def _att_specs(t):
    pair = lambda off: pl.BlockSpec((t, 128), functools.partial(lambda i, o: (0, o + i), o=off // 128))
    gain = pl.BlockSpec((1, FOX_DH), lambda i: (0, 0))
    hd = lambda i: (i, 0, 0)
    col, row = pl.BlockSpec((ATT_PAIR, t, 1), hd), pl.BlockSpec((ATT_PAIR, 1, t), hd)
    return [pair(C_FQ), pair(C_FK), pair(C_FV), gain, gain, col, row], pl.BlockSpec((t, 128), lambda i: (0, i)), col, row


def _att_fwd(proj, qg, kg, cfc, cfr, name):
    t = proj.shape[0]
    tq = min(ATT_TQ, t)
    in_specs, pair_out, col, _ = _att_specs(t)

    def body(q_ref, k_ref, v_ref, qg_ref, kg_ref, cfc_ref, cfr_ref, o_ref, lse_ref):
        for e in range(ATT_PAIR):
            lanes = slice(e * FOX_DH, (e + 1) * FOX_DH)
            qn = _qnorm(q_ref[:, lanes], qg_ref[...])
            kn = _rms(k_ref[:, lanes], kg_ref[...])
            v = v_ref[:, lanes].astype(BF16)
            cfr = cfr_ref[e]
            for qi in range(t // tq):
                kend = (qi + 1) * tq
                rows = slice(qi * tq, kend)
                s, mask = _att_scores(qn[rows], kn, cfc_ref[e, rows, :], cfr, qi, tq, kend)
                s = jnp.where(mask, s, NEG)
                m = jnp.max(s, axis=1, keepdims=True)
                p = jnp.exp(s - m)
                l = jnp.sum(p, axis=1, keepdims=True)
                o_ref[rows, lanes] = _dot(p.astype(BF16), v[:kend], NN) / l
                lse_ref[e, rows, :] = m + jnp.log(l)

    return pl.pallas_call(
        body, grid=(FOX_HEADS // ATT_PAIR,), in_specs=in_specs, out_specs=[pair_out, col],
        out_shape=[jax.ShapeDtypeStruct((t, BW), F32), jax.ShapeDtypeStruct((FOX_HEADS, t, 1), F32)], name=name,
    )(proj, proj, proj, qg, kg, cfc, cfr)


def _att_bwd(proj, qg, kg, cfc, cfr, lse, o, do, name):
    t = proj.shape[0]
    tq = min(ATT_TQ, t)
    in_specs, pair_out, col, row = _att_specs(t)

    def body(q_ref, k_ref, v_ref, qg_ref, kg_ref, cfc_ref, cfr_ref, lse_ref, o_ref, do_ref,
             dq_ref, dk_ref, dv_ref, dcfc_ref, dcfr_ref, dqg_ref, dkg_ref, dqn_sc, dkn_sc, dv_sc, dcfr_sc):
        for e in range(ATT_PAIR):
            lanes = slice(e * FOX_DH, (e + 1) * FOX_DH)
            qn, vjp_q = jax.vjp(_qnorm, q_ref[:, lanes], qg_ref[...])
            kn, vjp_k = jax.vjp(_rms, k_ref[:, lanes], kg_ref[...])
            v = v_ref[:, lanes].astype(BF16)
            cfr = cfr_ref[e]
            do_e = do_ref[:, lanes]
            delta = jnp.sum(do_e * o_ref[:, lanes], axis=1, keepdims=True)
            dkn_sc[...] = jnp.zeros_like(dkn_sc)
            dv_sc[...] = jnp.zeros_like(dv_sc)
            dcfr_sc[...] = jnp.zeros_like(dcfr_sc)
            for qi in range(t // tq):
                kend = (qi + 1) * tq
                rows = slice(qi * tq, kend)
                s, mask = _att_scores(qn[rows], kn, cfc_ref[e, rows, :], cfr, qi, tq, kend)
                p = jnp.where(mask, jnp.exp(jnp.where(mask, s, NEG) - lse_ref[e, rows, :]), 0.0)
                do_b = do_e[rows].astype(BF16)
                dv_sc[0:kend, :] += _dot(p.astype(BF16), do_b, TN)
                dp = _dot(do_b, v[:kend], NT)
                ds = p * (dp - delta[rows])
                ds_b = ds.astype(BF16)
                dqn_sc[rows, :] = _dot(ds_b, kn[:kend].astype(BF16), NN)
                dkn_sc[0:kend, :] += _dot(ds_b, qn[rows].astype(BF16), TN)
                dcfc_ref[e, rows, :] = jnp.sum(ds, axis=1, keepdims=True)
                dcfr_sc[:, 0:kend] -= jnp.sum(ds, axis=0, keepdims=True)
            dq, dqg = vjp_q(dqn_sc[...])
            dk, dkg = vjp_k(dkn_sc[...])
            dq_ref[:, lanes] = dq.astype(dq_ref.dtype)
            dk_ref[:, lanes] = dk.astype(dk_ref.dtype)
            dv_ref[:, lanes] = dv_sc[...].astype(dv_ref.dtype)
            dcfr_ref[e] = dcfr_sc[...]
            dqg_ref[e] = dqg
            dkg_ref[e] = dkg

    gsp = pl.BlockSpec((ATT_PAIR, 1, FOX_DH), lambda i: (i, 0, 0))
    return pl.pallas_call(
        body, grid=(FOX_HEADS // ATT_PAIR,),
        in_specs=in_specs + [col, pair_out, pair_out],
        out_specs=[pair_out] * 3 + [col, row, gsp, gsp],
        out_shape=[jax.ShapeDtypeStruct((t, BW), BF16)] * 3
        + [jax.ShapeDtypeStruct((FOX_HEADS, t, 1), F32), jax.ShapeDtypeStruct((FOX_HEADS, 1, t), F32)]
        + [jax.ShapeDtypeStruct((FOX_HEADS, 1, FOX_DH), F32)] * 2,
        scratch_shapes=[pltpu.VMEM((t, FOX_DH), F32)] * 3 + [pltpu.VMEM((1, t), F32)], name=name,
    )(proj, proj, proj, qg, kg, cfc, cfr, lse, o, do)


DN_SCALE = DN_DH ** -0.5


DN_BATCH = 8


def _mm3(a, b, dims, batch=False):
    if batch:
        dn = (((dims[0][0] + 1,), (dims[1][0] + 1,)), ((0,), (0,)))
        dot = lambda p, q: lax.dot_general(p, q, dn, preferred_element_type=F32)
    else:
        dot = lambda p, q: _dot(p, q, dims)
    ah, bh = a.astype(BF16), b.astype(BF16)
    al, bl = (a - ah.astype(F32)).astype(BF16), (b - bh.astype(F32)).astype(BF16)
    return dot(ah, bh) + (dot(ah, bl) + dot(al, bh))


def _dn_local(qc, kc, vc, g, beta):
    nb, c, _ = qc.shape
    ii = lax.broadcasted_iota(jnp.int32, (c, c), 0)
    jj = lax.broadcasted_iota(jnp.int32, (c, c), 1)
    incl, strict = ii >= jj, ii > jj
    lower = jnp.broadcast_to(incl.astype(F32), (nb, c, c))
    eye = (ii == jj).astype(F32)
    mm = functools.partial(_mm3, batch=True)
    dm = mm(lower, jnp.where(strict, g, 0.0), NN)
    decay = jnp.where(incl, jnp.exp(jnp.where(incl, dm, 0.0)), 0.0)
    gcum = mm(lower, g * jnp.ones((1, 1, DN_DH), F32), NN)
    eg = jnp.exp(gcum)
    glast = gcum[:, c - 1:c]
    kb = kc * beta
    n1 = jnp.where(strict, mm(kb, kc, NT) * decay, 0.0)
    inv = eye - n1
    pw = n1
    for _ in range(5):
        pw = mm(pw, pw, NN)
        inv = inv + mm(pw, inv, NN)
    sol = mm(inv, jnp.concatenate([vc * beta, kb * eg], axis=2), NN)
    qk = jnp.where(incl, mm(qc, kc, NT) * decay, 0.0)
    return sol[:, :, :DN_DH], sol[:, :, DN_DH:], qk, qc * eg, kc * jnp.exp(glast - gcum), jnp.exp(glast)


def _dn_state(u, kcum, qk, qdec, kdec, egl, state):
    v_new = u - _mm3(kcum, state, NN)
    out = _mm3(qdec, state, NN) + _mm3(qk, v_new, NN)
    return out, state * egl + _mm3(kdec, v_new, TN)


def _dn_pre_q(c):
    return _l2(_silu(c)) * DN_SCALE


def _dn_pre_k(c):
    return _l2(_silu(c))


def _dn_post(o, z, ng):
    return _rms(o, ng) * _silu(z)


def _dn_specs(t):
    cblk = lambda o: pl.BlockSpec((t, DN_DH), functools.partial(lambda h, o: (0, o + h), o=o // DN_DH))
    wblk = lambda o: pl.BlockSpec((4, DN_DH), functools.partial(lambda h, o: (0, o + h), o=o // DN_DH))
    proj_specs = [cblk(C_DQ), cblk(C_DK), cblk(C_DV), cblk(C_DZ)]
    w_specs = [wblk(0), wblk(BW), wblk(2 * BW)]
    gb_spec = pl.BlockSpec((1, t, 2), lambda h: (h, 0, 0))
    return proj_specs, w_specs, gb_spec


def _chunk_rows(n, count=1):
    return pl.ds(pl.multiple_of(n * DN_CHUNK, DN_CHUNK), count * DN_CHUNK)


def _egl_rows(n, count=1):
    return pl.ds(pl.multiple_of(n * 8, 8), count * 8)


def _dn_local_inputs(n, qn_sc, kn_sc, vv_sc, gb_ref):
    r = _chunk_rows(n, DN_BATCH)
    split = lambda v: v.reshape(DN_BATCH, DN_CHUNK, v.shape[-1])
    gbv = split(gb_ref[0, r, :])
    return split(qn_sc[r, :]), split(kn_sc[r, :]), split(vv_sc[r, :]), gbv[:, :, 0:1], gbv[:, :, 1:2]


def _dn_local_phase(nc, qn_sc, kn_sc, vv_sc, gb_ref, loc):
    def step(i, carry):
        n = i * DN_BATCH
        vals = _dn_local(*_dn_local_inputs(n, qn_sc, kn_sc, vv_sc, gb_ref))
        for sc, val in zip(loc[:5], vals[:5]):
            sc[_chunk_rows(n, DN_BATCH), :] = val.reshape(DN_BATCH * DN_CHUNK, val.shape[-1])
        loc[5][_egl_rows(n, DN_BATCH), :] = jnp.broadcast_to(vals[5], (DN_BATCH, 8, DN_DH)).reshape(DN_BATCH * 8, DN_DH)
        return carry

    lax.fori_loop(0, nc // DN_BATCH, step, 0)


def _dn_loc_scratch(t, nc):
    big = pltpu.VMEM((t, DN_DH), F32)
    return [big, big, pltpu.VMEM((t, DN_CHUNK), F32), big, big, pltpu.VMEM((nc * 8, DN_DH), F32)]


def _dn_fwd(proj, conv_w, gb, ng, name):
    t = proj.shape[0]
    nc = t // DN_CHUNK
    assert nc % DN_BATCH == 0
    proj_specs, w_specs, gb_spec = _dn_specs(t)

    def body(q_ref, k_ref, v_ref, z_ref, wq_ref, wk_ref, wv_ref, gb_ref, ng_ref, y_ref, o_ref, st_ref, qn_sc, kn_sc, vv_sc, *loc):
        qn_sc[...] = _dn_pre_q(_conv(q_ref[...], wq_ref[...]))
        kn_sc[...] = _dn_pre_k(_conv(k_ref[...], wk_ref[...]))
        vv_sc[...] = _silu(_conv(v_ref[...], wv_ref[...]))
        _dn_local_phase(nc, qn_sc, kn_sc, vv_sc, gb_ref, loc)
        u_sc, kcum_sc, qk_sc, qdec_sc, kdec_sc, egl_sc = loc

        def chunk(n, state):
            r = _chunk_rows(n)
            egl = egl_sc[_egl_rows(n), :][0:1]
            out, new_state = _dn_state(u_sc[r, :], kcum_sc[r, :], qk_sc[r, :], qdec_sc[r, :], kdec_sc[r, :], egl, state)
            st_ref[0, n] = state
            o_ref[r, :] = out
            return new_state

        lax.fori_loop(0, nc, chunk, jnp.zeros((DN_DH, DN_DH), F32))
        y_ref[...] = _dn_post(o_ref[...], z_ref[...], ng_ref[...])

    hblk = pl.BlockSpec((t, DN_DH), lambda h: (0, h))
    return pl.pallas_call(
        body, grid=(DN_HEADS,),
        in_specs=proj_specs + w_specs + [gb_spec, pl.BlockSpec((1, DN_DH), lambda h: (0, 0))],
        out_specs=[hblk, hblk, pl.BlockSpec((1, nc, DN_DH, DN_DH), lambda h: (h, 0, 0, 0))],
        out_shape=[jax.ShapeDtypeStruct((t, BW), F32), jax.ShapeDtypeStruct((t, BW), F32),
                   jax.ShapeDtypeStruct((DN_HEADS, nc, DN_DH, DN_DH), F32)],
        scratch_shapes=[pltpu.VMEM((t, DN_DH), F32)] * 3 + _dn_loc_scratch(t, nc), name=name,
    )(proj, proj, proj, proj, conv_w, conv_w, conv_w, gb, ng)


def _dn_bwd(proj, conv_w, gb, ng, o, states, dy, name):
    t = proj.shape[0]
    nc = t // DN_CHUNK
    proj_specs, w_specs, gb_spec = _dn_specs(t)
    nloc = 6

    def body(q_ref, k_ref, v_ref, z_ref, wq_ref, wk_ref, wv_ref, gb_ref, ng_ref, o_ref, st_ref, dy_ref,
             dq_ref, dk_ref, dv_ref, dz_ref, dwq_ref, dwk_ref, dwv_ref, dgb_ref, dng_ref,
             qn_sc, kn_sc, vv_sc, do_sc, *rest):
        loc, dloc = rest[:nloc], rest[nloc:]
        qn_sc[...] = _dn_pre_q(_conv(q_ref[...], wq_ref[...]))
        kn_sc[...] = _dn_pre_k(_conv(k_ref[...], wk_ref[...]))
        vv_sc[...] = _silu(_conv(v_ref[...], wv_ref[...]))
        _, vjp_y = jax.vjp(_dn_post, o_ref[...], z_ref[...], ng_ref[...])
        do, dz, dng = vjp_y(dy_ref[...])
        do_sc[...] = do
        dz_ref[...] = dz.astype(dz_ref.dtype)
        dng_ref[0] = dng
        _dn_local_phase(nc, qn_sc, kn_sc, vv_sc, gb_ref, loc)
        u_sc, kcum_sc, qk_sc, qdec_sc, kdec_sc, egl_sc = loc

        def state_bwd(i, dstate):
            n = nc - 1 - i
            r = _chunk_rows(n)
            r8 = _egl_rows(n)
            _, vjp = jax.vjp(_dn_state, u_sc[r, :], kcum_sc[r, :], qk_sc[r, :], qdec_sc[r, :], kdec_sc[r, :],
                             egl_sc[r8, :][0:1], st_ref[0, n])
            du, dkcum, dqk, dqdec, dkdec, degl, dprev = vjp((do_sc[r, :], dstate))
            for d_sc, val in zip(dloc[:5], (du, dkcum, dqk, dqdec, dkdec)):
                d_sc[r, :] = val
            dloc[5][r8, :] = jnp.broadcast_to(degl, (8, DN_DH))
            return dprev

        lax.fori_loop(0, nc, state_bwd, jnp.zeros((DN_DH, DN_DH), F32))

        def local_bwd(i, carry):
            n = i * DN_BATCH
            r = _chunk_rows(n, DN_BATCH)
            _, vjp = jax.vjp(_dn_local, *_dn_local_inputs(n, qn_sc, kn_sc, vv_sc, gb_ref))
            cts = tuple(d_sc[r, :].reshape(DN_BATCH, DN_CHUNK, d_sc.shape[-1]) for d_sc in dloc[:5])
            cts += (dloc[5][_egl_rows(n, DN_BATCH), :].reshape(DN_BATCH, 8, DN_DH)[:, 0:1],)
            dqc, dkc, dvc, dg, dbeta = vjp(cts)
            for d_sc, val in zip((dloc[0], dloc[1], dloc[3]), (dqc, dkc, dvc)):
                d_sc[r, :] = val.reshape(DN_BATCH * DN_CHUNK, DN_DH)
            dgb_ref[0, r, :] = jnp.concatenate([dg, dbeta], axis=2).reshape(DN_BATCH * DN_CHUNK, 2)
            return carry

        lax.fori_loop(0, nc // DN_BATCH, local_bwd, 0)
        for x_ref, w_ref, pre, d_sc, dx_ref, dw_ref in ((q_ref, wq_ref, _dn_pre_q, dloc[0], dq_ref, dwq_ref),
                                                       (k_ref, wk_ref, _dn_pre_k, dloc[1], dk_ref, dwk_ref),
                                                       (v_ref, wv_ref, _silu, dloc[3], dv_ref, dwv_ref)):
            _, vjp = jax.vjp(pre, _conv(x_ref[...], w_ref[...]))
            (dc,) = vjp(d_sc[...])
            dx, dws = _conv_bwd(x_ref[...], w_ref[...], dc)
            dx_ref[...] = dx.astype(dx_ref.dtype)
            for j, dw in enumerate(dws):
                dw_ref[j:j + 1, :] = dw

    hblk = pl.BlockSpec((t, DN_DH), lambda h: (0, h))
    wout = pl.BlockSpec((4, DN_DH), lambda h: (0, h))
    return pl.pallas_call(
        body, grid=(DN_HEADS,),
        in_specs=proj_specs + w_specs + [gb_spec, pl.BlockSpec((1, DN_DH), lambda h: (0, 0)), hblk,
                                         pl.BlockSpec((1, nc, DN_DH, DN_DH), lambda h: (h, 0, 0, 0)), hblk],
        out_specs=[hblk] * 4 + [wout] * 3 + [gb_spec, pl.BlockSpec((1, 1, DN_DH), lambda h: (h, 0, 0))],
        out_shape=[jax.ShapeDtypeStruct((t, BW), BF16)] * 4 + [jax.ShapeDtypeStruct((4, BW), F32)] * 3
        + [jax.ShapeDtypeStruct((DN_HEADS, t, 2), F32), jax.ShapeDtypeStruct((DN_HEADS, 1, DN_DH), F32)],
        scratch_shapes=[pltpu.VMEM((t, DN_DH), F32)] * 4 + _dn_loc_scratch(t, nc) * 2, name=name,
    )(proj, proj, proj, proj, conv_w, conv_w, conv_w, gb, ng, o, states, dy)


MERGE_TM, MERGE_TN = 512, 512


def _merge_specs(t):
    tm, tn = min(MERGE_TM, t), MERGE_TN
    y_spec = pl.BlockSpec((tm, BW), lambda i, j: (i, 0))
    w_spec = pl.BlockSpec((3, BW, tn), lambda i, j: (0, 0, j))
    gate_specs = [pl.BlockSpec((tm, tn), functools.partial(lambda i, j, o: (i, o + j), o=(C_GATE + n * D_MODEL) // tn))
                  for n in range(3)]
    return tm, tn, [y_spec] * 3 + [w_spec] + gate_specs


def _merge_fwd(ys, wb, proj, name):
    t = proj.shape[0]
    tm, tn, in_specs = _merge_specs(t)

    def body(y0, y1, y2, w_ref, g0, g1, g2, o_ref):
        acc = jnp.zeros((tm, tn), F32)
        for n, (y, g) in enumerate(((y0, g0), (y1, g1), (y2, g2))):
            acc = acc + _dot(y[...].astype(BF16), w_ref[n], NN) * _sigmoid(g[...])
        o_ref[...] = acc.astype(o_ref.dtype)

    return pl.pallas_call(
        body, grid=(t // tm, D_MODEL // tn), in_specs=in_specs,
        out_specs=pl.BlockSpec((tm, tn), lambda i, j: (i, j)),
        out_shape=jax.ShapeDtypeStruct((t, D_MODEL), BF16), name=name,
    )(*ys, wb, proj, proj, proj)


def _merge_bwd(ys, wb, proj, dmerged, name):
    t = proj.shape[0]
    tm, tn, in_specs = _merge_specs(t)

    def body(y0, y1, y2, w_ref, g0, g1, g2, dm_ref, dg_ref, dt_ref):
        dm = dm_ref[...]
        for n, (y, g) in enumerate(((y0, g0), (y1, g1), (y2, g2))):
            tn_ = _dot(y[...].astype(BF16), w_ref[n], NN)
            sg = _sigmoid(g[...])
            dg_ref[n] = (dm * tn_ * sg * (1.0 - sg)).astype(dg_ref.dtype)
            dt_ref[n] = (dm * sg).astype(dt_ref.dtype)

    o3 = pl.BlockSpec((3, tm, tn), lambda i, j: (0, i, j))
    return pl.pallas_call(
        body, grid=(t // tm, D_MODEL // tn), in_specs=in_specs + [pl.BlockSpec((tm, tn), lambda i, j: (i, j))],
        out_specs=[o3, o3], out_shape=[jax.ShapeDtypeStruct((3, t, D_MODEL), BF16)] * 2, name=name,
    )(*ys, wb, proj, proj, proj, dmerged)


def _prep_rows(sp):
    z4, z112 = jnp.zeros((1, 4), F32), jnp.zeros((1, 112), F32)
    bias_row = jnp.concatenate([sp['b_fox_f'], z4, sp['dn_dt_bias'], z112], axis=1)
    alog_row = jnp.concatenate([jnp.zeros((1, 12), F32), sp['dn_a_log'], z112], axis=1)
    return bias_row, alog_row


def _sc_fwd_tile(sb, sc, sv, w):
    return (sb * _conv(sc * sv, w),)


def _ffn_act(ug, uv):
    return _silu(ug) * uv


W_IN_COLS = [(C_FQ, 0, 1536), (C_SB, 1544, 1536), (C_DQ, 3080, 1536), (C_DZ, 4624, 512), (C_GATE, 5136, 3072),
             (C_SMALL, 1536, 8), (C_SMALL + 8, 4616, 8)]
W_IN_SHARD = IN_ORIG // N_DEV


def _w_in_segments():
    out = []
    for d, o, w in W_IN_COLS:
        end = o + w
        while o < end:
            k = o // W_IN_SHARD
            n = min(end, (k + 1) * W_IN_SHARD) - o
            out.append((d, k, o - k * W_IN_SHARD, n))
            o, d = o + n, d + n
    return out


def _w_in_assemble(g, name):
    tm = 256

    def body(g_ref, o_ref):
        for d, k, s, n in _w_in_segments():
            o_ref[:, d:d + n] = g_ref[k, :, s:s + n]
        o_ref[:, IN_ORIG:IN_P] = jnp.zeros((tm, IN_P - IN_ORIG), o_ref.dtype)

    return pl.pallas_call(
        body, grid=(D_MODEL // tm,),
        in_specs=[pl.BlockSpec((N_DEV, tm, W_IN_SHARD), lambda i: (0, i, 0))],
        out_specs=pl.BlockSpec((tm, IN_P), lambda i: (i, 0)),
        out_shape=jax.ShapeDtypeStruct((D_MODEL, IN_P), g.dtype), name=name,
    )(g)


W_IN_SPLIT_TM = 128


def _w_in_split_by_core(g, name):
    tm = W_IN_SPLIT_TM

    def scatter(g_ref, keep_ref, give_ref, core):
        for d, k, s, n in _w_in_segments():
            val = g_ref[:, d:d + n]
            if k % 2 == core:
                keep_ref[k // 2, :, s:s + n] = val
            else:
                give_ref[k // 2, :, s:s + n] = val.astype(give_ref.dtype)

    def body(g_ref, keep_ref, give_ref):
        c = lax.axis_index("c")
        for core in range(2):
            pl.when(c == core)(functools.partial(scatter, g_ref, keep_ref, give_ref, core))

    blk = pl.BlockSpec((4, tm, W_IN_SHARD), lambda i: (0, i, 0))
    shape = (4, D_MODEL, W_IN_SHARD)
    return pl.pallas_call(
        body, grid=(D_MODEL // tm,), in_specs=[pl.BlockSpec((tm, IN_P), lambda i: (i, 0))],
        out_specs=[blk, blk], out_shape=[jax.ShapeDtypeStruct(shape, F32), jax.ShapeDtypeStruct(shape, BF16)], name=name,
    )(g)


def _w_in_split_by_device(g, name):
    tm = W_IN_SPLIT_TM

    def body(g_ref, own_ref, send_ref):
        me = 4 * lax.axis_index("x") + 2 * lax.axis_index("y") + lax.axis_index("c")
        for d, k, s, n in _w_in_segments():
            val = g_ref[:, d:d + n]
            send_ref[k, :, s:s + n] = val.astype(send_ref.dtype)

            @pl.when(me == k)
            def _():
                own_ref[:, s:s + n] = val

    return pl.pallas_call(
        body, grid=(D_MODEL // tm,), in_specs=[pl.BlockSpec((tm, IN_P), lambda i: (i, 0))],
        out_specs=[pl.BlockSpec((tm, W_IN_SHARD), lambda i: (i, 0)), pl.BlockSpec((N_DEV, tm, W_IN_SHARD), lambda i: (0, i, 0))],
        out_shape=[jax.ShapeDtypeStruct((D_MODEL, W_IN_SHARD), F32), jax.ShapeDtypeStruct((N_DEV, D_MODEL, W_IN_SHARD), BF16)],
        name=name,
    )(g)


def _layer_fwd(x, p_i, w, sp, tag):
    t = x.shape[0]
    sv = {'x': x}
    (hn,) = _ew(lambda a, g: (_rms(a, g),), [x], [sp['g_mix']], [(D_MODEL, BF16)], name=f'rms_mix_{tag}')
    proj = _mm(hn, w['w_in'], 'nn', [F32], name=f'in_proj_{tag}')
    bias_row, alog_row = _prep_rows(sp)
    aux = _prep_fwd(proj, bias_row, alog_row, f'prep_{tag}')
    cf = aux[:, :FOX_HEADS].T
    cfc, cfr = cf[:, :, None], cf[:, None, :]
    y_fox, lse = _att_fwd(proj, sp['fox_q_gain'], sp['fox_k_gain'], cfc, cfr, f'fox_fwd_{tag}')
    (y_sc,) = _cb(_sc_fwd_tile, [(proj, C_SB), (proj, C_SC), (proj, C_SV)], [(w['sc_conv_w'], 0)], [F32], [],
                  tc=256, nblk=2, name=f'sc_fwd_{tag}')
    gb = jnp.stack([aux[:, 12:16].T, aux[:, 8:12].T], axis=-1)
    y_dn, o_dn, states = _dn_fwd(proj, w['dn_conv_w'], gb, sp['dn_norm_gain'], f'dn_fwd_{tag}')
    ys = (y_fox, y_sc, y_dn)
    merged = _merge_fwd(ys, w['w_branch'], proj, f'merge_fwd_{tag}')
    x1 = _mm(merged, w['w_o'], 'nn', [F32], epi=lambda acc, r: (acc + r,), epi_args=(x,), name=f'o_proj_{tag}')
    (hf,) = _ew(lambda a, g: (_rms(a, g),), [x1], [sp['g_ffn']], [(D_MODEL, BF16)], name=f'rms_ffn_{tag}')
    up = _mm(hf, w['w_up'], 'nn', [F32], name=f'up_proj_{tag}')
    (act,) = _cb(lambda ug, uv, wg, wv: (_ffn_act(_conv(ug, wg), _conv(uv, wv)),), [(up, 0), (up, D_FF)],
                 [(w['ffn_conv_w'], 0), (w['ffn_conv_w'], D_FF)], [BF16], [], tc=256, nblk=D_FF // 256, name=f'ffn_act_{tag}')
    x2 = _mm(act, w['w_down'], 'nn', [F32], epi=lambda acc, r: (acc + r,), epi_args=(x1,), name=f'down_proj_{tag}')
    (hp,) = _ew(lambda a, g: (_rms(a, g),), [x2], [sp['g_ple']], [(D_MODEL, BF16)], name=f'rms_ple_{tag}')
    gp = _mm(hp, w['w_ple_gate'], 'nn', [F32], name=f'ple_gate_{tag}')
    x3 = _mm(p_i, w['w_ple'], 'nn', [F32], epi=lambda acc, g, r: (r + _sigmoid(g) * acc,), epi_args=(gp, x2), name=f'ple_{tag}')
    sv.update(hn=hn, proj=proj, aux=aux, cfc=cfc, cfr=cfr, lse=lse, ys=ys, gb=gb, o_dn=o_dn,
              states=states, merged=merged, x1=x1, hf=hf, up=up, act=act, x2=x2, hp=hp, gp=gp, p=p_i,
              bias_row=bias_row, alog_row=alog_row)
    return x3, sv


def _rms_bwd(x, g, dh, dres, name):
    def fn(xv, dhv, dr, gv):
        _, vjp = jax.vjp(_rms, xv, gv)
        dx, dg = vjp(dhv)
        return dr + dx, dg
    return _ew(fn, [x, dh, dres], [g], [(D_MODEL, F32)], [(1, D_MODEL)], name=name)


def _layer_bwd(dx3, sv, w, sp, tag):
    t = dx3.shape[0]
    g = {}
    def ple_epi(acc, gpv, d):
        s = _sigmoid(gpv)
        return d * acc * s * (1.0 - s), d * s
    dgp, de = _mm(sv['p'], w['w_ple'], 'nn', [BF16, BF16], epi=ple_epi, epi_args=(sv['gp'], dx3), name=f'ple_bwd_{tag}')
    g['w_ple'] = _mm(sv['p'], de, 'tn', [F32], name=f'd_w_ple_{tag}')
    g['w_ple_gate'] = _mm(sv['hp'], dgp, 'tn', [F32], name=f'd_w_ple_gate_{tag}')
    dhp = _mm(dgp, w['w_ple_gate'], 'nt', [F32], name=f'd_hp_{tag}')
    dx2, g['g_ple'] = _rms_bwd(sv['x2'], sp['g_ple'], dhp, dx3, f'rms_ple_bwd_{tag}')
    dact = _mm(dx2, w['w_down'], 'nt', [F32], name=f'd_act_{tag}')
    g['w_down'] = _mm(sv['act'], dx2, 'tn', [F32], name=f'd_w_down_{tag}')

    def ffn_bwd_tile(ug, uv, da, wg, wv):
        cg, cv = _conv(ug, wg), _conv(uv, wv)
        _, vjp = jax.vjp(_ffn_act, cg, cv)
        dcg, dcv = vjp(da)
        dug, dwg = _conv_bwd(ug, wg, dcg)
        duv, dwv = _conv_bwd(uv, wv, dcv)
        return dug, duv, jnp.concatenate(dwg, axis=0), jnp.concatenate(dwv, axis=0)
    dupg, dupv, dwg, dwv = _cb(ffn_bwd_tile, [(sv['up'], 0), (sv['up'], D_FF), (dact, 0)],
                               [(w['ffn_conv_w'], 0), (w['ffn_conv_w'], D_FF)], [BF16, BF16], [3, 3], tc=256,
                               nblk=D_FF // 256, name=f'ffn_act_bwd_{tag}')
    dup = jnp.concatenate([dupg, dupv], axis=1)
    g['ffn_conv_w'] = jnp.concatenate([dwg, dwv], axis=1)
    g['w_up'] = _mm(sv['hf'], dup, 'tn', [F32], name=f'd_w_up_{tag}')
    dhf = _mm(dup, w['w_up'], 'nt', [F32], name=f'd_hf_{tag}')
    dx1, g['g_ffn'] = _rms_bwd(sv['x1'], sp['g_ffn'], dhf, dx2, f'rms_ffn_bwd_{tag}')
    dmerged = _mm(dx1, w['w_o'], 'nt', [F32], name=f'd_merged_{tag}')
    g['w_o'] = _mm(sv['merged'], dx1, 'tn', [F32], name=f'd_w_o_{tag}')
    dgate, dtn = _merge_bwd(sv['ys'], w['w_branch'], sv['proj'], dmerged, f'merge_bwd_{tag}')
    dys, dwb = [], []
    for n in range(3):
        dys.append(_mm(dtn[n], w['w_branch'][n], 'nt', [F32], name=f'd_y{n}_{tag}'))
        dwb.append(_mm(sv['ys'][n], dtn[n], 'tn', [F32], name=f'd_w_branch{n}_{tag}'))
    g['w_branch'] = jnp.stack(dwb)
    ddq, ddk, ddv, ddz, dwq, dwk, dwv_, dgb, dng = _dn_bwd(sv['proj'], w['dn_conv_w'], sv['gb'], sp['dn_norm_gain'],
                                                           sv['o_dn'], sv['states'], dys[2], f'dn_bwd_{tag}')
    g['dn_conv_w'] = jnp.concatenate([dwq, dwk, dwv_], axis=1)
    g['dn_norm_gain'] = jnp.sum(dng, axis=0)
    def sc_bwd_tile(sb, sc, svv, dy, wv):
        u = sc * svv
        dsb = dy * _conv(u, wv)
        du, dws = _conv_bwd(u, wv, dy * sb)
        return dsb, du * svv, du * sc, jnp.concatenate(dws, axis=0)
    dsb, dsc, dsv, g['sc_conv_w'] = _cb(sc_bwd_tile, [(sv['proj'], C_SB), (sv['proj'], C_SC), (sv['proj'], C_SV), (dys[1], 0)],
                                        [(w['sc_conv_w'], 0)], [BF16, BF16, BF16], [3], tc=256, nblk=2, name=f'sc_bwd_{tag}')
    dfq, dfk, dfv, dcfc, dcfr, dqg, dkg = _att_bwd(sv['proj'], sp['fox_q_gain'], sp['fox_k_gain'], sv['cfc'], sv['cfr'],
                                                   sv['lse'], sv['ys'][0], dys[0], f'fox_bwd_{tag}')
    g['fox_q_gain'] = jnp.sum(dqg, axis=0)
    g['fox_k_gain'] = jnp.sum(dkg, axis=0)
    dcf = (dcfc[:, :, 0] + dcfr[:, 0, :]).T
    daux = jnp.concatenate([dcf, dgb[:, :, 1].T, dgb[:, :, 0].T, jnp.zeros((t, 112), F32)], axis=1)
    dsmall, dbias, dalog = _prep_bwd(sv['proj'], sv['bias_row'], sv['alog_row'], daux, f'prep_bwd_{tag}')
    g['b_fox_f'] = dbias[:, 0:8]
    g['dn_dt_bias'] = dbias[:, 12:16]
    g['dn_a_log'] = dalog[:, 12:16]
    dproj = jnp.concatenate([dfq, dfk, dfv, dsb, dsc, dsv, ddq, ddk, ddv, ddz, dgate[0], dgate[1], dgate[2], dsmall], axis=1)
    g['w_in'] = _mm(sv['hn'], dproj, 'tn', [F32], name=f'd_w_in_{tag}')
    dhn = _mm(dproj, w['w_in'], 'nt', [F32], name=f'd_hn_{tag}')
    dx, g['g_mix'] = _rms_bwd(sv['x'], sp['g_mix'], dhn, dx1, f'rms_mix_bwd_{tag}')
    return dx, g


def _loss_bwd(y, target, name):
    inv = 1.0 / y.shape[1]

    def fn(yv, tv):
        err = yv - tv
        return err * inv, jnp.zeros((8, 128), F32) + 0.5 * inv * jnp.sum(err * err)
    return _ew(fn, [y, target], [], [(y.shape[1], F32)], [(8, 128)], name=name)


PACK_W = 1024
FULL_SHAPE = {'w_in': (D_MODEL, IN_ORIG), 'w_branch': (3, BW, D_MODEL), 'w_o': (D_MODEL, D_MODEL), 'w_up': (D_MODEL, 2 * D_FF),
              'w_down': (D_FF, D_MODEL), 'w_ple_gate': (D_MODEL, D_MODEL), 'w_ple': (PLE_DIM, D_MODEL),
              'sc_conv_w': (3, BW), 'dn_conv_w': (4, 3 * BW), 'ffn_conv_w': (3, 2 * D_FF)}
SMALL_SHAPE = {'g_mix': D_MODEL, 'b_fox_f': FOX_HEADS, 'fox_q_gain': FOX_DH, 'fox_k_gain': FOX_DH, 'dn_a_log': DN_HEADS,
               'dn_dt_bias': DN_HEADS, 'dn_norm_gain': DN_DH, 'g_ffn': D_MODEL, 'g_ple': D_MODEL}


def _shard_shape(name):
    s = list(FULL_SHAPE[name])
    s[SHARDED[name]] //= N_DEV
    return tuple(s)


def _full_from_gathered(g, name):
    sh, ax = _shard_shape(name), SHARDED[name]
    blocks = jnp.moveaxis(g, 0, ax)
    return blocks.reshape(sh[:ax] + (N_DEV * sh[ax],) + sh[ax + 1:])


def _by_dest(full, name):
    sh, ax = _shard_shape(name), SHARDED[name]
    return jnp.moveaxis(full.reshape(sh[:ax] + (N_DEV, sh[ax]) + sh[ax + 1:]), ax, 0)


def _flat_pack(arrs):
    flat = jnp.concatenate([a.reshape(-1).astype(F32) for a in arrs])
    rows = -(-flat.shape[0] // (8 * PACK_W)) * 8
    return jnp.pad(flat, (0, rows * PACK_W - flat.shape[0])).reshape(rows, PACK_W)


def _flat_unpack(pack, shapes):
    flat, out, off = pack.reshape(-1), [], 0
    for s in shapes:
        n = 1
        for d in s:
            n *= d
        out.append(flat[off:off + n].reshape(s))
        off += n
    return out


SMALL_SHAPES = [(DEPTH, SMALL_SHAPE[n]) for n in SMALL]
SMALL_LOSS_AT = sum(DEPTH * SMALL_SHAPE[n] for n in SMALL)
CONV_SHARD_SHAPES = [(DEPTH,) + _shard_shape(n) for n in CONVW]
CONV_FULL_SHAPES = [(DEPTH,) + FULL_SHAPE[n] for n in CONVW]


def _rows2d(a):
    return a.reshape(-1, a.shape[-1])


def _pick_tm(m, width):
    best = None
    for tm in range(16, m + 1, 16):
        if m % tm == 0 and tm * width * 4 <= (1 << 20):
            best = tm
    return best if best is not None else m


HBM_SPEC = pl.BlockSpec(memory_space=pltpu.HBM)


def _place():
    x, y, c = lax.axis_index("x"), lax.axis_index("y"), lax.axis_index("c")
    return x, y, c, [(1 - x, y), (x, 1 - y), (1 - x, 1 - y)]


def _all_gather(arrs, name):
    n = len(arrs)

    def body(*refs):
        ins, outs = refs[:n], refs[n:2 * n]
        send_sems, recv_sems, local_sems = refs[2 * n:]
        x, y, c, chips = _place()
        me, sibling = (x, y, c), (x, y, 1 - c)

        def block(a, p):
            return outs[a].at[4 * p[0] + 2 * p[1] + p[2]]

        def copy(a, k, blk, to, src=None):
            return pltpu.make_async_remote_copy(src_ref=block(a, blk) if src is None else src, dst_ref=block(a, blk),
                                                send_sem=send_sems.at[a, k], recv_sem=recv_sems.at[a, k],
                                                device_id=to, device_id_type=MESH)

        mine = [pltpu.make_async_copy(ins[a], block(a, me), local_sems.at[a]) for a in range(n)]
        first, passed = [], []
        for a in range(n):
            mine[a].start()
            first.append(copy(a, 0, me, sibling, src=ins[a]))
            first += [copy(a, 1 + j, me, (*chip, c), src=ins[a]) for j, chip in enumerate(chips)]
        for cp in first:
            cp.start()
        for j, chip in enumerate(chips):
            for a in range(n):
                copy(a, 1 + j, (*chip, c), me).wait_recv()
                fwd = copy(a, 4 + j, (*chip, c), sibling)
                fwd.start()
                passed.append(fwd)
        for a in range(n):
            copy(a, 0, sibling, me).wait_recv()
            for j, chip in enumerate(chips):
                copy(a, 4 + j, (*chip, 1 - c), me).wait_recv()
        for cp in first + passed:
            cp.wait_send()
        for a in range(n):
            mine[a].wait()

    return pl.pallas_call(
        body, in_specs=[HBM_SPEC] * n, out_specs=[HBM_SPEC] * n,
        out_shape=[jax.ShapeDtypeStruct((N_DEV,) + a.shape, a.dtype) for a in arrs],
        scratch_shapes=[pltpu.SemaphoreType.DMA((n, 7)), pltpu.SemaphoreType.DMA((n, 7)), pltpu.SemaphoreType.DMA((n,))],
        name=name,
    )(*arrs)


def _sibling_swap(arrs, name):
    n = len(arrs)

    def body(*refs):
        send_sems, recv_sems = refs[2 * n:]
        x, y, c, _ = _place()
        cps = [pltpu.make_async_remote_copy(src_ref=refs[a], dst_ref=refs[n + a], send_sem=send_sems.at[a], recv_sem=recv_sems.at[a],
                                            device_id=(x, y, 1 - c), device_id_type=MESH) for a in range(n)]
        for cp in cps:
            cp.start()
        for cp in cps:
            cp.wait()

    return pl.pallas_call(body, in_specs=[HBM_SPEC] * n, out_specs=[HBM_SPEC] * n,
                          out_shape=[jax.ShapeDtypeStruct(a.shape, a.dtype) for a in arrs],
                          scratch_shapes=[pltpu.SemaphoreType.DMA((n,)), pltpu.SemaphoreType.DMA((n,))], name=name)(*arrs)


def _chip_exchange(arrs, name):
    n = len(arrs)

    def body(*refs):
        send_sems, recv_sems = refs[2 * n:]
        x, y, c, chips = _place()
        cps = [pltpu.make_async_remote_copy(src_ref=refs[a].at[2 * cx + cy], dst_ref=refs[n + a].at[j], send_sem=send_sems.at[a, j],
                                            recv_sem=recv_sems.at[a, j], device_id=(cx, cy, c), device_id_type=MESH)
               for a in range(n) for j, (cx, cy) in enumerate(chips)]
        for cp in cps:
            cp.start()
        for cp in cps:
            cp.wait()

    return pl.pallas_call(body, in_specs=[HBM_SPEC] * n, out_specs=[HBM_SPEC] * n,
                          out_shape=[jax.ShapeDtypeStruct((3,) + a.shape[1:], a.dtype) for a in arrs],
                          scratch_shapes=[pltpu.SemaphoreType.DMA((n, 3)), pltpu.SemaphoreType.DMA((n, 3))], name=name)(*arrs)


ANY_SPEC = pl.BlockSpec(memory_space=pl.ANY)
SEM_SPEC = pl.BlockSpec(memory_space=pltpu.SEMAPHORE)
DATAFLOW = pltpu.SideEffectType.DATAFLOW_SIDE_EFFECTING
N_PEER = N_DEV - 1


def _direct_copies(src_refs, zone_refs, send_sems, recv_sems, per_dest):
    x, y, c, _ = _place()
    me = 4 * x + 2 * y + c
    cps = []
    for a, (src, zone) in enumerate(zip(src_refs, zone_refs)):
        for r in range(N_PEER):
            bits = r + 1
            px = 1 - x if bits & 4 else x
            py = 1 - y if bits & 2 else y
            pc = 1 - c if bits & 1 else c
            cps.append(pltpu.make_async_remote_copy(
                src_ref=src.at[4 * px + 2 * py + pc] if per_dest else src, dst_ref=zone.at[me],
                send_sem=send_sems.at[a * N_PEER + r], recv_sem=recv_sems.at[a * N_PEER + r],
                device_id=(px, py, pc), device_id_type=MESH))
    return cps


def _direct_start(srcs, per_dest, after, name):
    n = len(srcs)
    zones = [lax.empty((N_DEV,) + (s.shape[1:] if per_dest else s.shape), s.dtype) for s in srcs]

    def body(*refs):
        src_refs, zone_refs = refs[:n], refs[n:2 * n]
        send_sems, recv_sems = refs[2 * n + 1:2 * n + 3]
        for cp in _direct_copies(src_refs, zone_refs, send_sems, recv_sems, per_dest):
            cp.start()
        refs[-1][...] = jnp.zeros_like(refs[-1])

    sems = pltpu.SemaphoreType.DMA((n * N_PEER,))
    res = pl.pallas_call(
        body, name=name, in_specs=[HBM_SPEC] * (2 * n) + [ANY_SPEC],
        out_shape=[sems, sems] + [pltpu.HBM(s.shape, s.dtype) for s in srcs] + [pltpu.HBM(z.shape, z.dtype) for z in zones]
        + [jax.ShapeDtypeStruct((8, 128), F32)],
        out_specs=[SEM_SPEC, SEM_SPEC] + [HBM_SPEC] * (2 * n) + [pl.BlockSpec(memory_space=pltpu.VMEM)],
        input_output_aliases={i: 2 + i for i in range(2 * n)},
        compiler_params=pltpu.CompilerParams(has_side_effects=DATAFLOW),
    )(*[pltpu.with_memory_space_constraint(s, pltpu.HBM) for s in srcs],
      *[pltpu.with_memory_space_constraint(z, pltpu.HBM) for z in zones], after)
    return (res[0], res[1], list(res[2:2 + n]), list(res[2 + n:2 + 2 * n]), per_dest), res[-1][0:1, 0:1]


def _direct_wait(started, after, name):
    send_sems, recv_sems, srcs, zones, per_dest = started
    n = len(srcs)

    def body(*refs):
        src_refs, zone_refs = refs[:n], refs[n:2 * n]
        for cp in _direct_copies(src_refs, zone_refs, refs[2 * n], refs[2 * n + 1], per_dest):
            cp.wait_send()
            cp.wait_recv()

    res = pl.pallas_call(
        body, name=name, in_specs=[HBM_SPEC] * (2 * n) + [SEM_SPEC, SEM_SPEC, ANY_SPEC],
        out_shape=[pltpu.HBM(s.shape, s.dtype) for s in srcs] + [pltpu.HBM(z.shape, z.dtype) for z in zones],
        out_specs=[HBM_SPEC] * (2 * n), input_output_aliases={i: i for i in range(2 * n)},
        compiler_params=pltpu.CompilerParams(has_side_effects=DATAFLOW),
    )(*srcs, *zones, send_sems, recv_sems, after)
    return list(res[n:])


def _adamw(w, g, m, v):
    m = ADAM_B1 * m + (1.0 - ADAM_B1) * g
    v = ADAM_B2 * v + (1.0 - ADAM_B2) * jnp.square(g)
    m_hat = m / (1.0 - ADAM_B1 ** ADAM_STEP)
    v_hat = v / (1.0 - ADAM_B2 ** ADAM_STEP)
    delta = -ADAM_LR * (m_hat / (jnp.sqrt(v_hat) + ADAM_EPS) + ADAM_WD * w)
    return delta, m, v


def _adamw_layers(w, m, v, own0, ex0, own1, zone1, name):
    depth, rows, c = w.shape
    tm = _pick_tm(rows, c)
    nt = rows // tm
    first = lambda l, i: i * (1 - l) + (nt - 1) * l
    second = lambda l, i: i * l

    def body(w_ref, m_ref, v_ref, own0_ref, ex0_ref, own1_ref, zone1_ref, g_ref, d_ref, m2_ref, v2_ref):
        def finish(g):
            delta, m2, v2 = _adamw(w_ref[...], g, m_ref[...], v_ref[...])
            g_ref[...], d_ref[...], m2_ref[...], v2_ref[...] = g, delta, m2, v2

        @pl.when(pl.program_id(0) == 0)
        def _():
            g = own0_ref[...]
            for j in range(ex0_ref.shape[0]):
                g = g + ex0_ref[j].astype(F32)
            finish(g)

        @pl.when(pl.program_id(0) == 1)
        def _():
            g = own1_ref[...]
            for d in range(zone1_ref.shape[0]):
                g = g + zone1_ref[d].astype(F32)
            finish(g)

    wspec = pl.BlockSpec((None, tm, c), lambda l, i: (l, i, 0))
    return pl.pallas_call(
        body, grid=(depth, nt),
        in_specs=[wspec] * 3 + [pl.BlockSpec((tm, c), lambda l, i: (first(l, i), 0)),
                                pl.BlockSpec((ex0.shape[0], tm, c), lambda l, i: (0, first(l, i), 0)),
                                pl.BlockSpec((tm, c), lambda l, i: (second(l, i), 0)),
                                pl.BlockSpec((zone1.shape[0], tm, c), lambda l, i: (0, second(l, i), 0))],
        out_specs=[wspec] * 4, out_shape=[jax.ShapeDtypeStruct(w.shape, F32)] * 4, name=name,
    )(w, m, v, own0, ex0, own1, zone1)


def _adamw_call(w, m, v, gparts, name):
    def fn(wv, mv, vv, *gs):
        g = gs[0].astype(F32)
        for gp in gs[1:]:
            g = g + gp.astype(F32)
        delta, m2, v2 = _adamw(wv, g, mv, vv)
        return g, delta, m2, v2
    rows, c = w.shape
    return _ew(fn, [w, m, v] + list(gparts), [], [(c, F32)] * 4, tm=_pick_tm(rows, c), name=name)


def kernel(x, p, g_mix, w_in, b_fox_f, fox_q_gain, fox_k_gain, sc_conv_w, dn_conv_w, dn_a_log, dn_dt_bias,
           dn_norm_gain, w_branch, w_o, g_ffn, w_up, ffn_conv_w, w_down, g_ple, w_ple_gate, w_ple, loss_target,
           m_g_mix, m_w_in, m_b_fox_f, m_fox_q_gain, m_fox_k_gain, m_sc_conv_w, m_dn_conv_w, m_dn_a_log,
           m_dn_dt_bias, m_dn_norm_gain, m_w_branch, m_w_o, m_g_ffn, m_w_up, m_ffn_conv_w, m_w_down, m_g_ple,
           m_w_ple_gate, m_w_ple, v_g_mix, v_w_in, v_b_fox_f, v_fox_q_gain, v_fox_k_gain, v_sc_conv_w, v_dn_conv_w,
           v_dn_a_log, v_dn_dt_bias, v_dn_norm_gain, v_w_branch, v_w_o, v_g_ffn, v_w_up, v_ffn_conv_w, v_w_down,
           v_g_ple, v_w_ple_gate, v_w_ple):
    return _step(x, p, g_mix, w_in, b_fox_f, fox_q_gain, fox_k_gain, sc_conv_w, dn_conv_w, dn_a_log, dn_dt_bias,
                 dn_norm_gain, w_branch, w_o, g_ffn, w_up, ffn_conv_w, w_down, g_ple, w_ple_gate, w_ple, loss_target,
                 m_g_mix, m_w_in, m_b_fox_f, m_fox_q_gain, m_fox_k_gain, m_sc_conv_w, m_dn_conv_w, m_dn_a_log,
                 m_dn_dt_bias, m_dn_norm_gain, m_w_branch, m_w_o, m_g_ffn, m_w_up, m_ffn_conv_w, m_w_down, m_g_ple,
                 m_w_ple_gate, m_w_ple, v_g_mix, v_w_in, v_b_fox_f, v_fox_q_gain, v_fox_k_gain, v_sc_conv_w,
                 v_dn_conv_w, v_dn_a_log, v_dn_dt_bias, v_dn_norm_gain, v_w_branch, v_w_o, v_g_ffn, v_w_up,
                 v_ffn_conv_w, v_w_down, v_g_ple, v_w_ple_gate, v_w_ple)


def _step(*args):
    names = ['x', 'p'] + WEIGHTS + ['loss_target'] + ['m_' + n for n in WEIGHTS] + ['v_' + n for n in WEIGHTS]
    assert len(args) == len(names)
    a = dict(zip(names, args))
    x, target = a['x'][0], a['loss_target'][0]
    p = a['p'][:, 0]
    _, _, c, _ = _place()
    chip = 2 * lax.axis_index("x") + lax.axis_index("y")

    dev = 4 * lax.axis_index("x") + 2 * lax.axis_index("y") + c

    def full_weights(gathered, conv_full, layer):
        w = {n: _full_from_gathered(g, n) for n, g in zip(BIG, gathered) if n != 'w_in'}
        w.update({n: conv_full[n][layer] for n in CONVW})
        w['w_in'] = _w_in_assemble(gathered[list(BIG).index('w_in')], f'w_in_assemble_l{layer}')
        return w

    shards = [[a[n][layer].astype(BF16) for n in BIG] for layer in range(DEPTH)]
    gathered0 = _all_gather(shards[0] + [_flat_pack([a[n] for n in CONVW])], 'gather_weights_l0')
    conv_by_dev = [_flat_unpack(gathered0[-1][d], CONV_SHARD_SHAPES) for d in range(N_DEV)]
    conv_full = {n: jnp.concatenate([conv_by_dev[d][i] for d in range(N_DEV)], axis=2) for i, n in enumerate(CONVW)}
    w1_started, w1_zero = _direct_start(shards[1], False, gathered0[0], 'weights_l1_start')
    sps = [{n: a[n][layer][None, :] for n in SMALL} for layer in range(DEPTH)]
    sps[0]['g_mix'] = sps[0]['g_mix'] + w1_zero
    ws = [full_weights(gathered0[:-1], conv_full, 0), None]

    h, sv0 = _layer_fwd(x, p[0], ws[0], sps[0], 'l0')
    zones = _direct_wait(w1_started, h, 'weights_l1_wait')
    ws[1] = full_weights([lax.dynamic_update_index_in_dim(z, s, dev, 0) for z, s in zip(zones, shards[1])], conv_full, 1)
    h, sv1 = _layer_fwd(h, p[1], ws[1], sps[1], 'l1')
    dh, loss_part = _loss_bwd(h, target, 'loss')
    grads = [None] * DEPTH
    dh, grads[1] = _layer_bwd(dh, sv1, ws[1], sps[1], 'l1')
    owns1, sends1 = [], []
    for n in BIG:
        if n == 'w_in':
            own, send = _w_in_split_by_device(grads[1][n], 'w_in_grad_split_l1')
        else:
            by_dest = _by_dest(grads[1][n], n)
            own, send = lax.dynamic_index_in_dim(by_dest, dev, axis=0, keepdims=False), by_dest.astype(BF16)
        owns1.append(own)
        sends1.append(send)
    g1_started, g1_zero = _direct_start(sends1, True, dh, 'grads_l1_start')
    dh, grads[0] = _layer_bwd(dh, sv0, ws[0], {**sps[0], 'g_ple': sps[0]['g_ple'] + g1_zero}, 'l0')
    grad_x = dh[None]

    keeps, gives = [], []
    for n in BIG:
        if n == 'w_in':
            keep, give = _w_in_split_by_core(grads[0][n], 'w_in_grad_split_l0')
        else:
            by_dest = _by_dest(grads[0][n], n)
            by_dest = by_dest.reshape((4, 2) + by_dest.shape[1:])
            keep = lax.dynamic_index_in_dim(by_dest, c, axis=1, keepdims=False)
            give = lax.dynamic_index_in_dim(by_dest, 1 - c, axis=1, keepdims=False).astype(BF16)
        keeps.append(keep)
        gives.append(give)
    gots = _sibling_swap(gives, 'grad_sibling_swap')
    parts, parts_b = [], []
    for n, keep, got in zip(BIG, keeps, gots):
        k2, g2 = _rows2d(keep), _rows2d(got)
        part, part_b = _ew(lambda k, r: (k + r.astype(F32),) * 2, [k2, g2], [], [(k2.shape[1], F32), (k2.shape[1], BF16)],
                           tm=_pick_tm(*k2.shape), name=f'grad_chip_sum_{n}')
        parts.append(part.reshape(keep.shape))
        parts_b.append(part_b.reshape(keep.shape))
    others = _chip_exchange(parts_b, 'grad_chip_exchange')
    zones = _direct_wait(g1_started, others[0], 'grads_l1_wait')
    out = {}
    for n, part, other, own1, zone in zip(BIG, parts, others, owns1, zones):
        rc = (-1, a[n].shape[-1])
        own0 = lax.dynamic_index_in_dim(part, chip, axis=0, keepdims=False).reshape(rc)
        zone = lax.dynamic_update_index_in_dim(zone, jnp.zeros(zone.shape[1:], zone.dtype), dev, 0)
        view = lambda t: t.reshape((DEPTH,) + own0.shape)
        res = _adamw_layers(view(a[n]), view(a['m_' + n]), view(a['v_' + n]), own0, other.reshape((3,) + own0.shape),
                            own1.reshape(rc), zone.reshape((N_DEV,) + own0.shape), f'adamw_{n}')
        out[n] = [r.reshape(a[n].shape) for r in res]

    small_pack = _flat_pack([jnp.stack([grads[layer][n].reshape(-1) for layer in range(DEPTH)]) for n in SMALL] + [loss_part[0, 0]])
    conv_pack = _flat_pack([jnp.stack([grads[layer][n] for layer in range(DEPTH)]) for n in CONVW])
    small_all, conv_all = _all_gather([small_pack, conv_pack], 'gather_small_grads')
    res = _adamw_call(_flat_pack([a[n] for n in SMALL]), _flat_pack([a['m_' + n] for n in SMALL]),
                      _flat_pack([a['v_' + n] for n in SMALL]), [small_all[d] for d in range(N_DEV)], 'adamw_replicated')
    loss = res[0].reshape(-1)[SMALL_LOSS_AT]
    for k, r in enumerate(res):
        for n, val in zip(SMALL, _flat_unpack(r, SMALL_SHAPES)):
            out.setdefault(n, [None] * 4)[k] = val
    (conv_sum,) = _ew(lambda *gs: (functools.reduce(lambda s, t: s + t, gs),), [conv_all[d] for d in range(N_DEV)], [],
                      [(PACK_W, F32)], tm=conv_pack.shape[0], name='conv_grad_sum')
    conv_own = [lax.dynamic_slice_in_dim(g, dev * (g.shape[2] // N_DEV), g.shape[2] // N_DEV, axis=2)
                for g in _flat_unpack(conv_sum, CONV_FULL_SHAPES)]
    res = _adamw_call(_flat_pack([a[n] for n in CONVW]), _flat_pack([a['m_' + n] for n in CONVW]),
                      _flat_pack([a['v_' + n] for n in CONVW]), [_flat_pack(conv_own)], 'adamw_conv')
    for k, r in enumerate(res):
        for n, val in zip(CONVW, _flat_unpack(r, CONV_SHARD_SHAPES)):
            out.setdefault(n, [None] * 4)[k] = val

    outs = [loss, grad_x]
    for k in range(4):
        outs += [out[n][k] for n in WEIGHTS]
    return tuple(outs)
```

```python
import functools

import jax
import jax.numpy as jnp
from jax import lax
from jax.experimental import pallas as pl
from jax.experimental.pallas import tpu as pltpu

F32 = jnp.float32
BF16 = jnp.bfloat16
HI = lax.Precision.HIGHEST

D_MODEL = 1024
DEPTH = 2
N_DEV = 8
PLE_DIM = 256
BW = 512
FOX_HEADS, FOX_DH = 8, 64
DN_HEADS, DN_DH = 4, 128
DN_CHUNK = 64
D_FF = 2816
EPS = 1e-6
NEG = -1e30

ADAM_LR, ADAM_B1, ADAM_B2, ADAM_EPS, ADAM_WD, ADAM_STEP = 0.001, 0.9, 0.999, 1e-08, 0.01, 10

C_FQ, C_FK, C_FV = 0, 512, 1024
C_SB, C_SC, C_SV = 1536, 2048, 2560
C_DQ, C_DK, C_DV, C_DZ = 3072, 3584, 4096, 4608
C_GATE = 5120
C_SMALL = 8192
IN_P = 8320
IN_ORIG = 8208

WEIGHTS = ['g_mix', 'w_in', 'b_fox_f', 'fox_q_gain', 'fox_k_gain', 'sc_conv_w', 'dn_conv_w', 'dn_a_log',
           'dn_dt_bias', 'dn_norm_gain', 'w_branch', 'w_o', 'g_ffn', 'w_up', 'ffn_conv_w', 'w_down', 'g_ple',
           'w_ple_gate', 'w_ple']
BIG = {'w_in': 1, 'w_branch': 2, 'w_o': 0, 'w_up': 1, 'w_down': 0, 'w_ple_gate': 0, 'w_ple': 1}
CONVW = {'sc_conv_w': 1, 'dn_conv_w': 1, 'ffn_conv_w': 1}
SHARDED = {**BIG, **CONVW}
SMALL = [n for n in WEIGHTS if n not in SHARDED]
MESH = pl.DeviceIdType.MESH


def _sigmoid(x):
    return 0.5 * (jnp.tanh(0.5 * x) + 1.0)


def _silu(x):
    return x * _sigmoid(x)


def _log1pexp_negabs(z):
    return jnp.log(1.0 + jnp.exp(-jnp.abs(z)))


def _log_sigmoid(z):
    return jnp.minimum(z, 0.0) - _log1pexp_negabs(z)


def _softplus(z):
    return jnp.maximum(z, 0.0) + _log1pexp_negabs(z)


def _rms(x, g):
    return x * lax.rsqrt(jnp.mean(x * x, axis=-1, keepdims=True) + EPS) * g


def _l2(x):
    return x * lax.rsqrt(jnp.sum(x * x, axis=-1, keepdims=True) + EPS)


def _dot(a, b, dims, precision=None):
    return lax.dot_general(a, b, (dims, ((), ())), preferred_element_type=F32, precision=precision)


NN = ((1,), (0,))
NT = ((1,), (1,))
TN = ((0,), (0,))


def _shift_down(x, s):
    if s == 0:
        return x
    t = lax.broadcasted_iota(jnp.int32, x.shape, 0)
    return jnp.where(t >= s, pltpu.roll(x, s, 0), 0.0)


def _shift_up(x, s):
    if s == 0:
        return x
    n = x.shape[0]
    t = lax.broadcasted_iota(jnp.int32, x.shape, 0)
    return jnp.where(t < n - s, pltpu.roll(x, n - s, 0), 0.0)


def _conv(x, w):
    k = w.shape[0]
    y = w[k - 1:k] * x
    for j in range(k - 1):
        y = y + w[j:j + 1] * _shift_down(x, k - 1 - j)
    return y


def _conv_bwd(x, w, dy):
    k = w.shape[0]
    dx = w[k - 1:k] * dy
    dws = []
    for j in range(k - 1):
        dx = dx + w[j:j + 1] * _shift_up(dy, k - 1 - j)
        dws.append(jnp.sum(dy * _shift_down(x, k - 1 - j), axis=0, keepdims=True))
    dws.append(jnp.sum(dy * x, axis=0, keepdims=True))
    return dx, dws


MM_VMEM_BUDGET = 36 << 20
MM_STEP_BYTES = 1 << 20


def _mm_tiles(m, n, k, a_size, b_size, tile_size, cast_a):
    best = None
    for tm in [d for d in (2048, 1024, 512, 256, 128) if d <= m and m % d == 0] or [m]:
        for tn in [d for d in range(128, min(n, 2048) + 1, 128) if n % d == 0] or [n]:
            vmem = 2 * tm * k * a_size + (2 * tm * k if cast_a else 0) + 2 * k * tn * b_size + 2 * tm * tn * tile_size
            if vmem > MM_VMEM_BUDGET:
                continue
            steps = (m // tm) * (n // tn)
            cost = m * k * a_size + (m // tm) * k * n * b_size + m * n * tile_size + steps * MM_STEP_BYTES
            if best is None or cost < best[0]:
                best = (cost, tm, tn)
    assert best is not None, (m, n, k)
    return best[1], best[2]


def _mm(a, b, mode, outs, *, epi=None, epi_args=(), name):
    if mode == 'nn':
        (m, k), (k2, n) = a.shape, b.shape
    elif mode == 'nt':
        (m, k), (n, k2) = a.shape, b.shape
    else:
        (k, m), (k2, n) = a.shape, b.shape
    assert k == k2, (a.shape, b.shape, mode)
    tile_size = sum(jnp.dtype(dt).itemsize for dt in outs) + sum(e.dtype.itemsize for e in epi_args if e.shape[0] != 1)
    tm, tn = _mm_tiles(m, n, k, a.dtype.itemsize, b.dtype.itemsize, tile_size, a.dtype != BF16)
    dims = {'nn': NN, 'nt': NT, 'tn': TN}[mode]
    a_spec = pl.BlockSpec((k, tm), lambda i, j: (0, i)) if mode == 'tn' else pl.BlockSpec((tm, k), lambda i, j: (i, 0))
    b_spec = pl.BlockSpec((tn, k), lambda i, j: (j, 0)) if mode == 'nt' else pl.BlockSpec((k, tn), lambda i, j: (0, j))
    e_specs = [pl.BlockSpec((1, tn), lambda i, j: (0, j)) if e.shape[0] == 1 else pl.BlockSpec((tm, tn), lambda i, j: (i, j))
               for e in epi_args]
    ne, no = len(epi_args), len(outs)
    cast_a = a.dtype != BF16

    def body(a_ref, b_ref, *rest):
        if cast_a:
            a_sc = rest[-1]

            @pl.when(pl.program_id(1) == 0)
            def _():
                a_sc[...] = a_ref[...].astype(BF16)
            av = a_sc[...]
        else:
            av = a_ref[...]
        acc = _dot(av, b_ref[...].astype(BF16), dims)
        vals = epi(acc, *[e[...] for e in rest[:ne]]) if epi is not None else (acc,)
        for o_ref, v in zip(rest[ne:ne + no], vals):
            o_ref[...] = v.astype(o_ref.dtype)

    res = pl.pallas_call(
        body, grid=(m // tm, n // tn),
        in_specs=[a_spec, b_spec] + e_specs,
        out_specs=[pl.BlockSpec((tm, tn), lambda i, j: (i, j)) for _ in outs],
        out_shape=[jax.ShapeDtypeStruct((m, n), dt) for dt in outs],
        scratch_shapes=[pltpu.VMEM(a_spec.block_shape, BF16)] if cast_a else [],
        name=name,
    )(a, b, *epi_args)
    return res[0] if no == 1 else res


def _ew(fn, tiled, bcast, outs, reds=(), *, tm=256, name):
    secs = [(t, 0, t.shape[1]) if not isinstance(t, tuple) else t for t in tiled]
    m = secs[0][0].shape[0]
    tm = min(tm, m)
    assert m % tm == 0
    in_specs = []
    for arr, off, w in secs:
        assert off % w == 0
        in_specs.append(pl.BlockSpec((tm, w), functools.partial(lambda i, c: (i, c), c=off // w)))
    in_specs += [pl.BlockSpec(b.shape, lambda i: (0, 0)) for b in bcast]
    nin, no = len(in_specs), len(outs)

    def body(*refs):
        vals = fn(*[r[...] for r in refs[:nin]])
        for r, v in zip(refs[nin:nin + no], vals[:no]):
            r[...] = v.astype(r.dtype)
        i = pl.program_id(0)
        for r, v in zip(refs[nin + no:], vals[no:]):
            @pl.when(i == 0)
            def _():
                r[...] = v

            @pl.when(i > 0)
            def _():
                r[...] += v

    res = pl.pallas_call(
        body, grid=(m // tm,), in_specs=in_specs,
        out_specs=[pl.BlockSpec((tm, c), lambda i: (i, 0)) for c, _ in outs] + [pl.BlockSpec(s, lambda i: (0, 0)) for s in reds],
        out_shape=[jax.ShapeDtypeStruct((m, c), dt) for c, dt in outs] + [jax.ShapeDtypeStruct(s, F32) for s in reds],
        name=name,
    )(*[s[0] for s in secs], *bcast)
    return res


def _cb(fn, cols, params, outs, pouts, *, tc, nblk, name):
    t = cols[0][0].shape[0]
    in_specs = []
    for arr, off in cols:
        assert off % tc == 0
        in_specs.append(pl.BlockSpec((t, tc), functools.partial(lambda c, o: (0, o + c), o=off // tc)))
    for arr, off in params:
        in_specs.append(pl.BlockSpec((arr.shape[0], tc), functools.partial(lambda c, o: (0, o + c), o=off // tc)))
    nin, no = len(in_specs), len(outs)

    def body(*refs):
        vals = fn(*[r[...] for r in refs[:nin]])
        for r, v in zip(refs[nin:], vals):
            r[...] = v.astype(r.dtype)

    return pl.pallas_call(
        body, grid=(nblk,), in_specs=in_specs,
        out_specs=[pl.BlockSpec((t, tc), lambda c: (0, c)) for _ in outs] + [pl.BlockSpec((k, tc), lambda c: (0, c)) for k in pouts],
        out_shape=[jax.ShapeDtypeStruct((t, nblk * tc), dt) for dt in outs] + [jax.ShapeDtypeStruct((k, nblk * tc), F32) for k in pouts],
        name=name,
    )(*[c[0] for c in cols], *[p[0] for p in params])


CUM_BLK = 256


def _prep_point(s, bias, alog):
    lane = lax.broadcasted_iota(jnp.int32, s.shape, 1)
    z = s + bias
    return jnp.where(lane < 8, _log_sigmoid(z),
                     jnp.where(lane < 12, _sigmoid(s),
                               jnp.where(lane < 16, -jnp.exp(alog) * _softplus(z), 0.0)))


def _tri(n, upper):
    r = lax.broadcasted_iota(jnp.int32, (n, n), 0)
    c = lax.broadcasted_iota(jnp.int32, (n, n), 1)
    return (r <= c if upper else r >= c).astype(F32)


def _prep_fwd(proj, bias_row, alog_row, name):
    t = proj.shape[0]
    nb = t // CUM_BLK

    def body(s_ref, b_ref, a_ref, o_ref):
        pre = _prep_point(s_ref[...], b_ref[...], a_ref[...])
        lane = lax.broadcasted_iota(jnp.int32, (CUM_BLK, 128), 1)
        tri = _tri(CUM_BLK, False)
        carry = jnp.zeros((1, 128), F32)
        for blk in range(nb):
            xb = pre[blk * CUM_BLK:(blk + 1) * CUM_BLK]
            cb = _dot(tri, xb, NN, HI) + carry
            carry = cb[CUM_BLK - 1:CUM_BLK]
            o_ref[blk * CUM_BLK:(blk + 1) * CUM_BLK, :] = jnp.where(lane < 8, cb, xb)

    return pl.pallas_call(
        body, grid=(1,),
        in_specs=[pl.BlockSpec((t, 128), lambda i: (0, C_SMALL // 128)), pl.BlockSpec((1, 128), lambda i: (0, 0)),
                  pl.BlockSpec((1, 128), lambda i: (0, 0))],
        out_specs=pl.BlockSpec((t, 128), lambda i: (0, 0)),
        out_shape=jax.ShapeDtypeStruct((t, 128), F32), name=name,
    )(proj, bias_row, alog_row)


def _prep_bwd(proj, bias_row, alog_row, daux, name):
    t = proj.shape[0]
    nb = t // CUM_BLK

    def body(s_ref, b_ref, a_ref, d_ref, ds_ref, db_ref, da_ref, dpre_sc):
        lane = lax.broadcasted_iota(jnp.int32, (CUM_BLK, 128), 1)
        tri = _tri(CUM_BLK, True)
        carry = jnp.zeros((1, 128), F32)
        for blk in reversed(range(nb)):
            db = d_ref[blk * CUM_BLK:(blk + 1) * CUM_BLK, :]
            cb = _dot(tri, db, NN, HI) + carry
            carry = cb[0:1]
            dpre_sc[blk * CUM_BLK:(blk + 1) * CUM_BLK, :] = jnp.where(lane < 8, cb, db)
        _, vjp = jax.vjp(_prep_point, s_ref[...], b_ref[...], a_ref[...])
        ds, dbias, dalog = vjp(dpre_sc[...])
        ds_ref[...] = ds.astype(ds_ref.dtype)
        db_ref[...] = dbias
        da_ref[...] = dalog

    return pl.pallas_call(
        body, grid=(1,),
        in_specs=[pl.BlockSpec((t, 128), lambda i: (0, C_SMALL // 128)), pl.BlockSpec((1, 128), lambda i: (0, 0)),
                  pl.BlockSpec((1, 128), lambda i: (0, 0)), pl.BlockSpec((t, 128), lambda i: (0, 0))],
        out_specs=[pl.BlockSpec((t, 128), lambda i: (0, 0)), pl.BlockSpec((1, 128), lambda i: (0, 0)),
                   pl.BlockSpec((1, 128), lambda i: (0, 0))],
        out_shape=[jax.ShapeDtypeStruct((t, 128), BF16), jax.ShapeDtypeStruct((1, 128), F32), jax.ShapeDtypeStruct((1, 128), F32)],
        scratch_shapes=[pltpu.VMEM((t, 128), F32)], name=name,
    )(proj, bias_row, alog_row, daux)


ATT_TQ = 256
FOX_SCALE = FOX_DH ** -0.5


def _qnorm(q, g):
    return _rms(q, g) * FOX_SCALE


def _att_scores(qn_blk, kn, cfc_blk, cfr, qi, tq, kend):
    s = _dot(qn_blk.astype(BF16), kn[:kend].astype(BF16), NT) + cfc_blk - cfr[:, :kend]
    row = lax.broadcasted_iota(jnp.int32, (tq, kend), 0) + qi * tq
    col = lax.broadcasted_iota(jnp.int32, (tq, kend), 1)
    return s, row >= col


ATT_PAIR = 128 // FOX_DH


def _att_specs(t):
    pair = lambda off: pl.BlockSpec((t, 128), functools.partial(lambda i, o: (0, o + i), o=off // 128))
    gain = pl.BlockSpec((1, FOX_DH), lambda i: (0, 0))
    hd = lambda i: (i, 0, 0)
    col, row = pl.BlockSpec((ATT_PAIR, t, 1), hd), pl.BlockSpec((ATT_PAIR, 1, t), hd)
    return [pair(C_FQ), pair(C_FK), pair(C_FV), gain, gain, col, row], pl.BlockSpec((t, 128), lambda i: (0, i)), col, row


def _att_fwd(proj, qg, kg, cfc, cfr, name):
    t = proj.shape[0]
    tq = min(ATT_TQ, t)
    in_specs, pair_out, col, _ = _att_specs(t)

    def body(q_ref, k_ref, v_ref, qg_ref, kg_ref, cfc_ref, cfr_ref, o_ref, lse_ref):
        for e in range(ATT_PAIR):
            lanes = slice(e * FOX_DH, (e + 1) * FOX_DH)
            qn = _qnorm(q_ref[:, lanes], qg_ref[...])
            kn = _rms(k_ref[:, lanes], kg_ref[...])
            v = v_ref[:, lanes].astype(BF16)
            cfr = cfr_ref[e]
            for qi in range(t // tq):
                kend = (qi + 1) * tq
                rows = slice(qi * tq, kend)
                s, mask = _att_scores(qn[rows], kn, cfc_ref[e, rows, :], cfr, qi, tq, kend)
                s = jnp.where(mask, s, NEG)
                m = jnp.max(s, axis=1, keepdims=True)
                p = jnp.exp(s - m)
                l = jnp.sum(p, axis=1, keepdims=True)
                o_ref[rows, lanes] = _dot(p.astype(BF16), v[:kend], NN) / l
                lse_ref[e, rows, :] = m + jnp.log(l)

    return pl.pallas_call(
        body, grid=(FOX_HEADS // ATT_PAIR,), in_specs=in_specs, out_specs=[pair_out, col],
        out_shape=[jax.ShapeDtypeStruct((t, BW), F32), jax.ShapeDtypeStruct((FOX_HEADS, t, 1), F32)], name=name,
    )(proj, proj, proj, qg, kg, cfc, cfr)


def _att_bwd(proj, qg, kg, cfc, cfr, lse, o, do, name):
    t = proj.shape[0]
    tq = min(ATT_TQ, t)
    in_specs, pair_out, col, row = _att_specs(t)

    def body(q_ref, k_ref, v_ref, qg_ref, kg_ref, cfc_ref, cfr_ref, lse_ref, o_ref, do_ref,
             dq_ref, dk_ref, dv_ref, dcfc_ref, dcfr_ref, dqg_ref, dkg_ref, dqn_sc, dkn_sc, dv_sc, dcfr_sc):
        for e in range(ATT_PAIR):
            lanes = slice(e * FOX_DH, (e + 1) * FOX_DH)
            qn, vjp_q = jax.vjp(_qnorm, q_ref[:, lanes], qg_ref[...])
            kn, vjp_k = jax.vjp(_rms, k_ref[:, lanes], kg_ref[...])
            v = v_ref[:, lanes].astype(BF16)
            cfr = cfr_ref[e]
            do_e = do_ref[:, lanes]
            delta = jnp.sum(do_e * o_ref[:, lanes], axis=1, keepdims=True)
            dkn_sc[...] = jnp.zeros_like(dkn_sc)
            dv_sc[...] = jnp.zeros_like(dv_sc)
            dcfr_sc[...] = jnp.zeros_like(dcfr_sc)
            for qi in range(t // tq):
                kend = (qi + 1) * tq
                rows = slice(qi * tq, kend)
                s, mask = _att_scores(qn[rows], kn, cfc_ref[e, rows, :], cfr, qi, tq, kend)
                p = jnp.where(mask, jnp.exp(jnp.where(mask, s, NEG) - lse_ref[e, rows, :]), 0.0)
                do_b = do_e[rows].astype(BF16)
                dv_sc[0:kend, :] += _dot(p.astype(BF16), do_b, TN)
                dp = _dot(do_b, v[:kend], NT)
                ds = p * (dp - delta[rows])
                ds_b = ds.astype(BF16)
                dqn_sc[rows, :] = _dot(ds_b, kn[:kend].astype(BF16), NN)
                dkn_sc[0:kend, :] += _dot(ds_b, qn[rows].astype(BF16), TN)
                dcfc_ref[e, rows, :] = jnp.sum(ds, axis=1, keepdims=True)
                dcfr_sc[:, 0:kend] -= jnp.sum(ds, axis=0, keepdims=True)
            dq, dqg = vjp_q(dqn_sc[...])
            dk, dkg = vjp_k(dkn_sc[...])
            dq_ref[:, lanes] = dq.astype(dq_ref.dtype)
            dk_ref[:, lanes] = dk.astype(dk_ref.dtype)
            dv_ref[:, lanes] = dv_sc[...].astype(dv_ref.dtype)
            dcfr_ref[e] = dcfr_sc[...]
            dqg_ref[e] = dqg
            dkg_ref[e] = dkg

    gsp = pl.BlockSpec((ATT_PAIR, 1, FOX_DH), lambda i: (i, 0, 0))
    return pl.pallas_call(
        body, grid=(FOX_HEADS // ATT_PAIR,),
        in_specs=in_specs + [col, pair_out, pair_out],
        out_specs=[pair_out] * 3 + [col, row, gsp, gsp],
        out_shape=[jax.ShapeDtypeStruct((t, BW), BF16)] * 3
        + [jax.ShapeDtypeStruct((FOX_HEADS, t, 1), F32), jax.ShapeDtypeStruct((FOX_HEADS, 1, t), F32)]
        + [jax.ShapeDtypeStruct((FOX_HEADS, 1, FOX_DH), F32)] * 2,
        scratch_shapes=[pltpu.VMEM((t, FOX_DH), F32)] * 3 + [pltpu.VMEM((1, t), F32)], name=name,
    )(proj, proj, proj, qg, kg, cfc, cfr, lse, o, do)


DN_SCALE = DN_DH ** -0.5


DN_BATCH = 8


@functools.partial(jax.custom_vjp, nondiff_argnums=(2, 3))
def _mm3(a, b, dims, batch=False):
    return _mm3_passes(a, b, dims, batch)


def _mm3_fwd(a, b, dims, batch):
    return _mm3_passes(a, b, dims, batch), (a, b)


def _mm3_bwd(dims, batch, res, dc):
    a, b = res
    if dims == NN:
        return _mm3_passes(dc, b, NT, batch), _mm3_passes(a, dc, TN, batch)
    if dims == NT:
        return _mm3_passes(dc, b, NN, batch), _mm3_passes(dc, a, TN, batch)
    return _mm3_passes(b, dc, NT, batch), _mm3_passes(a, dc, NN, batch)


_mm3.defvjp(_mm3_fwd, _mm3_bwd)


def _mm3_passes(a, b, dims, batch):
    if batch:
        dn = (((dims[0][0] + 1,), (dims[1][0] + 1,)), ((0,), (0,)))
        dot = lambda p, q: lax.dot_general(p, q, dn, preferred_element_type=F32)
    else:
        dot = lambda p, q: _dot(p, q, dims)
    ah, bh = a.astype(BF16), b.astype(BF16)
    al, bl = (a - ah.astype(F32)).astype(BF16), (b - bh.astype(F32)).astype(BF16)
    return dot(ah, bh) + (dot(ah, bl) + dot(al, bh))


def _dn_local(qc, kc, vc, g, beta):
    nb, c, _ = qc.shape
    ii = lax.broadcasted_iota(jnp.int32, (c, c), 0)
    jj = lax.broadcasted_iota(jnp.int32, (c, c), 1)
    incl, strict = ii >= jj, ii > jj
    lower = jnp.broadcast_to(incl.astype(F32), (nb, c, c))
    eye = (ii == jj).astype(F32)
    mm = functools.partial(_mm3, batch=True)
    dm = mm(lower, jnp.where(strict, g, 0.0), NN)
    decay = jnp.where(incl, jnp.exp(jnp.where(incl, dm, 0.0)), 0.0)
    gcum = mm(lower, g * jnp.ones((1, 1, DN_DH), F32), NN)
    eg = jnp.exp(gcum)
    glast = gcum[:, c - 1:c]
    kb = kc * beta
    n1 = jnp.where(strict, mm(kb, kc, NT) * decay, 0.0)
    inv = eye - n1
    pw = n1
    for _ in range(5):
        pw = mm(pw, pw, NN)
        inv = inv + mm(pw, inv, NN)
    sol = mm(inv, jnp.concatenate([vc * beta, kb * eg], axis=2), NN)
    qk = jnp.where(incl, mm(qc, kc, NT) * decay, 0.0)
    return sol[:, :, :DN_DH], sol[:, :, DN_DH:], qk, qc * eg, kc * jnp.exp(glast - gcum), jnp.exp(glast)


def _dn_state(u, kcum, qk, qdec, kdec, egl, state):
    v_new = u - _mm3(kcum, state, NN)
    out = _mm3(qdec, state, NN) + _mm3(qk, v_new, NN)
    return out, state * egl + _mm3(kdec, v_new, TN)


def _dn_pre_q(c):
    return _l2(_silu(c)) * DN_SCALE


def _dn_pre_k(c):
    return _l2(_silu(c))


def _dn_post(o, z, ng):
    return _rms(o, ng) * _silu(z)


def _dn_specs(t):
    cblk = lambda o: pl.BlockSpec((t, DN_DH), functools.partial(lambda h, o: (0, o + h), o=o // DN_DH))
    wblk = lambda o: pl.BlockSpec((4, DN_DH), functools.partial(lambda h, o: (0, o + h), o=o // DN_DH))
    proj_specs = [cblk(C_DQ), cblk(C_DK), cblk(C_DV), cblk(C_DZ)]
    w_specs = [wblk(0), wblk(BW), wblk(2 * BW)]
    gb_spec = pl.BlockSpec((1, t, 2), lambda h: (h, 0, 0))
    return proj_specs, w_specs, gb_spec


def _chunk_rows(n, count=1):
    return pl.ds(pl.multiple_of(n * DN_CHUNK, DN_CHUNK), count * DN_CHUNK)


def _egl_rows(n, count=1):
    return pl.ds(pl.multiple_of(n * 8, 8), count * 8)


def _dn_local_inputs(n, qn_sc, kn_sc, vv_sc, gb_ref):
    r = _chunk_rows(n, DN_BATCH)
    split = lambda v: v.reshape(DN_BATCH, DN_CHUNK, v.shape[-1])
    gbv = split(gb_ref[0, r, :])
    return split(qn_sc[r, :]), split(kn_sc[r, :]), split(vv_sc[r, :]), gbv[:, :, 0:1], gbv[:, :, 1:2]


def _dn_local_phase(nc, qn_sc, kn_sc, vv_sc, gb_ref, loc):
    def step(i, carry):
        n = i * DN_BATCH
        vals = _dn_local(*_dn_local_inputs(n, qn_sc, kn_sc, vv_sc, gb_ref))
        for sc, val in zip(loc[:5], vals[:5]):
            sc[_chunk_rows(n, DN_BATCH), :] = val.reshape(DN_BATCH * DN_CHUNK, val.shape[-1])
        loc[5][_egl_rows(n, DN_BATCH), :] = jnp.broadcast_to(vals[5], (DN_BATCH, 8, DN_DH)).reshape(DN_BATCH * 8, DN_DH)
        return carry

    lax.fori_loop(0, nc // DN_BATCH, step, 0)


def _dn_loc_scratch(t, nc):
    big = pltpu.VMEM((t, DN_DH), F32)
    return [big, big, pltpu.VMEM((t, DN_CHUNK), F32), big, big, pltpu.VMEM((nc * 8, DN_DH), F32)]


def _dn_fwd(proj, conv_w, gb, ng, name):
    t = proj.shape[0]
    nc = t // DN_CHUNK
    assert nc % DN_BATCH == 0
    proj_specs, w_specs, gb_spec = _dn_specs(t)

    def body(q_ref, k_ref, v_ref, z_ref, wq_ref, wk_ref, wv_ref, gb_ref, ng_ref, y_ref, o_ref, st_ref, qn_sc, kn_sc, vv_sc, *loc):
        qn_sc[...] = _dn_pre_q(_conv(q_ref[...], wq_ref[...]))
        kn_sc[...] = _dn_pre_k(_conv(k_ref[...], wk_ref[...]))
        vv_sc[...] = _silu(_conv(v_ref[...], wv_ref[...]))
        _dn_local_phase(nc, qn_sc, kn_sc, vv_sc, gb_ref, loc)
        u_sc, kcum_sc, qk_sc, qdec_sc, kdec_sc, egl_sc = loc

        def chunk(n, state):
            r = _chunk_rows(n)
            egl = egl_sc[_egl_rows(n), :][0:1]
            out, new_state = _dn_state(u_sc[r, :], kcum_sc[r, :], qk_sc[r, :], qdec_sc[r, :], kdec_sc[r, :], egl, state)
            st_ref[0, n] = state
            o_ref[r, :] = out
            return new_state

        lax.fori_loop(0, nc, chunk, jnp.zeros((DN_DH, DN_DH), F32))
        y_ref[...] = _dn_post(o_ref[...], z_ref[...], ng_ref[...])

    hblk = pl.BlockSpec((t, DN_DH), lambda h: (0, h))
    return pl.pallas_call(
        body, grid=(DN_HEADS,),
        in_specs=proj_specs + w_specs + [gb_spec, pl.BlockSpec((1, DN_DH), lambda h: (0, 0))],
        out_specs=[hblk, hblk, pl.BlockSpec((1, nc, DN_DH, DN_DH), lambda h: (h, 0, 0, 0))],
        out_shape=[jax.ShapeDtypeStruct((t, BW), F32), jax.ShapeDtypeStruct((t, BW), F32),
                   jax.ShapeDtypeStruct((DN_HEADS, nc, DN_DH, DN_DH), F32)],
        scratch_shapes=[pltpu.VMEM((t, DN_DH), F32)] * 3 + _dn_loc_scratch(t, nc), name=name,
    )(proj, proj, proj, proj, conv_w, conv_w, conv_w, gb, ng)


def _dn_bwd(proj, conv_w, gb, ng, o, states, dy, name):
    t = proj.shape[0]
    nc = t // DN_CHUNK
    proj_specs, w_specs, gb_spec = _dn_specs(t)
    nloc = 6

    def body(q_ref, k_ref, v_ref, z_ref, wq_ref, wk_ref, wv_ref, gb_ref, ng_ref, o_ref, st_ref, dy_ref,
             dq_ref, dk_ref, dv_ref, dz_ref, dwq_ref, dwk_ref, dwv_ref, dgb_ref, dng_ref,
             qn_sc, kn_sc, vv_sc, do_sc, *rest):
        loc, dloc = rest[:nloc], rest[nloc:]
        qn_sc[...] = _dn_pre_q(_conv(q_ref[...], wq_ref[...]))
        kn_sc[...] = _dn_pre_k(_conv(k_ref[...], wk_ref[...]))
        vv_sc[...] = _silu(_conv(v_ref[...], wv_ref[...]))
        _, vjp_y = jax.vjp(_dn_post, o_ref[...], z_ref[...], ng_ref[...])
        do, dz, dng = vjp_y(dy_ref[...])
        do_sc[...] = do
        dz_ref[...] = dz.astype(dz_ref.dtype)
        dng_ref[0] = dng
        _dn_local_phase(nc, qn_sc, kn_sc, vv_sc, gb_ref, loc)
        u_sc, kcum_sc, qk_sc, qdec_sc, kdec_sc, egl_sc = loc

        def state_bwd(i, dstate):
            n = nc - 1 - i
            r = _chunk_rows(n)
            r8 = _egl_rows(n)
            _, vjp = jax.vjp(_dn_state, u_sc[r, :], kcum_sc[r, :], qk_sc[r, :], qdec_sc[r, :], kdec_sc[r, :],
                             egl_sc[r8, :][0:1], st_ref[0, n])
            du, dkcum, dqk, dqdec, dkdec, degl, dprev = vjp((do_sc[r, :], dstate))
            for d_sc, val in zip(dloc[:5], (du, dkcum, dqk, dqdec, dkdec)):
                d_sc[r, :] = val
            dloc[5][r8, :] = jnp.broadcast_to(degl, (8, DN_DH))
            return dprev

        lax.fori_loop(0, nc, state_bwd, jnp.zeros((DN_DH, DN_DH), F32))

        def local_bwd(i, carry):
            n = i * DN_BATCH
            r = _chunk_rows(n, DN_BATCH)
            _, vjp = jax.vjp(_dn_local, *_dn_local_inputs(n, qn_sc, kn_sc, vv_sc, gb_ref))
            cts = tuple(d_sc[r, :].reshape(DN_BATCH, DN_CHUNK, d_sc.shape[-1]) for d_sc in dloc[:5])
            cts += (dloc[5][_egl_rows(n, DN_BATCH), :].reshape(DN_BATCH, 8, DN_DH)[:, 0:1],)
            dqc, dkc, dvc, dg, dbeta = vjp(cts)
            for d_sc, val in zip((dloc[0], dloc[1], dloc[3]), (dqc, dkc, dvc)):
                d_sc[r, :] = val.reshape(DN_BATCH * DN_CHUNK, DN_DH)
            dgb_ref[0, r, :] = jnp.concatenate([dg, dbeta], axis=2).reshape(DN_BATCH * DN_CHUNK, 2)
            return carry

        lax.fori_loop(0, nc // DN_BATCH, local_bwd, 0)
        for x_ref, w_ref, pre, d_sc, dx_ref, dw_ref in ((q_ref, wq_ref, _dn_pre_q, dloc[0], dq_ref, dwq_ref),
                                                       (k_ref, wk_ref, _dn_pre_k, dloc[1], dk_ref, dwk_ref),
                                                       (v_ref, wv_ref, _silu, dloc[3], dv_ref, dwv_ref)):
            _, vjp = jax.vjp(pre, _conv(x_ref[...], w_ref[...]))
            (dc,) = vjp(d_sc[...])
            dx, dws = _conv_bwd(x_ref[...], w_ref[...], dc)
            dx_ref[...] = dx.astype(dx_ref.dtype)
            for j, dw in enumerate(dws):
                dw_ref[j:j + 1, :] = dw

    hblk = pl.BlockSpec((t, DN_DH), lambda h: (0, h))
    wout = pl.BlockSpec((4, DN_DH), lambda h: (0, h))
    return pl.pallas_call(
        body, grid=(DN_HEADS,),
        in_specs=proj_specs + w_specs + [gb_spec, pl.BlockSpec((1, DN_DH), lambda h: (0, 0)), hblk,
                                         pl.BlockSpec((1, nc, DN_DH, DN_DH), lambda h: (h, 0, 0, 0)), hblk],
        out_specs=[hblk] * 4 + [wout] * 3 + [gb_spec, pl.BlockSpec((1, 1, DN_DH), lambda h: (h, 0, 0))],
        out_shape=[jax.ShapeDtypeStruct((t, BW), BF16)] * 4 + [jax.ShapeDtypeStruct((4, BW), F32)] * 3
        + [jax.ShapeDtypeStruct((DN_HEADS, t, 2), F32), jax.ShapeDtypeStruct((DN_HEADS, 1, DN_DH), F32)],
        scratch_shapes=[pltpu.VMEM((t, DN_DH), F32)] * 4 + _dn_loc_scratch(t, nc) * 2, name=name,
    )(proj, proj, proj, proj, conv_w, conv_w, conv_w, gb, ng, o, states, dy)


MERGE_TM, MERGE_TN = 512, 512


def _merge_specs(t):
    tm, tn = min(MERGE_TM, t), MERGE_TN
    y_spec = pl.BlockSpec((tm, BW), lambda i, j: (i, 0))
    w_spec = pl.BlockSpec((3, BW, tn), lambda i, j: (0, 0, j))
    gate_specs = [pl.BlockSpec((tm, tn), functools.partial(lambda i, j, o: (i, o + j), o=(C_GATE + n * D_MODEL) // tn))
                  for n in range(3)]
    return tm, tn, [y_spec] * 3 + [w_spec] + gate_specs


def _merge_fwd(ys, wb, proj, name):
    t = proj.shape[0]
    tm, tn, in_specs = _merge_specs(t)

    def body(y0, y1, y2, w_ref, g0, g1, g2, o_ref):
        acc = jnp.zeros((tm, tn), F32)
        for n, (y, g) in enumerate(((y0, g0), (y1, g1), (y2, g2))):
            acc = acc + _dot(y[...].astype(BF16), w_ref[n], NN) * _sigmoid(g[...])
        o_ref[...] = acc.astype(o_ref.dtype)

    return pl.pallas_call(
        body, grid=(t // tm, D_MODEL // tn), in_specs=in_specs,
        out_specs=pl.BlockSpec((tm, tn), lambda i, j: (i, j)),
        out_shape=jax.ShapeDtypeStruct((t, D_MODEL), BF16), name=name,
    )(*ys, wb, proj, proj, proj)


def _merge_bwd(ys, wb, proj, dmerged, name):
    t = proj.shape[0]
    tm, tn, in_specs = _merge_specs(t)

    def body(y0, y1, y2, w_ref, g0, g1, g2, dm_ref, dg_ref, dt_ref):
        dm = dm_ref[...]
        for n, (y, g) in enumerate(((y0, g0), (y1, g1), (y2, g2))):
            tn_ = _dot(y[...].astype(BF16), w_ref[n], NN)
            sg = _sigmoid(g[...])
            dg_ref[n] = (dm * tn_ * sg * (1.0 - sg)).astype(dg_ref.dtype)
            dt_ref[n] = (dm * sg).astype(dt_ref.dtype)

    o3 = pl.BlockSpec((3, tm, tn), lambda i, j: (0, i, j))
    return pl.pallas_call(
        body, grid=(t // tm, D_MODEL // tn), in_specs=in_specs + [pl.BlockSpec((tm, tn), lambda i, j: (i, j))],
        out_specs=[o3, o3], out_shape=[jax.ShapeDtypeStruct((3, t, D_MODEL), BF16)] * 2, name=name,
    )(*ys, wb, proj, proj, proj, dmerged)


def _prep_rows(sp):
    z4, z112 = jnp.zeros((1, 4), F32), jnp.zeros((1, 112), F32)
    bias_row = jnp.concatenate([sp['b_fox_f'], z4, sp['dn_dt_bias'], z112], axis=1)
    alog_row = jnp.concatenate([jnp.zeros((1, 12), F32), sp['dn_a_log'], z112], axis=1)
    return bias_row, alog_row


def _sc_fwd_tile(sb, sc, sv, w):
    return (sb * _conv(sc * sv, w),)


def _ffn_act(ug, uv):
    return _silu(ug) * uv


W_IN_COLS = [(C_FQ, 0, 1536), (C_SB, 1544, 1536), (C_DQ, 3080, 1536), (C_DZ, 4624, 512), (C_GATE, 5136, 3072),
             (C_SMALL, 1536, 8), (C_SMALL + 8, 4616, 8)]
W_IN_SHARD = IN_ORIG // N_DEV


def _w_in_segments():
    out = []
    for d, o, w in W_IN_COLS:
        end = o + w
        while o < end:
            k = o // W_IN_SHARD
            n = min(end, (k + 1) * W_IN_SHARD) - o
            out.append((d, k, o - k * W_IN_SHARD, n))
            o, d = o + n, d + n
    return out


def _w_in_assemble(g, name):
    tm = 256

    def body(g_ref, o_ref):
        for d, k, s, n in _w_in_segments():
            o_ref[:, d:d + n] = g_ref[k, :, s:s + n]
        o_ref[:, IN_ORIG:IN_P] = jnp.zeros((tm, IN_P - IN_ORIG), o_ref.dtype)

    return pl.pallas_call(
        body, grid=(D_MODEL // tm,),
        in_specs=[pl.BlockSpec((N_DEV, tm, W_IN_SHARD), lambda i: (0, i, 0))],
        out_specs=pl.BlockSpec((tm, IN_P), lambda i: (i, 0)),
        out_shape=jax.ShapeDtypeStruct((D_MODEL, IN_P), g.dtype), name=name,
    )(g)


W_IN_SPLIT_TM = 128


def _w_in_split_by_core(g, name):
    tm = W_IN_SPLIT_TM

    def scatter(g_ref, keep_ref, give_ref, core):
        for d, k, s, n in _w_in_segments():
            val = g_ref[:, d:d + n]
            if k % 2 == core:
                keep_ref[k // 2, :, s:s + n] = val
            else:
                give_ref[k // 2, :, s:s + n] = val.astype(give_ref.dtype)

    def body(g_ref, keep_ref, give_ref):
        c = lax.axis_index("c")
        for core in range(2):
            pl.when(c == core)(functools.partial(scatter, g_ref, keep_ref, give_ref, core))

    blk = pl.BlockSpec((4, tm, W_IN_SHARD), lambda i: (0, i, 0))
    shape = (4, D_MODEL, W_IN_SHARD)
    return pl.pallas_call(
        body, grid=(D_MODEL // tm,), in_specs=[pl.BlockSpec((tm, IN_P), lambda i: (i, 0))],
        out_specs=[blk, blk], out_shape=[jax.ShapeDtypeStruct(shape, F32), jax.ShapeDtypeStruct(shape, BF16)], name=name,
    )(g)


def _w_in_split_by_device(g, name):
    tm = W_IN_SPLIT_TM

    def body(g_ref, own_ref, send_ref):
        me = 4 * lax.axis_index("x") + 2 * lax.axis_index("y") + lax.axis_index("c")
        for d, k, s, n in _w_in_segments():
            val = g_ref[:, d:d + n]
            send_ref[k, :, s:s + n] = val.astype(send_ref.dtype)

            @pl.when(me == k)
            def _():
                own_ref[:, s:s + n] = val

    return pl.pallas_call(
        body, grid=(D_MODEL // tm,), in_specs=[pl.BlockSpec((tm, IN_P), lambda i: (i, 0))],
        out_specs=[pl.BlockSpec((tm, W_IN_SHARD), lambda i: (i, 0)), pl.BlockSpec((N_DEV, tm, W_IN_SHARD), lambda i: (0, i, 0))],
        out_shape=[jax.ShapeDtypeStruct((D_MODEL, W_IN_SHARD), F32), jax.ShapeDtypeStruct((N_DEV, D_MODEL, W_IN_SHARD), BF16)],
        name=name,
    )(g)


def _layer_fwd(x, p_i, w, sp, tag, late_weights=None):
    t = x.shape[0]
    sv = {'x': x}
    (hn,) = _ew(lambda a, g: (_rms(a, g),), [x], [sp['g_mix']], [(D_MODEL, BF16)], name=f'rms_mix_{tag}')
    proj = _mm(hn, w['w_in'], 'nn', [F32], name=f'in_proj_{tag}')
    bias_row, alog_row = _prep_rows(sp)
    aux = _prep_fwd(proj, bias_row, alog_row, f'prep_{tag}')
    cf = aux[:, :FOX_HEADS].T
    cfc, cfr = cf[:, :, None], cf[:, None, :]
    y_fox, lse = _att_fwd(proj, sp['fox_q_gain'], sp['fox_k_gain'], cfc, cfr, f'fox_fwd_{tag}')
    (y_sc,) = _cb(_sc_fwd_tile, [(proj, C_SB), (proj, C_SC), (proj, C_SV)], [(w['sc_conv_w'], 0)], [F32], [],
                  tc=256, nblk=2, name=f'sc_fwd_{tag}')
    gb = jnp.stack([aux[:, 12:16].T, aux[:, 8:12].T], axis=-1)
    y_dn, o_dn, states = _dn_fwd(proj, w['dn_conv_w'], gb, sp['dn_norm_gain'], f'dn_fwd_{tag}')
    ys = (y_fox, y_sc, y_dn)
    if late_weights is not None:
        w = {**w, **late_weights(y_dn)}
    merged = _merge_fwd(ys, w['w_branch'], proj, f'merge_fwd_{tag}')
    x1 = _mm(merged, w['w_o'], 'nn', [F32], epi=lambda acc, r: (acc + r,), epi_args=(x,), name=f'o_proj_{tag}')
    (hf,) = _ew(lambda a, g: (_rms(a, g),), [x1], [sp['g_ffn']], [(D_MODEL, BF16)], name=f'rms_ffn_{tag}')
    up = _mm(hf, w['w_up'], 'nn', [F32], name=f'up_proj_{tag}')
    (act,) = _cb(lambda ug, uv, wg, wv: (_ffn_act(_conv(ug, wg), _conv(uv, wv)),), [(up, 0), (up, D_FF)],
                 [(w['ffn_conv_w'], 0), (w['ffn_conv_w'], D_FF)], [BF16], [], tc=256, nblk=D_FF // 256, name=f'ffn_act_{tag}')
    x2 = _mm(act, w['w_down'], 'nn', [F32], epi=lambda acc, r: (acc + r,), epi_args=(x1,), name=f'down_proj_{tag}')
    (hp,) = _ew(lambda a, g: (_rms(a, g),), [x2], [sp['g_ple']], [(D_MODEL, BF16)], name=f'rms_ple_{tag}')
    gp = _mm(hp, w['w_ple_gate'], 'nn', [F32], name=f'ple_gate_{tag}')
    x3 = _mm(p_i, w['w_ple'], 'nn', [F32], epi=lambda acc, g, r: (r + _sigmoid(g) * acc,), epi_args=(gp, x2), name=f'ple_{tag}')
    sv.update(hn=hn, proj=proj, aux=aux, cfc=cfc, cfr=cfr, lse=lse, ys=ys, gb=gb, o_dn=o_dn,
              states=states, merged=merged, x1=x1, hf=hf, up=up, act=act, x2=x2, hp=hp, gp=gp, p=p_i,
              bias_row=bias_row, alog_row=alog_row, w=w)
    return x3, sv


def _rms_bwd(x, g, dh, dres, name):
    def fn(xv, dhv, dr, gv):
        _, vjp = jax.vjp(_rms, xv, gv)
        dx, dg = vjp(dhv)
        return dr + dx, dg
    return _ew(fn, [x, dh, dres], [g], [(D_MODEL, F32)], [(1, D_MODEL)], name=name)


def _layer_bwd(dx3, sv, sp, tag, early_grads=None):
    t = dx3.shape[0]
    w = sv['w']
    g = {}
    def ple_epi(acc, gpv, d):
        s = _sigmoid(gpv)
        return d * acc * s * (1.0 - s), d * s
    dgp, de = _mm(sv['p'], w['w_ple'], 'nn', [BF16, BF16], epi=ple_epi, epi_args=(sv['gp'], dx3), name=f'ple_bwd_{tag}')
    g['w_ple'] = _mm(sv['p'], de, 'tn', [F32], name=f'd_w_ple_{tag}')
    g['w_ple_gate'] = _mm(sv['hp'], dgp, 'tn', [F32], name=f'd_w_ple_gate_{tag}')
    dhp = _mm(dgp, w['w_ple_gate'], 'nt', [F32], name=f'd_hp_{tag}')
    dx2, g['g_ple'] = _rms_bwd(sv['x2'], sp['g_ple'], dhp, dx3, f'rms_ple_bwd_{tag}')
    dact = _mm(dx2, w['w_down'], 'nt', [F32], name=f'd_act_{tag}')
    g['w_down'] = _mm(sv['act'], dx2, 'tn', [F32], name=f'd_w_down_{tag}')

    def ffn_bwd_tile(ug, uv, da, wg, wv):
        cg, cv = _conv(ug, wg), _conv(uv, wv)
        _, vjp = jax.vjp(_ffn_act, cg, cv)
        dcg, dcv = vjp(da)
        dug, dwg = _conv_bwd(ug, wg, dcg)
        duv, dwv = _conv_bwd(uv, wv, dcv)
        return dug, duv, jnp.concatenate(dwg, axis=0), jnp.concatenate(dwv, axis=0)
    dupg, dupv, dwg, dwv = _cb(ffn_bwd_tile, [(sv['up'], 0), (sv['up'], D_FF), (dact, 0)],
                               [(w['ffn_conv_w'], 0), (w['ffn_conv_w'], D_FF)], [BF16, BF16], [3, 3], tc=256,
                               nblk=D_FF // 256, name=f'ffn_act_bwd_{tag}')
    dup = jnp.concatenate([dupg, dupv], axis=1)
    g['ffn_conv_w'] = jnp.concatenate([dwg, dwv], axis=1)
    g['w_up'] = _mm(sv['hf'], dup, 'tn', [F32], name=f'd_w_up_{tag}')
    dhf = _mm(dup, w['w_up'], 'nt', [F32], name=f'd_hf_{tag}')
    dx1, g['g_ffn'] = _rms_bwd(sv['x1'], sp['g_ffn'], dhf, dx2, f'rms_ffn_bwd_{tag}')
    dmerged = _mm(dx1, w['w_o'], 'nt', [F32], name=f'd_merged_{tag}')
    g['w_o'] = _mm(sv['merged'], dx1, 'tn', [F32], name=f'd_w_o_{tag}')
    dgate, dtn = _merge_bwd(sv['ys'], w['w_branch'], sv['proj'], dmerged, f'merge_bwd_{tag}')
    dys, dwb = [], []
    for n in range(3):
        dys.append(_mm(dtn[n], w['w_branch'][n], 'nt', [F32], name=f'd_y{n}_{tag}'))
        dwb.append(_mm(sv['ys'][n], dtn[n], 'tn', [F32], name=f'd_w_branch{n}_{tag}'))
    g['w_branch'] = jnp.stack(dwb)
    if early_grads is not None:
        sp = early_grads(g, sp)
    ddq, ddk, ddv, ddz, dwq, dwk, dwv_, dgb, dng = _dn_bwd(sv['proj'], w['dn_conv_w'], sv['gb'], sp['dn_norm_gain'],
                                                           sv['o_dn'], sv['states'], dys[2], f'dn_bwd_{tag}')
    g['dn_conv_w'] = jnp.concatenate([dwq, dwk, dwv_], axis=1)
    g['dn_norm_gain'] = jnp.sum(dng, axis=0)
    def sc_bwd_tile(sb, sc, svv, dy, wv):
        u = sc * svv
        dsb = dy * _conv(u, wv)
        du, dws = _conv_bwd(u, wv, dy * sb)
        return dsb, du * svv, du * sc, jnp.concatenate(dws, axis=0)
    dsb, dsc, dsv, g['sc_conv_w'] = _cb(sc_bwd_tile, [(sv['proj'], C_SB), (sv['proj'], C_SC), (sv['proj'], C_SV), (dys[1], 0)],
                                        [(w['sc_conv_w'], 0)], [BF16, BF16, BF16], [3], tc=256, nblk=2, name=f'sc_bwd_{tag}')
    dfq, dfk, dfv, dcfc, dcfr, dqg, dkg = _att_bwd(sv['proj'], sp['fox_q_gain'], sp['fox_k_gain'], sv['cfc'], sv['cfr'],
                                                   sv['lse'], sv['ys'][0], dys[0], f'fox_bwd_{tag}')
    g['fox_q_gain'] = jnp.sum(dqg, axis=0)
    g['fox_k_gain'] = jnp.sum(dkg, axis=0)
    dcf = (dcfc[:, :, 0] + dcfr[:, 0, :]).T
    daux = jnp.concatenate([dcf, dgb[:, :, 1].T, dgb[:, :, 0].T, jnp.zeros((t, 112), F32)], axis=1)
    dsmall, dbias, dalog = _prep_bwd(sv['proj'], sv['bias_row'], sv['alog_row'], daux, f'prep_bwd_{tag}')
    g['b_fox_f'] = dbias[:, 0:8]
    g['dn_dt_bias'] = dbias[:, 12:16]
    g['dn_a_log'] = dalog[:, 12:16]
    dproj = jnp.concatenate([dfq, dfk, dfv, dsb, dsc, dsv, ddq, ddk, ddv, ddz, dgate[0], dgate[1], dgate[2], dsmall], axis=1)
    g['w_in'] = _mm(sv['hn'], dproj, 'tn', [F32], name=f'd_w_in_{tag}')
    dhn = _mm(dproj, w['w_in'], 'nt', [F32], name=f'd_hn_{tag}')
    dx, g['g_mix'] = _rms_bwd(sv['x'], sp['g_mix'], dhn, dx1, f'rms_mix_bwd_{tag}')
    return dx, g


def _loss_bwd(y, target, name):
    inv = 1.0 / y.shape[1]

    def fn(yv, tv):
        err = yv - tv
        return err * inv, jnp.zeros((8, 128), F32) + 0.5 * inv * jnp.sum(err * err)
    return _ew(fn, [y, target], [], [(y.shape[1], F32)], [(8, 128)], name=name)


PACK_W = 1024
FULL_SHAPE = {'w_in': (D_MODEL, IN_ORIG), 'w_branch': (3, BW, D_MODEL), 'w_o': (D_MODEL, D_MODEL), 'w_up': (D_MODEL, 2 * D_FF),
              'w_down': (D_FF, D_MODEL), 'w_ple_gate': (D_MODEL, D_MODEL), 'w_ple': (PLE_DIM, D_MODEL),
              'sc_conv_w': (3, BW), 'dn_conv_w': (4, 3 * BW), 'ffn_conv_w': (3, 2 * D_FF)}
SMALL_SHAPE = {'g_mix': D_MODEL, 'b_fox_f': FOX_HEADS, 'fox_q_gain': FOX_DH, 'fox_k_gain': FOX_DH, 'dn_a_log': DN_HEADS,
               'dn_dt_bias': DN_HEADS, 'dn_norm_gain': DN_DH, 'g_ffn': D_MODEL, 'g_ple': D_MODEL}


def _shard_shape(name):
    s = list(FULL_SHAPE[name])
    s[SHARDED[name]] //= N_DEV
    return tuple(s)


def _full_from_gathered(g, name):
    sh, ax = _shard_shape(name), SHARDED[name]
    blocks = jnp.moveaxis(g, 0, ax)
    return blocks.reshape(sh[:ax] + (N_DEV * sh[ax],) + sh[ax + 1:])


def _by_dest(full, name):
    sh, ax = _shard_shape(name), SHARDED[name]
    return jnp.moveaxis(full.reshape(sh[:ax] + (N_DEV, sh[ax]) + sh[ax + 1:]), ax, 0)


def _flat_pack(arrs):
    flat = jnp.concatenate([a.reshape(-1).astype(F32) for a in arrs])
    rows = -(-flat.shape[0] // (8 * PACK_W)) * 8
    return jnp.pad(flat, (0, rows * PACK_W - flat.shape[0])).reshape(rows, PACK_W)


def _flat_unpack(pack, shapes):
    flat, out, off = pack.reshape(-1), [], 0
    for s in shapes:
        n = 1
        for d in s:
            n *= d
        out.append(flat[off:off + n].reshape(s))
        off += n
    return out


SMALL_SHAPES = [(DEPTH, SMALL_SHAPE[n]) for n in SMALL]
SMALL_LOSS_AT = sum(DEPTH * SMALL_SHAPE[n] for n in SMALL)
CONV_SHARD_SHAPES = [(DEPTH,) + _shard_shape(n) for n in CONVW]
CONV_FULL_SHAPES = [(DEPTH,) + FULL_SHAPE[n] for n in CONVW]


def _rows2d(a):
    return a.reshape(-1, a.shape[-1])


def _pick_tm(m, width):
    best = None
    for tm in range(16, m + 1, 16):
        if m % tm == 0 and tm * width * 4 <= (1 << 20):
            best = tm
    return best if best is not None else m


HBM_SPEC = pl.BlockSpec(memory_space=pltpu.HBM)


def _place():
    x, y, c = lax.axis_index("x"), lax.axis_index("y"), lax.axis_index("c")
    return x, y, c, [(1 - x, y), (x, 1 - y), (1 - x, 1 - y)]


def _all_gather(arrs, name):
    n = len(arrs)

    def body(*refs):
        ins, outs = refs[:n], refs[n:2 * n]
        send_sems, recv_sems, local_sems = refs[2 * n:]
        x, y, c, chips = _place()
        me, sibling = (x, y, c), (x, y, 1 - c)

        def block(a, p):
            return outs[a].at[4 * p[0] + 2 * p[1] + p[2]]

        def copy(a, k, blk, to, src=None):
            return pltpu.make_async_remote_copy(src_ref=block(a, blk) if src is None else src, dst_ref=block(a, blk),
                                                send_sem=send_sems.at[a, k], recv_sem=recv_sems.at[a, k],
                                                device_id=to, device_id_type=MESH)

        mine = [pltpu.make_async_copy(ins[a], block(a, me), local_sems.at[a]) for a in range(n)]
        first, passed = [], []
        for a in range(n):
            mine[a].start()
            first.append(copy(a, 0, me, sibling, src=ins[a]))
            first += [copy(a, 1 + j, me, (*chip, c), src=ins[a]) for j, chip in enumerate(chips)]
        for cp in first:
            cp.start()
        for j, chip in enumerate(chips):
            for a in range(n):
                copy(a, 1 + j, (*chip, c), me).wait_recv()
                fwd = copy(a, 4 + j, (*chip, c), sibling)
                fwd.start()
                passed.append(fwd)
        for a in range(n):
            copy(a, 0, sibling, me).wait_recv()
            for j, chip in enumerate(chips):
                copy(a, 4 + j, (*chip, 1 - c), me).wait_recv()
        for cp in first + passed:
            cp.wait_send()
        for a in range(n):
            mine[a].wait()

    return pl.pallas_call(
        body, in_specs=[HBM_SPEC] * n, out_specs=[HBM_SPEC] * n,
        out_shape=[jax.ShapeDtypeStruct((N_DEV,) + a.shape, a.dtype) for a in arrs],
        scratch_shapes=[pltpu.SemaphoreType.DMA((n, 7)), pltpu.SemaphoreType.DMA((n, 7)), pltpu.SemaphoreType.DMA((n,))],
        name=name,
    )(*arrs)


def _sibling_swap(arrs, name):
    n = len(arrs)

    def body(*refs):
        send_sems, recv_sems = refs[2 * n:]
        x, y, c, _ = _place()
        cps = [pltpu.make_async_remote_copy(src_ref=refs[a], dst_ref=refs[n + a], send_sem=send_sems.at[a], recv_sem=recv_sems.at[a],
                                            device_id=(x, y, 1 - c), device_id_type=MESH) for a in range(n)]
        for cp in cps:
            cp.start()
        for cp in cps:
            cp.wait()

    return pl.pallas_call(body, in_specs=[HBM_SPEC] * n, out_specs=[HBM_SPEC] * n,
                          out_shape=[jax.ShapeDtypeStruct(a.shape, a.dtype) for a in arrs],
                          scratch_shapes=[pltpu.SemaphoreType.DMA((n,)), pltpu.SemaphoreType.DMA((n,))], name=name)(*arrs)


def _chip_exchange(arrs, name):
    n = len(arrs)

    def body(*refs):
        send_sems, recv_sems = refs[2 * n:]
        x, y, c, chips = _place()
        cps = [pltpu.make_async_remote_copy(src_ref=refs[a].at[2 * cx + cy], dst_ref=refs[n + a].at[j], send_sem=send_sems.at[a, j],
                                            recv_sem=recv_sems.at[a, j], device_id=(cx, cy, c), device_id_type=MESH)
               for a in range(n) for j, (cx, cy) in enumerate(chips)]
        for cp in cps:
            cp.start()
        for cp in cps:
            cp.wait()

    return pl.pallas_call(body, in_specs=[HBM_SPEC] * n, out_specs=[HBM_SPEC] * n,
                          out_shape=[jax.ShapeDtypeStruct((3,) + a.shape[1:], a.dtype) for a in arrs],
                          scratch_shapes=[pltpu.SemaphoreType.DMA((n, 3)), pltpu.SemaphoreType.DMA((n, 3))], name=name)(*arrs)


ANY_SPEC = pl.BlockSpec(memory_space=pl.ANY)
SEM_SPEC = pl.BlockSpec(memory_space=pltpu.SEMAPHORE)
DATAFLOW = pltpu.SideEffectType.DATAFLOW_SIDE_EFFECTING
N_PEER = N_DEV - 1


def _direct_copies(src_refs, zone_refs, send_sems, recv_sems, per_dest):
    x, y, c, _ = _place()
    me = 4 * x + 2 * y + c
    cps = []
    for a, (src, zone) in enumerate(zip(src_refs, zone_refs)):
        for r in range(N_PEER):
            bits = r + 1
            px = 1 - x if bits & 4 else x
            py = 1 - y if bits & 2 else y
            pc = 1 - c if bits & 1 else c
            cps.append(pltpu.make_async_remote_copy(
                src_ref=src.at[4 * px + 2 * py + pc] if per_dest else src, dst_ref=zone.at[me],
                send_sem=send_sems.at[a * N_PEER + r], recv_sem=recv_sems.at[a * N_PEER + r],
                device_id=(px, py, pc), device_id_type=MESH))
    return cps


def _direct_start(srcs, per_dest, after, name):
    n = len(srcs)
    zones = [lax.empty((N_DEV,) + (s.shape[1:] if per_dest else s.shape), s.dtype) for s in srcs]

    def body(*refs):
        src_refs, zone_refs = refs[:n], refs[n:2 * n]
        send_sems, recv_sems = refs[2 * n + 1:2 * n + 3]
        for cp in _direct_copies(src_refs, zone_refs, send_sems, recv_sems, per_dest):
            cp.start()
        refs[-1][...] = jnp.zeros_like(refs[-1])

    sems = pltpu.SemaphoreType.DMA((n * N_PEER,))
    res = pl.pallas_call(
        body, name=name, in_specs=[HBM_SPEC] * (2 * n) + [ANY_SPEC],
        out_shape=[sems, sems] + [pltpu.HBM(s.shape, s.dtype) for s in srcs] + [pltpu.HBM(z.shape, z.dtype) for z in zones]
        + [jax.ShapeDtypeStruct((8, 128), F32)],
        out_specs=[SEM_SPEC, SEM_SPEC] + [HBM_SPEC] * (2 * n) + [pl.BlockSpec(memory_space=pltpu.VMEM)],
        input_output_aliases={i: 2 + i for i in range(2 * n)},
        compiler_params=pltpu.CompilerParams(has_side_effects=DATAFLOW),
    )(*[pltpu.with_memory_space_constraint(s, pltpu.HBM) for s in srcs],
      *[pltpu.with_memory_space_constraint(z, pltpu.HBM) for z in zones], after)
    return (res[0], res[1], list(res[2:2 + n]), list(res[2 + n:2 + 2 * n]), per_dest), res[-1][0:1, 0:1]


def _direct_wait(started, after, name):
    send_sems, recv_sems, srcs, zones, per_dest = started
    n = len(srcs)

    def body(*refs):
        src_refs, zone_refs = refs[:n], refs[n:2 * n]
        for cp in _direct_copies(src_refs, zone_refs, refs[2 * n], refs[2 * n + 1], per_dest):
            cp.wait_send()
            cp.wait_recv()

    res = pl.pallas_call(
        body, name=name, in_specs=[HBM_SPEC] * (2 * n) + [SEM_SPEC, SEM_SPEC, ANY_SPEC],
        out_shape=[pltpu.HBM(s.shape, s.dtype) for s in srcs] + [pltpu.HBM(z.shape, z.dtype) for z in zones],
        out_specs=[HBM_SPEC] * (2 * n), input_output_aliases={i: i for i in range(2 * n)},
        compiler_params=pltpu.CompilerParams(has_side_effects=DATAFLOW),
    )(*srcs, *zones, send_sems, recv_sems, after)
    return list(res[n:])


def _adamw(w, g, m, v):
    m = ADAM_B1 * m + (1.0 - ADAM_B1) * g
    v = ADAM_B2 * v + (1.0 - ADAM_B2) * jnp.square(g)
    m_hat = m / (1.0 - ADAM_B1 ** ADAM_STEP)
    v_hat = v / (1.0 - ADAM_B2 ** ADAM_STEP)
    delta = -ADAM_LR * (m_hat / (jnp.sqrt(v_hat) + ADAM_EPS) + ADAM_WD * w)
    return delta, m, v


def _adamw_layers(w, m, v, own0, ex0, own1, zone1, name):
    depth, rows, c = w.shape
    tm = _pick_tm(rows, c)
    nt = rows // tm
    first = lambda l, i: i * (1 - l) + (nt - 1) * l
    second = lambda l, i: i * l

    def body(w_ref, m_ref, v_ref, own0_ref, ex0_ref, own1_ref, zone1_ref, g_ref, d_ref, m2_ref, v2_ref):
        def finish(g):
            delta, m2, v2 = _adamw(w_ref[...], g, m_ref[...], v_ref[...])
            g_ref[...], d_ref[...], m2_ref[...], v2_ref[...] = g, delta, m2, v2

        @pl.when(pl.program_id(0) == 0)
        def _():
            g = own0_ref[...]
            for j in range(ex0_ref.shape[0]):
                g = g + ex0_ref[j].astype(F32)
            finish(g)

        @pl.when(pl.program_id(0) == 1)
        def _():
            g = own1_ref[...]
            for d in range(zone1_ref.shape[0]):
                g = g + zone1_ref[d].astype(F32)
            finish(g)

    wspec = pl.BlockSpec((None, tm, c), lambda l, i: (l, i, 0))
    return pl.pallas_call(
        body, grid=(depth, nt),
        in_specs=[wspec] * 3 + [pl.BlockSpec((tm, c), lambda l, i: (first(l, i), 0)),
                                pl.BlockSpec((ex0.shape[0], tm, c), lambda l, i: (0, first(l, i), 0)),
                                pl.BlockSpec((tm, c), lambda l, i: (second(l, i), 0)),
                                pl.BlockSpec((zone1.shape[0], tm, c), lambda l, i: (0, second(l, i), 0))],
        out_specs=[wspec] * 4, out_shape=[jax.ShapeDtypeStruct(w.shape, F32)] * 4, name=name,
    )(w, m, v, own0, ex0, own1, zone1)


def _adamw_call(w, m, v, gparts, name):
    def fn(wv, mv, vv, *gs):
        g = gs[0].astype(F32)
        for gp in gs[1:]:
            g = g + gp.astype(F32)
        delta, m2, v2 = _adamw(wv, g, mv, vv)
        return g, delta, m2, v2
    rows, c = w.shape
    return _ew(fn, [w, m, v] + list(gparts), [], [(c, F32)] * 4, tm=_pick_tm(rows, c), name=name)


def kernel(x, p, g_mix, w_in, b_fox_f, fox_q_gain, fox_k_gain, sc_conv_w, dn_conv_w, dn_a_log, dn_dt_bias,
           dn_norm_gain, w_branch, w_o, g_ffn, w_up, ffn_conv_w, w_down, g_ple, w_ple_gate, w_ple, loss_target,
           m_g_mix, m_w_in, m_b_fox_f, m_fox_q_gain, m_fox_k_gain, m_sc_conv_w, m_dn_conv_w, m_dn_a_log,
           m_dn_dt_bias, m_dn_norm_gain, m_w_branch, m_w_o, m_g_ffn, m_w_up, m_ffn_conv_w, m_w_down, m_g_ple,
           m_w_ple_gate, m_w_ple, v_g_mix, v_w_in, v_b_fox_f, v_fox_q_gain, v_fox_k_gain, v_sc_conv_w, v_dn_conv_w,
           v_dn_a_log, v_dn_dt_bias, v_dn_norm_gain, v_w_branch, v_w_o, v_g_ffn, v_w_up, v_ffn_conv_w, v_w_down,
           v_g_ple, v_w_ple_gate, v_w_ple):
    return _step(x, p, g_mix, w_in, b_fox_f, fox_q_gain, fox_k_gain, sc_conv_w, dn_conv_w, dn_a_log, dn_dt_bias,
                 dn_norm_gain, w_branch, w_o, g_ffn, w_up, ffn_conv_w, w_down, g_ple, w_ple_gate, w_ple, loss_target,
                 m_g_mix, m_w_in, m_b_fox_f, m_fox_q_gain, m_fox_k_gain, m_sc_conv_w, m_dn_conv_w, m_dn_a_log,
                 m_dn_dt_bias, m_dn_norm_gain, m_w_branch, m_w_o, m_g_ffn, m_w_up, m_ffn_conv_w, m_w_down, m_g_ple,
                 m_w_ple_gate, m_w_ple, v_g_mix, v_w_in, v_b_fox_f, v_fox_q_gain, v_fox_k_gain, v_sc_conv_w,
                 v_dn_conv_w, v_dn_a_log, v_dn_dt_bias, v_dn_norm_gain, v_w_branch, v_w_o, v_g_ffn, v_w_up,
                 v_ffn_conv_w, v_w_down, v_g_ple, v_w_ple_gate, v_w_ple)


def _step(*args):
    names = ['x', 'p'] + WEIGHTS + ['loss_target'] + ['m_' + n for n in WEIGHTS] + ['v_' + n for n in WEIGHTS]
    assert len(args) == len(names)
    a = dict(zip(names, args))
    x, target = a['x'][0], a['loss_target'][0]
    p = a['p'][:, 0]
    _, _, c, _ = _place()
    chip = 2 * lax.axis_index("x") + lax.axis_index("y")

    dev = 4 * lax.axis_index("x") + 2 * lax.axis_index("y") + c

    LATE = [n for n in BIG if n != 'w_in']
    sps = [{n: a[n][layer][None, :] for n in SMALL} for layer in range(DEPTH)]
    shards = [{n: a[n][layer].astype(BF16) for n in BIG} for layer in range(DEPTH)]

    def placed(zones, own_blocks):
        return [lax.dynamic_update_index_in_dim(z, o, dev, 0) for z, o in zip(zones, own_blocks)]

    def by_device(full, n):
        by_dest = _by_dest(full, n)
        return lax.dynamic_index_in_dim(by_dest, dev, axis=0, keepdims=False), by_dest.astype(BF16)

    w_in_all, conv_all = _all_gather([shards[0]['w_in'], _flat_pack([a[n] for n in CONVW])], 'gather_w_in_l0')
    conv_by_dev = [_flat_unpack(conv_all[d], CONV_SHARD_SHAPES) for d in range(N_DEV)]
    conv_full = {n: jnp.concatenate([conv_by_dev[d][i] for d in range(N_DEV)], axis=2) for i, n in enumerate(CONVW)}
    late0_started, zero = _direct_start([shards[0][n] for n in LATE], False, w_in_all, 'weights_rest_l0_start')
    w1_started, zero = _direct_start([shards[1][n] for n in BIG], False, zero, 'weights_l1_start')
    sps[0]['g_mix'] = sps[0]['g_mix'] + zero

    def late_weights0(after):
        zones = placed(_direct_wait(late0_started, after, 'weights_rest_l0_wait'), [shards[0][n] for n in LATE])
        return {n: _full_from_gathered(z, n) for n, z in zip(LATE, zones)}

    w0 = {n: conv_full[n][0] for n in CONVW}
    w0['w_in'] = _w_in_assemble(w_in_all, 'w_in_assemble_l0')
    h, sv0 = _layer_fwd(x, p[0], w0, sps[0], 'l0', late_weights0)
    zones = placed(_direct_wait(w1_started, h, 'weights_l1_wait'), [shards[1][n] for n in BIG])
    w1 = {n: _full_from_gathered(z, n) for n, z in zip(BIG, zones) if n != 'w_in'}
    w1.update({n: conv_full[n][1] for n in CONVW})
    w1['w_in'] = _w_in_assemble(zones[list(BIG).index('w_in')], 'w_in_assemble_l1')
    h, sv1 = _layer_fwd(h, p[1], w1, sps[1], 'l1')
    dh, loss_part = _loss_bwd(h, target, 'loss')
    grads = [None] * DEPTH
    dh, grads[1] = _layer_bwd(dh, sv1, sps[1], 'l1')
    owns1, sends1 = {}, {}
    for n in BIG:
        owns1[n], sends1[n] = _w_in_split_by_device(grads[1][n], 'w_in_grad_split_l1') if n == 'w_in' else by_device(grads[1][n], n)
    g1_started, zero = _direct_start([sends1[n] for n in BIG], True, dh, 'grads_l1_start')
    early = {}

    def early_grads0(g, sp):
        owns, sends = zip(*[by_device(g[n], n) for n in LATE])
        early['started'], zero = _direct_start(list(sends), True, g['w_branch'], 'grads_rest_l0_start')
        early['owns'] = dict(zip(LATE, owns))
        return {**sp, 'dn_norm_gain': sp['dn_norm_gain'] + zero}

    dh, grads[0] = _layer_bwd(dh, sv0, {**sps[0], 'g_ple': sps[0]['g_ple'] + zero}, 'l0', early_grads0)
    grad_x = dh[None]

    keep, give = _w_in_split_by_core(grads[0]['w_in'], 'w_in_grad_split_l0')
    (got,) = _sibling_swap([give], 'grad_sibling_swap')
    part, part_b = _ew(lambda k, r: (k + r.astype(F32),) * 2, [_rows2d(keep), _rows2d(got)], [],
                       [(W_IN_SHARD, F32), (W_IN_SHARD, BF16)], tm=_pick_tm(4 * D_MODEL, W_IN_SHARD), name='grad_chip_sum_w_in')
    (other,) = _chip_exchange([part_b.reshape(keep.shape)], 'grad_chip_exchange')
    zones1 = dict(zip(BIG, _direct_wait(g1_started, other, 'grads_l1_wait')))
    zones0 = dict(zip(LATE, _direct_wait(early['started'], other, 'grads_rest_l0_wait')))
    out = {}
    for n in BIG:
        rc = (-1, a[n].shape[-1])
        blank = lambda z: lax.dynamic_update_index_in_dim(z, jnp.zeros(z.shape[1:], z.dtype), dev, 0)
        if n == 'w_in':
            own0, ex0 = lax.dynamic_index_in_dim(part.reshape(keep.shape), chip, axis=0, keepdims=False), other
        else:
            own0, ex0 = early['owns'][n], blank(zones0[n])
        own0 = own0.reshape(rc)
        view = lambda t: t.reshape((DEPTH,) + own0.shape)
        res = _adamw_layers(view(a[n]), view(a['m_' + n]), view(a['v_' + n]), own0, ex0.reshape((-1,) + own0.shape),
                            owns1[n].reshape(rc), blank(zones1[n]).reshape((N_DEV,) + own0.shape), f'adamw_{n}')
        out[n] = [r.reshape(a[n].shape) for r in res]

    small_pack = _flat_pack([jnp.stack([grads[layer][n].reshape(-1) for layer in range(DEPTH)]) for n in SMALL] + [loss_part[0, 0]])
    conv_pack = _flat_pack([jnp.stack([grads[layer][n] for layer in range(DEPTH)]) for n in CONVW])
    small_all, conv_all = _all_gather([small_pack, conv_pack], 'gather_small_grads')
    res = _adamw_call(_flat_pack([a[n] for n in SMALL]), _flat_pack([a['m_' + n] for n in SMALL]),
                      _flat_pack([a['v_' + n] for n in SMALL]), [small_all[d] for d in range(N_DEV)], 'adamw_replicated')
    loss = res[0].reshape(-1)[SMALL_LOSS_AT]
    for k, r in enumerate(res):
        for n, val in zip(SMALL, _flat_unpack(r, SMALL_SHAPES)):
            out.setdefault(n, [None] * 4)[k] = val
    (conv_sum,) = _ew(lambda *gs: (functools.reduce(lambda s, t: s + t, gs),), [conv_all[d] for d in range(N_DEV)], [],
                      [(PACK_W, F32)], tm=conv_pack.shape[0], name='conv_grad_sum')
    conv_own = [lax.dynamic_slice_in_dim(g, dev * (g.shape[2] // N_DEV), g.shape[2] // N_DEV, axis=2)
                for g in _flat_unpack(conv_sum, CONV_FULL_SHAPES)]
    res = _adamw_call(_flat_pack([a[n] for n in CONVW]), _flat_pack([a['m_' + n] for n in CONVW]),
                      _flat_pack([a['v_' + n] for n in CONVW]), [_flat_pack(conv_own)], 'adamw_conv')
    for k, r in enumerate(res):
        for n, val in zip(CONVW, _flat_unpack(r, CONV_SHARD_SHAPES)):
            out.setdefault(n, [None] * 4)[k] = val

    outs = [loss, grad_x]
    for k in range(4):
        outs += [out[n][k] for n in WEIGHTS]
    return tuple(outs)
```

```python
import functools

import jax
import jax.numpy as jnp
from jax import lax
from jax.experimental import pallas as pl
from jax.experimental.pallas import tpu as pltpu

F32 = jnp.float32
BF16 = jnp.bfloat16
HI = lax.Precision.HIGHEST

D_MODEL = 1024
DEPTH = 2
N_DEV = 8
PLE_DIM = 256
BW = 512
FOX_HEADS, FOX_DH = 8, 64
DN_HEADS, DN_DH = 4, 128
DN_CHUNK = 64
D_FF = 2816
EPS = 1e-6
NEG = -1e30

ADAM_LR, ADAM_B1, ADAM_B2, ADAM_EPS, ADAM_WD, ADAM_STEP = 0.001, 0.9, 0.999, 1e-08, 0.01, 10

C_FQ, C_FK, C_FV = 0, 512, 1024
C_SB, C_SC, C_SV = 1536, 2048, 2560
C_DQ, C_DK, C_DV, C_DZ = 3072, 3584, 4096, 4608
C_GATE = 5120
C_SMALL = 8192
IN_P = 8320
IN_ORIG = 8208

WEIGHTS = ['g_mix', 'w_in', 'b_fox_f', 'fox_q_gain', 'fox_k_gain', 'sc_conv_w', 'dn_conv_w', 'dn_a_log',
           'dn_dt_bias', 'dn_norm_gain', 'w_branch', 'w_o', 'g_ffn', 'w_up', 'ffn_conv_w', 'w_down', 'g_ple',
           'w_ple_gate', 'w_ple']
BIG = {'w_in': 1, 'w_branch': 2, 'w_o': 0, 'w_up': 1, 'w_down': 0, 'w_ple_gate': 0, 'w_ple': 1}
CONVW = {'sc_conv_w': 1, 'dn_conv_w': 1, 'ffn_conv_w': 1}
SHARDED = {**BIG, **CONVW}
SMALL = [n for n in WEIGHTS if n not in SHARDED]
MESH = pl.DeviceIdType.MESH


def _sigmoid(x):
    return 0.5 * (jnp.tanh(0.5 * x) + 1.0)


def _silu(x):
    return x * _sigmoid(x)


def _log1pexp_negabs(z):
    return jnp.log(1.0 + jnp.exp(-jnp.abs(z)))


def _log_sigmoid(z):
    return jnp.minimum(z, 0.0) - _log1pexp_negabs(z)


def _softplus(z):
    return jnp.maximum(z, 0.0) + _log1pexp_negabs(z)


def _rms(x, g):
    return x * lax.rsqrt(jnp.mean(x * x, axis=-1, keepdims=True) + EPS) * g


def _l2(x):
    return x * lax.rsqrt(jnp.sum(x * x, axis=-1, keepdims=True) + EPS)


def _dot(a, b, dims, precision=None):
    return lax.dot_general(a, b, (dims, ((), ())), preferred_element_type=F32, precision=precision)


NN = ((1,), (0,))
NT = ((1,), (1,))
TN = ((0,), (0,))


def _shift_down(x, s):
    if s == 0:
        return x
    t = lax.broadcasted_iota(jnp.int32, x.shape, 0)
    return jnp.where(t >= s, pltpu.roll(x, s, 0), 0.0)


def _shift_up(x, s):
    if s == 0:
        return x
    n = x.shape[0]
    t = lax.broadcasted_iota(jnp.int32, x.shape, 0)
    return jnp.where(t < n - s, pltpu.roll(x, n - s, 0), 0.0)


def _conv(x, w):
    k = w.shape[0]
    y = w[k - 1:k] * x
    for j in range(k - 1):
        y = y + w[j:j + 1] * _shift_down(x, k - 1 - j)
    return y


def _conv_bwd(x, w, dy):
    k = w.shape[0]
    dx = w[k - 1:k] * dy
    dws = []
    for j in range(k - 1):
        dx = dx + w[j:j + 1] * _shift_up(dy, k - 1 - j)
        dws.append(jnp.sum(dy * _shift_down(x, k - 1 - j), axis=0, keepdims=True))
    dws.append(jnp.sum(dy * x, axis=0, keepdims=True))
    return dx, dws


MM_VMEM_BUDGET = 36 << 20
MM_STEP_BYTES = 1 << 20


def _mm_tiles(m, n, k, a_size, b_size, tile_size, cast_a):
    best = None
    for tm in [d for d in (2048, 1024, 512, 256, 128) if d <= m and m % d == 0] or [m]:
        for tn in [d for d in range(128, min(n, 2048) + 1, 128) if n % d == 0] or [n]:
            vmem = 2 * tm * k * a_size + (2 * tm * k if cast_a else 0) + 2 * k * tn * b_size + 2 * tm * tn * tile_size
            if vmem > MM_VMEM_BUDGET:
                continue
            steps = (m // tm) * (n // tn)
            cost = m * k * a_size + (m // tm) * k * n * b_size + m * n * tile_size + steps * MM_STEP_BYTES
            if best is None or cost < best[0]:
                best = (cost, tm, tn)
    assert best is not None, (m, n, k)
    return best[1], best[2]


def _mm(a, b, mode, outs, *, epi=None, epi_args=(), name):
    if mode == 'nn':
        (m, k), (k2, n) = a.shape, b.shape
    elif mode == 'nt':
        (m, k), (n, k2) = a.shape, b.shape
    else:
        (k, m), (k2, n) = a.shape, b.shape
    assert k == k2, (a.shape, b.shape, mode)
    tile_size = sum(jnp.dtype(dt).itemsize for dt in outs) + sum(e.dtype.itemsize for e in epi_args if e.shape[0] != 1)
    tm, tn = _mm_tiles(m, n, k, a.dtype.itemsize, b.dtype.itemsize, tile_size, a.dtype != BF16)
    dims = {'nn': NN, 'nt': NT, 'tn': TN}[mode]
    a_spec = pl.BlockSpec((k, tm), lambda i, j: (0, i)) if mode == 'tn' else pl.BlockSpec((tm, k), lambda i, j: (i, 0))
    b_spec = pl.BlockSpec((tn, k), lambda i, j: (j, 0)) if mode == 'nt' else pl.BlockSpec((k, tn), lambda i, j: (0, j))
    e_specs = [pl.BlockSpec((1, tn), lambda i, j: (0, j)) if e.shape[0] == 1 else pl.BlockSpec((tm, tn), lambda i, j: (i, j))
               for e in epi_args]
    ne, no = len(epi_args), len(outs)
    cast_a = a.dtype != BF16

    def body(a_ref, b_ref, *rest):
        if cast_a:
            a_sc = rest[-1]

            @pl.when(pl.program_id(1) == 0)
            def _():
                a_sc[...] = a_ref[...].astype(BF16)
            av = a_sc[...]
        else:
            av = a_ref[...]
        acc = _dot(av, b_ref[...].astype(BF16), dims)
        vals = epi(acc, *[e[...] for e in rest[:ne]]) if epi is not None else (acc,)
        for o_ref, v in zip(rest[ne:ne + no], vals):
            o_ref[...] = v.astype(o_ref.dtype)

    res = pl.pallas_call(
        body, grid=(m // tm, n // tn),
        in_specs=[a_spec, b_spec] + e_specs,
        out_specs=[pl.BlockSpec((tm, tn), lambda i, j: (i, j)) for _ in outs],
        out_shape=[jax.ShapeDtypeStruct((m, n), dt) for dt in outs],
        scratch_shapes=[pltpu.VMEM(a_spec.block_shape, BF16)] if cast_a else [],
        name=name,
    )(a, b, *epi_args)
    return res[0] if no == 1 else res


def _ew(fn, tiled, bcast, outs, reds=(), *, tm=256, name):
    secs = [(t, 0, t.shape[1]) if not isinstance(t, tuple) else t for t in tiled]
    m = secs[0][0].shape[0]
    tm = min(tm, m)
    assert m % tm == 0
    in_specs = []
    for arr, off, w in secs:
        assert off % w == 0
        in_specs.append(pl.BlockSpec((tm, w), functools.partial(lambda i, c: (i, c), c=off // w)))
    in_specs += [pl.BlockSpec(b.shape, lambda i: (0, 0)) for b in bcast]
    nin, no = len(in_specs), len(outs)

    def body(*refs):
        vals = fn(*[r[...] for r in refs[:nin]])
        for r, v in zip(refs[nin:nin + no], vals[:no]):
            r[...] = v.astype(r.dtype)
        i = pl.program_id(0)
        for r, v in zip(refs[nin + no:], vals[no:]):
            @pl.when(i == 0)
            def _():
                r[...] = v

            @pl.when(i > 0)
            def _():
                r[...] += v

    res = pl.pallas_call(
        body, grid=(m // tm,), in_specs=in_specs,
        out_specs=[pl.BlockSpec((tm, c), lambda i: (i, 0)) for c, _ in outs] + [pl.BlockSpec(s, lambda i: (0, 0)) for s in reds],
        out_shape=[jax.ShapeDtypeStruct((m, c), dt) for c, dt in outs] + [jax.ShapeDtypeStruct(s, F32) for s in reds],
        name=name,
    )(*[s[0] for s in secs], *bcast)
    return res


def _cb(fn, cols, params, outs, pouts, *, tc, nblk, name):
    t = cols[0][0].shape[0]
    in_specs = []
    for arr, off in cols:
        assert off % tc == 0
        in_specs.append(pl.BlockSpec((t, tc), functools.partial(lambda c, o: (0, o + c), o=off // tc)))
    for arr, off in params:
        in_specs.append(pl.BlockSpec((arr.shape[0], tc), functools.partial(lambda c, o: (0, o + c), o=off // tc)))
    nin, no = len(in_specs), len(outs)

    def body(*refs):
        vals = fn(*[r[...] for r in refs[:nin]])
        for r, v in zip(refs[nin:], vals):
            r[...] = v.astype(r.dtype)

    return pl.pallas_call(
        body, grid=(nblk,), in_specs=in_specs,
        out_specs=[pl.BlockSpec((t, tc), lambda c: (0, c)) for _ in outs] + [pl.BlockSpec((k, tc), lambda c: (0, c)) for k in pouts],
        out_shape=[jax.ShapeDtypeStruct((t, nblk * tc), dt) for dt in outs] + [jax.ShapeDtypeStruct((k, nblk * tc), F32) for k in pouts],
        name=name,
    )(*[c[0] for c in cols], *[p[0] for p in params])


CUM_BLK = 256


def _prep_point(s, bias, alog):
    lane = lax.broadcasted_iota(jnp.int32, s.shape, 1)
    z = s + bias
    return jnp.where(lane < 8, _log_sigmoid(z),
                     jnp.where(lane < 12, _sigmoid(s),
                               jnp.where(lane < 16, -jnp.exp(alog) * _softplus(z), 0.0)))


def _tri(n, upper):
    r = lax.broadcasted_iota(jnp.int32, (n, n), 0)
    c = lax.broadcasted_iota(jnp.int32, (n, n), 1)
    return (r <= c if upper else r >= c).astype(F32)


def _prep_fwd(proj, bias_row, alog_row, name):
    t = proj.shape[0]
    nb = t // CUM_BLK

    def body(s_ref, b_ref, a_ref, o_ref):
        pre = _prep_point(s_ref[...], b_ref[...], a_ref[...])
        lane = lax.broadcasted_iota(jnp.int32, (CUM_BLK, 128), 1)
        tri = _tri(CUM_BLK, False)
        carry = jnp.zeros((1, 128), F32)
        for blk in range(nb):
            xb = pre[blk * CUM_BLK:(blk + 1) * CUM_BLK]
            cb = _dot(tri, xb, NN, HI) + carry
            carry = cb[CUM_BLK - 1:CUM_BLK]
            o_ref[blk * CUM_BLK:(blk + 1) * CUM_BLK, :] = jnp.where(lane < 8, cb, xb)

    return pl.pallas_call(
        body, grid=(1,),
        in_specs=[pl.BlockSpec((t, 128), lambda i: (0, C_SMALL // 128)), pl.BlockSpec((1, 128), lambda i: (0, 0)),
                  pl.BlockSpec((1, 128), lambda i: (0, 0))],
        out_specs=pl.BlockSpec((t, 128), lambda i: (0, 0)),
        out_shape=jax.ShapeDtypeStruct((t, 128), F32), name=name,
    )(proj, bias_row, alog_row)


def _prep_bwd(proj, bias_row, alog_row, daux, name):
    t = proj.shape[0]
    nb = t // CUM_BLK

    def body(s_ref, b_ref, a_ref, d_ref, ds_ref, db_ref, da_ref, dpre_sc):
        lane = lax.broadcasted_iota(jnp.int32, (CUM_BLK, 128), 1)
        tri = _tri(CUM_BLK, True)
        carry = jnp.zeros((1, 128), F32)
        for blk in reversed(range(nb)):
            db = d_ref[blk * CUM_BLK:(blk + 1) * CUM_BLK, :]
            cb = _dot(tri, db, NN, HI) + carry
            carry = cb[0:1]
            dpre_sc[blk * CUM_BLK:(blk + 1) * CUM_BLK, :] = jnp.where(lane < 8, cb, db)
        _, vjp = jax.vjp(_prep_point, s_ref[...], b_ref[...], a_ref[...])
        ds, dbias, dalog = vjp(dpre_sc[...])
        ds_ref[...] = ds.astype(ds_ref.dtype)
        db_ref[...] = dbias
        da_ref[...] = dalog

    return pl.pallas_call(
        body, grid=(1,),
        in_specs=[pl.BlockSpec((t, 128), lambda i: (0, C_SMALL // 128)), pl.BlockSpec((1, 128), lambda i: (0, 0)),
                  pl.BlockSpec((1, 128), lambda i: (0, 0)), pl.BlockSpec((t, 128), lambda i: (0, 0))],
        out_specs=[pl.BlockSpec((t, 128), lambda i: (0, 0)), pl.BlockSpec((1, 128), lambda i: (0, 0)),
                   pl.BlockSpec((1, 128), lambda i: (0, 0))],
        out_shape=[jax.ShapeDtypeStruct((t, 128), BF16), jax.ShapeDtypeStruct((1, 128), F32), jax.ShapeDtypeStruct((1, 128), F32)],
        scratch_shapes=[pltpu.VMEM((t, 128), F32)], name=name,
    )(proj, bias_row, alog_row, daux)


ATT_TQ = 256
FOX_SCALE = FOX_DH ** -0.5


def _qnorm(q, g):
    return _rms(q, g) * FOX_SCALE


def _att_scores(qn_blk, kn, cfc_blk, cfr, qi, tq, kend):
    s = _dot(qn_blk.astype(BF16), kn[:kend].astype(BF16), NT) + cfc_blk - cfr[:, :kend]
    row = lax.broadcasted_iota(jnp.int32, (tq, kend), 0) + qi * tq
    col = lax.broadcasted_iota(jnp.int32, (tq, kend), 1)
    return s, row >= col


ATT_PAIR = 128 // FOX_DH


def _att_specs(t):
    pair = lambda off: pl.BlockSpec((t, 128), functools.partial(lambda i, o: (0, o + i), o=off // 128))
    gain = pl.BlockSpec((1, FOX_DH), lambda i: (0, 0))
    hd = lambda i: (i, 0, 0)
    col, row = pl.BlockSpec((ATT_PAIR, t, 1), hd), pl.BlockSpec((ATT_PAIR, 1, t), hd)
    return [pair(C_FQ), pair(C_FK), pair(C_FV), gain, gain, col, row], pl.BlockSpec((t, 128), lambda i: (0, i)), col, row


def _att_fwd(proj, qg, kg, cfc, cfr, name):
    t = proj.shape[0]
    tq = min(ATT_TQ, t)
    in_specs, pair_out, col, _ = _att_specs(t)

    def body(q_ref, k_ref, v_ref, qg_ref, kg_ref, cfc_ref, cfr_ref, o_ref, lse_ref):
        for e in range(ATT_PAIR):
            lanes = slice(e * FOX_DH, (e + 1) * FOX_DH)
            qn = _qnorm(q_ref[:, lanes], qg_ref[...])
            kn = _rms(k_ref[:, lanes], kg_ref[...])
            v = v_ref[:, lanes].astype(BF16)
            cfr = cfr_ref[e]
            for qi in range(t // tq):
                kend = (qi + 1) * tq
                rows = slice(qi * tq, kend)
                s, mask = _att_scores(qn[rows], kn, cfc_ref[e, rows, :], cfr, qi, tq, kend)
                s = jnp.where(mask, s, NEG)
                m = jnp.max(s, axis=1, keepdims=True)
                p = jnp.exp(s - m)
                l = jnp.sum(p, axis=1, keepdims=True)
                o_ref[rows, lanes] = _dot(p.astype(BF16), v[:kend], NN) / l
                lse_ref[e, rows, :] = m + jnp.log(l)

    return pl.pallas_call(
        body, grid=(FOX_HEADS // ATT_PAIR,), in_specs=in_specs, out_specs=[pair_out, col],
        out_shape=[jax.ShapeDtypeStruct((t, BW), F32), jax.ShapeDtypeStruct((FOX_HEADS, t, 1), F32)], name=name,
    )(proj, proj, proj, qg, kg, cfc, cfr)


def _att_bwd(proj, qg, kg, cfc, cfr, lse, o, do, name):
    t = proj.shape[0]
    tq = min(ATT_TQ, t)
    in_specs, pair_out, col, row = _att_specs(t)

    def body(q_ref, k_ref, v_ref, qg_ref, kg_ref, cfc_ref, cfr_ref, lse_ref, o_ref, do_ref,
             dq_ref, dk_ref, dv_ref, dcfc_ref, dcfr_ref, dqg_ref, dkg_ref, dqn_sc, dkn_sc, dv_sc, dcfr_sc):
        for e in range(ATT_PAIR):
            lanes = slice(e * FOX_DH, (e + 1) * FOX_DH)
            qn, vjp_q = jax.vjp(_qnorm, q_ref[:, lanes], qg_ref[...])
            kn, vjp_k = jax.vjp(_rms, k_ref[:, lanes], kg_ref[...])
            v = v_ref[:, lanes].astype(BF16)
            cfr = cfr_ref[e]
            do_e = do_ref[:, lanes]
            delta = jnp.sum(do_e * o_ref[:, lanes], axis=1, keepdims=True)
            dkn_sc[...] = jnp.zeros_like(dkn_sc)
            dv_sc[...] = jnp.zeros_like(dv_sc)
            dcfr_sc[...] = jnp.zeros_like(dcfr_sc)
            for qi in range(t // tq):
                kend = (qi + 1) * tq
                rows = slice(qi * tq, kend)
                s, mask = _att_scores(qn[rows], kn, cfc_ref[e, rows, :], cfr, qi, tq, kend)
                p = jnp.where(mask, jnp.exp(jnp.where(mask, s, NEG) - lse_ref[e, rows, :]), 0.0)
                do_b = do_e[rows].astype(BF16)
                dv_sc[0:kend, :] += _dot(p.astype(BF16), do_b, TN)
                dp = _dot(do_b, v[:kend], NT)
                ds = p * (dp - delta[rows])
                ds_b = ds.astype(BF16)
                dqn_sc[rows, :] = _dot(ds_b, kn[:kend].astype(BF16), NN)
                dkn_sc[0:kend, :] += _dot(ds_b, qn[rows].astype(BF16), TN)
                dcfc_ref[e, rows, :] = jnp.sum(ds, axis=1, keepdims=True)
                dcfr_sc[:, 0:kend] -= jnp.sum(ds, axis=0, keepdims=True)
            dq, dqg = vjp_q(dqn_sc[...])
            dk, dkg = vjp_k(dkn_sc[...])
            dq_ref[:, lanes] = dq.astype(dq_ref.dtype)
            dk_ref[:, lanes] = dk.astype(dk_ref.dtype)
            dv_ref[:, lanes] = dv_sc[...].astype(dv_ref.dtype)
            dcfr_ref[e] = dcfr_sc[...]
            dqg_ref[e] = dqg
            dkg_ref[e] = dkg

    gsp = pl.BlockSpec((ATT_PAIR, 1, FOX_DH), lambda i: (i, 0, 0))
    return pl.pallas_call(
        body, grid=(FOX_HEADS // ATT_PAIR,),
        in_specs=in_specs + [col, pair_out, pair_out],
        out_specs=[pair_out] * 3 + [col, row, gsp, gsp],
        out_shape=[jax.ShapeDtypeStruct((t, BW), BF16)] * 3
        + [jax.ShapeDtypeStruct((FOX_HEADS, t, 1), F32), jax.ShapeDtypeStruct((FOX_HEADS, 1, t), F32)]
        + [jax.ShapeDtypeStruct((FOX_HEADS, 1, FOX_DH), F32)] * 2,
        scratch_shapes=[pltpu.VMEM((t, FOX_DH), F32)] * 3 + [pltpu.VMEM((1, t), F32)], name=name,
    )(proj, proj, proj, qg, kg, cfc, cfr, lse, o, do)


DN_SCALE = DN_DH ** -0.5


DN_BATCH = 8


@functools.partial(jax.custom_vjp, nondiff_argnums=(2, 3))
def _mm3(a, b, dims, batch=False):
    return _mm3_passes(a, b, dims, batch)


def _mm3_fwd(a, b, dims, batch):
    return _mm3_passes(a, b, dims, batch), (a, b)


def _mm3_bwd(dims, batch, res, dc):
    a, b = res
    if dims == NN:
        return _mm3_passes(dc, b, NT, batch), _mm3_passes(a, dc, TN, batch)
    if dims == NT:
        return _mm3_passes(dc, b, NN, batch), _mm3_passes(dc, a, TN, batch)
    return _mm3_passes(b, dc, NT, batch), _mm3_passes(a, dc, NN, batch)


_mm3.defvjp(_mm3_fwd, _mm3_bwd)


def _mm3_passes(a, b, dims, batch):
    if batch:
        dn = (((dims[0][0] + 1,), (dims[1][0] + 1,)), ((0,), (0,)))
        dot = lambda p, q: lax.dot_general(p, q, dn, preferred_element_type=F32)
    else:
        dot = lambda p, q: _dot(p, q, dims)
    ah, bh = a.astype(BF16), b.astype(BF16)
    al, bl = (a - ah.astype(F32)).astype(BF16), (b - bh.astype(F32)).astype(BF16)
    return dot(ah, bh) + (dot(ah, bl) + dot(al, bh))


def _dn_local(qc, kc, vc, g, beta):
    nb, c, _ = qc.shape
    ii = lax.broadcasted_iota(jnp.int32, (c, c), 0)
    jj = lax.broadcasted_iota(jnp.int32, (c, c), 1)
    incl, strict = ii >= jj, ii > jj
    lower = jnp.broadcast_to(incl.astype(F32), (nb, c, c))
    eye = (ii == jj).astype(F32)
    mm = functools.partial(_mm3, batch=True)
    dm = mm(lower, jnp.where(strict, g, 0.0), NN)
    decay = jnp.where(incl, jnp.exp(jnp.where(incl, dm, 0.0)), 0.0)
    gcum = mm(lower, g * jnp.ones((1, 1, DN_DH), F32), NN)
    eg = jnp.exp(gcum)
    glast = gcum[:, c - 1:c]
    kb = kc * beta
    n1 = jnp.where(strict, mm(kb, kc, NT) * decay, 0.0)
    inv = eye - n1
    pw = n1
    for _ in range(5):
        pw = mm(pw, pw, NN)
        inv = inv + mm(pw, inv, NN)
    sol = mm(inv, jnp.concatenate([vc * beta, kb * eg], axis=2), NN)
    qk = jnp.where(incl, mm(qc, kc, NT) * decay, 0.0)
    return sol[:, :, :DN_DH], sol[:, :, DN_DH:], qk, qc * eg, kc * jnp.exp(glast - gcum), jnp.exp(glast)


def _dn_state(u, kcum, qk, qdec, kdec, egl, state):
    v_new = u - _mm3(kcum, state, NN)
    out = _mm3(qdec, state, NN) + _mm3(qk, v_new, NN)
    return out, state * egl + _mm3(kdec, v_new, TN)


def _dn_pre_q(c):
    return _l2(_silu(c)) * DN_SCALE


def _dn_pre_k(c):
    return _l2(_silu(c))


def _dn_post(o, z, ng):
    return _rms(o, ng) * _silu(z)


def _dn_specs(t):
    cblk = lambda o: pl.BlockSpec((t, DN_DH), functools.partial(lambda h, o: (0, o + h), o=o // DN_DH))
    wblk = lambda o: pl.BlockSpec((4, DN_DH), functools.partial(lambda h, o: (0, o + h), o=o // DN_DH))
    proj_specs = [cblk(C_DQ), cblk(C_DK), cblk(C_DV), cblk(C_DZ)]
    w_specs = [wblk(0), wblk(BW), wblk(2 * BW)]
    gb_spec = pl.BlockSpec((1, t, 2), lambda h: (h, 0, 0))
    return proj_specs, w_specs, gb_spec


def _chunk_rows(n, count=1):
    return pl.ds(pl.multiple_of(n * DN_CHUNK, DN_CHUNK), count * DN_CHUNK)


def _egl_rows(n, count=1):
    return pl.ds(pl.multiple_of(n * 8, 8), count * 8)


def _dn_local_inputs(n, qn_sc, kn_sc, vv_sc, gb_ref):
    r = _chunk_rows(n, DN_BATCH)
    split = lambda v: v.reshape(DN_BATCH, DN_CHUNK, v.shape[-1])
    gbv = split(gb_ref[0, r, :])
    return split(qn_sc[r, :]), split(kn_sc[r, :]), split(vv_sc[r, :]), gbv[:, :, 0:1], gbv[:, :, 1:2]


def _dn_local_phase(nc, qn_sc, kn_sc, vv_sc, gb_ref, loc):
    def step(i, carry):
        n = i * DN_BATCH
        vals = _dn_local(*_dn_local_inputs(n, qn_sc, kn_sc, vv_sc, gb_ref))
        for sc, val in zip(loc[:5], vals[:5]):
            sc[_chunk_rows(n, DN_BATCH), :] = val.reshape(DN_BATCH * DN_CHUNK, val.shape[-1])
        loc[5][_egl_rows(n, DN_BATCH), :] = jnp.broadcast_to(vals[5], (DN_BATCH, 8, DN_DH)).reshape(DN_BATCH * 8, DN_DH)
        return carry

    lax.fori_loop(0, nc // DN_BATCH, step, 0)


def _dn_loc_scratch(t, nc):
    big = pltpu.VMEM((t, DN_DH), F32)
    return [big, big, pltpu.VMEM((t, DN_CHUNK), F32), big, big, pltpu.VMEM((nc * 8, DN_DH), F32)]


def _dn_fwd(proj, conv_w, gb, ng, name):
    t = proj.shape[0]
    nc = t // DN_CHUNK
    assert nc % DN_BATCH == 0
    proj_specs, w_specs, gb_spec = _dn_specs(t)

    def body(q_ref, k_ref, v_ref, z_ref, wq_ref, wk_ref, wv_ref, gb_ref, ng_ref, y_ref, o_ref, st_ref, qn_sc, kn_sc, vv_sc, *loc):
        qn_sc[...] = _dn_pre_q(_conv(q_ref[...], wq_ref[...]))
        kn_sc[...] = _dn_pre_k(_conv(k_ref[...], wk_ref[...]))
        vv_sc[...] = _silu(_conv(v_ref[...], wv_ref[...]))
        _dn_local_phase(nc, qn_sc, kn_sc, vv_sc, gb_ref, loc)
        u_sc, kcum_sc, qk_sc, qdec_sc, kdec_sc, egl_sc = loc

        def chunk(n, state):
            r = _chunk_rows(n)
            egl = egl_sc[_egl_rows(n), :][0:1]
            out, new_state = _dn_state(u_sc[r, :], kcum_sc[r, :], qk_sc[r, :], qdec_sc[r, :], kdec_sc[r, :], egl, state)
            st_ref[0, n] = state
            o_ref[r, :] = out
            return new_state

        lax.fori_loop(0, nc, chunk, jnp.zeros((DN_DH, DN_DH), F32))
        y_ref[...] = _dn_post(o_ref[...], z_ref[...], ng_ref[...])

    hblk = pl.BlockSpec((t, DN_DH), lambda h: (0, h))
    return pl.pallas_call(
        body, grid=(DN_HEADS,),
        in_specs=proj_specs + w_specs + [gb_spec, pl.BlockSpec((1, DN_DH), lambda h: (0, 0))],
        out_specs=[hblk, hblk, pl.BlockSpec((1, nc, DN_DH, DN_DH), lambda h: (h, 0, 0, 0))],
        out_shape=[jax.ShapeDtypeStruct((t, BW), F32), jax.ShapeDtypeStruct((t, BW), F32),
                   jax.ShapeDtypeStruct((DN_HEADS, nc, DN_DH, DN_DH), F32)],
        scratch_shapes=[pltpu.VMEM((t, DN_DH), F32)] * 3 + _dn_loc_scratch(t, nc), name=name,
    )(proj, proj, proj, proj, conv_w, conv_w, conv_w, gb, ng)


def _dn_bwd(proj, conv_w, gb, ng, o, states, dy, name):
    t = proj.shape[0]
    nc = t // DN_CHUNK
    proj_specs, w_specs, gb_spec = _dn_specs(t)
    nloc = 6

    def body(q_ref, k_ref, v_ref, z_ref, wq_ref, wk_ref, wv_ref, gb_ref, ng_ref, o_ref, st_ref, dy_ref,
             dq_ref, dk_ref, dv_ref, dz_ref, dwq_ref, dwk_ref, dwv_ref, dgb_ref, dng_ref,
             qn_sc, kn_sc, vv_sc, do_sc, *rest):
        loc, dloc = rest[:nloc], rest[nloc:]
        qn_sc[...] = _dn_pre_q(_conv(q_ref[...], wq_ref[...]))
        kn_sc[...] = _dn_pre_k(_conv(k_ref[...], wk_ref[...]))
        vv_sc[...] = _silu(_conv(v_ref[...], wv_ref[...]))
        _, vjp_y = jax.vjp(_dn_post, o_ref[...], z_ref[...], ng_ref[...])
        do, dz, dng = vjp_y(dy_ref[...])
        do_sc[...] = do
        dz_ref[...] = dz.astype(dz_ref.dtype)
        dng_ref[0] = dng
        _dn_local_phase(nc, qn_sc, kn_sc, vv_sc, gb_ref, loc)
        u_sc, kcum_sc, qk_sc, qdec_sc, kdec_sc, egl_sc = loc

        def state_bwd(i, dstate):
            n = nc - 1 - i
            r = _chunk_rows(n)
            r8 = _egl_rows(n)
            _, vjp = jax.vjp(_dn_state, u_sc[r, :], kcum_sc[r, :], qk_sc[r, :], qdec_sc[r, :], kdec_sc[r, :],
                             egl_sc[r8, :][0:1], st_ref[0, n])
            du, dkcum, dqk, dqdec, dkdec, degl, dprev = vjp((do_sc[r, :], dstate))
            for d_sc, val in zip(dloc[:5], (du, dkcum, dqk, dqdec, dkdec)):
                d_sc[r, :] = val
            dloc[5][r8, :] = jnp.broadcast_to(degl, (8, DN_DH))
            return dprev

        lax.fori_loop(0, nc, state_bwd, jnp.zeros((DN_DH, DN_DH), F32))

        def local_bwd(i, carry):
            n = i * DN_BATCH
            r = _chunk_rows(n, DN_BATCH)
            _, vjp = jax.vjp(_dn_local, *_dn_local_inputs(n, qn_sc, kn_sc, vv_sc, gb_ref))
            cts = tuple(d_sc[r, :].reshape(DN_BATCH, DN_CHUNK, d_sc.shape[-1]) for d_sc in dloc[:5])
            cts += (dloc[5][_egl_rows(n, DN_BATCH), :].reshape(DN_BATCH, 8, DN_DH)[:, 0:1],)
            dqc, dkc, dvc, dg, dbeta = vjp(cts)
            for d_sc, val in zip((dloc[0], dloc[1], dloc[3]), (dqc, dkc, dvc)):
                d_sc[r, :] = val.reshape(DN_BATCH * DN_CHUNK, DN_DH)
            dgb_ref[0, r, :] = jnp.concatenate([dg, dbeta], axis=2).reshape(DN_BATCH * DN_CHUNK, 2)
            return carry

        lax.fori_loop(0, nc // DN_BATCH, local_bwd, 0)
        for x_ref, w_ref, pre, d_sc, dx_ref, dw_ref in ((q_ref, wq_ref, _dn_pre_q, dloc[0], dq_ref, dwq_ref),
                                                       (k_ref, wk_ref, _dn_pre_k, dloc[1], dk_ref, dwk_ref),
                                                       (v_ref, wv_ref, _silu, dloc[3], dv_ref, dwv_ref)):
            _, vjp = jax.vjp(pre, _conv(x_ref[...], w_ref[...]))
            (dc,) = vjp(d_sc[...])
            dx, dws = _conv_bwd(x_ref[...], w_ref[...], dc)
            dx_ref[...] = dx.astype(dx_ref.dtype)
            for j, dw in enumerate(dws):
                dw_ref[j:j + 1, :] = dw

    hblk = pl.BlockSpec((t, DN_DH), lambda h: (0, h))
    wout = pl.BlockSpec((4, DN_DH), lambda h: (0, h))
    return pl.pallas_call(
        body, grid=(DN_HEADS,),
        in_specs=proj_specs + w_specs + [gb_spec, pl.BlockSpec((1, DN_DH), lambda h: (0, 0)), hblk,
                                         pl.BlockSpec((1, nc, DN_DH, DN_DH), lambda h: (h, 0, 0, 0)), hblk],
        out_specs=[hblk] * 4 + [wout] * 3 + [gb_spec, pl.BlockSpec((1, 1, DN_DH), lambda h: (h, 0, 0))],
        out_shape=[jax.ShapeDtypeStruct((t, BW), BF16)] * 4 + [jax.ShapeDtypeStruct((4, BW), F32)] * 3
        + [jax.ShapeDtypeStruct((DN_HEADS, t, 2), F32), jax.ShapeDtypeStruct((DN_HEADS, 1, DN_DH), F32)],
        scratch_shapes=[pltpu.VMEM((t, DN_DH), F32)] * 4 + _dn_loc_scratch(t, nc) * 2, name=name,
    )(proj, proj, proj, proj, conv_w, conv_w, conv_w, gb, ng, o, states, dy)


MERGE_TM, MERGE_TN = 512, 512


def _merge_specs(t):
    tm, tn = min(MERGE_TM, t), MERGE_TN
    y_spec = pl.BlockSpec((tm, BW), lambda i, j: (i, 0))
    w_spec = pl.BlockSpec((3, BW, tn), lambda i, j: (0, 0, j))
    gate_specs = [pl.BlockSpec((tm, tn), functools.partial(lambda i, j, o: (i, o + j), o=(C_GATE + n * D_MODEL) // tn))
                  for n in range(3)]
    return tm, tn, [y_spec] * 3 + [w_spec] + gate_specs


def _merge_fwd(ys, wb, proj, name):
    t = proj.shape[0]
    tm, tn, in_specs = _merge_specs(t)

    def body(y0, y1, y2, w_ref, g0, g1, g2, o_ref):
        acc = jnp.zeros((tm, tn), F32)
        for n, (y, g) in enumerate(((y0, g0), (y1, g1), (y2, g2))):
            acc = acc + _dot(y[...].astype(BF16), w_ref[n], NN) * _sigmoid(g[...])
        o_ref[...] = acc.astype(o_ref.dtype)

    return pl.pallas_call(
        body, grid=(t // tm, D_MODEL // tn), in_specs=in_specs,
        out_specs=pl.BlockSpec((tm, tn), lambda i, j: (i, j)),
        out_shape=jax.ShapeDtypeStruct((t, D_MODEL), BF16), name=name,
    )(*ys, wb, proj, proj, proj)


def _merge_bwd(ys, wb, proj, dmerged, name):
    t = proj.shape[0]
    tm, tn, in_specs = _merge_specs(t)

    def body(y0, y1, y2, w_ref, g0, g1, g2, dm_ref, dg_ref, dt_ref):
        dm = dm_ref[...]
        for n, (y, g) in enumerate(((y0, g0), (y1, g1), (y2, g2))):
            tn_ = _dot(y[...].astype(BF16), w_ref[n], NN)
            sg = _sigmoid(g[...])
            dg_ref[n] = (dm * tn_ * sg * (1.0 - sg)).astype(dg_ref.dtype)
            dt_ref[n] = (dm * sg).astype(dt_ref.dtype)

    o3 = pl.BlockSpec((3, tm, tn), lambda i, j: (0, i, j))
    return pl.pallas_call(
        body, grid=(t // tm, D_MODEL // tn), in_specs=in_specs + [pl.BlockSpec((tm, tn), lambda i, j: (i, j))],
        out_specs=[o3, o3], out_shape=[jax.ShapeDtypeStruct((3, t, D_MODEL), BF16)] * 2, name=name,
    )(*ys, wb, proj, proj, proj, dmerged)


def _prep_rows(sp):
    z4, z112 = jnp.zeros((1, 4), F32), jnp.zeros((1, 112), F32)
    bias_row = jnp.concatenate([sp['b_fox_f'], z4, sp['dn_dt_bias'], z112], axis=1)
    alog_row = jnp.concatenate([jnp.zeros((1, 12), F32), sp['dn_a_log'], z112], axis=1)
    return bias_row, alog_row


def _sc_fwd_tile(sb, sc, sv, w):
    return (sb * _conv(sc * sv, w),)


def _ffn_act(ug, uv):
    return _silu(ug) * uv


W_IN_COLS = [(C_FQ, 0, 1536), (C_SB, 1544, 1536), (C_DQ, 3080, 1536), (C_DZ, 4624, 512), (C_GATE, 5136, 3072),
             (C_SMALL, 1536, 8), (C_SMALL + 8, 4616, 8)]
W_IN_SHARD = IN_ORIG // N_DEV


def _w_in_segments():
    out = []
    for d, o, w in W_IN_COLS:
        end = o + w
        while o < end:
            k = o // W_IN_SHARD
            n = min(end, (k + 1) * W_IN_SHARD) - o
            out.append((d, k, o - k * W_IN_SHARD, n))
            o, d = o + n, d + n
    return out


def _w_in_assemble(g, name):
    tm = 256

    def body(g_ref, o_ref):
        for d, k, s, n in _w_in_segments():
            o_ref[:, d:d + n] = g_ref[k, :, s:s + n]
        o_ref[:, IN_ORIG:IN_P] = jnp.zeros((tm, IN_P - IN_ORIG), o_ref.dtype)

    return pl.pallas_call(
        body, grid=(D_MODEL // tm,),
        in_specs=[pl.BlockSpec((N_DEV, tm, W_IN_SHARD), lambda i: (0, i, 0))],
        out_specs=pl.BlockSpec((tm, IN_P), lambda i: (i, 0)),
        out_shape=jax.ShapeDtypeStruct((D_MODEL, IN_P), g.dtype), name=name,
    )(g)


W_IN_SPLIT_TM = 128


def _w_in_split_by_device(g, name):
    tm = W_IN_SPLIT_TM

    def body(g_ref, own_ref, send_ref):
        me = 4 * lax.axis_index("x") + 2 * lax.axis_index("y") + lax.axis_index("c")
        for d, k, s, n in _w_in_segments():
            val = g_ref[:, d:d + n]
            send_ref[k, :, s:s + n] = val.astype(send_ref.dtype)

            @pl.when(me == k)
            def _():
                own_ref[:, s:s + n] = val

    return pl.pallas_call(
        body, grid=(D_MODEL // tm,), in_specs=[pl.BlockSpec((tm, IN_P), lambda i: (i, 0))],
        out_specs=[pl.BlockSpec((tm, W_IN_SHARD), lambda i: (i, 0)), pl.BlockSpec((N_DEV, tm, W_IN_SHARD), lambda i: (0, i, 0))],
        out_shape=[jax.ShapeDtypeStruct((D_MODEL, W_IN_SHARD), F32), jax.ShapeDtypeStruct((N_DEV, D_MODEL, W_IN_SHARD), BF16)],
        name=name,
    )(g)


def _layer_fwd(x, p_i, w, sp, tag, late_weights=None):
    t = x.shape[0]
    sv = {'x': x}
    (hn,) = _ew(lambda a, g: (_rms(a, g),), [x], [sp['g_mix']], [(D_MODEL, BF16)], name=f'rms_mix_{tag}')
    proj = _mm(hn, w['w_in'], 'nn', [F32], name=f'in_proj_{tag}')
    bias_row, alog_row = _prep_rows(sp)
    aux = _prep_fwd(proj, bias_row, alog_row, f'prep_{tag}')
    cf = aux[:, :FOX_HEADS].T
    cfc, cfr = cf[:, :, None], cf[:, None, :]
    y_fox, lse = _att_fwd(proj, sp['fox_q_gain'], sp['fox_k_gain'], cfc, cfr, f'fox_fwd_{tag}')
    (y_sc,) = _cb(_sc_fwd_tile, [(proj, C_SB), (proj, C_SC), (proj, C_SV)], [(w['sc_conv_w'], 0)], [F32], [],
                  tc=256, nblk=2, name=f'sc_fwd_{tag}')
    gb = jnp.stack([aux[:, 12:16].T, aux[:, 8:12].T], axis=-1)
    y_dn, o_dn, states = _dn_fwd(proj, w['dn_conv_w'], gb, sp['dn_norm_gain'], f'dn_fwd_{tag}')
    ys = (y_fox, y_sc, y_dn)
    if late_weights is not None:
        w = {**w, **late_weights(y_dn)}
    merged = _merge_fwd(ys, w['w_branch'], proj, f'merge_fwd_{tag}')
    x1 = _mm(merged, w['w_o'], 'nn', [F32], epi=lambda acc, r: (acc + r,), epi_args=(x,), name=f'o_proj_{tag}')
    (hf,) = _ew(lambda a, g: (_rms(a, g),), [x1], [sp['g_ffn']], [(D_MODEL, BF16)], name=f'rms_ffn_{tag}')
    up = _mm(hf, w['w_up'], 'nn', [F32], name=f'up_proj_{tag}')
    (act,) = _cb(lambda ug, uv, wg, wv: (_ffn_act(_conv(ug, wg), _conv(uv, wv)),), [(up, 0), (up, D_FF)],
                 [(w['ffn_conv_w'], 0), (w['ffn_conv_w'], D_FF)], [BF16], [], tc=256, nblk=D_FF // 256, name=f'ffn_act_{tag}')
    x2 = _mm(act, w['w_down'], 'nn', [F32], epi=lambda acc, r: (acc + r,), epi_args=(x1,), name=f'down_proj_{tag}')
    (hp,) = _ew(lambda a, g: (_rms(a, g),), [x2], [sp['g_ple']], [(D_MODEL, BF16)], name=f'rms_ple_{tag}')
    gp = _mm(hp, w['w_ple_gate'], 'nn', [F32], name=f'ple_gate_{tag}')
    x3 = _mm(p_i, w['w_ple'], 'nn', [F32], epi=lambda acc, g, r: (r + _sigmoid(g) * acc,), epi_args=(gp, x2), name=f'ple_{tag}')
    sv.update(hn=hn, proj=proj, aux=aux, cfc=cfc, cfr=cfr, lse=lse, ys=ys, gb=gb, o_dn=o_dn,
              states=states, merged=merged, x1=x1, hf=hf, up=up, act=act, x2=x2, hp=hp, gp=gp, p=p_i,
              bias_row=bias_row, alog_row=alog_row, w=w)
    return x3, sv


def _rms_bwd(x, g, dh, dres, name):
    def fn(xv, dhv, dr, gv):
        _, vjp = jax.vjp(_rms, xv, gv)
        dx, dg = vjp(dhv)
        return dr + dx, dg
    return _ew(fn, [x, dh, dres], [g], [(D_MODEL, F32)], [(1, D_MODEL)], name=name)


def _layer_bwd(dx3, sv, sp, tag, early_grads=None):
    t = dx3.shape[0]
    w = sv['w']
    g = {}
    def ple_epi(acc, gpv, d):
        s = _sigmoid(gpv)
        return d * acc * s * (1.0 - s), d * s
    dgp, de = _mm(sv['p'], w['w_ple'], 'nn', [BF16, BF16], epi=ple_epi, epi_args=(sv['gp'], dx3), name=f'ple_bwd_{tag}')
    g['w_ple'] = _mm(sv['p'], de, 'tn', [F32], name=f'd_w_ple_{tag}')
    g['w_ple_gate'] = _mm(sv['hp'], dgp, 'tn', [F32], name=f'd_w_ple_gate_{tag}')
    dhp = _mm(dgp, w['w_ple_gate'], 'nt', [F32], name=f'd_hp_{tag}')
    dx2, g['g_ple'] = _rms_bwd(sv['x2'], sp['g_ple'], dhp, dx3, f'rms_ple_bwd_{tag}')
    dact = _mm(dx2, w['w_down'], 'nt', [F32], name=f'd_act_{tag}')
    g['w_down'] = _mm(sv['act'], dx2, 'tn', [F32], name=f'd_w_down_{tag}')

    def ffn_bwd_tile(ug, uv, da, wg, wv):
        cg, cv = _conv(ug, wg), _conv(uv, wv)
        _, vjp = jax.vjp(_ffn_act, cg, cv)
        dcg, dcv = vjp(da)
        dug, dwg = _conv_bwd(ug, wg, dcg)
        duv, dwv = _conv_bwd(uv, wv, dcv)
        return dug, duv, jnp.concatenate(dwg, axis=0), jnp.concatenate(dwv, axis=0)
    dupg, dupv, dwg, dwv = _cb(ffn_bwd_tile, [(sv['up'], 0), (sv['up'], D_FF), (dact, 0)],
                               [(w['ffn_conv_w'], 0), (w['ffn_conv_w'], D_FF)], [BF16, BF16], [3, 3], tc=256,
                               nblk=D_FF // 256, name=f'ffn_act_bwd_{tag}')
    dup = jnp.concatenate([dupg, dupv], axis=1)
    g['ffn_conv_w'] = jnp.concatenate([dwg, dwv], axis=1)
    g['w_up'] = _mm(sv['hf'], dup, 'tn', [F32], name=f'd_w_up_{tag}')
    dhf = _mm(dup, w['w_up'], 'nt', [F32], name=f'd_hf_{tag}')
    dx1, g['g_ffn'] = _rms_bwd(sv['x1'], sp['g_ffn'], dhf, dx2, f'rms_ffn_bwd_{tag}')
    dmerged = _mm(dx1, w['w_o'], 'nt', [F32], name=f'd_merged_{tag}')
    g['w_o'] = _mm(sv['merged'], dx1, 'tn', [F32], name=f'd_w_o_{tag}')
    dgate, dtn = _merge_bwd(sv['ys'], w['w_branch'], sv['proj'], dmerged, f'merge_bwd_{tag}')
    dys, dwb = [], []
    for n in range(3):
        dys.append(_mm(dtn[n], w['w_branch'][n], 'nt', [F32], name=f'd_y{n}_{tag}'))
        dwb.append(_mm(sv['ys'][n], dtn[n], 'tn', [F32], name=f'd_w_branch{n}_{tag}'))
    g['w_branch'] = jnp.stack(dwb)
    if early_grads is not None:
        sp = early_grads(g, sp)
    ddq, ddk, ddv, ddz, dwq, dwk, dwv_, dgb, dng = _dn_bwd(sv['proj'], w['dn_conv_w'], sv['gb'], sp['dn_norm_gain'],
                                                           sv['o_dn'], sv['states'], dys[2], f'dn_bwd_{tag}')
    g['dn_conv_w'] = jnp.concatenate([dwq, dwk, dwv_], axis=1)
    g['dn_norm_gain'] = jnp.sum(dng, axis=0)
    def sc_bwd_tile(sb, sc, svv, dy, wv):
        u = sc * svv
        dsb = dy * _conv(u, wv)
        du, dws = _conv_bwd(u, wv, dy * sb)
        return dsb, du * svv, du * sc, jnp.concatenate(dws, axis=0)
    dsb, dsc, dsv, g['sc_conv_w'] = _cb(sc_bwd_tile, [(sv['proj'], C_SB), (sv['proj'], C_SC), (sv['proj'], C_SV), (dys[1], 0)],
                                        [(w['sc_conv_w'], 0)], [BF16, BF16, BF16], [3], tc=256, nblk=2, name=f'sc_bwd_{tag}')
    dfq, dfk, dfv, dcfc, dcfr, dqg, dkg = _att_bwd(sv['proj'], sp['fox_q_gain'], sp['fox_k_gain'], sv['cfc'], sv['cfr'],
                                                   sv['lse'], sv['ys'][0], dys[0], f'fox_bwd_{tag}')
    g['fox_q_gain'] = jnp.sum(dqg, axis=0)
    g['fox_k_gain'] = jnp.sum(dkg, axis=0)
    dcf = (dcfc[:, :, 0] + dcfr[:, 0, :]).T
    daux = jnp.concatenate([dcf, dgb[:, :, 1].T, dgb[:, :, 0].T, jnp.zeros((t, 112), F32)], axis=1)
    dsmall, dbias, dalog = _prep_bwd(sv['proj'], sv['bias_row'], sv['alog_row'], daux, f'prep_bwd_{tag}')
    g['b_fox_f'] = dbias[:, 0:8]
    g['dn_dt_bias'] = dbias[:, 12:16]
    g['dn_a_log'] = dalog[:, 12:16]
    dproj = jnp.concatenate([dfq, dfk, dfv, dsb, dsc, dsv, ddq, ddk, ddv, ddz, dgate[0], dgate[1], dgate[2], dsmall], axis=1)
    g['w_in'] = _mm(sv['hn'], dproj, 'tn', [F32], name=f'd_w_in_{tag}')
    dhn = _mm(dproj, w['w_in'], 'nt', [F32], name=f'd_hn_{tag}')
    dx, g['g_mix'] = _rms_bwd(sv['x'], sp['g_mix'], dhn, dx1, f'rms_mix_bwd_{tag}')
    return dx, g


def _loss_bwd(y, target, name):
    inv = 1.0 / y.shape[1]

    def fn(yv, tv):
        err = yv - tv
        return err * inv, jnp.zeros((8, 128), F32) + 0.5 * inv * jnp.sum(err * err)
    return _ew(fn, [y, target], [], [(y.shape[1], F32)], [(8, 128)], name=name)


PACK_W = 1024
FULL_SHAPE = {'w_in': (D_MODEL, IN_ORIG), 'w_branch': (3, BW, D_MODEL), 'w_o': (D_MODEL, D_MODEL), 'w_up': (D_MODEL, 2 * D_FF),
              'w_down': (D_FF, D_MODEL), 'w_ple_gate': (D_MODEL, D_MODEL), 'w_ple': (PLE_DIM, D_MODEL),
              'sc_conv_w': (3, BW), 'dn_conv_w': (4, 3 * BW), 'ffn_conv_w': (3, 2 * D_FF)}
SMALL_SHAPE = {'g_mix': D_MODEL, 'b_fox_f': FOX_HEADS, 'fox_q_gain': FOX_DH, 'fox_k_gain': FOX_DH, 'dn_a_log': DN_HEADS,
               'dn_dt_bias': DN_HEADS, 'dn_norm_gain': DN_DH, 'g_ffn': D_MODEL, 'g_ple': D_MODEL}


def _shard_shape(name):
    s = list(FULL_SHAPE[name])
    s[SHARDED[name]] //= N_DEV
    return tuple(s)


def _full_from_gathered(g, name):
    sh, ax = _shard_shape(name), SHARDED[name]
    blocks = jnp.moveaxis(g, 0, ax)
    return blocks.reshape(sh[:ax] + (N_DEV * sh[ax],) + sh[ax + 1:])


def _by_dest(full, name):
    sh, ax = _shard_shape(name), SHARDED[name]
    return jnp.moveaxis(full.reshape(sh[:ax] + (N_DEV, sh[ax]) + sh[ax + 1:]), ax, 0)


def _flat_pack(arrs):
    flat = jnp.concatenate([a.reshape(-1).astype(F32) for a in arrs])
    rows = -(-flat.shape[0] // (8 * PACK_W)) * 8
    return jnp.pad(flat, (0, rows * PACK_W - flat.shape[0])).reshape(rows, PACK_W)


def _flat_unpack(pack, shapes):
    flat, out, off = pack.reshape(-1), [], 0
    for s in shapes:
        n = 1
        for d in s:
            n *= d
        out.append(flat[off:off + n].reshape(s))
        off += n
    return out


SMALL_SHAPES = [(DEPTH, SMALL_SHAPE[n]) for n in SMALL]
SMALL_LOSS_AT = sum(DEPTH * SMALL_SHAPE[n] for n in SMALL)
CONV_SHARD_SHAPES = [(DEPTH,) + _shard_shape(n) for n in CONVW]
CONV_FULL_SHAPES = [(DEPTH,) + FULL_SHAPE[n] for n in CONVW]


def _pick_tm(m, width):
    best = None
    for tm in range(16, m + 1, 16):
        if m % tm == 0 and tm * width * 4 <= (1 << 20):
            best = tm
    return best if best is not None else m


HBM_SPEC = pl.BlockSpec(memory_space=pltpu.HBM)


def _place():
    x, y, c = lax.axis_index("x"), lax.axis_index("y"), lax.axis_index("c")
    return x, y, c, [(1 - x, y), (x, 1 - y), (1 - x, 1 - y)]


def _all_gather(arrs, name):
    n = len(arrs)

    def body(*refs):
        ins, outs = refs[:n], refs[n:2 * n]
        send_sems, recv_sems, local_sems = refs[2 * n:]
        x, y, c, chips = _place()
        me, sibling = (x, y, c), (x, y, 1 - c)

        def block(a, p):
            return outs[a].at[4 * p[0] + 2 * p[1] + p[2]]

        def copy(a, k, blk, to, src=None):
            return pltpu.make_async_remote_copy(src_ref=block(a, blk) if src is None else src, dst_ref=block(a, blk),
                                                send_sem=send_sems.at[a, k], recv_sem=recv_sems.at[a, k],
                                                device_id=to, device_id_type=MESH)

        mine = [pltpu.make_async_copy(ins[a], block(a, me), local_sems.at[a]) for a in range(n)]
        first, passed = [], []
        for a in range(n):
            mine[a].start()
            first.append(copy(a, 0, me, sibling, src=ins[a]))
            first += [copy(a, 1 + j, me, (*chip, c), src=ins[a]) for j, chip in enumerate(chips)]
        for cp in first:
            cp.start()
        for j, chip in enumerate(chips):
            for a in range(n):
                copy(a, 1 + j, (*chip, c), me).wait_recv()
                fwd = copy(a, 4 + j, (*chip, c), sibling)
                fwd.start()
                passed.append(fwd)
        for a in range(n):
            copy(a, 0, sibling, me).wait_recv()
            for j, chip in enumerate(chips):
                copy(a, 4 + j, (*chip, 1 - c), me).wait_recv()
        for cp in first + passed:
            cp.wait_send()
        for a in range(n):
            mine[a].wait()

    return pl.pallas_call(
        body, in_specs=[HBM_SPEC] * n, out_specs=[HBM_SPEC] * n,
        out_shape=[jax.ShapeDtypeStruct((N_DEV,) + a.shape, a.dtype) for a in arrs],
        scratch_shapes=[pltpu.SemaphoreType.DMA((n, 7)), pltpu.SemaphoreType.DMA((n, 7)), pltpu.SemaphoreType.DMA((n,))],
        name=name,
    )(*arrs)


ANY_SPEC = pl.BlockSpec(memory_space=pl.ANY)
SEM_SPEC = pl.BlockSpec(memory_space=pltpu.SEMAPHORE)
DATAFLOW = pltpu.SideEffectType.DATAFLOW_SIDE_EFFECTING
ALL_PEERS = (1, 2, 3, 4, 5, 6, 7)
NEAR_PEERS = (1, 4, 2, 6)


def _direct_copies(src_refs, zone_refs, send_sems, recv_sems, per_dest, peers):
    x, y, c, _ = _place()
    me = 4 * x + 2 * y + c
    cps = []
    for a, (src, zone) in enumerate(zip(src_refs, zone_refs)):
        for r, bits in enumerate(peers):
            px = 1 - x if bits & 4 else x
            py = 1 - y if bits & 2 else y
            pc = 1 - c if bits & 1 else c
            cps.append(pltpu.make_async_remote_copy(
                src_ref=src.at[4 * px + 2 * py + pc] if per_dest else src, dst_ref=zone.at[me],
                send_sem=send_sems.at[a * len(peers) + r], recv_sem=recv_sems.at[a * len(peers) + r],
                device_id=(px, py, pc), device_id_type=MESH))
    return cps


def _direct_start(srcs, per_dest, after, name, peers=ALL_PEERS):
    n = len(srcs)
    zones = [lax.empty((N_DEV,) + (s.shape[1:] if per_dest else s.shape), s.dtype) for s in srcs]

    def body(*refs):
        src_refs, zone_refs = refs[:n], refs[n:2 * n]
        send_sems, recv_sems = refs[2 * n + 1:2 * n + 3]
        for cp in _direct_copies(src_refs, zone_refs, send_sems, recv_sems, per_dest, peers):
            cp.start()
        refs[-1][...] = jnp.zeros_like(refs[-1])

    sems = pltpu.SemaphoreType.DMA((n * len(peers),))
    res = pl.pallas_call(
        body, name=name, in_specs=[HBM_SPEC] * (2 * n) + [ANY_SPEC],
        out_shape=[sems, sems] + [pltpu.HBM(s.shape, s.dtype) for s in srcs] + [pltpu.HBM(z.shape, z.dtype) for z in zones]
        + [jax.ShapeDtypeStruct((8, 128), F32)],
        out_specs=[SEM_SPEC, SEM_SPEC] + [HBM_SPEC] * (2 * n) + [pl.BlockSpec(memory_space=pltpu.VMEM)],
        input_output_aliases={i: 2 + i for i in range(2 * n)},
        compiler_params=pltpu.CompilerParams(has_side_effects=DATAFLOW),
    )(*[pltpu.with_memory_space_constraint(s, pltpu.HBM) for s in srcs],
      *[pltpu.with_memory_space_constraint(z, pltpu.HBM) for z in zones], after)
    return (res[0], res[1], list(res[2:2 + n]), list(res[2 + n:2 + 2 * n]), per_dest, peers), res[-1][0:1, 0:1]


def _direct_wait(started, after, name):
    send_sems, recv_sems, srcs, zones, per_dest, peers = started
    n = len(srcs)

    def body(*refs):
        src_refs, zone_refs = refs[:n], refs[n:2 * n]
        for cp in _direct_copies(src_refs, zone_refs, refs[2 * n], refs[2 * n + 1], per_dest, peers):
            cp.wait_send()
            cp.wait_recv()

    res = pl.pallas_call(
        body, name=name, in_specs=[HBM_SPEC] * (2 * n) + [SEM_SPEC, SEM_SPEC, ANY_SPEC],
        out_shape=[pltpu.HBM(s.shape, s.dtype) for s in srcs] + [pltpu.HBM(z.shape, z.dtype) for z in zones],
        out_specs=[HBM_SPEC] * (2 * n), input_output_aliases={i: i for i in range(2 * n)},
        compiler_params=pltpu.CompilerParams(has_side_effects=DATAFLOW),
    )(*srcs, *zones, send_sems, recv_sems, after)
    return list(res[n:])


def _gather_finish(zones, shards, name):
    n = len(zones)

    def body(*refs):
        shard_refs, zone_in, zone_out = refs[:n], refs[n:2 * n], refs[2 * n:3 * n]
        send_sems, recv_sems, local_sems = refs[3 * n:]
        x, y, c, chips = _place()
        cps = []
        for a in range(n):
            mine = pltpu.make_async_copy(shard_refs[a], zone_out[a].at[4 * x + 2 * y + c], local_sems.at[a])
            mine.start()
            cps.append(mine)
            for j, (cx, cy) in enumerate(chips):
                blk = 4 * cx + 2 * cy + c
                cp = pltpu.make_async_remote_copy(src_ref=zone_in[a].at[blk], dst_ref=zone_out[a].at[blk],
                                                  send_sem=send_sems.at[a, j], recv_sem=recv_sems.at[a, j],
                                                  device_id=(x, y, 1 - c), device_id_type=MESH)
                cp.start()
                cps.append(cp)
        for cp in cps:
            cp.wait()

    return pl.pallas_call(
        body, in_specs=[HBM_SPEC] * (2 * n), out_specs=[HBM_SPEC] * n,
        out_shape=[jax.ShapeDtypeStruct(z.shape, z.dtype) for z in zones], input_output_aliases={n + a: a for a in range(n)},
        scratch_shapes=[pltpu.SemaphoreType.DMA((n, 3)), pltpu.SemaphoreType.DMA((n, 3)), pltpu.SemaphoreType.DMA((n,))],
        name=name,
    )(*shards, *zones)


def _adamw(w, g, m, v):
    m = ADAM_B1 * m + (1.0 - ADAM_B1) * g
    v = ADAM_B2 * v + (1.0 - ADAM_B2) * jnp.square(g)
    m_hat = m / (1.0 - ADAM_B1 ** ADAM_STEP)
    v_hat = v / (1.0 - ADAM_B2 ** ADAM_STEP)
    delta = -ADAM_LR * (m_hat / (jnp.sqrt(v_hat) + ADAM_EPS) + ADAM_WD * w)
    return delta, m, v


def _adamw_layer(w, m, v, own, parts, layer, prev, name):
    _, rows, c = w.shape
    tm = _pick_tm(rows, c)

    def body(w_ref, m_ref, v_ref, own_ref, parts_ref, *rest):
        g = own_ref[...]
        for j in range(parts_ref.shape[0]):
            g = g + parts_ref[j].astype(F32)
        delta, m2, v2 = _adamw(w_ref[...], g, m_ref[...], v_ref[...])
        g_ref, d_ref, m2_ref, v2_ref = rest[-4:]
        g_ref[...], d_ref[...], m2_ref[...], v2_ref[...] = g, delta, m2, v2

    wspec = pl.BlockSpec((None, tm, c), lambda i: (layer, i, 0))
    prev = list(prev) if prev is not None else []
    return pl.pallas_call(
        body, grid=(rows // tm,),
        in_specs=[wspec] * 3 + [pl.BlockSpec((tm, c), lambda i: (i, 0)), pl.BlockSpec((parts.shape[0], tm, c), lambda i: (0, i, 0))]
        + [ANY_SPEC] * len(prev),
        out_specs=[wspec] * 4, out_shape=[jax.ShapeDtypeStruct(w.shape, F32)] * 4,
        input_output_aliases={5 + k: k for k in range(len(prev))}, name=name,
    )(w, m, v, own, parts, *prev)


def _adamw_call(w, m, v, gparts, name):
    def fn(wv, mv, vv, *gs):
        g = gs[0].astype(F32)
        for gp in gs[1:]:
            g = g + gp.astype(F32)
        delta, m2, v2 = _adamw(wv, g, mv, vv)
        return g, delta, m2, v2
    rows, c = w.shape
    return _ew(fn, [w, m, v] + list(gparts), [], [(c, F32)] * 4, tm=_pick_tm(rows, c), name=name)


def kernel(x, p, g_mix, w_in, b_fox_f, fox_q_gain, fox_k_gain, sc_conv_w, dn_conv_w, dn_a_log, dn_dt_bias,
           dn_norm_gain, w_branch, w_o, g_ffn, w_up, ffn_conv_w, w_down, g_ple, w_ple_gate, w_ple, loss_target,
           m_g_mix, m_w_in, m_b_fox_f, m_fox_q_gain, m_fox_k_gain, m_sc_conv_w, m_dn_conv_w, m_dn_a_log,
           m_dn_dt_bias, m_dn_norm_gain, m_w_branch, m_w_o, m_g_ffn, m_w_up, m_ffn_conv_w, m_w_down, m_g_ple,
           m_w_ple_gate, m_w_ple, v_g_mix, v_w_in, v_b_fox_f, v_fox_q_gain, v_fox_k_gain, v_sc_conv_w, v_dn_conv_w,
           v_dn_a_log, v_dn_dt_bias, v_dn_norm_gain, v_w_branch, v_w_o, v_g_ffn, v_w_up, v_ffn_conv_w, v_w_down,
           v_g_ple, v_w_ple_gate, v_w_ple):
    return _step(x, p, g_mix, w_in, b_fox_f, fox_q_gain, fox_k_gain, sc_conv_w, dn_conv_w, dn_a_log, dn_dt_bias,
                 dn_norm_gain, w_branch, w_o, g_ffn, w_up, ffn_conv_w, w_down, g_ple, w_ple_gate, w_ple, loss_target,
                 m_g_mix, m_w_in, m_b_fox_f, m_fox_q_gain, m_fox_k_gain, m_sc_conv_w, m_dn_conv_w, m_dn_a_log,
                 m_dn_dt_bias, m_dn_norm_gain, m_w_branch, m_w_o, m_g_ffn, m_w_up, m_ffn_conv_w, m_w_down, m_g_ple,
                 m_w_ple_gate, m_w_ple, v_g_mix, v_w_in, v_b_fox_f, v_fox_q_gain, v_fox_k_gain, v_sc_conv_w,
                 v_dn_conv_w, v_dn_a_log, v_dn_dt_bias, v_dn_norm_gain, v_w_branch, v_w_o, v_g_ffn, v_w_up,
                 v_ffn_conv_w, v_w_down, v_g_ple, v_w_ple_gate, v_w_ple)


def _step(*args):
    names = ['x', 'p'] + WEIGHTS + ['loss_target'] + ['m_' + n for n in WEIGHTS] + ['v_' + n for n in WEIGHTS]
    assert len(args) == len(names)
    a = dict(zip(names, args))
    x, target = a['x'][0], a['loss_target'][0]
    p = a['p'][:, 0]
    dev = 4 * lax.axis_index("x") + 2 * lax.axis_index("y") + lax.axis_index("c")

    LATE = [n for n in BIG if n != 'w_in']
    sps = [{n: a[n][layer][None, :] for n in SMALL} for layer in range(DEPTH)]
    shards = [{n: a[n][layer].astype(BF16) for n in BIG} for layer in range(DEPTH)]

    def by_device(full, n):
        by_dest = _by_dest(full, n)
        return lax.dynamic_index_in_dim(by_dest, dev, axis=0, keepdims=False), by_dest.astype(BF16)

    def blank(zone):
        return lax.dynamic_update_index_in_dim(zone, jnp.zeros(zone.shape[1:], zone.dtype), dev, 0)

    w_in_all, conv_all = _all_gather([shards[0]['w_in'], _flat_pack([a[n] for n in CONVW])], 'gather_w_in_l0')
    conv_by_dev = [_flat_unpack(conv_all[d], CONV_SHARD_SHAPES) for d in range(N_DEV)]
    conv_full = {n: jnp.concatenate([conv_by_dev[d][i] for d in range(N_DEV)], axis=2) for i, n in enumerate(CONVW)}
    late0_started, zero = _direct_start([shards[0][n] for n in LATE], False, w_in_all, 'weights_rest_l0_start', NEAR_PEERS)
    w1_started, zero = _direct_start([shards[1][n] for n in BIG], False, late0_started[2][0], 'weights_l1_start', NEAR_PEERS)
    sps[0]['g_mix'] = sps[0]['g_mix'] + zero

    def late_weights0(after):
        zones = _gather_finish(_direct_wait(late0_started, after, 'weights_rest_l0_wait'), [shards[0][n] for n in LATE],
                               'weights_rest_l0_finish')
        return {n: _full_from_gathered(z, n) for n, z in zip(LATE, zones)}

    w0 = {n: conv_full[n][0] for n in CONVW}
    w0['w_in'] = _w_in_assemble(w_in_all, 'w_in_assemble_l0')
    h, sv0 = _layer_fwd(x, p[0], w0, sps[0], 'l0', late_weights0)
    zones = _gather_finish(_direct_wait(w1_started, h, 'weights_l1_wait'), [shards[1][n] for n in BIG], 'weights_l1_finish')
    w1 = {n: _full_from_gathered(z, n) for n, z in zip(BIG, zones) if n != 'w_in'}
    w1.update({n: conv_full[n][1] for n in CONVW})
    w1['w_in'] = _w_in_assemble(zones[list(BIG).index('w_in')], 'w_in_assemble_l1')
    h, sv1 = _layer_fwd(h, p[1], w1, sps[1], 'l1')
    dh, loss_part = _loss_bwd(h, target, 'loss')
    grads = [None] * DEPTH
    dh, grads[1] = _layer_bwd(dh, sv1, sps[1], 'l1')
    owns1, sends1 = {}, {}
    for n in BIG:
        owns1[n], sends1[n] = _w_in_split_by_device(grads[1][n], 'w_in_grad_split_l1') if n == 'w_in' else by_device(grads[1][n], n)
    g1_started, zero = _direct_start([sends1[n] for n in BIG], True, dh, 'grads_l1_start')
    early = {}

    def early_grads0(g, sp):
        owns, sends = zip(*[by_device(g[n], n) for n in LATE])
        early['started'], zero = _direct_start(list(sends), True, g['w_branch'], 'grads_rest_l0_start')
        early['owns'] = dict(zip(LATE, owns))
        return {**sp, 'dn_norm_gain': sp['dn_norm_gain'] + zero}

    dh, grads[0] = _layer_bwd(dh, sv0, {**sps[0], 'g_ple': sps[0]['g_ple'] + zero}, 'l0', early_grads0)
    grad_x = dh[None]

    own_in0, send_in0 = _w_in_split_by_device(grads[0]['w_in'], 'w_in_grad_split_l0')
    in0_started, zero = _direct_start([send_in0], True, dh, 'grads_w_in_l0_start')
    sent = in0_started[2][0]
    zones1 = dict(zip(BIG, _direct_wait(g1_started, sent, 'grads_l1_wait')))
    zones0 = dict(zip(LATE, _direct_wait(early['started'], sent, 'grads_rest_l0_wait')))

    def adamw(n, layer, own, zone, prev):
        rows = (-1, a[n].shape[-1])
        own = own.reshape(rows)
        view = lambda t: t.reshape((DEPTH,) + own.shape)
        return _adamw_layer(view(a[n]), view(a['m_' + n]), view(a['v_' + n]), own, blank(zone).reshape((N_DEV,) + own.shape),
                            layer, prev, f'adamw_{n}_l{layer}')

    out = {}
    for n in LATE:
        res = adamw(n, 0, early['owns'][n], zones0[n], adamw(n, 1, owns1[n], zones1[n], None))
        out[n] = [r.reshape(a[n].shape) for r in res]
    res = adamw('w_in', 1, owns1['w_in'], zones1['w_in'], None)
    (zone_in0,) = _direct_wait(in0_started, res[0], 'grads_w_in_l0_wait')
    out['w_in'] = [r.reshape(a['w_in'].shape) for r in adamw('w_in', 0, own_in0, zone_in0, res)]

    small_pack = _flat_pack([jnp.stack([grads[layer][n].reshape(-1) for layer in range(DEPTH)]) for n in SMALL] + [loss_part[0, 0]])
    conv_pack = _flat_pack([jnp.stack([grads[layer][n] for layer in range(DEPTH)]) for n in CONVW])
    small_all, conv_all = _all_gather([small_pack, conv_pack], 'gather_small_grads')
    res = _adamw_call(_flat_pack([a[n] for n in SMALL]), _flat_pack([a['m_' + n] for n in SMALL]),
                      _flat_pack([a['v_' + n] for n in SMALL]), [small_all[d] for d in range(N_DEV)], 'adamw_replicated')
    loss = res[0].reshape(-1)[SMALL_LOSS_AT]
    for k, r in enumerate(res):
        for n, val in zip(SMALL, _flat_unpack(r, SMALL_SHAPES)):
            out.setdefault(n, [None] * 4)[k] = val
    (conv_sum,) = _ew(lambda *gs: (functools.reduce(lambda s, t: s + t, gs),), [conv_all[d] for d in range(N_DEV)], [],
                      [(PACK_W, F32)], tm=conv_pack.shape[0], name='conv_grad_sum')
    conv_own = [lax.dynamic_slice_in_dim(g, dev * (g.shape[2] // N_DEV), g.shape[2] // N_DEV, axis=2)
                for g in _flat_unpack(conv_sum, CONV_FULL_SHAPES)]
    res = _adamw_call(_flat_pack([a[n] for n in CONVW]), _flat_pack([a['m_' + n] for n in CONVW]),
                      _flat_pack([a['v_' + n] for n in CONVW]), [_flat_pack(conv_own)], 'adamw_conv')
    for k, r in enumerate(res):
        for n, val in zip(CONVW, _flat_unpack(r, CONV_SHARD_SHAPES)):
            out.setdefault(n, [None] * 4)[k] = val

    outs = [loss, grad_x]
    for k in range(4):
        outs += [out[n][k] for n in WEIGHTS]
    return tuple(outs)
```

```python
import functools

import jax
import jax.numpy as jnp
from jax import lax
from jax.experimental import pallas as pl
from jax.experimental.pallas import tpu as pltpu

F32 = jnp.float32
BF16 = jnp.bfloat16
HI = lax.Precision.HIGHEST

D_MODEL = 1024
DEPTH = 2
N_DEV = 8
PLE_DIM = 256
BW = 512
FOX_HEADS, FOX_DH = 8, 64
DN_HEADS, DN_DH = 4, 128
DN_CHUNK = 64
D_FF = 2816
EPS = 1e-6
NEG = -1e30

ADAM_LR, ADAM_B1, ADAM_B2, ADAM_EPS, ADAM_WD, ADAM_STEP = 0.001, 0.9, 0.999, 1e-08, 0.01, 10

C_FQ, C_FK, C_FV = 0, 512, 1024
C_SB, C_SC, C_SV = 1536, 2048, 2560
C_DQ, C_DK, C_DV, C_DZ = 3072, 3584, 4096, 4608
C_GATE = 5120
C_SMALL = 8192
IN_P = 8320
IN_ORIG = 8208

WEIGHTS = ['g_mix', 'w_in', 'b_fox_f', 'fox_q_gain', 'fox_k_gain', 'sc_conv_w', 'dn_conv_w', 'dn_a_log',
           'dn_dt_bias', 'dn_norm_gain', 'w_branch', 'w_o', 'g_ffn', 'w_up', 'ffn_conv_w', 'w_down', 'g_ple',
           'w_ple_gate', 'w_ple']
BIG = {'w_in': 1, 'w_branch': 2, 'w_o': 0, 'w_up': 1, 'w_down': 0, 'w_ple_gate': 0, 'w_ple': 1}
CONVW = {'sc_conv_w': 1, 'dn_conv_w': 1, 'ffn_conv_w': 1}
SHARDED = {**BIG, **CONVW}
SMALL = [n for n in WEIGHTS if n not in SHARDED]
MESH = pl.DeviceIdType.MESH


def _sigmoid(x):
    return 0.5 * (jnp.tanh(0.5 * x) + 1.0)


def _silu(x):
    return x * _sigmoid(x)


def _log1pexp_negabs(z):
    return jnp.log(1.0 + jnp.exp(-jnp.abs(z)))


def _log_sigmoid(z):
    return jnp.minimum(z, 0.0) - _log1pexp_negabs(z)


def _softplus(z):
    return jnp.maximum(z, 0.0) + _log1pexp_negabs(z)


def _rms(x, g):
    return x * lax.rsqrt(jnp.mean(x * x, axis=-1, keepdims=True) + EPS) * g


def _l2(x):
    return x * lax.rsqrt(jnp.sum(x * x, axis=-1, keepdims=True) + EPS)


def _dot(a, b, dims, precision=None):
    return lax.dot_general(a, b, (dims, ((), ())), preferred_element_type=F32, precision=precision)


NN = ((1,), (0,))
NT = ((1,), (1,))
TN = ((0,), (0,))


def _shift_down(x, s):
    if s == 0:
        return x
    t = lax.broadcasted_iota(jnp.int32, x.shape, 0)
    return jnp.where(t >= s, pltpu.roll(x, s, 0), 0.0)


def _shift_up(x, s):
    if s == 0:
        return x
    n = x.shape[0]
    t = lax.broadcasted_iota(jnp.int32, x.shape, 0)
    return jnp.where(t < n - s, pltpu.roll(x, n - s, 0), 0.0)


def _conv(x, w):
    k = w.shape[0]
    y = w[k - 1:k] * x
    for j in range(k - 1):
        y = y + w[j:j + 1] * _shift_down(x, k - 1 - j)
    return y


def _conv_bwd(x, w, dy):
    k = w.shape[0]
    dx = w[k - 1:k] * dy
    dws = []
    for j in range(k - 1):
        dx = dx + w[j:j + 1] * _shift_up(dy, k - 1 - j)
        dws.append(jnp.sum(dy * _shift_down(x, k - 1 - j), axis=0, keepdims=True))
    dws.append(jnp.sum(dy * x, axis=0, keepdims=True))
    return dx, dws


MM_VMEM_BUDGET = 36 << 20
MM_STEP_BYTES = 1 << 20


def _mm_tiles(m, n, k, a_size, b_size, tile_size, cast_a):
    best = None
    for tm in [d for d in (2048, 1024, 512, 256, 128) if d <= m and m % d == 0] or [m]:
        for tn in [d for d in range(128, min(n, 2048) + 1, 128) if n % d == 0] or [n]:
            vmem = 2 * tm * k * a_size + (2 * tm * k if cast_a else 0) + 2 * k * tn * b_size + 2 * tm * tn * tile_size
            if vmem > MM_VMEM_BUDGET:
                continue
            steps = (m // tm) * (n // tn)
            cost = m * k * a_size + (m // tm) * k * n * b_size + m * n * tile_size + steps * MM_STEP_BYTES
            if best is None or cost < best[0]:
                best = (cost, tm, tn)
    assert best is not None, (m, n, k)
    return best[1], best[2]


def _mm(a, b, mode, outs, *, epi=None, epi_args=(), name):
    if mode == 'nn':
        (m, k), (k2, n) = a.shape, b.shape
    elif mode == 'nt':
        (m, k), (n, k2) = a.shape, b.shape
    else:
        (k, m), (k2, n) = a.shape, b.shape
    assert k == k2, (a.shape, b.shape, mode)
    tile_size = sum(jnp.dtype(dt).itemsize for dt in outs) + sum(e.dtype.itemsize for e in epi_args if e.shape[0] != 1)
    tm, tn = _mm_tiles(m, n, k, a.dtype.itemsize, b.dtype.itemsize, tile_size, a.dtype != BF16)
    dims = {'nn': NN, 'nt': NT, 'tn': TN}[mode]
    a_spec = pl.BlockSpec((k, tm), lambda i, j: (0, i)) if mode == 'tn' else pl.BlockSpec((tm, k), lambda i, j: (i, 0))
    b_spec = pl.BlockSpec((tn, k), lambda i, j: (j, 0)) if mode == 'nt' else pl.BlockSpec((k, tn), lambda i, j: (0, j))
    e_specs = [pl.BlockSpec((1, tn), lambda i, j: (0, j)) if e.shape[0] == 1 else pl.BlockSpec((tm, tn), lambda i, j: (i, j))
               for e in epi_args]
    ne, no = len(epi_args), len(outs)
    cast_a = a.dtype != BF16

    def body(a_ref, b_ref, *rest):
        if cast_a:
            a_sc = rest[-1]

            @pl.when(pl.program_id(1) == 0)
            def _():
                a_sc[...] = a_ref[...].astype(BF16)
            av = a_sc[...]
        else:
            av = a_ref[...]
        acc = _dot(av, b_ref[...].astype(BF16), dims)
        vals = epi(acc, *[e[...] for e in rest[:ne]]) if epi is not None else (acc,)
        for o_ref, v in zip(rest[ne:ne + no], vals):
            o_ref[...] = v.astype(o_ref.dtype)

    res = pl.pallas_call(
        body, grid=(m // tm, n // tn),
        in_specs=[a_spec, b_spec] + e_specs,
        out_specs=[pl.BlockSpec((tm, tn), lambda i, j: (i, j)) for _ in outs],
        out_shape=[jax.ShapeDtypeStruct((m, n), dt) for dt in outs],
        scratch_shapes=[pltpu.VMEM(a_spec.block_shape, BF16)] if cast_a else [],
        name=name,
    )(a, b, *epi_args)
    return res[0] if no == 1 else res


def _ew(fn, tiled, bcast, outs, reds=(), *, tm=256, name):
    secs = [(t, 0, t.shape[1]) if not isinstance(t, tuple) else t for t in tiled]
    m = secs[0][0].shape[0]
    tm = min(tm, m)
    assert m % tm == 0
    in_specs = []
    for arr, off, w in secs:
        assert off % w == 0
        in_specs.append(pl.BlockSpec((tm, w), functools.partial(lambda i, c: (i, c), c=off // w)))
    in_specs += [pl.BlockSpec(b.shape, lambda i: (0, 0)) for b in bcast]
    nin, no = len(in_specs), len(outs)

    def body(*refs):
        vals = fn(*[r[...] for r in refs[:nin]])
        for r, v in zip(refs[nin:nin + no], vals[:no]):
            r[...] = v.astype(r.dtype)
        i = pl.program_id(0)
        for r, v in zip(refs[nin + no:], vals[no:]):
            @pl.when(i == 0)
            def _():
                r[...] = v

            @pl.when(i > 0)
            def _():
                r[...] += v

    res = pl.pallas_call(
        body, grid=(m // tm,), in_specs=in_specs,
        out_specs=[pl.BlockSpec((tm, c), lambda i: (i, 0)) for c, _ in outs] + [pl.BlockSpec(s, lambda i: (0, 0)) for s in reds],
        out_shape=[jax.ShapeDtypeStruct((m, c), dt) for c, dt in outs] + [jax.ShapeDtypeStruct(s, F32) for s in reds],
        name=name,
    )(*[s[0] for s in secs], *bcast)
    return res


def _cb(fn, cols, params, outs, pouts, *, tc, nblk, name):
    t = cols[0][0].shape[0]
    in_specs = []
    for arr, off in cols:
        assert off % tc == 0
        in_specs.append(pl.BlockSpec((t, tc), functools.partial(lambda c, o: (0, o + c), o=off // tc)))
    for arr, off in params:
        in_specs.append(pl.BlockSpec((arr.shape[0], tc), functools.partial(lambda c, o: (0, o + c), o=off // tc)))
    nin, no = len(in_specs), len(outs)

    def body(*refs):
        vals = fn(*[r[...] for r in refs[:nin]])
        for r, v in zip(refs[nin:], vals):
            r[...] = v.astype(r.dtype)

    return pl.pallas_call(
        body, grid=(nblk,), in_specs=in_specs,
        out_specs=[pl.BlockSpec((t, tc), lambda c: (0, c)) for _ in outs] + [pl.BlockSpec((k, tc), lambda c: (0, c)) for k in pouts],
        out_shape=[jax.ShapeDtypeStruct((t, nblk * tc), dt) for dt in outs] + [jax.ShapeDtypeStruct((k, nblk * tc), F32) for k in pouts],
        name=name,
    )(*[c[0] for c in cols], *[p[0] for p in params])


CUM_BLK = 256


def _prep_point(s, bias, alog):
    lane = lax.broadcasted_iota(jnp.int32, s.shape, 1)
    z = s + bias
    return jnp.where(lane < 8, _log_sigmoid(z),
                     jnp.where(lane < 12, _sigmoid(s),
                               jnp.where(lane < 16, -jnp.exp(alog) * _softplus(z), 0.0)))


def _tri(n, upper):
    r = lax.broadcasted_iota(jnp.int32, (n, n), 0)
    c = lax.broadcasted_iota(jnp.int32, (n, n), 1)
    return (r <= c if upper else r >= c).astype(F32)


def _prep_fwd(proj, bias_row, alog_row, name):
    t = proj.shape[0]
    nb = t // CUM_BLK

    def body(s_ref, b_ref, a_ref, o_ref):
        pre = _prep_point(s_ref[...], b_ref[...], a_ref[...])
        lane = lax.broadcasted_iota(jnp.int32, (CUM_BLK, 128), 1)
        tri = _tri(CUM_BLK, False)
        carry = jnp.zeros((1, 128), F32)
        for blk in range(nb):
            xb = pre[blk * CUM_BLK:(blk + 1) * CUM_BLK]
            cb = _dot(tri, xb, NN, HI) + carry
            carry = cb[CUM_BLK - 1:CUM_BLK]
            o_ref[blk * CUM_BLK:(blk + 1) * CUM_BLK, :] = jnp.where(lane < 8, cb, xb)

    return pl.pallas_call(
        body, grid=(1,),
        in_specs=[pl.BlockSpec((t, 128), lambda i: (0, C_SMALL // 128)), pl.BlockSpec((1, 128), lambda i: (0, 0)),
                  pl.BlockSpec((1, 128), lambda i: (0, 0))],
        out_specs=pl.BlockSpec((t, 128), lambda i: (0, 0)),
        out_shape=jax.ShapeDtypeStruct((t, 128), F32), name=name,
    )(proj, bias_row, alog_row)


def _prep_bwd(proj, bias_row, alog_row, daux, name):
    t = proj.shape[0]
    nb = t // CUM_BLK

    def body(s_ref, b_ref, a_ref, d_ref, ds_ref, db_ref, da_ref, dpre_sc):
        lane = lax.broadcasted_iota(jnp.int32, (CUM_BLK, 128), 1)
        tri = _tri(CUM_BLK, True)
        carry = jnp.zeros((1, 128), F32)
        for blk in reversed(range(nb)):
            db = d_ref[blk * CUM_BLK:(blk + 1) * CUM_BLK, :]
            cb = _dot(tri, db, NN, HI) + carry
            carry = cb[0:1]
            dpre_sc[blk * CUM_BLK:(blk + 1) * CUM_BLK, :] = jnp.where(lane < 8, cb, db)
        _, vjp = jax.vjp(_prep_point, s_ref[...], b_ref[...], a_ref[...])
        ds, dbias, dalog = vjp(dpre_sc[...])
        ds_ref[...] = ds.astype(ds_ref.dtype)
        db_ref[...] = dbias
        da_ref[...] = dalog

    return pl.pallas_call(
        body, grid=(1,),
        in_specs=[pl.BlockSpec((t, 128), lambda i: (0, C_SMALL // 128)), pl.BlockSpec((1, 128), lambda i: (0, 0)),
                  pl.BlockSpec((1, 128), lambda i: (0, 0)), pl.BlockSpec((t, 128), lambda i: (0, 0))],
        out_specs=[pl.BlockSpec((t, 128), lambda i: (0, 0)), pl.BlockSpec((1, 128), lambda i: (0, 0)),
                   pl.BlockSpec((1, 128), lambda i: (0, 0))],
        out_shape=[jax.ShapeDtypeStruct((t, 128), BF16), jax.ShapeDtypeStruct((1, 128), F32), jax.ShapeDtypeStruct((1, 128), F32)],
        scratch_shapes=[pltpu.VMEM((t, 128), F32)], name=name,
    )(proj, bias_row, alog_row, daux)


ATT_TQ = 256
FOX_SCALE = FOX_DH ** -0.5


def _qnorm(q, g):
    return _rms(q, g) * FOX_SCALE


def _att_scores(qn_blk, kn, cfc_blk, cfr, qi, tq, kend):
    s = _dot(qn_blk.astype(BF16), kn[:kend].astype(BF16), NT) + cfc_blk - cfr[:, :kend]
    row = lax.broadcasted_iota(jnp.int32, (tq, kend), 0) + qi * tq
    col = lax.broadcasted_iota(jnp.int32, (tq, kend), 1)
    return s, row >= col


ATT_PAIR = 128 // FOX_DH


def _att_specs(t):
    pair = lambda off: pl.BlockSpec((t, 128), functools.partial(lambda i, o: (0, o + i), o=off // 128))
    gain = pl.BlockSpec((1, FOX_DH), lambda i: (0, 0))
    hd = lambda i: (i, 0, 0)
    col, row = pl.BlockSpec((ATT_PAIR, t, 1), hd), pl.BlockSpec((ATT_PAIR, 1, t), hd)
    return [pair(C_FQ), pair(C_FK), pair(C_FV), gain, gain, col, row], pl.BlockSpec((t, 128), lambda i: (0, i)), col, row


def _att_fwd(proj, qg, kg, cfc, cfr, name):
    t = proj.shape[0]
    tq = min(ATT_TQ, t)
    in_specs, pair_out, col, _ = _att_specs(t)

    def body(q_ref, k_ref, v_ref, qg_ref, kg_ref, cfc_ref, cfr_ref, o_ref, lse_ref):
        for e in range(ATT_PAIR):
            lanes = slice(e * FOX_DH, (e + 1) * FOX_DH)
            qn = _qnorm(q_ref[:, lanes], qg_ref[...])
            kn = _rms(k_ref[:, lanes], kg_ref[...])
            v = v_ref[:, lanes].astype(BF16)
            cfr = cfr_ref[e]
            for qi in range(t // tq):
                kend = (qi + 1) * tq
                rows = slice(qi * tq, kend)
                s, mask = _att_scores(qn[rows], kn, cfc_ref[e, rows, :], cfr, qi, tq, kend)
                s = jnp.where(mask, s, NEG)
                m = jnp.max(s, axis=1, keepdims=True)
                p = jnp.exp(s - m)
                l = jnp.sum(p, axis=1, keepdims=True)
                o_ref[rows, lanes] = _dot(p.astype(BF16), v[:kend], NN) / l
                lse_ref[e, rows, :] = m + jnp.log(l)

    return pl.pallas_call(
        body, grid=(FOX_HEADS // ATT_PAIR,), in_specs=in_specs, out_specs=[pair_out, col],
        out_shape=[jax.ShapeDtypeStruct((t, BW), F32), jax.ShapeDtypeStruct((FOX_HEADS, t, 1), F32)], name=name,
    )(proj, proj, proj, qg, kg, cfc, cfr)


def _att_bwd(proj, qg, kg, cfc, cfr, lse, o, do, name):
    t = proj.shape[0]
    tq = min(ATT_TQ, t)
    in_specs, pair_out, col, row = _att_specs(t)

    def body(q_ref, k_ref, v_ref, qg_ref, kg_ref, cfc_ref, cfr_ref, lse_ref, o_ref, do_ref,
             dq_ref, dk_ref, dv_ref, dcfc_ref, dcfr_ref, dqg_ref, dkg_ref, dqn_sc, dkn_sc, dv_sc, dcfr_sc):
        for e in range(ATT_PAIR):
            lanes = slice(e * FOX_DH, (e + 1) * FOX_DH)
            qn, vjp_q = jax.vjp(_qnorm, q_ref[:, lanes], qg_ref[...])
            kn, vjp_k = jax.vjp(_rms, k_ref[:, lanes], kg_ref[...])
            v = v_ref[:, lanes].astype(BF16)
            cfr = cfr_ref[e]
            do_e = do_ref[:, lanes]
            delta = jnp.sum(do_e * o_ref[:, lanes], axis=1, keepdims=True)
            dkn_sc[...] = jnp.zeros_like(dkn_sc)
            dv_sc[...] = jnp.zeros_like(dv_sc)
            dcfr_sc[...] = jnp.zeros_like(dcfr_sc)
            for qi in range(t // tq):
                kend = (qi + 1) * tq
                rows = slice(qi * tq, kend)
                s, mask = _att_scores(qn[rows], kn, cfc_ref[e, rows, :], cfr, qi, tq, kend)
                p = jnp.where(mask, jnp.exp(jnp.where(mask, s, NEG) - lse_ref[e, rows, :]), 0.0)
                do_b = do_e[rows].astype(BF16)
                dv_sc[0:kend, :] += _dot(p.astype(BF16), do_b, TN)
                dp = _dot(do_b, v[:kend], NT)
                ds = p * (dp - delta[rows])
                ds_b = ds.astype(BF16)
                dqn_sc[rows, :] = _dot(ds_b, kn[:kend].astype(BF16), NN)
                dkn_sc[0:kend, :] += _dot(ds_b, qn[rows].astype(BF16), TN)
                dcfc_ref[e, rows, :] = jnp.sum(ds, axis=1, keepdims=True)
                dcfr_sc[:, 0:kend] -= jnp.sum(ds, axis=0, keepdims=True)
            dq, dqg = vjp_q(dqn_sc[...])
            dk, dkg = vjp_k(dkn_sc[...])
            dq_ref[:, lanes] = dq.astype(dq_ref.dtype)
            dk_ref[:, lanes] = dk.astype(dk_ref.dtype)
            dv_ref[:, lanes] = dv_sc[...].astype(dv_ref.dtype)
            dcfr_ref[e] = dcfr_sc[...]
            dqg_ref[e] = dqg
            dkg_ref[e] = dkg

    gsp = pl.BlockSpec((ATT_PAIR, 1, FOX_DH), lambda i: (i, 0, 0))
    return pl.pallas_call(
        body, grid=(FOX_HEADS // ATT_PAIR,),
        in_specs=in_specs + [col, pair_out, pair_out],
        out_specs=[pair_out] * 3 + [col, row, gsp, gsp],
        out_shape=[jax.ShapeDtypeStruct((t, BW), BF16)] * 3
        + [jax.ShapeDtypeStruct((FOX_HEADS, t, 1), F32), jax.ShapeDtypeStruct((FOX_HEADS, 1, t), F32)]
        + [jax.ShapeDtypeStruct((FOX_HEADS, 1, FOX_DH), F32)] * 2,
        scratch_shapes=[pltpu.VMEM((t, FOX_DH), F32)] * 3 + [pltpu.VMEM((1, t), F32)], name=name,
    )(proj, proj, proj, qg, kg, cfc, cfr, lse, o, do)


DN_SCALE = DN_DH ** -0.5


DN_BATCH = 8


@functools.partial(jax.custom_vjp, nondiff_argnums=(2, 3))
def _mm3(a, b, dims, batch=False):
    return _mm3_passes(a, b, dims, batch)


def _mm3_fwd(a, b, dims, batch):
    return _mm3_passes(a, b, dims, batch), (a, b)


def _mm3_bwd(dims, batch, res, dc):
    a, b = res
    if dims == NN:
        return _mm3_passes(dc, b, NT, batch), _mm3_passes(a, dc, TN, batch)
    if dims == NT:
        return _mm3_passes(dc, b, NN, batch), _mm3_passes(dc, a, TN, batch)
    return _mm3_passes(b, dc, NT, batch), _mm3_passes(a, dc, NN, batch)


_mm3.defvjp(_mm3_fwd, _mm3_bwd)


def _mm3_passes(a, b, dims, batch):
    if batch:
        dn = (((dims[0][0] + 1,), (dims[1][0] + 1,)), ((0,), (0,)))
        dot = lambda p, q: lax.dot_general(p, q, dn, preferred_element_type=F32)
    else:
        dot = lambda p, q: _dot(p, q, dims)
    ah, bh = a.astype(BF16), b.astype(BF16)
    al, bl = (a - ah.astype(F32)).astype(BF16), (b - bh.astype(F32)).astype(BF16)
    return dot(ah, bh) + (dot(ah, bl) + dot(al, bh))


def _dn_local(qc, kc, vc, g, beta):
    nb, c, _ = qc.shape
    ii = lax.broadcasted_iota(jnp.int32, (c, c), 0)
    jj = lax.broadcasted_iota(jnp.int32, (c, c), 1)
    incl, strict = ii >= jj, ii > jj
    lower = jnp.broadcast_to(incl.astype(F32), (nb, c, c))
    eye = (ii == jj).astype(F32)
    mm = functools.partial(_mm3, batch=True)
    dm = mm(lower, jnp.where(strict, g, 0.0), NN)
    decay = jnp.where(incl, jnp.exp(jnp.where(incl, dm, 0.0)), 0.0)
    gcum = mm(lower, g * jnp.ones((1, 1, DN_DH), F32), NN)
    eg = jnp.exp(gcum)
    glast = gcum[:, c - 1:c]
    kb = kc * beta
    n1 = jnp.where(strict, mm(kb, kc, NT) * decay, 0.0)
    inv = eye - n1
    pw = n1
    for _ in range(5):
        pw = mm(pw, pw, NN)
        inv = inv + mm(pw, inv, NN)
    sol = mm(inv, jnp.concatenate([vc * beta, kb * eg], axis=2), NN)
    qk = jnp.where(incl, mm(qc, kc, NT) * decay, 0.0)
    return sol[:, :, :DN_DH], sol[:, :, DN_DH:], qk, qc * eg, kc * jnp.exp(glast - gcum), jnp.exp(glast)


def _dn_state(u, kcum, qk, qdec, kdec, egl, state):
    v_new = u - _mm3(kcum, state, NN)
    out = _mm3(qdec, state, NN) + _mm3(qk, v_new, NN)
    return out, state * egl + _mm3(kdec, v_new, TN)


def _dn_pre_q(c):
    return _l2(_silu(c)) * DN_SCALE


def _dn_pre_k(c):
    return _l2(_silu(c))


def _dn_post(o, z, ng):
    return _rms(o, ng) * _silu(z)


def _dn_specs(t):
    cblk = lambda o: pl.BlockSpec((t, DN_DH), functools.partial(lambda h, o: (0, o + h), o=o // DN_DH))
    wblk = lambda o: pl.BlockSpec((4, DN_DH), functools.partial(lambda h, o: (0, o + h), o=o // DN_DH))
    proj_specs = [cblk(C_DQ), cblk(C_DK), cblk(C_DV), cblk(C_DZ)]
    w_specs = [wblk(0), wblk(BW), wblk(2 * BW)]
    gb_spec = pl.BlockSpec((1, t, 2), lambda h: (h, 0, 0))
    return proj_specs, w_specs, gb_spec


def _chunk_rows(n, count=1):
    return pl.ds(pl.multiple_of(n * DN_CHUNK, DN_CHUNK), count * DN_CHUNK)


def _egl_rows(n, count=1):
    return pl.ds(pl.multiple_of(n * 8, 8), count * 8)


def _dn_local_inputs(n, qn_sc, kn_sc, vv_sc, gb_ref):
    r = _chunk_rows(n, DN_BATCH)
    split = lambda v: v.reshape(DN_BATCH, DN_CHUNK, v.shape[-1])
    gbv = split(gb_ref[0, r, :])
    return split(qn_sc[r, :]), split(kn_sc[r, :]), split(vv_sc[r, :]), gbv[:, :, 0:1], gbv[:, :, 1:2]


def _dn_local_phase(nc, qn_sc, kn_sc, vv_sc, gb_ref, loc):
    def step(i, carry):
        n = i * DN_BATCH
        vals = _dn_local(*_dn_local_inputs(n, qn_sc, kn_sc, vv_sc, gb_ref))
        for sc, val in zip(loc[:5], vals[:5]):
            sc[_chunk_rows(n, DN_BATCH), :] = val.reshape(DN_BATCH * DN_CHUNK, val.shape[-1])
        loc[5][_egl_rows(n, DN_BATCH), :] = jnp.broadcast_to(vals[5], (DN_BATCH, 8, DN_DH)).reshape(DN_BATCH * 8, DN_DH)
        return carry

    lax.fori_loop(0, nc // DN_BATCH, step, 0)


def _dn_loc_scratch(t, nc):
    big = pltpu.VMEM((t, DN_DH), F32)
    return [big, big, pltpu.VMEM((t, DN_CHUNK), F32), big, big, pltpu.VMEM((nc * 8, DN_DH), F32)]


def _dn_fwd(proj, conv_w, gb, ng, name):
    t = proj.shape[0]
    nc = t // DN_CHUNK
    assert nc % DN_BATCH == 0
    proj_specs, w_specs, gb_spec = _dn_specs(t)

    def body(q_ref, k_ref, v_ref, z_ref, wq_ref, wk_ref, wv_ref, gb_ref, ng_ref, y_ref, o_ref, st_ref, qn_sc, kn_sc, vv_sc, *loc):
        qn_sc[...] = _dn_pre_q(_conv(q_ref[...], wq_ref[...]))
        kn_sc[...] = _dn_pre_k(_conv(k_ref[...], wk_ref[...]))
        vv_sc[...] = _silu(_conv(v_ref[...], wv_ref[...]))
        _dn_local_phase(nc, qn_sc, kn_sc, vv_sc, gb_ref, loc)
        u_sc, kcum_sc, qk_sc, qdec_sc, kdec_sc, egl_sc = loc

        def chunk(n, state):
            r = _chunk_rows(n)
            egl = egl_sc[_egl_rows(n), :][0:1]
            out, new_state = _dn_state(u_sc[r, :], kcum_sc[r, :], qk_sc[r, :], qdec_sc[r, :], kdec_sc[r, :], egl, state)
            st_ref[0, n] = state
            o_ref[r, :] = out
            return new_state

        lax.fori_loop(0, nc, chunk, jnp.zeros((DN_DH, DN_DH), F32))
        y_ref[...] = _dn_post(o_ref[...], z_ref[...], ng_ref[...])

    hblk = pl.BlockSpec((t, DN_DH), lambda h: (0, h))
    return pl.pallas_call(
        body, grid=(DN_HEADS,),
        in_specs=proj_specs + w_specs + [gb_spec, pl.BlockSpec((1, DN_DH), lambda h: (0, 0))],
        out_specs=[hblk, hblk, pl.BlockSpec((1, nc, DN_DH, DN_DH), lambda h: (h, 0, 0, 0))],
        out_shape=[jax.ShapeDtypeStruct((t, BW), F32), jax.ShapeDtypeStruct((t, BW), F32),
                   jax.ShapeDtypeStruct((DN_HEADS, nc, DN_DH, DN_DH), F32)],
        scratch_shapes=[pltpu.VMEM((t, DN_DH), F32)] * 3 + _dn_loc_scratch(t, nc), name=name,
    )(proj, proj, proj, proj, conv_w, conv_w, conv_w, gb, ng)


def _dn_bwd(proj, conv_w, gb, ng, o, states, dy, name):
    t = proj.shape[0]
    nc = t // DN_CHUNK
    proj_specs, w_specs, gb_spec = _dn_specs(t)
    nloc = 6

    def body(q_ref, k_ref, v_ref, z_ref, wq_ref, wk_ref, wv_ref, gb_ref, ng_ref, o_ref, st_ref, dy_ref,
             dq_ref, dk_ref, dv_ref, dz_ref, dwq_ref, dwk_ref, dwv_ref, dgb_ref, dng_ref,
             qn_sc, kn_sc, vv_sc, do_sc, *rest):
        loc, dloc = rest[:nloc], rest[nloc:]
        qn_sc[...] = _dn_pre_q(_conv(q_ref[...], wq_ref[...]))
        kn_sc[...] = _dn_pre_k(_conv(k_ref[...], wk_ref[...]))
        vv_sc[...] = _silu(_conv(v_ref[...], wv_ref[...]))
        _, vjp_y = jax.vjp(_dn_post, o_ref[...], z_ref[...], ng_ref[...])
        do, dz, dng = vjp_y(dy_ref[...])
        do_sc[...] = do
        dz_ref[...] = dz.astype(dz_ref.dtype)
        dng_ref[0] = dng
        _dn_local_phase(nc, qn_sc, kn_sc, vv_sc, gb_ref, loc)
        u_sc, kcum_sc, qk_sc, qdec_sc, kdec_sc, egl_sc = loc

        def state_bwd(i, dstate):
            n = nc - 1 - i
            r = _chunk_rows(n)
            r8 = _egl_rows(n)
            _, vjp = jax.vjp(_dn_state, u_sc[r, :], kcum_sc[r, :], qk_sc[r, :], qdec_sc[r, :], kdec_sc[r, :],
                             egl_sc[r8, :][0:1], st_ref[0, n])
            du, dkcum, dqk, dqdec, dkdec, degl, dprev = vjp((do_sc[r, :], dstate))
            for d_sc, val in zip(dloc[:5], (du, dkcum, dqk, dqdec, dkdec)):
                d_sc[r, :] = val
            dloc[5][r8, :] = jnp.broadcast_to(degl, (8, DN_DH))
            return dprev

        lax.fori_loop(0, nc, state_bwd, jnp.zeros((DN_DH, DN_DH), F32))

        def local_bwd(i, carry):
            n = i * DN_BATCH
            r = _chunk_rows(n, DN_BATCH)
            _, vjp = jax.vjp(_dn_local, *_dn_local_inputs(n, qn_sc, kn_sc, vv_sc, gb_ref))
            cts = tuple(d_sc[r, :].reshape(DN_BATCH, DN_CHUNK, d_sc.shape[-1]) for d_sc in dloc[:5])
            cts += (dloc[5][_egl_rows(n, DN_BATCH), :].reshape(DN_BATCH, 8, DN_DH)[:, 0:1],)
            dqc, dkc, dvc, dg, dbeta = vjp(cts)
            for d_sc, val in zip((dloc[0], dloc[1], dloc[3]), (dqc, dkc, dvc)):
                d_sc[r, :] = val.reshape(DN_BATCH * DN_CHUNK, DN_DH)
            dgb_ref[0, r, :] = jnp.concatenate([dg, dbeta], axis=2).reshape(DN_BATCH * DN_CHUNK, 2)
            return carry

        lax.fori_loop(0, nc // DN_BATCH, local_bwd, 0)
        for x_ref, w_ref, pre, d_sc, dx_ref, dw_ref in ((q_ref, wq_ref, _dn_pre_q, dloc[0], dq_ref, dwq_ref),
                                                       (k_ref, wk_ref, _dn_pre_k, dloc[1], dk_ref, dwk_ref),
                                                       (v_ref, wv_ref, _silu, dloc[3], dv_ref, dwv_ref)):
            _, vjp = jax.vjp(pre, _conv(x_ref[...], w_ref[...]))
            (dc,) = vjp(d_sc[...])
            dx, dws = _conv_bwd(x_ref[...], w_ref[...], dc)
            dx_ref[...] = dx.astype(dx_ref.dtype)
            for j, dw in enumerate(dws):
                dw_ref[j:j + 1, :] = dw

    hblk = pl.BlockSpec((t, DN_DH), lambda h: (0, h))
    wout = pl.BlockSpec((4, DN_DH), lambda h: (0, h))
    return pl.pallas_call(
        body, grid=(DN_HEADS,),
        in_specs=proj_specs + w_specs + [gb_spec, pl.BlockSpec((1, DN_DH), lambda h: (0, 0)), hblk,
                                         pl.BlockSpec((1, nc, DN_DH, DN_DH), lambda h: (h, 0, 0, 0)), hblk],
        out_specs=[hblk] * 4 + [wout] * 3 + [gb_spec, pl.BlockSpec((1, 1, DN_DH), lambda h: (h, 0, 0))],
        out_shape=[jax.ShapeDtypeStruct((t, BW), BF16)] * 4 + [jax.ShapeDtypeStruct((4, BW), F32)] * 3
        + [jax.ShapeDtypeStruct((DN_HEADS, t, 2), F32), jax.ShapeDtypeStruct((DN_HEADS, 1, DN_DH), F32)],
        scratch_shapes=[pltpu.VMEM((t, DN_DH), F32)] * 4 + _dn_loc_scratch(t, nc) * 2, name=name,
    )(proj, proj, proj, proj, conv_w, conv_w, conv_w, gb, ng, o, states, dy)


MERGE_TM, MERGE_TN = 512, 512


def _merge_specs(t):
    tm, tn = min(MERGE_TM, t), MERGE_TN
    y_spec = pl.BlockSpec((tm, BW), lambda i, j: (i, 0))
    w_spec = pl.BlockSpec((3, BW, tn), lambda i, j: (0, 0, j))
    gate_specs = [pl.BlockSpec((tm, tn), functools.partial(lambda i, j, o: (i, o + j), o=(C_GATE + n * D_MODEL) // tn))
                  for n in range(3)]
    return tm, tn, [y_spec] * 3 + [w_spec] + gate_specs


def _merge_fwd(ys, wb, proj, name):
    t = proj.shape[0]
    tm, tn, in_specs = _merge_specs(t)

    def body(y0, y1, y2, w_ref, g0, g1, g2, o_ref):
        acc = jnp.zeros((tm, tn), F32)
        for n, (y, g) in enumerate(((y0, g0), (y1, g1), (y2, g2))):
            acc = acc + _dot(y[...].astype(BF16), w_ref[n], NN) * _sigmoid(g[...])
        o_ref[...] = acc.astype(o_ref.dtype)

    return pl.pallas_call(
        body, grid=(t // tm, D_MODEL // tn), in_specs=in_specs,
        out_specs=pl.BlockSpec((tm, tn), lambda i, j: (i, j)),
        out_shape=jax.ShapeDtypeStruct((t, D_MODEL), BF16), name=name,
    )(*ys, wb, proj, proj, proj)


def _merge_bwd(ys, wb, proj, dmerged, name):
    t = proj.shape[0]
    tm, tn, in_specs = _merge_specs(t)

    def body(y0, y1, y2, w_ref, g0, g1, g2, dm_ref, dg_ref, dt_ref):
        dm = dm_ref[...]
        for n, (y, g) in enumerate(((y0, g0), (y1, g1), (y2, g2))):
            tn_ = _dot(y[...].astype(BF16), w_ref[n], NN)
            sg = _sigmoid(g[...])
            dg_ref[n] = (dm * tn_ * sg * (1.0 - sg)).astype(dg_ref.dtype)
            dt_ref[n] = (dm * sg).astype(dt_ref.dtype)

    o3 = pl.BlockSpec((3, tm, tn), lambda i, j: (0, i, j))
    return pl.pallas_call(
        body, grid=(t // tm, D_MODEL // tn), in_specs=in_specs + [pl.BlockSpec((tm, tn), lambda i, j: (i, j))],
        out_specs=[o3, o3], out_shape=[jax.ShapeDtypeStruct((3, t, D_MODEL), BF16)] * 2, name=name,
    )(*ys, wb, proj, proj, proj, dmerged)


def _prep_rows(sp):
    z4, z112 = jnp.zeros((1, 4), F32), jnp.zeros((1, 112), F32)
    bias_row = jnp.concatenate([sp['b_fox_f'], z4, sp['dn_dt_bias'], z112], axis=1)
    alog_row = jnp.concatenate([jnp.zeros((1, 12), F32), sp['dn_a_log'], z112], axis=1)
    return bias_row, alog_row


def _sc_fwd_tile(sb, sc, sv, w):
    return (sb * _conv(sc * sv, w),)


def _ffn_act(ug, uv):
    return _silu(ug) * uv


W_IN_COLS = [(C_FQ, 0, 1536), (C_SB, 1544, 1536), (C_DQ, 3080, 1536), (C_DZ, 4624, 512), (C_GATE, 5136, 3072),
             (C_SMALL, 1536, 8), (C_SMALL + 8, 4616, 8)]
W_IN_SHARD = IN_ORIG // N_DEV


def _w_in_segments():
    out = []
    for d, o, w in W_IN_COLS:
        end = o + w
        while o < end:
            k = o // W_IN_SHARD
            n = min(end, (k + 1) * W_IN_SHARD) - o
            out.append((d, k, o - k * W_IN_SHARD, n))
            o, d = o + n, d + n
    return out


def _w_in_assemble(g, name):
    tm = 256

    def body(g_ref, o_ref):
        for d, k, s, n in _w_in_segments():
            o_ref[:, d:d + n] = g_ref[k, :, s:s + n]
        o_ref[:, IN_ORIG:IN_P] = jnp.zeros((tm, IN_P - IN_ORIG), o_ref.dtype)

    return pl.pallas_call(
        body, grid=(D_MODEL // tm,),
        in_specs=[pl.BlockSpec((N_DEV, tm, W_IN_SHARD), lambda i: (0, i, 0))],
        out_specs=pl.BlockSpec((tm, IN_P), lambda i: (i, 0)),
        out_shape=jax.ShapeDtypeStruct((D_MODEL, IN_P), g.dtype), name=name,
    )(g)


W_IN_SPLIT_TM = 128


def _w_in_split_by_device(g, name):
    tm = W_IN_SPLIT_TM

    def body(g_ref, own_ref, send_ref):
        me = 4 * lax.axis_index("x") + 2 * lax.axis_index("y") + lax.axis_index("c")
        for d, k, s, n in _w_in_segments():
            val = g_ref[:, d:d + n]
            send_ref[k, :, s:s + n] = val.astype(send_ref.dtype)

            @pl.when(me == k)
            def _():
                own_ref[:, s:s + n] = val

    return pl.pallas_call(
        body, grid=(D_MODEL // tm,), in_specs=[pl.BlockSpec((tm, IN_P), lambda i: (i, 0))],
        out_specs=[pl.BlockSpec((tm, W_IN_SHARD), lambda i: (i, 0)), pl.BlockSpec((N_DEV, tm, W_IN_SHARD), lambda i: (0, i, 0))],
        out_shape=[jax.ShapeDtypeStruct((D_MODEL, W_IN_SHARD), F32), jax.ShapeDtypeStruct((N_DEV, D_MODEL, W_IN_SHARD), BF16)],
        name=name,
    )(g)


def _layer_fwd(x, p_i, w, sp, tag, late_weights=None):
    t = x.shape[0]
    sv = {'x': x}
    (hn,) = _ew(lambda a, g: (_rms(a, g),), [x], [sp['g_mix']], [(D_MODEL, BF16)], name=f'rms_mix_{tag}')
    proj = _mm(hn, w['w_in'], 'nn', [F32], name=f'in_proj_{tag}')
    bias_row, alog_row = _prep_rows(sp)
    aux = _prep_fwd(proj, bias_row, alog_row, f'prep_{tag}')
    cf = aux[:, :FOX_HEADS].T
    cfc, cfr = cf[:, :, None], cf[:, None, :]
    y_fox, lse = _att_fwd(proj, sp['fox_q_gain'], sp['fox_k_gain'], cfc, cfr, f'fox_fwd_{tag}')
    (y_sc,) = _cb(_sc_fwd_tile, [(proj, C_SB), (proj, C_SC), (proj, C_SV)], [(w['sc_conv_w'], 0)], [F32], [],
                  tc=256, nblk=2, name=f'sc_fwd_{tag}')
    gb = jnp.stack([aux[:, 12:16].T, aux[:, 8:12].T], axis=-1)
    y_dn, o_dn, states = _dn_fwd(proj, w['dn_conv_w'], gb, sp['dn_norm_gain'], f'dn_fwd_{tag}')
    ys = (y_fox, y_sc, y_dn)
    if late_weights is not None:
        w = {**w, **late_weights(y_dn)}
    merged = _merge_fwd(ys, w['w_branch'], proj, f'merge_fwd_{tag}')
    x1 = _mm(merged, w['w_o'], 'nn', [F32], epi=lambda acc, r: (acc + r,), epi_args=(x,), name=f'o_proj_{tag}')
    (hf,) = _ew(lambda a, g: (_rms(a, g),), [x1], [sp['g_ffn']], [(D_MODEL, BF16)], name=f'rms_ffn_{tag}')
    up = _mm(hf, w['w_up'], 'nn', [F32], name=f'up_proj_{tag}')
    (act,) = _cb(lambda ug, uv, wg, wv: (_ffn_act(_conv(ug, wg), _conv(uv, wv)),), [(up, 0), (up, D_FF)],
                 [(w['ffn_conv_w'], 0), (w['ffn_conv_w'], D_FF)], [BF16], [], tc=256, nblk=D_FF // 256, name=f'ffn_act_{tag}')
    x2 = _mm(act, w['w_down'], 'nn', [F32], epi=lambda acc, r: (acc + r,), epi_args=(x1,), name=f'down_proj_{tag}')
    (hp,) = _ew(lambda a, g: (_rms(a, g),), [x2], [sp['g_ple']], [(D_MODEL, BF16)], name=f'rms_ple_{tag}')
    gp = _mm(hp, w['w_ple_gate'], 'nn', [F32], name=f'ple_gate_{tag}')
    x3 = _mm(p_i, w['w_ple'], 'nn', [F32], epi=lambda acc, g, r: (r + _sigmoid(g) * acc,), epi_args=(gp, x2), name=f'ple_{tag}')
    sv.update(hn=hn, proj=proj, aux=aux, cfc=cfc, cfr=cfr, lse=lse, ys=ys, gb=gb, o_dn=o_dn,
              states=states, merged=merged, x1=x1, hf=hf, up=up, act=act, x2=x2, hp=hp, gp=gp, p=p_i,
              bias_row=bias_row, alog_row=alog_row, w=w)
    return x3, sv


def _rms_bwd(x, g, dh, dres, name):
    def fn(xv, dhv, dr, gv):
        _, vjp = jax.vjp(_rms, xv, gv)
        dx, dg = vjp(dhv)
        return dr + dx, dg
    return _ew(fn, [x, dh, dres], [g], [(D_MODEL, F32)], [(1, D_MODEL)], name=name)


def _layer_bwd(dx3, sv, sp, tag, early_grads=None):
    t = dx3.shape[0]
    w = sv['w']
    g = {}
    def ple_epi(acc, gpv, d):
        s = _sigmoid(gpv)
        return d * acc * s * (1.0 - s), d * s
    dgp, de = _mm(sv['p'], w['w_ple'], 'nn', [BF16, BF16], epi=ple_epi, epi_args=(sv['gp'], dx3), name=f'ple_bwd_{tag}')
    g['w_ple'] = _mm(sv['p'], de, 'tn', [F32], name=f'd_w_ple_{tag}')
    g['w_ple_gate'] = _mm(sv['hp'], dgp, 'tn', [F32], name=f'd_w_ple_gate_{tag}')
    dhp = _mm(dgp, w['w_ple_gate'], 'nt', [F32], name=f'd_hp_{tag}')
    dx2, g['g_ple'] = _rms_bwd(sv['x2'], sp['g_ple'], dhp, dx3, f'rms_ple_bwd_{tag}')
    dact = _mm(dx2, w['w_down'], 'nt', [F32], name=f'd_act_{tag}')
    g['w_down'] = _mm(sv['act'], dx2, 'tn', [F32], name=f'd_w_down_{tag}')

    def ffn_bwd_tile(ug, uv, da, wg, wv):
        cg, cv = _conv(ug, wg), _conv(uv, wv)
        _, vjp = jax.vjp(_ffn_act, cg, cv)
        dcg, dcv = vjp(da)
        dug, dwg = _conv_bwd(ug, wg, dcg)
        duv, dwv = _conv_bwd(uv, wv, dcv)
        return dug, duv, jnp.concatenate(dwg, axis=0), jnp.concatenate(dwv, axis=0)
    dupg, dupv, dwg, dwv = _cb(ffn_bwd_tile, [(sv['up'], 0), (sv['up'], D_FF), (dact, 0)],
                               [(w['ffn_conv_w'], 0), (w['ffn_conv_w'], D_FF)], [BF16, BF16], [3, 3], tc=256,
                               nblk=D_FF // 256, name=f'ffn_act_bwd_{tag}')
    dup = jnp.concatenate([dupg, dupv], axis=1)
    g['ffn_conv_w'] = jnp.concatenate([dwg, dwv], axis=1)
    g['w_up'] = _mm(sv['hf'], dup, 'tn', [F32], name=f'd_w_up_{tag}')
    dhf = _mm(dup, w['w_up'], 'nt', [F32], name=f'd_hf_{tag}')
    dx1, g['g_ffn'] = _rms_bwd(sv['x1'], sp['g_ffn'], dhf, dx2, f'rms_ffn_bwd_{tag}')
    dmerged = _mm(dx1, w['w_o'], 'nt', [F32], name=f'd_merged_{tag}')
    g['w_o'] = _mm(sv['merged'], dx1, 'tn', [F32], name=f'd_w_o_{tag}')
    dgate, dtn = _merge_bwd(sv['ys'], w['w_branch'], sv['proj'], dmerged, f'merge_bwd_{tag}')
    dys, dwb = [], []
    for n in range(3):
        dys.append(_mm(dtn[n], w['w_branch'][n], 'nt', [F32], name=f'd_y{n}_{tag}'))
        dwb.append(_mm(sv['ys'][n], dtn[n], 'tn', [F32], name=f'd_w_branch{n}_{tag}'))
    g['w_branch'] = jnp.stack(dwb)
    if early_grads is not None:
        sp = early_grads(g, sp)
    ddq, ddk, ddv, ddz, dwq, dwk, dwv_, dgb, dng = _dn_bwd(sv['proj'], w['dn_conv_w'], sv['gb'], sp['dn_norm_gain'],
                                                           sv['o_dn'], sv['states'], dys[2], f'dn_bwd_{tag}')
    g['dn_conv_w'] = jnp.concatenate([dwq, dwk, dwv_], axis=1)
    g['dn_norm_gain'] = jnp.sum(dng, axis=0)
    def sc_bwd_tile(sb, sc, svv, dy, wv):
        u = sc * svv
        dsb = dy * _conv(u, wv)
        du, dws = _conv_bwd(u, wv, dy * sb)
        return dsb, du * svv, du * sc, jnp.concatenate(dws, axis=0)
    dsb, dsc, dsv, g['sc_conv_w'] = _cb(sc_bwd_tile, [(sv['proj'], C_SB), (sv['proj'], C_SC), (sv['proj'], C_SV), (dys[1], 0)],
                                        [(w['sc_conv_w'], 0)], [BF16, BF16, BF16], [3], tc=256, nblk=2, name=f'sc_bwd_{tag}')
    dfq, dfk, dfv, dcfc, dcfr, dqg, dkg = _att_bwd(sv['proj'], sp['fox_q_gain'], sp['fox_k_gain'], sv['cfc'], sv['cfr'],
                                                   sv['lse'], sv['ys'][0], dys[0], f'fox_bwd_{tag}')
    g['fox_q_gain'] = jnp.sum(dqg, axis=0)
    g['fox_k_gain'] = jnp.sum(dkg, axis=0)
    dcf = (dcfc[:, :, 0] + dcfr[:, 0, :]).T
    daux = jnp.concatenate([dcf, dgb[:, :, 1].T, dgb[:, :, 0].T, jnp.zeros((t, 112), F32)], axis=1)
    dsmall, dbias, dalog = _prep_bwd(sv['proj'], sv['bias_row'], sv['alog_row'], daux, f'prep_bwd_{tag}')
    g['b_fox_f'] = dbias[:, 0:8]
    g['dn_dt_bias'] = dbias[:, 12:16]
    g['dn_a_log'] = dalog[:, 12:16]
    dproj = jnp.concatenate([dfq, dfk, dfv, dsb, dsc, dsv, ddq, ddk, ddv, ddz, dgate[0], dgate[1], dgate[2], dsmall], axis=1)
    g['w_in'] = _mm(sv['hn'], dproj, 'tn', [F32], name=f'd_w_in_{tag}')
    dhn = _mm(dproj, w['w_in'], 'nt', [F32], name=f'd_hn_{tag}')
    dx, g['g_mix'] = _rms_bwd(sv['x'], sp['g_mix'], dhn, dx1, f'rms_mix_bwd_{tag}')
    return dx, g


def _loss_bwd(y, target, name):
    inv = 1.0 / y.shape[1]

    def fn(yv, tv):
        err = yv - tv
        return err * inv, jnp.zeros((8, 128), F32) + 0.5 * inv * jnp.sum(err * err)
    return _ew(fn, [y, target], [], [(y.shape[1], F32)], [(8, 128)], name=name)


PACK_W = 1024
FULL_SHAPE = {'w_in': (D_MODEL, IN_ORIG), 'w_branch': (3, BW, D_MODEL), 'w_o': (D_MODEL, D_MODEL), 'w_up': (D_MODEL, 2 * D_FF),
              'w_down': (D_FF, D_MODEL), 'w_ple_gate': (D_MODEL, D_MODEL), 'w_ple': (PLE_DIM, D_MODEL),
              'sc_conv_w': (3, BW), 'dn_conv_w': (4, 3 * BW), 'ffn_conv_w': (3, 2 * D_FF)}
SMALL_SHAPE = {'g_mix': D_MODEL, 'b_fox_f': FOX_HEADS, 'fox_q_gain': FOX_DH, 'fox_k_gain': FOX_DH, 'dn_a_log': DN_HEADS,
               'dn_dt_bias': DN_HEADS, 'dn_norm_gain': DN_DH, 'g_ffn': D_MODEL, 'g_ple': D_MODEL}


def _shard_shape(name):
    s = list(FULL_SHAPE[name])
    s[SHARDED[name]] //= N_DEV
    return tuple(s)


def _full_from_gathered(g, name):
    sh, ax = _shard_shape(name), SHARDED[name]
    blocks = jnp.moveaxis(g, 0, ax)
    return blocks.reshape(sh[:ax] + (N_DEV * sh[ax],) + sh[ax + 1:])


def _by_dest(full, name):
    sh, ax = _shard_shape(name), SHARDED[name]
    return jnp.moveaxis(full.reshape(sh[:ax] + (N_DEV, sh[ax]) + sh[ax + 1:]), ax, 0)


def _flat_pack(arrs):
    flat = jnp.concatenate([a.reshape(-1).astype(F32) for a in arrs])
    rows = -(-flat.shape[0] // (8 * PACK_W)) * 8
    return jnp.pad(flat, (0, rows * PACK_W - flat.shape[0])).reshape(rows, PACK_W)


def _flat_unpack(pack, shapes):
    flat, out, off = pack.reshape(-1), [], 0
    for s in shapes:
        n = 1
        for d in s:
            n *= d
        out.append(flat[off:off + n].reshape(s))
        off += n
    return out


SMALL_SHAPES = [(DEPTH, SMALL_SHAPE[n]) for n in SMALL]
SMALL_LOSS_AT = sum(DEPTH * SMALL_SHAPE[n] for n in SMALL)
CONV_SHARD_SHAPES = [(DEPTH,) + _shard_shape(n) for n in CONVW]
CONV_FULL_SHAPES = [(DEPTH,) + FULL_SHAPE[n] for n in CONVW]


def _pick_tm(m, width):
    best = None
    for tm in range(16, m + 1, 16):
        if m % tm == 0 and tm * width * 4 <= (1 << 20):
            best = tm
    return best if best is not None else m


HBM_SPEC = pl.BlockSpec(memory_space=pltpu.HBM)


def _place():
    x, y, c = lax.axis_index("x"), lax.axis_index("y"), lax.axis_index("c")
    return x, y, c, [(1 - x, y), (x, 1 - y), (1 - x, 1 - y)]


def _all_gather(arrs, name):
    n = len(arrs)

    def body(*refs):
        ins, outs = refs[:n], refs[n:2 * n]
        send_sems, recv_sems, local_sems = refs[2 * n:]
        x, y, c, chips = _place()
        me, sibling = (x, y, c), (x, y, 1 - c)

        def block(a, p):
            return outs[a].at[4 * p[0] + 2 * p[1] + p[2]]

        def copy(a, k, blk, to, src=None):
            return pltpu.make_async_remote_copy(src_ref=block(a, blk) if src is None else src, dst_ref=block(a, blk),
                                                send_sem=send_sems.at[a, k], recv_sem=recv_sems.at[a, k],
                                                device_id=to, device_id_type=MESH)

        mine = [pltpu.make_async_copy(ins[a], block(a, me), local_sems.at[a]) for a in range(n)]
        first, passed = [], []
        for a in range(n):
            mine[a].start()
            first.append(copy(a, 0, me, sibling, src=ins[a]))
            first += [copy(a, 1 + j, me, (*chip, c), src=ins[a]) for j, chip in enumerate(chips)]
        for cp in first:
            cp.start()
        for j, chip in enumerate(chips):
            for a in range(n):
                copy(a, 1 + j, (*chip, c), me).wait_recv()
                fwd = copy(a, 4 + j, (*chip, c), sibling)
                fwd.start()
                passed.append(fwd)
        for a in range(n):
            copy(a, 0, sibling, me).wait_recv()
            for j, chip in enumerate(chips):
                copy(a, 4 + j, (*chip, 1 - c), me).wait_recv()
        for cp in first + passed:
            cp.wait_send()
        for a in range(n):
            mine[a].wait()

    return pl.pallas_call(
        body, in_specs=[HBM_SPEC] * n, out_specs=[HBM_SPEC] * n,
        out_shape=[jax.ShapeDtypeStruct((N_DEV,) + a.shape, a.dtype) for a in arrs],
        scratch_shapes=[pltpu.SemaphoreType.DMA((n, 7)), pltpu.SemaphoreType.DMA((n, 7)), pltpu.SemaphoreType.DMA((n,))],
        name=name,
    )(*arrs)


SEM_SPEC = pl.BlockSpec(memory_space=pltpu.SEMAPHORE)
DATAFLOW = pltpu.SideEffectType.DATAFLOW_SIDE_EFFECTING
ALL_PEERS = (1, 2, 3, 4, 5, 6, 7)
NEAR_PEERS = (1, 4, 2, 6)


def _direct_copies(src_refs, zone_refs, send_sems, recv_sems, per_dest, peers):
    x, y, c, _ = _place()
    me = 4 * x + 2 * y + c
    cps = []
    for a, (src, zone) in enumerate(zip(src_refs, zone_refs)):
        for r, bits in enumerate(peers):
            px = 1 - x if bits & 4 else x
            py = 1 - y if bits & 2 else y
            pc = 1 - c if bits & 1 else c
            cps.append(pltpu.make_async_remote_copy(
                src_ref=src.at[4 * px + 2 * py + pc] if per_dest else src, dst_ref=zone.at[me],
                send_sem=send_sems.at[a * len(peers) + r], recv_sem=recv_sems.at[a * len(peers) + r],
                device_id=(px, py, pc), device_id_type=MESH))
    return cps


def _direct_start(srcs, per_dest, after, name, peers=ALL_PEERS):
    n = len(srcs)
    zones = [lax.empty((N_DEV,) + (s.shape[1:] if per_dest else s.shape), s.dtype) for s in srcs]

    def body(*refs):
        src_refs, zone_refs = refs[:n], refs[n:2 * n]
        send_sems, recv_sems = refs[2 * n + 1:2 * n + 3]
        for cp in _direct_copies(src_refs, zone_refs, send_sems, recv_sems, per_dest, peers):
            cp.start()
        refs[-1][...] = jnp.zeros_like(refs[-1])

    sems = pltpu.SemaphoreType.DMA((n * len(peers),))
    res = pl.pallas_call(
        body, name=name, in_specs=[HBM_SPEC] * (2 * n + 1),
        out_shape=[sems, sems] + [pltpu.HBM(s.shape, s.dtype) for s in srcs] + [pltpu.HBM(z.shape, z.dtype) for z in zones]
        + [jax.ShapeDtypeStruct((8, 128), F32)],
        out_specs=[SEM_SPEC, SEM_SPEC] + [HBM_SPEC] * (2 * n) + [pl.BlockSpec(memory_space=pltpu.VMEM)],
        input_output_aliases={i: 2 + i for i in range(2 * n)},
        compiler_params=pltpu.CompilerParams(has_side_effects=DATAFLOW),
    )(*[pltpu.with_memory_space_constraint(s, pltpu.HBM) for s in srcs],
      *[pltpu.with_memory_space_constraint(z, pltpu.HBM) for z in zones], after)
    return (res[0], res[1], list(res[2:2 + n]), list(res[2 + n:2 + 2 * n]), per_dest, peers), res[-1][0:1, 0:1]


def _direct_wait(started, after, name):
    send_sems, recv_sems, srcs, zones, per_dest, peers = started
    n = len(srcs)

    def body(*refs):
        src_refs, zone_refs = refs[:n], refs[n:2 * n]
        for cp in _direct_copies(src_refs, zone_refs, refs[2 * n], refs[2 * n + 1], per_dest, peers):
            cp.wait_send()
            cp.wait_recv()

    res = pl.pallas_call(
        body, name=name, in_specs=[HBM_SPEC] * (2 * n) + [SEM_SPEC, SEM_SPEC, HBM_SPEC],
        out_shape=[pltpu.HBM(s.shape, s.dtype) for s in srcs] + [pltpu.HBM(z.shape, z.dtype) for z in zones],
        out_specs=[HBM_SPEC] * (2 * n), input_output_aliases={i: i for i in range(2 * n)},
        compiler_params=pltpu.CompilerParams(has_side_effects=DATAFLOW),
    )(*srcs, *zones, send_sems, recv_sems, after)
    return list(res[n:])


def _gather_finish(zones, shards, name):
    n = len(zones)

    def body(*refs):
        shard_refs, zone_in, zone_out = refs[:n], refs[n:2 * n], refs[2 * n:3 * n]
        send_sems, recv_sems, local_sems = refs[3 * n:]
        x, y, c, chips = _place()
        cps = []
        for a in range(n):
            mine = pltpu.make_async_copy(shard_refs[a], zone_out[a].at[4 * x + 2 * y + c], local_sems.at[a])
            mine.start()
            cps.append(mine)
            for j, (cx, cy) in enumerate(chips):
                blk = 4 * cx + 2 * cy + c
                cp = pltpu.make_async_remote_copy(src_ref=zone_in[a].at[blk], dst_ref=zone_out[a].at[blk],
                                                  send_sem=send_sems.at[a, j], recv_sem=recv_sems.at[a, j],
                                                  device_id=(x, y, 1 - c), device_id_type=MESH)
                cp.start()
                cps.append(cp)
        for cp in cps:
            cp.wait()

    return pl.pallas_call(
        body, in_specs=[HBM_SPEC] * (2 * n), out_specs=[HBM_SPEC] * n,
        out_shape=[jax.ShapeDtypeStruct(z.shape, z.dtype) for z in zones], input_output_aliases={n + a: a for a in range(n)},
        scratch_shapes=[pltpu.SemaphoreType.DMA((n, 3)), pltpu.SemaphoreType.DMA((n, 3)), pltpu.SemaphoreType.DMA((n,))],
        name=name,
    )(*shards, *zones)


def _adamw(w, g, m, v):
    m = ADAM_B1 * m + (1.0 - ADAM_B1) * g
    v = ADAM_B2 * v + (1.0 - ADAM_B2) * jnp.square(g)
    m_hat = m / (1.0 - ADAM_B1 ** ADAM_STEP)
    v_hat = v / (1.0 - ADAM_B2 ** ADAM_STEP)
    delta = -ADAM_LR * (m_hat / (jnp.sqrt(v_hat) + ADAM_EPS) + ADAM_WD * w)
    return delta, m, v


def _adamw_layer(w, m, v, own, parts, layer, prev, name):
    _, rows, c = w.shape
    tm = _pick_tm(rows, c)

    def body(w_ref, m_ref, v_ref, own_ref, parts_ref, *rest):
        g = own_ref[...]
        for j in range(parts_ref.shape[0]):
            g = g + parts_ref[j].astype(F32)
        delta, m2, v2 = _adamw(w_ref[...], g, m_ref[...], v_ref[...])
        g_ref, d_ref, m2_ref, v2_ref = rest[-4:]
        g_ref[...], d_ref[...], m2_ref[...], v2_ref[...] = g, delta, m2, v2

    wspec = pl.BlockSpec((None, tm, c), lambda i: (layer, i, 0))
    prev = list(prev) if prev is not None else []
    return pl.pallas_call(
        body, grid=(rows // tm,),
        in_specs=[wspec] * 3 + [pl.BlockSpec((tm, c), lambda i: (i, 0)), pl.BlockSpec((parts.shape[0], tm, c), lambda i: (0, i, 0))]
        + [HBM_SPEC] * len(prev),
        out_specs=[wspec] * 4, out_shape=[jax.ShapeDtypeStruct(w.shape, F32)] * 4,
        input_output_aliases={5 + k: k for k in range(len(prev))}, name=name,
    )(w, m, v, own, parts, *prev)


def _adamw_call(w, m, v, gparts, name):
    def fn(wv, mv, vv, *gs):
        g = gs[0].astype(F32)
        for gp in gs[1:]:
            g = g + gp.astype(F32)
        delta, m2, v2 = _adamw(wv, g, mv, vv)
        return g, delta, m2, v2
    rows, c = w.shape
    return _ew(fn, [w, m, v] + list(gparts), [], [(c, F32)] * 4, tm=_pick_tm(rows, c), name=name)


def kernel(x, p, g_mix, w_in, b_fox_f, fox_q_gain, fox_k_gain, sc_conv_w, dn_conv_w, dn_a_log, dn_dt_bias,
           dn_norm_gain, w_branch, w_o, g_ffn, w_up, ffn_conv_w, w_down, g_ple, w_ple_gate, w_ple, loss_target,
           m_g_mix, m_w_in, m_b_fox_f, m_fox_q_gain, m_fox_k_gain, m_sc_conv_w, m_dn_conv_w, m_dn_a_log,
           m_dn_dt_bias, m_dn_norm_gain, m_w_branch, m_w_o, m_g_ffn, m_w_up, m_ffn_conv_w, m_w_down, m_g_ple,
           m_w_ple_gate, m_w_ple, v_g_mix, v_w_in, v_b_fox_f, v_fox_q_gain, v_fox_k_gain, v_sc_conv_w, v_dn_conv_w,
           v_dn_a_log, v_dn_dt_bias, v_dn_norm_gain, v_w_branch, v_w_o, v_g_ffn, v_w_up, v_ffn_conv_w, v_w_down,
           v_g_ple, v_w_ple_gate, v_w_ple):
    return _step(x, p, g_mix, w_in, b_fox_f, fox_q_gain, fox_k_gain, sc_conv_w, dn_conv_w, dn_a_log, dn_dt_bias,
                 dn_norm_gain, w_branch, w_o, g_ffn, w_up, ffn_conv_w, w_down, g_ple, w_ple_gate, w_ple, loss_target,
                 m_g_mix, m_w_in, m_b_fox_f, m_fox_q_gain, m_fox_k_gain, m_sc_conv_w, m_dn_conv_w, m_dn_a_log,
                 m_dn_dt_bias, m_dn_norm_gain, m_w_branch, m_w_o, m_g_ffn, m_w_up, m_ffn_conv_w, m_w_down, m_g_ple,
                 m_w_ple_gate, m_w_ple, v_g_mix, v_w_in, v_b_fox_f, v_fox_q_gain, v_fox_k_gain, v_sc_conv_w,
                 v_dn_conv_w, v_dn_a_log, v_dn_dt_bias, v_dn_norm_gain, v_w_branch, v_w_o, v_g_ffn, v_w_up,
                 v_ffn_conv_w, v_w_down, v_g_ple, v_w_ple_gate, v_w_ple)


def _step(*args):
    names = ['x', 'p'] + WEIGHTS + ['loss_target'] + ['m_' + n for n in WEIGHTS] + ['v_' + n for n in WEIGHTS]
    assert len(args) == len(names)
    a = dict(zip(names, args))
    x, target = a['x'][0], a['loss_target'][0]
    p = a['p'][:, 0]
    dev = 4 * lax.axis_index("x") + 2 * lax.axis_index("y") + lax.axis_index("c")

    LATE = [n for n in BIG if n != 'w_in']
    sps = [{n: a[n][layer][None, :] for n in SMALL} for layer in range(DEPTH)]
    shards = [{n: a[n][layer].astype(BF16) for n in BIG} for layer in range(DEPTH)]

    def by_device(full, n):
        by_dest = _by_dest(full, n)
        return lax.dynamic_index_in_dim(by_dest, dev, axis=0, keepdims=False), by_dest.astype(BF16)

    def blank(zone):
        return lax.dynamic_update_index_in_dim(zone, jnp.zeros(zone.shape[1:], zone.dtype), dev, 0)

    w_in_all, conv_all = _all_gather([shards[0]['w_in'], _flat_pack([a[n] for n in CONVW])], 'gather_w_in_l0')
    conv_by_dev = [_flat_unpack(conv_all[d], CONV_SHARD_SHAPES) for d in range(N_DEV)]
    conv_full = {n: jnp.concatenate([conv_by_dev[d][i] for d in range(N_DEV)], axis=2) for i, n in enumerate(CONVW)}
    late0_started, zero = _direct_start([shards[0][n] for n in LATE], False, w_in_all, 'weights_rest_l0_start', NEAR_PEERS)
    w1_started, zero = _direct_start([shards[1][n] for n in BIG], False, late0_started[2][0], 'weights_l1_start')
    sps[0]['g_mix'] = sps[0]['g_mix'] + zero

    def late_weights0(after):
        zones = _gather_finish(_direct_wait(late0_started, after, 'weights_rest_l0_wait'), [shards[0][n] for n in LATE],
                               'weights_rest_l0_finish')
        return {n: _full_from_gathered(z, n) for n, z in zip(LATE, zones)}

    w0 = {n: conv_full[n][0] for n in CONVW}
    w0['w_in'] = _w_in_assemble(w_in_all, 'w_in_assemble_l0')
    h, sv0 = _layer_fwd(x, p[0], w0, sps[0], 'l0', late_weights0)
    zones = [lax.dynamic_update_index_in_dim(z, shards[1][n], dev, 0)
             for n, z in zip(BIG, _direct_wait(w1_started, h, 'weights_l1_wait'))]
    w1 = {n: _full_from_gathered(z, n) for n, z in zip(BIG, zones) if n != 'w_in'}
    w1.update({n: conv_full[n][1] for n in CONVW})
    w1['w_in'] = _w_in_assemble(zones[list(BIG).index('w_in')], 'w_in_assemble_l1')
    h, sv1 = _layer_fwd(h, p[1], w1, sps[1], 'l1')
    dh, loss_part = _loss_bwd(h, target, 'loss')
    grads = [None] * DEPTH
    dh, grads[1] = _layer_bwd(dh, sv1, sps[1], 'l1')
    owns1, sends1 = {}, {}
    for n in BIG:
        owns1[n], sends1[n] = _w_in_split_by_device(grads[1][n], 'w_in_grad_split_l1') if n == 'w_in' else by_device(grads[1][n], n)
    g1_started, zero = _direct_start([sends1[n] for n in BIG], True, dh, 'grads_l1_start')
    early = {}

    def early_grads0(g, sp):
        owns, sends = zip(*[by_device(g[n], n) for n in LATE])
        early['started'], zero = _direct_start(list(sends), True, g['w_branch'], 'grads_rest_l0_start')
        early['owns'] = dict(zip(LATE, owns))
        return {**sp, 'dn_norm_gain': sp['dn_norm_gain'] + zero}

    dh, grads[0] = _layer_bwd(dh, sv0, {**sps[0], 'g_ple': sps[0]['g_ple'] + zero}, 'l0', early_grads0)
    grad_x = dh[None]

    own_in0, send_in0 = _w_in_split_by_device(grads[0]['w_in'], 'w_in_grad_split_l0')
    in0_started, zero = _direct_start([send_in0], True, dh, 'grads_w_in_l0_start')
    sent = in0_started[2][0]
    zones1 = dict(zip(BIG, _direct_wait(g1_started, sent, 'grads_l1_wait')))
    zones0 = dict(zip(LATE, _direct_wait(early['started'], sent, 'grads_rest_l0_wait')))

    def adamw(n, layer, own, zone, prev):
        rows = (-1, a[n].shape[-1])
        own = own.reshape(rows)
        view = lambda t: t.reshape((DEPTH,) + own.shape)
        return _adamw_layer(view(a[n]), view(a['m_' + n]), view(a['v_' + n]), own, blank(zone).reshape((N_DEV,) + own.shape),
                            layer, prev, f'adamw_{n}_l{layer}')

    out, done = {}, []
    for n in LATE:
        res = adamw(n, 0, early['owns'][n], zones0[n], adamw(n, 1, owns1[n], zones1[n], None))
        out[n] = [r.reshape(a[n].shape) for r in res]
        done.append(res[0][0, 0:1, 0:1])
    res = adamw('w_in', 1, owns1['w_in'], zones1['w_in'], None)
    done.append(res[0][1, 0:1, 0:1])
    (zone_in0,) = _direct_wait(in0_started, jnp.concatenate(done, axis=1), 'grads_w_in_l0_wait')
    out['w_in'] = [r.reshape(a['w_in'].shape) for r in adamw('w_in', 0, own_in0, zone_in0, res)]

    small_pack = _flat_pack([jnp.stack([grads[layer][n].reshape(-1) for layer in range(DEPTH)]) for n in SMALL] + [loss_part[0, 0]])
    conv_pack = _flat_pack([jnp.stack([grads[layer][n] for layer in range(DEPTH)]) for n in CONVW])
    small_all, conv_all = _all_gather([small_pack, conv_pack], 'gather_small_grads')
    res = _adamw_call(_flat_pack([a[n] for n in SMALL]), _flat_pack([a['m_' + n] for n in SMALL]),
                      _flat_pack([a['v_' + n] for n in SMALL]), [small_all[d] for d in range(N_DEV)], 'adamw_replicated')
    loss = res[0].reshape(-1)[SMALL_LOSS_AT]
    for k, r in enumerate(res):
        for n, val in zip(SMALL, _flat_unpack(r, SMALL_SHAPES)):
            out.setdefault(n, [None] * 4)[k] = val
    (conv_sum,) = _ew(lambda *gs: (functools.reduce(lambda s, t: s + t, gs),), [conv_all[d] for d in range(N_DEV)], [],
                      [(PACK_W, F32)], tm=conv_pack.shape[0], name='conv_grad_sum')
    conv_own = [lax.dynamic_slice_in_dim(g, dev * (g.shape[2] // N_DEV), g.shape[2] // N_DEV, axis=2)
                for g in _flat_unpack(conv_sum, CONV_FULL_SHAPES)]
    res = _adamw_call(_flat_pack([a[n] for n in CONVW]), _flat_pack([a['m_' + n] for n in CONVW]),
                      _flat_pack([a['v_' + n] for n in CONVW]), [_flat_pack(conv_own)], 'adamw_conv')
    for k, r in enumerate(res):
        for n, val in zip(CONVW, _flat_unpack(r, CONV_SHARD_SHAPES)):
            out.setdefault(n, [None] * 4)[k] = val

    outs = [loss, grad_x]
    for k in range(4):
        outs += [out[n][k] for n in WEIGHTS]
    return tuple(outs)
```

```python
import functools

import jax
import jax.numpy as jnp
from jax import lax
from jax.experimental import pallas as pl
from jax.experimental.pallas import tpu as pltpu

F32 = jnp.float32
BF16 = jnp.bfloat16
HI = lax.Precision.HIGHEST

D_MODEL = 1024
DEPTH = 2
N_DEV = 8
PLE_DIM = 256
BW = 512
FOX_HEADS, FOX_DH = 8, 64
DN_HEADS, DN_DH = 4, 128
DN_CHUNK = 64
D_FF = 2816
EPS = 1e-6
NEG = -1e30

ADAM_LR, ADAM_B1, ADAM_B2, ADAM_EPS, ADAM_WD, ADAM_STEP = 0.001, 0.9, 0.999, 1e-08, 0.01, 10

C_FQ, C_FK, C_FV = 0, 512, 1024
C_SB, C_SC, C_SV = 1536, 2048, 2560
C_DQ, C_DK, C_DV, C_DZ = 3072, 3584, 4096, 4608
C_GATE = 5120
C_SMALL = 8192
IN_P = 8320
IN_ORIG = 8208

WEIGHTS = ['g_mix', 'w_in', 'b_fox_f', 'fox_q_gain', 'fox_k_gain', 'sc_conv_w', 'dn_conv_w', 'dn_a_log',
           'dn_dt_bias', 'dn_norm_gain', 'w_branch', 'w_o', 'g_ffn', 'w_up', 'ffn_conv_w', 'w_down', 'g_ple',
           'w_ple_gate', 'w_ple']
BIG = {'w_in': 1, 'w_branch': 2, 'w_o': 0, 'w_up': 0, 'w_down': 0, 'w_ple_gate': 0, 'w_ple': 1}
HELD_TRANSPOSED = {'w_up': (0, 2, 1)}
CONVW = {'sc_conv_w': 1, 'dn_conv_w': 1, 'ffn_conv_w': 1}
SHARDED = {**BIG, **CONVW}
SMALL = [n for n in WEIGHTS if n not in SHARDED]
MESH = pl.DeviceIdType.MESH


def _sigmoid(x):
    return 0.5 * (jnp.tanh(0.5 * x) + 1.0)


def _silu(x):
    return x * _sigmoid(x)


def _log1pexp_negabs(z):
    return jnp.log(1.0 + jnp.exp(-jnp.abs(z)))


def _log_sigmoid(z):
    return jnp.minimum(z, 0.0) - _log1pexp_negabs(z)


def _softplus(z):
    return jnp.maximum(z, 0.0) + _log1pexp_negabs(z)


def _rms(x, g):
    return x * lax.rsqrt(jnp.mean(x * x, axis=-1, keepdims=True) + EPS) * g


def _l2(x):
    return x * lax.rsqrt(jnp.sum(x * x, axis=-1, keepdims=True) + EPS)


def _dot(a, b, dims, precision=None):
    return lax.dot_general(a, b, (dims, ((), ())), preferred_element_type=F32, precision=precision)


NN = ((1,), (0,))
NT = ((1,), (1,))
TN = ((0,), (0,))


def _shift_down(x, s):
    if s == 0:
        return x
    t = lax.broadcasted_iota(jnp.int32, x.shape, 0)
    return jnp.where(t >= s, pltpu.roll(x, s, 0), 0.0)


def _shift_up(x, s):
    if s == 0:
        return x
    n = x.shape[0]
    t = lax.broadcasted_iota(jnp.int32, x.shape, 0)
    return jnp.where(t < n - s, pltpu.roll(x, n - s, 0), 0.0)


def _conv(x, w):
    k = w.shape[0]
    y = w[k - 1:k] * x
    for j in range(k - 1):
        y = y + w[j:j + 1] * _shift_down(x, k - 1 - j)
    return y


def _conv_bwd(x, w, dy):
    k = w.shape[0]
    dx = w[k - 1:k] * dy
    dws = []
    for j in range(k - 1):
        dx = dx + w[j:j + 1] * _shift_up(dy, k - 1 - j)
        dws.append(jnp.sum(dy * _shift_down(x, k - 1 - j), axis=0, keepdims=True))
    dws.append(jnp.sum(dy * x, axis=0, keepdims=True))
    return dx, dws


MM_VMEM_BUDGET = 36 << 20
MM_STEP_BYTES = 1 << 20


def _mm_tiles(m, n, k, a_size, b_size, tile_size, cast_a):
    best = None
    for tm in [d for d in (2048, 1024, 512, 256, 128) if d <= m and m % d == 0] or [m]:
        for tn in [d for d in range(128, min(n, 2048) + 1, 128) if n % d == 0] or [n]:
            vmem = 2 * tm * k * a_size + (2 * tm * k if cast_a else 0) + 2 * k * tn * b_size + 2 * tm * tn * tile_size
            if vmem > MM_VMEM_BUDGET:
                continue
            steps = (m // tm) * (n // tn)
            cost = m * k * a_size + (m // tm) * k * n * b_size + m * n * tile_size + steps * MM_STEP_BYTES
            if best is None or cost < best[0]:
                best = (cost, tm, tn)
    assert best is not None, (m, n, k)
    return best[1], best[2]


def _mm(a, b, mode, outs, *, epi=None, epi_args=(), name):
    if mode == 'nn':
        (m, k), (k2, n) = a.shape, b.shape
    elif mode == 'nt':
        (m, k), (n, k2) = a.shape, b.shape
    else:
        (k, m), (k2, n) = a.shape, b.shape
    assert k == k2, (a.shape, b.shape, mode)
    tile_size = sum(jnp.dtype(dt).itemsize for dt in outs) + sum(e.dtype.itemsize for e in epi_args if e.shape[0] != 1)
    tm, tn = _mm_tiles(m, n, k, a.dtype.itemsize, b.dtype.itemsize, tile_size, a.dtype != BF16)
    dims = {'nn': NN, 'nt': NT, 'tn': TN}[mode]
    a_spec = pl.BlockSpec((k, tm), lambda i, j: (0, i)) if mode == 'tn' else pl.BlockSpec((tm, k), lambda i, j: (i, 0))
    b_spec = pl.BlockSpec((tn, k), lambda i, j: (j, 0)) if mode == 'nt' else pl.BlockSpec((k, tn), lambda i, j: (0, j))
    e_specs = [pl.BlockSpec((1, tn), lambda i, j: (0, j)) if e.shape[0] == 1 else pl.BlockSpec((tm, tn), lambda i, j: (i, j))
               for e in epi_args]
    ne, no = len(epi_args), len(outs)
    cast_a = a.dtype != BF16

    def body(a_ref, b_ref, *rest):
        if cast_a:
            a_sc = rest[-1]

            @pl.when(pl.program_id(1) == 0)
            def _():
                a_sc[...] = a_ref[...].astype(BF16)
            av = a_sc[...]
        else:
            av = a_ref[...]
        acc = _dot(av, b_ref[...].astype(BF16), dims)
        vals = epi(acc, *[e[...] for e in rest[:ne]]) if epi is not None else (acc,)
        for o_ref, v in zip(rest[ne:ne + no], vals):
            o_ref[...] = v.astype(o_ref.dtype)

    res = pl.pallas_call(
        body, grid=(m // tm, n // tn),
        in_specs=[a_spec, b_spec] + e_specs,
        out_specs=[pl.BlockSpec((tm, tn), lambda i, j: (i, j)) for _ in outs],
        out_shape=[jax.ShapeDtypeStruct((m, n), dt) for dt in outs],
        scratch_shapes=[pltpu.VMEM(a_spec.block_shape, BF16)] if cast_a else [],
        name=name,
    )(a, b, *epi_args)
    return res[0] if no == 1 else res


def _ew(fn, tiled, bcast, outs, reds=(), *, tm=256, name):
    secs = [(t, 0, t.shape[1]) if not isinstance(t, tuple) else t for t in tiled]
    m = secs[0][0].shape[0]
    tm = min(tm, m)
    assert m % tm == 0
    in_specs = []
    for arr, off, w in secs:
        assert off % w == 0
        in_specs.append(pl.BlockSpec((tm, w), functools.partial(lambda i, c: (i, c), c=off // w)))
    in_specs += [pl.BlockSpec(b.shape, lambda i: (0, 0)) for b in bcast]
    nin, no = len(in_specs), len(outs)

    def body(*refs):
        vals = fn(*[r[...] for r in refs[:nin]])
        for r, v in zip(refs[nin:nin + no], vals[:no]):
            r[...] = v.astype(r.dtype)
        i = pl.program_id(0)
        for r, v in zip(refs[nin + no:], vals[no:]):
            @pl.when(i == 0)
            def _():
                r[...] = v

            @pl.when(i > 0)
            def _():
                r[...] += v

    res = pl.pallas_call(
        body, grid=(m // tm,), in_specs=in_specs,
        out_specs=[pl.BlockSpec((tm, c), lambda i: (i, 0)) for c, _ in outs] + [pl.BlockSpec(s, lambda i: (0, 0)) for s in reds],
        out_shape=[jax.ShapeDtypeStruct((m, c), dt) for c, dt in outs] + [jax.ShapeDtypeStruct(s, F32) for s in reds],
        name=name,
    )(*[s[0] for s in secs], *bcast)
    return res


def _cb(fn, cols, params, outs, pouts, *, tc, nblk, name):
    t = cols[0][0].shape[0]
    in_specs = []
    for arr, off in cols:
        assert off % tc == 0
        in_specs.append(pl.BlockSpec((t, tc), functools.partial(lambda c, o: (0, o + c), o=off // tc)))
    for arr, off in params:
        in_specs.append(pl.BlockSpec((arr.shape[0], tc), functools.partial(lambda c, o: (0, o + c), o=off // tc)))
    nin, no = len(in_specs), len(outs)

    def body(*refs):
        vals = fn(*[r[...] for r in refs[:nin]])
        for r, v in zip(refs[nin:], vals):
            r[...] = v.astype(r.dtype)

    return pl.pallas_call(
        body, grid=(nblk,), in_specs=in_specs,
        out_specs=[pl.BlockSpec((t, tc), lambda c: (0, c)) for _ in outs] + [pl.BlockSpec((k, tc), lambda c: (0, c)) for k in pouts],
        out_shape=[jax.ShapeDtypeStruct((t, nblk * tc), dt) for dt in outs] + [jax.ShapeDtypeStruct((k, nblk * tc), F32) for k in pouts],
        name=name,
    )(*[c[0] for c in cols], *[p[0] for p in params])


CUM_BLK = 256


def _prep_point(s, bias, alog):
    lane = lax.broadcasted_iota(jnp.int32, s.shape, 1)
    z = s + bias
    return jnp.where(lane < 8, _log_sigmoid(z),
                     jnp.where(lane < 12, _sigmoid(s),
                               jnp.where(lane < 16, -jnp.exp(alog) * _softplus(z), 0.0)))


def _tri(n, upper):
    r = lax.broadcasted_iota(jnp.int32, (n, n), 0)
    c = lax.broadcasted_iota(jnp.int32, (n, n), 1)
    return (r <= c if upper else r >= c).astype(F32)


def _prep_fwd(proj, bias_row, alog_row, name):
    t = proj.shape[0]
    nb = t // CUM_BLK

    def body(s_ref, b_ref, a_ref, o_ref):
        pre = _prep_point(s_ref[...], b_ref[...], a_ref[...])
        lane = lax.broadcasted_iota(jnp.int32, (CUM_BLK, 128), 1)
        tri = _tri(CUM_BLK, False)
        carry = jnp.zeros((1, 128), F32)
        for blk in range(nb):
            xb = pre[blk * CUM_BLK:(blk + 1) * CUM_BLK]
            cb = _dot(tri, xb, NN, HI) + carry
            carry = cb[CUM_BLK - 1:CUM_BLK]
            o_ref[blk * CUM_BLK:(blk + 1) * CUM_BLK, :] = jnp.where(lane < 8, cb, xb)

    return pl.pallas_call(
        body, grid=(1,),
        in_specs=[pl.BlockSpec((t, 128), lambda i: (0, C_SMALL // 128)), pl.BlockSpec((1, 128), lambda i: (0, 0)),
                  pl.BlockSpec((1, 128), lambda i: (0, 0))],
        out_specs=pl.BlockSpec((t, 128), lambda i: (0, 0)),
        out_shape=jax.ShapeDtypeStruct((t, 128), F32), name=name,
    )(proj, bias_row, alog_row)


def _prep_bwd(proj, bias_row, alog_row, daux, name):
    t = proj.shape[0]
    nb = t // CUM_BLK

    def body(s_ref, b_ref, a_ref, d_ref, ds_ref, db_ref, da_ref, dpre_sc):
        lane = lax.broadcasted_iota(jnp.int32, (CUM_BLK, 128), 1)
        tri = _tri(CUM_BLK, True)
        carry = jnp.zeros((1, 128), F32)
        for blk in reversed(range(nb)):
            db = d_ref[blk * CUM_BLK:(blk + 1) * CUM_BLK, :]
            cb = _dot(tri, db, NN, HI) + carry
            carry = cb[0:1]
            dpre_sc[blk * CUM_BLK:(blk + 1) * CUM_BLK, :] = jnp.where(lane < 8, cb, db)
        _, vjp = jax.vjp(_prep_point, s_ref[...], b_ref[...], a_ref[...])
        ds, dbias, dalog = vjp(dpre_sc[...])
        ds_ref[...] = ds.astype(ds_ref.dtype)
        db_ref[...] = dbias
        da_ref[...] = dalog

    return pl.pallas_call(
        body, grid=(1,),
        in_specs=[pl.BlockSpec((t, 128), lambda i: (0, C_SMALL // 128)), pl.BlockSpec((1, 128), lambda i: (0, 0)),
                  pl.BlockSpec((1, 128), lambda i: (0, 0)), pl.BlockSpec((t, 128), lambda i: (0, 0))],
        out_specs=[pl.BlockSpec((t, 128), lambda i: (0, 0)), pl.BlockSpec((1, 128), lambda i: (0, 0)),
                   pl.BlockSpec((1, 128), lambda i: (0, 0))],
        out_shape=[jax.ShapeDtypeStruct((t, 128), BF16), jax.ShapeDtypeStruct((1, 128), F32), jax.ShapeDtypeStruct((1, 128), F32)],
        scratch_shapes=[pltpu.VMEM((t, 128), F32)], name=name,
    )(proj, bias_row, alog_row, daux)


ATT_TQ = 256
FOX_SCALE = FOX_DH ** -0.5


def _qnorm(q, g):
    return _rms(q, g) * FOX_SCALE


def _att_scores(qn_blk, kn, cfc_blk, cfr, qi, tq, kend):
    s = _dot(qn_blk.astype(BF16), kn[:kend].astype(BF16), NT) + cfc_blk - cfr[:, :kend]
    row = lax.broadcasted_iota(jnp.int32, (tq, kend), 0) + qi * tq
    col = lax.broadcasted_iota(jnp.int32, (tq, kend), 1)
    return s, row >= col


ATT_PAIR = 128 // FOX_DH


def _att_specs(t):
    pair = lambda off: pl.BlockSpec((t, 128), functools.partial(lambda i, o: (0, o + i), o=off // 128))
    gain = pl.BlockSpec((1, FOX_DH), lambda i: (0, 0))
    hd = lambda i: (i, 0, 0)
    col, row = pl.BlockSpec((ATT_PAIR, t, 1), hd), pl.BlockSpec((ATT_PAIR, 1, t), hd)
    return [pair(C_FQ), pair(C_FK), pair(C_FV), gain, gain, col, row], pl.BlockSpec((t, 128), lambda i: (0, i)), col, row


def _att_fwd(proj, qg, kg, cfc, cfr, name):
    t = proj.shape[0]
    tq = min(ATT_TQ, t)
    in_specs, pair_out, col, _ = _att_specs(t)

    def body(q_ref, k_ref, v_ref, qg_ref, kg_ref, cfc_ref, cfr_ref, o_ref, lse_ref):
        for e in range(ATT_PAIR):
            lanes = slice(e * FOX_DH, (e + 1) * FOX_DH)
            qn = _qnorm(q_ref[:, lanes], qg_ref[...])
            kn = _rms(k_ref[:, lanes], kg_ref[...])
            v = v_ref[:, lanes].astype(BF16)
            cfr = cfr_ref[e]
            for qi in range(t // tq):
                kend = (qi + 1) * tq
                rows = slice(qi * tq, kend)
                s, mask = _att_scores(qn[rows], kn, cfc_ref[e, rows, :], cfr, qi, tq, kend)
                s = jnp.where(mask, s, NEG)
                m = jnp.max(s, axis=1, keepdims=True)
                p = jnp.exp(s - m)
                l = jnp.sum(p, axis=1, keepdims=True)
                o_ref[rows, lanes] = _dot(p.astype(BF16), v[:kend], NN) / l
                lse_ref[e, rows, :] = m + jnp.log(l)

    return pl.pallas_call(
        body, grid=(FOX_HEADS // ATT_PAIR,), in_specs=in_specs, out_specs=[pair_out, col],
        out_shape=[jax.ShapeDtypeStruct((t, BW), F32), jax.ShapeDtypeStruct((FOX_HEADS, t, 1), F32)], name=name,
    )(proj, proj, proj, qg, kg, cfc, cfr)


def _att_bwd(proj, qg, kg, cfc, cfr, lse, o, do, name):
    t = proj.shape[0]
    tq = min(ATT_TQ, t)
    in_specs, pair_out, col, row = _att_specs(t)

    def body(q_ref, k_ref, v_ref, qg_ref, kg_ref, cfc_ref, cfr_ref, lse_ref, o_ref, do_ref,
             dq_ref, dk_ref, dv_ref, dcfc_ref, dcfr_ref, dqg_ref, dkg_ref, dqn_sc, dkn_sc, dv_sc, dcfr_sc):
        for e in range(ATT_PAIR):
            lanes = slice(e * FOX_DH, (e + 1) * FOX_DH)
            qn, vjp_q = jax.vjp(_qnorm, q_ref[:, lanes], qg_ref[...])
            kn, vjp_k = jax.vjp(_rms, k_ref[:, lanes], kg_ref[...])
            v = v_ref[:, lanes].astype(BF16)
            cfr = cfr_ref[e]
            do_e = do_ref[:, lanes]
            delta = jnp.sum(do_e * o_ref[:, lanes], axis=1, keepdims=True)
            dkn_sc[...] = jnp.zeros_like(dkn_sc)
            dv_sc[...] = jnp.zeros_like(dv_sc)
            dcfr_sc[...] = jnp.zeros_like(dcfr_sc)
            for qi in range(t // tq):
                kend = (qi + 1) * tq
                rows = slice(qi * tq, kend)
                s, mask = _att_scores(qn[rows], kn, cfc_ref[e, rows, :], cfr, qi, tq, kend)
                p = jnp.where(mask, jnp.exp(jnp.where(mask, s, NEG) - lse_ref[e, rows, :]), 0.0)
                do_b = do_e[rows].astype(BF16)
                dv_sc[0:kend, :] += _dot(p.astype(BF16), do_b, TN)
                dp = _dot(do_b, v[:kend], NT)
                ds = p * (dp - delta[rows])
                ds_b = ds.astype(BF16)
                dqn_sc[rows, :] = _dot(ds_b, kn[:kend].astype(BF16), NN)
                dkn_sc[0:kend, :] += _dot(ds_b, qn[rows].astype(BF16), TN)
                dcfc_ref[e, rows, :] = jnp.sum(ds, axis=1, keepdims=True)
                dcfr_sc[:, 0:kend] -= jnp.sum(ds, axis=0, keepdims=True)
            dq, dqg = vjp_q(dqn_sc[...])
            dk, dkg = vjp_k(dkn_sc[...])
            dq_ref[:, lanes] = dq.astype(dq_ref.dtype)
            dk_ref[:, lanes] = dk.astype(dk_ref.dtype)
            dv_ref[:, lanes] = dv_sc[...].astype(dv_ref.dtype)
            dcfr_ref[e] = dcfr_sc[...]
            dqg_ref[e] = dqg
            dkg_ref[e] = dkg

    gsp = pl.BlockSpec((ATT_PAIR, 1, FOX_DH), lambda i: (i, 0, 0))
    return pl.pallas_call(
        body, grid=(FOX_HEADS // ATT_PAIR,),
        in_specs=in_specs + [col, pair_out, pair_out],
        out_specs=[pair_out] * 3 + [col, row, gsp, gsp],
        out_shape=[jax.ShapeDtypeStruct((t, BW), BF16)] * 3
        + [jax.ShapeDtypeStruct((FOX_HEADS, t, 1), F32), jax.ShapeDtypeStruct((FOX_HEADS, 1, t), F32)]
        + [jax.ShapeDtypeStruct((FOX_HEADS, 1, FOX_DH), F32)] * 2,
        scratch_shapes=[pltpu.VMEM((t, FOX_DH), F32)] * 3 + [pltpu.VMEM((1, t), F32)], name=name,
    )(proj, proj, proj, qg, kg, cfc, cfr, lse, o, do)


DN_SCALE = DN_DH ** -0.5


DN_BATCH = 8


@functools.partial(jax.custom_vjp, nondiff_argnums=(2, 3))
def _mm3(a, b, dims, batch=False):
    return _mm3_passes(a, b, dims, batch)


def _mm3_fwd(a, b, dims, batch):
    return _mm3_passes(a, b, dims, batch), (a, b)


def _mm3_bwd(dims, batch, res, dc):
    a, b = res
    if dims == NN:
        return _mm3_passes(dc, b, NT, batch), _mm3_passes(a, dc, TN, batch)
    if dims == NT:
        return _mm3_passes(dc, b, NN, batch), _mm3_passes(dc, a, TN, batch)
    return _mm3_passes(b, dc, NT, batch), _mm3_passes(a, dc, NN, batch)


_mm3.defvjp(_mm3_fwd, _mm3_bwd)


def _mm3_passes(a, b, dims, batch):
    if batch:
        dn = (((dims[0][0] + 1,), (dims[1][0] + 1,)), ((0,), (0,)))
        dot = lambda p, q: lax.dot_general(p, q, dn, preferred_element_type=F32)
    else:
        dot = lambda p, q: _dot(p, q, dims)
    ah, bh = a.astype(BF16), b.astype(BF16)
    al, bl = (a - ah.astype(F32)).astype(BF16), (b - bh.astype(F32)).astype(BF16)
    return dot(ah, bh) + (dot(ah, bl) + dot(al, bh))


def _dn_local(qc, kc, vc, g, beta):
    nb, c, _ = qc.shape
    ii = lax.broadcasted_iota(jnp.int32, (c, c), 0)
    jj = lax.broadcasted_iota(jnp.int32, (c, c), 1)
    incl, strict = ii >= jj, ii > jj
    lower = jnp.broadcast_to(incl.astype(F32), (nb, c, c))
    eye = (ii == jj).astype(F32)
    mm = functools.partial(_mm3, batch=True)
    dm = mm(lower, jnp.where(strict, g, 0.0), NN)
    decay = jnp.where(incl, jnp.exp(jnp.where(incl, dm, 0.0)), 0.0)
    gcum = mm(lower, g * jnp.ones((1, 1, DN_DH), F32), NN)
    eg = jnp.exp(gcum)
    glast = gcum[:, c - 1:c]
    kb = kc * beta
    n1 = jnp.where(strict, mm(kb, kc, NT) * decay, 0.0)
    inv = eye - n1
    pw = n1
    for _ in range(5):
        pw = mm(pw, pw, NN)
        inv = inv + mm(pw, inv, NN)
    sol = mm(inv, jnp.concatenate([vc * beta, kb * eg], axis=2), NN)
    qk = jnp.where(incl, mm(qc, kc, NT) * decay, 0.0)
    return sol[:, :, :DN_DH], sol[:, :, DN_DH:], qk, qc * eg, kc * jnp.exp(glast - gcum), jnp.exp(glast)


def _dn_state(u, kcum, qk, qdec, kdec, egl, state):
    v_new = u - _mm3(kcum, state, NN)
    out = _mm3(qdec, state, NN) + _mm3(qk, v_new, NN)
    return out, state * egl + _mm3(kdec, v_new, TN)


def _dn_pre_q(c):
    return _l2(_silu(c)) * DN_SCALE


def _dn_pre_k(c):
    return _l2(_silu(c))


def _dn_post(o, z, ng):
    return _rms(o, ng) * _silu(z)


def _dn_specs(t):
    cblk = lambda o: pl.BlockSpec((t, DN_DH), functools.partial(lambda h, o: (0, o + h), o=o // DN_DH))
    wblk = lambda o: pl.BlockSpec((4, DN_DH), functools.partial(lambda h, o: (0, o + h), o=o // DN_DH))
    proj_specs = [cblk(C_DQ), cblk(C_DK), cblk(C_DV), cblk(C_DZ)]
    w_specs = [wblk(0), wblk(BW), wblk(2 * BW)]
    gb_spec = pl.BlockSpec((1, t, 2), lambda h: (h, 0, 0))
    return proj_specs, w_specs, gb_spec


def _chunk_rows(n, count=1):
    return pl.ds(pl.multiple_of(n * DN_CHUNK, DN_CHUNK), count * DN_CHUNK)


def _egl_rows(n, count=1):
    return pl.ds(pl.multiple_of(n * 8, 8), count * 8)


def _dn_local_inputs(n, qn_sc, kn_sc, vv_sc, gb_ref):
    r = _chunk_rows(n, DN_BATCH)
    split = lambda v: v.reshape(DN_BATCH, DN_CHUNK, v.shape[-1])
    gbv = split(gb_ref[0, r, :])
    return split(qn_sc[r, :]), split(kn_sc[r, :]), split(vv_sc[r, :]), gbv[:, :, 0:1], gbv[:, :, 1:2]


def _dn_local_phase(nc, qn_sc, kn_sc, vv_sc, gb_ref, loc):
    def step(i, carry):
        n = i * DN_BATCH
        vals = _dn_local(*_dn_local_inputs(n, qn_sc, kn_sc, vv_sc, gb_ref))
        for sc, val in zip(loc[:5], vals[:5]):
            sc[_chunk_rows(n, DN_BATCH), :] = val.reshape(DN_BATCH * DN_CHUNK, val.shape[-1])
        loc[5][_egl_rows(n, DN_BATCH), :] = jnp.broadcast_to(vals[5], (DN_BATCH, 8, DN_DH)).reshape(DN_BATCH * 8, DN_DH)
        return carry

    lax.fori_loop(0, nc // DN_BATCH, step, 0)


def _dn_loc_scratch(t, nc):
    big = pltpu.VMEM((t, DN_DH), F32)
    return [big, big, pltpu.VMEM((t, DN_CHUNK), F32), big, big, pltpu.VMEM((nc * 8, DN_DH), F32)]


def _dn_fwd(proj, conv_w, gb, ng, name):
    t = proj.shape[0]
    nc = t // DN_CHUNK
    assert nc % DN_BATCH == 0
    proj_specs, w_specs, gb_spec = _dn_specs(t)

    def body(q_ref, k_ref, v_ref, z_ref, wq_ref, wk_ref, wv_ref, gb_ref, ng_ref, y_ref, o_ref, st_ref, qn_sc, kn_sc, vv_sc, *loc):
        qn_sc[...] = _dn_pre_q(_conv(q_ref[...], wq_ref[...]))
        kn_sc[...] = _dn_pre_k(_conv(k_ref[...], wk_ref[...]))
        vv_sc[...] = _silu(_conv(v_ref[...], wv_ref[...]))
        _dn_local_phase(nc, qn_sc, kn_sc, vv_sc, gb_ref, loc)
        u_sc, kcum_sc, qk_sc, qdec_sc, kdec_sc, egl_sc = loc

        def chunk(n, state):
            r = _chunk_rows(n)
            egl = egl_sc[_egl_rows(n), :][0:1]
            out, new_state = _dn_state(u_sc[r, :], kcum_sc[r, :], qk_sc[r, :], qdec_sc[r, :], kdec_sc[r, :], egl, state)
            st_ref[0, n] = state
            o_ref[r, :] = out
            return new_state

        lax.fori_loop(0, nc, chunk, jnp.zeros((DN_DH, DN_DH), F32))
        y_ref[...] = _dn_post(o_ref[...], z_ref[...], ng_ref[...])

    hblk = pl.BlockSpec((t, DN_DH), lambda h: (0, h))
    return pl.pallas_call(
        body, grid=(DN_HEADS,),
        in_specs=proj_specs + w_specs + [gb_spec, pl.BlockSpec((1, DN_DH), lambda h: (0, 0))],
        out_specs=[hblk, hblk, pl.BlockSpec((1, nc, DN_DH, DN_DH), lambda h: (h, 0, 0, 0))],
        out_shape=[jax.ShapeDtypeStruct((t, BW), F32), jax.ShapeDtypeStruct((t, BW), F32),
                   jax.ShapeDtypeStruct((DN_HEADS, nc, DN_DH, DN_DH), F32)],
        scratch_shapes=[pltpu.VMEM((t, DN_DH), F32)] * 3 + _dn_loc_scratch(t, nc), name=name,
    )(proj, proj, proj, proj, conv_w, conv_w, conv_w, gb, ng)


def _dn_bwd(proj, conv_w, gb, ng, o, states, dy, name):
    t = proj.shape[0]
    nc = t // DN_CHUNK
    proj_specs, w_specs, gb_spec = _dn_specs(t)
    nloc = 6

    def body(q_ref, k_ref, v_ref, z_ref, wq_ref, wk_ref, wv_ref, gb_ref, ng_ref, o_ref, st_ref, dy_ref,
             dq_ref, dk_ref, dv_ref, dz_ref, dwq_ref, dwk_ref, dwv_ref, dgb_ref, dng_ref,
             qn_sc, kn_sc, vv_sc, do_sc, *rest):
        loc, dloc = rest[:nloc], rest[nloc:]
        qn_sc[...] = _dn_pre_q(_conv(q_ref[...], wq_ref[...]))
        kn_sc[...] = _dn_pre_k(_conv(k_ref[...], wk_ref[...]))
        vv_sc[...] = _silu(_conv(v_ref[...], wv_ref[...]))
        _, vjp_y = jax.vjp(_dn_post, o_ref[...], z_ref[...], ng_ref[...])
        do, dz, dng = vjp_y(dy_ref[...])
        do_sc[...] = do
        dz_ref[...] = dz.astype(dz_ref.dtype)
        dng_ref[0] = dng
        _dn_local_phase(nc, qn_sc, kn_sc, vv_sc, gb_ref, loc)
        u_sc, kcum_sc, qk_sc, qdec_sc, kdec_sc, egl_sc = loc

        def state_bwd(i, dstate):
            n = nc - 1 - i
            r = _chunk_rows(n)
            r8 = _egl_rows(n)
            _, vjp = jax.vjp(_dn_state, u_sc[r, :], kcum_sc[r, :], qk_sc[r, :], qdec_sc[r, :], kdec_sc[r, :],
                             egl_sc[r8, :][0:1], st_ref[0, n])
            du, dkcum, dqk, dqdec, dkdec, degl, dprev = vjp((do_sc[r, :], dstate))
            for d_sc, val in zip(dloc[:5], (du, dkcum, dqk, dqdec, dkdec)):
                d_sc[r, :] = val
            dloc[5][r8, :] = jnp.broadcast_to(degl, (8, DN_DH))
            return dprev

        lax.fori_loop(0, nc, state_bwd, jnp.zeros((DN_DH, DN_DH), F32))

        def local_bwd(i, carry):
            n = i * DN_BATCH
            r = _chunk_rows(n, DN_BATCH)
            _, vjp = jax.vjp(_dn_local, *_dn_local_inputs(n, qn_sc, kn_sc, vv_sc, gb_ref))
            cts = tuple(d_sc[r, :].reshape(DN_BATCH, DN_CHUNK, d_sc.shape[-1]) for d_sc in dloc[:5])
            cts += (dloc[5][_egl_rows(n, DN_BATCH), :].reshape(DN_BATCH, 8, DN_DH)[:, 0:1],)
            dqc, dkc, dvc, dg, dbeta = vjp(cts)
            for d_sc, val in zip((dloc[0], dloc[1], dloc[3]), (dqc, dkc, dvc)):
                d_sc[r, :] = val.reshape(DN_BATCH * DN_CHUNK, DN_DH)
            dgb_ref[0, r, :] = jnp.concatenate([dg, dbeta], axis=2).reshape(DN_BATCH * DN_CHUNK, 2)
            return carry

        lax.fori_loop(0, nc // DN_BATCH, local_bwd, 0)
        for x_ref, w_ref, pre, d_sc, dx_ref, dw_ref in ((q_ref, wq_ref, _dn_pre_q, dloc[0], dq_ref, dwq_ref),
                                                       (k_ref, wk_ref, _dn_pre_k, dloc[1], dk_ref, dwk_ref),
                                                       (v_ref, wv_ref, _silu, dloc[3], dv_ref, dwv_ref)):
            _, vjp = jax.vjp(pre, _conv(x_ref[...], w_ref[...]))
            (dc,) = vjp(d_sc[...])
            dx, dws = _conv_bwd(x_ref[...], w_ref[...], dc)
            dx_ref[...] = dx.astype(dx_ref.dtype)
            for j, dw in enumerate(dws):
                dw_ref[j:j + 1, :] = dw

    hblk = pl.BlockSpec((t, DN_DH), lambda h: (0, h))
    wout = pl.BlockSpec((4, DN_DH), lambda h: (0, h))
    return pl.pallas_call(
        body, grid=(DN_HEADS,),
        in_specs=proj_specs + w_specs + [gb_spec, pl.BlockSpec((1, DN_DH), lambda h: (0, 0)), hblk,
                                         pl.BlockSpec((1, nc, DN_DH, DN_DH), lambda h: (h, 0, 0, 0)), hblk],
        out_specs=[hblk] * 4 + [wout] * 3 + [gb_spec, pl.BlockSpec((1, 1, DN_DH), lambda h: (h, 0, 0))],
        out_shape=[jax.ShapeDtypeStruct((t, BW), BF16)] * 4 + [jax.ShapeDtypeStruct((4, BW), F32)] * 3
        + [jax.ShapeDtypeStruct((DN_HEADS, t, 2), F32), jax.ShapeDtypeStruct((DN_HEADS, 1, DN_DH), F32)],
        scratch_shapes=[pltpu.VMEM((t, DN_DH), F32)] * 4 + _dn_loc_scratch(t, nc) * 2, name=name,
    )(proj, proj, proj, proj, conv_w, conv_w, conv_w, gb, ng, o, states, dy)


MERGE_TM, MERGE_TN = 512, 512


def _merge_specs(t):
    tm, tn = min(MERGE_TM, t), MERGE_TN
    y_spec = pl.BlockSpec((tm, BW), lambda i, j: (i, 0))
    w_spec = pl.BlockSpec((3, BW, tn), lambda i, j: (0, 0, j))
    gate_specs = [pl.BlockSpec((tm, tn), functools.partial(lambda i, j, o: (i, o + j), o=(C_GATE + n * D_MODEL) // tn))
                  for n in range(3)]
    return tm, tn, [y_spec] * 3 + [w_spec] + gate_specs


def _merge_fwd(ys, wb, proj, name):
    t = proj.shape[0]
    tm, tn, in_specs = _merge_specs(t)

    def body(y0, y1, y2, w_ref, g0, g1, g2, o_ref):
        acc = jnp.zeros((tm, tn), F32)
        for n, (y, g) in enumerate(((y0, g0), (y1, g1), (y2, g2))):
            acc = acc + _dot(y[...].astype(BF16), w_ref[n], NN) * _sigmoid(g[...])
        o_ref[...] = acc.astype(o_ref.dtype)

    return pl.pallas_call(
        body, grid=(t // tm, D_MODEL // tn), in_specs=in_specs,
        out_specs=pl.BlockSpec((tm, tn), lambda i, j: (i, j)),
        out_shape=jax.ShapeDtypeStruct((t, D_MODEL), BF16), name=name,
    )(*ys, wb, proj, proj, proj)


def _merge_bwd(ys, wb, proj, dmerged, name):
    t = proj.shape[0]
    tm, tn, in_specs = _merge_specs(t)

    def body(y0, y1, y2, w_ref, g0, g1, g2, dm_ref, dg_ref, dt_ref):
        dm = dm_ref[...]
        for n, (y, g) in enumerate(((y0, g0), (y1, g1), (y2, g2))):
            tn_ = _dot(y[...].astype(BF16), w_ref[n], NN)
            sg = _sigmoid(g[...])
            dg_ref[n] = (dm * tn_ * sg * (1.0 - sg)).astype(dg_ref.dtype)
            dt_ref[n] = (dm * sg).astype(dt_ref.dtype)

    o3 = pl.BlockSpec((3, tm, tn), lambda i, j: (0, i, j))
    return pl.pallas_call(
        body, grid=(t // tm, D_MODEL // tn), in_specs=in_specs + [pl.BlockSpec((tm, tn), lambda i, j: (i, j))],
        out_specs=[o3, o3], out_shape=[jax.ShapeDtypeStruct((3, t, D_MODEL), BF16)] * 2, name=name,
    )(*ys, wb, proj, proj, proj, dmerged)


def _prep_rows(sp):
    z4, z112 = jnp.zeros((1, 4), F32), jnp.zeros((1, 112), F32)
    bias_row = jnp.concatenate([sp['b_fox_f'], z4, sp['dn_dt_bias'], z112], axis=1)
    alog_row = jnp.concatenate([jnp.zeros((1, 12), F32), sp['dn_a_log'], z112], axis=1)
    return bias_row, alog_row


def _sc_fwd_tile(sb, sc, sv, w):
    return (sb * _conv(sc * sv, w),)


def _ffn_act(ug, uv):
    return _silu(ug) * uv


W_IN_COLS = [(C_FQ, 0, 1536), (C_SB, 1544, 1536), (C_DQ, 3080, 1536), (C_DZ, 4624, 512), (C_GATE, 5136, 3072),
             (C_SMALL, 1536, 8), (C_SMALL + 8, 4616, 8)]
W_IN_SHARD = IN_ORIG // N_DEV


def _w_in_segments():
    out = []
    for d, o, w in W_IN_COLS:
        end = o + w
        while o < end:
            k = o // W_IN_SHARD
            n = min(end, (k + 1) * W_IN_SHARD) - o
            out.append((d, k, o - k * W_IN_SHARD, n))
            o, d = o + n, d + n
    return out


def _w_in_assemble(g, name):
    tm = 256

    def body(g_ref, o_ref):
        for d, k, s, n in _w_in_segments():
            o_ref[:, d:d + n] = g_ref[k, :, s:s + n]
        o_ref[:, IN_ORIG:IN_P] = jnp.zeros((tm, IN_P - IN_ORIG), o_ref.dtype)

    return pl.pallas_call(
        body, grid=(D_MODEL // tm,),
        in_specs=[pl.BlockSpec((N_DEV, tm, W_IN_SHARD), lambda i: (0, i, 0))],
        out_specs=pl.BlockSpec((tm, IN_P), lambda i: (i, 0)),
        out_shape=jax.ShapeDtypeStruct((D_MODEL, IN_P), g.dtype), name=name,
    )(g)


W_IN_SPLIT_TM = 128


def _w_in_split_by_device(g, name):
    tm = W_IN_SPLIT_TM

    def body(g_ref, own_ref, send_ref):
        me = 4 * lax.axis_index("x") + 2 * lax.axis_index("y") + lax.axis_index("c")
        for d, k, s, n in _w_in_segments():
            val = g_ref[:, d:d + n]
            send_ref[k, :, s:s + n] = val.astype(send_ref.dtype)

            @pl.when(me == k)
            def _():
                own_ref[:, s:s + n] = val

    return pl.pallas_call(
        body, grid=(D_MODEL // tm,), in_specs=[pl.BlockSpec((tm, IN_P), lambda i: (i, 0))],
        out_specs=[pl.BlockSpec((tm, W_IN_SHARD), lambda i: (i, 0)), pl.BlockSpec((N_DEV, tm, W_IN_SHARD), lambda i: (0, i, 0))],
        out_shape=[jax.ShapeDtypeStruct((D_MODEL, W_IN_SHARD), F32), jax.ShapeDtypeStruct((N_DEV, D_MODEL, W_IN_SHARD), BF16)],
        name=name,
    )(g)


def _layer_fwd(x, p_i, w, sp, tag, late_weights=None):
    t = x.shape[0]
    sv = {'x': x}
    (hn,) = _ew(lambda a, g: (_rms(a, g),), [x], [sp['g_mix']], [(D_MODEL, BF16)], name=f'rms_mix_{tag}')
    proj = _mm(hn, w['w_in'], 'nn', [F32], name=f'in_proj_{tag}')
    bias_row, alog_row = _prep_rows(sp)
    aux = _prep_fwd(proj, bias_row, alog_row, f'prep_{tag}')
    cf = aux[:, :FOX_HEADS].T
    cfc, cfr = cf[:, :, None], cf[:, None, :]
    y_fox, lse = _att_fwd(proj, sp['fox_q_gain'], sp['fox_k_gain'], cfc, cfr, f'fox_fwd_{tag}')
    (y_sc,) = _cb(_sc_fwd_tile, [(proj, C_SB), (proj, C_SC), (proj, C_SV)], [(w['sc_conv_w'], 0)], [F32], [],
                  tc=256, nblk=2, name=f'sc_fwd_{tag}')
    gb = jnp.stack([aux[:, 12:16].T, aux[:, 8:12].T], axis=-1)
    y_dn, o_dn, states = _dn_fwd(proj, w['dn_conv_w'], gb, sp['dn_norm_gain'], f'dn_fwd_{tag}')
    ys = (y_fox, y_sc, y_dn)
    if late_weights is not None:
        w = {**w, **late_weights(y_dn)}
    merged = _merge_fwd(ys, w['w_branch'], proj, f'merge_fwd_{tag}')
    x1 = _mm(merged, w['w_o'], 'nn', [F32], epi=lambda acc, r: (acc + r,), epi_args=(x,), name=f'o_proj_{tag}')
    (hf,) = _ew(lambda a, g: (_rms(a, g),), [x1], [sp['g_ffn']], [(D_MODEL, BF16)], name=f'rms_ffn_{tag}')
    up = _mm(hf, w['w_up'], 'nt', [F32], name=f'up_proj_{tag}')
    (act,) = _cb(lambda ug, uv, wg, wv: (_ffn_act(_conv(ug, wg), _conv(uv, wv)),), [(up, 0), (up, D_FF)],
                 [(w['ffn_conv_w'], 0), (w['ffn_conv_w'], D_FF)], [BF16], [], tc=256, nblk=D_FF // 256, name=f'ffn_act_{tag}')
    x2 = _mm(act, w['w_down'], 'nn', [F32], epi=lambda acc, r: (acc + r,), epi_args=(x1,), name=f'down_proj_{tag}')
    (hp,) = _ew(lambda a, g: (_rms(a, g),), [x2], [sp['g_ple']], [(D_MODEL, BF16)], name=f'rms_ple_{tag}')
    gp = _mm(hp, w['w_ple_gate'], 'nn', [F32], name=f'ple_gate_{tag}')
    x3 = _mm(p_i, w['w_ple'], 'nn', [F32], epi=lambda acc, g, r: (r + _sigmoid(g) * acc,), epi_args=(gp, x2), name=f'ple_{tag}')
    sv.update(hn=hn, proj=proj, aux=aux, cfc=cfc, cfr=cfr, lse=lse, ys=ys, gb=gb, o_dn=o_dn,
              states=states, merged=merged, x1=x1, hf=hf, up=up, act=act, x2=x2, hp=hp, gp=gp, p=p_i,
              bias_row=bias_row, alog_row=alog_row, w=w)
    return x3, sv


def _rms_bwd(x, g, dh, dres, name):
    def fn(xv, dhv, dr, gv):
        _, vjp = jax.vjp(_rms, xv, gv)
        dx, dg = vjp(dhv)
        return dr + dx, dg
    return _ew(fn, [x, dh, dres], [g], [(D_MODEL, F32)], [(1, D_MODEL)], name=name)


def _layer_bwd(dx3, sv, sp, tag, early_grads=None):
    t = dx3.shape[0]
    w = sv['w']
    g = {}
    def ple_epi(acc, gpv, d):
        s = _sigmoid(gpv)
        return d * acc * s * (1.0 - s), d * s
    dgp, de = _mm(sv['p'], w['w_ple'], 'nn', [BF16, BF16], epi=ple_epi, epi_args=(sv['gp'], dx3), name=f'ple_bwd_{tag}')
    g['w_ple'] = _mm(sv['p'], de, 'tn', [F32], name=f'd_w_ple_{tag}')
    g['w_ple_gate'] = _mm(sv['hp'], dgp, 'tn', [F32], name=f'd_w_ple_gate_{tag}')
    dhp = _mm(dgp, w['w_ple_gate'], 'nt', [F32], name=f'd_hp_{tag}')
    dx2, g['g_ple'] = _rms_bwd(sv['x2'], sp['g_ple'], dhp, dx3, f'rms_ple_bwd_{tag}')
    dact = _mm(dx2, w['w_down'], 'nt', [F32], name=f'd_act_{tag}')
    g['w_down'] = _mm(sv['act'], dx2, 'tn', [F32], name=f'd_w_down_{tag}')

    def ffn_bwd_tile(ug, uv, da, wg, wv):
        cg, cv = _conv(ug, wg), _conv(uv, wv)
        _, vjp = jax.vjp(_ffn_act, cg, cv)
        dcg, dcv = vjp(da)
        dug, dwg = _conv_bwd(ug, wg, dcg)
        duv, dwv = _conv_bwd(uv, wv, dcv)
        return dug, duv, jnp.concatenate(dwg, axis=0), jnp.concatenate(dwv, axis=0)
    dupg, dupv, dwg, dwv = _cb(ffn_bwd_tile, [(sv['up'], 0), (sv['up'], D_FF), (dact, 0)],
                               [(w['ffn_conv_w'], 0), (w['ffn_conv_w'], D_FF)], [BF16, BF16], [3, 3], tc=256,
                               nblk=D_FF // 256, name=f'ffn_act_bwd_{tag}')
    dup = jnp.concatenate([dupg, dupv], axis=1)
    g['ffn_conv_w'] = jnp.concatenate([dwg, dwv], axis=1)
    g['w_up'] = _mm(dup, sv['hf'], 'tn', [F32], name=f'd_w_up_{tag}')
    dhf = _mm(dup, w['w_up'], 'nn', [F32], name=f'd_hf_{tag}')
    dx1, g['g_ffn'] = _rms_bwd(sv['x1'], sp['g_ffn'], dhf, dx2, f'rms_ffn_bwd_{tag}')
    dmerged = _mm(dx1, w['w_o'], 'nt', [F32], name=f'd_merged_{tag}')
    g['w_o'] = _mm(sv['merged'], dx1, 'tn', [F32], name=f'd_w_o_{tag}')
    dgate, dtn = _merge_bwd(sv['ys'], w['w_branch'], sv['proj'], dmerged, f'merge_bwd_{tag}')
    dys, dwb = [], []
    for n in range(3):
        dys.append(_mm(dtn[n], w['w_branch'][n], 'nt', [F32], name=f'd_y{n}_{tag}'))
        dwb.append(_mm(sv['ys'][n], dtn[n], 'tn', [F32], name=f'd_w_branch{n}_{tag}'))
    g['w_branch'] = jnp.stack(dwb)
    if early_grads is not None:
        sp = early_grads(g, sp)
    ddq, ddk, ddv, ddz, dwq, dwk, dwv_, dgb, dng = _dn_bwd(sv['proj'], w['dn_conv_w'], sv['gb'], sp['dn_norm_gain'],
                                                           sv['o_dn'], sv['states'], dys[2], f'dn_bwd_{tag}')
    g['dn_conv_w'] = jnp.concatenate([dwq, dwk, dwv_], axis=1)
    g['dn_norm_gain'] = jnp.sum(dng, axis=0)
    def sc_bwd_tile(sb, sc, svv, dy, wv):
        u = sc * svv
        dsb = dy * _conv(u, wv)
        du, dws = _conv_bwd(u, wv, dy * sb)
        return dsb, du * svv, du * sc, jnp.concatenate(dws, axis=0)
    dsb, dsc, dsv, g['sc_conv_w'] = _cb(sc_bwd_tile, [(sv['proj'], C_SB), (sv['proj'], C_SC), (sv['proj'], C_SV), (dys[1], 0)],
                                        [(w['sc_conv_w'], 0)], [BF16, BF16, BF16], [3], tc=256, nblk=2, name=f'sc_bwd_{tag}')
    dfq, dfk, dfv, dcfc, dcfr, dqg, dkg = _att_bwd(sv['proj'], sp['fox_q_gain'], sp['fox_k_gain'], sv['cfc'], sv['cfr'],
                                                   sv['lse'], sv['ys'][0], dys[0], f'fox_bwd_{tag}')
    g['fox_q_gain'] = jnp.sum(dqg, axis=0)
    g['fox_k_gain'] = jnp.sum(dkg, axis=0)
    dcf = (dcfc[:, :, 0] + dcfr[:, 0, :]).T
    daux = jnp.concatenate([dcf, dgb[:, :, 1].T, dgb[:, :, 0].T, jnp.zeros((t, 112), F32)], axis=1)
    dsmall, dbias, dalog = _prep_bwd(sv['proj'], sv['bias_row'], sv['alog_row'], daux, f'prep_bwd_{tag}')
    g['b_fox_f'] = dbias[:, 0:8]
    g['dn_dt_bias'] = dbias[:, 12:16]
    g['dn_a_log'] = dalog[:, 12:16]
    dproj = jnp.concatenate([dfq, dfk, dfv, dsb, dsc, dsv, ddq, ddk, ddv, ddz, dgate[0], dgate[1], dgate[2], dsmall], axis=1)
    g['w_in'] = _mm(sv['hn'], dproj, 'tn', [F32], name=f'd_w_in_{tag}')
    dhn = _mm(dproj, w['w_in'], 'nt', [F32], name=f'd_hn_{tag}')
    dx, g['g_mix'] = _rms_bwd(sv['x'], sp['g_mix'], dhn, dx1, f'rms_mix_bwd_{tag}')
    return dx, g


def _loss_bwd(y, target, name):
    inv = 1.0 / y.shape[1]

    def fn(yv, tv):
        err = yv - tv
        return err * inv, jnp.zeros((8, 128), F32) + 0.5 * inv * jnp.sum(err * err)
    return _ew(fn, [y, target], [], [(y.shape[1], F32)], [(8, 128)], name=name)


PACK_W = 1024
FULL_SHAPE = {'w_in': (D_MODEL, IN_ORIG), 'w_branch': (3, BW, D_MODEL), 'w_o': (D_MODEL, D_MODEL), 'w_up': (2 * D_FF, D_MODEL),
              'w_down': (D_FF, D_MODEL), 'w_ple_gate': (D_MODEL, D_MODEL), 'w_ple': (PLE_DIM, D_MODEL),
              'sc_conv_w': (3, BW), 'dn_conv_w': (4, 3 * BW), 'ffn_conv_w': (3, 2 * D_FF)}
SMALL_SHAPE = {'g_mix': D_MODEL, 'b_fox_f': FOX_HEADS, 'fox_q_gain': FOX_DH, 'fox_k_gain': FOX_DH, 'dn_a_log': DN_HEADS,
               'dn_dt_bias': DN_HEADS, 'dn_norm_gain': DN_DH, 'g_ffn': D_MODEL, 'g_ple': D_MODEL}


def _shard_shape(name):
    s = list(FULL_SHAPE[name])
    s[SHARDED[name]] //= N_DEV
    return tuple(s)


def _full_from_gathered(g, name):
    sh, ax = _shard_shape(name), SHARDED[name]
    blocks = jnp.moveaxis(g, 0, ax)
    return blocks.reshape(sh[:ax] + (N_DEV * sh[ax],) + sh[ax + 1:])


def _by_dest(full, name):
    sh, ax = _shard_shape(name), SHARDED[name]
    return jnp.moveaxis(full.reshape(sh[:ax] + (N_DEV, sh[ax]) + sh[ax + 1:]), ax, 0)


def _flat_pack(arrs):
    flat = jnp.concatenate([a.reshape(-1).astype(F32) for a in arrs])
    rows = -(-flat.shape[0] // (8 * PACK_W)) * 8
    return jnp.pad(flat, (0, rows * PACK_W - flat.shape[0])).reshape(rows, PACK_W)


def _flat_unpack(pack, shapes):
    flat, out, off = pack.reshape(-1), [], 0
    for s in shapes:
        n = 1
        for d in s:
            n *= d
        out.append(flat[off:off + n].reshape(s))
        off += n
    return out


SMALL_SHAPES = [(DEPTH, SMALL_SHAPE[n]) for n in SMALL]
SMALL_LOSS_AT = sum(DEPTH * SMALL_SHAPE[n] for n in SMALL)
CONV_SHARD_SHAPES = [(DEPTH,) + _shard_shape(n) for n in CONVW]
CONV_FULL_SHAPES = [(DEPTH,) + FULL_SHAPE[n] for n in CONVW]


def _pick_tm(m, width):
    best = None
    for tm in range(16, m + 1, 16):
        if m % tm == 0 and tm * width * 4 <= (1 << 20):
            best = tm
    return best if best is not None else m


HBM_SPEC = pl.BlockSpec(memory_space=pltpu.HBM)


def _place():
    x, y, c = lax.axis_index("x"), lax.axis_index("y"), lax.axis_index("c")
    return x, y, c, [(1 - x, y), (x, 1 - y), (1 - x, 1 - y)]


def _all_gather(arrs, name):
    n = len(arrs)

    def body(*refs):
        ins, outs = refs[:n], refs[n:2 * n]
        send_sems, recv_sems, local_sems = refs[2 * n:]
        x, y, c, chips = _place()
        me, sibling = (x, y, c), (x, y, 1 - c)

        def block(a, p):
            return outs[a].at[4 * p[0] + 2 * p[1] + p[2]]

        def copy(a, k, blk, to, src=None):
            return pltpu.make_async_remote_copy(src_ref=block(a, blk) if src is None else src, dst_ref=block(a, blk),
                                                send_sem=send_sems.at[a, k], recv_sem=recv_sems.at[a, k],
                                                device_id=to, device_id_type=MESH)

        mine = [pltpu.make_async_copy(ins[a], block(a, me), local_sems.at[a]) for a in range(n)]
        first, passed = [], []
        for a in range(n):
            mine[a].start()
            first.append(copy(a, 0, me, sibling, src=ins[a]))
            first += [copy(a, 1 + j, me, (*chip, c), src=ins[a]) for j, chip in enumerate(chips)]
        for cp in first:
            cp.start()
        for j, chip in enumerate(chips):
            for a in range(n):
                copy(a, 1 + j, (*chip, c), me).wait_recv()
                fwd = copy(a, 4 + j, (*chip, c), sibling)
                fwd.start()
                passed.append(fwd)
        for a in range(n):
            copy(a, 0, sibling, me).wait_recv()
            for j, chip in enumerate(chips):
                copy(a, 4 + j, (*chip, 1 - c), me).wait_recv()
        for cp in first + passed:
            cp.wait_send()
        for a in range(n):
            mine[a].wait()

    return pl.pallas_call(
        body, in_specs=[HBM_SPEC] * n, out_specs=[HBM_SPEC] * n,
        out_shape=[jax.ShapeDtypeStruct((N_DEV,) + a.shape, a.dtype) for a in arrs],
        scratch_shapes=[pltpu.SemaphoreType.DMA((n, 7)), pltpu.SemaphoreType.DMA((n, 7)), pltpu.SemaphoreType.DMA((n,))],
        name=name,
    )(*arrs)


SEM_SPEC = pl.BlockSpec(memory_space=pltpu.SEMAPHORE)
DATAFLOW = pltpu.SideEffectType.DATAFLOW_SIDE_EFFECTING
ALL_PEERS = (1, 2, 3, 4, 5, 6, 7)
NEAR_PEERS = (1, 4, 2, 6)


def _direct_copies(src_refs, zone_refs, send_sems, recv_sems, per_dest, peers):
    x, y, c, _ = _place()
    me = 4 * x + 2 * y + c
    cps = []
    for a, (src, zone) in enumerate(zip(src_refs, zone_refs)):
        for r, bits in enumerate(peers):
            px = 1 - x if bits & 4 else x
            py = 1 - y if bits & 2 else y
            pc = 1 - c if bits & 1 else c
            cps.append(pltpu.make_async_remote_copy(
                src_ref=src.at[4 * px + 2 * py + pc] if per_dest else src, dst_ref=zone.at[me],
                send_sem=send_sems.at[a * len(peers) + r], recv_sem=recv_sems.at[a * len(peers) + r],
                device_id=(px, py, pc), device_id_type=MESH))
    return cps


def _direct_start(srcs, per_dest, after, name, peers=ALL_PEERS):
    n = len(srcs)
    zones = [lax.empty((N_DEV,) + (s.shape[1:] if per_dest else s.shape), s.dtype) for s in srcs]

    def body(*refs):
        src_refs, zone_refs = refs[:n], refs[n:2 * n]
        send_sems, recv_sems = refs[2 * n + 1:2 * n + 3]
        for cp in _direct_copies(src_refs, zone_refs, send_sems, recv_sems, per_dest, peers):
            cp.start()
        refs[-1][...] = jnp.zeros_like(refs[-1])

    sems = pltpu.SemaphoreType.DMA((n * len(peers),))
    res = pl.pallas_call(
        body, name=name, in_specs=[HBM_SPEC] * (2 * n + 1),
        out_shape=[sems, sems] + [pltpu.HBM(s.shape, s.dtype) for s in srcs] + [pltpu.HBM(z.shape, z.dtype) for z in zones]
        + [jax.ShapeDtypeStruct((8, 128), F32)],
        out_specs=[SEM_SPEC, SEM_SPEC] + [HBM_SPEC] * (2 * n) + [pl.BlockSpec(memory_space=pltpu.VMEM)],
        input_output_aliases={i: 2 + i for i in range(2 * n)},
        compiler_params=pltpu.CompilerParams(has_side_effects=DATAFLOW),
    )(*[pltpu.with_memory_space_constraint(s, pltpu.HBM) for s in srcs],
      *[pltpu.with_memory_space_constraint(z, pltpu.HBM) for z in zones], after)
    return (res[0], res[1], list(res[2:2 + n]), list(res[2 + n:2 + 2 * n]), per_dest, peers), res[-1][0:1, 0:1]


def _direct_wait(started, after, name):
    send_sems, recv_sems, srcs, zones, per_dest, peers = started
    n = len(srcs)

    def body(*refs):
        src_refs, zone_refs = refs[:n], refs[n:2 * n]
        for cp in _direct_copies(src_refs, zone_refs, refs[2 * n], refs[2 * n + 1], per_dest, peers):
            cp.wait_send()
            cp.wait_recv()

    res = pl.pallas_call(
        body, name=name, in_specs=[HBM_SPEC] * (2 * n) + [SEM_SPEC, SEM_SPEC, HBM_SPEC],
        out_shape=[pltpu.HBM(s.shape, s.dtype) for s in srcs] + [pltpu.HBM(z.shape, z.dtype) for z in zones],
        out_specs=[HBM_SPEC] * (2 * n), input_output_aliases={i: i for i in range(2 * n)},
        compiler_params=pltpu.CompilerParams(has_side_effects=DATAFLOW),
    )(*srcs, *zones, send_sems, recv_sems, after)
    return list(res[n:])


def _gather_finish(zones, shards, name):
    n = len(zones)

    def body(*refs):
        shard_refs, zone_in, zone_out = refs[:n], refs[n:2 * n], refs[2 * n:3 * n]
        send_sems, recv_sems, local_sems = refs[3 * n:]
        x, y, c, chips = _place()
        cps = []
        for a in range(n):
            mine = pltpu.make_async_copy(shard_refs[a], zone_out[a].at[4 * x + 2 * y + c], local_sems.at[a])
            mine.start()
            cps.append(mine)
            for j, (cx, cy) in enumerate(chips):
                blk = 4 * cx + 2 * cy + c
                cp = pltpu.make_async_remote_copy(src_ref=zone_in[a].at[blk], dst_ref=zone_out[a].at[blk],
                                                  send_sem=send_sems.at[a, j], recv_sem=recv_sems.at[a, j],
                                                  device_id=(x, y, 1 - c), device_id_type=MESH)
                cp.start()
                cps.append(cp)
        for cp in cps:
            cp.wait()

    return pl.pallas_call(
        body, in_specs=[HBM_SPEC] * (2 * n), out_specs=[HBM_SPEC] * n,
        out_shape=[jax.ShapeDtypeStruct(z.shape, z.dtype) for z in zones], input_output_aliases={n + a: a for a in range(n)},
        scratch_shapes=[pltpu.SemaphoreType.DMA((n, 3)), pltpu.SemaphoreType.DMA((n, 3)), pltpu.SemaphoreType.DMA((n,))],
        name=name,
    )(*shards, *zones)


def _adamw(w, g, m, v):
    m = ADAM_B1 * m + (1.0 - ADAM_B1) * g
    v = ADAM_B2 * v + (1.0 - ADAM_B2) * jnp.square(g)
    m_hat = m / (1.0 - ADAM_B1 ** ADAM_STEP)
    v_hat = v / (1.0 - ADAM_B2 ** ADAM_STEP)
    delta = -ADAM_LR * (m_hat / (jnp.sqrt(v_hat) + ADAM_EPS) + ADAM_WD * w)
    return delta, m, v


def _adamw_layer(w, m, v, own, parts, layer, prev, name):
    _, rows, c = w.shape
    tm = _pick_tm(rows, c)

    def body(w_ref, m_ref, v_ref, own_ref, parts_ref, *rest):
        g = own_ref[...]
        for j in range(parts_ref.shape[0]):
            g = g + parts_ref[j].astype(F32)
        delta, m2, v2 = _adamw(w_ref[...], g, m_ref[...], v_ref[...])
        g_ref, d_ref, m2_ref, v2_ref = rest[-4:]
        g_ref[...], d_ref[...], m2_ref[...], v2_ref[...] = g, delta, m2, v2

    wspec = pl.BlockSpec((None, tm, c), lambda i: (layer, i, 0))
    prev = list(prev) if prev is not None else []
    return pl.pallas_call(
        body, grid=(rows // tm,),
        in_specs=[wspec] * 3 + [pl.BlockSpec((tm, c), lambda i: (i, 0)), pl.BlockSpec((parts.shape[0], tm, c), lambda i: (0, i, 0))]
        + [HBM_SPEC] * len(prev),
        out_specs=[wspec] * 4, out_shape=[jax.ShapeDtypeStruct(w.shape, F32)] * 4,
        input_output_aliases={5 + k: k for k in range(len(prev))}, name=name,
    )(w, m, v, own, parts, *prev)


def _adamw_call(w, m, v, gparts, name):
    def fn(wv, mv, vv, *gs):
        g = gs[0].astype(F32)
        for gp in gs[1:]:
            g = g + gp.astype(F32)
        delta, m2, v2 = _adamw(wv, g, mv, vv)
        return g, delta, m2, v2
    rows, c = w.shape
    return _ew(fn, [w, m, v] + list(gparts), [], [(c, F32)] * 4, tm=_pick_tm(rows, c), name=name)


def kernel(x, p, g_mix, w_in, b_fox_f, fox_q_gain, fox_k_gain, sc_conv_w, dn_conv_w, dn_a_log, dn_dt_bias,
           dn_norm_gain, w_branch, w_o, g_ffn, w_up, ffn_conv_w, w_down, g_ple, w_ple_gate, w_ple, loss_target,
           m_g_mix, m_w_in, m_b_fox_f, m_fox_q_gain, m_fox_k_gain, m_sc_conv_w, m_dn_conv_w, m_dn_a_log,
           m_dn_dt_bias, m_dn_norm_gain, m_w_branch, m_w_o, m_g_ffn, m_w_up, m_ffn_conv_w, m_w_down, m_g_ple,
           m_w_ple_gate, m_w_ple, v_g_mix, v_w_in, v_b_fox_f, v_fox_q_gain, v_fox_k_gain, v_sc_conv_w, v_dn_conv_w,
           v_dn_a_log, v_dn_dt_bias, v_dn_norm_gain, v_w_branch, v_w_o, v_g_ffn, v_w_up, v_ffn_conv_w, v_w_down,
           v_g_ple, v_w_ple_gate, v_w_ple):
    return _step(x, p, g_mix, w_in, b_fox_f, fox_q_gain, fox_k_gain, sc_conv_w, dn_conv_w, dn_a_log, dn_dt_bias,
                 dn_norm_gain, w_branch, w_o, g_ffn, w_up, ffn_conv_w, w_down, g_ple, w_ple_gate, w_ple, loss_target,
                 m_g_mix, m_w_in, m_b_fox_f, m_fox_q_gain, m_fox_k_gain, m_sc_conv_w, m_dn_conv_w, m_dn_a_log,
                 m_dn_dt_bias, m_dn_norm_gain, m_w_branch, m_w_o, m_g_ffn, m_w_up, m_ffn_conv_w, m_w_down, m_g_ple,
                 m_w_ple_gate, m_w_ple, v_g_mix, v_w_in, v_b_fox_f, v_fox_q_gain, v_fox_k_gain, v_sc_conv_w,
                 v_dn_conv_w, v_dn_a_log, v_dn_dt_bias, v_dn_norm_gain, v_w_branch, v_w_o, v_g_ffn, v_w_up,
                 v_ffn_conv_w, v_w_down, v_g_ple, v_w_ple_gate, v_w_ple)


def _step(*args):
    names = ['x', 'p'] + WEIGHTS + ['loss_target'] + ['m_' + n for n in WEIGHTS] + ['v_' + n for n in WEIGHTS]
    assert len(args) == len(names)
    a = dict(zip(names, args))
    for n, perm in HELD_TRANSPOSED.items():
        for k in (n, 'm_' + n, 'v_' + n):
            a[k] = jnp.transpose(a[k], perm)
    x, target = a['x'][0], a['loss_target'][0]
    p = a['p'][:, 0]
    dev = 4 * lax.axis_index("x") + 2 * lax.axis_index("y") + lax.axis_index("c")

    LATE = [n for n in BIG if n != 'w_in']
    sps = [{n: a[n][layer][None, :] for n in SMALL} for layer in range(DEPTH)]
    shards = [{n: a[n][layer].astype(BF16) for n in BIG} for layer in range(DEPTH)]

    def by_device(full, n):
        by_dest = _by_dest(full, n)
        return lax.dynamic_index_in_dim(by_dest, dev, axis=0, keepdims=False), by_dest.astype(BF16)

    def blank(zone):
        return lax.dynamic_update_index_in_dim(zone, jnp.zeros(zone.shape[1:], zone.dtype), dev, 0)

    w_in_all, conv_all = _all_gather([shards[0]['w_in'], _flat_pack([a[n] for n in CONVW])], 'gather_w_in_l0')
    conv_by_dev = [_flat_unpack(conv_all[d], CONV_SHARD_SHAPES) for d in range(N_DEV)]
    conv_full = {n: jnp.concatenate([conv_by_dev[d][i] for d in range(N_DEV)], axis=2) for i, n in enumerate(CONVW)}
    late0_started, zero = _direct_start([shards[0][n] for n in LATE], False, w_in_all, 'weights_rest_l0_start', NEAR_PEERS)
    w1_started, zero = _direct_start([shards[1][n] for n in BIG], False, late0_started[2][0], 'weights_l1_start')
    sps[0]['g_mix'] = sps[0]['g_mix'] + zero

    def late_weights0(after):
        zones = _gather_finish(_direct_wait(late0_started, after, 'weights_rest_l0_wait'), [shards[0][n] for n in LATE],
                               'weights_rest_l0_finish')
        return {n: _full_from_gathered(z, n) for n, z in zip(LATE, zones)}

    w0 = {n: conv_full[n][0] for n in CONVW}
    w0['w_in'] = _w_in_assemble(w_in_all, 'w_in_assemble_l0')
    h, sv0 = _layer_fwd(x, p[0], w0, sps[0], 'l0', late_weights0)
    zones = [lax.dynamic_update_index_in_dim(z, shards[1][n], dev, 0)
             for n, z in zip(BIG, _direct_wait(w1_started, h, 'weights_l1_wait'))]
    w1 = {n: _full_from_gathered(z, n) for n, z in zip(BIG, zones) if n != 'w_in'}
    w1.update({n: conv_full[n][1] for n in CONVW})
    w1['w_in'] = _w_in_assemble(zones[list(BIG).index('w_in')], 'w_in_assemble_l1')
    h, sv1 = _layer_fwd(h, p[1], w1, sps[1], 'l1')
    dh, loss_part = _loss_bwd(h, target, 'loss')
    grads = [None] * DEPTH
    dh, grads[1] = _layer_bwd(dh, sv1, sps[1], 'l1')
    owns1, sends1 = {}, {}
    for n in BIG:
        owns1[n], sends1[n] = _w_in_split_by_device(grads[1][n], 'w_in_grad_split_l1') if n == 'w_in' else by_device(grads[1][n], n)
    g1_started, zero = _direct_start([sends1[n] for n in BIG], True, dh, 'grads_l1_start')
    early = {}

    def early_grads0(g, sp):
        owns, sends = zip(*[by_device(g[n], n) for n in LATE])
        early['started'], zero = _direct_start(list(sends), True, g['w_branch'], 'grads_rest_l0_start')
        early['owns'] = dict(zip(LATE, owns))
        return {**sp, 'dn_norm_gain': sp['dn_norm_gain'] + zero}

    dh, grads[0] = _layer_bwd(dh, sv0, {**sps[0], 'g_ple': sps[0]['g_ple'] + zero}, 'l0', early_grads0)
    grad_x = dh[None]

    own_in0, send_in0 = _w_in_split_by_device(grads[0]['w_in'], 'w_in_grad_split_l0')
    in0_started, zero = _direct_start([send_in0], True, dh, 'grads_w_in_l0_start')
    sent = in0_started[2][0]
    zones1 = dict(zip(BIG, _direct_wait(g1_started, sent, 'grads_l1_wait')))
    zones0 = dict(zip(LATE, _direct_wait(early['started'], sent, 'grads_rest_l0_wait')))

    def adamw(n, layer, own, zone, prev):
        rows = (-1, a[n].shape[-1])
        own = own.reshape(rows)
        view = lambda t: t.reshape((DEPTH,) + own.shape)
        return _adamw_layer(view(a[n]), view(a['m_' + n]), view(a['v_' + n]), own, blank(zone).reshape((N_DEV,) + own.shape),
                            layer, prev, f'adamw_{n}_l{layer}')

    out, done = {}, []
    for n in LATE:
        res = adamw(n, 0, early['owns'][n], zones0[n], adamw(n, 1, owns1[n], zones1[n], None))
        out[n] = [r.reshape(a[n].shape) for r in res]
        done.append(res[0][0, 0:1, 0:1])
    res = adamw('w_in', 1, owns1['w_in'], zones1['w_in'], None)
    done.append(res[0][1, 0:1, 0:1])
    (zone_in0,) = _direct_wait(in0_started, jnp.concatenate(done, axis=1), 'grads_w_in_l0_wait')
    out['w_in'] = [r.reshape(a['w_in'].shape) for r in adamw('w_in', 0, own_in0, zone_in0, res)]

    small_pack = _flat_pack([jnp.stack([grads[layer][n].reshape(-1) for layer in range(DEPTH)]) for n in SMALL] + [loss_part[0, 0]])
    conv_pack = _flat_pack([jnp.stack([grads[layer][n] for layer in range(DEPTH)]) for n in CONVW])
    small_all, conv_all = _all_gather([small_pack, conv_pack], 'gather_small_grads')
    res = _adamw_call(_flat_pack([a[n] for n in SMALL]), _flat_pack([a['m_' + n] for n in SMALL]),
                      _flat_pack([a['v_' + n] for n in SMALL]), [small_all[d] for d in range(N_DEV)], 'adamw_replicated')
    loss = res[0].reshape(-1)[SMALL_LOSS_AT]
    for k, r in enumerate(res):
        for n, val in zip(SMALL, _flat_unpack(r, SMALL_SHAPES)):
            out.setdefault(n, [None] * 4)[k] = val
    (conv_sum,) = _ew(lambda *gs: (functools.reduce(lambda s, t: s + t, gs),), [conv_all[d] for d in range(N_DEV)], [],
                      [(PACK_W, F32)], tm=conv_pack.shape[0], name='conv_grad_sum')
    conv_own = [lax.dynamic_slice_in_dim(g, dev * (g.shape[2] // N_DEV), g.shape[2] // N_DEV, axis=2)
                for g in _flat_unpack(conv_sum, CONV_FULL_SHAPES)]
    res = _adamw_call(_flat_pack([a[n] for n in CONVW]), _flat_pack([a['m_' + n] for n in CONVW]),
                      _flat_pack([a['v_' + n] for n in CONVW]), [_flat_pack(conv_own)], 'adamw_conv')
    for k, r in enumerate(res):
        for n, val in zip(CONVW, _flat_unpack(r, CONV_SHARD_SHAPES)):
            out.setdefault(n, [None] * 4)[k] = val

    for n, perm in HELD_TRANSPOSED.items():
        out[n] = [jnp.transpose(r, perm) for r in out[n]]
    outs = [loss, grad_x]
    for k in range(4):
        outs += [out[n][k] for n in WEIGHTS]
    return tuple(outs)
```

```python
import functools

import jax
import jax.numpy as jnp
from jax import lax
from jax.experimental import pallas as pl
from jax.experimental.pallas import tpu as pltpu

F32 = jnp.float32
BF16 = jnp.bfloat16
HI = lax.Precision.HIGHEST

D_MODEL = 1024
DEPTH = 2
N_DEV = 8
PLE_DIM = 256
BW = 512
FOX_HEADS, FOX_DH = 8, 64
DN_HEADS, DN_DH = 4, 128
DN_CHUNK = 64
D_FF = 2816
EPS = 1e-6
NEG = -1e30

ADAM_LR, ADAM_B1, ADAM_B2, ADAM_EPS, ADAM_WD, ADAM_STEP = 0.001, 0.9, 0.999, 1e-08, 0.01, 10

C_FQ, C_FK, C_FV = 0, 512, 1024
C_SB, C_SC, C_SV = 1536, 2048, 2560
C_DQ, C_DK, C_DV, C_DZ = 3072, 3584, 4096, 4608
C_GATE = 5120
C_SMALL = 8192
IN_P = 8320
IN_ORIG = 8208

WEIGHTS = ['g_mix', 'w_in', 'b_fox_f', 'fox_q_gain', 'fox_k_gain', 'sc_conv_w', 'dn_conv_w', 'dn_a_log',
           'dn_dt_bias', 'dn_norm_gain', 'w_branch', 'w_o', 'g_ffn', 'w_up', 'ffn_conv_w', 'w_down', 'g_ple',
           'w_ple_gate', 'w_ple']
BIG = {'w_in': 1, 'w_branch': 2, 'w_o': 0, 'w_up': 0, 'w_down': 0, 'w_ple_gate': 0, 'w_ple': 1}
HELD_TRANSPOSED = {'w_up': (0, 2, 1)}
CONVW = {'sc_conv_w': 1, 'dn_conv_w': 1, 'ffn_conv_w': 1}
SHARDED = {**BIG, **CONVW}
SMALL = [n for n in WEIGHTS if n not in SHARDED]
MESH = pl.DeviceIdType.MESH


def _sigmoid(x):
    return 0.5 * (jnp.tanh(0.5 * x) + 1.0)


def _silu(x):
    return x * _sigmoid(x)


def _log1pexp_negabs(z):
    return jnp.log(1.0 + jnp.exp(-jnp.abs(z)))


def _log_sigmoid(z):
    return jnp.minimum(z, 0.0) - _log1pexp_negabs(z)


def _softplus(z):
    return jnp.maximum(z, 0.0) + _log1pexp_negabs(z)


def _rms(x, g):
    return x * lax.rsqrt(jnp.mean(x * x, axis=-1, keepdims=True) + EPS) * g


def _l2(x):
    return x * lax.rsqrt(jnp.sum(x * x, axis=-1, keepdims=True) + EPS)


def _dot(a, b, dims, precision=None):
    return lax.dot_general(a, b, (dims, ((), ())), preferred_element_type=F32, precision=precision)


NN = ((1,), (0,))
NT = ((1,), (1,))
TN = ((0,), (0,))


def _shift_down(x, s):
    if s == 0:
        return x
    t = lax.broadcasted_iota(jnp.int32, x.shape, 0)
    return jnp.where(t >= s, pltpu.roll(x, s, 0), 0.0)


def _shift_up(x, s):
    if s == 0:
        return x
    n = x.shape[0]
    t = lax.broadcasted_iota(jnp.int32, x.shape, 0)
    return jnp.where(t < n - s, pltpu.roll(x, n - s, 0), 0.0)


def _conv(x, w):
    k = w.shape[0]
    y = w[k - 1:k] * x
    for j in range(k - 1):
        y = y + w[j:j + 1] * _shift_down(x, k - 1 - j)
    return y


def _conv_bwd(x, w, dy):
    k = w.shape[0]
    dx = w[k - 1:k] * dy
    dws = []
    for j in range(k - 1):
        dx = dx + w[j:j + 1] * _shift_up(dy, k - 1 - j)
        dws.append(jnp.sum(dy * _shift_down(x, k - 1 - j), axis=0, keepdims=True))
    dws.append(jnp.sum(dy * x, axis=0, keepdims=True))
    return dx, dws


MM_VMEM_BUDGET = 36 << 20
MM_STEP_BYTES = 1 << 20


def _mm_tiles(m, n, k, a_size, b_size, tile_size, cast_a):
    best = None
    for tm in [d for d in (2048, 1024, 512, 256, 128) if d <= m and m % d == 0] or [m]:
        for tn in [d for d in range(128, min(n, 2048) + 1, 128) if n % d == 0] or [n]:
            vmem = 2 * tm * k * a_size + (2 * tm * k if cast_a else 0) + 2 * k * tn * b_size + 2 * tm * tn * tile_size
            if vmem > MM_VMEM_BUDGET:
                continue
            steps = (m // tm) * (n // tn)
            cost = m * k * a_size + (m // tm) * k * n * b_size + m * n * tile_size + steps * MM_STEP_BYTES
            if best is None or cost < best[0]:
                best = (cost, tm, tn)
    assert best is not None, (m, n, k)
    return best[1], best[2]


def _mm(a, b, mode, outs, *, epi=None, epi_args=(), name):
    if mode == 'nn':
        (m, k), (k2, n) = a.shape, b.shape
    elif mode == 'nt':
        (m, k), (n, k2) = a.shape, b.shape
    else:
        (k, m), (k2, n) = a.shape, b.shape
    assert k == k2, (a.shape, b.shape, mode)
    tile_size = sum(jnp.dtype(dt).itemsize for dt in outs) + sum(e.dtype.itemsize for e in epi_args if e.shape[0] != 1)
    tm, tn = _mm_tiles(m, n, k, a.dtype.itemsize, b.dtype.itemsize, tile_size, a.dtype != BF16)
    dims = {'nn': NN, 'nt': NT, 'tn': TN}[mode]
    a_spec = pl.BlockSpec((k, tm), lambda i, j: (0, i)) if mode == 'tn' else pl.BlockSpec((tm, k), lambda i, j: (i, 0))
    b_spec = pl.BlockSpec((tn, k), lambda i, j: (j, 0)) if mode == 'nt' else pl.BlockSpec((k, tn), lambda i, j: (0, j))
    e_specs = [pl.BlockSpec((1, tn), lambda i, j: (0, j)) if e.shape[0] == 1 else pl.BlockSpec((tm, tn), lambda i, j: (i, j))
               for e in epi_args]
    ne, no = len(epi_args), len(outs)
    cast_a = a.dtype != BF16

    def body(a_ref, b_ref, *rest):
        if cast_a:
            a_sc = rest[-1]

            @pl.when(pl.program_id(1) == 0)
            def _():
                a_sc[...] = a_ref[...].astype(BF16)
            av = a_sc[...]
        else:
            av = a_ref[...]
        acc = _dot(av, b_ref[...].astype(BF16), dims)
        vals = epi(acc, *[e[...] for e in rest[:ne]]) if epi is not None else (acc,)
        for o_ref, v in zip(rest[ne:ne + no], vals):
            o_ref[...] = v.astype(o_ref.dtype)

    res = pl.pallas_call(
        body, grid=(m // tm, n // tn),
        in_specs=[a_spec, b_spec] + e_specs,
        out_specs=[pl.BlockSpec((tm, tn), lambda i, j: (i, j)) for _ in outs],
        out_shape=[jax.ShapeDtypeStruct((m, n), dt) for dt in outs],
        scratch_shapes=[pltpu.VMEM(a_spec.block_shape, BF16)] if cast_a else [],
        name=name,
    )(a, b, *epi_args)
    return res[0] if no == 1 else res


def _ew(fn, tiled, bcast, outs, reds=(), *, tm=256, name):
    secs = [(t, 0, t.shape[1]) if not isinstance(t, tuple) else t for t in tiled]
    m = secs[0][0].shape[0]
    tm = min(tm, m)
    assert m % tm == 0
    in_specs = []
    for arr, off, w in secs:
        assert off % w == 0
        in_specs.append(pl.BlockSpec((tm, w), functools.partial(lambda i, c: (i, c), c=off // w)))
    in_specs += [pl.BlockSpec(b.shape, lambda i: (0, 0)) for b in bcast]
    nin, no = len(in_specs), len(outs)

    def body(*refs):
        vals = fn(*[r[...] for r in refs[:nin]])
        for r, v in zip(refs[nin:nin + no], vals[:no]):
            r[...] = v.astype(r.dtype)
        i = pl.program_id(0)
        for r, v in zip(refs[nin + no:], vals[no:]):
            @pl.when(i == 0)
            def _():
                r[...] = v

            @pl.when(i > 0)
            def _():
                r[...] += v

    res = pl.pallas_call(
        body, grid=(m // tm,), in_specs=in_specs,
        out_specs=[pl.BlockSpec((tm, c), lambda i: (i, 0)) for c, _ in outs] + [pl.BlockSpec(s, lambda i: (0, 0)) for s in reds],
        out_shape=[jax.ShapeDtypeStruct((m, c), dt) for c, dt in outs] + [jax.ShapeDtypeStruct(s, F32) for s in reds],
        name=name,
    )(*[s[0] for s in secs], *bcast)
    return res


def _cb(fn, cols, params, outs, pouts, *, tc, nblk, name):
    t = cols[0][0].shape[0]
    in_specs = []
    for arr, off in cols:
        assert off % tc == 0
        in_specs.append(pl.BlockSpec((t, tc), functools.partial(lambda c, o: (0, o + c), o=off // tc)))
    for arr, off in params:
        in_specs.append(pl.BlockSpec((arr.shape[0], tc), functools.partial(lambda c, o: (0, o + c), o=off // tc)))
    nin, no = len(in_specs), len(outs)

    def body(*refs):
        vals = fn(*[r[...] for r in refs[:nin]])
        for r, v in zip(refs[nin:], vals):
            r[...] = v.astype(r.dtype)

    return pl.pallas_call(
        body, grid=(nblk,), in_specs=in_specs,
        out_specs=[pl.BlockSpec((t, tc), lambda c: (0, c)) for _ in outs] + [pl.BlockSpec((k, tc), lambda c: (0, c)) for k in pouts],
        out_shape=[jax.ShapeDtypeStruct((t, nblk * tc), dt) for dt in outs] + [jax.ShapeDtypeStruct((k, nblk * tc), F32) for k in pouts],
        name=name,
    )(*[c[0] for c in cols], *[p[0] for p in params])


CUM_BLK = 256


def _prep_point(s, bias, alog):
    lane = lax.broadcasted_iota(jnp.int32, s.shape, 1)
    z = s + bias
    return jnp.where(lane < 8, _log_sigmoid(z),
                     jnp.where(lane < 12, _sigmoid(s),
                               jnp.where(lane < 16, -jnp.exp(alog) * _softplus(z), 0.0)))


def _tri(n, upper):
    r = lax.broadcasted_iota(jnp.int32, (n, n), 0)
    c = lax.broadcasted_iota(jnp.int32, (n, n), 1)
    return (r <= c if upper else r >= c).astype(F32)


def _prep_fwd(proj, bias_row, alog_row, name):
    t = proj.shape[0]
    nb = t // CUM_BLK

    def body(s_ref, b_ref, a_ref, o_ref):
        pre = _prep_point(s_ref[...], b_ref[...], a_ref[...])
        lane = lax.broadcasted_iota(jnp.int32, (CUM_BLK, 128), 1)
        tri = _tri(CUM_BLK, False)
        carry = jnp.zeros((1, 128), F32)
        for blk in range(nb):
            xb = pre[blk * CUM_BLK:(blk + 1) * CUM_BLK]
            cb = _dot(tri, xb, NN, HI) + carry
            carry = cb[CUM_BLK - 1:CUM_BLK]
            o_ref[blk * CUM_BLK:(blk + 1) * CUM_BLK, :] = jnp.where(lane < 8, cb, xb)

    return pl.pallas_call(
        body, grid=(1,),
        in_specs=[pl.BlockSpec((t, 128), lambda i: (0, C_SMALL // 128)), pl.BlockSpec((1, 128), lambda i: (0, 0)),
                  pl.BlockSpec((1, 128), lambda i: (0, 0))],
        out_specs=pl.BlockSpec((t, 128), lambda i: (0, 0)),
        out_shape=jax.ShapeDtypeStruct((t, 128), F32), name=name,
    )(proj, bias_row, alog_row)


def _prep_bwd(proj, bias_row, alog_row, daux, name):
    t = proj.shape[0]
    nb = t // CUM_BLK

    def body(s_ref, b_ref, a_ref, d_ref, ds_ref, db_ref, da_ref, dpre_sc):
        lane = lax.broadcasted_iota(jnp.int32, (CUM_BLK, 128), 1)
        tri = _tri(CUM_BLK, True)
        carry = jnp.zeros((1, 128), F32)
        for blk in reversed(range(nb)):
            db = d_ref[blk * CUM_BLK:(blk + 1) * CUM_BLK, :]
            cb = _dot(tri, db, NN, HI) + carry
            carry = cb[0:1]
            dpre_sc[blk * CUM_BLK:(blk + 1) * CUM_BLK, :] = jnp.where(lane < 8, cb, db)
        _, vjp = jax.vjp(_prep_point, s_ref[...], b_ref[...], a_ref[...])
        ds, dbias, dalog = vjp(dpre_sc[...])
        ds_ref[...] = ds.astype(ds_ref.dtype)
        db_ref[...] = dbias
        da_ref[...] = dalog

    return pl.pallas_call(
        body, grid=(1,),
        in_specs=[pl.BlockSpec((t, 128), lambda i: (0, C_SMALL // 128)), pl.BlockSpec((1, 128), lambda i: (0, 0)),
                  pl.BlockSpec((1, 128), lambda i: (0, 0)), pl.BlockSpec((t, 128), lambda i: (0, 0))],
        out_specs=[pl.BlockSpec((t, 128), lambda i: (0, 0)), pl.BlockSpec((1, 128), lambda i: (0, 0)),
                   pl.BlockSpec((1, 128), lambda i: (0, 0))],
        out_shape=[jax.ShapeDtypeStruct((t, 128), BF16), jax.ShapeDtypeStruct((1, 128), F32), jax.ShapeDtypeStruct((1, 128), F32)],
        scratch_shapes=[pltpu.VMEM((t, 128), F32)], name=name,
    )(proj, bias_row, alog_row, daux)


ATT_TQ = 256
FOX_SCALE = FOX_DH ** -0.5


def _qnorm(q, g):
    return _rms(q, g) * FOX_SCALE


def _att_scores(qn_blk, kn, cfc_blk, cfr, qi, tq, kend):
    s = _dot(qn_blk.astype(BF16), kn[:kend].astype(BF16), NT) + cfc_blk - cfr[:, :kend]
    row = lax.broadcasted_iota(jnp.int32, (tq, kend), 0) + qi * tq
    col = lax.broadcasted_iota(jnp.int32, (tq, kend), 1)
    return s, row >= col


ATT_PAIR = 128 // FOX_DH


def _att_specs(t):
    pair = lambda off: pl.BlockSpec((t, 128), functools.partial(lambda i, o: (0, o + i), o=off // 128))
    gain = pl.BlockSpec((1, FOX_DH), lambda i: (0, 0))
    hd = lambda i: (i, 0, 0)
    col, row = pl.BlockSpec((ATT_PAIR, t, 1), hd), pl.BlockSpec((ATT_PAIR, 1, t), hd)
    return [pair(C_FQ), pair(C_FK), pair(C_FV), gain, gain, col, row], pl.BlockSpec((t, 128), lambda i: (0, i)), col, row


def _att_fwd(proj, qg, kg, cfc, cfr, name):
    t = proj.shape[0]
    tq = min(ATT_TQ, t)
    in_specs, pair_out, col, _ = _att_specs(t)

    def body(q_ref, k_ref, v_ref, qg_ref, kg_ref, cfc_ref, cfr_ref, o_ref, lse_ref):
        for e in range(ATT_PAIR):
            lanes = slice(e * FOX_DH, (e + 1) * FOX_DH)
            qn = _qnorm(q_ref[:, lanes], qg_ref[...])
            kn = _rms(k_ref[:, lanes], kg_ref[...])
            v = v_ref[:, lanes].astype(BF16)
            cfr = cfr_ref[e]
            for qi in range(t // tq):
                kend = (qi + 1) * tq
                rows = slice(qi * tq, kend)
                s, mask = _att_scores(qn[rows], kn, cfc_ref[e, rows, :], cfr, qi, tq, kend)
                s = jnp.where(mask, s, NEG)
                m = jnp.max(s, axis=1, keepdims=True)
                p = jnp.exp(s - m)
                l = jnp.sum(p, axis=1, keepdims=True)
                o_ref[rows, lanes] = _dot(p.astype(BF16), v[:kend], NN) / l
                lse_ref[e, rows, :] = m + jnp.log(l)

    return pl.pallas_call(
        body, grid=(FOX_HEADS // ATT_PAIR,), in_specs=in_specs, out_specs=[pair_out, col],
        out_shape=[jax.ShapeDtypeStruct((t, BW), F32), jax.ShapeDtypeStruct((FOX_HEADS, t, 1), F32)], name=name,
    )(proj, proj, proj, qg, kg, cfc, cfr)


def _att_bwd(proj, qg, kg, cfc, cfr, lse, o, do, name):
    t = proj.shape[0]
    tq = min(ATT_TQ, t)
    in_specs, pair_out, col, row = _att_specs(t)

    def body(q_ref, k_ref, v_ref, qg_ref, kg_ref, cfc_ref, cfr_ref, lse_ref, o_ref, do_ref,
             dq_ref, dk_ref, dv_ref, dcfc_ref, dcfr_ref, dqg_ref, dkg_ref, dqn_sc, dkn_sc, dv_sc, dcfr_sc):
        for e in range(ATT_PAIR):
            lanes = slice(e * FOX_DH, (e + 1) * FOX_DH)
            qn, vjp_q = jax.vjp(_qnorm, q_ref[:, lanes], qg_ref[...])
            kn, vjp_k = jax.vjp(_rms, k_ref[:, lanes], kg_ref[...])
            v = v_ref[:, lanes].astype(BF16)
            cfr = cfr_ref[e]
            do_e = do_ref[:, lanes]
            delta = jnp.sum(do_e * o_ref[:, lanes], axis=1, keepdims=True)
            dkn_sc[...] = jnp.zeros_like(dkn_sc)
            dv_sc[...] = jnp.zeros_like(dv_sc)
            dcfr_sc[...] = jnp.zeros_like(dcfr_sc)
            for qi in range(t // tq):
                kend = (qi + 1) * tq
                rows = slice(qi * tq, kend)
                s, mask = _att_scores(qn[rows], kn, cfc_ref[e, rows, :], cfr, qi, tq, kend)
                p = jnp.where(mask, jnp.exp(jnp.where(mask, s, NEG) - lse_ref[e, rows, :]), 0.0)
                do_b = do_e[rows].astype(BF16)
                dv_sc[0:kend, :] += _dot(p.astype(BF16), do_b, TN)
                dp = _dot(do_b, v[:kend], NT)
                ds = p * (dp - delta[rows])
                ds_b = ds.astype(BF16)
                dqn_sc[rows, :] = _dot(ds_b, kn[:kend].astype(BF16), NN)
                dkn_sc[0:kend, :] += _dot(ds_b, qn[rows].astype(BF16), TN)
                dcfc_ref[e, rows, :] = jnp.sum(ds, axis=1, keepdims=True)
                dcfr_sc[:, 0:kend] -= jnp.sum(ds, axis=0, keepdims=True)
            dq, dqg = vjp_q(dqn_sc[...])
            dk, dkg = vjp_k(dkn_sc[...])
            dq_ref[:, lanes] = dq.astype(dq_ref.dtype)
            dk_ref[:, lanes] = dk.astype(dk_ref.dtype)
            dv_ref[:, lanes] = dv_sc[...].astype(dv_ref.dtype)
            dcfr_ref[e] = dcfr_sc[...]
            dqg_ref[e] = dqg
            dkg_ref[e] = dkg

    gsp = pl.BlockSpec((ATT_PAIR, 1, FOX_DH), lambda i: (i, 0, 0))
    return pl.pallas_call(
        body, grid=(FOX_HEADS // ATT_PAIR,),
        in_specs=in_specs + [col, pair_out, pair_out],
        out_specs=[pair_out] * 3 + [col, row, gsp, gsp],
        out_shape=[jax.ShapeDtypeStruct((t, BW), BF16)] * 3
        + [jax.ShapeDtypeStruct((FOX_HEADS, t, 1), F32), jax.ShapeDtypeStruct((FOX_HEADS, 1, t), F32)]
        + [jax.ShapeDtypeStruct((FOX_HEADS, 1, FOX_DH), F32)] * 2,
        scratch_shapes=[pltpu.VMEM((t, FOX_DH), F32)] * 3 + [pltpu.VMEM((1, t), F32)], name=name,
    )(proj, proj, proj, qg, kg, cfc, cfr, lse, o, do)


DN_SCALE = DN_DH ** -0.5


DN_BATCH = 8


@functools.partial(jax.custom_vjp, nondiff_argnums=(2, 3))
def _mm3(a, b, dims, batch=False):
    return _mm3_passes(a, b, dims, batch)


def _mm3_fwd(a, b, dims, batch):
    return _mm3_passes(a, b, dims, batch), (a, b)


def _mm3_bwd(dims, batch, res, dc):
    a, b = res
    if dims == NN:
        return _mm3_passes(dc, b, NT, batch), _mm3_passes(a, dc, TN, batch)
    if dims == NT:
        return _mm3_passes(dc, b, NN, batch), _mm3_passes(dc, a, TN, batch)
    return _mm3_passes(b, dc, NT, batch), _mm3_passes(a, dc, NN, batch)


_mm3.defvjp(_mm3_fwd, _mm3_bwd)


def _mm3_passes(a, b, dims, batch):
    if batch:
        dn = (((dims[0][0] + 1,), (dims[1][0] + 1,)), ((0,), (0,)))
        dot = lambda p, q: lax.dot_general(p, q, dn, preferred_element_type=F32)
    else:
        dot = lambda p, q: _dot(p, q, dims)
    ah, bh = a.astype(BF16), b.astype(BF16)
    al, bl = (a - ah.astype(F32)).astype(BF16), (b - bh.astype(F32)).astype(BF16)
    return dot(ah, bh) + (dot(ah, bl) + dot(al, bh))


def _dn_local(qc, kc, vc, g, beta):
    nb, c, _ = qc.shape
    ii = lax.broadcasted_iota(jnp.int32, (c, c), 0)
    jj = lax.broadcasted_iota(jnp.int32, (c, c), 1)
    incl, strict = ii >= jj, ii > jj
    lower = jnp.broadcast_to(incl.astype(F32), (nb, c, c))
    eye = (ii == jj).astype(F32)
    mm = functools.partial(_mm3, batch=True)
    dm = mm(lower, jnp.where(strict, g, 0.0), NN)
    decay = jnp.where(incl, jnp.exp(jnp.where(incl, dm, 0.0)), 0.0)
    gcum = mm(lower, g * jnp.ones((1, 1, DN_DH), F32), NN)
    eg = jnp.exp(gcum)
    glast = gcum[:, c - 1:c]
    kb = kc * beta
    n1 = jnp.where(strict, mm(kb, kc, NT) * decay, 0.0)
    inv = eye - n1
    pw = n1
    for _ in range(5):
        pw = mm(pw, pw, NN)
        inv = inv + mm(pw, inv, NN)
    sol = mm(inv, jnp.concatenate([vc * beta, kb * eg], axis=2), NN)
    qk = jnp.where(incl, mm(qc, kc, NT) * decay, 0.0)
    return sol[:, :, :DN_DH], sol[:, :, DN_DH:], qk, qc * eg, kc * jnp.exp(glast - gcum), jnp.exp(glast)


@functools.partial(jax.custom_vjp, nondiff_argnums=(2,))
def _mm1(a, b, dims):
    return _dot(a.astype(BF16), b.astype(BF16), dims)


def _mm1_fwd(a, b, dims):
    return _mm1(a, b, dims), (a, b)


def _mm1_bwd(dims, res, dc):
    a, b = res
    if dims == NN:
        return _mm1(dc, b, NT), _mm1(a, dc, TN)
    if dims == NT:
        return _mm1(dc, b, NN), _mm1(dc, a, TN)
    return _mm1(b, dc, NT), _mm1(a, dc, NN)


_mm1.defvjp(_mm1_fwd, _mm1_bwd)


def _dn_state(u, kcum, qk, qdec, kdec, egl, state):
    v_new = u - _mm1(kcum, state, NN)
    out = _mm1(qdec, state, NN) + _mm1(qk, v_new, NN)
    return out, state * egl + _mm1(kdec, v_new, TN)


def _dn_pre_q(c):
    return _l2(_silu(c)) * DN_SCALE


def _dn_pre_k(c):
    return _l2(_silu(c))


def _dn_post(o, z, ng):
    return _rms(o, ng) * _silu(z)


def _dn_specs(t):
    cblk = lambda o: pl.BlockSpec((t, DN_DH), functools.partial(lambda h, o: (0, o + h), o=o // DN_DH))
    wblk = lambda o: pl.BlockSpec((4, DN_DH), functools.partial(lambda h, o: (0, o + h), o=o // DN_DH))
    proj_specs = [cblk(C_DQ), cblk(C_DK), cblk(C_DV), cblk(C_DZ)]
    w_specs = [wblk(0), wblk(BW), wblk(2 * BW)]
    gb_spec = pl.BlockSpec((1, t, 2), lambda h: (h, 0, 0))
    return proj_specs, w_specs, gb_spec


def _chunk_rows(n, count=1):
    return pl.ds(pl.multiple_of(n * DN_CHUNK, DN_CHUNK), count * DN_CHUNK)


def _egl_rows(n, count=1):
    return pl.ds(pl.multiple_of(n * 8, 8), count * 8)


def _dn_local_inputs(n, qn_sc, kn_sc, vv_sc, gb_ref):
    r = _chunk_rows(n, DN_BATCH)
    split = lambda v: v.reshape(DN_BATCH, DN_CHUNK, v.shape[-1])
    gbv = split(gb_ref[0, r, :])
    return split(qn_sc[r, :]), split(kn_sc[r, :]), split(vv_sc[r, :]), gbv[:, :, 0:1], gbv[:, :, 1:2]


def _dn_local_phase(nc, qn_sc, kn_sc, vv_sc, gb_ref, loc):
    def step(i, carry):
        n = i * DN_BATCH
        vals = _dn_local(*_dn_local_inputs(n, qn_sc, kn_sc, vv_sc, gb_ref))
        for sc, val in zip(loc[:5], vals[:5]):
            sc[_chunk_rows(n, DN_BATCH), :] = val.reshape(DN_BATCH * DN_CHUNK, val.shape[-1])
        loc[5][_egl_rows(n, DN_BATCH), :] = jnp.broadcast_to(vals[5], (DN_BATCH, 8, DN_DH)).reshape(DN_BATCH * 8, DN_DH)
        return carry

    lax.fori_loop(0, nc // DN_BATCH, step, 0)


def _dn_loc_scratch(t, nc):
    big = pltpu.VMEM((t, DN_DH), F32)
    return [big, big, pltpu.VMEM((t, DN_CHUNK), F32), big, big, pltpu.VMEM((nc * 8, DN_DH), F32)]


def _dn_fwd(proj, conv_w, gb, ng, name):
    t = proj.shape[0]
    nc = t // DN_CHUNK
    assert nc % DN_BATCH == 0
    proj_specs, w_specs, gb_spec = _dn_specs(t)

    def body(q_ref, k_ref, v_ref, z_ref, wq_ref, wk_ref, wv_ref, gb_ref, ng_ref, y_ref, o_ref, st_ref, qn_sc, kn_sc, vv_sc, *loc):
        qn_sc[...] = _dn_pre_q(_conv(q_ref[...], wq_ref[...]))
        kn_sc[...] = _dn_pre_k(_conv(k_ref[...], wk_ref[...]))
        vv_sc[...] = _silu(_conv(v_ref[...], wv_ref[...]))
        _dn_local_phase(nc, qn_sc, kn_sc, vv_sc, gb_ref, loc)
        u_sc, kcum_sc, qk_sc, qdec_sc, kdec_sc, egl_sc = loc

        def chunk(n, state):
            r = _chunk_rows(n)
            egl = egl_sc[_egl_rows(n), :][0:1]
            out, new_state = _dn_state(u_sc[r, :], kcum_sc[r, :], qk_sc[r, :], qdec_sc[r, :], kdec_sc[r, :], egl, state)
            st_ref[0, n] = state
            o_ref[r, :] = out
            return new_state

        lax.fori_loop(0, nc, chunk, jnp.zeros((DN_DH, DN_DH), F32))
        y_ref[...] = _dn_post(o_ref[...], z_ref[...], ng_ref[...])

    hblk = pl.BlockSpec((t, DN_DH), lambda h: (0, h))
    return pl.pallas_call(
        body, grid=(DN_HEADS,),
        in_specs=proj_specs + w_specs + [gb_spec, pl.BlockSpec((1, DN_DH), lambda h: (0, 0))],
        out_specs=[hblk, hblk, pl.BlockSpec((1, nc, DN_DH, DN_DH), lambda h: (h, 0, 0, 0))],
        out_shape=[jax.ShapeDtypeStruct((t, BW), F32), jax.ShapeDtypeStruct((t, BW), F32),
                   jax.ShapeDtypeStruct((DN_HEADS, nc, DN_DH, DN_DH), F32)],
        scratch_shapes=[pltpu.VMEM((t, DN_DH), F32)] * 3 + _dn_loc_scratch(t, nc), name=name,
    )(proj, proj, proj, proj, conv_w, conv_w, conv_w, gb, ng)


def _dn_bwd(proj, conv_w, gb, ng, o, states, dy, name):
    t = proj.shape[0]
    nc = t // DN_CHUNK
    proj_specs, w_specs, gb_spec = _dn_specs(t)
    nloc = 6

    def body(q_ref, k_ref, v_ref, z_ref, wq_ref, wk_ref, wv_ref, gb_ref, ng_ref, o_ref, st_ref, dy_ref,
             dq_ref, dk_ref, dv_ref, dz_ref, dwq_ref, dwk_ref, dwv_ref, dgb_ref, dng_ref,
             qn_sc, kn_sc, vv_sc, do_sc, *rest):
        loc, dloc = rest[:nloc], rest[nloc:]
        qn_sc[...] = _dn_pre_q(_conv(q_ref[...], wq_ref[...]))
        kn_sc[...] = _dn_pre_k(_conv(k_ref[...], wk_ref[...]))
        vv_sc[...] = _silu(_conv(v_ref[...], wv_ref[...]))
        _, vjp_y = jax.vjp(_dn_post, o_ref[...], z_ref[...], ng_ref[...])
        do, dz, dng = vjp_y(dy_ref[...])
        do_sc[...] = do
        dz_ref[...] = dz.astype(dz_ref.dtype)
        dng_ref[0] = dng
        _dn_local_phase(nc, qn_sc, kn_sc, vv_sc, gb_ref, loc)
        u_sc, kcum_sc, qk_sc, qdec_sc, kdec_sc, egl_sc = loc

        def state_bwd(i, dstate):
            n = nc - 1 - i
            r = _chunk_rows(n)
            r8 = _egl_rows(n)
            _, vjp = jax.vjp(_dn_state, u_sc[r, :], kcum_sc[r, :], qk_sc[r, :], qdec_sc[r, :], kdec_sc[r, :],
                             egl_sc[r8, :][0:1], st_ref[0, n])
            du, dkcum, dqk, dqdec, dkdec, degl, dprev = vjp((do_sc[r, :], dstate))
            for d_sc, val in zip(dloc[:5], (du, dkcum, dqk, dqdec, dkdec)):
                d_sc[r, :] = val
            dloc[5][r8, :] = jnp.broadcast_to(degl, (8, DN_DH))
            return dprev

        lax.fori_loop(0, nc, state_bwd, jnp.zeros((DN_DH, DN_DH), F32))

        def local_bwd(i, carry):
            n = i * DN_BATCH
            r = _chunk_rows(n, DN_BATCH)
            _, vjp = jax.vjp(_dn_local, *_dn_local_inputs(n, qn_sc, kn_sc, vv_sc, gb_ref))
            cts = tuple(d_sc[r, :].reshape(DN_BATCH, DN_CHUNK, d_sc.shape[-1]) for d_sc in dloc[:5])
            cts += (dloc[5][_egl_rows(n, DN_BATCH), :].reshape(DN_BATCH, 8, DN_DH)[:, 0:1],)
            dqc, dkc, dvc, dg, dbeta = vjp(cts)
            for d_sc, val in zip((dloc[0], dloc[1], dloc[3]), (dqc, dkc, dvc)):
                d_sc[r, :] = val.reshape(DN_BATCH * DN_CHUNK, DN_DH)
            dgb_ref[0, r, :] = jnp.concatenate([dg, dbeta], axis=2).reshape(DN_BATCH * DN_CHUNK, 2)
            return carry

        lax.fori_loop(0, nc // DN_BATCH, local_bwd, 0)
        for x_ref, w_ref, pre, d_sc, dx_ref, dw_ref in ((q_ref, wq_ref, _dn_pre_q, dloc[0], dq_ref, dwq_ref),
                                                       (k_ref, wk_ref, _dn_pre_k, dloc[1], dk_ref, dwk_ref),
                                                       (v_ref, wv_ref, _silu, dloc[3], dv_ref, dwv_ref)):
            _, vjp = jax.vjp(pre, _conv(x_ref[...], w_ref[...]))
            (dc,) = vjp(d_sc[...])
            dx, dws = _conv_bwd(x_ref[...], w_ref[...], dc)
            dx_ref[...] = dx.astype(dx_ref.dtype)
            for j, dw in enumerate(dws):
                dw_ref[j:j + 1, :] = dw

    hblk = pl.BlockSpec((t, DN_DH), lambda h: (0, h))
    wout = pl.BlockSpec((4, DN_DH), lambda h: (0, h))
    return pl.pallas_call(
        body, grid=(DN_HEADS,),
        in_specs=proj_specs + w_specs + [gb_spec, pl.BlockSpec((1, DN_DH), lambda h: (0, 0)), hblk,
                                         pl.BlockSpec((1, nc, DN_DH, DN_DH), lambda h: (h, 0, 0, 0)), hblk],
        out_specs=[hblk] * 4 + [wout] * 3 + [gb_spec, pl.BlockSpec((1, 1, DN_DH), lambda h: (h, 0, 0))],
        out_shape=[jax.ShapeDtypeStruct((t, BW), BF16)] * 4 + [jax.ShapeDtypeStruct((4, BW), F32)] * 3
        + [jax.ShapeDtypeStruct((DN_HEADS, t, 2), F32), jax.ShapeDtypeStruct((DN_HEADS, 1, DN_DH), F32)],
        scratch_shapes=[pltpu.VMEM((t, DN_DH), F32)] * 4 + _dn_loc_scratch(t, nc) * 2, name=name,
    )(proj, proj, proj, proj, conv_w, conv_w, conv_w, gb, ng, o, states, dy)


MERGE_TM, MERGE_TN = 512, 512


def _merge_specs(t):
    tm, tn = min(MERGE_TM, t), MERGE_TN
    y_spec = pl.BlockSpec((tm, BW), lambda i, j: (i, 0))
    w_spec = pl.BlockSpec((3, BW, tn), lambda i, j: (0, 0, j))
    gate_specs = [pl.BlockSpec((tm, tn), functools.partial(lambda i, j, o: (i, o + j), o=(C_GATE + n * D_MODEL) // tn))
                  for n in range(3)]
    return tm, tn, [y_spec] * 3 + [w_spec] + gate_specs


def _merge_fwd(ys, wb, proj, name):
    t = proj.shape[0]
    tm, tn, in_specs = _merge_specs(t)

    def body(y0, y1, y2, w_ref, g0, g1, g2, o_ref):
        acc = jnp.zeros((tm, tn), F32)
        for n, (y, g) in enumerate(((y0, g0), (y1, g1), (y2, g2))):
            acc = acc + _dot(y[...].astype(BF16), w_ref[n], NN) * _sigmoid(g[...])
        o_ref[...] = acc.astype(o_ref.dtype)

    return pl.pallas_call(
        body, grid=(t // tm, D_MODEL // tn), in_specs=in_specs,
        out_specs=pl.BlockSpec((tm, tn), lambda i, j: (i, j)),
        out_shape=jax.ShapeDtypeStruct((t, D_MODEL), BF16), name=name,
    )(*ys, wb, proj, proj, proj)


def _merge_bwd(ys, wb, proj, dmerged, name):
    t = proj.shape[0]
    tm, tn, in_specs = _merge_specs(t)

    def body(y0, y1, y2, w_ref, g0, g1, g2, dm_ref, dg_ref, dt_ref):
        dm = dm_ref[...]
        for n, (y, g) in enumerate(((y0, g0), (y1, g1), (y2, g2))):
            tn_ = _dot(y[...].astype(BF16), w_ref[n], NN)
            sg = _sigmoid(g[...])
            dg_ref[n] = (dm * tn_ * sg * (1.0 - sg)).astype(dg_ref.dtype)
            dt_ref[n] = (dm * sg).astype(dt_ref.dtype)

    o3 = pl.BlockSpec((3, tm, tn), lambda i, j: (0, i, j))
    return pl.pallas_call(
        body, grid=(t // tm, D_MODEL // tn), in_specs=in_specs + [pl.BlockSpec((tm, tn), lambda i, j: (i, j))],
        out_specs=[o3, o3], out_shape=[jax.ShapeDtypeStruct((3, t, D_MODEL), BF16)] * 2, name=name,
    )(*ys, wb, proj, proj, proj, dmerged)


def _prep_rows(sp):
    z4, z112 = jnp.zeros((1, 4), F32), jnp.zeros((1, 112), F32)
    bias_row = jnp.concatenate([sp['b_fox_f'], z4, sp['dn_dt_bias'], z112], axis=1)
    alog_row = jnp.concatenate([jnp.zeros((1, 12), F32), sp['dn_a_log'], z112], axis=1)
    return bias_row, alog_row


def _sc_fwd_tile(sb, sc, sv, w):
    return (sb * _conv(sc * sv, w),)


def _ffn_act(ug, uv):
    return _silu(ug) * uv


W_IN_COLS = [(C_FQ, 0, 1536), (C_SB, 1544, 1536), (C_DQ, 3080, 1536), (C_DZ, 4624, 512), (C_GATE, 5136, 3072),
             (C_SMALL, 1536, 8), (C_SMALL + 8, 4616, 8)]
W_IN_SHARD = IN_ORIG // N_DEV


def _w_in_segments():
    out = []
    for d, o, w in W_IN_COLS:
        end = o + w
        while o < end:
            k = o // W_IN_SHARD
            n = min(end, (k + 1) * W_IN_SHARD) - o
            out.append((d, k, o - k * W_IN_SHARD, n))
            o, d = o + n, d + n
    return out


def _w_in_assemble(g, name):
    tm = 256

    def body(g_ref, o_ref):
        for d, k, s, n in _w_in_segments():
            o_ref[:, d:d + n] = g_ref[k, :, s:s + n]
        o_ref[:, IN_ORIG:IN_P] = jnp.zeros((tm, IN_P - IN_ORIG), o_ref.dtype)

    return pl.pallas_call(
        body, grid=(D_MODEL // tm,),
        in_specs=[pl.BlockSpec((N_DEV, tm, W_IN_SHARD), lambda i: (0, i, 0))],
        out_specs=pl.BlockSpec((tm, IN_P), lambda i: (i, 0)),
        out_shape=jax.ShapeDtypeStruct((D_MODEL, IN_P), g.dtype), name=name,
    )(g)


W_IN_SPLIT_TM = 128


def _w_in_split_by_device(g, name):
    tm = W_IN_SPLIT_TM

    def body(g_ref, own_ref, send_ref):
        me = 4 * lax.axis_index("x") + 2 * lax.axis_index("y") + lax.axis_index("c")
        for d, k, s, n in _w_in_segments():
            val = g_ref[:, d:d + n]
            send_ref[k, :, s:s + n] = val.astype(send_ref.dtype)

            @pl.when(me == k)
            def _():
                own_ref[:, s:s + n] = val

    return pl.pallas_call(
        body, grid=(D_MODEL // tm,), in_specs=[pl.BlockSpec((tm, IN_P), lambda i: (i, 0))],
        out_specs=[pl.BlockSpec((tm, W_IN_SHARD), lambda i: (i, 0)), pl.BlockSpec((N_DEV, tm, W_IN_SHARD), lambda i: (0, i, 0))],
        out_shape=[jax.ShapeDtypeStruct((D_MODEL, W_IN_SHARD), F32), jax.ShapeDtypeStruct((N_DEV, D_MODEL, W_IN_SHARD), BF16)],
        name=name,
    )(g)


def _layer_fwd(x, p_i, w, sp, tag, late_weights=None):
    t = x.shape[0]
    sv = {'x': x}
    (hn,) = _ew(lambda a, g: (_rms(a, g),), [x], [sp['g_mix']], [(D_MODEL, BF16)], name=f'rms_mix_{tag}')
    proj = _mm(hn, w['w_in'], 'nn', [F32], name=f'in_proj_{tag}')
    bias_row, alog_row = _prep_rows(sp)
    aux = _prep_fwd(proj, bias_row, alog_row, f'prep_{tag}')
    cf = aux[:, :FOX_HEADS].T
    cfc, cfr = cf[:, :, None], cf[:, None, :]
    y_fox, lse = _att_fwd(proj, sp['fox_q_gain'], sp['fox_k_gain'], cfc, cfr, f'fox_fwd_{tag}')
    (y_sc,) = _cb(_sc_fwd_tile, [(proj, C_SB), (proj, C_SC), (proj, C_SV)], [(w['sc_conv_w'], 0)], [F32], [],
                  tc=256, nblk=2, name=f'sc_fwd_{tag}')
    gb = jnp.stack([aux[:, 12:16].T, aux[:, 8:12].T], axis=-1)
    y_dn, o_dn, states = _dn_fwd(proj, w['dn_conv_w'], gb, sp['dn_norm_gain'], f'dn_fwd_{tag}')
    ys = (y_fox, y_sc, y_dn)
    if late_weights is not None:
        w = {**w, **late_weights(y_dn)}
    merged = _merge_fwd(ys, w['w_branch'], proj, f'merge_fwd_{tag}')
    x1 = _mm(merged, w['w_o'], 'nn', [F32], epi=lambda acc, r: (acc + r,), epi_args=(x,), name=f'o_proj_{tag}')
    (hf,) = _ew(lambda a, g: (_rms(a, g),), [x1], [sp['g_ffn']], [(D_MODEL, BF16)], name=f'rms_ffn_{tag}')
    up = _mm(hf, w['w_up'], 'nt', [F32], name=f'up_proj_{tag}')
    (act,) = _cb(lambda ug, uv, wg, wv: (_ffn_act(_conv(ug, wg), _conv(uv, wv)),), [(up, 0), (up, D_FF)],
                 [(w['ffn_conv_w'], 0), (w['ffn_conv_w'], D_FF)], [BF16], [], tc=256, nblk=D_FF // 256, name=f'ffn_act_{tag}')
    x2 = _mm(act, w['w_down'], 'nn', [F32], epi=lambda acc, r: (acc + r,), epi_args=(x1,), name=f'down_proj_{tag}')
    (hp,) = _ew(lambda a, g: (_rms(a, g),), [x2], [sp['g_ple']], [(D_MODEL, BF16)], name=f'rms_ple_{tag}')
    gp = _mm(hp, w['w_ple_gate'], 'nn', [F32], name=f'ple_gate_{tag}')
    x3 = _mm(p_i, w['w_ple'], 'nn', [F32], epi=lambda acc, g, r: (r + _sigmoid(g) * acc,), epi_args=(gp, x2), name=f'ple_{tag}')
    sv.update(hn=hn, proj=proj, aux=aux, cfc=cfc, cfr=cfr, lse=lse, ys=ys, gb=gb, o_dn=o_dn,
              states=states, merged=merged, x1=x1, hf=hf, up=up, act=act, x2=x2, hp=hp, gp=gp, p=p_i,
              bias_row=bias_row, alog_row=alog_row, w=w)
    return x3, sv


def _rms_bwd(x, g, dh, dres, name):
    def fn(xv, dhv, dr, gv):
        _, vjp = jax.vjp(_rms, xv, gv)
        dx, dg = vjp(dhv)
        return dr + dx, dg
    return _ew(fn, [x, dh, dres], [g], [(D_MODEL, F32)], [(1, D_MODEL)], name=name)


def _layer_bwd(dx3, sv, sp, tag, early_grads=None):
    t = dx3.shape[0]
    w = sv['w']
    g = {}
    def ple_epi(acc, gpv, d):
        s = _sigmoid(gpv)
        return d * acc * s * (1.0 - s), d * s
    dgp, de = _mm(sv['p'], w['w_ple'], 'nn', [BF16, BF16], epi=ple_epi, epi_args=(sv['gp'], dx3), name=f'ple_bwd_{tag}')
    g['w_ple'] = _mm(sv['p'], de, 'tn', [F32], name=f'd_w_ple_{tag}')
    g['w_ple_gate'] = _mm(sv['hp'], dgp, 'tn', [F32], name=f'd_w_ple_gate_{tag}')
    dhp = _mm(dgp, w['w_ple_gate'], 'nt', [F32], name=f'd_hp_{tag}')
    dx2, g['g_ple'] = _rms_bwd(sv['x2'], sp['g_ple'], dhp, dx3, f'rms_ple_bwd_{tag}')
    dact = _mm(dx2, w['w_down'], 'nt', [F32], name=f'd_act_{tag}')
    g['w_down'] = _mm(sv['act'], dx2, 'tn', [F32], name=f'd_w_down_{tag}')

    def ffn_bwd_tile(ug, uv, da, wg, wv):
        cg, cv = _conv(ug, wg), _conv(uv, wv)
        _, vjp = jax.vjp(_ffn_act, cg, cv)
        dcg, dcv = vjp(da)
        dug, dwg = _conv_bwd(ug, wg, dcg)
        duv, dwv = _conv_bwd(uv, wv, dcv)
        return dug, duv, jnp.concatenate(dwg, axis=0), jnp.concatenate(dwv, axis=0)
    dupg, dupv, dwg, dwv = _cb(ffn_bwd_tile, [(sv['up'], 0), (sv['up'], D_FF), (dact, 0)],
                               [(w['ffn_conv_w'], 0), (w['ffn_conv_w'], D_FF)], [BF16, BF16], [3, 3], tc=256,
                               nblk=D_FF // 256, name=f'ffn_act_bwd_{tag}')
    dup = jnp.concatenate([dupg, dupv], axis=1)
    g['ffn_conv_w'] = jnp.concatenate([dwg, dwv], axis=1)
    g['w_up'] = _mm(dup, sv['hf'], 'tn', [F32], name=f'd_w_up_{tag}')
    dhf = _mm(dup, w['w_up'], 'nn', [F32], name=f'd_hf_{tag}')
    dx1, g['g_ffn'] = _rms_bwd(sv['x1'], sp['g_ffn'], dhf, dx2, f'rms_ffn_bwd_{tag}')
    dmerged = _mm(dx1, w['w_o'], 'nt', [F32], name=f'd_merged_{tag}')
    g['w_o'] = _mm(sv['merged'], dx1, 'tn', [F32], name=f'd_w_o_{tag}')
    dgate, dtn = _merge_bwd(sv['ys'], w['w_branch'], sv['proj'], dmerged, f'merge_bwd_{tag}')
    dys, dwb = [], []
    for n in range(3):
        dys.append(_mm(dtn[n], w['w_branch'][n], 'nt', [F32], name=f'd_y{n}_{tag}'))
        dwb.append(_mm(sv['ys'][n], dtn[n], 'tn', [F32], name=f'd_w_branch{n}_{tag}'))
    g['w_branch'] = jnp.stack(dwb)
    if early_grads is not None:
        sp = early_grads(g, sp)
    ddq, ddk, ddv, ddz, dwq, dwk, dwv_, dgb, dng = _dn_bwd(sv['proj'], w['dn_conv_w'], sv['gb'], sp['dn_norm_gain'],
                                                           sv['o_dn'], sv['states'], dys[2], f'dn_bwd_{tag}')
    g['dn_conv_w'] = jnp.concatenate([dwq, dwk, dwv_], axis=1)
    g['dn_norm_gain'] = jnp.sum(dng, axis=0)
    def sc_bwd_tile(sb, sc, svv, dy, wv):
        u = sc * svv
        dsb = dy * _conv(u, wv)
        du, dws = _conv_bwd(u, wv, dy * sb)
        return dsb, du * svv, du * sc, jnp.concatenate(dws, axis=0)
    dsb, dsc, dsv, g['sc_conv_w'] = _cb(sc_bwd_tile, [(sv['proj'], C_SB), (sv['proj'], C_SC), (sv['proj'], C_SV), (dys[1], 0)],
                                        [(w['sc_conv_w'], 0)], [BF16, BF16, BF16], [3], tc=256, nblk=2, name=f'sc_bwd_{tag}')
    dfq, dfk, dfv, dcfc, dcfr, dqg, dkg = _att_bwd(sv['proj'], sp['fox_q_gain'], sp['fox_k_gain'], sv['cfc'], sv['cfr'],
                                                   sv['lse'], sv['ys'][0], dys[0], f'fox_bwd_{tag}')
    g['fox_q_gain'] = jnp.sum(dqg, axis=0)
    g['fox_k_gain'] = jnp.sum(dkg, axis=0)
    dcf = (dcfc[:, :, 0] + dcfr[:, 0, :]).T
    daux = jnp.concatenate([dcf, dgb[:, :, 1].T, dgb[:, :, 0].T, jnp.zeros((t, 112), F32)], axis=1)
    dsmall, dbias, dalog = _prep_bwd(sv['proj'], sv['bias_row'], sv['alog_row'], daux, f'prep_bwd_{tag}')
    g['b_fox_f'] = dbias[:, 0:8]
    g['dn_dt_bias'] = dbias[:, 12:16]
    g['dn_a_log'] = dalog[:, 12:16]
    dproj = jnp.concatenate([dfq, dfk, dfv, dsb, dsc, dsv, ddq, ddk, ddv, ddz, dgate[0], dgate[1], dgate[2], dsmall], axis=1)
    g['w_in'] = _mm(sv['hn'], dproj, 'tn', [F32], name=f'd_w_in_{tag}')
    dhn = _mm(dproj, w['w_in'], 'nt', [F32], name=f'd_hn_{tag}')
    dx, g['g_mix'] = _rms_bwd(sv['x'], sp['g_mix'], dhn, dx1, f'rms_mix_bwd_{tag}')
    return dx, g


def _loss_bwd(y, target, name):
    inv = 1.0 / y.shape[1]

    def fn(yv, tv):
        err = yv - tv
        return err * inv, jnp.zeros((8, 128), F32) + 0.5 * inv * jnp.sum(err * err)
    return _ew(fn, [y, target], [], [(y.shape[1], F32)], [(8, 128)], name=name)


PACK_W = 1024
FULL_SHAPE = {'w_in': (D_MODEL, IN_ORIG), 'w_branch': (3, BW, D_MODEL), 'w_o': (D_MODEL, D_MODEL), 'w_up': (2 * D_FF, D_MODEL),
              'w_down': (D_FF, D_MODEL), 'w_ple_gate': (D_MODEL, D_MODEL), 'w_ple': (PLE_DIM, D_MODEL),
              'sc_conv_w': (3, BW), 'dn_conv_w': (4, 3 * BW), 'ffn_conv_w': (3, 2 * D_FF)}
SMALL_SHAPE = {'g_mix': D_MODEL, 'b_fox_f': FOX_HEADS, 'fox_q_gain': FOX_DH, 'fox_k_gain': FOX_DH, 'dn_a_log': DN_HEADS,
               'dn_dt_bias': DN_HEADS, 'dn_norm_gain': DN_DH, 'g_ffn': D_MODEL, 'g_ple': D_MODEL}


def _shard_shape(name):
    s = list(FULL_SHAPE[name])
    s[SHARDED[name]] //= N_DEV
    return tuple(s)


def _full_from_gathered(g, name):
    sh, ax = _shard_shape(name), SHARDED[name]
    blocks = jnp.moveaxis(g, 0, ax)
    return blocks.reshape(sh[:ax] + (N_DEV * sh[ax],) + sh[ax + 1:])


def _by_dest(full, name):
    sh, ax = _shard_shape(name), SHARDED[name]
    return jnp.moveaxis(full.reshape(sh[:ax] + (N_DEV, sh[ax]) + sh[ax + 1:]), ax, 0)


def _flat_pack(arrs):
    flat = jnp.concatenate([a.reshape(-1).astype(F32) for a in arrs])
    rows = -(-flat.shape[0] // (8 * PACK_W)) * 8
    return jnp.pad(flat, (0, rows * PACK_W - flat.shape[0])).reshape(rows, PACK_W)


def _flat_unpack(pack, shapes):
    flat, out, off = pack.reshape(-1), [], 0
    for s in shapes:
        n = 1
        for d in s:
            n *= d
        out.append(flat[off:off + n].reshape(s))
        off += n
    return out


SMALL_SHAPES = [(DEPTH, SMALL_SHAPE[n]) for n in SMALL]
SMALL_LOSS_AT = sum(DEPTH * SMALL_SHAPE[n] for n in SMALL)
CONV_SHARD_SHAPES = [(DEPTH,) + _shard_shape(n) for n in CONVW]
CONV_FULL_SHAPES = [(DEPTH,) + FULL_SHAPE[n] for n in CONVW]


def _pick_tm(m, width):
    best = None
    for tm in range(16, m + 1, 16):
        if m % tm == 0 and tm * width * 4 <= (1 << 20):
            best = tm
    return best if best is not None else m


HBM_SPEC = pl.BlockSpec(memory_space=pltpu.HBM)


def _place():
    x, y, c = lax.axis_index("x"), lax.axis_index("y"), lax.axis_index("c")
    return x, y, c, [(1 - x, y), (x, 1 - y), (1 - x, 1 - y)]


def _all_gather(arrs, name):
    n = len(arrs)

    def body(*refs):
        ins, outs = refs[:n], refs[n:2 * n]
        send_sems, recv_sems, local_sems = refs[2 * n:]
        x, y, c, chips = _place()
        me, sibling = (x, y, c), (x, y, 1 - c)

        def block(a, p):
            return outs[a].at[4 * p[0] + 2 * p[1] + p[2]]

        def copy(a, k, blk, to, src=None):
            return pltpu.make_async_remote_copy(src_ref=block(a, blk) if src is None else src, dst_ref=block(a, blk),
                                                send_sem=send_sems.at[a, k], recv_sem=recv_sems.at[a, k],
                                                device_id=to, device_id_type=MESH)

        mine = [pltpu.make_async_copy(ins[a], block(a, me), local_sems.at[a]) for a in range(n)]
        first, passed = [], []
        for a in range(n):
            mine[a].start()
            first.append(copy(a, 0, me, sibling, src=ins[a]))
            first += [copy(a, 1 + j, me, (*chip, c), src=ins[a]) for j, chip in enumerate(chips)]
        for cp in first:
            cp.start()
        for j, chip in enumerate(chips):
            for a in range(n):
                copy(a, 1 + j, (*chip, c), me).wait_recv()
                fwd = copy(a, 4 + j, (*chip, c), sibling)
                fwd.start()
                passed.append(fwd)
        for a in range(n):
            copy(a, 0, sibling, me).wait_recv()
            for j, chip in enumerate(chips):
                copy(a, 4 + j, (*chip, 1 - c), me).wait_recv()
        for cp in first + passed:
            cp.wait_send()
        for a in range(n):
            mine[a].wait()

    return pl.pallas_call(
        body, in_specs=[HBM_SPEC] * n, out_specs=[HBM_SPEC] * n,
        out_shape=[jax.ShapeDtypeStruct((N_DEV,) + a.shape, a.dtype) for a in arrs],
        scratch_shapes=[pltpu.SemaphoreType.DMA((n, 7)), pltpu.SemaphoreType.DMA((n, 7)), pltpu.SemaphoreType.DMA((n,))],
        name=name,
    )(*arrs)


SEM_SPEC = pl.BlockSpec(memory_space=pltpu.SEMAPHORE)
DATAFLOW = pltpu.SideEffectType.DATAFLOW_SIDE_EFFECTING
ALL_PEERS = (1, 2, 3, 4, 5, 6, 7)
NEAR_PEERS = (1, 4, 2, 6)


def _direct_copies(src_refs, zone_refs, send_sems, recv_sems, per_dest, peers):
    x, y, c, _ = _place()
    me = 4 * x + 2 * y + c
    cps = []
    for a, (src, zone) in enumerate(zip(src_refs, zone_refs)):
        for r, bits in enumerate(peers):
            px = 1 - x if bits & 4 else x
            py = 1 - y if bits & 2 else y
            pc = 1 - c if bits & 1 else c
            cps.append(pltpu.make_async_remote_copy(
                src_ref=src.at[4 * px + 2 * py + pc] if per_dest else src, dst_ref=zone.at[me],
                send_sem=send_sems.at[a * len(peers) + r], recv_sem=recv_sems.at[a * len(peers) + r],
                device_id=(px, py, pc), device_id_type=MESH))
    return cps


def _direct_start(srcs, per_dest, after, name, peers=ALL_PEERS):
    n = len(srcs)
    zones = [lax.empty((N_DEV,) + (s.shape[1:] if per_dest else s.shape), s.dtype) for s in srcs]

    def body(*refs):
        src_refs, zone_refs = refs[:n], refs[n:2 * n]
        send_sems, recv_sems = refs[2 * n + 1:2 * n + 3]
        for cp in _direct_copies(src_refs, zone_refs, send_sems, recv_sems, per_dest, peers):
            cp.start()
        refs[-1][...] = jnp.zeros_like(refs[-1])

    sems = pltpu.SemaphoreType.DMA((n * len(peers),))
    res = pl.pallas_call(
        body, name=name, in_specs=[HBM_SPEC] * (2 * n + 1),
        out_shape=[sems, sems] + [pltpu.HBM(s.shape, s.dtype) for s in srcs] + [pltpu.HBM(z.shape, z.dtype) for z in zones]
        + [jax.ShapeDtypeStruct((8, 128), F32)],
        out_specs=[SEM_SPEC, SEM_SPEC] + [HBM_SPEC] * (2 * n) + [pl.BlockSpec(memory_space=pltpu.VMEM)],
        input_output_aliases={i: 2 + i for i in range(2 * n)},
        compiler_params=pltpu.CompilerParams(has_side_effects=DATAFLOW),
    )(*[pltpu.with_memory_space_constraint(s, pltpu.HBM) for s in srcs],
      *[pltpu.with_memory_space_constraint(z, pltpu.HBM) for z in zones], after)
    return (res[0], res[1], list(res[2:2 + n]), list(res[2 + n:2 + 2 * n]), per_dest, peers), res[-1][0:1, 0:1]


def _direct_wait(started, after, name):
    send_sems, recv_sems, srcs, zones, per_dest, peers = started
    n = len(srcs)

    def body(*refs):
        src_refs, zone_refs = refs[:n], refs[n:2 * n]
        for cp in _direct_copies(src_refs, zone_refs, refs[2 * n], refs[2 * n + 1], per_dest, peers):
            cp.wait_send()
            cp.wait_recv()

    res = pl.pallas_call(
        body, name=name, in_specs=[HBM_SPEC] * (2 * n) + [SEM_SPEC, SEM_SPEC, HBM_SPEC],
        out_shape=[pltpu.HBM(s.shape, s.dtype) for s in srcs] + [pltpu.HBM(z.shape, z.dtype) for z in zones],
        out_specs=[HBM_SPEC] * (2 * n), input_output_aliases={i: i for i in range(2 * n)},
        compiler_params=pltpu.CompilerParams(has_side_effects=DATAFLOW),
    )(*srcs, *zones, send_sems, recv_sems, after)
    return list(res[n:])


def _gather_finish(zones, name):
    n = len(zones)

    def body(*refs):
        zone_in, zone_out = refs[:n], refs[n:2 * n]
        send_sems, recv_sems = refs[2 * n:]
        x, y, c, chips = _place()
        cps = []
        for a in range(n):
            for j, (cx, cy) in enumerate(chips):
                blk = 4 * cx + 2 * cy + c
                cp = pltpu.make_async_remote_copy(src_ref=zone_in[a].at[blk], dst_ref=zone_out[a].at[blk],
                                                  send_sem=send_sems.at[a, j], recv_sem=recv_sems.at[a, j],
                                                  device_id=(x, y, 1 - c), device_id_type=MESH)
                cp.start()
                cps.append(cp)
        for cp in cps:
            cp.wait()

    return pl.pallas_call(
        body, in_specs=[HBM_SPEC] * n, out_specs=[HBM_SPEC] * n,
        out_shape=[jax.ShapeDtypeStruct(z.shape, z.dtype) for z in zones], input_output_aliases={a: a for a in range(n)},
        scratch_shapes=[pltpu.SemaphoreType.DMA((n, 3)), pltpu.SemaphoreType.DMA((n, 3))], name=name,
    )(*zones)


def _adamw(w, g, m, v):
    m = ADAM_B1 * m + (1.0 - ADAM_B1) * g
    v = ADAM_B2 * v + (1.0 - ADAM_B2) * jnp.square(g)
    m_hat = m / (1.0 - ADAM_B1 ** ADAM_STEP)
    v_hat = v / (1.0 - ADAM_B2 ** ADAM_STEP)
    delta = -ADAM_LR * (m_hat / (jnp.sqrt(v_hat) + ADAM_EPS) + ADAM_WD * w)
    return delta, m, v


def _adamw_layer(w, m, v, own, parts, layer, prev, name):
    _, rows, c = w.shape
    tm = _pick_tm(rows, c)

    def body(w_ref, m_ref, v_ref, own_ref, parts_ref, *rest):
        g = own_ref[...]
        for j in range(parts_ref.shape[0]):
            g = g + parts_ref[j].astype(F32)
        delta, m2, v2 = _adamw(w_ref[...], g, m_ref[...], v_ref[...])
        g_ref, d_ref, m2_ref, v2_ref = rest[-4:]
        g_ref[...], d_ref[...], m2_ref[...], v2_ref[...] = g, delta, m2, v2

    wspec = pl.BlockSpec((None, tm, c), lambda i: (layer, i, 0))
    prev = list(prev) if prev is not None else []
    return pl.pallas_call(
        body, grid=(rows // tm,),
        in_specs=[wspec] * 3 + [pl.BlockSpec((tm, c), lambda i: (i, 0)), pl.BlockSpec((parts.shape[0], tm, c), lambda i: (0, i, 0))]
        + [HBM_SPEC] * len(prev),
        out_specs=[wspec] * 4, out_shape=[jax.ShapeDtypeStruct(w.shape, F32)] * 4,
        input_output_aliases={5 + k: k for k in range(len(prev))}, name=name,
    )(w, m, v, own, parts, *prev)


def _adamw_call(w, m, v, gparts, name):
    def fn(wv, mv, vv, *gs):
        g = gs[0].astype(F32)
        for gp in gs[1:]:
            g = g + gp.astype(F32)
        delta, m2, v2 = _adamw(wv, g, mv, vv)
        return g, delta, m2, v2
    rows, c = w.shape
    return _ew(fn, [w, m, v] + list(gparts), [], [(c, F32)] * 4, tm=_pick_tm(rows, c), name=name)


def kernel(x, p, g_mix, w_in, b_fox_f, fox_q_gain, fox_k_gain, sc_conv_w, dn_conv_w, dn_a_log, dn_dt_bias,
           dn_norm_gain, w_branch, w_o, g_ffn, w_up, ffn_conv_w, w_down, g_ple, w_ple_gate, w_ple, loss_target,
           m_g_mix, m_w_in, m_b_fox_f, m_fox_q_gain, m_fox_k_gain, m_sc_conv_w, m_dn_conv_w, m_dn_a_log,
           m_dn_dt_bias, m_dn_norm_gain, m_w_branch, m_w_o, m_g_ffn, m_w_up, m_ffn_conv_w, m_w_down, m_g_ple,
           m_w_ple_gate, m_w_ple, v_g_mix, v_w_in, v_b_fox_f, v_fox_q_gain, v_fox_k_gain, v_sc_conv_w, v_dn_conv_w,
           v_dn_a_log, v_dn_dt_bias, v_dn_norm_gain, v_w_branch, v_w_o, v_g_ffn, v_w_up, v_ffn_conv_w, v_w_down,
           v_g_ple, v_w_ple_gate, v_w_ple):
    return _step(x, p, g_mix, w_in, b_fox_f, fox_q_gain, fox_k_gain, sc_conv_w, dn_conv_w, dn_a_log, dn_dt_bias,
                 dn_norm_gain, w_branch, w_o, g_ffn, w_up, ffn_conv_w, w_down, g_ple, w_ple_gate, w_ple, loss_target,
                 m_g_mix, m_w_in, m_b_fox_f, m_fox_q_gain, m_fox_k_gain, m_sc_conv_w, m_dn_conv_w, m_dn_a_log,
                 m_dn_dt_bias, m_dn_norm_gain, m_w_branch, m_w_o, m_g_ffn, m_w_up, m_ffn_conv_w, m_w_down, m_g_ple,
                 m_w_ple_gate, m_w_ple, v_g_mix, v_w_in, v_b_fox_f, v_fox_q_gain, v_fox_k_gain, v_sc_conv_w,
                 v_dn_conv_w, v_dn_a_log, v_dn_dt_bias, v_dn_norm_gain, v_w_branch, v_w_o, v_g_ffn, v_w_up,
                 v_ffn_conv_w, v_w_down, v_g_ple, v_w_ple_gate, v_w_ple)


def _step(*args):
    names = ['x', 'p'] + WEIGHTS + ['loss_target'] + ['m_' + n for n in WEIGHTS] + ['v_' + n for n in WEIGHTS]
    assert len(args) == len(names)
    a = dict(zip(names, args))
    for n, perm in HELD_TRANSPOSED.items():
        for k in (n, 'm_' + n, 'v_' + n):
            a[k] = jnp.transpose(a[k], perm)
    x, target = a['x'][0], a['loss_target'][0]
    p = a['p'][:, 0]
    dev = 4 * lax.axis_index("x") + 2 * lax.axis_index("y") + lax.axis_index("c")

    LATE = [n for n in BIG if n != 'w_in']
    sps = [{n: a[n][layer][None, :] for n in SMALL} for layer in range(DEPTH)]
    shards = [{n: a[n][layer].astype(BF16) for n in BIG} for layer in range(DEPTH)]

    def by_device(full, n):
        by_dest = _by_dest(full, n)
        return lax.dynamic_index_in_dim(by_dest, dev, axis=0, keepdims=False), by_dest.astype(BF16)

    def blank(zone):
        return lax.dynamic_update_index_in_dim(zone, jnp.zeros(zone.shape[1:], zone.dtype), dev, 0)

    w_in_all, conv_all = _all_gather([shards[0]['w_in'], _flat_pack([a[n] for n in CONVW])], 'gather_w_in_l0')
    conv_by_dev = [_flat_unpack(conv_all[d], CONV_SHARD_SHAPES) for d in range(N_DEV)]
    conv_full = {n: jnp.concatenate([conv_by_dev[d][i] for d in range(N_DEV)], axis=2) for i, n in enumerate(CONVW)}
    late0_started, zero = _direct_start([shards[0][n] for n in LATE], False, w_in_all, 'weights_rest_l0_start', NEAR_PEERS)
    w1_started, zero = _direct_start([shards[1][n] for n in BIG], False, late0_started[2][0], 'weights_l1_start')
    sps[0]['g_mix'] = sps[0]['g_mix'] + zero

    def late_weights0(after):
        zones = _gather_finish(_direct_wait(late0_started, after, 'weights_rest_l0_wait'), 'weights_rest_l0_finish')
        return {n: _full_from_gathered(lax.dynamic_update_index_in_dim(z, shards[0][n], dev, 0), n) for n, z in zip(LATE, zones)}

    w0 = {n: conv_full[n][0] for n in CONVW}
    w0['w_in'] = _w_in_assemble(w_in_all, 'w_in_assemble_l0')
    h, sv0 = _layer_fwd(x, p[0], w0, sps[0], 'l0', late_weights0)
    zones = [lax.dynamic_update_index_in_dim(z, shards[1][n], dev, 0)
             for n, z in zip(BIG, _direct_wait(w1_started, h, 'weights_l1_wait'))]
    w1 = {n: _full_from_gathered(z, n) for n, z in zip(BIG, zones) if n != 'w_in'}
    w1.update({n: conv_full[n][1] for n in CONVW})
    w1['w_in'] = _w_in_assemble(zones[list(BIG).index('w_in')], 'w_in_assemble_l1')
    h, sv1 = _layer_fwd(h, p[1], w1, sps[1], 'l1')
    dh, loss_part = _loss_bwd(h, target, 'loss')
    grads = [None] * DEPTH
    dh, grads[1] = _layer_bwd(dh, sv1, sps[1], 'l1')
    owns1, sends1 = {}, {}
    for n in BIG:
        owns1[n], sends1[n] = _w_in_split_by_device(grads[1][n], 'w_in_grad_split_l1') if n == 'w_in' else by_device(grads[1][n], n)
    g1_started, zero = _direct_start([sends1[n] for n in BIG], True, dh, 'grads_l1_start')
    early = {}

    def early_grads0(g, sp):
        owns, sends = zip(*[by_device(g[n], n) for n in LATE])
        early['started'], zero = _direct_start(list(sends), True, g['w_branch'], 'grads_rest_l0_start')
        early['owns'] = dict(zip(LATE, owns))
        return {**sp, 'dn_norm_gain': sp['dn_norm_gain'] + zero}

    dh, grads[0] = _layer_bwd(dh, sv0, {**sps[0], 'g_ple': sps[0]['g_ple'] + zero}, 'l0', early_grads0)
    grad_x = dh[None]

    own_in0, send_in0 = _w_in_split_by_device(grads[0]['w_in'], 'w_in_grad_split_l0')
    in0_started, zero = _direct_start([send_in0], True, dh, 'grads_w_in_l0_start')
    sent = in0_started[2][0]
    zones1 = dict(zip(BIG, _direct_wait(g1_started, sent, 'grads_l1_wait')))
    zones0 = dict(zip(LATE, _direct_wait(early['started'], sent, 'grads_rest_l0_wait')))

    def adamw(n, layer, own, zone, prev):
        rows = (-1, a[n].shape[-1])
        own = own.reshape(rows)
        view = lambda t: t.reshape((DEPTH,) + own.shape)
        return _adamw_layer(view(a[n]), view(a['m_' + n]), view(a['v_' + n]), own, blank(zone).reshape((N_DEV,) + own.shape),
                            layer, prev, f'adamw_{n}_l{layer}')

    out, done = {}, []
    for n in LATE:
        res = adamw(n, 0, early['owns'][n], zones0[n], adamw(n, 1, owns1[n], zones1[n], None))
        out[n] = [r.reshape(a[n].shape) for r in res]
        done.append(res[0][0, 0:1, 0:1])
    res = adamw('w_in', 1, owns1['w_in'], zones1['w_in'], None)
    done.append(res[0][1, 0:1, 0:1])
    (zone_in0,) = _direct_wait(in0_started, jnp.concatenate(done, axis=1), 'grads_w_in_l0_wait')
    out['w_in'] = [r.reshape(a['w_in'].shape) for r in adamw('w_in', 0, own_in0, zone_in0, res)]

    small_pack = _flat_pack([jnp.stack([grads[layer][n].reshape(-1) for layer in range(DEPTH)]) for n in SMALL] + [loss_part[0, 0]])
    conv_pack = _flat_pack([jnp.stack([grads[layer][n] for layer in range(DEPTH)]) for n in CONVW])
    small_all, conv_all = _all_gather([small_pack, conv_pack], 'gather_small_grads')
    res = _adamw_call(_flat_pack([a[n] for n in SMALL]), _flat_pack([a['m_' + n] for n in SMALL]),
                      _flat_pack([a['v_' + n] for n in SMALL]), [small_all[d] for d in range(N_DEV)], 'adamw_replicated')
    loss = res[0].reshape(-1)[SMALL_LOSS_AT]
    for k, r in enumerate(res):
        for n, val in zip(SMALL, _flat_unpack(r, SMALL_SHAPES)):
            out.setdefault(n, [None] * 4)[k] = val
    (conv_sum,) = _ew(lambda *gs: (functools.reduce(lambda s, t: s + t, gs),), [conv_all[d] for d in range(N_DEV)], [],
                      [(PACK_W, F32)], tm=conv_pack.shape[0], name='conv_grad_sum')
    conv_own = [lax.dynamic_slice_in_dim(g, dev * (g.shape[2] // N_DEV), g.shape[2] // N_DEV, axis=2)
                for g in _flat_unpack(conv_sum, CONV_FULL_SHAPES)]
    res = _adamw_call(_flat_pack([a[n] for n in CONVW]), _flat_pack([a['m_' + n] for n in CONVW]),
                      _flat_pack([a['v_' + n] for n in CONVW]), [_flat_pack(conv_own)], 'adamw_conv')
    for k, r in enumerate(res):
        for n, val in zip(CONVW, _flat_unpack(r, CONV_SHARD_SHAPES)):
            out.setdefault(n, [None] * 4)[k] = val

    for n, perm in HELD_TRANSPOSED.items():
        out[n] = [jnp.transpose(r, perm) for r in out[n]]
    outs = [loss, grad_x]
    for k in range(4):
        outs += [out[n][k] for n in WEIGHTS]
    return tuple(outs)
```

```python
import functools

import jax
import jax.numpy as jnp
from jax import lax
from jax.experimental import pallas as pl
from jax.experimental.pallas import tpu as pltpu

F32 = jnp.float32
BF16 = jnp.bfloat16
HI = lax.Precision.HIGHEST

D_MODEL = 1024
DEPTH = 2
N_DEV = 8
PLE_DIM = 256
BW = 512
FOX_HEADS, FOX_DH = 8, 64
DN_HEADS, DN_DH = 4, 128
DN_CHUNK = 64
D_FF = 2816
EPS = 1e-6
NEG = -1e30

ADAM_LR, ADAM_B1, ADAM_B2, ADAM_EPS, ADAM_WD, ADAM_STEP = 0.001, 0.9, 0.999, 1e-08, 0.01, 10

C_FQ, C_FK, C_FV = 0, 512, 1024
C_SB, C_SC, C_SV = 1536, 2048, 2560
C_DQ, C_DK, C_DV, C_DZ = 3072, 3584, 4096, 4608
C_GATE = 5120
C_SMALL = 8192
IN_P = 8320
IN_ORIG = 8208

WEIGHTS = ['g_mix', 'w_in', 'b_fox_f', 'fox_q_gain', 'fox_k_gain', 'sc_conv_w', 'dn_conv_w', 'dn_a_log',
           'dn_dt_bias', 'dn_norm_gain', 'w_branch', 'w_o', 'g_ffn', 'w_up', 'ffn_conv_w', 'w_down', 'g_ple',
           'w_ple_gate', 'w_ple']
BIG = {'w_in': 1, 'w_branch': 2, 'w_o': 0, 'w_up': 0, 'w_down': 0, 'w_ple_gate': 0, 'w_ple': 1}
HELD_TRANSPOSED = {'w_up': (0, 2, 1)}
CONVW = {'sc_conv_w': 1, 'dn_conv_w': 1, 'ffn_conv_w': 1}
SHARDED = {**BIG, **CONVW}
SMALL = [n for n in WEIGHTS if n not in SHARDED]
MESH = pl.DeviceIdType.MESH


def _sigmoid(x):
    return 0.5 * (jnp.tanh(0.5 * x) + 1.0)


def _silu(x):
    return x * _sigmoid(x)


def _log1pexp_negabs(z):
    return jnp.log(1.0 + jnp.exp(-jnp.abs(z)))


def _log_sigmoid(z):
    return jnp.minimum(z, 0.0) - _log1pexp_negabs(z)


def _softplus(z):
    return jnp.maximum(z, 0.0) + _log1pexp_negabs(z)


def _rms(x, g):
    return x * lax.rsqrt(jnp.mean(x * x, axis=-1, keepdims=True) + EPS) * g


def _l2(x):
    return x * lax.rsqrt(jnp.sum(x * x, axis=-1, keepdims=True) + EPS)


def _dot(a, b, dims, precision=None):
    return lax.dot_general(a, b, (dims, ((), ())), preferred_element_type=F32, precision=precision)


NN = ((1,), (0,))
NT = ((1,), (1,))
TN = ((0,), (0,))


def _shift_down(x, s):
    if s == 0:
        return x
    t = lax.broadcasted_iota(jnp.int32, x.shape, 0)
    return jnp.where(t >= s, pltpu.roll(x, s, 0), 0.0)


def _shift_up(x, s):
    if s == 0:
        return x
    n = x.shape[0]
    t = lax.broadcasted_iota(jnp.int32, x.shape, 0)
    return jnp.where(t < n - s, pltpu.roll(x, n - s, 0), 0.0)


def _conv(x, w):
    k = w.shape[0]
    y = w[k - 1:k] * x
    for j in range(k - 1):
        y = y + w[j:j + 1] * _shift_down(x, k - 1 - j)
    return y


def _conv_bwd(x, w, dy):
    k = w.shape[0]
    dx = w[k - 1:k] * dy
    dws = []
    for j in range(k - 1):
        dx = dx + w[j:j + 1] * _shift_up(dy, k - 1 - j)
        dws.append(jnp.sum(dy * _shift_down(x, k - 1 - j), axis=0, keepdims=True))
    dws.append(jnp.sum(dy * x, axis=0, keepdims=True))
    return dx, dws


MM_VMEM_BUDGET = 36 << 20
MM_STEP_BYTES = 1 << 20


def _mm_tiles(m, n, k, a_size, b_size, tile_size, cast_a):
    best = None
    for tm in [d for d in (2048, 1024, 512, 256, 128) if d <= m and m % d == 0] or [m]:
        for tn in [d for d in range(128, min(n, 2048) + 1, 128) if n % d == 0] or [n]:
            vmem = 2 * tm * k * a_size + (2 * tm * k if cast_a else 0) + 2 * k * tn * b_size + 2 * tm * tn * tile_size
            if vmem > MM_VMEM_BUDGET:
                continue
            steps = (m // tm) * (n // tn)
            cost = m * k * a_size + (m // tm) * k * n * b_size + m * n * tile_size + steps * MM_STEP_BYTES
            if best is None or cost < best[0]:
                best = (cost, tm, tn)
    assert best is not None, (m, n, k)
    return best[1], best[2]


def _mm(a, b, mode, outs, *, epi=None, epi_args=(), name):
    if mode == 'nn':
        (m, k), (k2, n) = a.shape, b.shape
    elif mode == 'nt':
        (m, k), (n, k2) = a.shape, b.shape
    else:
        (k, m), (k2, n) = a.shape, b.shape
    assert k == k2, (a.shape, b.shape, mode)
    tile_size = sum(jnp.dtype(dt).itemsize for dt in outs) + sum(e.dtype.itemsize for e in epi_args if e.shape[0] != 1)
    tm, tn = _mm_tiles(m, n, k, a.dtype.itemsize, b.dtype.itemsize, tile_size, a.dtype != BF16)
    dims = {'nn': NN, 'nt': NT, 'tn': TN}[mode]
    a_spec = pl.BlockSpec((k, tm), lambda i, j: (0, i)) if mode == 'tn' else pl.BlockSpec((tm, k), lambda i, j: (i, 0))
    b_spec = pl.BlockSpec((tn, k), lambda i, j: (j, 0)) if mode == 'nt' else pl.BlockSpec((k, tn), lambda i, j: (0, j))
    e_specs = [pl.BlockSpec((1, tn), lambda i, j: (0, j)) if e.shape[0] == 1 else pl.BlockSpec((tm, tn), lambda i, j: (i, j))
               for e in epi_args]
    ne, no = len(epi_args), len(outs)
    cast_a = a.dtype != BF16

    def body(a_ref, b_ref, *rest):
        if cast_a:
            a_sc = rest[-1]

            @pl.when(pl.program_id(1) == 0)
            def _():
                a_sc[...] = a_ref[...].astype(BF16)
            av = a_sc[...]
        else:
            av = a_ref[...]
        acc = _dot(av, b_ref[...].astype(BF16), dims)
        vals = epi(acc, *[e[...] for e in rest[:ne]]) if epi is not None else (acc,)
        for o_ref, v in zip(rest[ne:ne + no], vals):
            o_ref[...] = v.astype(o_ref.dtype)

    res = pl.pallas_call(
        body, grid=(m // tm, n // tn),
        in_specs=[a_spec, b_spec] + e_specs,
        out_specs=[pl.BlockSpec((tm, tn), lambda i, j: (i, j)) for _ in outs],
        out_shape=[jax.ShapeDtypeStruct((m, n), dt) for dt in outs],
        scratch_shapes=[pltpu.VMEM(a_spec.block_shape, BF16)] if cast_a else [],
        name=name,
    )(a, b, *epi_args)
    return res[0] if no == 1 else res


def _ew(fn, tiled, bcast, outs, reds=(), *, tm=256, name):
    secs = [(t, 0, t.shape[1]) if not isinstance(t, tuple) else t for t in tiled]
    m = secs[0][0].shape[0]
    tm = min(tm, m)
    assert m % tm == 0
    in_specs = []
    for arr, off, w in secs:
        assert off % w == 0
        in_specs.append(pl.BlockSpec((tm, w), functools.partial(lambda i, c: (i, c), c=off // w)))
    in_specs += [pl.BlockSpec(b.shape, lambda i: (0, 0)) for b in bcast]
    nin, no = len(in_specs), len(outs)

    def body(*refs):
        vals = fn(*[r[...] for r in refs[:nin]])
        for r, v in zip(refs[nin:nin + no], vals[:no]):
            r[...] = v.astype(r.dtype)
        i = pl.program_id(0)
        for r, v in zip(refs[nin + no:], vals[no:]):
            @pl.when(i == 0)
            def _():
                r[...] = v

            @pl.when(i > 0)
            def _():
                r[...] += v

    res = pl.pallas_call(
        body, grid=(m // tm,), in_specs=in_specs,
        out_specs=[pl.BlockSpec((tm, c), lambda i: (i, 0)) for c, _ in outs] + [pl.BlockSpec(s, lambda i: (0, 0)) for s in reds],
        out_shape=[jax.ShapeDtypeStruct((m, c), dt) for c, dt in outs] + [jax.ShapeDtypeStruct(s, F32) for s in reds],
        name=name,
    )(*[s[0] for s in secs], *bcast)
    return res


def _cb(fn, cols, params, outs, pouts, *, tc, nblk, name):
    t = cols[0][0].shape[0]
    in_specs = []
    for arr, off in cols:
        assert off % tc == 0
        in_specs.append(pl.BlockSpec((t, tc), functools.partial(lambda c, o: (0, o + c), o=off // tc)))
    for arr, off in params:
        in_specs.append(pl.BlockSpec((arr.shape[0], tc), functools.partial(lambda c, o: (0, o + c), o=off // tc)))
    nin, no = len(in_specs), len(outs)

    def body(*refs):
        vals = fn(*[r[...] for r in refs[:nin]])
        for r, v in zip(refs[nin:], vals):
            r[...] = v.astype(r.dtype)

    return pl.pallas_call(
        body, grid=(nblk,), in_specs=in_specs,
        out_specs=[pl.BlockSpec((t, tc), lambda c: (0, c)) for _ in outs] + [pl.BlockSpec((k, tc), lambda c: (0, c)) for k in pouts],
        out_shape=[jax.ShapeDtypeStruct((t, nblk * tc), dt) for dt in outs] + [jax.ShapeDtypeStruct((k, nblk * tc), F32) for k in pouts],
        name=name,
    )(*[c[0] for c in cols], *[p[0] for p in params])


CUM_BLK = 256


def _prep_point(s, bias, alog):
    lane = lax.broadcasted_iota(jnp.int32, s.shape, 1)
    z = s + bias
    return jnp.where(lane < 8, _log_sigmoid(z),
                     jnp.where(lane < 12, _sigmoid(s),
                               jnp.where(lane < 16, -jnp.exp(alog) * _softplus(z), 0.0)))


def _tri(n, upper):
    r = lax.broadcasted_iota(jnp.int32, (n, n), 0)
    c = lax.broadcasted_iota(jnp.int32, (n, n), 1)
    return (r <= c if upper else r >= c).astype(F32)


def _prep_fwd(proj, bias_row, alog_row, name):
    t = proj.shape[0]
    nb = t // CUM_BLK

    def body(s_ref, b_ref, a_ref, o_ref):
        pre = _prep_point(s_ref[...], b_ref[...], a_ref[...])
        lane = lax.broadcasted_iota(jnp.int32, (CUM_BLK, 128), 1)
        tri = _tri(CUM_BLK, False)
        carry = jnp.zeros((1, 128), F32)
        for blk in range(nb):
            xb = pre[blk * CUM_BLK:(blk + 1) * CUM_BLK]
            cb = _dot(tri, xb, NN, HI) + carry
            carry = cb[CUM_BLK - 1:CUM_BLK]
            o_ref[blk * CUM_BLK:(blk + 1) * CUM_BLK, :] = jnp.where(lane < 8, cb, xb)

    return pl.pallas_call(
        body, grid=(1,),
        in_specs=[pl.BlockSpec((t, 128), lambda i: (0, C_SMALL // 128)), pl.BlockSpec((1, 128), lambda i: (0, 0)),
                  pl.BlockSpec((1, 128), lambda i: (0, 0))],
        out_specs=pl.BlockSpec((t, 128), lambda i: (0, 0)),
        out_shape=jax.ShapeDtypeStruct((t, 128), F32), name=name,
    )(proj, bias_row, alog_row)


def _prep_bwd(proj, bias_row, alog_row, daux, name):
    t = proj.shape[0]
    nb = t // CUM_BLK

    def body(s_ref, b_ref, a_ref, d_ref, ds_ref, db_ref, da_ref, dpre_sc):
        lane = lax.broadcasted_iota(jnp.int32, (CUM_BLK, 128), 1)
        tri = _tri(CUM_BLK, True)
        carry = jnp.zeros((1, 128), F32)
        for blk in reversed(range(nb)):
            db = d_ref[blk * CUM_BLK:(blk + 1) * CUM_BLK, :]
            cb = _dot(tri, db, NN, HI) + carry
            carry = cb[0:1]
            dpre_sc[blk * CUM_BLK:(blk + 1) * CUM_BLK, :] = jnp.where(lane < 8, cb, db)
        _, vjp = jax.vjp(_prep_point, s_ref[...], b_ref[...], a_ref[...])
        ds, dbias, dalog = vjp(dpre_sc[...])
        ds_ref[...] = ds.astype(ds_ref.dtype)
        db_ref[...] = dbias
        da_ref[...] = dalog

    return pl.pallas_call(
        body, grid=(1,),
        in_specs=[pl.BlockSpec((t, 128), lambda i: (0, C_SMALL // 128)), pl.BlockSpec((1, 128), lambda i: (0, 0)),
                  pl.BlockSpec((1, 128), lambda i: (0, 0)), pl.BlockSpec((t, 128), lambda i: (0, 0))],
        out_specs=[pl.BlockSpec((t, 128), lambda i: (0, 0)), pl.BlockSpec((1, 128), lambda i: (0, 0)),
                   pl.BlockSpec((1, 128), lambda i: (0, 0))],
        out_shape=[jax.ShapeDtypeStruct((t, 128), BF16), jax.ShapeDtypeStruct((1, 128), F32), jax.ShapeDtypeStruct((1, 128), F32)],
        scratch_shapes=[pltpu.VMEM((t, 128), F32)], name=name,
    )(proj, bias_row, alog_row, daux)


ATT_TQ = 256
FOX_SCALE = FOX_DH ** -0.5


def _qnorm(q, g):
    return _rms(q, g) * FOX_SCALE


def _att_scores(qn_blk, kn, cfc_blk, cfr, qi, tq, kend):
    s = _dot(qn_blk.astype(BF16), kn[:kend].astype(BF16), NT) + cfc_blk - cfr[:, :kend]
    row = lax.broadcasted_iota(jnp.int32, (tq, kend), 0) + qi * tq
    col = lax.broadcasted_iota(jnp.int32, (tq, kend), 1)
    return s, row >= col


ATT_PAIR = 128 // FOX_DH


def _att_specs(t):
    pair = lambda off: pl.BlockSpec((t, 128), functools.partial(lambda i, o: (0, o + i), o=off // 128))
    gain = pl.BlockSpec((1, FOX_DH), lambda i: (0, 0))
    hd = lambda i: (i, 0, 0)
    col, row = pl.BlockSpec((ATT_PAIR, t, 1), hd), pl.BlockSpec((ATT_PAIR, 1, t), hd)
    return [pair(C_FQ), pair(C_FK), pair(C_FV), gain, gain, col, row], pl.BlockSpec((t, 128), lambda i: (0, i)), col, row


def _att_fwd(proj, qg, kg, cfc, cfr, name):
    t = proj.shape[0]
    tq = min(ATT_TQ, t)
    in_specs, pair_out, col, _ = _att_specs(t)

    def body(q_ref, k_ref, v_ref, qg_ref, kg_ref, cfc_ref, cfr_ref, o_ref, lse_ref):
        for e in range(ATT_PAIR):
            lanes = slice(e * FOX_DH, (e + 1) * FOX_DH)
            qn = _qnorm(q_ref[:, lanes], qg_ref[...])
            kn = _rms(k_ref[:, lanes], kg_ref[...])
            v = v_ref[:, lanes].astype(BF16)
            cfr = cfr_ref[e]
            for qi in range(t // tq):
                kend = (qi + 1) * tq
                rows = slice(qi * tq, kend)
                s, mask = _att_scores(qn[rows], kn, cfc_ref[e, rows, :], cfr, qi, tq, kend)
                s = jnp.where(mask, s, NEG)
                m = jnp.max(s, axis=1, keepdims=True)
                p = jnp.exp(s - m)
                l = jnp.sum(p, axis=1, keepdims=True)
                o_ref[rows, lanes] = _dot(p.astype(BF16), v[:kend], NN) / l
                lse_ref[e, rows, :] = m + jnp.log(l)

    return pl.pallas_call(
        body, grid=(FOX_HEADS // ATT_PAIR,), in_specs=in_specs, out_specs=[pair_out, col],
        out_shape=[jax.ShapeDtypeStruct((t, BW), F32), jax.ShapeDtypeStruct((FOX_HEADS, t, 1), F32)], name=name,
    )(proj, proj, proj, qg, kg, cfc, cfr)


def _att_bwd(proj, qg, kg, cfc, cfr, lse, o, do, name):
    t = proj.shape[0]
    tq = min(ATT_TQ, t)
    in_specs, pair_out, col, row = _att_specs(t)

    def body(q_ref, k_ref, v_ref, qg_ref, kg_ref, cfc_ref, cfr_ref, lse_ref, o_ref, do_ref,
             dq_ref, dk_ref, dv_ref, dcfc_ref, dcfr_ref, dqg_ref, dkg_ref, dqn_sc, dkn_sc, dv_sc, dcfr_sc):
        for e in range(ATT_PAIR):
            lanes = slice(e * FOX_DH, (e + 1) * FOX_DH)
            qn, vjp_q = jax.vjp(_qnorm, q_ref[:, lanes], qg_ref[...])
            kn, vjp_k = jax.vjp(_rms, k_ref[:, lanes], kg_ref[...])
            v = v_ref[:, lanes].astype(BF16)
            cfr = cfr_ref[e]
            do_e = do_ref[:, lanes]
            delta = jnp.sum(do_e * o_ref[:, lanes], axis=1, keepdims=True)
            dkn_sc[...] = jnp.zeros_like(dkn_sc)
            dv_sc[...] = jnp.zeros_like(dv_sc)
            dcfr_sc[...] = jnp.zeros_like(dcfr_sc)
            for qi in range(t // tq):
                kend = (qi + 1) * tq
                rows = slice(qi * tq, kend)
                s, mask = _att_scores(qn[rows], kn, cfc_ref[e, rows, :], cfr, qi, tq, kend)
                p = jnp.where(mask, jnp.exp(jnp.where(mask, s, NEG) - lse_ref[e, rows, :]), 0.0)
                do_b = do_e[rows].astype(BF16)
                dv_sc[0:kend, :] += _dot(p.astype(BF16), do_b, TN)
                dp = _dot(do_b, v[:kend], NT)
                ds = p * (dp - delta[rows])
                ds_b = ds.astype(BF16)
                dqn_sc[rows, :] = _dot(ds_b, kn[:kend].astype(BF16), NN)
                dkn_sc[0:kend, :] += _dot(ds_b, qn[rows].astype(BF16), TN)
                dcfc_ref[e, rows, :] = jnp.sum(ds, axis=1, keepdims=True)
                dcfr_sc[:, 0:kend] -= jnp.sum(ds, axis=0, keepdims=True)
            dq, dqg = vjp_q(dqn_sc[...])
            dk, dkg = vjp_k(dkn_sc[...])
            dq_ref[:, lanes] = dq.astype(dq_ref.dtype)
            dk_ref[:, lanes] = dk.astype(dk_ref.dtype)
            dv_ref[:, lanes] = dv_sc[...].astype(dv_ref.dtype)
            dcfr_ref[e] = dcfr_sc[...]
            dqg_ref[e] = dqg
            dkg_ref[e] = dkg

    gsp = pl.BlockSpec((ATT_PAIR, 1, FOX_DH), lambda i: (i, 0, 0))
    return pl.pallas_call(
        body, grid=(FOX_HEADS // ATT_PAIR,),
        in_specs=in_specs + [col, pair_out, pair_out],
        out_specs=[pair_out] * 3 + [col, row, gsp, gsp],
        out_shape=[jax.ShapeDtypeStruct((t, BW), BF16)] * 3
        + [jax.ShapeDtypeStruct((FOX_HEADS, t, 1), F32), jax.ShapeDtypeStruct((FOX_HEADS, 1, t), F32)]
        + [jax.ShapeDtypeStruct((FOX_HEADS, 1, FOX_DH), F32)] * 2,
        scratch_shapes=[pltpu.VMEM((t, FOX_DH), F32)] * 3 + [pltpu.VMEM((1, t), F32)], name=name,
    )(proj, proj, proj, qg, kg, cfc, cfr, lse, o, do)


DN_SCALE = DN_DH ** -0.5


DN_BATCH = 8


@functools.partial(jax.custom_vjp, nondiff_argnums=(2, 3))
def _mm3(a, b, dims, batch=False):
    return _mm3_passes(a, b, dims, batch)


def _mm3_fwd(a, b, dims, batch):
    return _mm3_passes(a, b, dims, batch), (a, b)


def _mm3_bwd(dims, batch, res, dc):
    a, b = res
    if dims == NN:
        return _mm3_passes(dc, b, NT, batch), _mm3_passes(a, dc, TN, batch)
    if dims == NT:
        return _mm3_passes(dc, b, NN, batch), _mm3_passes(dc, a, TN, batch)
    return _mm3_passes(b, dc, NT, batch), _mm3_passes(a, dc, NN, batch)


_mm3.defvjp(_mm3_fwd, _mm3_bwd)


def _mm3_passes(a, b, dims, batch):
    if batch:
        dn = (((dims[0][0] + 1,), (dims[1][0] + 1,)), ((0,), (0,)))
        dot = lambda p, q: lax.dot_general(p, q, dn, preferred_element_type=F32)
    else:
        dot = lambda p, q: _dot(p, q, dims)
    ah, bh = a.astype(BF16), b.astype(BF16)
    al, bl = (a - ah.astype(F32)).astype(BF16), (b - bh.astype(F32)).astype(BF16)
    return dot(ah, bh) + (dot(ah, bl) + dot(al, bh))


def _dn_local(qc, kc, vc, g, beta):
    nb, c, _ = qc.shape
    ii = lax.broadcasted_iota(jnp.int32, (c, c), 0)
    jj = lax.broadcasted_iota(jnp.int32, (c, c), 1)
    incl, strict = ii >= jj, ii > jj
    lower = jnp.broadcast_to(incl.astype(F32), (nb, c, c))
    eye = (ii == jj).astype(F32)
    mm = functools.partial(_mm3, batch=True)
    dm = mm(lower, jnp.where(strict, g, 0.0), NN)
    decay = jnp.where(incl, jnp.exp(jnp.where(incl, dm, 0.0)), 0.0)
    gcum = mm(lower, g * jnp.ones((1, 1, DN_DH), F32), NN)
    eg = jnp.exp(gcum)
    glast = gcum[:, c - 1:c]
    kb = kc * beta
    n1 = jnp.where(strict, mm(kb, kc, NT) * decay, 0.0)
    inv = eye - n1
    pw = n1
    for _ in range(5):
        pw = mm(pw, pw, NN)
        inv = inv + mm(pw, inv, NN)
    sol = mm(inv, jnp.concatenate([vc * beta, kb * eg], axis=2), NN)
    qk = jnp.where(incl, mm(qc, kc, NT) * decay, 0.0)
    return sol[:, :, :DN_DH], sol[:, :, DN_DH:], qk, qc * eg, kc * jnp.exp(glast - gcum), jnp.exp(glast)


@functools.partial(jax.custom_vjp, nondiff_argnums=(2,))
def _mm1(a, b, dims):
    return _dot(a.astype(BF16), b.astype(BF16), dims)


def _mm1_fwd(a, b, dims):
    return _mm1(a, b, dims), (a, b)


def _mm1_bwd(dims, res, dc):
    a, b = res
    if dims == NN:
        return _mm1(dc, b, NT), _mm1(a, dc, TN)
    if dims == NT:
        return _mm1(dc, b, NN), _mm1(dc, a, TN)
    return _mm1(b, dc, NT), _mm1(a, dc, NN)


_mm1.defvjp(_mm1_fwd, _mm1_bwd)


def _dn_state(u, kcum, qk, qdec, kdec, egl, state):
    v_new = u - _mm1(kcum, state, NN)
    out = _mm1(qdec, state, NN) + _mm1(qk, v_new, NN)
    return out, state * egl + _mm1(kdec, v_new, TN)


def _dn_pre_q(c):
    return _l2(_silu(c)) * DN_SCALE


def _dn_pre_k(c):
    return _l2(_silu(c))


def _dn_post(o, z, ng):
    return _rms(o, ng) * _silu(z)


def _dn_specs(t):
    cblk = lambda o: pl.BlockSpec((t, DN_DH), functools.partial(lambda h, o: (0, o + h), o=o // DN_DH))
    wblk = lambda o: pl.BlockSpec((4, DN_DH), functools.partial(lambda h, o: (0, o + h), o=o // DN_DH))
    proj_specs = [cblk(C_DQ), cblk(C_DK), cblk(C_DV), cblk(C_DZ)]
    w_specs = [wblk(0), wblk(BW), wblk(2 * BW)]
    gb_spec = pl.BlockSpec((1, t, 2), lambda h: (h, 0, 0))
    return proj_specs, w_specs, gb_spec


def _chunk_rows(n, count=1):
    return pl.ds(pl.multiple_of(n * DN_CHUNK, DN_CHUNK), count * DN_CHUNK)


def _egl_rows(n, count=1):
    return pl.ds(pl.multiple_of(n * 8, 8), count * 8)


def _dn_local_inputs(n, qn_sc, kn_sc, vv_sc, gb_ref):
    r = _chunk_rows(n, DN_BATCH)
    split = lambda v: v.reshape(DN_BATCH, DN_CHUNK, v.shape[-1])
    gbv = split(gb_ref[0, r, :])
    return split(qn_sc[r, :]), split(kn_sc[r, :]), split(vv_sc[r, :]), gbv[:, :, 0:1], gbv[:, :, 1:2]


def _dn_local_phase(nc, qn_sc, kn_sc, vv_sc, gb_ref, loc):
    def step(i, carry):
        n = i * DN_BATCH
        vals = _dn_local(*_dn_local_inputs(n, qn_sc, kn_sc, vv_sc, gb_ref))
        for sc, val in zip(loc[:5], vals[:5]):
            sc[_chunk_rows(n, DN_BATCH), :] = val.reshape(DN_BATCH * DN_CHUNK, val.shape[-1])
        loc[5][_egl_rows(n, DN_BATCH), :] = jnp.broadcast_to(vals[5], (DN_BATCH, 8, DN_DH)).reshape(DN_BATCH * 8, DN_DH)
        return carry

    lax.fori_loop(0, nc // DN_BATCH, step, 0)


def _dn_loc_scratch(t, nc):
    big = pltpu.VMEM((t, DN_DH), F32)
    return [big, big, pltpu.VMEM((t, DN_CHUNK), F32), big, big, pltpu.VMEM((nc * 8, DN_DH), F32)]


def _dn_fwd(proj, conv_w, gb, ng, name):
    t = proj.shape[0]
    nc = t // DN_CHUNK
    assert nc % DN_BATCH == 0
    proj_specs, w_specs, gb_spec = _dn_specs(t)

    def body(q_ref, k_ref, v_ref, z_ref, wq_ref, wk_ref, wv_ref, gb_ref, ng_ref, y_ref, o_ref, st_ref, qn_sc, kn_sc, vv_sc, *loc):
        qn_sc[...] = _dn_pre_q(_conv(q_ref[...], wq_ref[...]))
        kn_sc[...] = _dn_pre_k(_conv(k_ref[...], wk_ref[...]))
        vv_sc[...] = _silu(_conv(v_ref[...], wv_ref[...]))
        _dn_local_phase(nc, qn_sc, kn_sc, vv_sc, gb_ref, loc)
        u_sc, kcum_sc, qk_sc, qdec_sc, kdec_sc, egl_sc = loc

        def chunk(n, state):
            r = _chunk_rows(n)
            egl = egl_sc[_egl_rows(n), :][0:1]
            out, new_state = _dn_state(u_sc[r, :], kcum_sc[r, :], qk_sc[r, :], qdec_sc[r, :], kdec_sc[r, :], egl, state)
            st_ref[0, n] = state
            o_ref[r, :] = out
            return new_state

        lax.fori_loop(0, nc, chunk, jnp.zeros((DN_DH, DN_DH), F32))
        y_ref[...] = _dn_post(o_ref[...], z_ref[...], ng_ref[...])

    hblk = pl.BlockSpec((t, DN_DH), lambda h: (0, h))
    return pl.pallas_call(
        body, grid=(DN_HEADS,),
        in_specs=proj_specs + w_specs + [gb_spec, pl.BlockSpec((1, DN_DH), lambda h: (0, 0))],
        out_specs=[hblk, hblk, pl.BlockSpec((1, nc, DN_DH, DN_DH), lambda h: (h, 0, 0, 0))],
        out_shape=[jax.ShapeDtypeStruct((t, BW), F32), jax.ShapeDtypeStruct((t, BW), F32),
                   jax.ShapeDtypeStruct((DN_HEADS, nc, DN_DH, DN_DH), F32)],
        scratch_shapes=[pltpu.VMEM((t, DN_DH), F32)] * 3 + _dn_loc_scratch(t, nc), name=name,
    )(proj, proj, proj, proj, conv_w, conv_w, conv_w, gb, ng)


def _dn_bwd(proj, conv_w, gb, ng, o, states, dy, name):
    t = proj.shape[0]
    nc = t // DN_CHUNK
    proj_specs, w_specs, gb_spec = _dn_specs(t)
    nloc = 6

    def body(q_ref, k_ref, v_ref, z_ref, wq_ref, wk_ref, wv_ref, gb_ref, ng_ref, o_ref, st_ref, dy_ref,
             dq_ref, dk_ref, dv_ref, dz_ref, dwq_ref, dwk_ref, dwv_ref, dgb_ref, dng_ref,
             qn_sc, kn_sc, vv_sc, do_sc, *rest):
        loc, dloc = rest[:nloc], rest[nloc:]
        qn_sc[...] = _dn_pre_q(_conv(q_ref[...], wq_ref[...]))
        kn_sc[...] = _dn_pre_k(_conv(k_ref[...], wk_ref[...]))
        vv_sc[...] = _silu(_conv(v_ref[...], wv_ref[...]))
        _, vjp_y = jax.vjp(_dn_post, o_ref[...], z_ref[...], ng_ref[...])
        do, dz, dng = vjp_y(dy_ref[...])
        do_sc[...] = do
        dz_ref[...] = dz.astype(dz_ref.dtype)
        dng_ref[0] = dng
        _dn_local_phase(nc, qn_sc, kn_sc, vv_sc, gb_ref, loc)
        u_sc, kcum_sc, qk_sc, qdec_sc, kdec_sc, egl_sc = loc

        def state_bwd(i, dstate):
            n = nc - 1 - i
            r = _chunk_rows(n)
            r8 = _egl_rows(n)
            _, vjp = jax.vjp(_dn_state, u_sc[r, :], kcum_sc[r, :], qk_sc[r, :], qdec_sc[r, :], kdec_sc[r, :],
                             egl_sc[r8, :][0:1], st_ref[0, n])
            du, dkcum, dqk, dqdec, dkdec, degl, dprev = vjp((do_sc[r, :], dstate))
            for d_sc, val in zip(dloc[:5], (du, dkcum, dqk, dqdec, dkdec)):
                d_sc[r, :] = val
            dloc[5][r8, :] = jnp.broadcast_to(degl, (8, DN_DH))
            return dprev

        lax.fori_loop(0, nc, state_bwd, jnp.zeros((DN_DH, DN_DH), F32))

        def local_bwd(i, carry):
            n = i * DN_BATCH
            r = _chunk_rows(n, DN_BATCH)
            _, vjp = jax.vjp(_dn_local, *_dn_local_inputs(n, qn_sc, kn_sc, vv_sc, gb_ref))
            cts = tuple(d_sc[r, :].reshape(DN_BATCH, DN_CHUNK, d_sc.shape[-1]) for d_sc in dloc[:5])
            cts += (dloc[5][_egl_rows(n, DN_BATCH), :].reshape(DN_BATCH, 8, DN_DH)[:, 0:1],)
            dqc, dkc, dvc, dg, dbeta = vjp(cts)
            for d_sc, val in zip((dloc[0], dloc[1], dloc[3]), (dqc, dkc, dvc)):
                d_sc[r, :] = val.reshape(DN_BATCH * DN_CHUNK, DN_DH)
            dgb_ref[0, r, :] = jnp.concatenate([dg, dbeta], axis=2).reshape(DN_BATCH * DN_CHUNK, 2)
            return carry

        lax.fori_loop(0, nc // DN_BATCH, local_bwd, 0)
        for x_ref, w_ref, pre, d_sc, dx_ref, dw_ref in ((q_ref, wq_ref, _dn_pre_q, dloc[0], dq_ref, dwq_ref),
                                                       (k_ref, wk_ref, _dn_pre_k, dloc[1], dk_ref, dwk_ref),
                                                       (v_ref, wv_ref, _silu, dloc[3], dv_ref, dwv_ref)):
            _, vjp = jax.vjp(pre, _conv(x_ref[...], w_ref[...]))
            (dc,) = vjp(d_sc[...])
            dx, dws = _conv_bwd(x_ref[...], w_ref[...], dc)
            dx_ref[...] = dx.astype(dx_ref.dtype)
            for j, dw in enumerate(dws):
                dw_ref[j:j + 1, :] = dw

    hblk = pl.BlockSpec((t, DN_DH), lambda h: (0, h))
    wout = pl.BlockSpec((4, DN_DH), lambda h: (0, h))
    return pl.pallas_call(
        body, grid=(DN_HEADS,),
        in_specs=proj_specs + w_specs + [gb_spec, pl.BlockSpec((1, DN_DH), lambda h: (0, 0)), hblk,
                                         pl.BlockSpec((1, nc, DN_DH, DN_DH), lambda h: (h, 0, 0, 0)), hblk],
        out_specs=[hblk] * 4 + [wout] * 3 + [gb_spec, pl.BlockSpec((1, 1, DN_DH), lambda h: (h, 0, 0))],
        out_shape=[jax.ShapeDtypeStruct((t, BW), BF16)] * 4 + [jax.ShapeDtypeStruct((4, BW), F32)] * 3
        + [jax.ShapeDtypeStruct((DN_HEADS, t, 2), F32), jax.ShapeDtypeStruct((DN_HEADS, 1, DN_DH), F32)],
        scratch_shapes=[pltpu.VMEM((t, DN_DH), F32)] * 4 + _dn_loc_scratch(t, nc) * 2, name=name,
    )(proj, proj, proj, proj, conv_w, conv_w, conv_w, gb, ng, o, states, dy)


MERGE_TM, MERGE_TN = 512, 512


def _merge_specs(t):
    tm, tn = min(MERGE_TM, t), MERGE_TN
    y_spec = pl.BlockSpec((tm, BW), lambda i, j: (i, 0))
    w_spec = pl.BlockSpec((3, BW, tn), lambda i, j: (0, 0, j))
    gate_specs = [pl.BlockSpec((tm, tn), functools.partial(lambda i, j, o: (i, o + j), o=(C_GATE + n * D_MODEL) // tn))
                  for n in range(3)]
    return tm, tn, [y_spec] * 3 + [w_spec] + gate_specs


def _merge_fwd(ys, wb, proj, name):
    t = proj.shape[0]
    tm, tn, in_specs = _merge_specs(t)

    def body(y0, y1, y2, w_ref, g0, g1, g2, o_ref):
        acc = jnp.zeros((tm, tn), F32)
        for n, (y, g) in enumerate(((y0, g0), (y1, g1), (y2, g2))):
            acc = acc + _dot(y[...].astype(BF16), w_ref[n], NN) * _sigmoid(g[...])
        o_ref[...] = acc.astype(o_ref.dtype)

    return pl.pallas_call(
        body, grid=(t // tm, D_MODEL // tn), in_specs=in_specs,
        out_specs=pl.BlockSpec((tm, tn), lambda i, j: (i, j)),
        out_shape=jax.ShapeDtypeStruct((t, D_MODEL), BF16), name=name,
    )(*ys, wb, proj, proj, proj)


def _merge_bwd(ys, wb, proj, dmerged, name):
    t = proj.shape[0]
    tm, tn, in_specs = _merge_specs(t)

    def body(y0, y1, y2, w_ref, g0, g1, g2, dm_ref, dg_ref, dt_ref):
        dm = dm_ref[...]
        for n, (y, g) in enumerate(((y0, g0), (y1, g1), (y2, g2))):
            tn_ = _dot(y[...].astype(BF16), w_ref[n], NN)
            sg = _sigmoid(g[...])
            dg_ref[n] = (dm * tn_ * sg * (1.0 - sg)).astype(dg_ref.dtype)
            dt_ref[n] = (dm * sg).astype(dt_ref.dtype)

    o3 = pl.BlockSpec((3, tm, tn), lambda i, j: (0, i, j))
    return pl.pallas_call(
        body, grid=(t // tm, D_MODEL // tn), in_specs=in_specs + [pl.BlockSpec((tm, tn), lambda i, j: (i, j))],
        out_specs=[o3, o3], out_shape=[jax.ShapeDtypeStruct((3, t, D_MODEL), BF16)] * 2, name=name,
    )(*ys, wb, proj, proj, proj, dmerged)


def _prep_rows(sp):
    z4, z112 = jnp.zeros((1, 4), F32), jnp.zeros((1, 112), F32)
    bias_row = jnp.concatenate([sp['b_fox_f'], z4, sp['dn_dt_bias'], z112], axis=1)
    alog_row = jnp.concatenate([jnp.zeros((1, 12), F32), sp['dn_a_log'], z112], axis=1)
    return bias_row, alog_row


def _sc_fwd_tile(sb, sc, sv, w):
    return (sb * _conv(sc * sv, w),)


def _ffn_act(ug, uv):
    return _silu(ug) * uv


W_IN_COLS = [(C_FQ, 0, 1536), (C_SB, 1544, 1536), (C_DQ, 3080, 1536), (C_DZ, 4624, 512), (C_GATE, 5136, 3072),
             (C_SMALL, 1536, 8), (C_SMALL + 8, 4616, 8)]
W_IN_SHARD = IN_ORIG // N_DEV


def _w_in_segments():
    out = []
    for d, o, w in W_IN_COLS:
        end = o + w
        while o < end:
            k = o // W_IN_SHARD
            n = min(end, (k + 1) * W_IN_SHARD) - o
            out.append((d, k, o - k * W_IN_SHARD, n))
            o, d = o + n, d + n
    return out


def _w_in_assemble(g, name):
    tm = 256

    def body(g_ref, o_ref):
        for d, k, s, n in _w_in_segments():
            o_ref[:, d:d + n] = g_ref[k, :, s:s + n]
        o_ref[:, IN_ORIG:IN_P] = jnp.zeros((tm, IN_P - IN_ORIG), o_ref.dtype)

    return pl.pallas_call(
        body, grid=(D_MODEL // tm,),
        in_specs=[pl.BlockSpec((N_DEV, tm, W_IN_SHARD), lambda i: (0, i, 0))],
        out_specs=pl.BlockSpec((tm, IN_P), lambda i: (i, 0)),
        out_shape=jax.ShapeDtypeStruct((D_MODEL, IN_P), g.dtype), name=name,
    )(g)


W_IN_SPLIT_TM = 128


def _w_in_split_by_device(g, name):
    tm = W_IN_SPLIT_TM

    def body(g_ref, own_ref, send_ref):
        me = 4 * lax.axis_index("x") + 2 * lax.axis_index("y") + lax.axis_index("c")
        for d, k, s, n in _w_in_segments():
            val = g_ref[:, d:d + n]
            send_ref[k, :, s:s + n] = val.astype(send_ref.dtype)

            @pl.when(me == k)
            def _():
                own_ref[:, s:s + n] = val

    return pl.pallas_call(
        body, grid=(D_MODEL // tm,), in_specs=[pl.BlockSpec((tm, IN_P), lambda i: (i, 0))],
        out_specs=[pl.BlockSpec((tm, W_IN_SHARD), lambda i: (i, 0)), pl.BlockSpec((N_DEV, tm, W_IN_SHARD), lambda i: (0, i, 0))],
        out_shape=[jax.ShapeDtypeStruct((D_MODEL, W_IN_SHARD), F32), jax.ShapeDtypeStruct((N_DEV, D_MODEL, W_IN_SHARD), BF16)],
        name=name,
    )(g)


def _layer_fwd(x, p_i, w, sp, tag, late_weights=None):
    t = x.shape[0]
    sv = {'x': x}
    (hn,) = _ew(lambda a, g: (_rms(a, g),), [x], [sp['g_mix']], [(D_MODEL, BF16)], name=f'rms_mix_{tag}')
    proj = _mm(hn, w['w_in'], 'nn', [F32], name=f'in_proj_{tag}')
    bias_row, alog_row = _prep_rows(sp)
    aux = _prep_fwd(proj, bias_row, alog_row, f'prep_{tag}')
    cf = aux[:, :FOX_HEADS].T
    cfc, cfr = cf[:, :, None], cf[:, None, :]
    y_fox, lse = _att_fwd(proj, sp['fox_q_gain'], sp['fox_k_gain'], cfc, cfr, f'fox_fwd_{tag}')
    (y_sc,) = _cb(_sc_fwd_tile, [(proj, C_SB), (proj, C_SC), (proj, C_SV)], [(w['sc_conv_w'], 0)], [F32], [],
                  tc=256, nblk=2, name=f'sc_fwd_{tag}')
    gb = jnp.stack([aux[:, 12:16].T, aux[:, 8:12].T], axis=-1)
    y_dn, o_dn, states = _dn_fwd(proj, w['dn_conv_w'], gb, sp['dn_norm_gain'], f'dn_fwd_{tag}')
    ys = (y_fox, y_sc, y_dn)
    if late_weights is not None:
        w = {**w, **late_weights(y_dn)}
    merged = _merge_fwd(ys, w['w_branch'], proj, f'merge_fwd_{tag}')
    x1 = _mm(merged, w['w_o'], 'nn', [F32], epi=lambda acc, r: (acc + r,), epi_args=(x,), name=f'o_proj_{tag}')
    (hf,) = _ew(lambda a, g: (_rms(a, g),), [x1], [sp['g_ffn']], [(D_MODEL, BF16)], name=f'rms_ffn_{tag}')
    up = _mm(hf, w['w_up'], 'nt', [F32], name=f'up_proj_{tag}')
    (act,) = _cb(lambda ug, uv, wg, wv: (_ffn_act(_conv(ug, wg), _conv(uv, wv)),), [(up, 0), (up, D_FF)],
                 [(w['ffn_conv_w'], 0), (w['ffn_conv_w'], D_FF)], [BF16], [], tc=256, nblk=D_FF // 256, name=f'ffn_act_{tag}')
    x2 = _mm(act, w['w_down'], 'nn', [F32], epi=lambda acc, r: (acc + r,), epi_args=(x1,), name=f'down_proj_{tag}')
    (hp,) = _ew(lambda a, g: (_rms(a, g),), [x2], [sp['g_ple']], [(D_MODEL, BF16)], name=f'rms_ple_{tag}')
    gp = _mm(hp, w['w_ple_gate'], 'nn', [F32], name=f'ple_gate_{tag}')
    x3 = _mm(p_i, w['w_ple'], 'nn', [F32], epi=lambda acc, g, r: (r + _sigmoid(g) * acc,), epi_args=(gp, x2), name=f'ple_{tag}')
    sv.update(hn=hn, proj=proj, aux=aux, cfc=cfc, cfr=cfr, lse=lse, ys=ys, gb=gb, o_dn=o_dn,
              states=states, merged=merged, x1=x1, hf=hf, up=up, act=act, x2=x2, hp=hp, gp=gp, p=p_i,
              bias_row=bias_row, alog_row=alog_row, w=w)
    return x3, sv


def _rms_bwd(x, g, dh, dres, name):
    def fn(xv, dhv, dr, gv):
        _, vjp = jax.vjp(_rms, xv, gv)
        dx, dg = vjp(dhv)
        return dr + dx, dg
    return _ew(fn, [x, dh, dres], [g], [(D_MODEL, F32)], [(1, D_MODEL)], name=name)


def _layer_bwd(dx3, sv, sp, tag, early_grads=None):
    t = dx3.shape[0]
    w = sv['w']
    g = {}
    def ple_epi(acc, gpv, d):
        s = _sigmoid(gpv)
        return d * acc * s * (1.0 - s), d * s
    dgp, de = _mm(sv['p'], w['w_ple'], 'nn', [BF16, BF16], epi=ple_epi, epi_args=(sv['gp'], dx3), name=f'ple_bwd_{tag}')
    g['w_ple'] = _mm(sv['p'], de, 'tn', [F32], name=f'd_w_ple_{tag}')
    g['w_ple_gate'] = _mm(sv['hp'], dgp, 'tn', [F32], name=f'd_w_ple_gate_{tag}')
    dhp = _mm(dgp, w['w_ple_gate'], 'nt', [F32], name=f'd_hp_{tag}')
    dx2, g['g_ple'] = _rms_bwd(sv['x2'], sp['g_ple'], dhp, dx3, f'rms_ple_bwd_{tag}')
    dact = _mm(dx2, w['w_down'], 'nt', [F32], name=f'd_act_{tag}')
    g['w_down'] = _mm(sv['act'], dx2, 'tn', [F32], name=f'd_w_down_{tag}')

    def ffn_bwd_tile(ug, uv, da, wg, wv):
        cg, cv = _conv(ug, wg), _conv(uv, wv)
        _, vjp = jax.vjp(_ffn_act, cg, cv)
        dcg, dcv = vjp(da)
        dug, dwg = _conv_bwd(ug, wg, dcg)
        duv, dwv = _conv_bwd(uv, wv, dcv)
        return dug, duv, jnp.concatenate(dwg, axis=0), jnp.concatenate(dwv, axis=0)
    dupg, dupv, dwg, dwv = _cb(ffn_bwd_tile, [(sv['up'], 0), (sv['up'], D_FF), (dact, 0)],
                               [(w['ffn_conv_w'], 0), (w['ffn_conv_w'], D_FF)], [BF16, BF16], [3, 3], tc=256,
                               nblk=D_FF // 256, name=f'ffn_act_bwd_{tag}')
    dup = jnp.concatenate([dupg, dupv], axis=1)
    g['ffn_conv_w'] = jnp.concatenate([dwg, dwv], axis=1)
    g['w_up'] = _mm(dup, sv['hf'], 'tn', [F32], name=f'd_w_up_{tag}')
    dhf = _mm(dup, w['w_up'], 'nn', [F32], name=f'd_hf_{tag}')
    dx1, g['g_ffn'] = _rms_bwd(sv['x1'], sp['g_ffn'], dhf, dx2, f'rms_ffn_bwd_{tag}')
    dmerged = _mm(dx1, w['w_o'], 'nt', [F32], name=f'd_merged_{tag}')
    g['w_o'] = _mm(sv['merged'], dx1, 'tn', [F32], name=f'd_w_o_{tag}')
    dgate, dtn = _merge_bwd(sv['ys'], w['w_branch'], sv['proj'], dmerged, f'merge_bwd_{tag}')
    dys, dwb = [], []
    for n in range(3):
        dys.append(_mm(dtn[n], w['w_branch'][n], 'nt', [F32], name=f'd_y{n}_{tag}'))
        dwb.append(_mm(sv['ys'][n], dtn[n], 'tn', [F32], name=f'd_w_branch{n}_{tag}'))
    g['w_branch'] = jnp.stack(dwb)
    if early_grads is not None:
        sp = early_grads(g, sp)
    ddq, ddk, ddv, ddz, dwq, dwk, dwv_, dgb, dng = _dn_bwd(sv['proj'], w['dn_conv_w'], sv['gb'], sp['dn_norm_gain'],
                                                           sv['o_dn'], sv['states'], dys[2], f'dn_bwd_{tag}')
    g['dn_conv_w'] = jnp.concatenate([dwq, dwk, dwv_], axis=1)
    g['dn_norm_gain'] = jnp.sum(dng, axis=0)
    def sc_bwd_tile(sb, sc, svv, dy, wv):
        u = sc * svv
        dsb = dy * _conv(u, wv)
        du, dws = _conv_bwd(u, wv, dy * sb)
        return dsb, du * svv, du * sc, jnp.concatenate(dws, axis=0)
    dsb, dsc, dsv, g['sc_conv_w'] = _cb(sc_bwd_tile, [(sv['proj'], C_SB), (sv['proj'], C_SC), (sv['proj'], C_SV), (dys[1], 0)],
                                        [(w['sc_conv_w'], 0)], [BF16, BF16, BF16], [3], tc=256, nblk=2, name=f'sc_bwd_{tag}')
    dfq, dfk, dfv, dcfc, dcfr, dqg, dkg = _att_bwd(sv['proj'], sp['fox_q_gain'], sp['fox_k_gain'], sv['cfc'], sv['cfr'],
                                                   sv['lse'], sv['ys'][0], dys[0], f'fox_bwd_{tag}')
    g['fox_q_gain'] = jnp.sum(dqg, axis=0)
    g['fox_k_gain'] = jnp.sum(dkg, axis=0)
    dcf = (dcfc[:, :, 0] + dcfr[:, 0, :]).T
    daux = jnp.concatenate([dcf, dgb[:, :, 1].T, dgb[:, :, 0].T, jnp.zeros((t, 112), F32)], axis=1)
    dsmall, dbias, dalog = _prep_bwd(sv['proj'], sv['bias_row'], sv['alog_row'], daux, f'prep_bwd_{tag}')
    g['b_fox_f'] = dbias[:, 0:8]
    g['dn_dt_bias'] = dbias[:, 12:16]
    g['dn_a_log'] = dalog[:, 12:16]
    dproj = jnp.concatenate([dfq, dfk, dfv, dsb, dsc, dsv, ddq, ddk, ddv, ddz, dgate[0], dgate[1], dgate[2], dsmall], axis=1)
    g['w_in'] = _mm(sv['hn'], dproj, 'tn', [F32], name=f'd_w_in_{tag}')
    dhn = _mm(dproj, w['w_in'], 'nt', [F32], name=f'd_hn_{tag}')
    dx, g['g_mix'] = _rms_bwd(sv['x'], sp['g_mix'], dhn, dx1, f'rms_mix_bwd_{tag}')
    return dx, g


def _loss_bwd(y, target, name):
    inv = 1.0 / y.shape[1]

    def fn(yv, tv):
        err = yv - tv
        return err * inv, jnp.zeros((8, 128), F32) + 0.5 * inv * jnp.sum(err * err)
    return _ew(fn, [y, target], [], [(y.shape[1], F32)], [(8, 128)], name=name)


PACK_W = 1024
FULL_SHAPE = {'w_in': (D_MODEL, IN_ORIG), 'w_branch': (3, BW, D_MODEL), 'w_o': (D_MODEL, D_MODEL), 'w_up': (2 * D_FF, D_MODEL),
              'w_down': (D_FF, D_MODEL), 'w_ple_gate': (D_MODEL, D_MODEL), 'w_ple': (PLE_DIM, D_MODEL),
              'sc_conv_w': (3, BW), 'dn_conv_w': (4, 3 * BW), 'ffn_conv_w': (3, 2 * D_FF)}
SMALL_SHAPE = {'g_mix': D_MODEL, 'b_fox_f': FOX_HEADS, 'fox_q_gain': FOX_DH, 'fox_k_gain': FOX_DH, 'dn_a_log': DN_HEADS,
               'dn_dt_bias': DN_HEADS, 'dn_norm_gain': DN_DH, 'g_ffn': D_MODEL, 'g_ple': D_MODEL}


def _shard_shape(name):
    s = list(FULL_SHAPE[name])
    s[SHARDED[name]] //= N_DEV
    return tuple(s)


def _full_from_gathered(g, name):
    sh, ax = _shard_shape(name), SHARDED[name]
    blocks = jnp.moveaxis(g, 0, ax)
    return blocks.reshape(sh[:ax] + (N_DEV * sh[ax],) + sh[ax + 1:])


def _by_dest(full, name):
    sh, ax = _shard_shape(name), SHARDED[name]
    return jnp.moveaxis(full.reshape(sh[:ax] + (N_DEV, sh[ax]) + sh[ax + 1:]), ax, 0)


def _flat_pack(arrs):
    flat = jnp.concatenate([a.reshape(-1).astype(F32) for a in arrs])
    rows = -(-flat.shape[0] // (8 * PACK_W)) * 8
    return jnp.pad(flat, (0, rows * PACK_W - flat.shape[0])).reshape(rows, PACK_W)


def _flat_unpack(pack, shapes):
    flat, out, off = pack.reshape(-1), [], 0
    for s in shapes:
        n = 1
        for d in s:
            n *= d
        out.append(flat[off:off + n].reshape(s))
        off += n
    return out


SMALL_SHAPES = [(DEPTH, SMALL_SHAPE[n]) for n in SMALL]
SMALL_LOSS_AT = sum(DEPTH * SMALL_SHAPE[n] for n in SMALL)
CONV_SHARD_SHAPES = [(DEPTH,) + _shard_shape(n) for n in CONVW]
CONV_FULL_SHAPES = [(DEPTH,) + FULL_SHAPE[n] for n in CONVW]


def _pick_tm(m, width):
    best = None
    for tm in range(16, m + 1, 16):
        if m % tm == 0 and tm * width * 4 <= (1 << 20):
            best = tm
    return best if best is not None else m


HBM_SPEC = pl.BlockSpec(memory_space=pltpu.HBM)


def _place():
    x, y, c = lax.axis_index("x"), lax.axis_index("y"), lax.axis_index("c")
    return x, y, c, [(1 - x, y), (x, 1 - y), (1 - x, 1 - y)]


def _all_gather(arrs, name):
    n = len(arrs)

    def body(*refs):
        ins, outs = refs[:n], refs[n:2 * n]
        send_sems, recv_sems = refs[2 * n:]
        x, y, c, chips = _place()
        me, sibling = (x, y, c), (x, y, 1 - c)

        def block(a, p):
            return outs[a].at[4 * p[0] + 2 * p[1] + p[2]]

        def copy(a, k, blk, to, src=None):
            return pltpu.make_async_remote_copy(src_ref=block(a, blk) if src is None else src, dst_ref=block(a, blk),
                                                send_sem=send_sems.at[a, k], recv_sem=recv_sems.at[a, k],
                                                device_id=to, device_id_type=MESH)

        first, passed = [], []
        for a in range(n):
            first.append(copy(a, 0, me, sibling, src=ins[a]))
            first += [copy(a, 1 + j, me, (*chip, c), src=ins[a]) for j, chip in enumerate(chips)]
        for cp in first:
            cp.start()
        for j, chip in enumerate(chips):
            for a in range(n):
                copy(a, 1 + j, (*chip, c), me).wait_recv()
                fwd = copy(a, 4 + j, (*chip, c), sibling)
                fwd.start()
                passed.append(fwd)
        for a in range(n):
            copy(a, 0, sibling, me).wait_recv()
            for j, chip in enumerate(chips):
                copy(a, 4 + j, (*chip, 1 - c), me).wait_recv()
        for cp in first + passed:
            cp.wait_send()

    gathered = pl.pallas_call(
        body, in_specs=[HBM_SPEC] * n, out_specs=[HBM_SPEC] * n,
        out_shape=[jax.ShapeDtypeStruct((N_DEV,) + a.shape, a.dtype) for a in arrs],
        scratch_shapes=[pltpu.SemaphoreType.DMA((n, 7)), pltpu.SemaphoreType.DMA((n, 7))], name=name,
    )(*arrs)
    me = 4 * lax.axis_index("x") + 2 * lax.axis_index("y") + lax.axis_index("c")
    return [lax.dynamic_update_index_in_dim(g, a, me, 0) for g, a in zip(gathered, arrs)]


SEM_SPEC = pl.BlockSpec(memory_space=pltpu.SEMAPHORE)
DATAFLOW = pltpu.SideEffectType.DATAFLOW_SIDE_EFFECTING
ALL_PEERS = (1, 2, 3, 4, 5, 6, 7)
NEAR_PEERS = (1, 4, 2, 6)


def _direct_copies(src_refs, zone_refs, send_sems, recv_sems, per_dest, peers):
    x, y, c, _ = _place()
    me = 4 * x + 2 * y + c
    cps = []
    for a, (src, zone) in enumerate(zip(src_refs, zone_refs)):
        for r, bits in enumerate(peers):
            px = 1 - x if bits & 4 else x
            py = 1 - y if bits & 2 else y
            pc = 1 - c if bits & 1 else c
            cps.append(pltpu.make_async_remote_copy(
                src_ref=src.at[4 * px + 2 * py + pc] if per_dest else src, dst_ref=zone.at[me],
                send_sem=send_sems.at[a * len(peers) + r], recv_sem=recv_sems.at[a * len(peers) + r],
                device_id=(px, py, pc), device_id_type=MESH))
    return cps


def _direct_start(srcs, per_dest, after, name, peers=ALL_PEERS):
    n = len(srcs)
    zones = [lax.empty((N_DEV,) + (s.shape[1:] if per_dest else s.shape), s.dtype) for s in srcs]

    def body(*refs):
        src_refs, zone_refs = refs[:n], refs[n:2 * n]
        send_sems, recv_sems = refs[2 * n + 1:2 * n + 3]
        for cp in _direct_copies(src_refs, zone_refs, send_sems, recv_sems, per_dest, peers):
            cp.start()
        refs[-1][...] = jnp.zeros_like(refs[-1])

    sems = pltpu.SemaphoreType.DMA((n * len(peers),))
    res = pl.pallas_call(
        body, name=name, in_specs=[HBM_SPEC] * (2 * n + 1),
        out_shape=[sems, sems] + [pltpu.HBM(s.shape, s.dtype) for s in srcs] + [pltpu.HBM(z.shape, z.dtype) for z in zones]
        + [jax.ShapeDtypeStruct((8, 128), F32)],
        out_specs=[SEM_SPEC, SEM_SPEC] + [HBM_SPEC] * (2 * n) + [pl.BlockSpec(memory_space=pltpu.VMEM)],
        input_output_aliases={i: 2 + i for i in range(2 * n)},
        compiler_params=pltpu.CompilerParams(has_side_effects=DATAFLOW),
    )(*[pltpu.with_memory_space_constraint(s, pltpu.HBM) for s in srcs],
      *[pltpu.with_memory_space_constraint(z, pltpu.HBM) for z in zones], after)
    return (res[0], res[1], list(res[2:2 + n]), list(res[2 + n:2 + 2 * n]), per_dest, peers), res[-1][0:1, 0:1]


def _direct_wait(started, after, name):
    send_sems, recv_sems, srcs, zones, per_dest, peers = started
    n = len(srcs)

    def body(*refs):
        src_refs, zone_refs = refs[:n], refs[n:2 * n]
        for cp in _direct_copies(src_refs, zone_refs, refs[2 * n], refs[2 * n + 1], per_dest, peers):
            cp.wait_send()
            cp.wait_recv()

    res = pl.pallas_call(
        body, name=name, in_specs=[HBM_SPEC] * (2 * n) + [SEM_SPEC, SEM_SPEC, HBM_SPEC],
        out_shape=[pltpu.HBM(s.shape, s.dtype) for s in srcs] + [pltpu.HBM(z.shape, z.dtype) for z in zones],
        out_specs=[HBM_SPEC] * (2 * n), input_output_aliases={i: i for i in range(2 * n)},
        compiler_params=pltpu.CompilerParams(has_side_effects=DATAFLOW),
    )(*srcs, *zones, send_sems, recv_sems, after)
    return list(res[n:])


def _gather_finish(zones, name):
    n = len(zones)

    def body(*refs):
        zone_in, zone_out = refs[:n], refs[n:2 * n]
        send_sems, recv_sems = refs[2 * n:]
        x, y, c, chips = _place()
        cps = []
        for a in range(n):
            for j, (cx, cy) in enumerate(chips):
                blk = 4 * cx + 2 * cy + c
                cp = pltpu.make_async_remote_copy(src_ref=zone_in[a].at[blk], dst_ref=zone_out[a].at[blk],
                                                  send_sem=send_sems.at[a, j], recv_sem=recv_sems.at[a, j],
                                                  device_id=(x, y, 1 - c), device_id_type=MESH)
                cp.start()
                cps.append(cp)
        for cp in cps:
            cp.wait()

    return pl.pallas_call(
        body, in_specs=[HBM_SPEC] * n, out_specs=[HBM_SPEC] * n,
        out_shape=[jax.ShapeDtypeStruct(z.shape, z.dtype) for z in zones], input_output_aliases={a: a for a in range(n)},
        scratch_shapes=[pltpu.SemaphoreType.DMA((n, 3)), pltpu.SemaphoreType.DMA((n, 3))], name=name,
    )(*zones)


def _adamw(w, g, m, v):
    m = ADAM_B1 * m + (1.0 - ADAM_B1) * g
    v = ADAM_B2 * v + (1.0 - ADAM_B2) * jnp.square(g)
    m_hat = m / (1.0 - ADAM_B1 ** ADAM_STEP)
    v_hat = v / (1.0 - ADAM_B2 ** ADAM_STEP)
    delta = -ADAM_LR * (m_hat / (jnp.sqrt(v_hat) + ADAM_EPS) + ADAM_WD * w)
    return delta, m, v


def _adamw_layer(w, m, v, own, parts, layer, prev, name):
    _, rows, c = w.shape
    tm = _pick_tm(rows, c)

    def body(w_ref, m_ref, v_ref, own_ref, parts_ref, *rest):
        g = own_ref[...]
        for j in range(parts_ref.shape[0]):
            g = g + parts_ref[j].astype(F32)
        delta, m2, v2 = _adamw(w_ref[...], g, m_ref[...], v_ref[...])
        g_ref, d_ref, m2_ref, v2_ref = rest[-4:]
        g_ref[...], d_ref[...], m2_ref[...], v2_ref[...] = g, delta, m2, v2

    wspec = pl.BlockSpec((None, tm, c), lambda i: (layer, i, 0))
    prev = list(prev) if prev is not None else []
    return pl.pallas_call(
        body, grid=(rows // tm,),
        in_specs=[wspec] * 3 + [pl.BlockSpec((tm, c), lambda i: (i, 0)), pl.BlockSpec((parts.shape[0], tm, c), lambda i: (0, i, 0))]
        + [HBM_SPEC] * len(prev),
        out_specs=[wspec] * 4, out_shape=[jax.ShapeDtypeStruct(w.shape, F32)] * 4,
        input_output_aliases={5 + k: k for k in range(len(prev))}, name=name,
    )(w, m, v, own, parts, *prev)


def _adamw_call(w, m, v, gparts, name):
    def fn(wv, mv, vv, *gs):
        g = gs[0].astype(F32)
        for gp in gs[1:]:
            g = g + gp.astype(F32)
        delta, m2, v2 = _adamw(wv, g, mv, vv)
        return g, delta, m2, v2
    rows, c = w.shape
    return _ew(fn, [w, m, v] + list(gparts), [], [(c, F32)] * 4, tm=_pick_tm(rows, c), name=name)


def kernel(x, p, g_mix, w_in, b_fox_f, fox_q_gain, fox_k_gain, sc_conv_w, dn_conv_w, dn_a_log, dn_dt_bias,
           dn_norm_gain, w_branch, w_o, g_ffn, w_up, ffn_conv_w, w_down, g_ple, w_ple_gate, w_ple, loss_target,
           m_g_mix, m_w_in, m_b_fox_f, m_fox_q_gain, m_fox_k_gain, m_sc_conv_w, m_dn_conv_w, m_dn_a_log,
           m_dn_dt_bias, m_dn_norm_gain, m_w_branch, m_w_o, m_g_ffn, m_w_up, m_ffn_conv_w, m_w_down, m_g_ple,
           m_w_ple_gate, m_w_ple, v_g_mix, v_w_in, v_b_fox_f, v_fox_q_gain, v_fox_k_gain, v_sc_conv_w, v_dn_conv_w,
           v_dn_a_log, v_dn_dt_bias, v_dn_norm_gain, v_w_branch, v_w_o, v_g_ffn, v_w_up, v_ffn_conv_w, v_w_down,
           v_g_ple, v_w_ple_gate, v_w_ple):
    return _step(x, p, g_mix, w_in, b_fox_f, fox_q_gain, fox_k_gain, sc_conv_w, dn_conv_w, dn_a_log, dn_dt_bias,
                 dn_norm_gain, w_branch, w_o, g_ffn, w_up, ffn_conv_w, w_down, g_ple, w_ple_gate, w_ple, loss_target,
                 m_g_mix, m_w_in, m_b_fox_f, m_fox_q_gain, m_fox_k_gain, m_sc_conv_w, m_dn_conv_w, m_dn_a_log,
                 m_dn_dt_bias, m_dn_norm_gain, m_w_branch, m_w_o, m_g_ffn, m_w_up, m_ffn_conv_w, m_w_down, m_g_ple,
                 m_w_ple_gate, m_w_ple, v_g_mix, v_w_in, v_b_fox_f, v_fox_q_gain, v_fox_k_gain, v_sc_conv_w,
                 v_dn_conv_w, v_dn_a_log, v_dn_dt_bias, v_dn_norm_gain, v_w_branch, v_w_o, v_g_ffn, v_w_up,
                 v_ffn_conv_w, v_w_down, v_g_ple, v_w_ple_gate, v_w_ple)


def _step(*args):
    names = ['x', 'p'] + WEIGHTS + ['loss_target'] + ['m_' + n for n in WEIGHTS] + ['v_' + n for n in WEIGHTS]
    assert len(args) == len(names)
    a = dict(zip(names, args))
    for n, perm in HELD_TRANSPOSED.items():
        for k in (n, 'm_' + n, 'v_' + n):
            a[k] = jnp.transpose(a[k], perm)
    x, target = a['x'][0], a['loss_target'][0]
    p = a['p'][:, 0]
    dev = 4 * lax.axis_index("x") + 2 * lax.axis_index("y") + lax.axis_index("c")

    LATE = [n for n in BIG if n != 'w_in']
    sps = [{n: a[n][layer][None, :] for n in SMALL} for layer in range(DEPTH)]
    shards = [{n: a[n][layer].astype(BF16) for n in BIG} for layer in range(DEPTH)]

    def by_device(full, n):
        by_dest = _by_dest(full, n)
        return lax.dynamic_index_in_dim(by_dest, dev, axis=0, keepdims=False), by_dest.astype(BF16)

    def blank(zone):
        return lax.dynamic_update_index_in_dim(zone, jnp.zeros(zone.shape[1:], zone.dtype), dev, 0)

    w_in_all, conv_all = _all_gather([shards[0]['w_in'], _flat_pack([a[n] for n in CONVW])], 'gather_w_in_l0')
    conv_by_dev = [_flat_unpack(conv_all[d], CONV_SHARD_SHAPES) for d in range(N_DEV)]
    conv_full = {n: jnp.concatenate([conv_by_dev[d][i] for d in range(N_DEV)], axis=2) for i, n in enumerate(CONVW)}
    late0_started, zero = _direct_start([shards[0][n] for n in LATE], False, w_in_all, 'weights_rest_l0_start', NEAR_PEERS)
    w1_started, zero = _direct_start([shards[1][n] for n in BIG], False, late0_started[2][0], 'weights_l1_start', NEAR_PEERS)
    sps[0]['g_mix'] = sps[0]['g_mix'] + zero

    def late_weights0(after):
        zones = _gather_finish(_direct_wait(late0_started, after, 'weights_rest_l0_wait'), 'weights_rest_l0_finish')
        return {n: _full_from_gathered(lax.dynamic_update_index_in_dim(z, shards[0][n], dev, 0), n) for n, z in zip(LATE, zones)}

    w0 = {n: conv_full[n][0] for n in CONVW}
    w0['w_in'] = _w_in_assemble(w_in_all, 'w_in_assemble_l0')
    h, sv0 = _layer_fwd(x, p[0], w0, sps[0], 'l0', late_weights0)
    zones = [lax.dynamic_update_index_in_dim(z, shards[1][n], dev, 0)
             for n, z in zip(BIG, _gather_finish(_direct_wait(w1_started, h, 'weights_l1_wait'), 'weights_l1_finish'))]
    w1 = {n: _full_from_gathered(z, n) for n, z in zip(BIG, zones) if n != 'w_in'}
    w1.update({n: conv_full[n][1] for n in CONVW})
    w1['w_in'] = _w_in_assemble(zones[list(BIG).index('w_in')], 'w_in_assemble_l1')
    h, sv1 = _layer_fwd(h, p[1], w1, sps[1], 'l1')
    dh, loss_part = _loss_bwd(h, target, 'loss')
    grads = [None] * DEPTH
    dh, grads[1] = _layer_bwd(dh, sv1, sps[1], 'l1')
    owns1, sends1 = {}, {}
    for n in BIG:
        owns1[n], sends1[n] = _w_in_split_by_device(grads[1][n], 'w_in_grad_split_l1') if n == 'w_in' else by_device(grads[1][n], n)
    g1_started, zero = _direct_start([sends1[n] for n in BIG], True, dh, 'grads_l1_start')
    early = {}

    def early_grads0(g, sp):
        owns, sends = zip(*[by_device(g[n], n) for n in LATE])
        early['started'], zero = _direct_start(list(sends), True, g['w_branch'], 'grads_rest_l0_start')
        early['owns'] = dict(zip(LATE, owns))
        return {**sp, 'dn_norm_gain': sp['dn_norm_gain'] + zero}

    dh, grads[0] = _layer_bwd(dh, sv0, {**sps[0], 'g_ple': sps[0]['g_ple'] + zero}, 'l0', early_grads0)
    grad_x = dh[None]

    own_in0, send_in0 = _w_in_split_by_device(grads[0]['w_in'], 'w_in_grad_split_l0')
    in0_started, zero = _direct_start([send_in0], True, dh, 'grads_w_in_l0_start')
    sent = in0_started[2][0]
    zones1 = dict(zip(BIG, _direct_wait(g1_started, sent, 'grads_l1_wait')))
    zones0 = dict(zip(LATE, _direct_wait(early['started'], sent, 'grads_rest_l0_wait')))

    def adamw(n, layer, own, zone, prev):
        rows = (-1, a[n].shape[-1])
        own = own.reshape(rows)
        view = lambda t: t.reshape((DEPTH,) + own.shape)
        return _adamw_layer(view(a[n]), view(a['m_' + n]), view(a['v_' + n]), own, blank(zone).reshape((N_DEV,) + own.shape),
                            layer, prev, f'adamw_{n}_l{layer}')

    out, done = {}, []
    for n in LATE:
        res = adamw(n, 0, early['owns'][n], zones0[n], adamw(n, 1, owns1[n], zones1[n], None))
        out[n] = [r.reshape(a[n].shape) for r in res]
        done.append(res[0][0, 0:1, 0:1])
    res = adamw('w_in', 1, owns1['w_in'], zones1['w_in'], None)
    done.append(res[0][1, 0:1, 0:1])
    (zone_in0,) = _direct_wait(in0_started, jnp.concatenate(done, axis=1), 'grads_w_in_l0_wait')
    out['w_in'] = [r.reshape(a['w_in'].shape) for r in adamw('w_in', 0, own_in0, zone_in0, res)]

    small_pack = _flat_pack([jnp.stack([grads[layer][n].reshape(-1) for layer in range(DEPTH)]) for n in SMALL] + [loss_part[0, 0]])
    conv_pack = _flat_pack([jnp.stack([grads[layer][n] for layer in range(DEPTH)]) for n in CONVW])
    small_all, conv_all = _all_gather([small_pack, conv_pack], 'gather_small_grads')
    res = _adamw_call(_flat_pack([a[n] for n in SMALL]), _flat_pack([a['m_' + n] for n in SMALL]),
                      _flat_pack([a['v_' + n] for n in SMALL]), [small_all[d] for d in range(N_DEV)], 'adamw_replicated')
    loss = res[0].reshape(-1)[SMALL_LOSS_AT]
    for k, r in enumerate(res):
        for n, val in zip(SMALL, _flat_unpack(r, SMALL_SHAPES)):
            out.setdefault(n, [None] * 4)[k] = val
    (conv_sum,) = _ew(lambda *gs: (functools.reduce(lambda s, t: s + t, gs),), [conv_all[d] for d in range(N_DEV)], [],
                      [(PACK_W, F32)], tm=conv_pack.shape[0], name='conv_grad_sum')
    conv_own = [lax.dynamic_slice_in_dim(g, dev * (g.shape[2] // N_DEV), g.shape[2] // N_DEV, axis=2)
                for g in _flat_unpack(conv_sum, CONV_FULL_SHAPES)]
    res = _adamw_call(_flat_pack([a[n] for n in CONVW]), _flat_pack([a['m_' + n] for n in CONVW]),
                      _flat_pack([a['v_' + n] for n in CONVW]), [_flat_pack(conv_own)], 'adamw_conv')
    for k, r in enumerate(res):
        for n, val in zip(CONVW, _flat_unpack(r, CONV_SHARD_SHAPES)):
            out.setdefault(n, [None] * 4)[k] = val

    for n, perm in HELD_TRANSPOSED.items():
        out[n] = [jnp.transpose(r, perm) for r in out[n]]
    outs = [loss, grad_x]
    for k in range(4):
        outs += [out[n][k] for n in WEIGHTS]
    return tuple(outs)
```

```python
import functools

import jax
import jax.numpy as jnp
from jax import lax
from jax.experimental import pallas as pl
from jax.experimental.pallas import tpu as pltpu

F32 = jnp.float32
BF16 = jnp.bfloat16
HI = lax.Precision.HIGHEST

D_MODEL = 1024
DEPTH = 2
N_DEV = 8
PLE_DIM = 256
BW = 512
FOX_HEADS, FOX_DH = 8, 64
DN_HEADS, DN_DH = 4, 128
DN_CHUNK = 64
D_FF = 2816
EPS = 1e-6
NEG = -1e30

ADAM_LR, ADAM_B1, ADAM_B2, ADAM_EPS, ADAM_WD, ADAM_STEP = 0.001, 0.9, 0.999, 1e-08, 0.01, 10

C_FQ, C_FK, C_FV = 0, 512, 1024
C_SB, C_SC, C_SV = 1536, 2048, 2560
C_DQ, C_DK, C_DV, C_DZ = 3072, 3584, 4096, 4608
C_GATE = 5120
C_SMALL = 8192
IN_P = 8320
IN_ORIG = 8208

WEIGHTS = ['g_mix', 'w_in', 'b_fox_f', 'fox_q_gain', 'fox_k_gain', 'sc_conv_w', 'dn_conv_w', 'dn_a_log',
           'dn_dt_bias', 'dn_norm_gain', 'w_branch', 'w_o', 'g_ffn', 'w_up', 'ffn_conv_w', 'w_down', 'g_ple',
           'w_ple_gate', 'w_ple']
BIG = {'w_in': 1, 'w_branch': 2, 'w_o': 0, 'w_up': 0, 'w_down': 0, 'w_ple_gate': 0, 'w_ple': 1}
HELD_TRANSPOSED = {'w_up': (0, 2, 1)}
CONVW = {'sc_conv_w': 1, 'dn_conv_w': 1, 'ffn_conv_w': 1}
SHARDED = {**BIG, **CONVW}
SMALL = [n for n in WEIGHTS if n not in SHARDED]
MESH = pl.DeviceIdType.MESH


def _sigmoid(x):
    return 0.5 * (jnp.tanh(0.5 * x) + 1.0)


def _silu(x):
    return x * _sigmoid(x)


def _log1pexp_negabs(z):
    return jnp.log(1.0 + jnp.exp(-jnp.abs(z)))


def _log_sigmoid(z):
    return jnp.minimum(z, 0.0) - _log1pexp_negabs(z)


def _softplus(z):
    return jnp.maximum(z, 0.0) + _log1pexp_negabs(z)


def _rms(x, g):
    return x * lax.rsqrt(jnp.mean(x * x, axis=-1, keepdims=True) + EPS) * g


def _l2(x):
    return x * lax.rsqrt(jnp.sum(x * x, axis=-1, keepdims=True) + EPS)


def _dot(a, b, dims, precision=None):
    return lax.dot_general(a, b, (dims, ((), ())), preferred_element_type=F32, precision=precision)


NN = ((1,), (0,))
NT = ((1,), (1,))
TN = ((0,), (0,))


def _shift_down(x, s):
    if s == 0:
        return x
    t = lax.broadcasted_iota(jnp.int32, x.shape, 0)
    return jnp.where(t >= s, pltpu.roll(x, s, 0), 0.0)


def _shift_up(x, s):
    if s == 0:
        return x
    n = x.shape[0]
    t = lax.broadcasted_iota(jnp.int32, x.shape, 0)
    return jnp.where(t < n - s, pltpu.roll(x, n - s, 0), 0.0)


def _conv(x, w):
    k = w.shape[0]
    y = w[k - 1:k] * x
    for j in range(k - 1):
        y = y + w[j:j + 1] * _shift_down(x, k - 1 - j)
    return y


def _conv_bwd(x, w, dy):
    k = w.shape[0]
    dx = w[k - 1:k] * dy
    dws = []
    for j in range(k - 1):
        dx = dx + w[j:j + 1] * _shift_up(dy, k - 1 - j)
        dws.append(jnp.sum(dy * _shift_down(x, k - 1 - j), axis=0, keepdims=True))
    dws.append(jnp.sum(dy * x, axis=0, keepdims=True))
    return dx, dws


MM_VMEM_BUDGET = 36 << 20
MM_STEP_BYTES = 1 << 20


def _mm_tiles(m, n, k, a_size, b_size, tile_size, cast_a):
    best = None
    for tm in [d for d in (2048, 1024, 512, 256, 128) if d <= m and m % d == 0] or [m]:
        for tn in [d for d in range(128, min(n, 2048) + 1, 128) if n % d == 0] or [n]:
            vmem = 2 * tm * k * a_size + (2 * tm * k if cast_a else 0) + 2 * k * tn * b_size + 2 * tm * tn * tile_size
            if vmem > MM_VMEM_BUDGET:
                continue
            steps = (m // tm) * (n // tn)
            cost = m * k * a_size + (m // tm) * k * n * b_size + m * n * tile_size + steps * MM_STEP_BYTES
            if best is None or cost < best[0]:
                best = (cost, tm, tn)
    assert best is not None, (m, n, k)
    return best[1], best[2]


def _mm(a, b, mode, outs, *, epi=None, epi_args=(), name):
    if mode == 'nn':
        (m, k), (k2, n) = a.shape, b.shape
    elif mode == 'nt':
        (m, k), (n, k2) = a.shape, b.shape
    else:
        (k, m), (k2, n) = a.shape, b.shape
    assert k == k2, (a.shape, b.shape, mode)
    tile_size = sum(jnp.dtype(dt).itemsize for dt in outs) + sum(e.dtype.itemsize for e in epi_args if e.shape[0] != 1)
    tm, tn = _mm_tiles(m, n, k, a.dtype.itemsize, b.dtype.itemsize, tile_size, a.dtype != BF16)
    dims = {'nn': NN, 'nt': NT, 'tn': TN}[mode]
    a_spec = pl.BlockSpec((k, tm), lambda i, j: (0, i)) if mode == 'tn' else pl.BlockSpec((tm, k), lambda i, j: (i, 0))
    b_spec = pl.BlockSpec((tn, k), lambda i, j: (j, 0)) if mode == 'nt' else pl.BlockSpec((k, tn), lambda i, j: (0, j))
    e_specs = [pl.BlockSpec((1, tn), lambda i, j: (0, j)) if e.shape[0] == 1 else pl.BlockSpec((tm, tn), lambda i, j: (i, j))
               for e in epi_args]
    ne, no = len(epi_args), len(outs)
    cast_a = a.dtype != BF16

    def body(a_ref, b_ref, *rest):
        if cast_a:
            a_sc = rest[-1]

            @pl.when(pl.program_id(1) == 0)
            def _():
                a_sc[...] = a_ref[...].astype(BF16)
            av = a_sc[...]
        else:
            av = a_ref[...]
        acc = _dot(av, b_ref[...].astype(BF16), dims)
        vals = epi(acc, *[e[...] for e in rest[:ne]]) if epi is not None else (acc,)
        for o_ref, v in zip(rest[ne:ne + no], vals):
            o_ref[...] = v.astype(o_ref.dtype)

    res = pl.pallas_call(
        body, grid=(m // tm, n // tn),
        in_specs=[a_spec, b_spec] + e_specs,
        out_specs=[pl.BlockSpec((tm, tn), lambda i, j: (i, j)) for _ in outs],
        out_shape=[jax.ShapeDtypeStruct((m, n), dt) for dt in outs],
        scratch_shapes=[pltpu.VMEM(a_spec.block_shape, BF16)] if cast_a else [],
        name=name,
    )(a, b, *epi_args)
    return res[0] if no == 1 else res


def _ew(fn, tiled, bcast, outs, reds=(), *, tm=256, name):
    secs = [(t, 0, t.shape[1]) if not isinstance(t, tuple) else t for t in tiled]
    m = secs[0][0].shape[0]
    tm = min(tm, m)
    assert m % tm == 0
    in_specs = []
    for arr, off, w in secs:
        assert off % w == 0
        in_specs.append(pl.BlockSpec((tm, w), functools.partial(lambda i, c: (i, c), c=off // w)))
    in_specs += [pl.BlockSpec(b.shape, lambda i: (0, 0)) for b in bcast]
    nin, no = len(in_specs), len(outs)

    def body(*refs):
        vals = fn(*[r[...] for r in refs[:nin]])
        for r, v in zip(refs[nin:nin + no], vals[:no]):
            r[...] = v.astype(r.dtype)
        i = pl.program_id(0)
        for r, v in zip(refs[nin + no:], vals[no:]):
            @pl.when(i == 0)
            def _():
                r[...] = v

            @pl.when(i > 0)
            def _():
                r[...] += v

    res = pl.pallas_call(
        body, grid=(m // tm,), in_specs=in_specs,
        out_specs=[pl.BlockSpec((tm, c), lambda i: (i, 0)) for c, _ in outs] + [pl.BlockSpec(s, lambda i: (0, 0)) for s in reds],
        out_shape=[jax.ShapeDtypeStruct((m, c), dt) for c, dt in outs] + [jax.ShapeDtypeStruct(s, F32) for s in reds],
        name=name,
    )(*[s[0] for s in secs], *bcast)
    return res


def _cb(fn, cols, params, outs, pouts, *, tc, nblk, name):
    t = cols[0][0].shape[0]
    in_specs = []
    for arr, off in cols:
        assert off % tc == 0
        in_specs.append(pl.BlockSpec((t, tc), functools.partial(lambda c, o: (0, o + c), o=off // tc)))
    for arr, off in params:
        in_specs.append(pl.BlockSpec((arr.shape[0], tc), functools.partial(lambda c, o: (0, o + c), o=off // tc)))
    nin, no = len(in_specs), len(outs)

    def body(*refs):
        vals = fn(*[r[...] for r in refs[:nin]])
        for r, v in zip(refs[nin:], vals):
            r[...] = v.astype(r.dtype)

    return pl.pallas_call(
        body, grid=(nblk,), in_specs=in_specs,
        out_specs=[pl.BlockSpec((t, tc), lambda c: (0, c)) for _ in outs] + [pl.BlockSpec((k, tc), lambda c: (0, c)) for k in pouts],
        out_shape=[jax.ShapeDtypeStruct((t, nblk * tc), dt) for dt in outs] + [jax.ShapeDtypeStruct((k, nblk * tc), F32) for k in pouts],
        name=name,
    )(*[c[0] for c in cols], *[p[0] for p in params])


CUM_BLK = 256


def _prep_point(s, bias, alog):
    lane = lax.broadcasted_iota(jnp.int32, s.shape, 1)
    z = s + bias
    return jnp.where(lane < 8, _log_sigmoid(z),
                     jnp.where(lane < 12, _sigmoid(s),
                               jnp.where(lane < 16, -jnp.exp(alog) * _softplus(z), 0.0)))


def _tri(n, upper):
    r = lax.broadcasted_iota(jnp.int32, (n, n), 0)
    c = lax.broadcasted_iota(jnp.int32, (n, n), 1)
    return (r <= c if upper else r >= c).astype(F32)


def _prep_fwd(proj, bias_row, alog_row, name):
    t = proj.shape[0]
    nb = t // CUM_BLK

    def body(s_ref, b_ref, a_ref, o_ref):
        pre = _prep_point(s_ref[...], b_ref[...], a_ref[...])
        lane = lax.broadcasted_iota(jnp.int32, (CUM_BLK, 128), 1)
        tri = _tri(CUM_BLK, False)
        carry = jnp.zeros((1, 128), F32)
        for blk in range(nb):
            xb = pre[blk * CUM_BLK:(blk + 1) * CUM_BLK]
            cb = _dot(tri, xb, NN, HI) + carry
            carry = cb[CUM_BLK - 1:CUM_BLK]
            o_ref[blk * CUM_BLK:(blk + 1) * CUM_BLK, :] = jnp.where(lane < 8, cb, xb)

    return pl.pallas_call(
        body, grid=(1,),
        in_specs=[pl.BlockSpec((t, 128), lambda i: (0, C_SMALL // 128)), pl.BlockSpec((1, 128), lambda i: (0, 0)),
                  pl.BlockSpec((1, 128), lambda i: (0, 0))],
        out_specs=pl.BlockSpec((t, 128), lambda i: (0, 0)),
        out_shape=jax.ShapeDtypeStruct((t, 128), F32), name=name,
    )(proj, bias_row, alog_row)


def _prep_bwd(proj, bias_row, alog_row, daux, name):
    t = proj.shape[0]
    nb = t // CUM_BLK

    def body(s_ref, b_ref, a_ref, d_ref, ds_ref, db_ref, da_ref, dpre_sc):
        lane = lax.broadcasted_iota(jnp.int32, (CUM_BLK, 128), 1)
        tri = _tri(CUM_BLK, True)
        carry = jnp.zeros((1, 128), F32)
        for blk in reversed(range(nb)):
            db = d_ref[blk * CUM_BLK:(blk + 1) * CUM_BLK, :]
            cb = _dot(tri, db, NN, HI) + carry
            carry = cb[0:1]
            dpre_sc[blk * CUM_BLK:(blk + 1) * CUM_BLK, :] = jnp.where(lane < 8, cb, db)
        _, vjp = jax.vjp(_prep_point, s_ref[...], b_ref[...], a_ref[...])
        ds, dbias, dalog = vjp(dpre_sc[...])
        ds_ref[...] = ds.astype(ds_ref.dtype)
        db_ref[...] = dbias
        da_ref[...] = dalog

    return pl.pallas_call(
        body, grid=(1,),
        in_specs=[pl.BlockSpec((t, 128), lambda i: (0, C_SMALL // 128)), pl.BlockSpec((1, 128), lambda i: (0, 0)),
                  pl.BlockSpec((1, 128), lambda i: (0, 0)), pl.BlockSpec((t, 128), lambda i: (0, 0))],
        out_specs=[pl.BlockSpec((t, 128), lambda i: (0, 0)), pl.BlockSpec((1, 128), lambda i: (0, 0)),
                   pl.BlockSpec((1, 128), lambda i: (0, 0))],
        out_shape=[jax.ShapeDtypeStruct((t, 128), BF16), jax.ShapeDtypeStruct((1, 128), F32), jax.ShapeDtypeStruct((1, 128), F32)],
        scratch_shapes=[pltpu.VMEM((t, 128), F32)], name=name,
    )(proj, bias_row, alog_row, daux)


ATT_TQ = 256
FOX_SCALE = FOX_DH ** -0.5


def _qnorm(q, g):
    return _rms(q, g) * FOX_SCALE


def _att_scores(qn_blk, kn, cfc_blk, cfr, qi, tq, kend):
    s = _dot(qn_blk.astype(BF16), kn[:kend].astype(BF16), NT) + cfc_blk - cfr[:, :kend]
    row = lax.broadcasted_iota(jnp.int32, (tq, kend), 0) + qi * tq
    col = lax.broadcasted_iota(jnp.int32, (tq, kend), 1)
    return s, row >= col


ATT_PAIR = 128 // FOX_DH


def _att_specs(t):
    pair = lambda off: pl.BlockSpec((t, 128), functools.partial(lambda i, o: (0, o + i), o=off // 128))
    gain = pl.BlockSpec((1, FOX_DH), lambda i: (0, 0))
    hd = lambda i: (i, 0, 0)
    col, row = pl.BlockSpec((ATT_PAIR, t, 1), hd), pl.BlockSpec((ATT_PAIR, 1, t), hd)
    return [pair(C_FQ), pair(C_FK), pair(C_FV), gain, gain, col, row], pl.BlockSpec((t, 128), lambda i: (0, i)), col, row


def _att_fwd(proj, qg, kg, cfc, cfr, name):
    t = proj.shape[0]
    tq = min(ATT_TQ, t)
    in_specs, pair_out, col, _ = _att_specs(t)

    def body(q_ref, k_ref, v_ref, qg_ref, kg_ref, cfc_ref, cfr_ref, o_ref, lse_ref):
        for e in range(ATT_PAIR):
            lanes = slice(e * FOX_DH, (e + 1) * FOX_DH)
            qn = _qnorm(q_ref[:, lanes], qg_ref[...])
            kn = _rms(k_ref[:, lanes], kg_ref[...])
            v = v_ref[:, lanes].astype(BF16)
            cfr = cfr_ref[e]
            for qi in range(t // tq):
                kend = (qi + 1) * tq
                rows = slice(qi * tq, kend)
                s, mask = _att_scores(qn[rows], kn, cfc_ref[e, rows, :], cfr, qi, tq, kend)
                s = jnp.where(mask, s, NEG)
                m = jnp.max(s, axis=1, keepdims=True)
                p = jnp.exp(s - m)
                l = jnp.sum(p, axis=1, keepdims=True)
                o_ref[rows, lanes] = _dot(p.astype(BF16), v[:kend], NN) / l
                lse_ref[e, rows, :] = m + jnp.log(l)

    return pl.pallas_call(
        body, grid=(FOX_HEADS // ATT_PAIR,), in_specs=in_specs, out_specs=[pair_out, col],
        out_shape=[jax.ShapeDtypeStruct((t, BW), F32), jax.ShapeDtypeStruct((FOX_HEADS, t, 1), F32)], name=name,
    )(proj, proj, proj, qg, kg, cfc, cfr)


def _att_bwd(proj, qg, kg, cfc, cfr, lse, o, do, name):
    t = proj.shape[0]
    tq = min(ATT_TQ, t)
    in_specs, pair_out, col, row = _att_specs(t)

    def body(q_ref, k_ref, v_ref, qg_ref, kg_ref, cfc_ref, cfr_ref, lse_ref, o_ref, do_ref,
             dq_ref, dk_ref, dv_ref, dcfc_ref, dcfr_ref, dqg_ref, dkg_ref, dqn_sc, dkn_sc, dv_sc, dcfr_sc):
        for e in range(ATT_PAIR):
            lanes = slice(e * FOX_DH, (e + 1) * FOX_DH)
            qn, vjp_q = jax.vjp(_qnorm, q_ref[:, lanes], qg_ref[...])
            kn, vjp_k = jax.vjp(_rms, k_ref[:, lanes], kg_ref[...])
            v = v_ref[:, lanes].astype(BF16)
            cfr = cfr_ref[e]
            do_e = do_ref[:, lanes]
            delta = jnp.sum(do_e * o_ref[:, lanes], axis=1, keepdims=True)
            dkn_sc[...] = jnp.zeros_like(dkn_sc)
            dv_sc[...] = jnp.zeros_like(dv_sc)
            dcfr_sc[...] = jnp.zeros_like(dcfr_sc)
            for qi in range(t // tq):
                kend = (qi + 1) * tq
                rows = slice(qi * tq, kend)
                s, mask = _att_scores(qn[rows], kn, cfc_ref[e, rows, :], cfr, qi, tq, kend)
                p = jnp.where(mask, jnp.exp(jnp.where(mask, s, NEG) - lse_ref[e, rows, :]), 0.0)
                do_b = do_e[rows].astype(BF16)
                dv_sc[0:kend, :] += _dot(p.astype(BF16), do_b, TN)
                dp = _dot(do_b, v[:kend], NT)
                ds = p * (dp - delta[rows])
                ds_b = ds.astype(BF16)
                dqn_sc[rows, :] = _dot(ds_b, kn[:kend].astype(BF16), NN)
                dkn_sc[0:kend, :] += _dot(ds_b, qn[rows].astype(BF16), TN)
                dcfc_ref[e, rows, :] = jnp.sum(ds, axis=1, keepdims=True)
                dcfr_sc[:, 0:kend] -= jnp.sum(ds, axis=0, keepdims=True)
            dq, dqg = vjp_q(dqn_sc[...])
            dk, dkg = vjp_k(dkn_sc[...])
            dq_ref[:, lanes] = dq.astype(dq_ref.dtype)
            dk_ref[:, lanes] = dk.astype(dk_ref.dtype)
            dv_ref[:, lanes] = dv_sc[...].astype(dv_ref.dtype)
            dcfr_ref[e] = dcfr_sc[...]
            dqg_ref[e] = dqg
            dkg_ref[e] = dkg

    gsp = pl.BlockSpec((ATT_PAIR, 1, FOX_DH), lambda i: (i, 0, 0))
    return pl.pallas_call(
        body, grid=(FOX_HEADS // ATT_PAIR,),
        in_specs=in_specs + [col, pair_out, pair_out],
        out_specs=[pair_out] * 3 + [col, row, gsp, gsp],
        out_shape=[jax.ShapeDtypeStruct((t, BW), BF16)] * 3
        + [jax.ShapeDtypeStruct((FOX_HEADS, t, 1), F32), jax.ShapeDtypeStruct((FOX_HEADS, 1, t), F32)]
        + [jax.ShapeDtypeStruct((FOX_HEADS, 1, FOX_DH), F32)] * 2,
        scratch_shapes=[pltpu.VMEM((t, FOX_DH), F32)] * 3 + [pltpu.VMEM((1, t), F32)], name=name,
    )(proj, proj, proj, qg, kg, cfc, cfr, lse, o, do)


DN_SCALE = DN_DH ** -0.5


DN_BATCH = 8


@functools.partial(jax.custom_vjp, nondiff_argnums=(2, 3))
def _mm3(a, b, dims, batch=False):
    return _mm3_passes(a, b, dims, batch)


def _mm3_fwd(a, b, dims, batch):
    return _mm3_passes(a, b, dims, batch), (a, b)


def _mm3_bwd(dims, batch, res, dc):
    a, b = res
    if dims == NN:
        return _mm3_passes(dc, b, NT, batch), _mm3_passes(a, dc, TN, batch)
    if dims == NT:
        return _mm3_passes(dc, b, NN, batch), _mm3_passes(dc, a, TN, batch)
    return _mm3_passes(b, dc, NT, batch), _mm3_passes(a, dc, NN, batch)


_mm3.defvjp(_mm3_fwd, _mm3_bwd)


def _mm3_passes(a, b, dims, batch):
    if batch:
        dn = (((dims[0][0] + 1,), (dims[1][0] + 1,)), ((0,), (0,)))
        dot = lambda p, q: lax.dot_general(p, q, dn, preferred_element_type=F32)
    else:
        dot = lambda p, q: _dot(p, q, dims)
    ah, bh = a.astype(BF16), b.astype(BF16)
    al, bl = (a - ah.astype(F32)).astype(BF16), (b - bh.astype(F32)).astype(BF16)
    return dot(ah, bh) + (dot(ah, bl) + dot(al, bh))


def _dn_local(qc, kc, vc, g, beta):
    nb, c, _ = qc.shape
    ii = lax.broadcasted_iota(jnp.int32, (c, c), 0)
    jj = lax.broadcasted_iota(jnp.int32, (c, c), 1)
    incl, strict = ii >= jj, ii > jj
    lower = jnp.broadcast_to(incl.astype(F32), (nb, c, c))
    eye = (ii == jj).astype(F32)
    mm = functools.partial(_mm3, batch=True)
    dm = mm(lower, jnp.where(strict, g, 0.0), NN)
    decay = jnp.where(incl, jnp.exp(jnp.where(incl, dm, 0.0)), 0.0)
    gcum = mm(lower, g * jnp.ones((1, 1, DN_DH), F32), NN)
    eg = jnp.exp(gcum)
    glast = gcum[:, c - 1:c]
    kb = kc * beta
    n1 = jnp.where(strict, mm(kb, kc, NT) * decay, 0.0)
    inv = eye - n1
    pw = n1
    for _ in range(5):
        pw = mm(pw, pw, NN)
        inv = inv + mm(pw, inv, NN)
    sol = mm(inv, jnp.concatenate([vc * beta, kb * eg], axis=2), NN)
    qk = jnp.where(incl, mm(qc, kc, NT) * decay, 0.0)
    return sol[:, :, :DN_DH], sol[:, :, DN_DH:], qk, qc * eg, kc * jnp.exp(glast - gcum), jnp.exp(glast)


@functools.partial(jax.custom_vjp, nondiff_argnums=(2,))
def _mm1(a, b, dims):
    return _dot(a.astype(BF16), b.astype(BF16), dims)


def _mm1_fwd(a, b, dims):
    return _mm1(a, b, dims), (a, b)


def _mm1_bwd(dims, res, dc):
    a, b = res
    if dims == NN:
        return _mm1(dc, b, NT), _mm1(a, dc, TN)
    if dims == NT:
        return _mm1(dc, b, NN), _mm1(dc, a, TN)
    return _mm1(b, dc, NT), _mm1(a, dc, NN)


_mm1.defvjp(_mm1_fwd, _mm1_bwd)


def _dn_state(u, kcum, qk, qdec, kdec, egl, state):
    v_new = u - _mm1(kcum, state, NN)
    out = _mm1(qdec, state, NN) + _mm1(qk, v_new, NN)
    return out, state * egl + _mm1(kdec, v_new, TN)


def _dn_pre_q(c):
    return _l2(_silu(c)) * DN_SCALE


def _dn_pre_k(c):
    return _l2(_silu(c))


def _dn_post(o, z, ng):
    return _rms(o, ng) * _silu(z)


def _dn_specs(t):
    cblk = lambda o: pl.BlockSpec((t, DN_DH), functools.partial(lambda h, o: (0, o + h), o=o // DN_DH))
    wblk = lambda o: pl.BlockSpec((4, DN_DH), functools.partial(lambda h, o: (0, o + h), o=o // DN_DH))
    proj_specs = [cblk(C_DQ), cblk(C_DK), cblk(C_DV), cblk(C_DZ)]
    w_specs = [wblk(0), wblk(BW), wblk(2 * BW)]
    gb_spec = pl.BlockSpec((1, t, 2), lambda h: (h, 0, 0))
    return proj_specs, w_specs, gb_spec


def _chunk_rows(n, count=1):
    return pl.ds(pl.multiple_of(n * DN_CHUNK, DN_CHUNK), count * DN_CHUNK)


def _egl_rows(n, count=1):
    return pl.ds(pl.multiple_of(n * 8, 8), count * 8)


def _dn_local_inputs(n, qn_sc, kn_sc, vv_sc, gb_ref):
    r = _chunk_rows(n, DN_BATCH)
    split = lambda v: v.reshape(DN_BATCH, DN_CHUNK, v.shape[-1])
    gbv = split(gb_ref[0, r, :])
    return split(qn_sc[r, :]), split(kn_sc[r, :]), split(vv_sc[r, :]), gbv[:, :, 0:1], gbv[:, :, 1:2]


def _dn_local_phase(nc, qn_sc, kn_sc, vv_sc, gb_ref, loc):
    def step(i, carry):
        n = i * DN_BATCH
        vals = _dn_local(*_dn_local_inputs(n, qn_sc, kn_sc, vv_sc, gb_ref))
        for sc, val in zip(loc[:5], vals[:5]):
            sc[_chunk_rows(n, DN_BATCH), :] = val.reshape(DN_BATCH * DN_CHUNK, val.shape[-1])
        loc[5][_egl_rows(n, DN_BATCH), :] = jnp.broadcast_to(vals[5], (DN_BATCH, 8, DN_DH)).reshape(DN_BATCH * 8, DN_DH)
        return carry

    lax.fori_loop(0, nc // DN_BATCH, step, 0)


def _dn_loc_scratch(t, nc):
    big = pltpu.VMEM((t, DN_DH), F32)
    return [big, big, pltpu.VMEM((t, DN_CHUNK), F32), big, big, pltpu.VMEM((nc * 8, DN_DH), F32)]


def _dn_fwd(proj, conv_w, gb, ng, name):
    t = proj.shape[0]
    nc = t // DN_CHUNK
    assert nc % DN_BATCH == 0
    proj_specs, w_specs, gb_spec = _dn_specs(t)

    def body(q_ref, k_ref, v_ref, z_ref, wq_ref, wk_ref, wv_ref, gb_ref, ng_ref, y_ref, o_ref, st_ref, qn_sc, kn_sc, vv_sc, *loc):
        qn_sc[...] = _dn_pre_q(_conv(q_ref[...], wq_ref[...]))
        kn_sc[...] = _dn_pre_k(_conv(k_ref[...], wk_ref[...]))
        vv_sc[...] = _silu(_conv(v_ref[...], wv_ref[...]))
        _dn_local_phase(nc, qn_sc, kn_sc, vv_sc, gb_ref, loc)
        u_sc, kcum_sc, qk_sc, qdec_sc, kdec_sc, egl_sc = loc

        def chunk(n, state):
            r = _chunk_rows(n)
            egl = egl_sc[_egl_rows(n), :][0:1]
            out, new_state = _dn_state(u_sc[r, :], kcum_sc[r, :], qk_sc[r, :], qdec_sc[r, :], kdec_sc[r, :], egl, state)
            st_ref[0, n] = state
            o_ref[r, :] = out
            return new_state

        lax.fori_loop(0, nc, chunk, jnp.zeros((DN_DH, DN_DH), F32))
        y_ref[...] = _dn_post(o_ref[...], z_ref[...], ng_ref[...])

    hblk = pl.BlockSpec((t, DN_DH), lambda h: (0, h))
    return pl.pallas_call(
        body, grid=(DN_HEADS,),
        in_specs=proj_specs + w_specs + [gb_spec, pl.BlockSpec((1, DN_DH), lambda h: (0, 0))],
        out_specs=[hblk, hblk, pl.BlockSpec((1, nc, DN_DH, DN_DH), lambda h: (h, 0, 0, 0))],
        out_shape=[jax.ShapeDtypeStruct((t, BW), F32), jax.ShapeDtypeStruct((t, BW), F32),
                   jax.ShapeDtypeStruct((DN_HEADS, nc, DN_DH, DN_DH), F32)],
        scratch_shapes=[pltpu.VMEM((t, DN_DH), F32)] * 3 + _dn_loc_scratch(t, nc), name=name,
    )(proj, proj, proj, proj, conv_w, conv_w, conv_w, gb, ng)


def _dn_bwd(proj, conv_w, gb, ng, o, states, dy, name):
    t = proj.shape[0]
    nc = t // DN_CHUNK
    proj_specs, w_specs, gb_spec = _dn_specs(t)
    nloc = 6

    def body(q_ref, k_ref, v_ref, z_ref, wq_ref, wk_ref, wv_ref, gb_ref, ng_ref, o_ref, st_ref, dy_ref,
             dq_ref, dk_ref, dv_ref, dz_ref, dwq_ref, dwk_ref, dwv_ref, dgb_ref, dng_ref,
             qn_sc, kn_sc, vv_sc, do_sc, *rest):
        loc, dloc = rest[:nloc], rest[nloc:]
        qn_sc[...] = _dn_pre_q(_conv(q_ref[...], wq_ref[...]))
        kn_sc[...] = _dn_pre_k(_conv(k_ref[...], wk_ref[...]))
        vv_sc[...] = _silu(_conv(v_ref[...], wv_ref[...]))
        _, vjp_y = jax.vjp(_dn_post, o_ref[...], z_ref[...], ng_ref[...])
        do, dz, dng = vjp_y(dy_ref[...])
        do_sc[...] = do
        dz_ref[...] = dz.astype(dz_ref.dtype)
        dng_ref[0] = dng
        _dn_local_phase(nc, qn_sc, kn_sc, vv_sc, gb_ref, loc)
        u_sc, kcum_sc, qk_sc, qdec_sc, kdec_sc, egl_sc = loc

        def state_bwd(i, dstate):
            n = nc - 1 - i
            r = _chunk_rows(n)
            r8 = _egl_rows(n)
            _, vjp = jax.vjp(_dn_state, u_sc[r, :], kcum_sc[r, :], qk_sc[r, :], qdec_sc[r, :], kdec_sc[r, :],
                             egl_sc[r8, :][0:1], st_ref[0, n])
            du, dkcum, dqk, dqdec, dkdec, degl, dprev = vjp((do_sc[r, :], dstate))
            for d_sc, val in zip(dloc[:5], (du, dkcum, dqk, dqdec, dkdec)):
                d_sc[r, :] = val
            dloc[5][r8, :] = jnp.broadcast_to(degl, (8, DN_DH))
            return dprev

        lax.fori_loop(0, nc, state_bwd, jnp.zeros((DN_DH, DN_DH), F32))

        def local_bwd(i, carry):
            n = i * DN_BATCH
            r = _chunk_rows(n, DN_BATCH)
            _, vjp = jax.vjp(_dn_local, *_dn_local_inputs(n, qn_sc, kn_sc, vv_sc, gb_ref))
            cts = tuple(d_sc[r, :].reshape(DN_BATCH, DN_CHUNK, d_sc.shape[-1]) for d_sc in dloc[:5])
            cts += (dloc[5][_egl_rows(n, DN_BATCH), :].reshape(DN_BATCH, 8, DN_DH)[:, 0:1],)
            dqc, dkc, dvc, dg, dbeta = vjp(cts)
            for d_sc, val in zip((dloc[0], dloc[1], dloc[3]), (dqc, dkc, dvc)):
                d_sc[r, :] = val.reshape(DN_BATCH * DN_CHUNK, DN_DH)
            dgb_ref[0, r, :] = jnp.concatenate([dg, dbeta], axis=2).reshape(DN_BATCH * DN_CHUNK, 2)
            return carry

        lax.fori_loop(0, nc // DN_BATCH, local_bwd, 0)
        for x_ref, w_ref, pre, d_sc, dx_ref, dw_ref in ((q_ref, wq_ref, _dn_pre_q, dloc[0], dq_ref, dwq_ref),
                                                       (k_ref, wk_ref, _dn_pre_k, dloc[1], dk_ref, dwk_ref),
                                                       (v_ref, wv_ref, _silu, dloc[3], dv_ref, dwv_ref)):
            _, vjp = jax.vjp(pre, _conv(x_ref[...], w_ref[...]))
            (dc,) = vjp(d_sc[...])
            dx, dws = _conv_bwd(x_ref[...], w_ref[...], dc)
            dx_ref[...] = dx.astype(dx_ref.dtype)
            for j, dw in enumerate(dws):
                dw_ref[j:j + 1, :] = dw

    hblk = pl.BlockSpec((t, DN_DH), lambda h: (0, h))
    wout = pl.BlockSpec((4, DN_DH), lambda h: (0, h))
    return pl.pallas_call(
        body, grid=(DN_HEADS,),
        in_specs=proj_specs + w_specs + [gb_spec, pl.BlockSpec((1, DN_DH), lambda h: (0, 0)), hblk,
                                         pl.BlockSpec((1, nc, DN_DH, DN_DH), lambda h: (h, 0, 0, 0)), hblk],
        out_specs=[hblk] * 4 + [wout] * 3 + [gb_spec, pl.BlockSpec((1, 1, DN_DH), lambda h: (h, 0, 0))],
        out_shape=[jax.ShapeDtypeStruct((t, BW), BF16)] * 4 + [jax.ShapeDtypeStruct((4, BW), F32)] * 3
        + [jax.ShapeDtypeStruct((DN_HEADS, t, 2), F32), jax.ShapeDtypeStruct((DN_HEADS, 1, DN_DH), F32)],
        scratch_shapes=[pltpu.VMEM((t, DN_DH), F32)] * 4 + _dn_loc_scratch(t, nc) * 2, name=name,
    )(proj, proj, proj, proj, conv_w, conv_w, conv_w, gb, ng, o, states, dy)


MERGE_TM, MERGE_TN = 512, 512


def _merge_specs(t):
    tm, tn = min(MERGE_TM, t), MERGE_TN
    y_spec = pl.BlockSpec((tm, BW), lambda i, j: (i, 0))
    w_spec = pl.BlockSpec((3, BW, tn), lambda i, j: (0, 0, j))
    gate_specs = [pl.BlockSpec((tm, tn), functools.partial(lambda i, j, o: (i, o + j), o=(C_GATE + n * D_MODEL) // tn))
                  for n in range(3)]
    return tm, tn, [y_spec] * 3 + [w_spec] + gate_specs


def _merge_fwd(ys, wb, proj, name):
    t = proj.shape[0]
    tm, tn, in_specs = _merge_specs(t)

    def body(y0, y1, y2, w_ref, g0, g1, g2, o_ref):
        acc = jnp.zeros((tm, tn), F32)
        for n, (y, g) in enumerate(((y0, g0), (y1, g1), (y2, g2))):
            acc = acc + _dot(y[...].astype(BF16), w_ref[n], NN) * _sigmoid(g[...])
        o_ref[...] = acc.astype(o_ref.dtype)

    return pl.pallas_call(
        body, grid=(t // tm, D_MODEL // tn), in_specs=in_specs,
        out_specs=pl.BlockSpec((tm, tn), lambda i, j: (i, j)),
        out_shape=jax.ShapeDtypeStruct((t, D_MODEL), BF16), name=name,
    )(*ys, wb, proj, proj, proj)


def _merge_bwd(ys, wb, proj, dmerged, name):
    t = proj.shape[0]
    tm, tn, in_specs = _merge_specs(t)

    def body(y0, y1, y2, w_ref, g0, g1, g2, dm_ref, dg_ref, dt_ref):
        dm = dm_ref[...]
        for n, (y, g) in enumerate(((y0, g0), (y1, g1), (y2, g2))):
            tn_ = _dot(y[...].astype(BF16), w_ref[n], NN)
            sg = _sigmoid(g[...])
            dg_ref[n] = (dm * tn_ * sg * (1.0 - sg)).astype(dg_ref.dtype)
            dt_ref[n] = (dm * sg).astype(dt_ref.dtype)

    o3 = pl.BlockSpec((3, tm, tn), lambda i, j: (0, i, j))
    return pl.pallas_call(
        body, grid=(t // tm, D_MODEL // tn), in_specs=in_specs + [pl.BlockSpec((tm, tn), lambda i, j: (i, j))],
        out_specs=[o3, o3], out_shape=[jax.ShapeDtypeStruct((3, t, D_MODEL), BF16)] * 2, name=name,
    )(*ys, wb, proj, proj, proj, dmerged)


def _prep_rows(sp):
    z4, z112 = jnp.zeros((1, 4), F32), jnp.zeros((1, 112), F32)
    bias_row = jnp.concatenate([sp['b_fox_f'], z4, sp['dn_dt_bias'], z112], axis=1)
    alog_row = jnp.concatenate([jnp.zeros((1, 12), F32), sp['dn_a_log'], z112], axis=1)
    return bias_row, alog_row


def _sc_fwd_tile(sb, sc, sv, w):
    return (sb * _conv(sc * sv, w),)


def _ffn_act(ug, uv):
    return _silu(ug) * uv


W_IN_COLS = [(C_FQ, 0, 1536), (C_SB, 1544, 1536), (C_DQ, 3080, 1536), (C_DZ, 4624, 512), (C_GATE, 5136, 3072),
             (C_SMALL, 1536, 8), (C_SMALL + 8, 4616, 8)]
W_IN_SHARD = IN_ORIG // N_DEV


def _w_in_segments():
    out = []
    for d, o, w in W_IN_COLS:
        end = o + w
        while o < end:
            k = o // W_IN_SHARD
            n = min(end, (k + 1) * W_IN_SHARD) - o
            out.append((d, k, o - k * W_IN_SHARD, n))
            o, d = o + n, d + n
    return out


def _w_in_assemble(g, name):
    tm = 256

    def body(g_ref, o_ref):
        for d, k, s, n in _w_in_segments():
            o_ref[:, d:d + n] = g_ref[k, :, s:s + n]
        o_ref[:, IN_ORIG:IN_P] = jnp.zeros((tm, IN_P - IN_ORIG), o_ref.dtype)

    return pl.pallas_call(
        body, grid=(D_MODEL // tm,),
        in_specs=[pl.BlockSpec((N_DEV, tm, W_IN_SHARD), lambda i: (0, i, 0))],
        out_specs=pl.BlockSpec((tm, IN_P), lambda i: (i, 0)),
        out_shape=jax.ShapeDtypeStruct((D_MODEL, IN_P), g.dtype), name=name,
    )(g)


W_IN_SPLIT_TM = 128


def _w_in_split_by_device(g, name):
    tm = W_IN_SPLIT_TM

    def body(g_ref, own_ref, send_ref):
        me = 4 * lax.axis_index("x") + 2 * lax.axis_index("y") + lax.axis_index("c")
        for d, k, s, n in _w_in_segments():
            val = g_ref[:, d:d + n]
            send_ref[k, :, s:s + n] = val.astype(send_ref.dtype)

            @pl.when(me == k)
            def _():
                own_ref[:, s:s + n] = val

    return pl.pallas_call(
        body, grid=(D_MODEL // tm,), in_specs=[pl.BlockSpec((tm, IN_P), lambda i: (i, 0))],
        out_specs=[pl.BlockSpec((tm, W_IN_SHARD), lambda i: (i, 0)), pl.BlockSpec((N_DEV, tm, W_IN_SHARD), lambda i: (0, i, 0))],
        out_shape=[jax.ShapeDtypeStruct((D_MODEL, W_IN_SHARD), F32), jax.ShapeDtypeStruct((N_DEV, D_MODEL, W_IN_SHARD), BF16)],
        name=name,
    )(g)


def _layer_fwd(x, p_i, w, sp, tag, late_weights=None):
    t = x.shape[0]
    sv = {'x': x}
    (hn,) = _ew(lambda a, g: (_rms(a, g),), [x], [sp['g_mix']], [(D_MODEL, BF16)], name=f'rms_mix_{tag}')
    proj = _mm(hn, w['w_in'], 'nn', [F32], name=f'in_proj_{tag}')
    bias_row, alog_row = _prep_rows(sp)
    aux = _prep_fwd(proj, bias_row, alog_row, f'prep_{tag}')
    cf = aux[:, :FOX_HEADS].T
    cfc, cfr = cf[:, :, None], cf[:, None, :]
    y_fox, lse = _att_fwd(proj, sp['fox_q_gain'], sp['fox_k_gain'], cfc, cfr, f'fox_fwd_{tag}')
    (y_sc,) = _cb(_sc_fwd_tile, [(proj, C_SB), (proj, C_SC), (proj, C_SV)], [(w['sc_conv_w'], 0)], [F32], [],
                  tc=256, nblk=2, name=f'sc_fwd_{tag}')
    gb = jnp.stack([aux[:, 12:16].T, aux[:, 8:12].T], axis=-1)
    y_dn, o_dn, states = _dn_fwd(proj, w['dn_conv_w'], gb, sp['dn_norm_gain'], f'dn_fwd_{tag}')
    ys = (y_fox, y_sc, y_dn)
    if late_weights is not None:
        w = {**w, **late_weights(y_dn)}
    merged = _merge_fwd(ys, w['w_branch'], proj, f'merge_fwd_{tag}')
    x1 = _mm(merged, w['w_o'], 'nn', [F32], epi=lambda acc, r: (acc + r,), epi_args=(x,), name=f'o_proj_{tag}')
    (hf,) = _ew(lambda a, g: (_rms(a, g),), [x1], [sp['g_ffn']], [(D_MODEL, BF16)], name=f'rms_ffn_{tag}')
    up = _mm(hf, w['w_up'], 'nt', [F32], name=f'up_proj_{tag}')
    (act,) = _cb(lambda ug, uv, wg, wv: (_ffn_act(_conv(ug, wg), _conv(uv, wv)),), [(up, 0), (up, D_FF)],
                 [(w['ffn_conv_w'], 0), (w['ffn_conv_w'], D_FF)], [BF16], [], tc=256, nblk=D_FF // 256, name=f'ffn_act_{tag}')
    x2 = _mm(act, w['w_down'], 'nn', [F32], epi=lambda acc, r: (acc + r,), epi_args=(x1,), name=f'down_proj_{tag}')
    (hp,) = _ew(lambda a, g: (_rms(a, g),), [x2], [sp['g_ple']], [(D_MODEL, BF16)], name=f'rms_ple_{tag}')
    gp = _mm(hp, w['w_ple_gate'], 'nn', [F32], name=f'ple_gate_{tag}')
    x3 = _mm(p_i, w['w_ple'], 'nn', [F32], epi=lambda acc, g, r: (r + _sigmoid(g) * acc,), epi_args=(gp, x2), name=f'ple_{tag}')
    sv.update(hn=hn, proj=proj, aux=aux, cfc=cfc, cfr=cfr, lse=lse, ys=ys, gb=gb, o_dn=o_dn,
              states=states, merged=merged, x1=x1, hf=hf, up=up, act=act, x2=x2, hp=hp, gp=gp, p=p_i,
              bias_row=bias_row, alog_row=alog_row, w=w)
    return x3, sv


def _rms_bwd(x, g, dh, dres, name):
    def fn(xv, dhv, dr, gv):
        _, vjp = jax.vjp(_rms, xv, gv)
        dx, dg = vjp(dhv)
        return dr + dx, dg
    return _ew(fn, [x, dh, dres], [g], [(D_MODEL, F32)], [(1, D_MODEL)], name=name)


def _layer_bwd(dx3, sv, sp, tag, early_grads=None, last_grad=None):
    t = dx3.shape[0]
    w = sv['w']
    g = {}
    def ple_epi(acc, gpv, d):
        s = _sigmoid(gpv)
        return d * acc * s * (1.0 - s), d * s
    dgp, de = _mm(sv['p'], w['w_ple'], 'nn', [BF16, BF16], epi=ple_epi, epi_args=(sv['gp'], dx3), name=f'ple_bwd_{tag}')
    g['w_ple'] = _mm(sv['p'], de, 'tn', [F32], name=f'd_w_ple_{tag}')
    g['w_ple_gate'] = _mm(sv['hp'], dgp, 'tn', [F32], name=f'd_w_ple_gate_{tag}')
    dhp = _mm(dgp, w['w_ple_gate'], 'nt', [F32], name=f'd_hp_{tag}')
    dx2, g['g_ple'] = _rms_bwd(sv['x2'], sp['g_ple'], dhp, dx3, f'rms_ple_bwd_{tag}')
    dact = _mm(dx2, w['w_down'], 'nt', [F32], name=f'd_act_{tag}')
    g['w_down'] = _mm(sv['act'], dx2, 'tn', [F32], name=f'd_w_down_{tag}')

    def ffn_bwd_tile(ug, uv, da, wg, wv):
        cg, cv = _conv(ug, wg), _conv(uv, wv)
        _, vjp = jax.vjp(_ffn_act, cg, cv)
        dcg, dcv = vjp(da)
        dug, dwg = _conv_bwd(ug, wg, dcg)
        duv, dwv = _conv_bwd(uv, wv, dcv)
        return dug, duv, jnp.concatenate(dwg, axis=0), jnp.concatenate(dwv, axis=0)
    dupg, dupv, dwg, dwv = _cb(ffn_bwd_tile, [(sv['up'], 0), (sv['up'], D_FF), (dact, 0)],
                               [(w['ffn_conv_w'], 0), (w['ffn_conv_w'], D_FF)], [BF16, BF16], [3, 3], tc=256,
                               nblk=D_FF // 256, name=f'ffn_act_bwd_{tag}')
    dup = jnp.concatenate([dupg, dupv], axis=1)
    g['ffn_conv_w'] = jnp.concatenate([dwg, dwv], axis=1)
    g['w_up'] = _mm(dup, sv['hf'], 'tn', [F32], name=f'd_w_up_{tag}')
    dhf = _mm(dup, w['w_up'], 'nn', [F32], name=f'd_hf_{tag}')
    dx1, g['g_ffn'] = _rms_bwd(sv['x1'], sp['g_ffn'], dhf, dx2, f'rms_ffn_bwd_{tag}')
    dmerged = _mm(dx1, w['w_o'], 'nt', [F32], name=f'd_merged_{tag}')
    g['w_o'] = _mm(sv['merged'], dx1, 'tn', [F32], name=f'd_w_o_{tag}')
    dgate, dtn = _merge_bwd(sv['ys'], w['w_branch'], sv['proj'], dmerged, f'merge_bwd_{tag}')
    dys, dwb = [], []
    for n in range(3):
        dys.append(_mm(dtn[n], w['w_branch'][n], 'nt', [F32], name=f'd_y{n}_{tag}'))
        dwb.append(_mm(sv['ys'][n], dtn[n], 'tn', [F32], name=f'd_w_branch{n}_{tag}'))
    g['w_branch'] = jnp.stack(dwb)
    if early_grads is not None:
        sp = early_grads(g, sp)
    ddq, ddk, ddv, ddz, dwq, dwk, dwv_, dgb, dng = _dn_bwd(sv['proj'], w['dn_conv_w'], sv['gb'], sp['dn_norm_gain'],
                                                           sv['o_dn'], sv['states'], dys[2], f'dn_bwd_{tag}')
    g['dn_conv_w'] = jnp.concatenate([dwq, dwk, dwv_], axis=1)
    g['dn_norm_gain'] = jnp.sum(dng, axis=0)
    def sc_bwd_tile(sb, sc, svv, dy, wv):
        u = sc * svv
        dsb = dy * _conv(u, wv)
        du, dws = _conv_bwd(u, wv, dy * sb)
        return dsb, du * svv, du * sc, jnp.concatenate(dws, axis=0)
    dsb, dsc, dsv, g['sc_conv_w'] = _cb(sc_bwd_tile, [(sv['proj'], C_SB), (sv['proj'], C_SC), (sv['proj'], C_SV), (dys[1], 0)],
                                        [(w['sc_conv_w'], 0)], [BF16, BF16, BF16], [3], tc=256, nblk=2, name=f'sc_bwd_{tag}')
    dfq, dfk, dfv, dcfc, dcfr, dqg, dkg = _att_bwd(sv['proj'], sp['fox_q_gain'], sp['fox_k_gain'], sv['cfc'], sv['cfr'],
                                                   sv['lse'], sv['ys'][0], dys[0], f'fox_bwd_{tag}')
    g['fox_q_gain'] = jnp.sum(dqg, axis=0)
    g['fox_k_gain'] = jnp.sum(dkg, axis=0)
    dcf = (dcfc[:, :, 0] + dcfr[:, 0, :]).T
    daux = jnp.concatenate([dcf, dgb[:, :, 1].T, dgb[:, :, 0].T, jnp.zeros((t, 112), F32)], axis=1)
    dsmall, dbias, dalog = _prep_bwd(sv['proj'], sv['bias_row'], sv['alog_row'], daux, f'prep_bwd_{tag}')
    g['b_fox_f'] = dbias[:, 0:8]
    g['dn_dt_bias'] = dbias[:, 12:16]
    g['dn_a_log'] = dalog[:, 12:16]
    dproj = jnp.concatenate([dfq, dfk, dfv, dsb, dsc, dsv, ddq, ddk, ddv, ddz, dgate[0], dgate[1], dgate[2], dsmall], axis=1)
    g['w_in'] = _mm(sv['hn'], dproj, 'tn', [F32], name=f'd_w_in_{tag}')
    if last_grad is not None:
        sp = last_grad(g, sp)
    dhn = _mm(dproj, w['w_in'], 'nt', [F32], name=f'd_hn_{tag}')
    dx, g['g_mix'] = _rms_bwd(sv['x'], sp['g_mix'], dhn, dx1, f'rms_mix_bwd_{tag}')
    return dx, g


def _loss_bwd(y, target, name):
    inv = 1.0 / y.shape[1]

    def fn(yv, tv):
        err = yv - tv
        return err * inv, jnp.zeros((8, 128), F32) + 0.5 * inv * jnp.sum(err * err)
    return _ew(fn, [y, target], [], [(y.shape[1], F32)], [(8, 128)], name=name)


PACK_W = 1024
FULL_SHAPE = {'w_in': (D_MODEL, IN_ORIG), 'w_branch': (3, BW, D_MODEL), 'w_o': (D_MODEL, D_MODEL), 'w_up': (2 * D_FF, D_MODEL),
              'w_down': (D_FF, D_MODEL), 'w_ple_gate': (D_MODEL, D_MODEL), 'w_ple': (PLE_DIM, D_MODEL),
              'sc_conv_w': (3, BW), 'dn_conv_w': (4, 3 * BW), 'ffn_conv_w': (3, 2 * D_FF)}
SMALL_SHAPE = {'g_mix': D_MODEL, 'b_fox_f': FOX_HEADS, 'fox_q_gain': FOX_DH, 'fox_k_gain': FOX_DH, 'dn_a_log': DN_HEADS,
               'dn_dt_bias': DN_HEADS, 'dn_norm_gain': DN_DH, 'g_ffn': D_MODEL, 'g_ple': D_MODEL}


def _shard_shape(name):
    s = list(FULL_SHAPE[name])
    s[SHARDED[name]] //= N_DEV
    return tuple(s)


def _full_from_gathered(g, name):
    sh, ax = _shard_shape(name), SHARDED[name]
    blocks = jnp.moveaxis(g, 0, ax)
    return blocks.reshape(sh[:ax] + (N_DEV * sh[ax],) + sh[ax + 1:])


def _by_dest(full, name):
    sh, ax = _shard_shape(name), SHARDED[name]
    return jnp.moveaxis(full.reshape(sh[:ax] + (N_DEV, sh[ax]) + sh[ax + 1:]), ax, 0)


def _flat_pack(arrs):
    flat = jnp.concatenate([a.reshape(-1).astype(F32) for a in arrs])
    rows = -(-flat.shape[0] // (8 * PACK_W)) * 8
    return jnp.pad(flat, (0, rows * PACK_W - flat.shape[0])).reshape(rows, PACK_W)


def _flat_unpack(pack, shapes):
    flat, out, off = pack.reshape(-1), [], 0
    for s in shapes:
        n = 1
        for d in s:
            n *= d
        out.append(flat[off:off + n].reshape(s))
        off += n
    return out


SMALL_SHAPES = [(DEPTH, SMALL_SHAPE[n]) for n in SMALL]
SMALL_LOSS_AT = sum(DEPTH * SMALL_SHAPE[n] for n in SMALL)
CONV_SHARD_SHAPES = [(DEPTH,) + _shard_shape(n) for n in CONVW]
CONV_FULL_SHAPES = [(DEPTH,) + FULL_SHAPE[n] for n in CONVW]


def _pick_tm(m, width):
    best = None
    for tm in range(16, m + 1, 16):
        if m % tm == 0 and tm * width * 4 <= (1 << 20):
            best = tm
    return best if best is not None else m


HBM_SPEC = pl.BlockSpec(memory_space=pltpu.HBM)


def _place():
    x, y, c = lax.axis_index("x"), lax.axis_index("y"), lax.axis_index("c")
    return x, y, c, [(1 - x, y), (x, 1 - y), (1 - x, 1 - y)]


def _all_gather(arrs, name):
    n = len(arrs)

    def body(*refs):
        ins, outs = refs[:n], refs[n:2 * n]
        send_sems, recv_sems = refs[2 * n:]
        x, y, c, chips = _place()
        me, sibling = (x, y, c), (x, y, 1 - c)

        def block(a, p):
            return outs[a].at[4 * p[0] + 2 * p[1] + p[2]]

        def copy(a, k, blk, to, src=None):
            return pltpu.make_async_remote_copy(src_ref=block(a, blk) if src is None else src, dst_ref=block(a, blk),
                                                send_sem=send_sems.at[a, k], recv_sem=recv_sems.at[a, k],
                                                device_id=to, device_id_type=MESH)

        first, passed = [], []
        for a in range(n):
            first.append(copy(a, 0, me, sibling, src=ins[a]))
            first += [copy(a, 1 + j, me, (*chip, c), src=ins[a]) for j, chip in enumerate(chips)]
        for cp in first:
            cp.start()
        for j, chip in enumerate(chips):
            for a in range(n):
                copy(a, 1 + j, (*chip, c), me).wait_recv()
                fwd = copy(a, 4 + j, (*chip, c), sibling)
                fwd.start()
                passed.append(fwd)
        for a in range(n):
            copy(a, 0, sibling, me).wait_recv()
            for j, chip in enumerate(chips):
                copy(a, 4 + j, (*chip, 1 - c), me).wait_recv()
        for cp in first + passed:
            cp.wait_send()

    gathered = pl.pallas_call(
        body, in_specs=[HBM_SPEC] * n, out_specs=[HBM_SPEC] * n,
        out_shape=[jax.ShapeDtypeStruct((N_DEV,) + a.shape, a.dtype) for a in arrs],
        scratch_shapes=[pltpu.SemaphoreType.DMA((n, 7)), pltpu.SemaphoreType.DMA((n, 7))], name=name,
    )(*arrs)
    me = 4 * lax.axis_index("x") + 2 * lax.axis_index("y") + lax.axis_index("c")
    return [lax.dynamic_update_index_in_dim(g, a, me, 0) for g, a in zip(gathered, arrs)]


SEM_SPEC = pl.BlockSpec(memory_space=pltpu.SEMAPHORE)
DATAFLOW = pltpu.SideEffectType.DATAFLOW_SIDE_EFFECTING
ALL_PEERS = (1, 2, 3, 4, 5, 6, 7)
NEAR_PEERS = (1, 4, 2, 6)


def _direct_copies(src_refs, zone_refs, send_sems, recv_sems, per_dest, peers):
    x, y, c, _ = _place()
    me = 4 * x + 2 * y + c
    cps = []
    for a, (src, zone) in enumerate(zip(src_refs, zone_refs)):
        for r, bits in enumerate(peers):
            px = 1 - x if bits & 4 else x
            py = 1 - y if bits & 2 else y
            pc = 1 - c if bits & 1 else c
            cps.append(pltpu.make_async_remote_copy(
                src_ref=src.at[4 * px + 2 * py + pc] if per_dest else src, dst_ref=zone.at[me],
                send_sem=send_sems.at[a * len(peers) + r], recv_sem=recv_sems.at[a * len(peers) + r],
                device_id=(px, py, pc), device_id_type=MESH))
    return cps


def _direct_start(srcs, per_dest, after, name, peers=ALL_PEERS):
    n = len(srcs)
    zones = [lax.empty((N_DEV,) + (s.shape[1:] if per_dest else s.shape), s.dtype) for s in srcs]

    def body(*refs):
        src_refs, zone_refs = refs[:n], refs[n:2 * n]
        send_sems, recv_sems = refs[2 * n + 1:2 * n + 3]
        for cp in _direct_copies(src_refs, zone_refs, send_sems, recv_sems, per_dest, peers):
            cp.start()
        refs[-1][...] = jnp.zeros_like(refs[-1])

    sems = pltpu.SemaphoreType.DMA((n * len(peers),))
    res = pl.pallas_call(
        body, name=name, in_specs=[HBM_SPEC] * (2 * n + 1),
        out_shape=[sems, sems] + [pltpu.HBM(s.shape, s.dtype) for s in srcs] + [pltpu.HBM(z.shape, z.dtype) for z in zones]
        + [jax.ShapeDtypeStruct((8, 128), F32)],
        out_specs=[SEM_SPEC, SEM_SPEC] + [HBM_SPEC] * (2 * n) + [pl.BlockSpec(memory_space=pltpu.VMEM)],
        input_output_aliases={i: 2 + i for i in range(2 * n)},
        compiler_params=pltpu.CompilerParams(has_side_effects=DATAFLOW),
    )(*[pltpu.with_memory_space_constraint(s, pltpu.HBM) for s in srcs],
      *[pltpu.with_memory_space_constraint(z, pltpu.HBM) for z in zones], after)
    return (res[0], res[1], list(res[2:2 + n]), list(res[2 + n:2 + 2 * n]), per_dest, peers), res[-1][0:1, 0:1]


def _direct_wait(started, after, name):
    send_sems, recv_sems, srcs, zones, per_dest, peers = started
    n = len(srcs)

    def body(*refs):
        src_refs, zone_refs = refs[:n], refs[n:2 * n]
        for cp in _direct_copies(src_refs, zone_refs, refs[2 * n], refs[2 * n + 1], per_dest, peers):
            cp.wait_send()
            cp.wait_recv()

    res = pl.pallas_call(
        body, name=name, in_specs=[HBM_SPEC] * (2 * n) + [SEM_SPEC, SEM_SPEC, HBM_SPEC],
        out_shape=[pltpu.HBM(s.shape, s.dtype) for s in srcs] + [pltpu.HBM(z.shape, z.dtype) for z in zones],
        out_specs=[HBM_SPEC] * (2 * n), input_output_aliases={i: i for i in range(2 * n)},
        compiler_params=pltpu.CompilerParams(has_side_effects=DATAFLOW),
    )(*srcs, *zones, send_sems, recv_sems, after)
    return list(res[n:])


def _gather_finish(zones, name):
    n = len(zones)

    def body(*refs):
        zone_in, zone_out = refs[:n], refs[n:2 * n]
        send_sems, recv_sems = refs[2 * n:]
        x, y, c, chips = _place()
        cps = []
        for a in range(n):
            for j, (cx, cy) in enumerate(chips):
                blk = 4 * cx + 2 * cy + c
                cp = pltpu.make_async_remote_copy(src_ref=zone_in[a].at[blk], dst_ref=zone_out[a].at[blk],
                                                  send_sem=send_sems.at[a, j], recv_sem=recv_sems.at[a, j],
                                                  device_id=(x, y, 1 - c), device_id_type=MESH)
                cp.start()
                cps.append(cp)
        for cp in cps:
            cp.wait()

    return pl.pallas_call(
        body, in_specs=[HBM_SPEC] * n, out_specs=[HBM_SPEC] * n,
        out_shape=[jax.ShapeDtypeStruct(z.shape, z.dtype) for z in zones], input_output_aliases={a: a for a in range(n)},
        scratch_shapes=[pltpu.SemaphoreType.DMA((n, 3)), pltpu.SemaphoreType.DMA((n, 3))], name=name,
    )(*zones)


def _adamw(w, g, m, v):
    m = ADAM_B1 * m + (1.0 - ADAM_B1) * g
    v = ADAM_B2 * v + (1.0 - ADAM_B2) * jnp.square(g)
    m_hat = m / (1.0 - ADAM_B1 ** ADAM_STEP)
    v_hat = v / (1.0 - ADAM_B2 ** ADAM_STEP)
    delta = -ADAM_LR * (m_hat / (jnp.sqrt(v_hat) + ADAM_EPS) + ADAM_WD * w)
    return delta, m, v


def _adamw_layer(w, m, v, own, parts, layer, prev, name):
    _, rows, c = w.shape
    tm = _pick_tm(rows, c)

    def body(w_ref, m_ref, v_ref, own_ref, parts_ref, *rest):
        g = own_ref[...]
        for j in range(parts_ref.shape[0]):
            g = g + parts_ref[j].astype(F32)
        delta, m2, v2 = _adamw(w_ref[...], g, m_ref[...], v_ref[...])
        g_ref, d_ref, m2_ref, v2_ref = rest[-4:]
        g_ref[...], d_ref[...], m2_ref[...], v2_ref[...] = g, delta, m2, v2

    wspec = pl.BlockSpec((None, tm, c), lambda i: (layer, i, 0))
    prev = list(prev) if prev is not None else []
    return pl.pallas_call(
        body, grid=(rows // tm,),
        in_specs=[wspec] * 3 + [pl.BlockSpec((tm, c), lambda i: (i, 0)), pl.BlockSpec((parts.shape[0], tm, c), lambda i: (0, i, 0))]
        + [HBM_SPEC] * len(prev),
        out_specs=[wspec] * 4, out_shape=[jax.ShapeDtypeStruct(w.shape, F32)] * 4,
        input_output_aliases={5 + k: k for k in range(len(prev))}, name=name,
    )(w, m, v, own, parts, *prev)


def _adamw_call(w, m, v, gparts, name):
    def fn(wv, mv, vv, *gs):
        g = gs[0].astype(F32)
        for gp in gs[1:]:
            g = g + gp.astype(F32)
        delta, m2, v2 = _adamw(wv, g, mv, vv)
        return g, delta, m2, v2
    rows, c = w.shape
    return _ew(fn, [w, m, v] + list(gparts), [], [(c, F32)] * 4, tm=_pick_tm(rows, c), name=name)


def kernel(x, p, g_mix, w_in, b_fox_f, fox_q_gain, fox_k_gain, sc_conv_w, dn_conv_w, dn_a_log, dn_dt_bias,
           dn_norm_gain, w_branch, w_o, g_ffn, w_up, ffn_conv_w, w_down, g_ple, w_ple_gate, w_ple, loss_target,
           m_g_mix, m_w_in, m_b_fox_f, m_fox_q_gain, m_fox_k_gain, m_sc_conv_w, m_dn_conv_w, m_dn_a_log,
           m_dn_dt_bias, m_dn_norm_gain, m_w_branch, m_w_o, m_g_ffn, m_w_up, m_ffn_conv_w, m_w_down, m_g_ple,
           m_w_ple_gate, m_w_ple, v_g_mix, v_w_in, v_b_fox_f, v_fox_q_gain, v_fox_k_gain, v_sc_conv_w, v_dn_conv_w,
           v_dn_a_log, v_dn_dt_bias, v_dn_norm_gain, v_w_branch, v_w_o, v_g_ffn, v_w_up, v_ffn_conv_w, v_w_down,
           v_g_ple, v_w_ple_gate, v_w_ple):
    return _step(x, p, g_mix, w_in, b_fox_f, fox_q_gain, fox_k_gain, sc_conv_w, dn_conv_w, dn_a_log, dn_dt_bias,
                 dn_norm_gain, w_branch, w_o, g_ffn, w_up, ffn_conv_w, w_down, g_ple, w_ple_gate, w_ple, loss_target,
                 m_g_mix, m_w_in, m_b_fox_f, m_fox_q_gain, m_fox_k_gain, m_sc_conv_w, m_dn_conv_w, m_dn_a_log,
                 m_dn_dt_bias, m_dn_norm_gain, m_w_branch, m_w_o, m_g_ffn, m_w_up, m_ffn_conv_w, m_w_down, m_g_ple,
                 m_w_ple_gate, m_w_ple, v_g_mix, v_w_in, v_b_fox_f, v_fox_q_gain, v_fox_k_gain, v_sc_conv_w,
                 v_dn_conv_w, v_dn_a_log, v_dn_dt_bias, v_dn_norm_gain, v_w_branch, v_w_o, v_g_ffn, v_w_up,
                 v_ffn_conv_w, v_w_down, v_g_ple, v_w_ple_gate, v_w_ple)


def _step(*args):
    names = ['x', 'p'] + WEIGHTS + ['loss_target'] + ['m_' + n for n in WEIGHTS] + ['v_' + n for n in WEIGHTS]
    assert len(args) == len(names)
    a = dict(zip(names, args))
    for n, perm in HELD_TRANSPOSED.items():
        for k in (n, 'm_' + n, 'v_' + n):
            a[k] = jnp.transpose(a[k], perm)
    x, target = a['x'][0], a['loss_target'][0]
    p = a['p'][:, 0]
    dev = 4 * lax.axis_index("x") + 2 * lax.axis_index("y") + lax.axis_index("c")

    LATE = [n for n in BIG if n != 'w_in']
    sps = [{n: a[n][layer][None, :] for n in SMALL} for layer in range(DEPTH)]
    shards = [{n: a[n][layer].astype(BF16) for n in BIG} for layer in range(DEPTH)]

    def by_device(full, n):
        by_dest = _by_dest(full, n)
        return lax.dynamic_index_in_dim(by_dest, dev, axis=0, keepdims=False), by_dest.astype(BF16)

    def blank(zone):
        return lax.dynamic_update_index_in_dim(zone, jnp.zeros(zone.shape[1:], zone.dtype), dev, 0)

    w_in_all, conv_all = _all_gather([shards[0]['w_in'], _flat_pack([a[n] for n in CONVW])], 'gather_w_in_l0')
    conv_by_dev = [_flat_unpack(conv_all[d], CONV_SHARD_SHAPES) for d in range(N_DEV)]
    conv_full = {n: jnp.concatenate([conv_by_dev[d][i] for d in range(N_DEV)], axis=2) for i, n in enumerate(CONVW)}
    late0_started, zero = _direct_start([shards[0][n] for n in LATE], False, w_in_all, 'weights_rest_l0_start', NEAR_PEERS)
    w1_started, zero = _direct_start([shards[1][n] for n in BIG], False, late0_started[2][0], 'weights_l1_start', NEAR_PEERS)
    sps[0]['g_mix'] = sps[0]['g_mix'] + zero

    def late_weights0(after):
        zones = _gather_finish(_direct_wait(late0_started, after, 'weights_rest_l0_wait'), 'weights_rest_l0_finish')
        return {n: _full_from_gathered(lax.dynamic_update_index_in_dim(z, shards[0][n], dev, 0), n) for n, z in zip(LATE, zones)}

    w0 = {n: conv_full[n][0] for n in CONVW}
    w0['w_in'] = _w_in_assemble(w_in_all, 'w_in_assemble_l0')
    h, sv0 = _layer_fwd(x, p[0], w0, sps[0], 'l0', late_weights0)
    zones = [lax.dynamic_update_index_in_dim(z, shards[1][n], dev, 0)
             for n, z in zip(BIG, _gather_finish(_direct_wait(w1_started, h, 'weights_l1_wait'), 'weights_l1_finish'))]
    w1 = {n: _full_from_gathered(z, n) for n, z in zip(BIG, zones) if n != 'w_in'}
    w1.update({n: conv_full[n][1] for n in CONVW})
    w1['w_in'] = _w_in_assemble(zones[list(BIG).index('w_in')], 'w_in_assemble_l1')
    h, sv1 = _layer_fwd(h, p[1], w1, sps[1], 'l1')
    dh, loss_part = _loss_bwd(h, target, 'loss')
    grads = [None] * DEPTH
    dh, grads[1] = _layer_bwd(dh, sv1, sps[1], 'l1')
    owns1, sends1 = {}, {}
    for n in BIG:
        owns1[n], sends1[n] = _w_in_split_by_device(grads[1][n], 'w_in_grad_split_l1') if n == 'w_in' else by_device(grads[1][n], n)
    g1_started, zero = _direct_start([sends1[n] for n in BIG], True, dh, 'grads_l1_start')
    early = {}

    def early_grads0(g, sp):
        owns, sends = zip(*[by_device(g[n], n) for n in LATE])
        early['started'], zero = _direct_start(list(sends), True, g['w_branch'], 'grads_rest_l0_start')
        early['owns'] = dict(zip(LATE, owns))
        return {**sp, 'dn_norm_gain': sp['dn_norm_gain'] + zero}

    def last_grad0(g, sp):
        early['own_in'], send = _w_in_split_by_device(g['w_in'], 'w_in_grad_split_l0')
        early['in_started'], zero = _direct_start([send], True, g['w_in'], 'grads_w_in_l0_start')
        return {**sp, 'g_mix': sp['g_mix'] + zero}

    dh, grads[0] = _layer_bwd(dh, sv0, {**sps[0], 'g_ple': sps[0]['g_ple'] + zero}, 'l0', early_grads0, last_grad0)
    grad_x = dh[None]
    own_in0, in0_started = early['own_in'], early['in_started']
    sent = in0_started[2][0]
    zones1 = dict(zip(BIG, _direct_wait(g1_started, sent, 'grads_l1_wait')))
    zones0 = dict(zip(LATE, _direct_wait(early['started'], sent, 'grads_rest_l0_wait')))

    def adamw(n, layer, own, zone, prev):
        rows = (-1, a[n].shape[-1])
        own = own.reshape(rows)
        view = lambda t: t.reshape((DEPTH,) + own.shape)
        return _adamw_layer(view(a[n]), view(a['m_' + n]), view(a['v_' + n]), own, blank(zone).reshape((N_DEV,) + own.shape),
                            layer, prev, f'adamw_{n}_l{layer}')

    out, done = {}, []
    for n in LATE:
        res = adamw(n, 0, early['owns'][n], zones0[n], adamw(n, 1, owns1[n], zones1[n], None))
        out[n] = [r.reshape(a[n].shape) for r in res]
        done.append(res[0][0, 0:1, 0:1])
    res = adamw('w_in', 1, owns1['w_in'], zones1['w_in'], None)
    done.append(res[0][1, 0:1, 0:1])
    (zone_in0,) = _direct_wait(in0_started, jnp.concatenate(done, axis=1), 'grads_w_in_l0_wait')
    out['w_in'] = [r.reshape(a['w_in'].shape) for r in adamw('w_in', 0, own_in0, zone_in0, res)]

    small_pack = _flat_pack([jnp.stack([grads[layer][n].reshape(-1) for layer in range(DEPTH)]) for n in SMALL] + [loss_part[0, 0]])
    conv_pack = _flat_pack([jnp.stack([grads[layer][n] for layer in range(DEPTH)]) for n in CONVW])
    small_all, conv_all = _all_gather([small_pack, conv_pack], 'gather_small_grads')
    res = _adamw_call(_flat_pack([a[n] for n in SMALL]), _flat_pack([a['m_' + n] for n in SMALL]),
                      _flat_pack([a['v_' + n] for n in SMALL]), [small_all[d] for d in range(N_DEV)], 'adamw_replicated')
    loss = res[0].reshape(-1)[SMALL_LOSS_AT]
    for k, r in enumerate(res):
        for n, val in zip(SMALL, _flat_unpack(r, SMALL_SHAPES)):
            out.setdefault(n, [None] * 4)[k] = val
    (conv_sum,) = _ew(lambda *gs: (functools.reduce(lambda s, t: s + t, gs),), [conv_all[d] for d in range(N_DEV)], [],
                      [(PACK_W, F32)], tm=conv_pack.shape[0], name='conv_grad_sum')
    conv_own = [lax.dynamic_slice_in_dim(g, dev * (g.shape[2] // N_DEV), g.shape[2] // N_DEV, axis=2)
                for g in _flat_unpack(conv_sum, CONV_FULL_SHAPES)]
    res = _adamw_call(_flat_pack([a[n] for n in CONVW]), _flat_pack([a['m_' + n] for n in CONVW]),
                      _flat_pack([a['v_' + n] for n in CONVW]), [_flat_pack(conv_own)], 'adamw_conv')
    for k, r in enumerate(res):
        for n, val in zip(CONVW, _flat_unpack(r, CONV_SHARD_SHAPES)):
            out.setdefault(n, [None] * 4)[k] = val

    for n, perm in HELD_TRANSPOSED.items():
        out[n] = [jnp.transpose(r, perm) for r in out[n]]
    outs = [loss, grad_x]
    for k in range(4):
        outs += [out[n][k] for n in WEIGHTS]
    return tuple(outs)
```

```python
import functools

import jax
import jax.numpy as jnp
from jax import lax
from jax.experimental import pallas as pl
from jax.experimental.pallas import tpu as pltpu

F32 = jnp.float32
BF16 = jnp.bfloat16
HI = lax.Precision.HIGHEST

D_MODEL = 1024
DEPTH = 2
N_DEV = 8
PLE_DIM = 256
BW = 512
FOX_HEADS, FOX_DH = 8, 64
DN_HEADS, DN_DH = 4, 128
DN_CHUNK = 64
D_FF = 2816
EPS = 1e-6
NEG = -1e30

ADAM_LR, ADAM_B1, ADAM_B2, ADAM_EPS, ADAM_WD, ADAM_STEP = 0.001, 0.9, 0.999, 1e-08, 0.01, 10

C_FQ, C_FK, C_FV = 0, 512, 1024
C_SB, C_SC, C_SV = 1536, 2048, 2560
C_DQ, C_DK, C_DV, C_DZ = 3072, 3584, 4096, 4608
C_GATE = 5120
C_SMALL = 8192
IN_P = 8320
IN_ORIG = 8208

WEIGHTS = ['g_mix', 'w_in', 'b_fox_f', 'fox_q_gain', 'fox_k_gain', 'sc_conv_w', 'dn_conv_w', 'dn_a_log',
           'dn_dt_bias', 'dn_norm_gain', 'w_branch', 'w_o', 'g_ffn', 'w_up', 'ffn_conv_w', 'w_down', 'g_ple',
           'w_ple_gate', 'w_ple']
BIG = {'w_in': 1, 'w_branch': 2, 'w_o': 0, 'w_up': 0, 'w_down': 0, 'w_ple_gate': 0, 'w_ple': 1}
HELD_TRANSPOSED = {'w_up': (0, 2, 1)}
CONVW = {'sc_conv_w': 1, 'dn_conv_w': 1, 'ffn_conv_w': 1}
SHARDED = {**BIG, **CONVW}
SMALL = [n for n in WEIGHTS if n not in SHARDED]
MESH = pl.DeviceIdType.MESH


def _sigmoid(x):
    return 0.5 * (jnp.tanh(0.5 * x) + 1.0)


def _silu(x):
    return x * _sigmoid(x)


def _log1pexp_negabs(z):
    return jnp.log(1.0 + jnp.exp(-jnp.abs(z)))


def _log_sigmoid(z):
    return jnp.minimum(z, 0.0) - _log1pexp_negabs(z)


def _softplus(z):
    return jnp.maximum(z, 0.0) + _log1pexp_negabs(z)


def _rms(x, g):
    return x * lax.rsqrt(jnp.mean(x * x, axis=-1, keepdims=True) + EPS) * g


def _l2(x):
    return x * lax.rsqrt(jnp.sum(x * x, axis=-1, keepdims=True) + EPS)


def _dot(a, b, dims, precision=None):
    return lax.dot_general(a, b, (dims, ((), ())), preferred_element_type=F32, precision=precision)


NN = ((1,), (0,))
NT = ((1,), (1,))
TN = ((0,), (0,))


def _shift_down(x, s):
    if s == 0:
        return x
    t = lax.broadcasted_iota(jnp.int32, x.shape, 0)
    return jnp.where(t >= s, pltpu.roll(x, s, 0), 0.0)


def _shift_up(x, s):
    if s == 0:
        return x
    n = x.shape[0]
    t = lax.broadcasted_iota(jnp.int32, x.shape, 0)
    return jnp.where(t < n - s, pltpu.roll(x, n - s, 0), 0.0)


def _conv(x, w):
    k = w.shape[0]
    y = w[k - 1:k] * x
    for j in range(k - 1):
        y = y + w[j:j + 1] * _shift_down(x, k - 1 - j)
    return y


def _conv_bwd(x, w, dy):
    k = w.shape[0]
    dx = w[k - 1:k] * dy
    dws = []
    for j in range(k - 1):
        dx = dx + w[j:j + 1] * _shift_up(dy, k - 1 - j)
        dws.append(jnp.sum(dy * _shift_down(x, k - 1 - j), axis=0, keepdims=True))
    dws.append(jnp.sum(dy * x, axis=0, keepdims=True))
    return dx, dws


MM_VMEM_BUDGET = 36 << 20
MM_STEP_BYTES = 1 << 20


def _mm_tiles(m, n, k, a_size, b_size, tile_size, cast_a):
    best = None
    for tm in [d for d in (2048, 1024, 512, 256, 128) if d <= m and m % d == 0] or [m]:
        for tn in [d for d in range(128, min(n, 2048) + 1, 128) if n % d == 0] or [n]:
            vmem = 2 * tm * k * a_size + (2 * tm * k if cast_a else 0) + 2 * k * tn * b_size + 2 * tm * tn * tile_size
            if vmem > MM_VMEM_BUDGET:
                continue
            steps = (m // tm) * (n // tn)
            cost = m * k * a_size + (m // tm) * k * n * b_size + m * n * tile_size + steps * MM_STEP_BYTES
            if best is None or cost < best[0]:
                best = (cost, tm, tn)
    assert best is not None, (m, n, k)
    return best[1], best[2]


def _mm(a, b, mode, outs, *, epi=None, epi_args=(), name):
    if mode == 'nn':
        (m, k), (k2, n) = a.shape, b.shape
    elif mode == 'nt':
        (m, k), (n, k2) = a.shape, b.shape
    else:
        (k, m), (k2, n) = a.shape, b.shape
    assert k == k2, (a.shape, b.shape, mode)
    tile_size = sum(jnp.dtype(dt).itemsize for dt in outs) + sum(e.dtype.itemsize for e in epi_args if e.shape[0] != 1)
    tm, tn = _mm_tiles(m, n, k, a.dtype.itemsize, b.dtype.itemsize, tile_size, a.dtype != BF16)
    dims = {'nn': NN, 'nt': NT, 'tn': TN}[mode]
    a_spec = pl.BlockSpec((k, tm), lambda i, j: (0, i)) if mode == 'tn' else pl.BlockSpec((tm, k), lambda i, j: (i, 0))
    b_spec = pl.BlockSpec((tn, k), lambda i, j: (j, 0)) if mode == 'nt' else pl.BlockSpec((k, tn), lambda i, j: (0, j))
    e_specs = [pl.BlockSpec((1, tn), lambda i, j: (0, j)) if e.shape[0] == 1 else pl.BlockSpec((tm, tn), lambda i, j: (i, j))
               for e in epi_args]
    ne, no = len(epi_args), len(outs)
    cast_a = a.dtype != BF16

    def body(a_ref, b_ref, *rest):
        if cast_a:
            a_sc = rest[-1]

            @pl.when(pl.program_id(1) == 0)
            def _():
                a_sc[...] = a_ref[...].astype(BF16)
            av = a_sc[...]
        else:
            av = a_ref[...]
        acc = _dot(av, b_ref[...].astype(BF16), dims)
        vals = epi(acc, *[e[...] for e in rest[:ne]]) if epi is not None else (acc,)
        for o_ref, v in zip(rest[ne:ne + no], vals):
            o_ref[...] = v.astype(o_ref.dtype)

    res = pl.pallas_call(
        body, grid=(m // tm, n // tn),
        in_specs=[a_spec, b_spec] + e_specs,
        out_specs=[pl.BlockSpec((tm, tn), lambda i, j: (i, j)) for _ in outs],
        out_shape=[jax.ShapeDtypeStruct((m, n), dt) for dt in outs],
        scratch_shapes=[pltpu.VMEM(a_spec.block_shape, BF16)] if cast_a else [],
        name=name,
    )(a, b, *epi_args)
    return res[0] if no == 1 else res


def _ew(fn, tiled, bcast, outs, reds=(), *, tm=256, name):
    secs = [(t, 0, t.shape[1]) if not isinstance(t, tuple) else t for t in tiled]
    m = secs[0][0].shape[0]
    tm = min(tm, m)
    assert m % tm == 0
    in_specs = []
    for arr, off, w in secs:
        assert off % w == 0
        in_specs.append(pl.BlockSpec((tm, w), functools.partial(lambda i, c: (i, c), c=off // w)))
    in_specs += [pl.BlockSpec(b.shape, lambda i: (0, 0)) for b in bcast]
    nin, no = len(in_specs), len(outs)

    def body(*refs):
        vals = fn(*[r[...] for r in refs[:nin]])
        for r, v in zip(refs[nin:nin + no], vals[:no]):
            r[...] = v.astype(r.dtype)
        i = pl.program_id(0)
        for r, v in zip(refs[nin + no:], vals[no:]):
            @pl.when(i == 0)
            def _():
                r[...] = v

            @pl.when(i > 0)
            def _():
                r[...] += v

    res = pl.pallas_call(
        body, grid=(m // tm,), in_specs=in_specs,
        out_specs=[pl.BlockSpec((tm, c), lambda i: (i, 0)) for c, _ in outs] + [pl.BlockSpec(s, lambda i: (0, 0)) for s in reds],
        out_shape=[jax.ShapeDtypeStruct((m, c), dt) for c, dt in outs] + [jax.ShapeDtypeStruct(s, F32) for s in reds],
        name=name,
    )(*[s[0] for s in secs], *bcast)
    return res


def _cb(fn, cols, params, outs, pouts, *, tc, nblk, name):
    t = cols[0][0].shape[0]
    in_specs = []
    for arr, off in cols:
        assert off % tc == 0
        in_specs.append(pl.BlockSpec((t, tc), functools.partial(lambda c, o: (0, o + c), o=off // tc)))
    for arr, off in params:
        in_specs.append(pl.BlockSpec((arr.shape[0], tc), functools.partial(lambda c, o: (0, o + c), o=off // tc)))
    nin, no = len(in_specs), len(outs)

    def body(*refs):
        vals = fn(*[r[...] for r in refs[:nin]])
        for r, v in zip(refs[nin:], vals):
            r[...] = v.astype(r.dtype)

    return pl.pallas_call(
        body, grid=(nblk,), in_specs=in_specs,
        out_specs=[pl.BlockSpec((t, tc), lambda c: (0, c)) for _ in outs] + [pl.BlockSpec((k, tc), lambda c: (0, c)) for k in pouts],
        out_shape=[jax.ShapeDtypeStruct((t, nblk * tc), dt) for dt in outs] + [jax.ShapeDtypeStruct((k, nblk * tc), F32) for k in pouts],
        name=name,
    )(*[c[0] for c in cols], *[p[0] for p in params])


CUM_BLK = 256


def _prep_point(s, bias, alog):
    lane = lax.broadcasted_iota(jnp.int32, s.shape, 1)
    z = s + bias
    return jnp.where(lane < 8, _log_sigmoid(z),
                     jnp.where(lane < 12, _sigmoid(s),
                               jnp.where(lane < 16, -jnp.exp(alog) * _softplus(z), 0.0)))


def _tri(n, upper):
    r = lax.broadcasted_iota(jnp.int32, (n, n), 0)
    c = lax.broadcasted_iota(jnp.int32, (n, n), 1)
    return (r <= c if upper else r >= c).astype(F32)


def _prep_fwd(proj, bias_row, alog_row, name):
    t = proj.shape[0]
    nb = t // CUM_BLK

    def body(s_ref, b_ref, a_ref, o_ref):
        pre = _prep_point(s_ref[...], b_ref[...], a_ref[...])
        lane = lax.broadcasted_iota(jnp.int32, (CUM_BLK, 128), 1)
        tri = _tri(CUM_BLK, False)
        carry = jnp.zeros((1, 128), F32)
        for blk in range(nb):
            xb = pre[blk * CUM_BLK:(blk + 1) * CUM_BLK]
            cb = _dot(tri, xb, NN, HI) + carry
            carry = cb[CUM_BLK - 1:CUM_BLK]
            o_ref[blk * CUM_BLK:(blk + 1) * CUM_BLK, :] = jnp.where(lane < 8, cb, xb)

    return pl.pallas_call(
        body, grid=(1,),
        in_specs=[pl.BlockSpec((t, 128), lambda i: (0, C_SMALL // 128)), pl.BlockSpec((1, 128), lambda i: (0, 0)),
                  pl.BlockSpec((1, 128), lambda i: (0, 0))],
        out_specs=pl.BlockSpec((t, 128), lambda i: (0, 0)),
        out_shape=jax.ShapeDtypeStruct((t, 128), F32), name=name,
    )(proj, bias_row, alog_row)


def _prep_bwd(proj, bias_row, alog_row, daux, name):
    t = proj.shape[0]
    nb = t // CUM_BLK

    def body(s_ref, b_ref, a_ref, d_ref, ds_ref, db_ref, da_ref, dpre_sc):
        lane = lax.broadcasted_iota(jnp.int32, (CUM_BLK, 128), 1)
        tri = _tri(CUM_BLK, True)
        carry = jnp.zeros((1, 128), F32)
        for blk in reversed(range(nb)):
            db = d_ref[blk * CUM_BLK:(blk + 1) * CUM_BLK, :]
            cb = _dot(tri, db, NN, HI) + carry
            carry = cb[0:1]
            dpre_sc[blk * CUM_BLK:(blk + 1) * CUM_BLK, :] = jnp.where(lane < 8, cb, db)
        _, vjp = jax.vjp(_prep_point, s_ref[...], b_ref[...], a_ref[...])
        ds, dbias, dalog = vjp(dpre_sc[...])
        ds_ref[...] = ds.astype(ds_ref.dtype)
        db_ref[...] = dbias
        da_ref[...] = dalog

    return pl.pallas_call(
        body, grid=(1,),
        in_specs=[pl.BlockSpec((t, 128), lambda i: (0, C_SMALL // 128)), pl.BlockSpec((1, 128), lambda i: (0, 0)),
                  pl.BlockSpec((1, 128), lambda i: (0, 0)), pl.BlockSpec((t, 128), lambda i: (0, 0))],
        out_specs=[pl.BlockSpec((t, 128), lambda i: (0, 0)), pl.BlockSpec((1, 128), lambda i: (0, 0)),
                   pl.BlockSpec((1, 128), lambda i: (0, 0))],
        out_shape=[jax.ShapeDtypeStruct((t, 128), BF16), jax.ShapeDtypeStruct((1, 128), F32), jax.ShapeDtypeStruct((1, 128), F32)],
        scratch_shapes=[pltpu.VMEM((t, 128), F32)], name=name,
    )(proj, bias_row, alog_row, daux)


ATT_TQ = 256
FOX_SCALE = FOX_DH ** -0.5


def _qnorm(q, g):
    return _rms(q, g) * FOX_SCALE


def _att_scores(qn_blk, kn, cfc_blk, cfr, qi, tq, kend):
    s = _dot(qn_blk.astype(BF16), kn[:kend].astype(BF16), NT) + cfc_blk - cfr[:, :kend]
    row = lax.broadcasted_iota(jnp.int32, (tq, kend), 0) + qi * tq
    col = lax.broadcasted_iota(jnp.int32, (tq, kend), 1)
    return s, row >= col


ATT_PAIR = 128 // FOX_DH


def _att_specs(t):
    pair = lambda off: pl.BlockSpec((t, 128), functools.partial(lambda i, o: (0, o + i), o=off // 128))
    gain = pl.BlockSpec((1, FOX_DH), lambda i: (0, 0))
    hd = lambda i: (i, 0, 0)
    col, row = pl.BlockSpec((ATT_PAIR, t, 1), hd), pl.BlockSpec((ATT_PAIR, 1, t), hd)
    return [pair(C_FQ), pair(C_FK), pair(C_FV), gain, gain, col, row], pl.BlockSpec((t, 128), lambda i: (0, i)), col, row


def _att_fwd(proj, qg, kg, cfc, cfr, name):
    t = proj.shape[0]
    tq = min(ATT_TQ, t)
    in_specs, pair_out, col, _ = _att_specs(t)

    def body(q_ref, k_ref, v_ref, qg_ref, kg_ref, cfc_ref, cfr_ref, o_ref, lse_ref):
        for e in range(ATT_PAIR):
            lanes = slice(e * FOX_DH, (e + 1) * FOX_DH)
            qn = _qnorm(q_ref[:, lanes], qg_ref[...])
            kn = _rms(k_ref[:, lanes], kg_ref[...])
            v = v_ref[:, lanes].astype(BF16)
            cfr = cfr_ref[e]
            for qi in range(t // tq):
                kend = (qi + 1) * tq
                rows = slice(qi * tq, kend)
                s, mask = _att_scores(qn[rows], kn, cfc_ref[e, rows, :], cfr, qi, tq, kend)
                s = jnp.where(mask, s, NEG)
                m = jnp.max(s, axis=1, keepdims=True)
                p = jnp.exp(s - m)
                l = jnp.sum(p, axis=1, keepdims=True)
                o_ref[rows, lanes] = _dot(p.astype(BF16), v[:kend], NN) / l
                lse_ref[e, rows, :] = m + jnp.log(l)

    return pl.pallas_call(
        body, grid=(FOX_HEADS // ATT_PAIR,), in_specs=in_specs, out_specs=[pair_out, col],
        out_shape=[jax.ShapeDtypeStruct((t, BW), F32), jax.ShapeDtypeStruct((FOX_HEADS, t, 1), F32)], name=name,
    )(proj, proj, proj, qg, kg, cfc, cfr)


def _att_bwd(proj, qg, kg, cfc, cfr, lse, o, do, name):
    t = proj.shape[0]
    tq = min(ATT_TQ, t)
    in_specs, pair_out, col, row = _att_specs(t)

    def body(q_ref, k_ref, v_ref, qg_ref, kg_ref, cfc_ref, cfr_ref, lse_ref, o_ref, do_ref,
             dq_ref, dk_ref, dv_ref, dcfc_ref, dcfr_ref, dqg_ref, dkg_ref, dqn_sc, dkn_sc, dv_sc, dcfr_sc):
        for e in range(ATT_PAIR):
            lanes = slice(e * FOX_DH, (e + 1) * FOX_DH)
            qn, vjp_q = jax.vjp(_qnorm, q_ref[:, lanes], qg_ref[...])
            kn, vjp_k = jax.vjp(_rms, k_ref[:, lanes], kg_ref[...])
            v = v_ref[:, lanes].astype(BF16)
            cfr = cfr_ref[e]
            do_e = do_ref[:, lanes]
            delta = jnp.sum(do_e * o_ref[:, lanes], axis=1, keepdims=True)
            dkn_sc[...] = jnp.zeros_like(dkn_sc)
            dv_sc[...] = jnp.zeros_like(dv_sc)
            dcfr_sc[...] = jnp.zeros_like(dcfr_sc)
            for qi in range(t // tq):
                kend = (qi + 1) * tq
                rows = slice(qi * tq, kend)
                s, mask = _att_scores(qn[rows], kn, cfc_ref[e, rows, :], cfr, qi, tq, kend)
                p = jnp.where(mask, jnp.exp(jnp.where(mask, s, NEG) - lse_ref[e, rows, :]), 0.0)
                do_b = do_e[rows].astype(BF16)
                dv_sc[0:kend, :] += _dot(p.astype(BF16), do_b, TN)
                dp = _dot(do_b, v[:kend], NT)
                ds = p * (dp - delta[rows])
                ds_b = ds.astype(BF16)
                dqn_sc[rows, :] = _dot(ds_b, kn[:kend].astype(BF16), NN)
                dkn_sc[0:kend, :] += _dot(ds_b, qn[rows].astype(BF16), TN)
                dcfc_ref[e, rows, :] = jnp.sum(ds, axis=1, keepdims=True)
                dcfr_sc[:, 0:kend] -= jnp.sum(ds, axis=0, keepdims=True)
            dq, dqg = vjp_q(dqn_sc[...])
            dk, dkg = vjp_k(dkn_sc[...])
            dq_ref[:, lanes] = dq.astype(dq_ref.dtype)
            dk_ref[:, lanes] = dk.astype(dk_ref.dtype)
            dv_ref[:, lanes] = dv_sc[...].astype(dv_ref.dtype)
            dcfr_ref[e] = dcfr_sc[...]
            dqg_ref[e] = dqg
            dkg_ref[e] = dkg

    gsp = pl.BlockSpec((ATT_PAIR, 1, FOX_DH), lambda i: (i, 0, 0))
    return pl.pallas_call(
        body, grid=(FOX_HEADS // ATT_PAIR,),
        in_specs=in_specs + [col, pair_out, pair_out],
        out_specs=[pair_out] * 3 + [col, row, gsp, gsp],
        out_shape=[jax.ShapeDtypeStruct((t, BW), BF16)] * 3
        + [jax.ShapeDtypeStruct((FOX_HEADS, t, 1), F32), jax.ShapeDtypeStruct((FOX_HEADS, 1, t), F32)]
        + [jax.ShapeDtypeStruct((FOX_HEADS, 1, FOX_DH), F32)] * 2,
        scratch_shapes=[pltpu.VMEM((t, FOX_DH), F32)] * 3 + [pltpu.VMEM((1, t), F32)], name=name,
    )(proj, proj, proj, qg, kg, cfc, cfr, lse, o, do)


DN_SCALE = DN_DH ** -0.5


DN_BATCH = 8


@functools.partial(jax.custom_vjp, nondiff_argnums=(2, 3))
def _mm3(a, b, dims, batch=False):
    return _mm3_passes(a, b, dims, batch)


def _mm3_fwd(a, b, dims, batch):
    return _mm3_passes(a, b, dims, batch), (a, b)


def _mm3_bwd(dims, batch, res, dc):
    a, b = res
    if dims == NN:
        return _mm3_passes(dc, b, NT, batch), _mm3_passes(a, dc, TN, batch)
    if dims == NT:
        return _mm3_passes(dc, b, NN, batch), _mm3_passes(dc, a, TN, batch)
    return _mm3_passes(b, dc, NT, batch), _mm3_passes(a, dc, NN, batch)


_mm3.defvjp(_mm3_fwd, _mm3_bwd)


def _mm3_passes(a, b, dims, batch):
    if batch:
        dn = (((dims[0][0] + 1,), (dims[1][0] + 1,)), ((0,), (0,)))
        dot = lambda p, q: lax.dot_general(p, q, dn, preferred_element_type=F32)
    else:
        dot = lambda p, q: _dot(p, q, dims)
    ah, bh = a.astype(BF16), b.astype(BF16)
    al, bl = (a - ah.astype(F32)).astype(BF16), (b - bh.astype(F32)).astype(BF16)
    return dot(ah, bh) + (dot(ah, bl) + dot(al, bh))


def _dn_local(qc, kc, vc, g, beta):
    nb, c, _ = qc.shape
    ii = lax.broadcasted_iota(jnp.int32, (c, c), 0)
    jj = lax.broadcasted_iota(jnp.int32, (c, c), 1)
    incl, strict = ii >= jj, ii > jj
    lower = jnp.broadcast_to(incl.astype(F32), (nb, c, c))
    eye = (ii == jj).astype(F32)
    mm = functools.partial(_mm3, batch=True)
    dm = mm(lower, jnp.where(strict, g, 0.0), NN)
    decay = jnp.where(incl, jnp.exp(jnp.where(incl, dm, 0.0)), 0.0)
    gcum = mm(lower, g * jnp.ones((1, 1, DN_DH), F32), NN)
    eg = jnp.exp(gcum)
    glast = gcum[:, c - 1:c]
    kb = kc * beta
    n1 = jnp.where(strict, mm(kb, kc, NT) * decay, 0.0)
    inv = eye - n1
    pw = n1
    for _ in range(5):
        pw = mm(pw, pw, NN)
        inv = inv + mm(pw, inv, NN)
    sol = mm(inv, jnp.concatenate([vc * beta, kb * eg], axis=2), NN)
    qk = jnp.where(incl, mm(qc, kc, NT) * decay, 0.0)
    return sol[:, :, :DN_DH], sol[:, :, DN_DH:], qk, qc * eg, kc * jnp.exp(glast - gcum), jnp.exp(glast)


@functools.partial(jax.custom_vjp, nondiff_argnums=(2,))
def _mm1(a, b, dims):
    return _dot(a.astype(BF16), b.astype(BF16), dims)


def _mm1_fwd(a, b, dims):
    return _mm1(a, b, dims), (a, b)


def _mm1_bwd(dims, res, dc):
    a, b = res
    if dims == NN:
        return _mm1(dc, b, NT), _mm1(a, dc, TN)
    if dims == NT:
        return _mm1(dc, b, NN), _mm1(dc, a, TN)
    return _mm1(b, dc, NT), _mm1(a, dc, NN)


_mm1.defvjp(_mm1_fwd, _mm1_bwd)


def _dn_state(u, kcum, qk, qdec, kdec, egl, state):
    v_new = u - _mm1(kcum, state, NN)
    out = _mm1(qdec, state, NN) + _mm1(qk, v_new, NN)
    return out, state * egl + _mm1(kdec, v_new, TN)


def _dn_pre_q(c):
    return _l2(_silu(c)) * DN_SCALE


def _dn_pre_k(c):
    return _l2(_silu(c))


def _dn_post(o, z, ng):
    return _rms(o, ng) * _silu(z)


def _dn_specs(t):
    cblk = lambda o: pl.BlockSpec((t, DN_DH), functools.partial(lambda h, o: (0, o + h), o=o // DN_DH))
    wblk = lambda o: pl.BlockSpec((4, DN_DH), functools.partial(lambda h, o: (0, o + h), o=o // DN_DH))
    proj_specs = [cblk(C_DQ), cblk(C_DK), cblk(C_DV), cblk(C_DZ)]
    w_specs = [wblk(0), wblk(BW), wblk(2 * BW)]
    gb_spec = pl.BlockSpec((1, t, 2), lambda h: (h, 0, 0))
    return proj_specs, w_specs, gb_spec


def _chunk_rows(n, count=1):
    return pl.ds(pl.multiple_of(n * DN_CHUNK, DN_CHUNK), count * DN_CHUNK)


def _egl_rows(n, count=1):
    return pl.ds(pl.multiple_of(n * 8, 8), count * 8)


def _dn_local_inputs(n, qn_sc, kn_sc, vv_sc, gb_ref):
    r = _chunk_rows(n, DN_BATCH)
    split = lambda v: v.reshape(DN_BATCH, DN_CHUNK, v.shape[-1])
    gbv = split(gb_ref[0, r, :])
    return split(qn_sc[r, :]), split(kn_sc[r, :]), split(vv_sc[r, :]), gbv[:, :, 0:1], gbv[:, :, 1:2]


def _dn_local_phase(nc, qn_sc, kn_sc, vv_sc, gb_ref, loc):
    def step(i, carry):
        n = i * DN_BATCH
        vals = _dn_local(*_dn_local_inputs(n, qn_sc, kn_sc, vv_sc, gb_ref))
        for sc, val in zip(loc[:5], vals[:5]):
            sc[_chunk_rows(n, DN_BATCH), :] = val.reshape(DN_BATCH * DN_CHUNK, val.shape[-1])
        loc[5][_egl_rows(n, DN_BATCH), :] = jnp.broadcast_to(vals[5], (DN_BATCH, 8, DN_DH)).reshape(DN_BATCH * 8, DN_DH)
        return carry

    lax.fori_loop(0, nc // DN_BATCH, step, 0)


def _dn_loc_scratch(t, nc):
    big = pltpu.VMEM((t, DN_DH), F32)
    return [big, big, pltpu.VMEM((t, DN_CHUNK), F32), big, big, pltpu.VMEM((nc * 8, DN_DH), F32)]


def _dn_fwd(proj, conv_w, gb, ng, name):
    t = proj.shape[0]
    nc = t // DN_CHUNK
    assert nc % DN_BATCH == 0
    proj_specs, w_specs, gb_spec = _dn_specs(t)

    def body(q_ref, k_ref, v_ref, z_ref, wq_ref, wk_ref, wv_ref, gb_ref, ng_ref, y_ref, o_ref, st_ref, qn_sc, kn_sc, vv_sc, *loc):
        qn_sc[...] = _dn_pre_q(_conv(q_ref[...], wq_ref[...]))
        kn_sc[...] = _dn_pre_k(_conv(k_ref[...], wk_ref[...]))
        vv_sc[...] = _silu(_conv(v_ref[...], wv_ref[...]))
        _dn_local_phase(nc, qn_sc, kn_sc, vv_sc, gb_ref, loc)
        u_sc, kcum_sc, qk_sc, qdec_sc, kdec_sc, egl_sc = loc

        def chunk(n, state):
            r = _chunk_rows(n)
            egl = egl_sc[_egl_rows(n), :][0:1]
            out, new_state = _dn_state(u_sc[r, :], kcum_sc[r, :], qk_sc[r, :], qdec_sc[r, :], kdec_sc[r, :], egl, state)
            st_ref[0, n] = state
            o_ref[r, :] = out
            return new_state

        lax.fori_loop(0, nc, chunk, jnp.zeros((DN_DH, DN_DH), F32))
        y_ref[...] = _dn_post(o_ref[...], z_ref[...], ng_ref[...])

    hblk = pl.BlockSpec((t, DN_DH), lambda h: (0, h))
    return pl.pallas_call(
        body, grid=(DN_HEADS,),
        in_specs=proj_specs + w_specs + [gb_spec, pl.BlockSpec((1, DN_DH), lambda h: (0, 0))],
        out_specs=[hblk, hblk, pl.BlockSpec((1, nc, DN_DH, DN_DH), lambda h: (h, 0, 0, 0))],
        out_shape=[jax.ShapeDtypeStruct((t, BW), F32), jax.ShapeDtypeStruct((t, BW), F32),
                   jax.ShapeDtypeStruct((DN_HEADS, nc, DN_DH, DN_DH), F32)],
        scratch_shapes=[pltpu.VMEM((t, DN_DH), F32)] * 3 + _dn_loc_scratch(t, nc), name=name,
    )(proj, proj, proj, proj, conv_w, conv_w, conv_w, gb, ng)


def _dn_bwd(proj, conv_w, gb, ng, o, states, dy, name):
    t = proj.shape[0]
    nc = t // DN_CHUNK
    proj_specs, w_specs, gb_spec = _dn_specs(t)
    nloc = 6

    def body(q_ref, k_ref, v_ref, z_ref, wq_ref, wk_ref, wv_ref, gb_ref, ng_ref, o_ref, st_ref, dy_ref,
             dq_ref, dk_ref, dv_ref, dz_ref, dwq_ref, dwk_ref, dwv_ref, dgb_ref, dng_ref,
             qn_sc, kn_sc, vv_sc, do_sc, *rest):
        loc, dloc = rest[:nloc], rest[nloc:]
        qn_sc[...] = _dn_pre_q(_conv(q_ref[...], wq_ref[...]))
        kn_sc[...] = _dn_pre_k(_conv(k_ref[...], wk_ref[...]))
        vv_sc[...] = _silu(_conv(v_ref[...], wv_ref[...]))
        _, vjp_y = jax.vjp(_dn_post, o_ref[...], z_ref[...], ng_ref[...])
        do, dz, dng = vjp_y(dy_ref[...])
        do_sc[...] = do
        dz_ref[...] = dz.astype(dz_ref.dtype)
        dng_ref[0] = dng
        _dn_local_phase(nc, qn_sc, kn_sc, vv_sc, gb_ref, loc)
        u_sc, kcum_sc, qk_sc, qdec_sc, kdec_sc, egl_sc = loc

        def state_bwd(i, dstate):
            n = nc - 1 - i
            r = _chunk_rows(n)
            r8 = _egl_rows(n)
            _, vjp = jax.vjp(_dn_state, u_sc[r, :], kcum_sc[r, :], qk_sc[r, :], qdec_sc[r, :], kdec_sc[r, :],
                             egl_sc[r8, :][0:1], st_ref[0, n])
            du, dkcum, dqk, dqdec, dkdec, degl, dprev = vjp((do_sc[r, :], dstate))
            for d_sc, val in zip(dloc[:5], (du, dkcum, dqk, dqdec, dkdec)):
                d_sc[r, :] = val
            dloc[5][r8, :] = jnp.broadcast_to(degl, (8, DN_DH))
            return dprev

        lax.fori_loop(0, nc, state_bwd, jnp.zeros((DN_DH, DN_DH), F32))

        def local_bwd(i, carry):
            n = i * DN_BATCH
            r = _chunk_rows(n, DN_BATCH)
            _, vjp = jax.vjp(_dn_local, *_dn_local_inputs(n, qn_sc, kn_sc, vv_sc, gb_ref))
            cts = tuple(d_sc[r, :].reshape(DN_BATCH, DN_CHUNK, d_sc.shape[-1]) for d_sc in dloc[:5])
            cts += (dloc[5][_egl_rows(n, DN_BATCH), :].reshape(DN_BATCH, 8, DN_DH)[:, 0:1],)
            dqc, dkc, dvc, dg, dbeta = vjp(cts)
            for d_sc, val in zip((dloc[0], dloc[1], dloc[3]), (dqc, dkc, dvc)):
                d_sc[r, :] = val.reshape(DN_BATCH * DN_CHUNK, DN_DH)
            dgb_ref[0, r, :] = jnp.concatenate([dg, dbeta], axis=2).reshape(DN_BATCH * DN_CHUNK, 2)
            return carry

        lax.fori_loop(0, nc // DN_BATCH, local_bwd, 0)
        for x_ref, w_ref, pre, d_sc, dx_ref, dw_ref in ((q_ref, wq_ref, _dn_pre_q, dloc[0], dq_ref, dwq_ref),
                                                       (k_ref, wk_ref, _dn_pre_k, dloc[1], dk_ref, dwk_ref),
                                                       (v_ref, wv_ref, _silu, dloc[3], dv_ref, dwv_ref)):
            _, vjp = jax.vjp(pre, _conv(x_ref[...], w_ref[...]))
            (dc,) = vjp(d_sc[...])
            dx, dws = _conv_bwd(x_ref[...], w_ref[...], dc)
            dx_ref[...] = dx.astype(dx_ref.dtype)
            for j, dw in enumerate(dws):
                dw_ref[j:j + 1, :] = dw

    hblk = pl.BlockSpec((t, DN_DH), lambda h: (0, h))
    wout = pl.BlockSpec((4, DN_DH), lambda h: (0, h))
    return pl.pallas_call(
        body, grid=(DN_HEADS,),
        in_specs=proj_specs + w_specs + [gb_spec, pl.BlockSpec((1, DN_DH), lambda h: (0, 0)), hblk,
                                         pl.BlockSpec((1, nc, DN_DH, DN_DH), lambda h: (h, 0, 0, 0)), hblk],
        out_specs=[hblk] * 4 + [wout] * 3 + [gb_spec, pl.BlockSpec((1, 1, DN_DH), lambda h: (h, 0, 0))],
        out_shape=[jax.ShapeDtypeStruct((t, BW), BF16)] * 4 + [jax.ShapeDtypeStruct((4, BW), F32)] * 3
        + [jax.ShapeDtypeStruct((DN_HEADS, t, 2), F32), jax.ShapeDtypeStruct((DN_HEADS, 1, DN_DH), F32)],
        scratch_shapes=[pltpu.VMEM((t, DN_DH), F32)] * 4 + _dn_loc_scratch(t, nc) * 2, name=name,
    )(proj, proj, proj, proj, conv_w, conv_w, conv_w, gb, ng, o, states, dy)


MERGE_TM, MERGE_TN = 512, 512


def _merge_specs(t):
    tm, tn = min(MERGE_TM, t), MERGE_TN
    y_spec = pl.BlockSpec((tm, BW), lambda i, j: (i, 0))
    w_spec = pl.BlockSpec((3, BW, tn), lambda i, j: (0, 0, j))
    gate_specs = [pl.BlockSpec((tm, tn), functools.partial(lambda i, j, o: (i, o + j), o=(C_GATE + n * D_MODEL) // tn))
                  for n in range(3)]
    return tm, tn, [y_spec] * 3 + [w_spec] + gate_specs


def _merge_fwd(ys, wb, proj, name):
    t = proj.shape[0]
    tm, tn, in_specs = _merge_specs(t)

    def body(y0, y1, y2, w_ref, g0, g1, g2, o_ref):
        acc = jnp.zeros((tm, tn), F32)
        for n, (y, g) in enumerate(((y0, g0), (y1, g1), (y2, g2))):
            acc = acc + _dot(y[...].astype(BF16), w_ref[n], NN) * _sigmoid(g[...])
        o_ref[...] = acc.astype(o_ref.dtype)

    return pl.pallas_call(
        body, grid=(t // tm, D_MODEL // tn), in_specs=in_specs,
        out_specs=pl.BlockSpec((tm, tn), lambda i, j: (i, j)),
        out_shape=jax.ShapeDtypeStruct((t, D_MODEL), BF16), name=name,
    )(*ys, wb, proj, proj, proj)


def _merge_bwd(ys, wb, proj, dmerged, name):
    t = proj.shape[0]
    tm, tn, in_specs = _merge_specs(t)

    def body(y0, y1, y2, w_ref, g0, g1, g2, dm_ref, dg_ref, dt_ref):
        dm = dm_ref[...]
        for n, (y, g) in enumerate(((y0, g0), (y1, g1), (y2, g2))):
            tn_ = _dot(y[...].astype(BF16), w_ref[n], NN)
            sg = _sigmoid(g[...])
            dg_ref[n] = (dm * tn_ * sg * (1.0 - sg)).astype(dg_ref.dtype)
            dt_ref[n] = (dm * sg).astype(dt_ref.dtype)

    o3 = pl.BlockSpec((3, tm, tn), lambda i, j: (0, i, j))
    return pl.pallas_call(
        body, grid=(t // tm, D_MODEL // tn), in_specs=in_specs + [pl.BlockSpec((tm, tn), lambda i, j: (i, j))],
        out_specs=[o3, o3], out_shape=[jax.ShapeDtypeStruct((3, t, D_MODEL), BF16)] * 2, name=name,
    )(*ys, wb, proj, proj, proj, dmerged)


def _prep_rows(sp):
    z4, z112 = jnp.zeros((1, 4), F32), jnp.zeros((1, 112), F32)
    bias_row = jnp.concatenate([sp['b_fox_f'], z4, sp['dn_dt_bias'], z112], axis=1)
    alog_row = jnp.concatenate([jnp.zeros((1, 12), F32), sp['dn_a_log'], z112], axis=1)
    return bias_row, alog_row


def _sc_fwd_tile(sb, sc, sv, w):
    return (sb * _conv(sc * sv, w),)


def _ffn_act(ug, uv):
    return _silu(ug) * uv


W_IN_COLS = [(C_FQ, 0, 1536), (C_SB, 1544, 1536), (C_DQ, 3080, 1536), (C_DZ, 4624, 512), (C_GATE, 5136, 3072),
             (C_SMALL, 1536, 8), (C_SMALL + 8, 4616, 8)]
W_IN_SHARD = IN_ORIG // N_DEV


def _w_in_segments():
    out = []
    for d, o, w in W_IN_COLS:
        end = o + w
        while o < end:
            k = o // W_IN_SHARD
            n = min(end, (k + 1) * W_IN_SHARD) - o
            out.append((d, k, o - k * W_IN_SHARD, n))
            o, d = o + n, d + n
    return out


def _w_in_assemble(g, name):
    tm = 256

    def body(g_ref, o_ref):
        for d, k, s, n in _w_in_segments():
            o_ref[:, d:d + n] = g_ref[k, :, s:s + n]
        o_ref[:, IN_ORIG:IN_P] = jnp.zeros((tm, IN_P - IN_ORIG), o_ref.dtype)

    return pl.pallas_call(
        body, grid=(D_MODEL // tm,),
        in_specs=[pl.BlockSpec((N_DEV, tm, W_IN_SHARD), lambda i: (0, i, 0))],
        out_specs=pl.BlockSpec((tm, IN_P), lambda i: (i, 0)),
        out_shape=jax.ShapeDtypeStruct((D_MODEL, IN_P), g.dtype), name=name,
    )(g)


W_IN_SPLIT_TM = 128


def _w_in_split_by_device(g, name):
    tm = W_IN_SPLIT_TM

    def body(g_ref, own_ref, send_ref):
        me = 4 * lax.axis_index("x") + 2 * lax.axis_index("y") + lax.axis_index("c")
        for d, k, s, n in _w_in_segments():
            val = g_ref[:, d:d + n]
            send_ref[k, :, s:s + n] = val.astype(send_ref.dtype)

            @pl.when(me == k)
            def _():
                own_ref[:, s:s + n] = val

    return pl.pallas_call(
        body, grid=(D_MODEL // tm,), in_specs=[pl.BlockSpec((tm, IN_P), lambda i: (i, 0))],
        out_specs=[pl.BlockSpec((tm, W_IN_SHARD), lambda i: (i, 0)), pl.BlockSpec((N_DEV, tm, W_IN_SHARD), lambda i: (0, i, 0))],
        out_shape=[jax.ShapeDtypeStruct((D_MODEL, W_IN_SHARD), F32), jax.ShapeDtypeStruct((N_DEV, D_MODEL, W_IN_SHARD), BF16)],
        name=name,
    )(g)


def _layer_fwd(x, p_i, w, sp, tag, late_weights=None):
    t = x.shape[0]
    sv = {'x': x}
    (hn,) = _ew(lambda a, g: (_rms(a, g),), [x], [sp['g_mix']], [(D_MODEL, BF16)], name=f'rms_mix_{tag}')
    proj = _mm(hn, w['w_in'], 'nn', [F32], name=f'in_proj_{tag}')
    bias_row, alog_row = _prep_rows(sp)
    aux = _prep_fwd(proj, bias_row, alog_row, f'prep_{tag}')
    cf = aux[:, :FOX_HEADS].T
    cfc, cfr = cf[:, :, None], cf[:, None, :]
    y_fox, lse = _att_fwd(proj, sp['fox_q_gain'], sp['fox_k_gain'], cfc, cfr, f'fox_fwd_{tag}')
    (y_sc,) = _cb(_sc_fwd_tile, [(proj, C_SB), (proj, C_SC), (proj, C_SV)], [(w['sc_conv_w'], 0)], [F32], [],
                  tc=256, nblk=2, name=f'sc_fwd_{tag}')
    gb = jnp.stack([aux[:, 12:16].T, aux[:, 8:12].T], axis=-1)
    y_dn, o_dn, states = _dn_fwd(proj, w['dn_conv_w'], gb, sp['dn_norm_gain'], f'dn_fwd_{tag}')
    ys = (y_fox, y_sc, y_dn)
    if late_weights is not None:
        w = {**w, **late_weights(y_dn)}
    merged = _merge_fwd(ys, w['w_branch'], proj, f'merge_fwd_{tag}')
    x1 = _mm(merged, w['w_o'], 'nn', [F32], epi=lambda acc, r: (acc + r,), epi_args=(x,), name=f'o_proj_{tag}')
    (hf,) = _ew(lambda a, g: (_rms(a, g),), [x1], [sp['g_ffn']], [(D_MODEL, BF16)], name=f'rms_ffn_{tag}')
    up = _mm(hf, w['w_up'], 'nt', [F32], name=f'up_proj_{tag}')
    (act,) = _cb(lambda ug, uv, wg, wv: (_ffn_act(_conv(ug, wg), _conv(uv, wv)),), [(up, 0), (up, D_FF)],
                 [(w['ffn_conv_w'], 0), (w['ffn_conv_w'], D_FF)], [BF16], [], tc=256, nblk=D_FF // 256, name=f'ffn_act_{tag}')
    x2 = _mm(act, w['w_down'], 'nn', [F32], epi=lambda acc, r: (acc + r,), epi_args=(x1,), name=f'down_proj_{tag}')
    (hp,) = _ew(lambda a, g: (_rms(a, g),), [x2], [sp['g_ple']], [(D_MODEL, BF16)], name=f'rms_ple_{tag}')
    gp = _mm(hp, w['w_ple_gate'], 'nn', [F32], name=f'ple_gate_{tag}')
    x3 = _mm(p_i, w['w_ple'], 'nn', [F32], epi=lambda acc, g, r: (r + _sigmoid(g) * acc,), epi_args=(gp, x2), name=f'ple_{tag}')
    sv.update(hn=hn, proj=proj, aux=aux, cfc=cfc, cfr=cfr, lse=lse, ys=ys, gb=gb, o_dn=o_dn,
              states=states, merged=merged, x1=x1, hf=hf, up=up, act=act, x2=x2, hp=hp, gp=gp, p=p_i,
              bias_row=bias_row, alog_row=alog_row, w=w)
    return x3, sv


def _rms_bwd(x, g, dh, dres, name):
    def fn(xv, dhv, dr, gv):
        _, vjp = jax.vjp(_rms, xv, gv)
        dx, dg = vjp(dhv)
        return dr + dx, dg
    return _ew(fn, [x, dh, dres], [g], [(D_MODEL, F32)], [(1, D_MODEL)], name=name)


def _layer_bwd(dx3, sv, sp, tag, early_grads=None, last_grad=None):
    t = dx3.shape[0]
    w = sv['w']
    g = {}
    def ple_epi(acc, gpv, d):
        s = _sigmoid(gpv)
        return d * acc * s * (1.0 - s), d * s
    dgp, de = _mm(sv['p'], w['w_ple'], 'nn', [BF16, BF16], epi=ple_epi, epi_args=(sv['gp'], dx3), name=f'ple_bwd_{tag}')
    g['w_ple'] = _mm(sv['p'], de, 'tn', [F32], name=f'd_w_ple_{tag}')
    g['w_ple_gate'] = _mm(sv['hp'], dgp, 'tn', [F32], name=f'd_w_ple_gate_{tag}')
    dhp = _mm(dgp, w['w_ple_gate'], 'nt', [F32], name=f'd_hp_{tag}')
    dx2, g['g_ple'] = _rms_bwd(sv['x2'], sp['g_ple'], dhp, dx3, f'rms_ple_bwd_{tag}')
    dact = _mm(dx2, w['w_down'], 'nt', [F32], name=f'd_act_{tag}')
    g['w_down'] = _mm(sv['act'], dx2, 'tn', [F32], name=f'd_w_down_{tag}')

    def ffn_bwd_tile(ug, uv, da, wg, wv):
        cg, cv = _conv(ug, wg), _conv(uv, wv)
        _, vjp = jax.vjp(_ffn_act, cg, cv)
        dcg, dcv = vjp(da)
        dug, dwg = _conv_bwd(ug, wg, dcg)
        duv, dwv = _conv_bwd(uv, wv, dcv)
        return dug, duv, jnp.concatenate(dwg, axis=0), jnp.concatenate(dwv, axis=0)
    dupg, dupv, dwg, dwv = _cb(ffn_bwd_tile, [(sv['up'], 0), (sv['up'], D_FF), (dact, 0)],
                               [(w['ffn_conv_w'], 0), (w['ffn_conv_w'], D_FF)], [BF16, BF16], [3, 3], tc=256,
                               nblk=D_FF // 256, name=f'ffn_act_bwd_{tag}')
    dup = jnp.concatenate([dupg, dupv], axis=1)
    g['ffn_conv_w'] = jnp.concatenate([dwg, dwv], axis=1)
    g['w_up'] = _mm(dup, sv['hf'], 'tn', [F32], name=f'd_w_up_{tag}')
    dhf = _mm(dup, w['w_up'], 'nn', [F32], name=f'd_hf_{tag}')
    dx1, g['g_ffn'] = _rms_bwd(sv['x1'], sp['g_ffn'], dhf, dx2, f'rms_ffn_bwd_{tag}')
    dmerged = _mm(dx1, w['w_o'], 'nt', [F32], name=f'd_merged_{tag}')
    g['w_o'] = _mm(sv['merged'], dx1, 'tn', [F32], name=f'd_w_o_{tag}')
    dgate, dtn = _merge_bwd(sv['ys'], w['w_branch'], sv['proj'], dmerged, f'merge_bwd_{tag}')
    dys, dwb = [], []
    for n in range(3):
        dys.append(_mm(dtn[n], w['w_branch'][n], 'nt', [F32], name=f'd_y{n}_{tag}'))
        dwb.append(_mm(sv['ys'][n], dtn[n], 'tn', [F32], name=f'd_w_branch{n}_{tag}'))
    g['w_branch'] = jnp.stack(dwb)
    if early_grads is not None:
        sp = early_grads(g, sp)
    ddq, ddk, ddv, ddz, dwq, dwk, dwv_, dgb, dng = _dn_bwd(sv['proj'], w['dn_conv_w'], sv['gb'], sp['dn_norm_gain'],
                                                           sv['o_dn'], sv['states'], dys[2], f'dn_bwd_{tag}')
    g['dn_conv_w'] = jnp.concatenate([dwq, dwk, dwv_], axis=1)
    g['dn_norm_gain'] = jnp.sum(dng, axis=0)
    def sc_bwd_tile(sb, sc, svv, dy, wv):
        u = sc * svv
        dsb = dy * _conv(u, wv)
        du, dws = _conv_bwd(u, wv, dy * sb)
        return dsb, du * svv, du * sc, jnp.concatenate(dws, axis=0)
    dsb, dsc, dsv, g['sc_conv_w'] = _cb(sc_bwd_tile, [(sv['proj'], C_SB), (sv['proj'], C_SC), (sv['proj'], C_SV), (dys[1], 0)],
                                        [(w['sc_conv_w'], 0)], [BF16, BF16, BF16], [3], tc=256, nblk=2, name=f'sc_bwd_{tag}')
    dfq, dfk, dfv, dcfc, dcfr, dqg, dkg = _att_bwd(sv['proj'], sp['fox_q_gain'], sp['fox_k_gain'], sv['cfc'], sv['cfr'],
                                                   sv['lse'], sv['ys'][0], dys[0], f'fox_bwd_{tag}')
    g['fox_q_gain'] = jnp.sum(dqg, axis=0)
    g['fox_k_gain'] = jnp.sum(dkg, axis=0)
    dcf = (dcfc[:, :, 0] + dcfr[:, 0, :]).T
    daux = jnp.concatenate([dcf, dgb[:, :, 1].T, dgb[:, :, 0].T, jnp.zeros((t, 112), F32)], axis=1)
    dsmall, dbias, dalog = _prep_bwd(sv['proj'], sv['bias_row'], sv['alog_row'], daux, f'prep_bwd_{tag}')
    g['b_fox_f'] = dbias[:, 0:8]
    g['dn_dt_bias'] = dbias[:, 12:16]
    g['dn_a_log'] = dalog[:, 12:16]
    dproj = jnp.concatenate([dfq, dfk, dfv, dsb, dsc, dsv, ddq, ddk, ddv, ddz, dgate[0], dgate[1], dgate[2], dsmall], axis=1)
    g['w_in'] = _mm(sv['hn'], dproj, 'tn', [F32], name=f'd_w_in_{tag}')
    if last_grad is not None:
        sp = last_grad(g, sp)
    dhn = _mm(dproj, w['w_in'], 'nt', [F32], name=f'd_hn_{tag}')
    dx, g['g_mix'] = _rms_bwd(sv['x'], sp['g_mix'], dhn, dx1, f'rms_mix_bwd_{tag}')
    return dx, g


def _loss_bwd(y, target, name):
    inv = 1.0 / y.shape[1]

    def fn(yv, tv):
        err = yv - tv
        return err * inv, jnp.zeros((8, 128), F32) + 0.5 * inv * jnp.sum(err * err)
    return _ew(fn, [y, target], [], [(y.shape[1], F32)], [(8, 128)], name=name)


PACK_W = 1024
FULL_SHAPE = {'w_in': (D_MODEL, IN_ORIG), 'w_branch': (3, BW, D_MODEL), 'w_o': (D_MODEL, D_MODEL), 'w_up': (2 * D_FF, D_MODEL),
              'w_down': (D_FF, D_MODEL), 'w_ple_gate': (D_MODEL, D_MODEL), 'w_ple': (PLE_DIM, D_MODEL),
              'sc_conv_w': (3, BW), 'dn_conv_w': (4, 3 * BW), 'ffn_conv_w': (3, 2 * D_FF)}
SMALL_SHAPE = {'g_mix': D_MODEL, 'b_fox_f': FOX_HEADS, 'fox_q_gain': FOX_DH, 'fox_k_gain': FOX_DH, 'dn_a_log': DN_HEADS,
               'dn_dt_bias': DN_HEADS, 'dn_norm_gain': DN_DH, 'g_ffn': D_MODEL, 'g_ple': D_MODEL}


def _shard_shape(name):
    s = list(FULL_SHAPE[name])
    s[SHARDED[name]] //= N_DEV
    return tuple(s)


def _full_from_gathered(g, name):
    sh, ax = _shard_shape(name), SHARDED[name]
    blocks = jnp.moveaxis(g, 0, ax)
    return blocks.reshape(sh[:ax] + (N_DEV * sh[ax],) + sh[ax + 1:])


def _by_dest(full, name):
    sh, ax = _shard_shape(name), SHARDED[name]
    return jnp.moveaxis(full.reshape(sh[:ax] + (N_DEV, sh[ax]) + sh[ax + 1:]), ax, 0)


def _flat_pack(arrs):
    flat = jnp.concatenate([a.reshape(-1).astype(F32) for a in arrs])
    rows = -(-flat.shape[0] // (8 * PACK_W)) * 8
    return jnp.pad(flat, (0, rows * PACK_W - flat.shape[0])).reshape(rows, PACK_W)


def _flat_unpack(pack, shapes):
    flat, out, off = pack.reshape(-1), [], 0
    for s in shapes:
        n = 1
        for d in s:
            n *= d
        out.append(flat[off:off + n].reshape(s))
        off += n
    return out


SMALL_SHAPES = [(DEPTH, SMALL_SHAPE[n]) for n in SMALL]
SMALL_LOSS_AT = sum(DEPTH * SMALL_SHAPE[n] for n in SMALL)
CONV_SHARD_SHAPES = [(DEPTH,) + _shard_shape(n) for n in CONVW]
CONV_FULL_SHAPES = [(DEPTH,) + FULL_SHAPE[n] for n in CONVW]


def _pick_tm(m, width):
    best = None
    for tm in range(16, m + 1, 16):
        if m % tm == 0 and tm * width * 4 <= (1 << 20):
            best = tm
    return best if best is not None else m


HBM_SPEC = pl.BlockSpec(memory_space=pltpu.HBM)


def _place():
    x, y, c = lax.axis_index("x"), lax.axis_index("y"), lax.axis_index("c")
    return x, y, c, [(1 - x, y), (x, 1 - y), (1 - x, 1 - y)]


def _all_gather(arrs, name):
    n = len(arrs)

    def body(*refs):
        ins, outs = refs[:n], refs[n:2 * n]
        send_sems, recv_sems = refs[2 * n:]
        x, y, c, chips = _place()
        me, sibling = (x, y, c), (x, y, 1 - c)

        def block(a, p):
            return outs[a].at[4 * p[0] + 2 * p[1] + p[2]]

        def copy(a, k, blk, to, src=None):
            return pltpu.make_async_remote_copy(src_ref=block(a, blk) if src is None else src, dst_ref=block(a, blk),
                                                send_sem=send_sems.at[a, k], recv_sem=recv_sems.at[a, k],
                                                device_id=to, device_id_type=MESH)

        first, passed = [], []
        for a in range(n):
            first.append(copy(a, 0, me, sibling, src=ins[a]))
            first += [copy(a, 1 + j, me, (*chip, c), src=ins[a]) for j, chip in enumerate(chips)]
        for cp in first:
            cp.start()
        for j, chip in enumerate(chips):
            for a in range(n):
                copy(a, 1 + j, (*chip, c), me).wait_recv()
                fwd = copy(a, 4 + j, (*chip, c), sibling)
                fwd.start()
                passed.append(fwd)
        for a in range(n):
            copy(a, 0, sibling, me).wait_recv()
            for j, chip in enumerate(chips):
                copy(a, 4 + j, (*chip, 1 - c), me).wait_recv()
        for cp in first + passed:
            cp.wait_send()

    gathered = pl.pallas_call(
        body, in_specs=[HBM_SPEC] * n, out_specs=[HBM_SPEC] * n,
        out_shape=[jax.ShapeDtypeStruct((N_DEV,) + a.shape, a.dtype) for a in arrs],
        scratch_shapes=[pltpu.SemaphoreType.DMA((n, 7)), pltpu.SemaphoreType.DMA((n, 7))], name=name,
    )(*arrs)
    me = 4 * lax.axis_index("x") + 2 * lax.axis_index("y") + lax.axis_index("c")
    return [lax.dynamic_update_index_in_dim(g, a, me, 0) for g, a in zip(gathered, arrs)]


SEM_SPEC = pl.BlockSpec(memory_space=pltpu.SEMAPHORE)
DATAFLOW = pltpu.SideEffectType.DATAFLOW_SIDE_EFFECTING
ALL_PEERS = (1, 2, 3, 4, 5, 6, 7)
NEAR_PEERS = (1, 4, 2, 6)


def _direct_copies(src_refs, zone_refs, send_sems, recv_sems, per_dest, peers):
    x, y, c, _ = _place()
    me = 4 * x + 2 * y + c
    cps = []
    for a, (src, zone) in enumerate(zip(src_refs, zone_refs)):
        for r, bits in enumerate(peers):
            px = 1 - x if bits & 4 else x
            py = 1 - y if bits & 2 else y
            pc = 1 - c if bits & 1 else c
            cps.append(pltpu.make_async_remote_copy(
                src_ref=src.at[4 * px + 2 * py + pc] if per_dest else src, dst_ref=zone.at[me],
                send_sem=send_sems.at[a * len(peers) + r], recv_sem=recv_sems.at[a * len(peers) + r],
                device_id=(px, py, pc), device_id_type=MESH))
    return cps


def _direct_start(srcs, per_dest, after, name, peers=ALL_PEERS):
    n = len(srcs)
    zones = [lax.empty((N_DEV,) + (s.shape[1:] if per_dest else s.shape), s.dtype) for s in srcs]

    def body(*refs):
        src_refs, zone_refs = refs[:n], refs[n:2 * n]
        send_sems, recv_sems = refs[2 * n + 1:2 * n + 3]
        for cp in _direct_copies(src_refs, zone_refs, send_sems, recv_sems, per_dest, peers):
            cp.start()
        refs[-1][...] = jnp.zeros_like(refs[-1])

    sems = pltpu.SemaphoreType.DMA((n * len(peers),))
    res = pl.pallas_call(
        body, name=name, in_specs=[HBM_SPEC] * (2 * n + 1),
        out_shape=[sems, sems] + [pltpu.HBM(s.shape, s.dtype) for s in srcs] + [pltpu.HBM(z.shape, z.dtype) for z in zones]
        + [jax.ShapeDtypeStruct((8, 128), F32)],
        out_specs=[SEM_SPEC, SEM_SPEC] + [HBM_SPEC] * (2 * n) + [pl.BlockSpec(memory_space=pltpu.VMEM)],
        input_output_aliases={i: 2 + i for i in range(2 * n)},
        compiler_params=pltpu.CompilerParams(has_side_effects=DATAFLOW),
    )(*[pltpu.with_memory_space_constraint(s, pltpu.HBM) for s in srcs],
      *[pltpu.with_memory_space_constraint(z, pltpu.HBM) for z in zones], after)
    return (res[0], res[1], list(res[2:2 + n]), list(res[2 + n:2 + 2 * n]), per_dest, peers), res[-1][0:1, 0:1]


def _direct_wait(started, after, name):
    send_sems, recv_sems, srcs, zones, per_dest, peers = started
    n = len(srcs)

    def body(*refs):
        src_refs, zone_refs = refs[:n], refs[n:2 * n]
        for cp in _direct_copies(src_refs, zone_refs, refs[2 * n], refs[2 * n + 1], per_dest, peers):
            cp.wait_send()
            cp.wait_recv()

    res = pl.pallas_call(
        body, name=name, in_specs=[HBM_SPEC] * (2 * n) + [SEM_SPEC, SEM_SPEC, HBM_SPEC],
        out_shape=[pltpu.HBM(s.shape, s.dtype) for s in srcs] + [pltpu.HBM(z.shape, z.dtype) for z in zones],
        out_specs=[HBM_SPEC] * (2 * n), input_output_aliases={i: i for i in range(2 * n)},
        compiler_params=pltpu.CompilerParams(has_side_effects=DATAFLOW),
    )(*srcs, *zones, send_sems, recv_sems, after)
    return list(res[n:])


def _gather_finish(zones, name):
    n = len(zones)

    def body(*refs):
        zone_in, zone_out = refs[:n], refs[n:2 * n]
        send_sems, recv_sems = refs[2 * n:]
        x, y, c, chips = _place()
        cps = []
        for a in range(n):
            for j, (cx, cy) in enumerate(chips):
                blk = 4 * cx + 2 * cy + c
                cp = pltpu.make_async_remote_copy(src_ref=zone_in[a].at[blk], dst_ref=zone_out[a].at[blk],
                                                  send_sem=send_sems.at[a, j], recv_sem=recv_sems.at[a, j],
                                                  device_id=(x, y, 1 - c), device_id_type=MESH)
                cp.start()
                cps.append(cp)
        for cp in cps:
            cp.wait()

    return pl.pallas_call(
        body, in_specs=[HBM_SPEC] * n, out_specs=[HBM_SPEC] * n,
        out_shape=[jax.ShapeDtypeStruct(z.shape, z.dtype) for z in zones], input_output_aliases={a: a for a in range(n)},
        scratch_shapes=[pltpu.SemaphoreType.DMA((n, 3)), pltpu.SemaphoreType.DMA((n, 3))], name=name,
    )(*zones)


def _adamw(w, g, m, v):
    m = ADAM_B1 * m + (1.0 - ADAM_B1) * g
    v = ADAM_B2 * v + (1.0 - ADAM_B2) * jnp.square(g)
    m_hat = m / (1.0 - ADAM_B1 ** ADAM_STEP)
    v_hat = v / (1.0 - ADAM_B2 ** ADAM_STEP)
    delta = -ADAM_LR * (m_hat / (jnp.sqrt(v_hat) + ADAM_EPS) + ADAM_WD * w)
    return delta, m, v


def _adamw_layer(w, m, v, own, parts, layer, prev, name):
    _, rows, c = w.shape
    tm = _pick_tm(rows, c)

    def body(w_ref, m_ref, v_ref, own_ref, parts_ref, *rest):
        g = own_ref[...]
        for j in range(parts_ref.shape[0]):
            g = g + parts_ref[j].astype(F32)
        delta, m2, v2 = _adamw(w_ref[...], g, m_ref[...], v_ref[...])
        g_ref, d_ref, m2_ref, v2_ref = rest[-4:]
        g_ref[...], d_ref[...], m2_ref[...], v2_ref[...] = g, delta, m2, v2

    wspec = pl.BlockSpec((None, tm, c), lambda i: (layer, i, 0))
    prev = list(prev) if prev is not None else []
    return pl.pallas_call(
        body, grid=(rows // tm,),
        in_specs=[wspec] * 3 + [pl.BlockSpec((tm, c), lambda i: (i, 0)), pl.BlockSpec((parts.shape[0], tm, c), lambda i: (0, i, 0))]
        + [HBM_SPEC] * len(prev),
        out_specs=[wspec] * 4, out_shape=[jax.ShapeDtypeStruct(w.shape, F32)] * 4,
        input_output_aliases={5 + k: k for k in range(len(prev))}, name=name,
    )(w, m, v, own, parts, *prev)


def _adamw_call(w, m, v, gparts, name):
    def fn(wv, mv, vv, *gs):
        g = gs[0].astype(F32)
        for gp in gs[1:]:
            g = g + gp.astype(F32)
        delta, m2, v2 = _adamw(wv, g, mv, vv)
        return g, delta, m2, v2
    rows, c = w.shape
    return _ew(fn, [w, m, v] + list(gparts), [], [(c, F32)] * 4, tm=_pick_tm(rows, c), name=name)


def kernel(x, p, g_mix, w_in, b_fox_f, fox_q_gain, fox_k_gain, sc_conv_w, dn_conv_w, dn_a_log, dn_dt_bias,
           dn_norm_gain, w_branch, w_o, g_ffn, w_up, ffn_conv_w, w_down, g_ple, w_ple_gate, w_ple, loss_target,
           m_g_mix, m_w_in, m_b_fox_f, m_fox_q_gain, m_fox_k_gain, m_sc_conv_w, m_dn_conv_w, m_dn_a_log,
           m_dn_dt_bias, m_dn_norm_gain, m_w_branch, m_w_o, m_g_ffn, m_w_up, m_ffn_conv_w, m_w_down, m_g_ple,
           m_w_ple_gate, m_w_ple, v_g_mix, v_w_in, v_b_fox_f, v_fox_q_gain, v_fox_k_gain, v_sc_conv_w, v_dn_conv_w,
           v_dn_a_log, v_dn_dt_bias, v_dn_norm_gain, v_w_branch, v_w_o, v_g_ffn, v_w_up, v_ffn_conv_w, v_w_down,
           v_g_ple, v_w_ple_gate, v_w_ple):
    return _step(x, p, g_mix, w_in, b_fox_f, fox_q_gain, fox_k_gain, sc_conv_w, dn_conv_w, dn_a_log, dn_dt_bias,
                 dn_norm_gain, w_branch, w_o, g_ffn, w_up, ffn_conv_w, w_down, g_ple, w_ple_gate, w_ple, loss_target,
                 m_g_mix, m_w_in, m_b_fox_f, m_fox_q_gain, m_fox_k_gain, m_sc_conv_w, m_dn_conv_w, m_dn_a_log,
                 m_dn_dt_bias, m_dn_norm_gain, m_w_branch, m_w_o, m_g_ffn, m_w_up, m_ffn_conv_w, m_w_down, m_g_ple,
                 m_w_ple_gate, m_w_ple, v_g_mix, v_w_in, v_b_fox_f, v_fox_q_gain, v_fox_k_gain, v_sc_conv_w,
                 v_dn_conv_w, v_dn_a_log, v_dn_dt_bias, v_dn_norm_gain, v_w_branch, v_w_o, v_g_ffn, v_w_up,
                 v_ffn_conv_w, v_w_down, v_g_ple, v_w_ple_gate, v_w_ple)


def _step(*args):
    names = ['x', 'p'] + WEIGHTS + ['loss_target'] + ['m_' + n for n in WEIGHTS] + ['v_' + n for n in WEIGHTS]
    assert len(args) == len(names)
    a = dict(zip(names, args))
    for n, perm in HELD_TRANSPOSED.items():
        for k in (n, 'm_' + n, 'v_' + n):
            a[k] = jnp.transpose(a[k], perm)
    x, target = a['x'][0], a['loss_target'][0]
    p = a['p'][:, 0]
    dev = 4 * lax.axis_index("x") + 2 * lax.axis_index("y") + lax.axis_index("c")

    LATE = [n for n in BIG if n != 'w_in']
    sps = [{n: a[n][layer][None, :] for n in SMALL} for layer in range(DEPTH)]
    shards = [{n: a[n][layer].astype(BF16) for n in BIG} for layer in range(DEPTH)]

    def by_device(full, n):
        by_dest = _by_dest(full, n)
        return lax.dynamic_index_in_dim(by_dest, dev, axis=0, keepdims=False), by_dest.astype(BF16)

    def blank(zone):
        return lax.dynamic_update_index_in_dim(zone, jnp.zeros(zone.shape[1:], zone.dtype), dev, 0)

    w_in_all, conv_all = _all_gather([shards[0]['w_in'], _flat_pack([a[n] for n in CONVW])], 'gather_w_in_l0')
    conv_by_dev = [_flat_unpack(conv_all[d], CONV_SHARD_SHAPES) for d in range(N_DEV)]
    conv_full = {n: jnp.concatenate([conv_by_dev[d][i] for d in range(N_DEV)], axis=2) for i, n in enumerate(CONVW)}
    late0_started, zero = _direct_start([shards[0][n] for n in LATE], False, w_in_all, 'weights_rest_l0_start', NEAR_PEERS)
    w1_started, zero = _direct_start([shards[1][n] for n in BIG], False, late0_started[2][0], 'weights_l1_start', NEAR_PEERS)
    sps[0]['g_mix'] = sps[0]['g_mix'] + zero

    def late_weights0(after):
        zones = _gather_finish(_direct_wait(late0_started, after, 'weights_rest_l0_wait'), 'weights_rest_l0_finish')
        return {n: _full_from_gathered(lax.dynamic_update_index_in_dim(z, shards[0][n], dev, 0), n) for n, z in zip(LATE, zones)}

    w0 = {n: conv_full[n][0] for n in CONVW}
    w0['w_in'] = _w_in_assemble(w_in_all, 'w_in_assemble_l0')
    h, sv0 = _layer_fwd(x, p[0], w0, sps[0], 'l0', late_weights0)
    zones = [lax.dynamic_update_index_in_dim(z, shards[1][n], dev, 0)
             for n, z in zip(BIG, _gather_finish(_direct_wait(w1_started, h, 'weights_l1_wait'), 'weights_l1_finish'))]
    w1 = {n: _full_from_gathered(z, n) for n, z in zip(BIG, zones) if n != 'w_in'}
    w1.update({n: conv_full[n][1] for n in CONVW})
    w1['w_in'] = _w_in_assemble(zones[list(BIG).index('w_in')], 'w_in_assemble_l1')
    h, sv1 = _layer_fwd(h, p[1], w1, sps[1], 'l1')
    dh, loss_part = _loss_bwd(h, target, 'loss')
    grads = [None] * DEPTH
    dh, grads[1] = _layer_bwd(dh, sv1, sps[1], 'l1')
    owns1, sends1 = {}, {}
    for n in BIG:
        owns1[n], sends1[n] = _w_in_split_by_device(grads[1][n], 'w_in_grad_split_l1') if n == 'w_in' else by_device(grads[1][n], n)
    g1_started, zero = _direct_start([sends1[n] for n in BIG], True, dh, 'grads_l1_start')
    early = {}

    def early_grads0(g, sp):
        owns, sends = zip(*[by_device(g[n], n) for n in LATE])
        early['started'], zero = _direct_start(list(sends), True, g['w_branch'], 'grads_rest_l0_start')
        early['owns'] = dict(zip(LATE, owns))
        return {**sp, 'dn_norm_gain': sp['dn_norm_gain'] + zero}

    def last_grad0(g, sp):
        early['own_in'], send = _w_in_split_by_device(g['w_in'], 'w_in_grad_split_l0')
        early['in_started'], zero = _direct_start([send], True, g['w_in'], 'grads_w_in_l0_start')
        return {**sp, 'g_mix': sp['g_mix'] + zero}

    dh, grads[0] = _layer_bwd(dh, sv0, {**sps[0], 'g_ple': sps[0]['g_ple'] + zero}, 'l0', early_grads0, last_grad0)
    grad_x = dh[None]
    own_in0, in0_started = early['own_in'], early['in_started']
    sent = in0_started[2][0]
    zones1 = dict(zip(BIG, _direct_wait(g1_started, sent, 'grads_l1_wait')))
    zones0 = dict(zip(LATE, _direct_wait(early['started'], sent, 'grads_rest_l0_wait')))

    def adamw(n, layer, own, zone, prev):
        rows = (-1, a[n].shape[-1])
        own = own.reshape(rows)
        view = lambda t: t.reshape((DEPTH,) + own.shape)
        return _adamw_layer(view(a[n]), view(a['m_' + n]), view(a['v_' + n]), own, blank(zone).reshape((N_DEV,) + own.shape),
                            layer, prev, f'adamw_{n}_l{layer}')

    out, done = {}, []
    for n in LATE:
        res = adamw(n, 0, early['owns'][n], zones0[n], adamw(n, 1, owns1[n], zones1[n], None))
        out[n] = [r.reshape(a[n].shape) for r in res]
        done.append(res[0][0, 0:1, 0:1])
    res = adamw('w_in', 1, owns1['w_in'], zones1['w_in'], None)
    done.append(res[0][1, 0:1, 0:1])
    (zone_in0,) = _direct_wait(in0_started, jnp.concatenate(done, axis=1), 'grads_w_in_l0_wait')
    out['w_in'] = [r.reshape(a['w_in'].shape) for r in adamw('w_in', 0, own_in0, zone_in0, res)]

    small_pack = _flat_pack([jnp.stack([grads[layer][n].reshape(-1) for layer in range(DEPTH)]) for n in SMALL] + [loss_part[0, 0]])
    conv_pack = _flat_pack([jnp.stack([grads[layer][n] for layer in range(DEPTH)]) for n in CONVW])
    small_started, _ = _direct_start([small_pack, conv_pack], False, small_pack, 'small_grads_start')
    small_all, conv_all = [lax.dynamic_update_index_in_dim(z, s, dev, 0) for z, s in
                           zip(_direct_wait(small_started, out['w_in'][0], 'small_grads_wait'), (small_pack, conv_pack))]
    res = _adamw_call(_flat_pack([a[n] for n in SMALL]), _flat_pack([a['m_' + n] for n in SMALL]),
                      _flat_pack([a['v_' + n] for n in SMALL]), [small_all[d] for d in range(N_DEV)], 'adamw_replicated')
    loss = res[0].reshape(-1)[SMALL_LOSS_AT]
    for k, r in enumerate(res):
        for n, val in zip(SMALL, _flat_unpack(r, SMALL_SHAPES)):
            out.setdefault(n, [None] * 4)[k] = val
    (conv_sum,) = _ew(lambda *gs: (functools.reduce(lambda s, t: s + t, gs),), [conv_all[d] for d in range(N_DEV)], [],
                      [(PACK_W, F32)], tm=conv_pack.shape[0], name='conv_grad_sum')
    conv_own = [lax.dynamic_slice_in_dim(g, dev * (g.shape[2] // N_DEV), g.shape[2] // N_DEV, axis=2)
                for g in _flat_unpack(conv_sum, CONV_FULL_SHAPES)]
    res = _adamw_call(_flat_pack([a[n] for n in CONVW]), _flat_pack([a['m_' + n] for n in CONVW]),
                      _flat_pack([a['v_' + n] for n in CONVW]), [_flat_pack(conv_own)], 'adamw_conv')
    for k, r in enumerate(res):
        for n, val in zip(CONVW, _flat_unpack(r, CONV_SHARD_SHAPES)):
            out.setdefault(n, [None] * 4)[k] = val

    for n, perm in HELD_TRANSPOSED.items():
        out[n] = [jnp.transpose(r, perm) for r in out[n]]
    outs = [loss, grad_x]
    for k in range(4):
        outs += [out[n][k] for n in WEIGHTS]
    return tuple(outs)
```

```python
import functools

import jax
import jax.numpy as jnp
from jax import lax
from jax.experimental import pallas as pl
from jax.experimental.pallas import tpu as pltpu

F32 = jnp.float32
BF16 = jnp.bfloat16
HI = lax.Precision.HIGHEST

D_MODEL = 1024
DEPTH = 2
N_DEV = 8
PLE_DIM = 256
BW = 512
FOX_HEADS, FOX_DH = 8, 64
DN_HEADS, DN_DH = 4, 128
DN_CHUNK = 64
D_FF = 2816
EPS = 1e-6
NEG = -1e30

ADAM_LR, ADAM_B1, ADAM_B2, ADAM_EPS, ADAM_WD, ADAM_STEP = 0.001, 0.9, 0.999, 1e-08, 0.01, 10

C_FQ, C_FK, C_FV = 0, 512, 1024
C_SB, C_SC, C_SV = 1536, 2048, 2560
C_DQ, C_DK, C_DV, C_DZ = 3072, 3584, 4096, 4608
C_GATE = 5120
C_SMALL = 8192
IN_P = 8320
IN_ORIG = 8208

WEIGHTS = ['g_mix', 'w_in', 'b_fox_f', 'fox_q_gain', 'fox_k_gain', 'sc_conv_w', 'dn_conv_w', 'dn_a_log',
           'dn_dt_bias', 'dn_norm_gain', 'w_branch', 'w_o', 'g_ffn', 'w_up', 'ffn_conv_w', 'w_down', 'g_ple',
           'w_ple_gate', 'w_ple']
BIG = {'w_in': 1, 'w_branch': 2, 'w_o': 0, 'w_up': 0, 'w_down': 0, 'w_ple_gate': 0, 'w_ple': 1}
HELD_TRANSPOSED = {'w_up': (0, 2, 1)}
CONVW = {'sc_conv_w': 1, 'dn_conv_w': 1, 'ffn_conv_w': 1}
SHARDED = {**BIG, **CONVW}
SMALL = [n for n in WEIGHTS if n not in SHARDED]
MESH = pl.DeviceIdType.MESH


def _sigmoid(x):
    return 0.5 * (jnp.tanh(0.5 * x) + 1.0)


def _silu(x):
    return x * _sigmoid(x)


def _log1pexp_negabs(z):
    return jnp.log(1.0 + jnp.exp(-jnp.abs(z)))


def _log_sigmoid(z):
    return jnp.minimum(z, 0.0) - _log1pexp_negabs(z)


def _softplus(z):
    return jnp.maximum(z, 0.0) + _log1pexp_negabs(z)


def _rms(x, g):
    return x * lax.rsqrt(jnp.mean(x * x, axis=-1, keepdims=True) + EPS) * g


def _l2(x):
    return x * lax.rsqrt(jnp.sum(x * x, axis=-1, keepdims=True) + EPS)


def _dot(a, b, dims, precision=None):
    return lax.dot_general(a, b, (dims, ((), ())), preferred_element_type=F32, precision=precision)


NN = ((1,), (0,))
NT = ((1,), (1,))
TN = ((0,), (0,))


def _shift_down(x, s):
    if s == 0:
        return x
    t = lax.broadcasted_iota(jnp.int32, x.shape, 0)
    return jnp.where(t >= s, pltpu.roll(x, s, 0), 0.0)


def _shift_up(x, s):
    if s == 0:
        return x
    n = x.shape[0]
    t = lax.broadcasted_iota(jnp.int32, x.shape, 0)
    return jnp.where(t < n - s, pltpu.roll(x, n - s, 0), 0.0)


def _conv(x, w):
    k = w.shape[0]
    y = w[k - 1:k] * x
    for j in range(k - 1):
        y = y + w[j:j + 1] * _shift_down(x, k - 1 - j)
    return y


def _conv_bwd(x, w, dy):
    k = w.shape[0]
    dx = w[k - 1:k] * dy
    dws = []
    for j in range(k - 1):
        dx = dx + w[j:j + 1] * _shift_up(dy, k - 1 - j)
        dws.append(jnp.sum(dy * _shift_down(x, k - 1 - j), axis=0, keepdims=True))
    dws.append(jnp.sum(dy * x, axis=0, keepdims=True))
    return dx, dws


MM_VMEM_BUDGET = 36 << 20
MM_STEP_BYTES = 1 << 20


def _mm_tiles(m, n, k, a_size, b_size, tile_size, cast_a):
    best = None
    for tm in [d for d in (2048, 1024, 512, 256, 128) if d <= m and m % d == 0] or [m]:
        for tn in [d for d in range(128, min(n, 2048) + 1, 128) if n % d == 0] or [n]:
            vmem = 2 * tm * k * a_size + (2 * tm * k if cast_a else 0) + 2 * k * tn * b_size + 2 * tm * tn * tile_size
            if vmem > MM_VMEM_BUDGET:
                continue
            steps = (m // tm) * (n // tn)
            cost = m * k * a_size + (m // tm) * k * n * b_size + m * n * tile_size + steps * MM_STEP_BYTES
            if best is None or cost < best[0]:
                best = (cost, tm, tn)
    assert best is not None, (m, n, k)
    return best[1], best[2]


def _mm(a, b, mode, outs, *, epi=None, epi_args=(), name):
    if mode == 'nn':
        (m, k), (k2, n) = a.shape, b.shape
    elif mode == 'nt':
        (m, k), (n, k2) = a.shape, b.shape
    else:
        (k, m), (k2, n) = a.shape, b.shape
    assert k == k2, (a.shape, b.shape, mode)
    tile_size = sum(jnp.dtype(dt).itemsize for dt in outs) + sum(e.dtype.itemsize for e in epi_args if e.shape[0] != 1)
    tm, tn = _mm_tiles(m, n, k, a.dtype.itemsize, b.dtype.itemsize, tile_size, a.dtype != BF16)
    dims = {'nn': NN, 'nt': NT, 'tn': TN}[mode]
    a_spec = pl.BlockSpec((k, tm), lambda i, j: (0, i)) if mode == 'tn' else pl.BlockSpec((tm, k), lambda i, j: (i, 0))
    b_spec = pl.BlockSpec((tn, k), lambda i, j: (j, 0)) if mode == 'nt' else pl.BlockSpec((k, tn), lambda i, j: (0, j))
    e_specs = [pl.BlockSpec((1, tn), lambda i, j: (0, j)) if e.shape[0] == 1 else pl.BlockSpec((tm, tn), lambda i, j: (i, j))
               for e in epi_args]
    ne, no = len(epi_args), len(outs)
    cast_a = a.dtype != BF16

    def body(a_ref, b_ref, *rest):
        if cast_a:
            a_sc = rest[-1]

            @pl.when(pl.program_id(1) == 0)
            def _():
                a_sc[...] = a_ref[...].astype(BF16)
            av = a_sc[...]
        else:
            av = a_ref[...]
        acc = _dot(av, b_ref[...].astype(BF16), dims)
        vals = epi(acc, *[e[...] for e in rest[:ne]]) if epi is not None else (acc,)
        for o_ref, v in zip(rest[ne:ne + no], vals):
            o_ref[...] = v.astype(o_ref.dtype)

    res = pl.pallas_call(
        body, grid=(m // tm, n // tn),
        in_specs=[a_spec, b_spec] + e_specs,
        out_specs=[pl.BlockSpec((tm, tn), lambda i, j: (i, j)) for _ in outs],
        out_shape=[jax.ShapeDtypeStruct((m, n), dt) for dt in outs],
        scratch_shapes=[pltpu.VMEM(a_spec.block_shape, BF16)] if cast_a else [],
        name=name,
    )(a, b, *epi_args)
    return res[0] if no == 1 else res


def _ew(fn, tiled, bcast, outs, reds=(), *, tm=256, name):
    secs = [(t, 0, t.shape[1]) if not isinstance(t, tuple) else t for t in tiled]
    m = secs[0][0].shape[0]
    tm = min(tm, m)
    assert m % tm == 0
    in_specs = []
    for arr, off, w in secs:
        assert off % w == 0
        in_specs.append(pl.BlockSpec((tm, w), functools.partial(lambda i, c: (i, c), c=off // w)))
    in_specs += [pl.BlockSpec(b.shape, lambda i: (0, 0)) for b in bcast]
    nin, no = len(in_specs), len(outs)

    def body(*refs):
        vals = fn(*[r[...] for r in refs[:nin]])
        for r, v in zip(refs[nin:nin + no], vals[:no]):
            r[...] = v.astype(r.dtype)
        i = pl.program_id(0)
        for r, v in zip(refs[nin + no:], vals[no:]):
            @pl.when(i == 0)
            def _():
                r[...] = v

            @pl.when(i > 0)
            def _():
                r[...] += v

    res = pl.pallas_call(
        body, grid=(m // tm,), in_specs=in_specs,
        out_specs=[pl.BlockSpec((tm, c), lambda i: (i, 0)) for c, _ in outs] + [pl.BlockSpec(s, lambda i: (0, 0)) for s in reds],
        out_shape=[jax.ShapeDtypeStruct((m, c), dt) for c, dt in outs] + [jax.ShapeDtypeStruct(s, F32) for s in reds],
        name=name,
    )(*[s[0] for s in secs], *bcast)
    return res


def _cb(fn, cols, params, outs, pouts, *, tc, nblk, name):
    t = cols[0][0].shape[0]
    in_specs = []
    for arr, off in cols:
        assert off % tc == 0
        in_specs.append(pl.BlockSpec((t, tc), functools.partial(lambda c, o: (0, o + c), o=off // tc)))
    for arr, off in params:
        in_specs.append(pl.BlockSpec((arr.shape[0], tc), functools.partial(lambda c, o: (0, o + c), o=off // tc)))
    nin, no = len(in_specs), len(outs)

    def body(*refs):
        vals = fn(*[r[...] for r in refs[:nin]])
        for r, v in zip(refs[nin:], vals):
            r[...] = v.astype(r.dtype)

    return pl.pallas_call(
        body, grid=(nblk,), in_specs=in_specs,
        out_specs=[pl.BlockSpec((t, tc), lambda c: (0, c)) for _ in outs] + [pl.BlockSpec((k, tc), lambda c: (0, c)) for k in pouts],
        out_shape=[jax.ShapeDtypeStruct((t, nblk * tc), dt) for dt in outs] + [jax.ShapeDtypeStruct((k, nblk * tc), F32) for k in pouts],
        name=name,
    )(*[c[0] for c in cols], *[p[0] for p in params])


CUM_BLK = 256


def _prep_point(s, bias, alog):
    lane = lax.broadcasted_iota(jnp.int32, s.shape, 1)
    z = s + bias
    return jnp.where(lane < 8, _log_sigmoid(z),
                     jnp.where(lane < 12, _sigmoid(s),
                               jnp.where(lane < 16, -jnp.exp(alog) * _softplus(z), 0.0)))


def _tri(n, upper):
    r = lax.broadcasted_iota(jnp.int32, (n, n), 0)
    c = lax.broadcasted_iota(jnp.int32, (n, n), 1)
    return (r <= c if upper else r >= c).astype(F32)


def _prep_fwd(proj, bias_row, alog_row, name):
    t = proj.shape[0]
    nb = t // CUM_BLK

    def body(s_ref, b_ref, a_ref, o_ref):
        pre = _prep_point(s_ref[...], b_ref[...], a_ref[...])
        lane = lax.broadcasted_iota(jnp.int32, (CUM_BLK, 128), 1)
        tri = _tri(CUM_BLK, False)
        carry = jnp.zeros((1, 128), F32)
        for blk in range(nb):
            xb = pre[blk * CUM_BLK:(blk + 1) * CUM_BLK]
            cb = _dot(tri, xb, NN, HI) + carry
            carry = cb[CUM_BLK - 1:CUM_BLK]
            o_ref[blk * CUM_BLK:(blk + 1) * CUM_BLK, :] = jnp.where(lane < 8, cb, xb)

    return pl.pallas_call(
        body, grid=(1,),
        in_specs=[pl.BlockSpec((t, 128), lambda i: (0, C_SMALL // 128)), pl.BlockSpec((1, 128), lambda i: (0, 0)),
                  pl.BlockSpec((1, 128), lambda i: (0, 0))],
        out_specs=pl.BlockSpec((t, 128), lambda i: (0, 0)),
        out_shape=jax.ShapeDtypeStruct((t, 128), F32), name=name,
    )(proj, bias_row, alog_row)


def _prep_bwd(proj, bias_row, alog_row, daux, name):
    t = proj.shape[0]
    nb = t // CUM_BLK

    def body(s_ref, b_ref, a_ref, d_ref, ds_ref, db_ref, da_ref, dpre_sc):
        lane = lax.broadcasted_iota(jnp.int32, (CUM_BLK, 128), 1)
        tri = _tri(CUM_BLK, True)
        carry = jnp.zeros((1, 128), F32)
        for blk in reversed(range(nb)):
            db = d_ref[blk * CUM_BLK:(blk + 1) * CUM_BLK, :]
            cb = _dot(tri, db, NN, HI) + carry
            carry = cb[0:1]
            dpre_sc[blk * CUM_BLK:(blk + 1) * CUM_BLK, :] = jnp.where(lane < 8, cb, db)
        _, vjp = jax.vjp(_prep_point, s_ref[...], b_ref[...], a_ref[...])
        ds, dbias, dalog = vjp(dpre_sc[...])
        ds_ref[...] = ds.astype(ds_ref.dtype)
        db_ref[...] = dbias
        da_ref[...] = dalog

    return pl.pallas_call(
        body, grid=(1,),
        in_specs=[pl.BlockSpec((t, 128), lambda i: (0, C_SMALL // 128)), pl.BlockSpec((1, 128), lambda i: (0, 0)),
                  pl.BlockSpec((1, 128), lambda i: (0, 0)), pl.BlockSpec((t, 128), lambda i: (0, 0))],
        out_specs=[pl.BlockSpec((t, 128), lambda i: (0, 0)), pl.BlockSpec((1, 128), lambda i: (0, 0)),
                   pl.BlockSpec((1, 128), lambda i: (0, 0))],
        out_shape=[jax.ShapeDtypeStruct((t, 128), BF16), jax.ShapeDtypeStruct((1, 128), F32), jax.ShapeDtypeStruct((1, 128), F32)],
        scratch_shapes=[pltpu.VMEM((t, 128), F32)], name=name,
    )(proj, bias_row, alog_row, daux)


ATT_TQ = 256
FOX_SCALE = FOX_DH ** -0.5


def _qnorm(q, g):
    return _rms(q, g) * FOX_SCALE


def _att_scores(qn_blk, kn, cfc_blk, cfr, qi, tq, kend):
    s = _dot(qn_blk.astype(BF16), kn[:kend].astype(BF16), NT) + cfc_blk - cfr[:, :kend]
    row = lax.broadcasted_iota(jnp.int32, (tq, kend), 0) + qi * tq
    col = lax.broadcasted_iota(jnp.int32, (tq, kend), 1)
    return s, row >= col


ATT_PAIR = 128 // FOX_DH


def _att_specs(t):
    pair = lambda off: pl.BlockSpec((t, 128), functools.partial(lambda i, o: (0, o + i), o=off // 128))
    gain = pl.BlockSpec((1, FOX_DH), lambda i: (0, 0))
    hd = lambda i: (i, 0, 0)
    col, row = pl.BlockSpec((ATT_PAIR, t, 1), hd), pl.BlockSpec((ATT_PAIR, 1, t), hd)
    return [pair(C_FQ), pair(C_FK), pair(C_FV), gain, gain, col, row], pl.BlockSpec((t, 128), lambda i: (0, i)), col, row


def _att_fwd(proj, qg, kg, cfc, cfr, name):
    t = proj.shape[0]
    tq = min(ATT_TQ, t)
    in_specs, pair_out, col, _ = _att_specs(t)

    def body(q_ref, k_ref, v_ref, qg_ref, kg_ref, cfc_ref, cfr_ref, o_ref, lse_ref):
        for e in range(ATT_PAIR):
            lanes = slice(e * FOX_DH, (e + 1) * FOX_DH)
            qn = _qnorm(q_ref[:, lanes], qg_ref[...])
            kn = _rms(k_ref[:, lanes], kg_ref[...])
            v = v_ref[:, lanes].astype(BF16)
            cfr = cfr_ref[e]
            for qi in range(t // tq):
                kend = (qi + 1) * tq
                rows = slice(qi * tq, kend)
                s, mask = _att_scores(qn[rows], kn, cfc_ref[e, rows, :], cfr, qi, tq, kend)
                s = jnp.where(mask, s, NEG)
                m = jnp.max(s, axis=1, keepdims=True)
                p = jnp.exp(s - m)
                l = jnp.sum(p, axis=1, keepdims=True)
                o_ref[rows, lanes] = _dot(p.astype(BF16), v[:kend], NN) / l
                lse_ref[e, rows, :] = m + jnp.log(l)

    return pl.pallas_call(
        body, grid=(FOX_HEADS // ATT_PAIR,), in_specs=in_specs, out_specs=[pair_out, col],
        out_shape=[jax.ShapeDtypeStruct((t, BW), F32), jax.ShapeDtypeStruct((FOX_HEADS, t, 1), F32)], name=name,
    )(proj, proj, proj, qg, kg, cfc, cfr)


def _att_bwd(proj, qg, kg, cfc, cfr, lse, o, do, name):
    t = proj.shape[0]
    tq = min(ATT_TQ, t)
    in_specs, pair_out, col, row = _att_specs(t)

    def body(q_ref, k_ref, v_ref, qg_ref, kg_ref, cfc_ref, cfr_ref, lse_ref, o_ref, do_ref,
             dq_ref, dk_ref, dv_ref, dcfc_ref, dcfr_ref, dqg_ref, dkg_ref, dqn_sc, dkn_sc, dv_sc, dcfr_sc):
        for e in range(ATT_PAIR):
            lanes = slice(e * FOX_DH, (e + 1) * FOX_DH)
            qn, vjp_q = jax.vjp(_qnorm, q_ref[:, lanes], qg_ref[...])
            kn, vjp_k = jax.vjp(_rms, k_ref[:, lanes], kg_ref[...])
            v = v_ref[:, lanes].astype(BF16)
            cfr = cfr_ref[e]
            do_e = do_ref[:, lanes]
            delta = jnp.sum(do_e * o_ref[:, lanes], axis=1, keepdims=True)
            dkn_sc[...] = jnp.zeros_like(dkn_sc)
            dv_sc[...] = jnp.zeros_like(dv_sc)
            dcfr_sc[...] = jnp.zeros_like(dcfr_sc)
            for qi in range(t // tq):
                kend = (qi + 1) * tq
                rows = slice(qi * tq, kend)
                s, mask = _att_scores(qn[rows], kn, cfc_ref[e, rows, :], cfr, qi, tq, kend)
                p = jnp.where(mask, jnp.exp(jnp.where(mask, s, NEG) - lse_ref[e, rows, :]), 0.0)
                do_b = do_e[rows].astype(BF16)
                dv_sc[0:kend, :] += _dot(p.astype(BF16), do_b, TN)
                dp = _dot(do_b, v[:kend], NT)
                ds = p * (dp - delta[rows])
                ds_b = ds.astype(BF16)
                dqn_sc[rows, :] = _dot(ds_b, kn[:kend].astype(BF16), NN)
                dkn_sc[0:kend, :] += _dot(ds_b, qn[rows].astype(BF16), TN)
                dcfc_ref[e, rows, :] = jnp.sum(ds, axis=1, keepdims=True)
                dcfr_sc[:, 0:kend] -= jnp.sum(ds, axis=0, keepdims=True)
            dq, dqg = vjp_q(dqn_sc[...])
            dk, dkg = vjp_k(dkn_sc[...])
            dq_ref[:, lanes] = dq.astype(dq_ref.dtype)
            dk_ref[:, lanes] = dk.astype(dk_ref.dtype)
            dv_ref[:, lanes] = dv_sc[...].astype(dv_ref.dtype)
            dcfr_ref[e] = dcfr_sc[...]
            dqg_ref[e] = dqg
            dkg_ref[e] = dkg

    gsp = pl.BlockSpec((ATT_PAIR, 1, FOX_DH), lambda i: (i, 0, 0))
    return pl.pallas_call(
        body, grid=(FOX_HEADS // ATT_PAIR,),
        in_specs=in_specs + [col, pair_out, pair_out],
        out_specs=[pair_out] * 3 + [col, row, gsp, gsp],
        out_shape=[jax.ShapeDtypeStruct((t, BW), BF16)] * 3
        + [jax.ShapeDtypeStruct((FOX_HEADS, t, 1), F32), jax.ShapeDtypeStruct((FOX_HEADS, 1, t), F32)]
        + [jax.ShapeDtypeStruct((FOX_HEADS, 1, FOX_DH), F32)] * 2,
        scratch_shapes=[pltpu.VMEM((t, FOX_DH), F32)] * 3 + [pltpu.VMEM((1, t), F32)], name=name,
    )(proj, proj, proj, qg, kg, cfc, cfr, lse, o, do)


DN_SCALE = DN_DH ** -0.5


DN_BATCH = 8


@functools.partial(jax.custom_vjp, nondiff_argnums=(2, 3))
def _mm3(a, b, dims, batch=False):
    return _mm3_passes(a, b, dims, batch)


def _mm3_fwd(a, b, dims, batch):
    return _mm3_passes(a, b, dims, batch), (a, b)


def _mm3_bwd(dims, batch, res, dc):
    a, b = res
    if dims == NN:
        return _mm3_passes(dc, b, NT, batch), _mm3_passes(a, dc, TN, batch)
    if dims == NT:
        return _mm3_passes(dc, b, NN, batch), _mm3_passes(dc, a, TN, batch)
    return _mm3_passes(b, dc, NT, batch), _mm3_passes(a, dc, NN, batch)


_mm3.defvjp(_mm3_fwd, _mm3_bwd)


def _mm3_passes(a, b, dims, batch):
    if batch:
        dn = (((dims[0][0] + 1,), (dims[1][0] + 1,)), ((0,), (0,)))
        dot = lambda p, q: lax.dot_general(p, q, dn, preferred_element_type=F32)
    else:
        dot = lambda p, q: _dot(p, q, dims)
    ah, bh = a.astype(BF16), b.astype(BF16)
    al, bl = (a - ah.astype(F32)).astype(BF16), (b - bh.astype(F32)).astype(BF16)
    return dot(ah, bh) + (dot(ah, bl) + dot(al, bh))


def _dn_local(qc, kc, vc, g, beta):
    nb, c, _ = qc.shape
    ii = lax.broadcasted_iota(jnp.int32, (c, c), 0)
    jj = lax.broadcasted_iota(jnp.int32, (c, c), 1)
    incl, strict = ii >= jj, ii > jj
    lower = jnp.broadcast_to(incl.astype(F32), (nb, c, c))
    eye = (ii == jj).astype(F32)
    mm = functools.partial(_mm3, batch=True)
    dm = mm(lower, jnp.where(strict, g, 0.0), NN)
    decay = jnp.where(incl, jnp.exp(jnp.where(incl, dm, 0.0)), 0.0)
    gcum = mm(lower, g * jnp.ones((1, 1, DN_DH), F32), NN)
    eg = jnp.exp(gcum)
    glast = gcum[:, c - 1:c]
    kb = kc * beta
    n1 = jnp.where(strict, mm(kb, kc, NT) * decay, 0.0)
    inv = eye - n1
    pw = n1
    for _ in range(5):
        pw = mm(pw, pw, NN)
        inv = inv + mm(pw, inv, NN)
    sol = mm(inv, jnp.concatenate([vc * beta, kb * eg], axis=2), NN)
    qk = jnp.where(incl, mm(qc, kc, NT) * decay, 0.0)
    return sol[:, :, :DN_DH], sol[:, :, DN_DH:], qk, qc * eg, kc * jnp.exp(glast - gcum), jnp.exp(glast)


@functools.partial(jax.custom_vjp, nondiff_argnums=(2,))
def _mm1(a, b, dims):
    return _dot(a.astype(BF16), b.astype(BF16), dims)


def _mm1_fwd(a, b, dims):
    return _mm1(a, b, dims), (a, b)


def _mm1_bwd(dims, res, dc):
    a, b = res
    if dims == NN:
        return _mm1(dc, b, NT), _mm1(a, dc, TN)
    if dims == NT:
        return _mm1(dc, b, NN), _mm1(dc, a, TN)
    return _mm1(b, dc, NT), _mm1(a, dc, NN)


_mm1.defvjp(_mm1_fwd, _mm1_bwd)


def _dn_state(u, kcum, qk, qdec, kdec, egl, state):
    v_new = u - _mm1(kcum, state, NN)
    out = _mm1(qdec, state, NN) + _mm1(qk, v_new, NN)
    return out, state * egl + _mm1(kdec, v_new, TN)


def _dn_pre_q(c):
    return _l2(_silu(c)) * DN_SCALE


def _dn_pre_k(c):
    return _l2(_silu(c))


def _dn_post(o, z, ng):
    return _rms(o, ng) * _silu(z)


def _dn_specs(t):
    cblk = lambda o: pl.BlockSpec((t, DN_DH), functools.partial(lambda h, o: (0, o + h), o=o // DN_DH))
    wblk = lambda o: pl.BlockSpec((4, DN_DH), functools.partial(lambda h, o: (0, o + h), o=o // DN_DH))
    proj_specs = [cblk(C_DQ), cblk(C_DK), cblk(C_DV), cblk(C_DZ)]
    w_specs = [wblk(0), wblk(BW), wblk(2 * BW)]
    gb_spec = pl.BlockSpec((1, t, 2), lambda h: (h, 0, 0))
    return proj_specs, w_specs, gb_spec


def _chunk_rows(n, count=1):
    return pl.ds(pl.multiple_of(n * DN_CHUNK, DN_CHUNK), count * DN_CHUNK)


def _egl_rows(n, count=1):
    return pl.ds(pl.multiple_of(n * 8, 8), count * 8)


def _dn_local_inputs(n, qn_sc, kn_sc, vv_sc, gb_ref):
    r = _chunk_rows(n, DN_BATCH)
    split = lambda v: v.reshape(DN_BATCH, DN_CHUNK, v.shape[-1])
    gbv = split(gb_ref[0, r, :])
    return split(qn_sc[r, :]), split(kn_sc[r, :]), split(vv_sc[r, :]), gbv[:, :, 0:1], gbv[:, :, 1:2]


def _dn_local_phase(nc, qn_sc, kn_sc, vv_sc, gb_ref, loc):
    def step(i, carry):
        n = i * DN_BATCH
        vals = _dn_local(*_dn_local_inputs(n, qn_sc, kn_sc, vv_sc, gb_ref))
        for sc, val in zip(loc[:5], vals[:5]):
            sc[_chunk_rows(n, DN_BATCH), :] = val.reshape(DN_BATCH * DN_CHUNK, val.shape[-1])
        loc[5][_egl_rows(n, DN_BATCH), :] = jnp.broadcast_to(vals[5], (DN_BATCH, 8, DN_DH)).reshape(DN_BATCH * 8, DN_DH)
        return carry

    lax.fori_loop(0, nc // DN_BATCH, step, 0)


def _dn_loc_scratch(t, nc):
    big = pltpu.VMEM((t, DN_DH), F32)
    return [big, big, pltpu.VMEM((t, DN_CHUNK), F32), big, big, pltpu.VMEM((nc * 8, DN_DH), F32)]


def _dn_loc_io(t, nc, **spec_args):
    head_cols = pl.BlockSpec((t, DN_DH), lambda h: (0, h), **spec_args)
    per_head = lambda rows, cols: pl.BlockSpec((1, rows, cols), lambda h: (h, 0, 0), **spec_args)
    wide = jax.ShapeDtypeStruct((t, BW), F32)
    shapes = [wide, wide, jax.ShapeDtypeStruct((DN_HEADS, t, DN_CHUNK), F32), wide, wide,
              jax.ShapeDtypeStruct((DN_HEADS, nc * 8, DN_DH), F32)]
    return shapes, [head_cols, head_cols, per_head(t, DN_CHUNK), head_cols, head_cols, per_head(nc * 8, DN_DH)]


def _dn_loc_refs(refs):
    return refs[0], refs[1], refs[2].at[0], refs[3], refs[4], refs[5].at[0]


def _dn_fwd(proj, conv_w, gb, ng, name):
    t = proj.shape[0]
    nc = t // DN_CHUNK
    assert nc % DN_BATCH == 0
    proj_specs, w_specs, gb_spec = _dn_specs(t)
    loc_shapes, loc_specs = _dn_loc_io(t, nc)

    def body(q_ref, k_ref, v_ref, z_ref, wq_ref, wk_ref, wv_ref, gb_ref, ng_ref, y_ref, o_ref, st_ref, *rest):
        loc = _dn_loc_refs(rest[:6])
        qn_sc, kn_sc, vv_sc = rest[6:]
        qn_sc[...] = _dn_pre_q(_conv(q_ref[...], wq_ref[...]))
        kn_sc[...] = _dn_pre_k(_conv(k_ref[...], wk_ref[...]))
        vv_sc[...] = _silu(_conv(v_ref[...], wv_ref[...]))
        _dn_local_phase(nc, qn_sc, kn_sc, vv_sc, gb_ref, loc)
        u_sc, kcum_sc, qk_sc, qdec_sc, kdec_sc, egl_sc = loc

        def chunk(n, state):
            r = _chunk_rows(n)
            egl = egl_sc[_egl_rows(n), :][0:1]
            out, new_state = _dn_state(u_sc[r, :], kcum_sc[r, :], qk_sc[r, :], qdec_sc[r, :], kdec_sc[r, :], egl, state)
            st_ref[0, n] = state
            o_ref[r, :] = out
            return new_state

        lax.fori_loop(0, nc, chunk, jnp.zeros((DN_DH, DN_DH), F32))
        y_ref[...] = _dn_post(o_ref[...], z_ref[...], ng_ref[...])

    hblk = pl.BlockSpec((t, DN_DH), lambda h: (0, h))
    return pl.pallas_call(
        body, grid=(DN_HEADS,),
        in_specs=proj_specs + w_specs + [gb_spec, pl.BlockSpec((1, DN_DH), lambda h: (0, 0))],
        out_specs=[hblk, hblk, pl.BlockSpec((1, nc, DN_DH, DN_DH), lambda h: (h, 0, 0, 0))] + loc_specs,
        out_shape=[jax.ShapeDtypeStruct((t, BW), F32), jax.ShapeDtypeStruct((t, BW), F32),
                   jax.ShapeDtypeStruct((DN_HEADS, nc, DN_DH, DN_DH), F32)] + loc_shapes,
        scratch_shapes=[pltpu.VMEM((t, DN_DH), F32)] * 3, name=name,
    )(proj, proj, proj, proj, conv_w, conv_w, conv_w, gb, ng)


def _dn_bwd(proj, conv_w, gb, ng, o, states, loc_saved, dy, name):
    t = proj.shape[0]
    nc = t // DN_CHUNK
    proj_specs, w_specs, gb_spec = _dn_specs(t)
    _, loc_specs = _dn_loc_io(t, nc, pipeline_mode=pl.Buffered(1))

    def body(q_ref, k_ref, v_ref, z_ref, wq_ref, wk_ref, wv_ref, gb_ref, ng_ref, o_ref, st_ref, dy_ref,
             u_ref, kcum_ref, qk_ref, qdec_ref, kdec_ref, egl_ref,
             dq_ref, dk_ref, dv_ref, dz_ref, dwq_ref, dwk_ref, dwv_ref, dgb_ref, dng_ref,
             qn_sc, kn_sc, vv_sc, do_sc, *dloc):
        loc = _dn_loc_refs((u_ref, kcum_ref, qk_ref, qdec_ref, kdec_ref, egl_ref))
        qn_sc[...] = _dn_pre_q(_conv(q_ref[...], wq_ref[...]))
        kn_sc[...] = _dn_pre_k(_conv(k_ref[...], wk_ref[...]))
        vv_sc[...] = _silu(_conv(v_ref[...], wv_ref[...]))
        _, vjp_y = jax.vjp(_dn_post, o_ref[...], z_ref[...], ng_ref[...])
        do, dz, dng = vjp_y(dy_ref[...])
        do_sc[...] = do
        dz_ref[...] = dz.astype(dz_ref.dtype)
        dng_ref[0] = dng
        u_sc, kcum_sc, qk_sc, qdec_sc, kdec_sc, egl_sc = loc

        def state_bwd(i, dstate):
            n = nc - 1 - i
            r = _chunk_rows(n)
            r8 = _egl_rows(n)
            _, vjp = jax.vjp(_dn_state, u_sc[r, :], kcum_sc[r, :], qk_sc[r, :], qdec_sc[r, :], kdec_sc[r, :],
                             egl_sc[r8, :][0:1], st_ref[0, n])
            du, dkcum, dqk, dqdec, dkdec, degl, dprev = vjp((do_sc[r, :], dstate))
            for d_sc, val in zip(dloc[:5], (du, dkcum, dqk, dqdec, dkdec)):
                d_sc[r, :] = val
            dloc[5][r8, :] = jnp.broadcast_to(degl, (8, DN_DH))
            return dprev

        lax.fori_loop(0, nc, state_bwd, jnp.zeros((DN_DH, DN_DH), F32))

        def local_bwd(i, carry):
            n = i * DN_BATCH
            r = _chunk_rows(n, DN_BATCH)
            _, vjp = jax.vjp(_dn_local, *_dn_local_inputs(n, qn_sc, kn_sc, vv_sc, gb_ref))
            cts = tuple(d_sc[r, :].reshape(DN_BATCH, DN_CHUNK, d_sc.shape[-1]) for d_sc in dloc[:5])
            cts += (dloc[5][_egl_rows(n, DN_BATCH), :].reshape(DN_BATCH, 8, DN_DH)[:, 0:1],)
            dqc, dkc, dvc, dg, dbeta = vjp(cts)
            for d_sc, val in zip((dloc[0], dloc[1], dloc[3]), (dqc, dkc, dvc)):
                d_sc[r, :] = val.reshape(DN_BATCH * DN_CHUNK, DN_DH)
            dgb_ref[0, r, :] = jnp.concatenate([dg, dbeta], axis=2).reshape(DN_BATCH * DN_CHUNK, 2)
            return carry

        lax.fori_loop(0, nc // DN_BATCH, local_bwd, 0)
        for x_ref, w_ref, pre, d_sc, dx_ref, dw_ref in ((q_ref, wq_ref, _dn_pre_q, dloc[0], dq_ref, dwq_ref),
                                                       (k_ref, wk_ref, _dn_pre_k, dloc[1], dk_ref, dwk_ref),
                                                       (v_ref, wv_ref, _silu, dloc[3], dv_ref, dwv_ref)):
            _, vjp = jax.vjp(pre, _conv(x_ref[...], w_ref[...]))
            (dc,) = vjp(d_sc[...])
            dx, dws = _conv_bwd(x_ref[...], w_ref[...], dc)
            dx_ref[...] = dx.astype(dx_ref.dtype)
            for j, dw in enumerate(dws):
                dw_ref[j:j + 1, :] = dw

    hblk = pl.BlockSpec((t, DN_DH), lambda h: (0, h))
    wout = pl.BlockSpec((4, DN_DH), lambda h: (0, h))
    return pl.pallas_call(
        body, grid=(DN_HEADS,),
        in_specs=proj_specs + w_specs + [gb_spec, pl.BlockSpec((1, DN_DH), lambda h: (0, 0)), hblk,
                                         pl.BlockSpec((1, nc, DN_DH, DN_DH), lambda h: (h, 0, 0, 0)), hblk] + loc_specs,
        out_specs=[hblk] * 4 + [wout] * 3 + [gb_spec, pl.BlockSpec((1, 1, DN_DH), lambda h: (h, 0, 0))],
        out_shape=[jax.ShapeDtypeStruct((t, BW), BF16)] * 4 + [jax.ShapeDtypeStruct((4, BW), F32)] * 3
        + [jax.ShapeDtypeStruct((DN_HEADS, t, 2), F32), jax.ShapeDtypeStruct((DN_HEADS, 1, DN_DH), F32)],
        scratch_shapes=[pltpu.VMEM((t, DN_DH), F32)] * 4 + _dn_loc_scratch(t, nc), name=name,
    )(proj, proj, proj, proj, conv_w, conv_w, conv_w, gb, ng, o, states, dy, *loc_saved)


MERGE_TM, MERGE_TN = 512, 512


def _merge_specs(t):
    tm, tn = min(MERGE_TM, t), MERGE_TN
    y_spec = pl.BlockSpec((tm, BW), lambda i, j: (i, 0))
    w_spec = pl.BlockSpec((3, BW, tn), lambda i, j: (0, 0, j))
    gate_specs = [pl.BlockSpec((tm, tn), functools.partial(lambda i, j, o: (i, o + j), o=(C_GATE + n * D_MODEL) // tn))
                  for n in range(3)]
    return tm, tn, [y_spec] * 3 + [w_spec] + gate_specs


def _merge_fwd(ys, wb, proj, name):
    t = proj.shape[0]
    tm, tn, in_specs = _merge_specs(t)

    def body(y0, y1, y2, w_ref, g0, g1, g2, o_ref):
        acc = jnp.zeros((tm, tn), F32)
        for n, (y, g) in enumerate(((y0, g0), (y1, g1), (y2, g2))):
            acc = acc + _dot(y[...].astype(BF16), w_ref[n], NN) * _sigmoid(g[...])
        o_ref[...] = acc.astype(o_ref.dtype)

    return pl.pallas_call(
        body, grid=(t // tm, D_MODEL // tn), in_specs=in_specs,
        out_specs=pl.BlockSpec((tm, tn), lambda i, j: (i, j)),
        out_shape=jax.ShapeDtypeStruct((t, D_MODEL), BF16), name=name,
    )(*ys, wb, proj, proj, proj)


def _merge_bwd(ys, wb, proj, dmerged, name):
    t = proj.shape[0]
    tm, tn, in_specs = _merge_specs(t)

    def body(y0, y1, y2, w_ref, g0, g1, g2, dm_ref, dg_ref, dt_ref):
        dm = dm_ref[...]
        for n, (y, g) in enumerate(((y0, g0), (y1, g1), (y2, g2))):
            tn_ = _dot(y[...].astype(BF16), w_ref[n], NN)
            sg = _sigmoid(g[...])
            dg_ref[n] = (dm * tn_ * sg * (1.0 - sg)).astype(dg_ref.dtype)
            dt_ref[n] = (dm * sg).astype(dt_ref.dtype)

    o3 = pl.BlockSpec((3, tm, tn), lambda i, j: (0, i, j))
    return pl.pallas_call(
        body, grid=(t // tm, D_MODEL // tn), in_specs=in_specs + [pl.BlockSpec((tm, tn), lambda i, j: (i, j))],
        out_specs=[o3, o3], out_shape=[jax.ShapeDtypeStruct((3, t, D_MODEL), BF16)] * 2, name=name,
    )(*ys, wb, proj, proj, proj, dmerged)


def _prep_rows(sp):
    z4, z112 = jnp.zeros((1, 4), F32), jnp.zeros((1, 112), F32)
    bias_row = jnp.concatenate([sp['b_fox_f'], z4, sp['dn_dt_bias'], z112], axis=1)
    alog_row = jnp.concatenate([jnp.zeros((1, 12), F32), sp['dn_a_log'], z112], axis=1)
    return bias_row, alog_row


def _sc_fwd_tile(sb, sc, sv, w):
    return (sb * _conv(sc * sv, w),)


def _ffn_act(ug, uv):
    return _silu(ug) * uv


W_IN_COLS = [(C_FQ, 0, 1536), (C_SB, 1544, 1536), (C_DQ, 3080, 1536), (C_DZ, 4624, 512), (C_GATE, 5136, 3072),
             (C_SMALL, 1536, 8), (C_SMALL + 8, 4616, 8)]
W_IN_SHARD = IN_ORIG // N_DEV


def _w_in_segments():
    out = []
    for d, o, w in W_IN_COLS:
        end = o + w
        while o < end:
            k = o // W_IN_SHARD
            n = min(end, (k + 1) * W_IN_SHARD) - o
            out.append((d, k, o - k * W_IN_SHARD, n))
            o, d = o + n, d + n
    return out


def _w_in_assemble(g, name):
    tm = 256

    def body(g_ref, o_ref):
        for d, k, s, n in _w_in_segments():
            o_ref[:, d:d + n] = g_ref[k, :, s:s + n]
        o_ref[:, IN_ORIG:IN_P] = jnp.zeros((tm, IN_P - IN_ORIG), o_ref.dtype)

    return pl.pallas_call(
        body, grid=(D_MODEL // tm,),
        in_specs=[pl.BlockSpec((N_DEV, tm, W_IN_SHARD), lambda i: (0, i, 0))],
        out_specs=pl.BlockSpec((tm, IN_P), lambda i: (i, 0)),
        out_shape=jax.ShapeDtypeStruct((D_MODEL, IN_P), g.dtype), name=name,
    )(g)


W_IN_SPLIT_TM = 128


def _w_in_split_by_device(g, name):
    tm = W_IN_SPLIT_TM

    def body(g_ref, own_ref, send_ref):
        me = 4 * lax.axis_index("x") + 2 * lax.axis_index("y") + lax.axis_index("c")
        for d, k, s, n in _w_in_segments():
            val = g_ref[:, d:d + n]
            send_ref[k, :, s:s + n] = val.astype(send_ref.dtype)

            @pl.when(me == k)
            def _():
                own_ref[:, s:s + n] = val

    return pl.pallas_call(
        body, grid=(D_MODEL // tm,), in_specs=[pl.BlockSpec((tm, IN_P), lambda i: (i, 0))],
        out_specs=[pl.BlockSpec((tm, W_IN_SHARD), lambda i: (i, 0)), pl.BlockSpec((N_DEV, tm, W_IN_SHARD), lambda i: (0, i, 0))],
        out_shape=[jax.ShapeDtypeStruct((D_MODEL, W_IN_SHARD), F32), jax.ShapeDtypeStruct((N_DEV, D_MODEL, W_IN_SHARD), BF16)],
        name=name,
    )(g)


def _layer_fwd(x, p_i, w, sp, tag, late_weights=None):
    t = x.shape[0]
    sv = {'x': x}
    (hn,) = _ew(lambda a, g: (_rms(a, g),), [x], [sp['g_mix']], [(D_MODEL, BF16)], name=f'rms_mix_{tag}')
    proj = _mm(hn, w['w_in'], 'nn', [F32], name=f'in_proj_{tag}')
    bias_row, alog_row = _prep_rows(sp)
    aux = _prep_fwd(proj, bias_row, alog_row, f'prep_{tag}')
    cf = aux[:, :FOX_HEADS].T
    cfc, cfr = cf[:, :, None], cf[:, None, :]
    y_fox, lse = _att_fwd(proj, sp['fox_q_gain'], sp['fox_k_gain'], cfc, cfr, f'fox_fwd_{tag}')
    (y_sc,) = _cb(_sc_fwd_tile, [(proj, C_SB), (proj, C_SC), (proj, C_SV)], [(w['sc_conv_w'], 0)], [F32], [],
                  tc=256, nblk=2, name=f'sc_fwd_{tag}')
    gb = jnp.stack([aux[:, 12:16].T, aux[:, 8:12].T], axis=-1)
    y_dn, o_dn, states, *dn_loc = _dn_fwd(proj, w['dn_conv_w'], gb, sp['dn_norm_gain'], f'dn_fwd_{tag}')
    ys = (y_fox, y_sc, y_dn)
    if late_weights is not None:
        w = {**w, **late_weights(y_dn)}
    merged = _merge_fwd(ys, w['w_branch'], proj, f'merge_fwd_{tag}')
    x1 = _mm(merged, w['w_o'], 'nn', [F32], epi=lambda acc, r: (acc + r,), epi_args=(x,), name=f'o_proj_{tag}')
    (hf,) = _ew(lambda a, g: (_rms(a, g),), [x1], [sp['g_ffn']], [(D_MODEL, BF16)], name=f'rms_ffn_{tag}')
    up = _mm(hf, w['w_up'], 'nt', [F32], name=f'up_proj_{tag}')
    (act,) = _cb(lambda ug, uv, wg, wv: (_ffn_act(_conv(ug, wg), _conv(uv, wv)),), [(up, 0), (up, D_FF)],
                 [(w['ffn_conv_w'], 0), (w['ffn_conv_w'], D_FF)], [BF16], [], tc=256, nblk=D_FF // 256, name=f'ffn_act_{tag}')
    x2 = _mm(act, w['w_down'], 'nn', [F32], epi=lambda acc, r: (acc + r,), epi_args=(x1,), name=f'down_proj_{tag}')
    (hp,) = _ew(lambda a, g: (_rms(a, g),), [x2], [sp['g_ple']], [(D_MODEL, BF16)], name=f'rms_ple_{tag}')
    gp = _mm(hp, w['w_ple_gate'], 'nn', [F32], name=f'ple_gate_{tag}')
    x3 = _mm(p_i, w['w_ple'], 'nn', [F32], epi=lambda acc, g, r: (r + _sigmoid(g) * acc,), epi_args=(gp, x2), name=f'ple_{tag}')
    sv.update(hn=hn, proj=proj, aux=aux, cfc=cfc, cfr=cfr, lse=lse, ys=ys, gb=gb, o_dn=o_dn,
              states=states, merged=merged, x1=x1, hf=hf, up=up, act=act, x2=x2, hp=hp, gp=gp, p=p_i,
              bias_row=bias_row, alog_row=alog_row, w=w, dn_loc=dn_loc)
    return x3, sv


def _rms_bwd(x, g, dh, dres, name):
    def fn(xv, dhv, dr, gv):
        _, vjp = jax.vjp(_rms, xv, gv)
        dx, dg = vjp(dhv)
        return dr + dx, dg
    return _ew(fn, [x, dh, dres], [g], [(D_MODEL, F32)], [(1, D_MODEL)], name=name)


def _layer_bwd(dx3, sv, sp, tag, early_grads=None, last_grad=None):
    t = dx3.shape[0]
    w = sv['w']
    g = {}
    def ple_epi(acc, gpv, d):
        s = _sigmoid(gpv)
        return d * acc * s * (1.0 - s), d * s
    dgp, de = _mm(sv['p'], w['w_ple'], 'nn', [BF16, BF16], epi=ple_epi, epi_args=(sv['gp'], dx3), name=f'ple_bwd_{tag}')
    g['w_ple'] = _mm(sv['p'], de, 'tn', [F32], name=f'd_w_ple_{tag}')
    g['w_ple_gate'] = _mm(sv['hp'], dgp, 'tn', [F32], name=f'd_w_ple_gate_{tag}')
    dhp = _mm(dgp, w['w_ple_gate'], 'nt', [F32], name=f'd_hp_{tag}')
    dx2, g['g_ple'] = _rms_bwd(sv['x2'], sp['g_ple'], dhp, dx3, f'rms_ple_bwd_{tag}')
    dact = _mm(dx2, w['w_down'], 'nt', [F32], name=f'd_act_{tag}')
    g['w_down'] = _mm(sv['act'], dx2, 'tn', [F32], name=f'd_w_down_{tag}')

    def ffn_bwd_tile(ug, uv, da, wg, wv):
        cg, cv = _conv(ug, wg), _conv(uv, wv)
        _, vjp = jax.vjp(_ffn_act, cg, cv)
        dcg, dcv = vjp(da)
        dug, dwg = _conv_bwd(ug, wg, dcg)
        duv, dwv = _conv_bwd(uv, wv, dcv)
        return dug, duv, jnp.concatenate(dwg, axis=0), jnp.concatenate(dwv, axis=0)
    dupg, dupv, dwg, dwv = _cb(ffn_bwd_tile, [(sv['up'], 0), (sv['up'], D_FF), (dact, 0)],
                               [(w['ffn_conv_w'], 0), (w['ffn_conv_w'], D_FF)], [BF16, BF16], [3, 3], tc=256,
                               nblk=D_FF // 256, name=f'ffn_act_bwd_{tag}')
    dup = jnp.concatenate([dupg, dupv], axis=1)
    g['ffn_conv_w'] = jnp.concatenate([dwg, dwv], axis=1)
    g['w_up'] = _mm(dup, sv['hf'], 'tn', [F32], name=f'd_w_up_{tag}')
    dhf = _mm(dup, w['w_up'], 'nn', [F32], name=f'd_hf_{tag}')
    dx1, g['g_ffn'] = _rms_bwd(sv['x1'], sp['g_ffn'], dhf, dx2, f'rms_ffn_bwd_{tag}')
    dmerged = _mm(dx1, w['w_o'], 'nt', [F32], name=f'd_merged_{tag}')
    g['w_o'] = _mm(sv['merged'], dx1, 'tn', [F32], name=f'd_w_o_{tag}')
    dgate, dtn = _merge_bwd(sv['ys'], w['w_branch'], sv['proj'], dmerged, f'merge_bwd_{tag}')
    dys, dwb = [], []
    for n in range(3):
        dys.append(_mm(dtn[n], w['w_branch'][n], 'nt', [F32], name=f'd_y{n}_{tag}'))
        dwb.append(_mm(sv['ys'][n], dtn[n], 'tn', [F32], name=f'd_w_branch{n}_{tag}'))
    g['w_branch'] = jnp.stack(dwb)
    if early_grads is not None:
        sp = early_grads(g, sp)
    ddq, ddk, ddv, ddz, dwq, dwk, dwv_, dgb, dng = _dn_bwd(sv['proj'], w['dn_conv_w'], sv['gb'], sp['dn_norm_gain'],
                                                           sv['o_dn'], sv['states'], sv['dn_loc'], dys[2], f'dn_bwd_{tag}')
    g['dn_conv_w'] = jnp.concatenate([dwq, dwk, dwv_], axis=1)
    g['dn_norm_gain'] = jnp.sum(dng, axis=0)
    def sc_bwd_tile(sb, sc, svv, dy, wv):
        u = sc * svv
        dsb = dy * _conv(u, wv)
        du, dws = _conv_bwd(u, wv, dy * sb)
        return dsb, du * svv, du * sc, jnp.concatenate(dws, axis=0)
    dsb, dsc, dsv, g['sc_conv_w'] = _cb(sc_bwd_tile, [(sv['proj'], C_SB), (sv['proj'], C_SC), (sv['proj'], C_SV), (dys[1], 0)],
                                        [(w['sc_conv_w'], 0)], [BF16, BF16, BF16], [3], tc=256, nblk=2, name=f'sc_bwd_{tag}')
    dfq, dfk, dfv, dcfc, dcfr, dqg, dkg = _att_bwd(sv['proj'], sp['fox_q_gain'], sp['fox_k_gain'], sv['cfc'], sv['cfr'],
                                                   sv['lse'], sv['ys'][0], dys[0], f'fox_bwd_{tag}')
    g['fox_q_gain'] = jnp.sum(dqg, axis=0)
    g['fox_k_gain'] = jnp.sum(dkg, axis=0)
    dcf = (dcfc[:, :, 0] + dcfr[:, 0, :]).T
    daux = jnp.concatenate([dcf, dgb[:, :, 1].T, dgb[:, :, 0].T, jnp.zeros((t, 112), F32)], axis=1)
    dsmall, dbias, dalog = _prep_bwd(sv['proj'], sv['bias_row'], sv['alog_row'], daux, f'prep_bwd_{tag}')
    g['b_fox_f'] = dbias[:, 0:8]
    g['dn_dt_bias'] = dbias[:, 12:16]
    g['dn_a_log'] = dalog[:, 12:16]
    dproj = jnp.concatenate([dfq, dfk, dfv, dsb, dsc, dsv, ddq, ddk, ddv, ddz, dgate[0], dgate[1], dgate[2], dsmall], axis=1)
    g['w_in'] = _mm(sv['hn'], dproj, 'tn', [F32], name=f'd_w_in_{tag}')
    if last_grad is not None:
        sp = last_grad(g, sp)
    dhn = _mm(dproj, w['w_in'], 'nt', [F32], name=f'd_hn_{tag}')
    dx, g['g_mix'] = _rms_bwd(sv['x'], sp['g_mix'], dhn, dx1, f'rms_mix_bwd_{tag}')
    return dx, g


def _loss_bwd(y, target, name):
    inv = 1.0 / y.shape[1]

    def fn(yv, tv):
        err = yv - tv
        return err * inv, jnp.zeros((8, 128), F32) + 0.5 * inv * jnp.sum(err * err)
    return _ew(fn, [y, target], [], [(y.shape[1], F32)], [(8, 128)], name=name)


PACK_W = 1024
FULL_SHAPE = {'w_in': (D_MODEL, IN_ORIG), 'w_branch': (3, BW, D_MODEL), 'w_o': (D_MODEL, D_MODEL), 'w_up': (2 * D_FF, D_MODEL),
              'w_down': (D_FF, D_MODEL), 'w_ple_gate': (D_MODEL, D_MODEL), 'w_ple': (PLE_DIM, D_MODEL),
              'sc_conv_w': (3, BW), 'dn_conv_w': (4, 3 * BW), 'ffn_conv_w': (3, 2 * D_FF)}
SMALL_SHAPE = {'g_mix': D_MODEL, 'b_fox_f': FOX_HEADS, 'fox_q_gain': FOX_DH, 'fox_k_gain': FOX_DH, 'dn_a_log': DN_HEADS,
               'dn_dt_bias': DN_HEADS, 'dn_norm_gain': DN_DH, 'g_ffn': D_MODEL, 'g_ple': D_MODEL}


def _shard_shape(name):
    s = list(FULL_SHAPE[name])
    s[SHARDED[name]] //= N_DEV
    return tuple(s)


def _full_from_gathered(g, name):
    sh, ax = _shard_shape(name), SHARDED[name]
    blocks = jnp.moveaxis(g, 0, ax)
    return blocks.reshape(sh[:ax] + (N_DEV * sh[ax],) + sh[ax + 1:])


def _by_dest(full, name):
    sh, ax = _shard_shape(name), SHARDED[name]
    return jnp.moveaxis(full.reshape(sh[:ax] + (N_DEV, sh[ax]) + sh[ax + 1:]), ax, 0)


def _flat_pack(arrs):
    flat = jnp.concatenate([a.reshape(-1).astype(F32) for a in arrs])
    rows = -(-flat.shape[0] // (8 * PACK_W)) * 8
    return jnp.pad(flat, (0, rows * PACK_W - flat.shape[0])).reshape(rows, PACK_W)


def _flat_unpack(pack, shapes):
    flat, out, off = pack.reshape(-1), [], 0
    for s in shapes:
        n = 1
        for d in s:
            n *= d
        out.append(flat[off:off + n].reshape(s))
        off += n
    return out


SMALL_SHAPES = [(DEPTH, SMALL_SHAPE[n]) for n in SMALL]
SMALL_LOSS_AT = sum(DEPTH * SMALL_SHAPE[n] for n in SMALL)
CONV_SHARD_SHAPES = [(DEPTH,) + _shard_shape(n) for n in CONVW]
CONV_FULL_SHAPES = [(DEPTH,) + FULL_SHAPE[n] for n in CONVW]


def _pick_tm(m, width):
    best = None
    for tm in range(16, m + 1, 16):
        if m % tm == 0 and tm * width * 4 <= (1 << 20):
            best = tm
    return best if best is not None else m


HBM_SPEC = pl.BlockSpec(memory_space=pltpu.HBM)


def _place():
    x, y, c = lax.axis_index("x"), lax.axis_index("y"), lax.axis_index("c")
    return x, y, c, [(1 - x, y), (x, 1 - y), (1 - x, 1 - y)]


def _all_gather(arrs, name):
    n = len(arrs)

    def body(*refs):
        ins, outs = refs[:n], refs[n:2 * n]
        send_sems, recv_sems = refs[2 * n:]
        x, y, c, chips = _place()
        me, sibling = (x, y, c), (x, y, 1 - c)

        def block(a, p):
            return outs[a].at[4 * p[0] + 2 * p[1] + p[2]]

        def copy(a, k, blk, to, src=None):
            return pltpu.make_async_remote_copy(src_ref=block(a, blk) if src is None else src, dst_ref=block(a, blk),
                                                send_sem=send_sems.at[a, k], recv_sem=recv_sems.at[a, k],
                                                device_id=to, device_id_type=MESH)

        first, passed = [], []
        for a in range(n):
            first.append(copy(a, 0, me, sibling, src=ins[a]))
            first += [copy(a, 1 + j, me, (*chip, c), src=ins[a]) for j, chip in enumerate(chips)]
        for cp in first:
            cp.start()
        for j, chip in enumerate(chips):
            for a in range(n):
                copy(a, 1 + j, (*chip, c), me).wait_recv()
                fwd = copy(a, 4 + j, (*chip, c), sibling)
                fwd.start()
                passed.append(fwd)
        for a in range(n):
            copy(a, 0, sibling, me).wait_recv()
            for j, chip in enumerate(chips):
                copy(a, 4 + j, (*chip, 1 - c), me).wait_recv()
        for cp in first + passed:
            cp.wait_send()

    gathered = pl.pallas_call(
        body, in_specs=[HBM_SPEC] * n, out_specs=[HBM_SPEC] * n,
        out_shape=[jax.ShapeDtypeStruct((N_DEV,) + a.shape, a.dtype) for a in arrs],
        scratch_shapes=[pltpu.SemaphoreType.DMA((n, 7)), pltpu.SemaphoreType.DMA((n, 7))], name=name,
    )(*arrs)
    me = 4 * lax.axis_index("x") + 2 * lax.axis_index("y") + lax.axis_index("c")
    return [lax.dynamic_update_index_in_dim(g, a, me, 0) for g, a in zip(gathered, arrs)]


SEM_SPEC = pl.BlockSpec(memory_space=pltpu.SEMAPHORE)
DATAFLOW = pltpu.SideEffectType.DATAFLOW_SIDE_EFFECTING
ALL_PEERS = (1, 2, 3, 4, 5, 6, 7)
NEAR_PEERS = (1, 4, 2, 6)


def _direct_copies(src_refs, zone_refs, send_sems, recv_sems, per_dest, peers):
    x, y, c, _ = _place()
    me = 4 * x + 2 * y + c
    cps = []
    for a, (src, zone) in enumerate(zip(src_refs, zone_refs)):
        for r, bits in enumerate(peers):
            px = 1 - x if bits & 4 else x
            py = 1 - y if bits & 2 else y
            pc = 1 - c if bits & 1 else c
            cps.append(pltpu.make_async_remote_copy(
                src_ref=src.at[4 * px + 2 * py + pc] if per_dest else src, dst_ref=zone.at[me],
                send_sem=send_sems.at[a * len(peers) + r], recv_sem=recv_sems.at[a * len(peers) + r],
                device_id=(px, py, pc), device_id_type=MESH))
    return cps


def _direct_start(srcs, per_dest, after, name, peers=ALL_PEERS):
    n = len(srcs)
    zones = [lax.empty((N_DEV,) + (s.shape[1:] if per_dest else s.shape), s.dtype) for s in srcs]

    def body(*refs):
        src_refs, zone_refs = refs[:n], refs[n:2 * n]
        send_sems, recv_sems = refs[2 * n + 1:2 * n + 3]
        for cp in _direct_copies(src_refs, zone_refs, send_sems, recv_sems, per_dest, peers):
            cp.start()
        refs[-1][...] = jnp.zeros_like(refs[-1])

    sems = pltpu.SemaphoreType.DMA((n * len(peers),))
    res = pl.pallas_call(
        body, name=name, in_specs=[HBM_SPEC] * (2 * n + 1),
        out_shape=[sems, sems] + [pltpu.HBM(s.shape, s.dtype) for s in srcs] + [pltpu.HBM(z.shape, z.dtype) for z in zones]
        + [jax.ShapeDtypeStruct((8, 128), F32)],
        out_specs=[SEM_SPEC, SEM_SPEC] + [HBM_SPEC] * (2 * n) + [pl.BlockSpec(memory_space=pltpu.VMEM)],
        input_output_aliases={i: 2 + i for i in range(2 * n)},
        compiler_params=pltpu.CompilerParams(has_side_effects=DATAFLOW),
    )(*[pltpu.with_memory_space_constraint(s, pltpu.HBM) for s in srcs],
      *[pltpu.with_memory_space_constraint(z, pltpu.HBM) for z in zones], after)
    return (res[0], res[1], list(res[2:2 + n]), list(res[2 + n:2 + 2 * n]), per_dest, peers), res[-1][0:1, 0:1]


def _direct_wait(started, after, name):
    send_sems, recv_sems, srcs, zones, per_dest, peers = started
    n = len(srcs)

    def body(*refs):
        src_refs, zone_refs = refs[:n], refs[n:2 * n]
        for cp in _direct_copies(src_refs, zone_refs, refs[2 * n], refs[2 * n + 1], per_dest, peers):
            cp.wait_send()
            cp.wait_recv()

    res = pl.pallas_call(
        body, name=name, in_specs=[HBM_SPEC] * (2 * n) + [SEM_SPEC, SEM_SPEC, HBM_SPEC],
        out_shape=[pltpu.HBM(s.shape, s.dtype) for s in srcs] + [pltpu.HBM(z.shape, z.dtype) for z in zones],
        out_specs=[HBM_SPEC] * (2 * n), input_output_aliases={i: i for i in range(2 * n)},
        compiler_params=pltpu.CompilerParams(has_side_effects=DATAFLOW),
    )(*srcs, *zones, send_sems, recv_sems, after)
    return list(res[n:])


def _gather_finish(zones, name):
    n = len(zones)

    def body(*refs):
        zone_in, zone_out = refs[:n], refs[n:2 * n]
        send_sems, recv_sems = refs[2 * n:]
        x, y, c, chips = _place()
        cps = []
        for a in range(n):
            for j, (cx, cy) in enumerate(chips):
                blk = 4 * cx + 2 * cy + c
                cp = pltpu.make_async_remote_copy(src_ref=zone_in[a].at[blk], dst_ref=zone_out[a].at[blk],
                                                  send_sem=send_sems.at[a, j], recv_sem=recv_sems.at[a, j],
                                                  device_id=(x, y, 1 - c), device_id_type=MESH)
                cp.start()
                cps.append(cp)
        for cp in cps:
            cp.wait()

    return pl.pallas_call(
        body, in_specs=[HBM_SPEC] * n, out_specs=[HBM_SPEC] * n,
        out_shape=[jax.ShapeDtypeStruct(z.shape, z.dtype) for z in zones], input_output_aliases={a: a for a in range(n)},
        scratch_shapes=[pltpu.SemaphoreType.DMA((n, 3)), pltpu.SemaphoreType.DMA((n, 3))], name=name,
    )(*zones)


def _adamw(w, g, m, v):
    m = ADAM_B1 * m + (1.0 - ADAM_B1) * g
    v = ADAM_B2 * v + (1.0 - ADAM_B2) * jnp.square(g)
    m_hat = m / (1.0 - ADAM_B1 ** ADAM_STEP)
    v_hat = v / (1.0 - ADAM_B2 ** ADAM_STEP)
    delta = -ADAM_LR * (m_hat / (jnp.sqrt(v_hat) + ADAM_EPS) + ADAM_WD * w)
    return delta, m, v


def _adamw_layer(w, m, v, own, parts, layer, prev, name):
    _, rows, c = w.shape
    tm = _pick_tm(rows, c)

    def body(w_ref, m_ref, v_ref, own_ref, parts_ref, *rest):
        g = own_ref[...]
        for j in range(parts_ref.shape[0]):
            g = g + parts_ref[j].astype(F32)
        delta, m2, v2 = _adamw(w_ref[...], g, m_ref[...], v_ref[...])
        g_ref, d_ref, m2_ref, v2_ref = rest[-4:]
        g_ref[...], d_ref[...], m2_ref[...], v2_ref[...] = g, delta, m2, v2

    wspec = pl.BlockSpec((None, tm, c), lambda i: (layer, i, 0))
    prev = list(prev) if prev is not None else []
    return pl.pallas_call(
        body, grid=(rows // tm,),
        in_specs=[wspec] * 3 + [pl.BlockSpec((tm, c), lambda i: (i, 0)), pl.BlockSpec((parts.shape[0], tm, c), lambda i: (0, i, 0))]
        + [HBM_SPEC] * len(prev),
        out_specs=[wspec] * 4, out_shape=[jax.ShapeDtypeStruct(w.shape, F32)] * 4,
        input_output_aliases={5 + k: k for k in range(len(prev))}, name=name,
    )(w, m, v, own, parts, *prev)


def _adamw_call(w, m, v, gparts, name):
    def fn(wv, mv, vv, *gs):
        g = gs[0].astype(F32)
        for gp in gs[1:]:
            g = g + gp.astype(F32)
        delta, m2, v2 = _adamw(wv, g, mv, vv)
        return g, delta, m2, v2
    rows, c = w.shape
    return _ew(fn, [w, m, v] + list(gparts), [], [(c, F32)] * 4, tm=_pick_tm(rows, c), name=name)


def kernel(x, p, g_mix, w_in, b_fox_f, fox_q_gain, fox_k_gain, sc_conv_w, dn_conv_w, dn_a_log, dn_dt_bias,
           dn_norm_gain, w_branch, w_o, g_ffn, w_up, ffn_conv_w, w_down, g_ple, w_ple_gate, w_ple, loss_target,
           m_g_mix, m_w_in, m_b_fox_f, m_fox_q_gain, m_fox_k_gain, m_sc_conv_w, m_dn_conv_w, m_dn_a_log,
           m_dn_dt_bias, m_dn_norm_gain, m_w_branch, m_w_o, m_g_ffn, m_w_up, m_ffn_conv_w, m_w_down, m_g_ple,
           m_w_ple_gate, m_w_ple, v_g_mix, v_w_in, v_b_fox_f, v_fox_q_gain, v_fox_k_gain, v_sc_conv_w, v_dn_conv_w,
           v_dn_a_log, v_dn_dt_bias, v_dn_norm_gain, v_w_branch, v_w_o, v_g_ffn, v_w_up, v_ffn_conv_w, v_w_down,
           v_g_ple, v_w_ple_gate, v_w_ple):
    return _step(x, p, g_mix, w_in, b_fox_f, fox_q_gain, fox_k_gain, sc_conv_w, dn_conv_w, dn_a_log, dn_dt_bias,
                 dn_norm_gain, w_branch, w_o, g_ffn, w_up, ffn_conv_w, w_down, g_ple, w_ple_gate, w_ple, loss_target,
                 m_g_mix, m_w_in, m_b_fox_f, m_fox_q_gain, m_fox_k_gain, m_sc_conv_w, m_dn_conv_w, m_dn_a_log,
                 m_dn_dt_bias, m_dn_norm_gain, m_w_branch, m_w_o, m_g_ffn, m_w_up, m_ffn_conv_w, m_w_down, m_g_ple,
                 m_w_ple_gate, m_w_ple, v_g_mix, v_w_in, v_b_fox_f, v_fox_q_gain, v_fox_k_gain, v_sc_conv_w,
                 v_dn_conv_w, v_dn_a_log, v_dn_dt_bias, v_dn_norm_gain, v_w_branch, v_w_o, v_g_ffn, v_w_up,
                 v_ffn_conv_w, v_w_down, v_g_ple, v_w_ple_gate, v_w_ple)


def _step(*args):
    names = ['x', 'p'] + WEIGHTS + ['loss_target'] + ['m_' + n for n in WEIGHTS] + ['v_' + n for n in WEIGHTS]
    assert len(args) == len(names)
    a = dict(zip(names, args))
    for n, perm in HELD_TRANSPOSED.items():
        for k in (n, 'm_' + n, 'v_' + n):
            a[k] = jnp.transpose(a[k], perm)
    x, target = a['x'][0], a['loss_target'][0]
    p = a['p'][:, 0]
    dev = 4 * lax.axis_index("x") + 2 * lax.axis_index("y") + lax.axis_index("c")

    LATE = [n for n in BIG if n != 'w_in']
    sps = [{n: a[n][layer][None, :] for n in SMALL} for layer in range(DEPTH)]
    shards = [{n: a[n][layer].astype(BF16) for n in BIG} for layer in range(DEPTH)]

    def by_device(full, n):
        by_dest = _by_dest(full, n)
        return lax.dynamic_index_in_dim(by_dest, dev, axis=0, keepdims=False), by_dest.astype(BF16)

    def blank(zone):
        return lax.dynamic_update_index_in_dim(zone, jnp.zeros(zone.shape[1:], zone.dtype), dev, 0)

    w_in_all, conv_all = _all_gather([shards[0]['w_in'], _flat_pack([a[n] for n in CONVW])], 'gather_w_in_l0')
    conv_by_dev = [_flat_unpack(conv_all[d], CONV_SHARD_SHAPES) for d in range(N_DEV)]
    conv_full = {n: jnp.concatenate([conv_by_dev[d][i] for d in range(N_DEV)], axis=2) for i, n in enumerate(CONVW)}
    late0_started, zero = _direct_start([shards[0][n] for n in LATE], False, w_in_all, 'weights_rest_l0_start', NEAR_PEERS)
    w1_started, zero = _direct_start([shards[1][n] for n in BIG], False, late0_started[2][0], 'weights_l1_start', NEAR_PEERS)
    sps[0]['g_mix'] = sps[0]['g_mix'] + zero

    def late_weights0(after):
        zones = _gather_finish(_direct_wait(late0_started, after, 'weights_rest_l0_wait'), 'weights_rest_l0_finish')
        return {n: _full_from_gathered(lax.dynamic_update_index_in_dim(z, shards[0][n], dev, 0), n) for n, z in zip(LATE, zones)}

    w0 = {n: conv_full[n][0] for n in CONVW}
    w0['w_in'] = _w_in_assemble(w_in_all, 'w_in_assemble_l0')
    h, sv0 = _layer_fwd(x, p[0], w0, sps[0], 'l0', late_weights0)
    zones = [lax.dynamic_update_index_in_dim(z, shards[1][n], dev, 0)
             for n, z in zip(BIG, _gather_finish(_direct_wait(w1_started, h, 'weights_l1_wait'), 'weights_l1_finish'))]
    w1 = {n: _full_from_gathered(z, n) for n, z in zip(BIG, zones) if n != 'w_in'}
    w1.update({n: conv_full[n][1] for n in CONVW})
    w1['w_in'] = _w_in_assemble(zones[list(BIG).index('w_in')], 'w_in_assemble_l1')
    h, sv1 = _layer_fwd(h, p[1], w1, sps[1], 'l1')
    dh, loss_part = _loss_bwd(h, target, 'loss')
    grads = [None] * DEPTH
    dh, grads[1] = _layer_bwd(dh, sv1, sps[1], 'l1')
    owns1, sends1 = {}, {}
    for n in BIG:
        owns1[n], sends1[n] = _w_in_split_by_device(grads[1][n], 'w_in_grad_split_l1') if n == 'w_in' else by_device(grads[1][n], n)
    g1_started, zero = _direct_start([sends1[n] for n in BIG], True, dh, 'grads_l1_start')
    early = {}

    def early_grads0(g, sp):
        owns, sends = zip(*[by_device(g[n], n) for n in LATE])
        early['started'], zero = _direct_start(list(sends), True, g['w_branch'], 'grads_rest_l0_start')
        early['owns'] = dict(zip(LATE, owns))
        return {**sp, 'dn_norm_gain': sp['dn_norm_gain'] + zero}

    def last_grad0(g, sp):
        early['own_in'], send = _w_in_split_by_device(g['w_in'], 'w_in_grad_split_l0')
        early['in_started'], zero = _direct_start([send], True, g['w_in'], 'grads_w_in_l0_start')
        return {**sp, 'g_mix': sp['g_mix'] + zero}

    dh, grads[0] = _layer_bwd(dh, sv0, {**sps[0], 'g_ple': sps[0]['g_ple'] + zero}, 'l0', early_grads0, last_grad0)
    grad_x = dh[None]
    own_in0, in0_started = early['own_in'], early['in_started']
    sent = in0_started[2][0]
    zones1 = dict(zip(BIG, _direct_wait(g1_started, sent, 'grads_l1_wait')))
    zones0 = dict(zip(LATE, _direct_wait(early['started'], sent, 'grads_rest_l0_wait')))

    def adamw(n, layer, own, zone, prev):
        rows = (-1, a[n].shape[-1])
        own = own.reshape(rows)
        view = lambda t: t.reshape((DEPTH,) + own.shape)
        return _adamw_layer(view(a[n]), view(a['m_' + n]), view(a['v_' + n]), own, blank(zone).reshape((N_DEV,) + own.shape),
                            layer, prev, f'adamw_{n}_l{layer}')

    out, done = {}, []
    for n in LATE:
        res = adamw(n, 0, early['owns'][n], zones0[n], adamw(n, 1, owns1[n], zones1[n], None))
        out[n] = [r.reshape(a[n].shape) for r in res]
        done.append(res[0][0, 0:1, 0:1])
    res = adamw('w_in', 1, owns1['w_in'], zones1['w_in'], None)
    done.append(res[0][1, 0:1, 0:1])
    (zone_in0,) = _direct_wait(in0_started, jnp.concatenate(done, axis=1), 'grads_w_in_l0_wait')
    out['w_in'] = [r.reshape(a['w_in'].shape) for r in adamw('w_in', 0, own_in0, zone_in0, res)]

    small_pack = _flat_pack([jnp.stack([grads[layer][n].reshape(-1) for layer in range(DEPTH)]) for n in SMALL] + [loss_part[0, 0]])
    conv_pack = _flat_pack([jnp.stack([grads[layer][n] for layer in range(DEPTH)]) for n in CONVW])
    small_started, _ = _direct_start([small_pack, conv_pack], False, small_pack, 'small_grads_start')
    small_all, conv_all = [lax.dynamic_update_index_in_dim(z, s, dev, 0) for z, s in
                           zip(_direct_wait(small_started, out['w_in'][0], 'small_grads_wait'), (small_pack, conv_pack))]
    res = _adamw_call(_flat_pack([a[n] for n in SMALL]), _flat_pack([a['m_' + n] for n in SMALL]),
                      _flat_pack([a['v_' + n] for n in SMALL]), [small_all[d] for d in range(N_DEV)], 'adamw_replicated')
    loss = res[0].reshape(-1)[SMALL_LOSS_AT]
    for k, r in enumerate(res):
        for n, val in zip(SMALL, _flat_unpack(r, SMALL_SHAPES)):
            out.setdefault(n, [None] * 4)[k] = val
    (conv_sum,) = _ew(lambda *gs: (functools.reduce(lambda s, t: s + t, gs),), [conv_all[d] for d in range(N_DEV)], [],
                      [(PACK_W, F32)], tm=conv_pack.shape[0], name='conv_grad_sum')
    conv_own = [lax.dynamic_slice_in_dim(g, dev * (g.shape[2] // N_DEV), g.shape[2] // N_DEV, axis=2)
                for g in _flat_unpack(conv_sum, CONV_FULL_SHAPES)]
    res = _adamw_call(_flat_pack([a[n] for n in CONVW]), _flat_pack([a['m_' + n] for n in CONVW]),
                      _flat_pack([a['v_' + n] for n in CONVW]), [_flat_pack(conv_own)], 'adamw_conv')
    for k, r in enumerate(res):
        for n, val in zip(CONVW, _flat_unpack(r, CONV_SHARD_SHAPES)):
            out.setdefault(n, [None] * 4)[k] = val

    for n, perm in HELD_TRANSPOSED.items():
        out[n] = [jnp.transpose(r, perm) for r in out[n]]
    outs = [loss, grad_x]
    for k in range(4):
        outs += [out[n][k] for n in WEIGHTS]
    return tuple(outs)
```

```python
import functools

import jax
import jax.numpy as jnp
from jax import lax
from jax.experimental import pallas as pl
from jax.experimental.pallas import tpu as pltpu

F32 = jnp.float32
BF16 = jnp.bfloat16
HI = lax.Precision.HIGHEST

D_MODEL = 1024
DEPTH = 2
N_DEV = 8
PLE_DIM = 256
BW = 512
FOX_HEADS, FOX_DH = 8, 64
DN_HEADS, DN_DH = 4, 128
DN_CHUNK = 64
D_FF = 2816
EPS = 1e-6
NEG = -1e30

ADAM_LR, ADAM_B1, ADAM_B2, ADAM_EPS, ADAM_WD, ADAM_STEP = 0.001, 0.9, 0.999, 1e-08, 0.01, 10

C_FQ, C_FK, C_FV = 0, 512, 1024
C_SB, C_SC, C_SV = 1536, 2048, 2560
C_DQ, C_DK, C_DV, C_DZ = 3072, 3584, 4096, 4608
C_GATE = 5120
C_SMALL = 8192
IN_P = 8320
IN_ORIG = 8208

WEIGHTS = ['g_mix', 'w_in', 'b_fox_f', 'fox_q_gain', 'fox_k_gain', 'sc_conv_w', 'dn_conv_w', 'dn_a_log',
           'dn_dt_bias', 'dn_norm_gain', 'w_branch', 'w_o', 'g_ffn', 'w_up', 'ffn_conv_w', 'w_down', 'g_ple',
           'w_ple_gate', 'w_ple']
BIG = {'w_in': 1, 'w_branch': 2, 'w_o': 0, 'w_up': 0, 'w_down': 0, 'w_ple_gate': 0, 'w_ple': 1}
HELD_TRANSPOSED = {'w_up': (0, 2, 1)}
CONVW = {'sc_conv_w': 1, 'dn_conv_w': 1, 'ffn_conv_w': 1}
SHARDED = {**BIG, **CONVW}
SMALL = [n for n in WEIGHTS if n not in SHARDED]
MESH = pl.DeviceIdType.MESH


def _sigmoid(x):
    return 0.5 * (jnp.tanh(0.5 * x) + 1.0)


def _silu(x):
    return x * _sigmoid(x)


def _log1pexp_negabs(z):
    return jnp.log(1.0 + jnp.exp(-jnp.abs(z)))


def _log_sigmoid(z):
    return jnp.minimum(z, 0.0) - _log1pexp_negabs(z)


def _softplus(z):
    return jnp.maximum(z, 0.0) + _log1pexp_negabs(z)


def _rms(x, g):
    return x * lax.rsqrt(jnp.mean(x * x, axis=-1, keepdims=True) + EPS) * g


def _l2(x):
    return x * lax.rsqrt(jnp.sum(x * x, axis=-1, keepdims=True) + EPS)


def _dot(a, b, dims, precision=None):
    return lax.dot_general(a, b, (dims, ((), ())), preferred_element_type=F32, precision=precision)


NN = ((1,), (0,))
NT = ((1,), (1,))
TN = ((0,), (0,))


def _shift_down(x, s):
    if s == 0:
        return x
    t = lax.broadcasted_iota(jnp.int32, x.shape, 0)
    return jnp.where(t >= s, pltpu.roll(x, s, 0), 0.0)


def _shift_up(x, s):
    if s == 0:
        return x
    n = x.shape[0]
    t = lax.broadcasted_iota(jnp.int32, x.shape, 0)
    return jnp.where(t < n - s, pltpu.roll(x, n - s, 0), 0.0)


def _conv(x, w):
    k = w.shape[0]
    y = w[k - 1:k] * x
    for j in range(k - 1):
        y = y + w[j:j + 1] * _shift_down(x, k - 1 - j)
    return y


def _conv_bwd(x, w, dy):
    k = w.shape[0]
    dx = w[k - 1:k] * dy
    dws = []
    for j in range(k - 1):
        dx = dx + w[j:j + 1] * _shift_up(dy, k - 1 - j)
        dws.append(jnp.sum(dy * _shift_down(x, k - 1 - j), axis=0, keepdims=True))
    dws.append(jnp.sum(dy * x, axis=0, keepdims=True))
    return dx, dws


MM_VMEM_BUDGET = 36 << 20
MM_STEP_BYTES = 1 << 20


def _mm_tiles(m, n, k, a_size, b_size, tile_size, cast_a):
    best = None
    for tm in [d for d in (2048, 1024, 512, 256, 128) if d <= m and m % d == 0] or [m]:
        for tn in [d for d in range(128, min(n, 2048) + 1, 128) if n % d == 0] or [n]:
            vmem = 2 * tm * k * a_size + (2 * tm * k if cast_a else 0) + 2 * k * tn * b_size + 2 * tm * tn * tile_size
            if vmem > MM_VMEM_BUDGET:
                continue
            steps = (m // tm) * (n // tn)
            cost = m * k * a_size + (m // tm) * k * n * b_size + m * n * tile_size + steps * MM_STEP_BYTES
            if best is None or cost < best[0]:
                best = (cost, tm, tn)
    assert best is not None, (m, n, k)
    return best[1], best[2]


def _mm(a, b, mode, outs, *, epi=None, epi_args=(), name):
    if mode == 'nn':
        (m, k), (k2, n) = a.shape, b.shape
    elif mode == 'nt':
        (m, k), (n, k2) = a.shape, b.shape
    else:
        (k, m), (k2, n) = a.shape, b.shape
    assert k == k2, (a.shape, b.shape, mode)
    tile_size = sum(jnp.dtype(dt).itemsize for dt in outs) + sum(e.dtype.itemsize for e in epi_args if e.shape[0] != 1)
    tm, tn = _mm_tiles(m, n, k, a.dtype.itemsize, b.dtype.itemsize, tile_size, a.dtype != BF16)
    dims = {'nn': NN, 'nt': NT, 'tn': TN}[mode]
    a_spec = pl.BlockSpec((k, tm), lambda i, j: (0, i)) if mode == 'tn' else pl.BlockSpec((tm, k), lambda i, j: (i, 0))
    b_spec = pl.BlockSpec((tn, k), lambda i, j: (j, 0)) if mode == 'nt' else pl.BlockSpec((k, tn), lambda i, j: (0, j))
    e_specs = [pl.BlockSpec((1, tn), lambda i, j: (0, j)) if e.shape[0] == 1 else pl.BlockSpec((tm, tn), lambda i, j: (i, j))
               for e in epi_args]
    ne, no = len(epi_args), len(outs)
    cast_a = a.dtype != BF16

    def body(a_ref, b_ref, *rest):
        if cast_a:
            a_sc = rest[-1]

            @pl.when(pl.program_id(1) == 0)
            def _():
                a_sc[...] = a_ref[...].astype(BF16)
            av = a_sc[...]
        else:
            av = a_ref[...]
        acc = _dot(av, b_ref[...].astype(BF16), dims)
        vals = epi(acc, *[e[...] for e in rest[:ne]]) if epi is not None else (acc,)
        for o_ref, v in zip(rest[ne:ne + no], vals):
            o_ref[...] = v.astype(o_ref.dtype)

    res = pl.pallas_call(
        body, grid=(m // tm, n // tn),
        in_specs=[a_spec, b_spec] + e_specs,
        out_specs=[pl.BlockSpec((tm, tn), lambda i, j: (i, j)) for _ in outs],
        out_shape=[jax.ShapeDtypeStruct((m, n), dt) for dt in outs],
        scratch_shapes=[pltpu.VMEM(a_spec.block_shape, BF16)] if cast_a else [],
        name=name,
    )(a, b, *epi_args)
    return res[0] if no == 1 else res


def _ew(fn, tiled, bcast, outs, reds=(), *, tm=256, name):
    secs = [(t, 0, t.shape[1]) if not isinstance(t, tuple) else t for t in tiled]
    m = secs[0][0].shape[0]
    tm = min(tm, m)
    assert m % tm == 0
    in_specs = []
    for arr, off, w in secs:
        assert off % w == 0
        in_specs.append(pl.BlockSpec((tm, w), functools.partial(lambda i, c: (i, c), c=off // w)))
    in_specs += [pl.BlockSpec(b.shape, lambda i: (0, 0)) for b in bcast]
    nin, no = len(in_specs), len(outs)

    def body(*refs):
        vals = fn(*[r[...] for r in refs[:nin]])
        for r, v in zip(refs[nin:nin + no], vals[:no]):
            r[...] = v.astype(r.dtype)
        i = pl.program_id(0)
        for r, v in zip(refs[nin + no:], vals[no:]):
            @pl.when(i == 0)
            def _():
                r[...] = v

            @pl.when(i > 0)
            def _():
                r[...] += v

    res = pl.pallas_call(
        body, grid=(m // tm,), in_specs=in_specs,
        out_specs=[pl.BlockSpec((tm, c), lambda i: (i, 0)) for c, _ in outs] + [pl.BlockSpec(s, lambda i: (0, 0)) for s in reds],
        out_shape=[jax.ShapeDtypeStruct((m, c), dt) for c, dt in outs] + [jax.ShapeDtypeStruct(s, F32) for s in reds],
        name=name,
    )(*[s[0] for s in secs], *bcast)
    return res


def _cb(fn, cols, params, outs, pouts, *, tc, nblk, name):
    t = cols[0][0].shape[0]
    in_specs = []
    for arr, off in cols:
        assert off % tc == 0
        in_specs.append(pl.BlockSpec((t, tc), functools.partial(lambda c, o: (0, o + c), o=off // tc)))
    for arr, off in params:
        in_specs.append(pl.BlockSpec((arr.shape[0], tc), functools.partial(lambda c, o: (0, o + c), o=off // tc)))
    nin, no = len(in_specs), len(outs)

    def body(*refs):
        vals = fn(*[r[...] for r in refs[:nin]])
        for r, v in zip(refs[nin:], vals):
            r[...] = v.astype(r.dtype)

    return pl.pallas_call(
        body, grid=(nblk,), in_specs=in_specs,
        out_specs=[pl.BlockSpec((t, tc), lambda c: (0, c)) for _ in outs] + [pl.BlockSpec((k, tc), lambda c: (0, c)) for k in pouts],
        out_shape=[jax.ShapeDtypeStruct((t, nblk * tc), dt) for dt in outs] + [jax.ShapeDtypeStruct((k, nblk * tc), F32) for k in pouts],
        name=name,
    )(*[c[0] for c in cols], *[p[0] for p in params])


CUM_BLK = 256


def _prep_point(s, bias, alog):
    lane = lax.broadcasted_iota(jnp.int32, s.shape, 1)
    z = s + bias
    return jnp.where(lane < 8, _log_sigmoid(z),
                     jnp.where(lane < 12, _sigmoid(s),
                               jnp.where(lane < 16, -jnp.exp(alog) * _softplus(z), 0.0)))


def _tri(n, upper):
    r = lax.broadcasted_iota(jnp.int32, (n, n), 0)
    c = lax.broadcasted_iota(jnp.int32, (n, n), 1)
    return (r <= c if upper else r >= c).astype(F32)


def _prep_fwd(proj, bias_row, alog_row, name):
    t = proj.shape[0]
    nb = t // CUM_BLK

    def body(s_ref, b_ref, a_ref, o_ref):
        pre = _prep_point(s_ref[...], b_ref[...], a_ref[...])
        lane = lax.broadcasted_iota(jnp.int32, (CUM_BLK, 128), 1)
        tri = _tri(CUM_BLK, False)
        carry = jnp.zeros((1, 128), F32)
        for blk in range(nb):
            xb = pre[blk * CUM_BLK:(blk + 1) * CUM_BLK]
            cb = _dot(tri, xb, NN, HI) + carry
            carry = cb[CUM_BLK - 1:CUM_BLK]
            o_ref[blk * CUM_BLK:(blk + 1) * CUM_BLK, :] = jnp.where(lane < 8, cb, xb)

    return pl.pallas_call(
        body, grid=(1,),
        in_specs=[pl.BlockSpec((t, 128), lambda i: (0, C_SMALL // 128)), pl.BlockSpec((1, 128), lambda i: (0, 0)),
                  pl.BlockSpec((1, 128), lambda i: (0, 0))],
        out_specs=pl.BlockSpec((t, 128), lambda i: (0, 0)),
        out_shape=jax.ShapeDtypeStruct((t, 128), F32), name=name,
    )(proj, bias_row, alog_row)


def _prep_bwd(proj, bias_row, alog_row, daux, name):
    t = proj.shape[0]
    nb = t // CUM_BLK

    def body(s_ref, b_ref, a_ref, d_ref, ds_ref, db_ref, da_ref, dpre_sc):
        lane = lax.broadcasted_iota(jnp.int32, (CUM_BLK, 128), 1)
        tri = _tri(CUM_BLK, True)
        carry = jnp.zeros((1, 128), F32)
        for blk in reversed(range(nb)):
            db = d_ref[blk * CUM_BLK:(blk + 1) * CUM_BLK, :]
            cb = _dot(tri, db, NN, HI) + carry
            carry = cb[0:1]
            dpre_sc[blk * CUM_BLK:(blk + 1) * CUM_BLK, :] = jnp.where(lane < 8, cb, db)
        _, vjp = jax.vjp(_prep_point, s_ref[...], b_ref[...], a_ref[...])
        ds, dbias, dalog = vjp(dpre_sc[...])
        ds_ref[...] = ds.astype(ds_ref.dtype)
        db_ref[...] = dbias
        da_ref[...] = dalog

    return pl.pallas_call(
        body, grid=(1,),
        in_specs=[pl.BlockSpec((t, 128), lambda i: (0, C_SMALL // 128)), pl.BlockSpec((1, 128), lambda i: (0, 0)),
                  pl.BlockSpec((1, 128), lambda i: (0, 0)), pl.BlockSpec((t, 128), lambda i: (0, 0))],
        out_specs=[pl.BlockSpec((t, 128), lambda i: (0, 0)), pl.BlockSpec((1, 128), lambda i: (0, 0)),
                   pl.BlockSpec((1, 128), lambda i: (0, 0))],
        out_shape=[jax.ShapeDtypeStruct((t, 128), BF16), jax.ShapeDtypeStruct((1, 128), F32), jax.ShapeDtypeStruct((1, 128), F32)],
        scratch_shapes=[pltpu.VMEM((t, 128), F32)], name=name,
    )(proj, bias_row, alog_row, daux)


ATT_TQ = 256
FOX_SCALE = FOX_DH ** -0.5


def _qnorm(q, g):
    return _rms(q, g) * FOX_SCALE


def _att_scores(qn_blk, kn, cfc_blk, cfr, qi, tq, kend):
    s = _dot(qn_blk.astype(BF16), kn[:kend].astype(BF16), NT) + cfc_blk - cfr[:, :kend]
    row = lax.broadcasted_iota(jnp.int32, (tq, kend), 0) + qi * tq
    col = lax.broadcasted_iota(jnp.int32, (tq, kend), 1)
    return s, row >= col


ATT_PAIR = 128 // FOX_DH


def _att_specs(t):
    pair = lambda off: pl.BlockSpec((t, 128), functools.partial(lambda i, o: (0, o + i), o=off // 128))
    gain = pl.BlockSpec((1, FOX_DH), lambda i: (0, 0))
    hd = lambda i: (i, 0, 0)
    col, row = pl.BlockSpec((ATT_PAIR, t, 1), hd), pl.BlockSpec((ATT_PAIR, 1, t), hd)
    return [pair(C_FQ), pair(C_FK), pair(C_FV), gain, gain, col, row], pl.BlockSpec((t, 128), lambda i: (0, i)), col, row


def _att_fwd(proj, qg, kg, cfc, cfr, name):
    t = proj.shape[0]
    tq = min(ATT_TQ, t)
    in_specs, pair_out, col, _ = _att_specs(t)

    def body(q_ref, k_ref, v_ref, qg_ref, kg_ref, cfc_ref, cfr_ref, o_ref, lse_ref):
        for e in range(ATT_PAIR):
            lanes = slice(e * FOX_DH, (e + 1) * FOX_DH)
            qn = _qnorm(q_ref[:, lanes], qg_ref[...])
            kn = _rms(k_ref[:, lanes], kg_ref[...])
            v = v_ref[:, lanes].astype(BF16)
            cfr = cfr_ref[e]
            for qi in range(t // tq):
                kend = (qi + 1) * tq
                rows = slice(qi * tq, kend)
                s, mask = _att_scores(qn[rows], kn, cfc_ref[e, rows, :], cfr, qi, tq, kend)
                s = jnp.where(mask, s, NEG)
                m = jnp.max(s, axis=1, keepdims=True)
                p = jnp.exp(s - m)
                l = jnp.sum(p, axis=1, keepdims=True)
                o_ref[rows, lanes] = _dot(p.astype(BF16), v[:kend], NN) / l
                lse_ref[e, rows, :] = m + jnp.log(l)

    return pl.pallas_call(
        body, grid=(FOX_HEADS // ATT_PAIR,), in_specs=in_specs, out_specs=[pair_out, col],
        out_shape=[jax.ShapeDtypeStruct((t, BW), F32), jax.ShapeDtypeStruct((FOX_HEADS, t, 1), F32)], name=name,
    )(proj, proj, proj, qg, kg, cfc, cfr)


def _att_bwd(proj, qg, kg, cfc, cfr, lse, o, do, name):
    t = proj.shape[0]
    tq = min(ATT_TQ, t)
    in_specs, pair_out, col, row = _att_specs(t)

    def body(q_ref, k_ref, v_ref, qg_ref, kg_ref, cfc_ref, cfr_ref, lse_ref, o_ref, do_ref,
             dq_ref, dk_ref, dv_ref, dcfc_ref, dcfr_ref, dqg_ref, dkg_ref, dqn_sc, dkn_sc, dv_sc, dcfr_sc):
        for e in range(ATT_PAIR):
            lanes = slice(e * FOX_DH, (e + 1) * FOX_DH)
            qn, vjp_q = jax.vjp(_qnorm, q_ref[:, lanes], qg_ref[...])
            kn, vjp_k = jax.vjp(_rms, k_ref[:, lanes], kg_ref[...])
            v = v_ref[:, lanes].astype(BF16)
            cfr = cfr_ref[e]
            do_e = do_ref[:, lanes]
            delta = jnp.sum(do_e * o_ref[:, lanes], axis=1, keepdims=True)
            dkn_sc[...] = jnp.zeros_like(dkn_sc)
            dv_sc[...] = jnp.zeros_like(dv_sc)
            dcfr_sc[...] = jnp.zeros_like(dcfr_sc)
            for qi in range(t // tq):
                kend = (qi + 1) * tq
                rows = slice(qi * tq, kend)
                s, mask = _att_scores(qn[rows], kn, cfc_ref[e, rows, :], cfr, qi, tq, kend)
                p = jnp.where(mask, jnp.exp(jnp.where(mask, s, NEG) - lse_ref[e, rows, :]), 0.0)
                do_b = do_e[rows].astype(BF16)
                dv_sc[0:kend, :] += _dot(p.astype(BF16), do_b, TN)
                dp = _dot(do_b, v[:kend], NT)
                ds = p * (dp - delta[rows])
                ds_b = ds.astype(BF16)
                dqn_sc[rows, :] = _dot(ds_b, kn[:kend].astype(BF16), NN)
                dkn_sc[0:kend, :] += _dot(ds_b, qn[rows].astype(BF16), TN)
                dcfc_ref[e, rows, :] = jnp.sum(ds, axis=1, keepdims=True)
                dcfr_sc[:, 0:kend] -= jnp.sum(ds, axis=0, keepdims=True)
            dq, dqg = vjp_q(dqn_sc[...])
            dk, dkg = vjp_k(dkn_sc[...])
            dq_ref[:, lanes] = dq.astype(dq_ref.dtype)
            dk_ref[:, lanes] = dk.astype(dk_ref.dtype)
            dv_ref[:, lanes] = dv_sc[...].astype(dv_ref.dtype)
            dcfr_ref[e] = dcfr_sc[...]
            dqg_ref[e] = dqg
            dkg_ref[e] = dkg

    gsp = pl.BlockSpec((ATT_PAIR, 1, FOX_DH), lambda i: (i, 0, 0))
    return pl.pallas_call(
        body, grid=(FOX_HEADS // ATT_PAIR,),
        in_specs=in_specs + [col, pair_out, pair_out],
        out_specs=[pair_out] * 3 + [col, row, gsp, gsp],
        out_shape=[jax.ShapeDtypeStruct((t, BW), BF16)] * 3
        + [jax.ShapeDtypeStruct((FOX_HEADS, t, 1), F32), jax.ShapeDtypeStruct((FOX_HEADS, 1, t), F32)]
        + [jax.ShapeDtypeStruct((FOX_HEADS, 1, FOX_DH), F32)] * 2,
        scratch_shapes=[pltpu.VMEM((t, FOX_DH), F32)] * 3 + [pltpu.VMEM((1, t), F32)], name=name,
    )(proj, proj, proj, qg, kg, cfc, cfr, lse, o, do)


DN_SCALE = DN_DH ** -0.5


DN_BATCH = 8


@functools.partial(jax.custom_vjp, nondiff_argnums=(2, 3))
def _mm3(a, b, dims, batch=False):
    return _mm3_passes(a, b, dims, batch)


def _mm3_fwd(a, b, dims, batch):
    return _mm3_passes(a, b, dims, batch), (a, b)


def _mm3_bwd(dims, batch, res, dc):
    a, b = res
    if dims == NN:
        return _mm3_passes(dc, b, NT, batch), _mm3_passes(a, dc, TN, batch)
    if dims == NT:
        return _mm3_passes(dc, b, NN, batch), _mm3_passes(dc, a, TN, batch)
    return _mm3_passes(b, dc, NT, batch), _mm3_passes(a, dc, NN, batch)


_mm3.defvjp(_mm3_fwd, _mm3_bwd)


def _mm3_passes(a, b, dims, batch):
    if batch:
        dn = (((dims[0][0] + 1,), (dims[1][0] + 1,)), ((0,), (0,)))
        dot = lambda p, q: lax.dot_general(p, q, dn, preferred_element_type=F32)
    else:
        dot = lambda p, q: _dot(p, q, dims)
    ah, bh = a.astype(BF16), b.astype(BF16)
    al, bl = (a - ah.astype(F32)).astype(BF16), (b - bh.astype(F32)).astype(BF16)
    return dot(ah, bh) + (dot(ah, bl) + dot(al, bh))


def _dn_local(qc, kc, vc, g, beta):
    nb, c, _ = qc.shape
    ii = lax.broadcasted_iota(jnp.int32, (c, c), 0)
    jj = lax.broadcasted_iota(jnp.int32, (c, c), 1)
    incl, strict = ii >= jj, ii > jj
    lower = jnp.broadcast_to(incl.astype(F32), (nb, c, c))
    eye = (ii == jj).astype(F32)
    mm = functools.partial(_mm3, batch=True)
    dm = mm(lower, jnp.where(strict, g, 0.0), NN)
    decay = jnp.where(incl, jnp.exp(jnp.where(incl, dm, 0.0)), 0.0)
    gcum = mm(lower, g * jnp.ones((1, 1, DN_DH), F32), NN)
    eg = jnp.exp(gcum)
    glast = gcum[:, c - 1:c]
    kb = kc * beta
    n1 = jnp.where(strict, mm(kb, kc, NT) * decay, 0.0)
    inv = eye - n1
    pw = n1
    for _ in range(5):
        pw = mm(pw, pw, NN)
        inv = inv + mm(pw, inv, NN)
    sol = mm(inv, jnp.concatenate([vc * beta, kb * eg], axis=2), NN)
    qk = jnp.where(incl, mm(qc, kc, NT) * decay, 0.0)
    return sol[:, :, :DN_DH], sol[:, :, DN_DH:], qk, qc * eg, kc * jnp.exp(glast - gcum), jnp.exp(glast)


@functools.partial(jax.custom_vjp, nondiff_argnums=(2,))
def _mm1(a, b, dims):
    return _dot(a.astype(BF16), b.astype(BF16), dims)


def _mm1_fwd(a, b, dims):
    return _mm1(a, b, dims), (a, b)


def _mm1_bwd(dims, res, dc):
    a, b = res
    if dims == NN:
        return _mm1(dc, b, NT), _mm1(a, dc, TN)
    if dims == NT:
        return _mm1(dc, b, NN), _mm1(dc, a, TN)
    return _mm1(b, dc, NT), _mm1(a, dc, NN)


_mm1.defvjp(_mm1_fwd, _mm1_bwd)


def _dn_state(u, kcum, qk, qdec, kdec, egl, state):
    v_new = u - _mm1(kcum, state, NN)
    out = _mm1(qdec, state, NN) + _mm1(qk, v_new, NN)
    return out, state * egl + _mm1(kdec, v_new, TN)


def _dn_pre_q(c):
    return _l2(_silu(c)) * DN_SCALE


def _dn_pre_k(c):
    return _l2(_silu(c))


def _dn_post(o, z, ng):
    return _rms(o, ng) * _silu(z)


def _dn_specs(t):
    cblk = lambda o: pl.BlockSpec((t, DN_DH), functools.partial(lambda h, o: (0, o + h), o=o // DN_DH))
    wblk = lambda o: pl.BlockSpec((4, DN_DH), functools.partial(lambda h, o: (0, o + h), o=o // DN_DH))
    proj_specs = [cblk(C_DQ), cblk(C_DK), cblk(C_DV), cblk(C_DZ)]
    w_specs = [wblk(0), wblk(BW), wblk(2 * BW)]
    gb_spec = pl.BlockSpec((1, t, 2), lambda h: (h, 0, 0))
    return proj_specs, w_specs, gb_spec


def _chunk_rows(n, count=1):
    return pl.ds(pl.multiple_of(n * DN_CHUNK, DN_CHUNK), count * DN_CHUNK)


def _egl_rows(n, count=1):
    return pl.ds(pl.multiple_of(n * 8, 8), count * 8)


def _dn_local_inputs(n, qn_sc, kn_sc, vv_sc, gb_ref):
    r = _chunk_rows(n, DN_BATCH)
    split = lambda v: v.reshape(DN_BATCH, DN_CHUNK, v.shape[-1])
    gbv = split(gb_ref[0, r, :])
    return split(qn_sc[r, :]), split(kn_sc[r, :]), split(vv_sc[r, :]), gbv[:, :, 0:1], gbv[:, :, 1:2]


def _dn_local_phase(nc, qn_sc, kn_sc, vv_sc, gb_ref, loc):
    def step(i, carry):
        n = i * DN_BATCH
        vals = _dn_local(*_dn_local_inputs(n, qn_sc, kn_sc, vv_sc, gb_ref))
        for sc, val in zip(loc[:5], vals[:5]):
            sc[_chunk_rows(n, DN_BATCH), :] = val.reshape(DN_BATCH * DN_CHUNK, val.shape[-1])
        loc[5][_egl_rows(n, DN_BATCH), :] = jnp.broadcast_to(vals[5], (DN_BATCH, 8, DN_DH)).reshape(DN_BATCH * 8, DN_DH)
        return carry

    lax.fori_loop(0, nc // DN_BATCH, step, 0)


def _dn_loc_scratch(t, nc):
    big = pltpu.VMEM((t, DN_DH), F32)
    return [big, big, pltpu.VMEM((t, DN_CHUNK), F32), big, big, pltpu.VMEM((nc * 8, DN_DH), F32)]


def _dn_loc_io(t, nc, **spec_args):
    head_cols = pl.BlockSpec((t, DN_DH), lambda h: (0, h), **spec_args)
    per_head = lambda rows, cols: pl.BlockSpec((1, rows, cols), lambda h: (h, 0, 0), **spec_args)
    wide = jax.ShapeDtypeStruct((t, BW), F32)
    shapes = [wide, wide, jax.ShapeDtypeStruct((DN_HEADS, t, DN_CHUNK), F32), wide, wide,
              jax.ShapeDtypeStruct((DN_HEADS, nc * 8, DN_DH), F32)]
    return shapes, [head_cols, head_cols, per_head(t, DN_CHUNK), head_cols, head_cols, per_head(nc * 8, DN_DH)]


def _dn_loc_refs(refs):
    return refs[0], refs[1], refs[2].at[0], refs[3], refs[4], refs[5].at[0]


def _dn_fwd(proj, conv_w, gb, ng, name):
    t = proj.shape[0]
    nc = t // DN_CHUNK
    assert nc % DN_BATCH == 0
    proj_specs, w_specs, gb_spec = _dn_specs(t)
    loc_shapes, loc_specs = _dn_loc_io(t, nc)

    def body(q_ref, k_ref, v_ref, z_ref, wq_ref, wk_ref, wv_ref, gb_ref, ng_ref, y_ref, o_ref, st_ref, *rest):
        loc = _dn_loc_refs(rest[:6])
        qn_sc, kn_sc, vv_sc = rest[6:]
        qn_sc[...] = _dn_pre_q(_conv(q_ref[...], wq_ref[...]))
        kn_sc[...] = _dn_pre_k(_conv(k_ref[...], wk_ref[...]))
        vv_sc[...] = _silu(_conv(v_ref[...], wv_ref[...]))
        _dn_local_phase(nc, qn_sc, kn_sc, vv_sc, gb_ref, loc)
        u_sc, kcum_sc, qk_sc, qdec_sc, kdec_sc, egl_sc = loc

        def chunk(n, state):
            r = _chunk_rows(n)
            egl = egl_sc[_egl_rows(n), :][0:1]
            out, new_state = _dn_state(u_sc[r, :], kcum_sc[r, :], qk_sc[r, :], qdec_sc[r, :], kdec_sc[r, :], egl, state)
            st_ref[0, n] = state
            o_ref[r, :] = out
            return new_state

        lax.fori_loop(0, nc, chunk, jnp.zeros((DN_DH, DN_DH), F32))
        y_ref[...] = _dn_post(o_ref[...], z_ref[...], ng_ref[...])

    hblk = pl.BlockSpec((t, DN_DH), lambda h: (0, h))
    return pl.pallas_call(
        body, grid=(DN_HEADS,),
        in_specs=proj_specs + w_specs + [gb_spec, pl.BlockSpec((1, DN_DH), lambda h: (0, 0))],
        out_specs=[hblk, hblk, pl.BlockSpec((1, nc, DN_DH, DN_DH), lambda h: (h, 0, 0, 0))] + loc_specs,
        out_shape=[jax.ShapeDtypeStruct((t, BW), F32), jax.ShapeDtypeStruct((t, BW), F32),
                   jax.ShapeDtypeStruct((DN_HEADS, nc, DN_DH, DN_DH), F32)] + loc_shapes,
        scratch_shapes=[pltpu.VMEM((t, DN_DH), F32)] * 3, name=name,
    )(proj, proj, proj, proj, conv_w, conv_w, conv_w, gb, ng)


def _dn_bwd(proj, conv_w, gb, ng, o, states, loc_saved, dy, name):
    t = proj.shape[0]
    nc = t // DN_CHUNK
    proj_specs, w_specs, gb_spec = _dn_specs(t)
    _, loc_specs = _dn_loc_io(t, nc, pipeline_mode=pl.Buffered(1))

    def body(q_ref, k_ref, v_ref, z_ref, wq_ref, wk_ref, wv_ref, gb_ref, ng_ref, o_ref, st_ref, dy_ref,
             u_ref, kcum_ref, qk_ref, qdec_ref, kdec_ref, egl_ref,
             dq_ref, dk_ref, dv_ref, dz_ref, dwq_ref, dwk_ref, dwv_ref, dgb_ref, dng_ref,
             qn_sc, kn_sc, vv_sc, do_sc, *dloc):
        loc = _dn_loc_refs((u_ref, kcum_ref, qk_ref, qdec_ref, kdec_ref, egl_ref))
        qn_sc[...] = _dn_pre_q(_conv(q_ref[...], wq_ref[...]))
        kn_sc[...] = _dn_pre_k(_conv(k_ref[...], wk_ref[...]))
        vv_sc[...] = _silu(_conv(v_ref[...], wv_ref[...]))
        _, vjp_y = jax.vjp(_dn_post, o_ref[...], z_ref[...], ng_ref[...])
        do, dz, dng = vjp_y(dy_ref[...])
        do_sc[...] = do
        dz_ref[...] = dz.astype(dz_ref.dtype)
        dng_ref[0] = dng
        u_sc, kcum_sc, qk_sc, qdec_sc, kdec_sc, egl_sc = loc

        def state_bwd(i, dstate):
            n = nc - 1 - i
            r = _chunk_rows(n)
            r8 = _egl_rows(n)
            _, vjp = jax.vjp(_dn_state, u_sc[r, :], kcum_sc[r, :], qk_sc[r, :], qdec_sc[r, :], kdec_sc[r, :],
                             egl_sc[r8, :][0:1], st_ref[0, n])
            du, dkcum, dqk, dqdec, dkdec, degl, dprev = vjp((do_sc[r, :], dstate))
            for d_sc, val in zip(dloc[:5], (du, dkcum, dqk, dqdec, dkdec)):
                d_sc[r, :] = val
            dloc[5][r8, :] = jnp.broadcast_to(degl, (8, DN_DH))
            return dprev

        lax.fori_loop(0, nc, state_bwd, jnp.zeros((DN_DH, DN_DH), F32))

        def local_bwd(i, carry):
            n = i * DN_BATCH
            r = _chunk_rows(n, DN_BATCH)
            _, vjp = jax.vjp(_dn_local, *_dn_local_inputs(n, qn_sc, kn_sc, vv_sc, gb_ref))
            cts = tuple(d_sc[r, :].reshape(DN_BATCH, DN_CHUNK, d_sc.shape[-1]) for d_sc in dloc[:5])
            cts += (dloc[5][_egl_rows(n, DN_BATCH), :].reshape(DN_BATCH, 8, DN_DH)[:, 0:1],)
            dqc, dkc, dvc, dg, dbeta = vjp(cts)
            for d_sc, val in zip((dloc[0], dloc[1], dloc[3]), (dqc, dkc, dvc)):
                d_sc[r, :] = val.reshape(DN_BATCH * DN_CHUNK, DN_DH)
            dgb_ref[0, r, :] = jnp.concatenate([dg, dbeta], axis=2).reshape(DN_BATCH * DN_CHUNK, 2)
            return carry

        lax.fori_loop(0, nc // DN_BATCH, local_bwd, 0)
        for x_ref, w_ref, pre, d_sc, dx_ref, dw_ref in ((q_ref, wq_ref, _dn_pre_q, dloc[0], dq_ref, dwq_ref),
                                                       (k_ref, wk_ref, _dn_pre_k, dloc[1], dk_ref, dwk_ref),
                                                       (v_ref, wv_ref, _silu, dloc[3], dv_ref, dwv_ref)):
            _, vjp = jax.vjp(pre, _conv(x_ref[...], w_ref[...]))
            (dc,) = vjp(d_sc[...])
            dx, dws = _conv_bwd(x_ref[...], w_ref[...], dc)
            dx_ref[...] = dx.astype(dx_ref.dtype)
            for j, dw in enumerate(dws):
                dw_ref[j:j + 1, :] = dw

    hblk = pl.BlockSpec((t, DN_DH), lambda h: (0, h))
    wout = pl.BlockSpec((4, DN_DH), lambda h: (0, h))
    return pl.pallas_call(
        body, grid=(DN_HEADS,),
        in_specs=proj_specs + w_specs + [gb_spec, pl.BlockSpec((1, DN_DH), lambda h: (0, 0)), hblk,
                                         pl.BlockSpec((1, nc, DN_DH, DN_DH), lambda h: (h, 0, 0, 0)), hblk] + loc_specs,
        out_specs=[hblk] * 4 + [wout] * 3 + [gb_spec, pl.BlockSpec((1, 1, DN_DH), lambda h: (h, 0, 0))],
        out_shape=[jax.ShapeDtypeStruct((t, BW), BF16)] * 4 + [jax.ShapeDtypeStruct((4, BW), F32)] * 3
        + [jax.ShapeDtypeStruct((DN_HEADS, t, 2), F32), jax.ShapeDtypeStruct((DN_HEADS, 1, DN_DH), F32)],
        scratch_shapes=[pltpu.VMEM((t, DN_DH), F32)] * 4 + _dn_loc_scratch(t, nc), name=name,
    )(proj, proj, proj, proj, conv_w, conv_w, conv_w, gb, ng, o, states, dy, *loc_saved)


MERGE_TM, MERGE_TN = 512, 512


def _merge_specs(t):
    tm, tn = min(MERGE_TM, t), MERGE_TN
    y_spec = pl.BlockSpec((tm, BW), lambda i, j: (i, 0))
    w_spec = pl.BlockSpec((3, BW, tn), lambda i, j: (0, 0, j))
    gate_specs = [pl.BlockSpec((tm, tn), functools.partial(lambda i, j, o: (i, o + j), o=(C_GATE + n * D_MODEL) // tn))
                  for n in range(3)]
    return tm, tn, [y_spec] * 3 + [w_spec] + gate_specs


def _merge_fwd(ys, wb, proj, name):
    t = proj.shape[0]
    tm, tn, in_specs = _merge_specs(t)

    def body(y0, y1, y2, w_ref, g0, g1, g2, o_ref):
        acc = jnp.zeros((tm, tn), F32)
        for n, (y, g) in enumerate(((y0, g0), (y1, g1), (y2, g2))):
            acc = acc + _dot(y[...].astype(BF16), w_ref[n], NN) * _sigmoid(g[...])
        o_ref[...] = acc.astype(o_ref.dtype)

    return pl.pallas_call(
        body, grid=(t // tm, D_MODEL // tn), in_specs=in_specs,
        out_specs=pl.BlockSpec((tm, tn), lambda i, j: (i, j)),
        out_shape=jax.ShapeDtypeStruct((t, D_MODEL), BF16), name=name,
    )(*ys, wb, proj, proj, proj)


def _merge_bwd(ys, wb, proj, dmerged, name):
    t = proj.shape[0]
    tm, tn, in_specs = _merge_specs(t)

    def body(y0, y1, y2, w_ref, g0, g1, g2, dm_ref, dg_ref, dt_ref):
        dm = dm_ref[...]
        for n, (y, g) in enumerate(((y0, g0), (y1, g1), (y2, g2))):
            tn_ = _dot(y[...].astype(BF16), w_ref[n], NN)
            sg = _sigmoid(g[...])
            dg_ref[n] = (dm * tn_ * sg * (1.0 - sg)).astype(dg_ref.dtype)
            dt_ref[n] = (dm * sg).astype(dt_ref.dtype)

    o3 = pl.BlockSpec((3, tm, tn), lambda i, j: (0, i, j))
    return pl.pallas_call(
        body, grid=(t // tm, D_MODEL // tn), in_specs=in_specs + [pl.BlockSpec((tm, tn), lambda i, j: (i, j))],
        out_specs=[o3, o3], out_shape=[jax.ShapeDtypeStruct((3, t, D_MODEL), BF16)] * 2, name=name,
    )(*ys, wb, proj, proj, proj, dmerged)


def _prep_rows(sp):
    z4, z112 = jnp.zeros((1, 4), F32), jnp.zeros((1, 112), F32)
    bias_row = jnp.concatenate([sp['b_fox_f'], z4, sp['dn_dt_bias'], z112], axis=1)
    alog_row = jnp.concatenate([jnp.zeros((1, 12), F32), sp['dn_a_log'], z112], axis=1)
    return bias_row, alog_row


def _sc_fwd_tile(sb, sc, sv, w):
    return (sb * _conv(sc * sv, w),)


def _ffn_act(ug, uv):
    return _silu(ug) * uv


W_IN_COLS = [(C_FQ, 0, 1536), (C_SB, 1544, 1536), (C_DQ, 3080, 1536), (C_DZ, 4624, 512), (C_GATE, 5136, 3072),
             (C_SMALL, 1536, 8), (C_SMALL + 8, 4616, 8)]
W_IN_SHARD = IN_ORIG // N_DEV


def _w_in_segments():
    out = []
    for d, o, w in W_IN_COLS:
        end = o + w
        while o < end:
            k = o // W_IN_SHARD
            n = min(end, (k + 1) * W_IN_SHARD) - o
            out.append((d, k, o - k * W_IN_SHARD, n))
            o, d = o + n, d + n
    return out


def _w_in_assemble(g, name):
    tm = 256

    def body(g_ref, o_ref):
        for d, k, s, n in _w_in_segments():
            o_ref[:, d:d + n] = g_ref[k, :, s:s + n]
        o_ref[:, IN_ORIG:IN_P] = jnp.zeros((tm, IN_P - IN_ORIG), o_ref.dtype)

    return pl.pallas_call(
        body, grid=(D_MODEL // tm,),
        in_specs=[pl.BlockSpec((N_DEV, tm, W_IN_SHARD), lambda i: (0, i, 0))],
        out_specs=pl.BlockSpec((tm, IN_P), lambda i: (i, 0)),
        out_shape=jax.ShapeDtypeStruct((D_MODEL, IN_P), g.dtype), name=name,
    )(g)


W_IN_SPLIT_TM = 128


def _w_in_split_by_device(g, name):
    tm = W_IN_SPLIT_TM

    def body(g_ref, own_ref, send_ref):
        me = 4 * lax.axis_index("x") + 2 * lax.axis_index("y") + lax.axis_index("c")
        for d, k, s, n in _w_in_segments():
            val = g_ref[:, d:d + n]
            send_ref[k, :, s:s + n] = val.astype(send_ref.dtype)

            @pl.when(me == k)
            def _():
                own_ref[:, s:s + n] = val

    return pl.pallas_call(
        body, grid=(D_MODEL // tm,), in_specs=[pl.BlockSpec((tm, IN_P), lambda i: (i, 0))],
        out_specs=[pl.BlockSpec((tm, W_IN_SHARD), lambda i: (i, 0)), pl.BlockSpec((N_DEV, tm, W_IN_SHARD), lambda i: (0, i, 0))],
        out_shape=[jax.ShapeDtypeStruct((D_MODEL, W_IN_SHARD), F32), jax.ShapeDtypeStruct((N_DEV, D_MODEL, W_IN_SHARD), BF16)],
        name=name,
    )(g)


def _layer_fwd(x, p_i, w, sp, tag, late_weights=None):
    t = x.shape[0]
    sv = {'x': x}
    (hn,) = _ew(lambda a, g: (_rms(a, g),), [x], [sp['g_mix']], [(D_MODEL, BF16)], name=f'rms_mix_{tag}')
    proj = _mm(hn, w['w_in'], 'nn', [F32], name=f'in_proj_{tag}')
    bias_row, alog_row = _prep_rows(sp)
    aux = _prep_fwd(proj, bias_row, alog_row, f'prep_{tag}')
    cf = aux[:, :FOX_HEADS].T
    cfc, cfr = cf[:, :, None], cf[:, None, :]
    y_fox, lse = _att_fwd(proj, sp['fox_q_gain'], sp['fox_k_gain'], cfc, cfr, f'fox_fwd_{tag}')
    (y_sc,) = _cb(_sc_fwd_tile, [(proj, C_SB), (proj, C_SC), (proj, C_SV)], [(w['sc_conv_w'], 0)], [F32], [],
                  tc=256, nblk=2, name=f'sc_fwd_{tag}')
    gb = jnp.stack([aux[:, 12:16].T, aux[:, 8:12].T], axis=-1)
    y_dn, o_dn, states, *dn_loc = _dn_fwd(proj, w['dn_conv_w'], gb, sp['dn_norm_gain'], f'dn_fwd_{tag}')
    ys = (y_fox, y_sc, y_dn)
    if late_weights is not None:
        w = {**w, **late_weights(y_dn)}
    merged = _merge_fwd(ys, w['w_branch'], proj, f'merge_fwd_{tag}')
    x1 = _mm(merged, w['w_o'], 'nn', [F32], epi=lambda acc, r: (acc + r,), epi_args=(x,), name=f'o_proj_{tag}')
    (hf,) = _ew(lambda a, g: (_rms(a, g),), [x1], [sp['g_ffn']], [(D_MODEL, BF16)], name=f'rms_ffn_{tag}')
    up = _mm(hf, w['w_up'], 'nt', [F32], name=f'up_proj_{tag}')
    (act,) = _cb(lambda ug, uv, wg, wv: (_ffn_act(_conv(ug, wg), _conv(uv, wv)),), [(up, 0), (up, D_FF)],
                 [(w['ffn_conv_w'], 0), (w['ffn_conv_w'], D_FF)], [BF16], [], tc=256, nblk=D_FF // 256, name=f'ffn_act_{tag}')
    x2 = _mm(act, w['w_down'], 'nn', [F32], epi=lambda acc, r: (acc + r,), epi_args=(x1,), name=f'down_proj_{tag}')
    (hp,) = _ew(lambda a, g: (_rms(a, g),), [x2], [sp['g_ple']], [(D_MODEL, BF16)], name=f'rms_ple_{tag}')
    gp = _mm(hp, w['w_ple_gate'], 'nn', [F32], name=f'ple_gate_{tag}')
    x3 = _mm(p_i, w['w_ple'], 'nn', [F32], epi=lambda acc, g, r: (r + _sigmoid(g) * acc,), epi_args=(gp, x2), name=f'ple_{tag}')
    sv.update(hn=hn, proj=proj, aux=aux, cfc=cfc, cfr=cfr, lse=lse, ys=ys, gb=gb, o_dn=o_dn,
              states=states, merged=merged, x1=x1, hf=hf, up=up, act=act, x2=x2, hp=hp, gp=gp, p=p_i,
              bias_row=bias_row, alog_row=alog_row, w=w, dn_loc=dn_loc)
    return x3, sv


def _rms_bwd(x, g, dh, dres, name):
    def fn(xv, dhv, dr, gv):
        _, vjp = jax.vjp(_rms, xv, gv)
        dx, dg = vjp(dhv)
        return dr + dx, dg
    return _ew(fn, [x, dh, dres], [g], [(D_MODEL, F32)], [(1, D_MODEL)], name=name)


def _layer_bwd(dx3, sv, sp, tag, early_grads=None, last_grad=None):
    t = dx3.shape[0]
    w = sv['w']
    g = {}
    def ple_epi(acc, gpv, d):
        s = _sigmoid(gpv)
        return d * acc * s * (1.0 - s), d * s
    dgp, de = _mm(sv['p'], w['w_ple'], 'nn', [BF16, BF16], epi=ple_epi, epi_args=(sv['gp'], dx3), name=f'ple_bwd_{tag}')
    g['w_ple'] = _mm(sv['p'], de, 'tn', [F32], name=f'd_w_ple_{tag}')
    g['w_ple_gate'] = _mm(sv['hp'], dgp, 'tn', [F32], name=f'd_w_ple_gate_{tag}')
    dhp = _mm(dgp, w['w_ple_gate'], 'nt', [F32], name=f'd_hp_{tag}')
    dx2, g['g_ple'] = _rms_bwd(sv['x2'], sp['g_ple'], dhp, dx3, f'rms_ple_bwd_{tag}')
    dact = _mm(dx2, w['w_down'], 'nt', [F32], name=f'd_act_{tag}')
    g['w_down'] = _mm(sv['act'], dx2, 'tn', [F32], name=f'd_w_down_{tag}')

    def ffn_bwd_tile(ug, uv, da, wg, wv):
        cg, cv = _conv(ug, wg), _conv(uv, wv)
        _, vjp = jax.vjp(_ffn_act, cg, cv)
        dcg, dcv = vjp(da)
        dug, dwg = _conv_bwd(ug, wg, dcg)
        duv, dwv = _conv_bwd(uv, wv, dcv)
        return dug, duv, jnp.concatenate(dwg, axis=0), jnp.concatenate(dwv, axis=0)
    dupg, dupv, dwg, dwv = _cb(ffn_bwd_tile, [(sv['up'], 0), (sv['up'], D_FF), (dact, 0)],
                               [(w['ffn_conv_w'], 0), (w['ffn_conv_w'], D_FF)], [BF16, BF16], [3, 3], tc=256,
                               nblk=D_FF // 256, name=f'ffn_act_bwd_{tag}')
    dup = jnp.concatenate([dupg, dupv], axis=1)
    g['ffn_conv_w'] = jnp.concatenate([dwg, dwv], axis=1)
    g['w_up'] = _mm(dup, sv['hf'], 'tn', [F32], name=f'd_w_up_{tag}')
    dhf = _mm(dup, w['w_up'], 'nn', [F32], name=f'd_hf_{tag}')
    dx1, g['g_ffn'] = _rms_bwd(sv['x1'], sp['g_ffn'], dhf, dx2, f'rms_ffn_bwd_{tag}')
    dmerged = _mm(dx1, w['w_o'], 'nt', [F32], name=f'd_merged_{tag}')
    g['w_o'] = _mm(sv['merged'], dx1, 'tn', [F32], name=f'd_w_o_{tag}')
    dgate, dtn = _merge_bwd(sv['ys'], w['w_branch'], sv['proj'], dmerged, f'merge_bwd_{tag}')
    dys, dwb = [], []
    for n in range(3):
        dys.append(_mm(dtn[n], w['w_branch'][n], 'nt', [F32], name=f'd_y{n}_{tag}'))
        dwb.append(_mm(sv['ys'][n], dtn[n], 'tn', [F32], name=f'd_w_branch{n}_{tag}'))
    g['w_branch'] = jnp.stack(dwb)
    if early_grads is not None:
        sp = early_grads(g, sp)
    ddq, ddk, ddv, ddz, dwq, dwk, dwv_, dgb, dng = _dn_bwd(sv['proj'], w['dn_conv_w'], sv['gb'], sp['dn_norm_gain'],
                                                           sv['o_dn'], sv['states'], sv['dn_loc'], dys[2], f'dn_bwd_{tag}')
    g['dn_conv_w'] = jnp.concatenate([dwq, dwk, dwv_], axis=1)
    g['dn_norm_gain'] = jnp.sum(dng, axis=0)
    def sc_bwd_tile(sb, sc, svv, dy, wv):
        u = sc * svv
        dsb = dy * _conv(u, wv)
        du, dws = _conv_bwd(u, wv, dy * sb)
        return dsb, du * svv, du * sc, jnp.concatenate(dws, axis=0)
    dsb, dsc, dsv, g['sc_conv_w'] = _cb(sc_bwd_tile, [(sv['proj'], C_SB), (sv['proj'], C_SC), (sv['proj'], C_SV), (dys[1], 0)],
                                        [(w['sc_conv_w'], 0)], [BF16, BF16, BF16], [3], tc=256, nblk=2, name=f'sc_bwd_{tag}')
    dfq, dfk, dfv, dcfc, dcfr, dqg, dkg = _att_bwd(sv['proj'], sp['fox_q_gain'], sp['fox_k_gain'], sv['cfc'], sv['cfr'],
                                                   sv['lse'], sv['ys'][0], dys[0], f'fox_bwd_{tag}')
    g['fox_q_gain'] = jnp.sum(dqg, axis=0)
    g['fox_k_gain'] = jnp.sum(dkg, axis=0)
    dcf = (dcfc[:, :, 0] + dcfr[:, 0, :]).T
    daux = jnp.concatenate([dcf, dgb[:, :, 1].T, dgb[:, :, 0].T, jnp.zeros((t, 112), F32)], axis=1)
    dsmall, dbias, dalog = _prep_bwd(sv['proj'], sv['bias_row'], sv['alog_row'], daux, f'prep_bwd_{tag}')
    g['b_fox_f'] = dbias[:, 0:8]
    g['dn_dt_bias'] = dbias[:, 12:16]
    g['dn_a_log'] = dalog[:, 12:16]
    dproj = jnp.concatenate([dfq, dfk, dfv, dsb, dsc, dsv, ddq, ddk, ddv, ddz, dgate[0], dgate[1], dgate[2], dsmall], axis=1)
    g['w_in'] = _mm(sv['hn'], dproj, 'tn', [F32], name=f'd_w_in_{tag}')
    if last_grad is not None:
        sp = last_grad(g, sp)
    dhn = _mm(dproj, w['w_in'], 'nt', [F32], name=f'd_hn_{tag}')
    dx, g['g_mix'] = _rms_bwd(sv['x'], sp['g_mix'], dhn, dx1, f'rms_mix_bwd_{tag}')
    return dx, g


def _loss_bwd(y, target, name):
    inv = 1.0 / y.shape[1]

    def fn(yv, tv):
        err = yv - tv
        return err * inv, jnp.zeros((8, 128), F32) + 0.5 * inv * jnp.sum(err * err)
    return _ew(fn, [y, target], [], [(y.shape[1], F32)], [(8, 128)], name=name)


PACK_W = 1024
FULL_SHAPE = {'w_in': (D_MODEL, IN_ORIG), 'w_branch': (3, BW, D_MODEL), 'w_o': (D_MODEL, D_MODEL), 'w_up': (2 * D_FF, D_MODEL),
              'w_down': (D_FF, D_MODEL), 'w_ple_gate': (D_MODEL, D_MODEL), 'w_ple': (PLE_DIM, D_MODEL),
              'sc_conv_w': (3, BW), 'dn_conv_w': (4, 3 * BW), 'ffn_conv_w': (3, 2 * D_FF)}
SMALL_SHAPE = {'g_mix': D_MODEL, 'b_fox_f': FOX_HEADS, 'fox_q_gain': FOX_DH, 'fox_k_gain': FOX_DH, 'dn_a_log': DN_HEADS,
               'dn_dt_bias': DN_HEADS, 'dn_norm_gain': DN_DH, 'g_ffn': D_MODEL, 'g_ple': D_MODEL}


def _shard_shape(name):
    s = list(FULL_SHAPE[name])
    s[SHARDED[name]] //= N_DEV
    return tuple(s)


def _full_from_gathered(g, name):
    sh, ax = _shard_shape(name), SHARDED[name]
    blocks = jnp.moveaxis(g, 0, ax)
    return blocks.reshape(sh[:ax] + (N_DEV * sh[ax],) + sh[ax + 1:])


def _by_dest(full, name):
    sh, ax = _shard_shape(name), SHARDED[name]
    return jnp.moveaxis(full.reshape(sh[:ax] + (N_DEV, sh[ax]) + sh[ax + 1:]), ax, 0)


def _flat_pack(arrs):
    flat = jnp.concatenate([a.reshape(-1).astype(F32) for a in arrs])
    rows = -(-flat.shape[0] // (8 * PACK_W)) * 8
    return jnp.pad(flat, (0, rows * PACK_W - flat.shape[0])).reshape(rows, PACK_W)


def _flat_unpack(pack, shapes):
    flat, out, off = pack.reshape(-1), [], 0
    for s in shapes:
        n = 1
        for d in s:
            n *= d
        out.append(flat[off:off + n].reshape(s))
        off += n
    return out


SMALL_SHAPES = [(DEPTH, SMALL_SHAPE[n]) for n in SMALL]
SMALL_LOSS_AT = sum(DEPTH * SMALL_SHAPE[n] for n in SMALL)
CONV_SHARD_SHAPES = [(DEPTH,) + _shard_shape(n) for n in CONVW]
CONV_FULL_SHAPES = [(DEPTH,) + FULL_SHAPE[n] for n in CONVW]


def _pick_tm(m, width):
    best = None
    for tm in range(16, m + 1, 16):
        if m % tm == 0 and tm * width * 4 <= (1 << 20):
            best = tm
    return best if best is not None else m


HBM_SPEC = pl.BlockSpec(memory_space=pltpu.HBM)


def _place():
    x, y, c = lax.axis_index("x"), lax.axis_index("y"), lax.axis_index("c")
    return x, y, c, [(1 - x, y), (x, 1 - y), (1 - x, 1 - y)]


def _all_gather(arrs, name):
    n = len(arrs)

    def body(*refs):
        ins, outs = refs[:n], refs[n:2 * n]
        send_sems, recv_sems = refs[2 * n:]
        x, y, c, chips = _place()
        me, sibling = (x, y, c), (x, y, 1 - c)

        def block(a, p):
            return outs[a].at[4 * p[0] + 2 * p[1] + p[2]]

        def copy(a, k, blk, to, src=None):
            return pltpu.make_async_remote_copy(src_ref=block(a, blk) if src is None else src, dst_ref=block(a, blk),
                                                send_sem=send_sems.at[a, k], recv_sem=recv_sems.at[a, k],
                                                device_id=to, device_id_type=MESH)

        first, passed = [], []
        for a in range(n):
            first.append(copy(a, 0, me, sibling, src=ins[a]))
            first += [copy(a, 1 + j, me, (*chip, c), src=ins[a]) for j, chip in enumerate(chips)]
        for cp in first:
            cp.start()
        for j, chip in enumerate(chips):
            for a in range(n):
                copy(a, 1 + j, (*chip, c), me).wait_recv()
                fwd = copy(a, 4 + j, (*chip, c), sibling)
                fwd.start()
                passed.append(fwd)
        for a in range(n):
            copy(a, 0, sibling, me).wait_recv()
            for j, chip in enumerate(chips):
                copy(a, 4 + j, (*chip, 1 - c), me).wait_recv()
        for cp in first + passed:
            cp.wait_send()

    gathered = pl.pallas_call(
        body, in_specs=[HBM_SPEC] * n, out_specs=[HBM_SPEC] * n,
        out_shape=[jax.ShapeDtypeStruct((N_DEV,) + a.shape, a.dtype) for a in arrs],
        scratch_shapes=[pltpu.SemaphoreType.DMA((n, 7)), pltpu.SemaphoreType.DMA((n, 7))], name=name,
    )(*arrs)
    me = 4 * lax.axis_index("x") + 2 * lax.axis_index("y") + lax.axis_index("c")
    return [lax.dynamic_update_index_in_dim(g, a, me, 0) for g, a in zip(gathered, arrs)]


SEM_SPEC = pl.BlockSpec(memory_space=pltpu.SEMAPHORE)
DATAFLOW = pltpu.SideEffectType.DATAFLOW_SIDE_EFFECTING
ALL_PEERS = (1, 2, 3, 4, 5, 6, 7)
NEAR_PEERS = (1, 4, 2, 6)


def _direct_copies(src_refs, zone_refs, send_sems, recv_sems, per_dest, peers):
    x, y, c, _ = _place()
    me = 4 * x + 2 * y + c
    cps = []
    for a, (src, zone) in enumerate(zip(src_refs, zone_refs)):
        for r, bits in enumerate(peers):
            px = 1 - x if bits & 4 else x
            py = 1 - y if bits & 2 else y
            pc = 1 - c if bits & 1 else c
            cps.append(pltpu.make_async_remote_copy(
                src_ref=src.at[4 * px + 2 * py + pc] if per_dest else src, dst_ref=zone.at[me],
                send_sem=send_sems.at[a * len(peers) + r], recv_sem=recv_sems.at[a * len(peers) + r],
                device_id=(px, py, pc), device_id_type=MESH))
    return cps


def _direct_start(srcs, per_dest, after, name, peers=ALL_PEERS):
    n = len(srcs)
    zones = [lax.empty((N_DEV,) + (s.shape[1:] if per_dest else s.shape), s.dtype) for s in srcs]

    def body(*refs):
        src_refs, zone_refs = refs[:n], refs[n:2 * n]
        send_sems, recv_sems = refs[2 * n + 1:2 * n + 3]
        for cp in _direct_copies(src_refs, zone_refs, send_sems, recv_sems, per_dest, peers):
            cp.start()
        refs[-1][...] = jnp.zeros_like(refs[-1])

    sems = pltpu.SemaphoreType.DMA((n * len(peers),))
    res = pl.pallas_call(
        body, name=name, in_specs=[HBM_SPEC] * (2 * n + 1),
        out_shape=[sems, sems] + [pltpu.HBM(s.shape, s.dtype) for s in srcs] + [pltpu.HBM(z.shape, z.dtype) for z in zones]
        + [jax.ShapeDtypeStruct((8, 128), F32)],
        out_specs=[SEM_SPEC, SEM_SPEC] + [HBM_SPEC] * (2 * n) + [pl.BlockSpec(memory_space=pltpu.VMEM)],
        input_output_aliases={i: 2 + i for i in range(2 * n)},
        compiler_params=pltpu.CompilerParams(has_side_effects=DATAFLOW),
    )(*[pltpu.with_memory_space_constraint(s, pltpu.HBM) for s in srcs],
      *[pltpu.with_memory_space_constraint(z, pltpu.HBM) for z in zones], after)
    return (res[0], res[1], list(res[2:2 + n]), list(res[2 + n:2 + 2 * n]), per_dest, peers), res[-1][0:1, 0:1]


def _direct_wait(started, after, name):
    send_sems, recv_sems, srcs, zones, per_dest, peers = started
    n = len(srcs)

    def body(*refs):
        src_refs, zone_refs = refs[:n], refs[n:2 * n]
        for cp in _direct_copies(src_refs, zone_refs, refs[2 * n], refs[2 * n + 1], per_dest, peers):
            cp.wait_send()
            cp.wait_recv()

    res = pl.pallas_call(
        body, name=name, in_specs=[HBM_SPEC] * (2 * n) + [SEM_SPEC, SEM_SPEC, HBM_SPEC],
        out_shape=[pltpu.HBM(s.shape, s.dtype) for s in srcs] + [pltpu.HBM(z.shape, z.dtype) for z in zones],
        out_specs=[HBM_SPEC] * (2 * n), input_output_aliases={i: i for i in range(2 * n)},
        compiler_params=pltpu.CompilerParams(has_side_effects=DATAFLOW),
    )(*srcs, *zones, send_sems, recv_sems, after)
    return list(res[n:])


def _gather_finish(zones, name):
    n = len(zones)

    def body(*refs):
        zone_in, zone_out = refs[:n], refs[n:2 * n]
        send_sems, recv_sems = refs[2 * n:]
        x, y, c, chips = _place()
        cps = []
        for a in range(n):
            for j, (cx, cy) in enumerate(chips):
                blk = 4 * cx + 2 * cy + c
                cp = pltpu.make_async_remote_copy(src_ref=zone_in[a].at[blk], dst_ref=zone_out[a].at[blk],
                                                  send_sem=send_sems.at[a, j], recv_sem=recv_sems.at[a, j],
                                                  device_id=(x, y, 1 - c), device_id_type=MESH)
                cp.start()
                cps.append(cp)
        for cp in cps:
            cp.wait()

    return pl.pallas_call(
        body, in_specs=[HBM_SPEC] * n, out_specs=[HBM_SPEC] * n,
        out_shape=[jax.ShapeDtypeStruct(z.shape, z.dtype) for z in zones], input_output_aliases={a: a for a in range(n)},
        scratch_shapes=[pltpu.SemaphoreType.DMA((n, 3)), pltpu.SemaphoreType.DMA((n, 3))], name=name,
    )(*zones)


def _adamw(w, g, m, v):
    m = ADAM_B1 * m + (1.0 - ADAM_B1) * g
    v = ADAM_B2 * v + (1.0 - ADAM_B2) * jnp.square(g)
    m_hat = m / (1.0 - ADAM_B1 ** ADAM_STEP)
    v_hat = v / (1.0 - ADAM_B2 ** ADAM_STEP)
    delta = -ADAM_LR * (m_hat / (jnp.sqrt(v_hat) + ADAM_EPS) + ADAM_WD * w)
    return delta, m, v


def _adamw_layer(w, m, v, own, parts, layer, prev, name):
    _, rows, c = w.shape
    tm = _pick_tm(rows, c)

    def body(w_ref, m_ref, v_ref, own_ref, parts_ref, *rest):
        g = own_ref[...]
        for j in range(parts_ref.shape[0]):
            g = g + parts_ref[j].astype(F32)
        delta, m2, v2 = _adamw(w_ref[...], g, m_ref[...], v_ref[...])
        g_ref, d_ref, m2_ref, v2_ref = rest[-4:]
        g_ref[...], d_ref[...], m2_ref[...], v2_ref[...] = g, delta, m2, v2

    wspec = pl.BlockSpec((None, tm, c), lambda i: (layer, i, 0))
    prev = list(prev) if prev is not None else []
    return pl.pallas_call(
        body, grid=(rows // tm,),
        in_specs=[wspec] * 3 + [pl.BlockSpec((tm, c), lambda i: (i, 0)), pl.BlockSpec((parts.shape[0], tm, c), lambda i: (0, i, 0))]
        + [HBM_SPEC] * len(prev),
        out_specs=[wspec] * 4, out_shape=[jax.ShapeDtypeStruct(w.shape, F32)] * 4,
        input_output_aliases={5 + k: k for k in range(len(prev))}, name=name,
    )(w, m, v, own, parts, *prev)


W_IN_ADAMW_TM = 128


def _adamw_w_in(w_t, m_t, v_t, grads, name):
    tm = W_IN_ADAMW_TM
    ng = [p.shape[0] for _, p in grads]

    def body(w_ref, m_ref, v_ref, *rest):
        outs = rest[2 * DEPTH:]
        for layer in range(DEPTH):
            own_ref, parts_ref = rest[2 * layer], rest[2 * layer + 1]
            g = own_ref[...]
            for j in range(ng[layer]):
                g = g + parts_ref[j].astype(F32)
            delta, m2, v2 = _adamw(w_ref[:, layer, :].T, g, m_ref[:, layer, :].T, v_ref[:, layer, :].T)
            for o_ref, val in zip(outs, (g, delta, m2, v2)):
                o_ref[:, layer, :] = val.T

    tspec = pl.BlockSpec((W_IN_SHARD, DEPTH, tm), lambda i: (0, 0, i))
    gspecs = []
    for _, parts in grads:
        gspecs += [pl.BlockSpec((tm, W_IN_SHARD), lambda i: (i, 0)), pl.BlockSpec((parts.shape[0], tm, W_IN_SHARD), lambda i: (0, i, 0))]
    return pl.pallas_call(
        body, grid=(D_MODEL // tm,), in_specs=[tspec] * 3 + gspecs, out_specs=[tspec] * 4,
        out_shape=[jax.ShapeDtypeStruct(w_t.shape, F32)] * 4, name=name,
    )(w_t, m_t, v_t, *[x for pair in grads for x in pair])


def _adamw_call(w, m, v, gparts, name):
    def fn(wv, mv, vv, *gs):
        g = gs[0].astype(F32)
        for gp in gs[1:]:
            g = g + gp.astype(F32)
        delta, m2, v2 = _adamw(wv, g, mv, vv)
        return g, delta, m2, v2
    rows, c = w.shape
    return _ew(fn, [w, m, v] + list(gparts), [], [(c, F32)] * 4, tm=_pick_tm(rows, c), name=name)


def kernel(x, p, g_mix, w_in, b_fox_f, fox_q_gain, fox_k_gain, sc_conv_w, dn_conv_w, dn_a_log, dn_dt_bias,
           dn_norm_gain, w_branch, w_o, g_ffn, w_up, ffn_conv_w, w_down, g_ple, w_ple_gate, w_ple, loss_target,
           m_g_mix, m_w_in, m_b_fox_f, m_fox_q_gain, m_fox_k_gain, m_sc_conv_w, m_dn_conv_w, m_dn_a_log,
           m_dn_dt_bias, m_dn_norm_gain, m_w_branch, m_w_o, m_g_ffn, m_w_up, m_ffn_conv_w, m_w_down, m_g_ple,
           m_w_ple_gate, m_w_ple, v_g_mix, v_w_in, v_b_fox_f, v_fox_q_gain, v_fox_k_gain, v_sc_conv_w, v_dn_conv_w,
           v_dn_a_log, v_dn_dt_bias, v_dn_norm_gain, v_w_branch, v_w_o, v_g_ffn, v_w_up, v_ffn_conv_w, v_w_down,
           v_g_ple, v_w_ple_gate, v_w_ple):
    return _step(x, p, g_mix, w_in, b_fox_f, fox_q_gain, fox_k_gain, sc_conv_w, dn_conv_w, dn_a_log, dn_dt_bias,
                 dn_norm_gain, w_branch, w_o, g_ffn, w_up, ffn_conv_w, w_down, g_ple, w_ple_gate, w_ple, loss_target,
                 m_g_mix, m_w_in, m_b_fox_f, m_fox_q_gain, m_fox_k_gain, m_sc_conv_w, m_dn_conv_w, m_dn_a_log,
                 m_dn_dt_bias, m_dn_norm_gain, m_w_branch, m_w_o, m_g_ffn, m_w_up, m_ffn_conv_w, m_w_down, m_g_ple,
                 m_w_ple_gate, m_w_ple, v_g_mix, v_w_in, v_b_fox_f, v_fox_q_gain, v_fox_k_gain, v_sc_conv_w,
                 v_dn_conv_w, v_dn_a_log, v_dn_dt_bias, v_dn_norm_gain, v_w_branch, v_w_o, v_g_ffn, v_w_up,
                 v_ffn_conv_w, v_w_down, v_g_ple, v_w_ple_gate, v_w_ple)


def _step(*args):
    names = ['x', 'p'] + WEIGHTS + ['loss_target'] + ['m_' + n for n in WEIGHTS] + ['v_' + n for n in WEIGHTS]
    assert len(args) == len(names)
    a = dict(zip(names, args))
    for n, perm in HELD_TRANSPOSED.items():
        for k in (n, 'm_' + n, 'v_' + n):
            a[k] = jnp.transpose(a[k], perm)
    x, target = a['x'][0], a['loss_target'][0]
    p = a['p'][:, 0]
    dev = 4 * lax.axis_index("x") + 2 * lax.axis_index("y") + lax.axis_index("c")

    LATE = [n for n in BIG if n != 'w_in']
    sps = [{n: a[n][layer][None, :] for n in SMALL} for layer in range(DEPTH)]
    shards = [{n: a[n][layer].astype(BF16) for n in BIG} for layer in range(DEPTH)]

    def by_device(full, n):
        by_dest = _by_dest(full, n)
        return lax.dynamic_index_in_dim(by_dest, dev, axis=0, keepdims=False), by_dest.astype(BF16)

    def blank(zone):
        return lax.dynamic_update_index_in_dim(zone, jnp.zeros(zone.shape[1:], zone.dtype), dev, 0)

    w_in_all, conv_all = _all_gather([shards[0]['w_in'], _flat_pack([a[n] for n in CONVW])], 'gather_w_in_l0')
    conv_by_dev = [_flat_unpack(conv_all[d], CONV_SHARD_SHAPES) for d in range(N_DEV)]
    conv_full = {n: jnp.concatenate([conv_by_dev[d][i] for d in range(N_DEV)], axis=2) for i, n in enumerate(CONVW)}
    late0_started, zero = _direct_start([shards[0][n] for n in LATE], False, w_in_all, 'weights_rest_l0_start', NEAR_PEERS)
    w1_started, zero = _direct_start([shards[1][n] for n in BIG], False, late0_started[2][0], 'weights_l1_start', NEAR_PEERS)
    sps[0]['g_mix'] = sps[0]['g_mix'] + zero

    def late_weights0(after):
        zones = _gather_finish(_direct_wait(late0_started, after, 'weights_rest_l0_wait'), 'weights_rest_l0_finish')
        return {n: _full_from_gathered(lax.dynamic_update_index_in_dim(z, shards[0][n], dev, 0), n) for n, z in zip(LATE, zones)}

    w0 = {n: conv_full[n][0] for n in CONVW}
    w0['w_in'] = _w_in_assemble(w_in_all, 'w_in_assemble_l0')
    h, sv0 = _layer_fwd(x, p[0], w0, sps[0], 'l0', late_weights0)
    zones = [lax.dynamic_update_index_in_dim(z, shards[1][n], dev, 0)
             for n, z in zip(BIG, _gather_finish(_direct_wait(w1_started, h, 'weights_l1_wait'), 'weights_l1_finish'))]
    w1 = {n: _full_from_gathered(z, n) for n, z in zip(BIG, zones) if n != 'w_in'}
    w1.update({n: conv_full[n][1] for n in CONVW})
    w1['w_in'] = _w_in_assemble(zones[list(BIG).index('w_in')], 'w_in_assemble_l1')
    h, sv1 = _layer_fwd(h, p[1], w1, sps[1], 'l1')
    dh, loss_part = _loss_bwd(h, target, 'loss')
    grads = [None] * DEPTH
    dh, grads[1] = _layer_bwd(dh, sv1, sps[1], 'l1')
    owns1, sends1 = {}, {}
    for n in BIG:
        owns1[n], sends1[n] = _w_in_split_by_device(grads[1][n], 'w_in_grad_split_l1') if n == 'w_in' else by_device(grads[1][n], n)
    g1_started, zero = _direct_start([sends1[n] for n in BIG], True, dh, 'grads_l1_start')
    early = {}

    def early_grads0(g, sp):
        owns, sends = zip(*[by_device(g[n], n) for n in LATE])
        early['started'], zero = _direct_start(list(sends), True, g['w_branch'], 'grads_rest_l0_start')
        early['owns'] = dict(zip(LATE, owns))
        return {**sp, 'dn_norm_gain': sp['dn_norm_gain'] + zero}

    def last_grad0(g, sp):
        early['own_in'], send = _w_in_split_by_device(g['w_in'], 'w_in_grad_split_l0')
        early['in_started'], zero = _direct_start([send], True, g['w_in'], 'grads_w_in_l0_start')
        return {**sp, 'g_mix': sp['g_mix'] + zero}

    dh, grads[0] = _layer_bwd(dh, sv0, {**sps[0], 'g_ple': sps[0]['g_ple'] + zero}, 'l0', early_grads0, last_grad0)
    grad_x = dh[None]
    own_in0, in0_started = early['own_in'], early['in_started']
    sent = in0_started[2][0]
    zones1 = dict(zip(BIG, _direct_wait(g1_started, sent, 'grads_l1_wait')))
    zones0 = dict(zip(LATE, _direct_wait(early['started'], sent, 'grads_rest_l0_wait')))

    def adamw(n, layer, own, zone, prev):
        rows = (-1, a[n].shape[-1])
        own = own.reshape(rows)
        view = lambda t: t.reshape((DEPTH,) + own.shape)
        return _adamw_layer(view(a[n]), view(a['m_' + n]), view(a['v_' + n]), own, blank(zone).reshape((N_DEV,) + own.shape),
                            layer, prev, f'adamw_{n}_l{layer}')

    out, done = {}, []
    for n in LATE:
        res = adamw(n, 0, early['owns'][n], zones0[n], adamw(n, 1, owns1[n], zones1[n], None))
        out[n] = [r.reshape(a[n].shape) for r in res]
        done.append(res[0][0, 0:1, 0:1])
    (zone_in0,) = _direct_wait(in0_started, jnp.concatenate(done, axis=1), 'grads_w_in_l0_wait')
    res = _adamw_w_in(*[jnp.transpose(a[k], (2, 0, 1)) for k in ('w_in', 'm_w_in', 'v_w_in')],
                      [(own_in0, blank(zone_in0)), (owns1['w_in'], blank(zones1['w_in']))], 'adamw_w_in')
    out['w_in'] = [jnp.transpose(r, (1, 2, 0)) for r in res]

    small_pack = _flat_pack([jnp.stack([grads[layer][n].reshape(-1) for layer in range(DEPTH)]) for n in SMALL] + [loss_part[0, 0]])
    conv_pack = _flat_pack([jnp.stack([grads[layer][n] for layer in range(DEPTH)]) for n in CONVW])
    small_started, _ = _direct_start([small_pack, conv_pack], False, small_pack, 'small_grads_start')
    small_all, conv_all = [lax.dynamic_update_index_in_dim(z, s, dev, 0) for z, s in
                           zip(_direct_wait(small_started, out['w_in'][0], 'small_grads_wait'), (small_pack, conv_pack))]
    res = _adamw_call(_flat_pack([a[n] for n in SMALL]), _flat_pack([a['m_' + n] for n in SMALL]),
                      _flat_pack([a['v_' + n] for n in SMALL]), [small_all[d] for d in range(N_DEV)], 'adamw_replicated')
    loss = res[0].reshape(-1)[SMALL_LOSS_AT]
    for k, r in enumerate(res):
        for n, val in zip(SMALL, _flat_unpack(r, SMALL_SHAPES)):
            out.setdefault(n, [None] * 4)[k] = val
    (conv_sum,) = _ew(lambda *gs: (functools.reduce(lambda s, t: s + t, gs),), [conv_all[d] for d in range(N_DEV)], [],
                      [(PACK_W, F32)], tm=conv_pack.shape[0], name='conv_grad_sum')
    conv_own = [lax.dynamic_slice_in_dim(g, dev * (g.shape[2] // N_DEV), g.shape[2] // N_DEV, axis=2)
                for g in _flat_unpack(conv_sum, CONV_FULL_SHAPES)]
    res = _adamw_call(_flat_pack([a[n] for n in CONVW]), _flat_pack([a['m_' + n] for n in CONVW]),
                      _flat_pack([a['v_' + n] for n in CONVW]), [_flat_pack(conv_own)], 'adamw_conv')
    for k, r in enumerate(res):
        for n, val in zip(CONVW, _flat_unpack(r, CONV_SHARD_SHAPES)):
            out.setdefault(n, [None] * 4)[k] = val

    for n, perm in HELD_TRANSPOSED.items():
        out[n] = [jnp.transpose(r, perm) for r in out[n]]
    outs = [loss, grad_x]
    for k in range(4):
        outs += [out[n][k] for n in WEIGHTS]
    return tuple(outs)
```

```python
import functools

import jax
import jax.numpy as jnp
from jax import lax
from jax.experimental import pallas as pl
from jax.experimental.pallas import tpu as pltpu

F32 = jnp.float32
BF16 = jnp.bfloat16
HI = lax.Precision.HIGHEST

D_MODEL = 1024
DEPTH = 2
N_DEV = 8
PLE_DIM = 256
BW = 512
FOX_HEADS, FOX_DH = 8, 64
DN_HEADS, DN_DH = 4, 128
DN_CHUNK = 64
D_FF = 2816
EPS = 1e-6
NEG = -1e30

ADAM_LR, ADAM_B1, ADAM_B2, ADAM_EPS, ADAM_WD, ADAM_STEP = 0.001, 0.9, 0.999, 1e-08, 0.01, 10

C_FQ, C_FK, C_FV = 0, 512, 1024
C_SB, C_SC, C_SV = 1536, 2048, 2560
C_DQ, C_DK, C_DV, C_DZ = 3072, 3584, 4096, 4608
C_GATE = 5120
C_SMALL = 8192
IN_P = 8320
IN_ORIG = 8208

WEIGHTS = ['g_mix', 'w_in', 'b_fox_f', 'fox_q_gain', 'fox_k_gain', 'sc_conv_w', 'dn_conv_w', 'dn_a_log',
           'dn_dt_bias', 'dn_norm_gain', 'w_branch', 'w_o', 'g_ffn', 'w_up', 'ffn_conv_w', 'w_down', 'g_ple',
           'w_ple_gate', 'w_ple']
BIG = {'w_in': 1, 'w_branch': 2, 'w_o': 0, 'w_up': 0, 'w_down': 0, 'w_ple_gate': 0, 'w_ple': 1}
HELD_TRANSPOSED = {'w_up': (0, 2, 1)}
CONVW = {'sc_conv_w': 1, 'dn_conv_w': 1, 'ffn_conv_w': 1}
SHARDED = {**BIG, **CONVW}
SMALL = [n for n in WEIGHTS if n not in SHARDED]
MESH = pl.DeviceIdType.MESH


def _sigmoid(x):
    return 0.5 * (jnp.tanh(0.5 * x) + 1.0)


def _silu(x):
    return x * _sigmoid(x)


def _log1pexp_negabs(z):
    return jnp.log(1.0 + jnp.exp(-jnp.abs(z)))


def _log_sigmoid(z):
    return jnp.minimum(z, 0.0) - _log1pexp_negabs(z)


def _softplus(z):
    return jnp.maximum(z, 0.0) + _log1pexp_negabs(z)


def _rms(x, g):
    return x * lax.rsqrt(jnp.mean(x * x, axis=-1, keepdims=True) + EPS) * g


def _l2(x):
    return x * lax.rsqrt(jnp.sum(x * x, axis=-1, keepdims=True) + EPS)


def _dot(a, b, dims, precision=None):
    return lax.dot_general(a, b, (dims, ((), ())), preferred_element_type=F32, precision=precision)


NN = ((1,), (0,))
NT = ((1,), (1,))
TN = ((0,), (0,))


def _shift_down(x, s):
    if s == 0:
        return x
    t = lax.broadcasted_iota(jnp.int32, x.shape, 0)
    return jnp.where(t >= s, pltpu.roll(x, s, 0), 0.0)


def _shift_up(x, s):
    if s == 0:
        return x
    n = x.shape[0]
    t = lax.broadcasted_iota(jnp.int32, x.shape, 0)
    return jnp.where(t < n - s, pltpu.roll(x, n - s, 0), 0.0)


def _conv(x, w):
    k = w.shape[0]
    y = w[k - 1:k] * x
    for j in range(k - 1):
        y = y + w[j:j + 1] * _shift_down(x, k - 1 - j)
    return y


def _conv_bwd(x, w, dy):
    k = w.shape[0]
    dx = w[k - 1:k] * dy
    dws = []
    for j in range(k - 1):
        dx = dx + w[j:j + 1] * _shift_up(dy, k - 1 - j)
        dws.append(jnp.sum(dy * _shift_down(x, k - 1 - j), axis=0, keepdims=True))
    dws.append(jnp.sum(dy * x, axis=0, keepdims=True))
    return dx, dws


MM_VMEM_BUDGET = 36 << 20
MM_STEP_BYTES = 1 << 20


def _mm_tiles(m, n, k, a_size, b_size, tile_size, cast_a):
    best = None
    for tm in [d for d in (2048, 1024, 512, 256, 128) if d <= m and m % d == 0] or [m]:
        for tn in [d for d in range(128, min(n, 2048) + 1, 128) if n % d == 0] or [n]:
            vmem = 2 * tm * k * a_size + (2 * tm * k if cast_a else 0) + 2 * k * tn * b_size + 2 * tm * tn * tile_size
            if vmem > MM_VMEM_BUDGET:
                continue
            steps = (m // tm) * (n // tn)
            cost = m * k * a_size + (m // tm) * k * n * b_size + m * n * tile_size + steps * MM_STEP_BYTES
            if best is None or cost < best[0]:
                best = (cost, tm, tn)
    assert best is not None, (m, n, k)
    return best[1], best[2]


def _mm(a, b, mode, outs, *, epi=None, epi_args=(), name):
    if mode == 'nn':
        (m, k), (k2, n) = a.shape, b.shape
    elif mode == 'nt':
        (m, k), (n, k2) = a.shape, b.shape
    else:
        (k, m), (k2, n) = a.shape, b.shape
    assert k == k2, (a.shape, b.shape, mode)
    tile_size = sum(jnp.dtype(dt).itemsize for dt in outs) + sum(e.dtype.itemsize for e in epi_args if e.shape[0] != 1)
    tm, tn = _mm_tiles(m, n, k, a.dtype.itemsize, b.dtype.itemsize, tile_size, a.dtype != BF16)
    dims = {'nn': NN, 'nt': NT, 'tn': TN}[mode]
    a_spec = pl.BlockSpec((k, tm), lambda i, j: (0, i)) if mode == 'tn' else pl.BlockSpec((tm, k), lambda i, j: (i, 0))
    b_spec = pl.BlockSpec((tn, k), lambda i, j: (j, 0)) if mode == 'nt' else pl.BlockSpec((k, tn), lambda i, j: (0, j))
    e_specs = [pl.BlockSpec((1, tn), lambda i, j: (0, j)) if e.shape[0] == 1 else pl.BlockSpec((tm, tn), lambda i, j: (i, j))
               for e in epi_args]
    ne, no = len(epi_args), len(outs)
    cast_a = a.dtype != BF16

    def body(a_ref, b_ref, *rest):
        if cast_a:
            a_sc = rest[-1]

            @pl.when(pl.program_id(1) == 0)
            def _():
                a_sc[...] = a_ref[...].astype(BF16)
            av = a_sc[...]
        else:
            av = a_ref[...]
        acc = _dot(av, b_ref[...].astype(BF16), dims)
        vals = epi(acc, *[e[...] for e in rest[:ne]]) if epi is not None else (acc,)
        for o_ref, v in zip(rest[ne:ne + no], vals):
            o_ref[...] = v.astype(o_ref.dtype)

    res = pl.pallas_call(
        body, grid=(m // tm, n // tn),
        in_specs=[a_spec, b_spec] + e_specs,
        out_specs=[pl.BlockSpec((tm, tn), lambda i, j: (i, j)) for _ in outs],
        out_shape=[jax.ShapeDtypeStruct((m, n), dt) for dt in outs],
        scratch_shapes=[pltpu.VMEM(a_spec.block_shape, BF16)] if cast_a else [],
        name=name,
    )(a, b, *epi_args)
    return res[0] if no == 1 else res


def _ew(fn, tiled, bcast, outs, reds=(), *, tm=256, name):
    secs = [(t, 0, t.shape[1]) if not isinstance(t, tuple) else t for t in tiled]
    m = secs[0][0].shape[0]
    tm = min(tm, m)
    assert m % tm == 0
    in_specs = []
    for arr, off, w in secs:
        assert off % w == 0
        in_specs.append(pl.BlockSpec((tm, w), functools.partial(lambda i, c: (i, c), c=off // w)))
    in_specs += [pl.BlockSpec(b.shape, lambda i: (0, 0)) for b in bcast]
    nin, no = len(in_specs), len(outs)

    def body(*refs):
        vals = fn(*[r[...] for r in refs[:nin]])
        for r, v in zip(refs[nin:nin + no], vals[:no]):
            r[...] = v.astype(r.dtype)
        i = pl.program_id(0)
        for r, v in zip(refs[nin + no:], vals[no:]):
            @pl.when(i == 0)
            def _():
                r[...] = v

            @pl.when(i > 0)
            def _():
                r[...] += v

    res = pl.pallas_call(
        body, grid=(m // tm,), in_specs=in_specs,
        out_specs=[pl.BlockSpec((tm, c), lambda i: (i, 0)) for c, _ in outs] + [pl.BlockSpec(s, lambda i: (0, 0)) for s in reds],
        out_shape=[jax.ShapeDtypeStruct((m, c), dt) for c, dt in outs] + [jax.ShapeDtypeStruct(s, F32) for s in reds],
        name=name,
    )(*[s[0] for s in secs], *bcast)
    return res


def _cb(fn, cols, params, outs, pouts, *, tc, nblk, name):
    t = cols[0][0].shape[0]
    in_specs = []
    for arr, off in cols:
        assert off % tc == 0
        in_specs.append(pl.BlockSpec((t, tc), functools.partial(lambda c, o: (0, o + c), o=off // tc)))
    for arr, off in params:
        in_specs.append(pl.BlockSpec((arr.shape[0], tc), functools.partial(lambda c, o: (0, o + c), o=off // tc)))
    nin, no = len(in_specs), len(outs)

    def body(*refs):
        vals = fn(*[r[...] for r in refs[:nin]])
        for r, v in zip(refs[nin:], vals):
            r[...] = v.astype(r.dtype)

    return pl.pallas_call(
        body, grid=(nblk,), in_specs=in_specs,
        out_specs=[pl.BlockSpec((t, tc), lambda c: (0, c)) for _ in outs] + [pl.BlockSpec((k, tc), lambda c: (0, c)) for k in pouts],
        out_shape=[jax.ShapeDtypeStruct((t, nblk * tc), dt) for dt in outs] + [jax.ShapeDtypeStruct((k, nblk * tc), F32) for k in pouts],
        name=name,
    )(*[c[0] for c in cols], *[p[0] for p in params])


CUM_BLK = 256


def _prep_point(s, bias, alog):
    lane = lax.broadcasted_iota(jnp.int32, s.shape, 1)
    z = s + bias
    return jnp.where(lane < 8, _log_sigmoid(z),
                     jnp.where(lane < 12, _sigmoid(s),
                               jnp.where(lane < 16, -jnp.exp(alog) * _softplus(z), 0.0)))


def _tri(n, upper):
    r = lax.broadcasted_iota(jnp.int32, (n, n), 0)
    c = lax.broadcasted_iota(jnp.int32, (n, n), 1)
    return (r <= c if upper else r >= c).astype(F32)


def _prep_fwd(proj, bias_row, alog_row, name):
    t = proj.shape[0]
    nb = t // CUM_BLK

    def body(s_ref, b_ref, a_ref, o_ref):
        pre = _prep_point(s_ref[...], b_ref[...], a_ref[...])
        lane = lax.broadcasted_iota(jnp.int32, (CUM_BLK, 128), 1)
        tri = _tri(CUM_BLK, False)
        carry = jnp.zeros((1, 128), F32)
        for blk in range(nb):
            xb = pre[blk * CUM_BLK:(blk + 1) * CUM_BLK]
            cb = _dot(tri, xb, NN, HI) + carry
            carry = cb[CUM_BLK - 1:CUM_BLK]
            o_ref[blk * CUM_BLK:(blk + 1) * CUM_BLK, :] = jnp.where(lane < 8, cb, xb)

    return pl.pallas_call(
        body, grid=(1,),
        in_specs=[pl.BlockSpec((t, 128), lambda i: (0, C_SMALL // 128)), pl.BlockSpec((1, 128), lambda i: (0, 0)),
                  pl.BlockSpec((1, 128), lambda i: (0, 0))],
        out_specs=pl.BlockSpec((t, 128), lambda i: (0, 0)),
        out_shape=jax.ShapeDtypeStruct((t, 128), F32), name=name,
    )(proj, bias_row, alog_row)


def _prep_bwd(proj, bias_row, alog_row, daux, name):
    t = proj.shape[0]
    nb = t // CUM_BLK

    def body(s_ref, b_ref, a_ref, d_ref, ds_ref, db_ref, da_ref, dpre_sc):
        lane = lax.broadcasted_iota(jnp.int32, (CUM_BLK, 128), 1)
        tri = _tri(CUM_BLK, True)
        carry = jnp.zeros((1, 128), F32)
        for blk in reversed(range(nb)):
            db = d_ref[blk * CUM_BLK:(blk + 1) * CUM_BLK, :]
            cb = _dot(tri, db, NN, HI) + carry
            carry = cb[0:1]
            dpre_sc[blk * CUM_BLK:(blk + 1) * CUM_BLK, :] = jnp.where(lane < 8, cb, db)
        _, vjp = jax.vjp(_prep_point, s_ref[...], b_ref[...], a_ref[...])
        ds, dbias, dalog = vjp(dpre_sc[...])
        ds_ref[...] = ds.astype(ds_ref.dtype)
        db_ref[...] = dbias
        da_ref[...] = dalog

    return pl.pallas_call(
        body, grid=(1,),
        in_specs=[pl.BlockSpec((t, 128), lambda i: (0, C_SMALL // 128)), pl.BlockSpec((1, 128), lambda i: (0, 0)),
                  pl.BlockSpec((1, 128), lambda i: (0, 0)), pl.BlockSpec((t, 128), lambda i: (0, 0))],
        out_specs=[pl.BlockSpec((t, 128), lambda i: (0, 0)), pl.BlockSpec((1, 128), lambda i: (0, 0)),
                   pl.BlockSpec((1, 128), lambda i: (0, 0))],
        out_shape=[jax.ShapeDtypeStruct((t, 128), BF16), jax.ShapeDtypeStruct((1, 128), F32), jax.ShapeDtypeStruct((1, 128), F32)],
        scratch_shapes=[pltpu.VMEM((t, 128), F32)], name=name,
    )(proj, bias_row, alog_row, daux)


ATT_TQ = 256
FOX_SCALE = FOX_DH ** -0.5


def _qnorm(q, g):
    return _rms(q, g) * FOX_SCALE


def _att_scores(qn_blk, kn, cfc_blk, cfr, qi, tq, kend):
    s = _dot(qn_blk.astype(BF16), kn[:kend].astype(BF16), NT) + cfc_blk - cfr[:, :kend]
    row = lax.broadcasted_iota(jnp.int32, (tq, kend), 0) + qi * tq
    col = lax.broadcasted_iota(jnp.int32, (tq, kend), 1)
    return s, row >= col


ATT_PAIR = 128 // FOX_DH


def _att_specs(t):
    pair = lambda off: pl.BlockSpec((t, 128), functools.partial(lambda i, o: (0, o + i), o=off // 128))
    gain = pl.BlockSpec((1, FOX_DH), lambda i: (0, 0))
    hd = lambda i: (i, 0, 0)
    col, row = pl.BlockSpec((ATT_PAIR, t, 1), hd), pl.BlockSpec((ATT_PAIR, 1, t), hd)
    return [pair(C_FQ), pair(C_FK), pair(C_FV), gain, gain, col, row], pl.BlockSpec((t, 128), lambda i: (0, i)), col, row


def _att_fwd(proj, qg, kg, cfc, cfr, name):
    t = proj.shape[0]
    tq = min(ATT_TQ, t)
    in_specs, pair_out, col, _ = _att_specs(t)

    def body(q_ref, k_ref, v_ref, qg_ref, kg_ref, cfc_ref, cfr_ref, o_ref, lse_ref):
        for e in range(ATT_PAIR):
            lanes = slice(e * FOX_DH, (e + 1) * FOX_DH)
            qn = _qnorm(q_ref[:, lanes], qg_ref[...])
            kn = _rms(k_ref[:, lanes], kg_ref[...])
            v = v_ref[:, lanes].astype(BF16)
            cfr = cfr_ref[e]
            for qi in range(t // tq):
                kend = (qi + 1) * tq
                rows = slice(qi * tq, kend)
                s, mask = _att_scores(qn[rows], kn, cfc_ref[e, rows, :], cfr, qi, tq, kend)
                s = jnp.where(mask, s, NEG)
                m = jnp.max(s, axis=1, keepdims=True)
                p = jnp.exp(s - m)
                l = jnp.sum(p, axis=1, keepdims=True)
                o_ref[rows, lanes] = _dot(p.astype(BF16), v[:kend], NN) / l
                lse_ref[e, rows, :] = m + jnp.log(l)

    return pl.pallas_call(
        body, grid=(FOX_HEADS // ATT_PAIR,), in_specs=in_specs, out_specs=[pair_out, col],
        out_shape=[jax.ShapeDtypeStruct((t, BW), F32), jax.ShapeDtypeStruct((FOX_HEADS, t, 1), F32)], name=name,
    )(proj, proj, proj, qg, kg, cfc, cfr)


def _att_bwd(proj, qg, kg, cfc, cfr, lse, o, do, name):
    t = proj.shape[0]
    tq = min(ATT_TQ, t)
    in_specs, pair_out, col, row = _att_specs(t)

    def body(q_ref, k_ref, v_ref, qg_ref, kg_ref, cfc_ref, cfr_ref, lse_ref, o_ref, do_ref,
             dq_ref, dk_ref, dv_ref, dcfc_ref, dcfr_ref, dqg_ref, dkg_ref, dqn_sc, dkn_sc, dv_sc, dcfr_sc):
        for e in range(ATT_PAIR):
            lanes = slice(e * FOX_DH, (e + 1) * FOX_DH)
            qn, vjp_q = jax.vjp(_qnorm, q_ref[:, lanes], qg_ref[...])
            kn, vjp_k = jax.vjp(_rms, k_ref[:, lanes], kg_ref[...])
            v = v_ref[:, lanes].astype(BF16)
            cfr = cfr_ref[e]
            do_e = do_ref[:, lanes]
            delta = jnp.sum(do_e * o_ref[:, lanes], axis=1, keepdims=True)
            dkn_sc[...] = jnp.zeros_like(dkn_sc)
            dv_sc[...] = jnp.zeros_like(dv_sc)
            dcfr_sc[...] = jnp.zeros_like(dcfr_sc)
            for qi in range(t // tq):
                kend = (qi + 1) * tq
                rows = slice(qi * tq, kend)
                s, mask = _att_scores(qn[rows], kn, cfc_ref[e, rows, :], cfr, qi, tq, kend)
                p = jnp.where(mask, jnp.exp(jnp.where(mask, s, NEG) - lse_ref[e, rows, :]), 0.0)
                do_b = do_e[rows].astype(BF16)
                dv_sc[0:kend, :] += _dot(p.astype(BF16), do_b, TN)
                dp = _dot(do_b, v[:kend], NT)
                ds = p * (dp - delta[rows])
                ds_b = ds.astype(BF16)
                dqn_sc[rows, :] = _dot(ds_b, kn[:kend].astype(BF16), NN)
                dkn_sc[0:kend, :] += _dot(ds_b, qn[rows].astype(BF16), TN)
                dcfc_ref[e, rows, :] = jnp.sum(ds, axis=1, keepdims=True)
                dcfr_sc[:, 0:kend] -= jnp.sum(ds, axis=0, keepdims=True)
            dq, dqg = vjp_q(dqn_sc[...])
            dk, dkg = vjp_k(dkn_sc[...])
            dq_ref[:, lanes] = dq.astype(dq_ref.dtype)
            dk_ref[:, lanes] = dk.astype(dk_ref.dtype)
            dv_ref[:, lanes] = dv_sc[...].astype(dv_ref.dtype)
            dcfr_ref[e] = dcfr_sc[...]
            dqg_ref[e] = dqg
            dkg_ref[e] = dkg

    gsp = pl.BlockSpec((ATT_PAIR, 1, FOX_DH), lambda i: (i, 0, 0))
    return pl.pallas_call(
        body, grid=(FOX_HEADS // ATT_PAIR,),
        in_specs=in_specs + [col, pair_out, pair_out],
        out_specs=[pair_out] * 3 + [col, row, gsp, gsp],
        out_shape=[jax.ShapeDtypeStruct((t, BW), BF16)] * 3
        + [jax.ShapeDtypeStruct((FOX_HEADS, t, 1), F32), jax.ShapeDtypeStruct((FOX_HEADS, 1, t), F32)]
        + [jax.ShapeDtypeStruct((FOX_HEADS, 1, FOX_DH), F32)] * 2,
        scratch_shapes=[pltpu.VMEM((t, FOX_DH), F32)] * 3 + [pltpu.VMEM((1, t), F32)], name=name,
    )(proj, proj, proj, qg, kg, cfc, cfr, lse, o, do)


DN_SCALE = DN_DH ** -0.5


DN_BATCH = 8


@functools.partial(jax.custom_vjp, nondiff_argnums=(2, 3))
def _mm3(a, b, dims, batch=False):
    return _mm3_passes(a, b, dims, batch)


def _mm3_fwd(a, b, dims, batch):
    return _mm3_passes(a, b, dims, batch), (a, b)


def _mm3_bwd(dims, batch, res, dc):
    a, b = res
    if dims == NN:
        return _mm3_passes(dc, b, NT, batch), _mm3_passes(a, dc, TN, batch)
    if dims == NT:
        return _mm3_passes(dc, b, NN, batch), _mm3_passes(dc, a, TN, batch)
    return _mm3_passes(b, dc, NT, batch), _mm3_passes(a, dc, NN, batch)


_mm3.defvjp(_mm3_fwd, _mm3_bwd)


def _mm3_passes(a, b, dims, batch):
    if batch:
        dn = (((dims[0][0] + 1,), (dims[1][0] + 1,)), ((0,), (0,)))
        dot = lambda p, q: lax.dot_general(p, q, dn, preferred_element_type=F32)
    else:
        dot = lambda p, q: _dot(p, q, dims)
    ah, bh = a.astype(BF16), b.astype(BF16)
    al, bl = (a - ah.astype(F32)).astype(BF16), (b - bh.astype(F32)).astype(BF16)
    return dot(ah, bh) + (dot(ah, bl) + dot(al, bh))


def _dn_local(qc, kc, vc, g, beta):
    nb, c, _ = qc.shape
    ii = lax.broadcasted_iota(jnp.int32, (c, c), 0)
    jj = lax.broadcasted_iota(jnp.int32, (c, c), 1)
    incl, strict = ii >= jj, ii > jj
    lower = jnp.broadcast_to(incl.astype(F32), (nb, c, c))
    eye = (ii == jj).astype(F32)
    mm = functools.partial(_mm3, batch=True)
    dm = mm(lower, jnp.where(strict, g, 0.0), NN)
    decay = jnp.where(incl, jnp.exp(jnp.where(incl, dm, 0.0)), 0.0)
    gcum = mm(lower, g * jnp.ones((1, 1, DN_DH), F32), NN)
    eg = jnp.exp(gcum)
    glast = gcum[:, c - 1:c]
    kb = kc * beta
    n1 = jnp.where(strict, mm(kb, kc, NT) * decay, 0.0)
    inv = eye - n1
    pw = n1
    for _ in range(5):
        pw = mm(pw, pw, NN)
        inv = inv + mm(pw, inv, NN)
    sol = mm(inv, jnp.concatenate([vc * beta, kb * eg], axis=2), NN)
    qk = jnp.where(incl, mm(qc, kc, NT) * decay, 0.0)
    return sol[:, :, :DN_DH], sol[:, :, DN_DH:], qk, qc * eg, kc * jnp.exp(glast - gcum), jnp.exp(glast)


@functools.partial(jax.custom_vjp, nondiff_argnums=(2,))
def _mm1(a, b, dims):
    return _dot(a.astype(BF16), b.astype(BF16), dims)


def _mm1_fwd(a, b, dims):
    return _mm1(a, b, dims), (a, b)


def _mm1_bwd(dims, res, dc):
    a, b = res
    if dims == NN:
        return _mm1(dc, b, NT), _mm1(a, dc, TN)
    if dims == NT:
        return _mm1(dc, b, NN), _mm1(dc, a, TN)
    return _mm1(b, dc, NT), _mm1(a, dc, NN)


_mm1.defvjp(_mm1_fwd, _mm1_bwd)


def _dn_state(u, kcum, qk, qdec, kdec, egl, state):
    v_new = u - _mm1(kcum, state, NN)
    out = _mm1(qdec, state, NN) + _mm1(qk, v_new, NN)
    return out, state * egl + _mm1(kdec, v_new, TN)


def _dn_pre_q(c):
    return _l2(_silu(c)) * DN_SCALE


def _dn_pre_k(c):
    return _l2(_silu(c))


def _dn_post(o, z, ng):
    return _rms(o, ng) * _silu(z)


def _dn_specs(t):
    cblk = lambda o: pl.BlockSpec((t, DN_DH), functools.partial(lambda h, o: (0, o + h), o=o // DN_DH))
    wblk = lambda o: pl.BlockSpec((4, DN_DH), functools.partial(lambda h, o: (0, o + h), o=o // DN_DH))
    proj_specs = [cblk(C_DQ), cblk(C_DK), cblk(C_DV), cblk(C_DZ)]
    w_specs = [wblk(0), wblk(BW), wblk(2 * BW)]
    gb_spec = pl.BlockSpec((1, t, 2), lambda h: (h, 0, 0))
    return proj_specs, w_specs, gb_spec


def _chunk_rows(n, count=1):
    return pl.ds(pl.multiple_of(n * DN_CHUNK, DN_CHUNK), count * DN_CHUNK)


def _egl_rows(n, count=1):
    return pl.ds(pl.multiple_of(n * 8, 8), count * 8)


def _dn_local_inputs(n, qn_sc, kn_sc, vv_sc, gb_ref):
    r = _chunk_rows(n, DN_BATCH)
    split = lambda v: v.reshape(DN_BATCH, DN_CHUNK, v.shape[-1])
    gbv = split(gb_ref[0, r, :])
    return split(qn_sc[r, :]), split(kn_sc[r, :]), split(vv_sc[r, :]), gbv[:, :, 0:1], gbv[:, :, 1:2]


def _dn_local_phase(nc, qn_sc, kn_sc, vv_sc, gb_ref, loc):
    def step(i, carry):
        n = i * DN_BATCH
        vals = _dn_local(*_dn_local_inputs(n, qn_sc, kn_sc, vv_sc, gb_ref))
        for sc, val in zip(loc[:5], vals[:5]):
            sc[_chunk_rows(n, DN_BATCH), :] = val.reshape(DN_BATCH * DN_CHUNK, val.shape[-1])
        loc[5][_egl_rows(n, DN_BATCH), :] = jnp.broadcast_to(vals[5], (DN_BATCH, 8, DN_DH)).reshape(DN_BATCH * 8, DN_DH)
        return carry

    lax.fori_loop(0, nc // DN_BATCH, step, 0)


def _dn_loc_scratch(t, nc):
    big = pltpu.VMEM((t, DN_DH), F32)
    return [big, big, pltpu.VMEM((t, DN_CHUNK), F32), big, big, pltpu.VMEM((nc * 8, DN_DH), F32)]


def _dn_loc_io(t, nc, **spec_args):
    head_cols = pl.BlockSpec((t, DN_DH), lambda h: (0, h), **spec_args)
    per_head = lambda rows, cols: pl.BlockSpec((1, rows, cols), lambda h: (h, 0, 0), **spec_args)
    wide = jax.ShapeDtypeStruct((t, BW), F32)
    shapes = [wide, wide, jax.ShapeDtypeStruct((DN_HEADS, t, DN_CHUNK), F32), wide, wide,
              jax.ShapeDtypeStruct((DN_HEADS, nc * 8, DN_DH), F32)]
    return shapes, [head_cols, head_cols, per_head(t, DN_CHUNK), head_cols, head_cols, per_head(nc * 8, DN_DH)]


def _dn_loc_refs(refs):
    return refs[0], refs[1], refs[2].at[0], refs[3], refs[4], refs[5].at[0]


def _dn_fwd(proj, conv_w, gb, ng, name):
    t = proj.shape[0]
    nc = t // DN_CHUNK
    assert nc % DN_BATCH == 0
    proj_specs, w_specs, gb_spec = _dn_specs(t)
    loc_shapes, loc_specs = _dn_loc_io(t, nc)

    def body(q_ref, k_ref, v_ref, z_ref, wq_ref, wk_ref, wv_ref, gb_ref, ng_ref, y_ref, o_ref, st_ref, *rest):
        loc = _dn_loc_refs(rest[:6])
        qn_sc, kn_sc, vv_sc = rest[6:]
        qn_sc[...] = _dn_pre_q(_conv(q_ref[...], wq_ref[...]))
        kn_sc[...] = _dn_pre_k(_conv(k_ref[...], wk_ref[...]))
        vv_sc[...] = _silu(_conv(v_ref[...], wv_ref[...]))
        _dn_local_phase(nc, qn_sc, kn_sc, vv_sc, gb_ref, loc)
        u_sc, kcum_sc, qk_sc, qdec_sc, kdec_sc, egl_sc = loc

        def chunk(n, state):
            r = _chunk_rows(n)
            egl = egl_sc[_egl_rows(n), :][0:1]
            out, new_state = _dn_state(u_sc[r, :], kcum_sc[r, :], qk_sc[r, :], qdec_sc[r, :], kdec_sc[r, :], egl, state)
            st_ref[0, n] = state
            o_ref[r, :] = out
            return new_state

        lax.fori_loop(0, nc, chunk, jnp.zeros((DN_DH, DN_DH), F32))
        y_ref[...] = _dn_post(o_ref[...], z_ref[...], ng_ref[...])

    hblk = pl.BlockSpec((t, DN_DH), lambda h: (0, h))
    return pl.pallas_call(
        body, grid=(DN_HEADS,),
        in_specs=proj_specs + w_specs + [gb_spec, pl.BlockSpec((1, DN_DH), lambda h: (0, 0))],
        out_specs=[hblk, hblk, pl.BlockSpec((1, nc, DN_DH, DN_DH), lambda h: (h, 0, 0, 0))] + loc_specs,
        out_shape=[jax.ShapeDtypeStruct((t, BW), F32), jax.ShapeDtypeStruct((t, BW), F32),
                   jax.ShapeDtypeStruct((DN_HEADS, nc, DN_DH, DN_DH), F32)] + loc_shapes,
        scratch_shapes=[pltpu.VMEM((t, DN_DH), F32)] * 3, name=name,
    )(proj, proj, proj, proj, conv_w, conv_w, conv_w, gb, ng)


def _dn_bwd(proj, conv_w, gb, ng, o, states, loc_saved, dy, name):
    t = proj.shape[0]
    nc = t // DN_CHUNK
    proj_specs, w_specs, gb_spec = _dn_specs(t)
    _, loc_specs = _dn_loc_io(t, nc, pipeline_mode=pl.Buffered(1))

    def body(q_ref, k_ref, v_ref, z_ref, wq_ref, wk_ref, wv_ref, gb_ref, ng_ref, o_ref, st_ref, dy_ref,
             u_ref, kcum_ref, qk_ref, qdec_ref, kdec_ref, egl_ref,
             dq_ref, dk_ref, dv_ref, dz_ref, dwq_ref, dwk_ref, dwv_ref, dgb_ref, dng_ref,
             qn_sc, kn_sc, vv_sc, do_sc, *dloc):
        loc = _dn_loc_refs((u_ref, kcum_ref, qk_ref, qdec_ref, kdec_ref, egl_ref))
        qn_sc[...] = _dn_pre_q(_conv(q_ref[...], wq_ref[...]))
        kn_sc[...] = _dn_pre_k(_conv(k_ref[...], wk_ref[...]))
        vv_sc[...] = _silu(_conv(v_ref[...], wv_ref[...]))
        _, vjp_y = jax.vjp(_dn_post, o_ref[...], z_ref[...], ng_ref[...])
        do, dz, dng = vjp_y(dy_ref[...])
        do_sc[...] = do
        dz_ref[...] = dz.astype(dz_ref.dtype)
        dng_ref[0] = dng
        u_sc, kcum_sc, qk_sc, qdec_sc, kdec_sc, egl_sc = loc

        def state_bwd(i, dstate):
            n = nc - 1 - i
            r = _chunk_rows(n)
            r8 = _egl_rows(n)
            _, vjp = jax.vjp(_dn_state, u_sc[r, :], kcum_sc[r, :], qk_sc[r, :], qdec_sc[r, :], kdec_sc[r, :],
                             egl_sc[r8, :][0:1], st_ref[0, n])
            du, dkcum, dqk, dqdec, dkdec, degl, dprev = vjp((do_sc[r, :], dstate))
            for d_sc, val in zip(dloc[:5], (du, dkcum, dqk, dqdec, dkdec)):
                d_sc[r, :] = val
            dloc[5][r8, :] = jnp.broadcast_to(degl, (8, DN_DH))
            return dprev

        lax.fori_loop(0, nc, state_bwd, jnp.zeros((DN_DH, DN_DH), F32))

        def local_bwd(i, carry):
            n = i * DN_BATCH
            r = _chunk_rows(n, DN_BATCH)
            _, vjp = jax.vjp(_dn_local, *_dn_local_inputs(n, qn_sc, kn_sc, vv_sc, gb_ref))
            cts = tuple(d_sc[r, :].reshape(DN_BATCH, DN_CHUNK, d_sc.shape[-1]) for d_sc in dloc[:5])
            cts += (dloc[5][_egl_rows(n, DN_BATCH), :].reshape(DN_BATCH, 8, DN_DH)[:, 0:1],)
            dqc, dkc, dvc, dg, dbeta = vjp(cts)
            for d_sc, val in zip((dloc[0], dloc[1], dloc[3]), (dqc, dkc, dvc)):
                d_sc[r, :] = val.reshape(DN_BATCH * DN_CHUNK, DN_DH)
            dgb_ref[0, r, :] = jnp.concatenate([dg, dbeta], axis=2).reshape(DN_BATCH * DN_CHUNK, 2)
            return carry

        lax.fori_loop(0, nc // DN_BATCH, local_bwd, 0)
        for x_ref, w_ref, pre, d_sc, dx_ref, dw_ref in ((q_ref, wq_ref, _dn_pre_q, dloc[0], dq_ref, dwq_ref),
                                                       (k_ref, wk_ref, _dn_pre_k, dloc[1], dk_ref, dwk_ref),
                                                       (v_ref, wv_ref, _silu, dloc[3], dv_ref, dwv_ref)):
            _, vjp = jax.vjp(pre, _conv(x_ref[...], w_ref[...]))
            (dc,) = vjp(d_sc[...])
            dx, dws = _conv_bwd(x_ref[...], w_ref[...], dc)
            dx_ref[...] = dx.astype(dx_ref.dtype)
            for j, dw in enumerate(dws):
                dw_ref[j:j + 1, :] = dw

    hblk = pl.BlockSpec((t, DN_DH), lambda h: (0, h))
    wout = pl.BlockSpec((4, DN_DH), lambda h: (0, h))
    return pl.pallas_call(
        body, grid=(DN_HEADS,),
        in_specs=proj_specs + w_specs + [gb_spec, pl.BlockSpec((1, DN_DH), lambda h: (0, 0)), hblk,
                                         pl.BlockSpec((1, nc, DN_DH, DN_DH), lambda h: (h, 0, 0, 0)), hblk] + loc_specs,
        out_specs=[hblk] * 4 + [wout] * 3 + [gb_spec, pl.BlockSpec((1, 1, DN_DH), lambda h: (h, 0, 0))],
        out_shape=[jax.ShapeDtypeStruct((t, BW), BF16)] * 4 + [jax.ShapeDtypeStruct((4, BW), F32)] * 3
        + [jax.ShapeDtypeStruct((DN_HEADS, t, 2), F32), jax.ShapeDtypeStruct((DN_HEADS, 1, DN_DH), F32)],
        scratch_shapes=[pltpu.VMEM((t, DN_DH), F32)] * 4 + _dn_loc_scratch(t, nc), name=name,
    )(proj, proj, proj, proj, conv_w, conv_w, conv_w, gb, ng, o, states, dy, *loc_saved)


MERGE_TM, MERGE_TN = 512, 512


def _merge_specs(t):
    tm, tn = min(MERGE_TM, t), MERGE_TN
    y_spec = pl.BlockSpec((tm, BW), lambda i, j: (i, 0))
    w_spec = pl.BlockSpec((3, BW, tn), lambda i, j: (0, 0, j))
    gate_specs = [pl.BlockSpec((tm, tn), functools.partial(lambda i, j, o: (i, o + j), o=(C_GATE + n * D_MODEL) // tn))
                  for n in range(3)]
    return tm, tn, [y_spec] * 3 + [w_spec] + gate_specs


def _merge_fwd(ys, wb, proj, name):
    t = proj.shape[0]
    tm, tn, in_specs = _merge_specs(t)

    def body(y0, y1, y2, w_ref, g0, g1, g2, o_ref):
        acc = jnp.zeros((tm, tn), F32)
        for n, (y, g) in enumerate(((y0, g0), (y1, g1), (y2, g2))):
            acc = acc + _dot(y[...].astype(BF16), w_ref[n], NN) * _sigmoid(g[...])
        o_ref[...] = acc.astype(o_ref.dtype)

    return pl.pallas_call(
        body, grid=(t // tm, D_MODEL // tn), in_specs=in_specs,
        out_specs=pl.BlockSpec((tm, tn), lambda i, j: (i, j)),
        out_shape=jax.ShapeDtypeStruct((t, D_MODEL), BF16), name=name,
    )(*ys, wb, proj, proj, proj)


def _merge_bwd(ys, wb, proj, dmerged, name):
    t = proj.shape[0]
    tm, tn, in_specs = _merge_specs(t)

    def body(y0, y1, y2, w_ref, g0, g1, g2, dm_ref, dg_ref, dt_ref):
        dm = dm_ref[...]
        for n, (y, g) in enumerate(((y0, g0), (y1, g1), (y2, g2))):
            tn_ = _dot(y[...].astype(BF16), w_ref[n], NN)
            sg = _sigmoid(g[...])
            dg_ref[n] = (dm * tn_ * sg * (1.0 - sg)).astype(dg_ref.dtype)
            dt_ref[n] = (dm * sg).astype(dt_ref.dtype)

    o3 = pl.BlockSpec((3, tm, tn), lambda i, j: (0, i, j))
    return pl.pallas_call(
        body, grid=(t // tm, D_MODEL // tn), in_specs=in_specs + [pl.BlockSpec((tm, tn), lambda i, j: (i, j))],
        out_specs=[o3, o3], out_shape=[jax.ShapeDtypeStruct((3, t, D_MODEL), BF16)] * 2, name=name,
    )(*ys, wb, proj, proj, proj, dmerged)


def _prep_rows(sp):
    z4, z112 = jnp.zeros((1, 4), F32), jnp.zeros((1, 112), F32)
    bias_row = jnp.concatenate([sp['b_fox_f'], z4, sp['dn_dt_bias'], z112], axis=1)
    alog_row = jnp.concatenate([jnp.zeros((1, 12), F32), sp['dn_a_log'], z112], axis=1)
    return bias_row, alog_row


def _sc_fwd_tile(sb, sc, sv, w):
    return (sb * _conv(sc * sv, w),)


def _ffn_act(ug, uv):
    return _silu(ug) * uv


W_IN_COLS = [(C_FQ, 0, 1536), (C_SB, 1544, 1536), (C_DQ, 3080, 1536), (C_DZ, 4624, 512), (C_GATE, 5136, 3072),
             (C_SMALL, 1536, 8), (C_SMALL + 8, 4616, 8)]
W_IN_SHARD = IN_ORIG // N_DEV


def _w_in_segments():
    out = []
    for d, o, w in W_IN_COLS:
        end = o + w
        while o < end:
            k = o // W_IN_SHARD
            n = min(end, (k + 1) * W_IN_SHARD) - o
            out.append((d, k, o - k * W_IN_SHARD, n))
            o, d = o + n, d + n
    return out


def _w_in_assemble(g, name):
    tm = 256

    def body(g_ref, o_ref):
        for d, k, s, n in _w_in_segments():
            o_ref[:, d:d + n] = g_ref[k, :, s:s + n]
        o_ref[:, IN_ORIG:IN_P] = jnp.zeros((tm, IN_P - IN_ORIG), o_ref.dtype)

    return pl.pallas_call(
        body, grid=(D_MODEL // tm,),
        in_specs=[pl.BlockSpec((N_DEV, tm, W_IN_SHARD), lambda i: (0, i, 0))],
        out_specs=pl.BlockSpec((tm, IN_P), lambda i: (i, 0)),
        out_shape=jax.ShapeDtypeStruct((D_MODEL, IN_P), g.dtype), name=name,
    )(g)


W_IN_SPLIT_TM = 128


def _w_in_split_by_device(g, name):
    tm = W_IN_SPLIT_TM

    def body(g_ref, own_ref, send_ref):
        me = 4 * lax.axis_index("x") + 2 * lax.axis_index("y") + lax.axis_index("c")
        for d, k, s, n in _w_in_segments():
            val = g_ref[:, d:d + n]
            send_ref[k, :, s:s + n] = val.astype(send_ref.dtype)

            @pl.when(me == k)
            def _():
                own_ref[:, s:s + n] = val

    return pl.pallas_call(
        body, grid=(D_MODEL // tm,), in_specs=[pl.BlockSpec((tm, IN_P), lambda i: (i, 0))],
        out_specs=[pl.BlockSpec((tm, W_IN_SHARD), lambda i: (i, 0)), pl.BlockSpec((N_DEV, tm, W_IN_SHARD), lambda i: (0, i, 0))],
        out_shape=[jax.ShapeDtypeStruct((D_MODEL, W_IN_SHARD), F32), jax.ShapeDtypeStruct((N_DEV, D_MODEL, W_IN_SHARD), BF16)],
        name=name,
    )(g)


def _layer_fwd(x, p_i, w, sp, tag, late_weights=None):
    t = x.shape[0]
    sv = {'x': x}
    (hn,) = _ew(lambda a, g: (_rms(a, g),), [x], [sp['g_mix']], [(D_MODEL, BF16)], name=f'rms_mix_{tag}')
    proj = _mm(hn, w['w_in'], 'nn', [F32], name=f'in_proj_{tag}')
    bias_row, alog_row = _prep_rows(sp)
    aux = _prep_fwd(proj, bias_row, alog_row, f'prep_{tag}')
    cf = aux[:, :FOX_HEADS].T
    cfc, cfr = cf[:, :, None], cf[:, None, :]
    y_fox, lse = _att_fwd(proj, sp['fox_q_gain'], sp['fox_k_gain'], cfc, cfr, f'fox_fwd_{tag}')
    (y_sc,) = _cb(_sc_fwd_tile, [(proj, C_SB), (proj, C_SC), (proj, C_SV)], [(w['sc_conv_w'], 0)], [F32], [],
                  tc=256, nblk=2, name=f'sc_fwd_{tag}')
    gb = jnp.stack([aux[:, 12:16].T, aux[:, 8:12].T], axis=-1)
    y_dn, o_dn, states, *dn_loc = _dn_fwd(proj, w['dn_conv_w'], gb, sp['dn_norm_gain'], f'dn_fwd_{tag}')
    ys = (y_fox, y_sc, y_dn)
    if late_weights is not None:
        w = {**w, **late_weights(y_dn)}
    merged = _merge_fwd(ys, w['w_branch'], proj, f'merge_fwd_{tag}')
    x1 = _mm(merged, w['w_o'], 'nn', [F32], epi=lambda acc, r: (acc + r,), epi_args=(x,), name=f'o_proj_{tag}')
    (hf,) = _ew(lambda a, g: (_rms(a, g),), [x1], [sp['g_ffn']], [(D_MODEL, BF16)], name=f'rms_ffn_{tag}')
    up = _mm(hf, w['w_up'], 'nt', [F32], name=f'up_proj_{tag}')
    (act,) = _cb(lambda ug, uv, wg, wv: (_ffn_act(_conv(ug, wg), _conv(uv, wv)),), [(up, 0), (up, D_FF)],
                 [(w['ffn_conv_w'], 0), (w['ffn_conv_w'], D_FF)], [BF16], [], tc=256, nblk=D_FF // 256, name=f'ffn_act_{tag}')
    x2 = _mm(act, w['w_down'], 'nn', [F32], epi=lambda acc, r: (acc + r,), epi_args=(x1,), name=f'down_proj_{tag}')
    (hp,) = _ew(lambda a, g: (_rms(a, g),), [x2], [sp['g_ple']], [(D_MODEL, BF16)], name=f'rms_ple_{tag}')
    gp = _mm(hp, w['w_ple_gate'], 'nn', [F32], name=f'ple_gate_{tag}')
    x3 = _mm(p_i, w['w_ple'], 'nn', [F32], epi=lambda acc, g, r: (r + _sigmoid(g) * acc,), epi_args=(gp, x2), name=f'ple_{tag}')
    sv.update(hn=hn, proj=proj, aux=aux, cfc=cfc, cfr=cfr, lse=lse, ys=ys, gb=gb, o_dn=o_dn,
              states=states, merged=merged, x1=x1, hf=hf, up=up, act=act, x2=x2, hp=hp, gp=gp, p=p_i,
              bias_row=bias_row, alog_row=alog_row, w=w, dn_loc=dn_loc)
    return x3, sv


def _rms_bwd(x, g, dh, dres, name):
    def fn(xv, dhv, dr, gv):
        _, vjp = jax.vjp(_rms, xv, gv)
        dx, dg = vjp(dhv)
        return dr + dx, dg
    return _ew(fn, [x, dh, dres], [g], [(D_MODEL, F32)], [(1, D_MODEL)], name=name)


def _layer_bwd(dx3, sv, sp, tag, early_grads=None, last_grad=None):
    t = dx3.shape[0]
    w = sv['w']
    g = {}
    def ple_epi(acc, gpv, d):
        s = _sigmoid(gpv)
        return d * acc * s * (1.0 - s), d * s
    dgp, de = _mm(sv['p'], w['w_ple'], 'nn', [BF16, BF16], epi=ple_epi, epi_args=(sv['gp'], dx3), name=f'ple_bwd_{tag}')
    g['w_ple'] = _mm(sv['p'], de, 'tn', [F32], name=f'd_w_ple_{tag}')
    g['w_ple_gate'] = _mm(sv['hp'], dgp, 'tn', [F32], name=f'd_w_ple_gate_{tag}')
    dhp = _mm(dgp, w['w_ple_gate'], 'nt', [F32], name=f'd_hp_{tag}')
    dx2, g['g_ple'] = _rms_bwd(sv['x2'], sp['g_ple'], dhp, dx3, f'rms_ple_bwd_{tag}')
    dact = _mm(dx2, w['w_down'], 'nt', [F32], name=f'd_act_{tag}')
    g['w_down'] = _mm(sv['act'], dx2, 'tn', [F32], name=f'd_w_down_{tag}')

    def ffn_bwd_tile(ug, uv, da, wg, wv):
        cg, cv = _conv(ug, wg), _conv(uv, wv)
        _, vjp = jax.vjp(_ffn_act, cg, cv)
        dcg, dcv = vjp(da)
        dug, dwg = _conv_bwd(ug, wg, dcg)
        duv, dwv = _conv_bwd(uv, wv, dcv)
        return dug, duv, jnp.concatenate(dwg, axis=0), jnp.concatenate(dwv, axis=0)
    dupg, dupv, dwg, dwv = _cb(ffn_bwd_tile, [(sv['up'], 0), (sv['up'], D_FF), (dact, 0)],
                               [(w['ffn_conv_w'], 0), (w['ffn_conv_w'], D_FF)], [BF16, BF16], [3, 3], tc=256,
                               nblk=D_FF // 256, name=f'ffn_act_bwd_{tag}')
    dup = jnp.concatenate([dupg, dupv], axis=1)
    g['ffn_conv_w'] = jnp.concatenate([dwg, dwv], axis=1)
    g['w_up'] = _mm(dup, sv['hf'], 'tn', [F32], name=f'd_w_up_{tag}')
    dhf = _mm(dup, w['w_up'], 'nn', [F32], name=f'd_hf_{tag}')
    dx1, g['g_ffn'] = _rms_bwd(sv['x1'], sp['g_ffn'], dhf, dx2, f'rms_ffn_bwd_{tag}')
    dmerged = _mm(dx1, w['w_o'], 'nt', [F32], name=f'd_merged_{tag}')
    g['w_o'] = _mm(sv['merged'], dx1, 'tn', [F32], name=f'd_w_o_{tag}')
    dgate, dtn = _merge_bwd(sv['ys'], w['w_branch'], sv['proj'], dmerged, f'merge_bwd_{tag}')
    dys, dwb = [], []
    for n in range(3):
        dys.append(_mm(dtn[n], w['w_branch'][n], 'nt', [F32], name=f'd_y{n}_{tag}'))
        dwb.append(_mm(sv['ys'][n], dtn[n], 'tn', [F32], name=f'd_w_branch{n}_{tag}'))
    g['w_branch'] = jnp.stack(dwb)
    if early_grads is not None:
        sp = early_grads(g, sp)
    ddq, ddk, ddv, ddz, dwq, dwk, dwv_, dgb, dng = _dn_bwd(sv['proj'], w['dn_conv_w'], sv['gb'], sp['dn_norm_gain'],
                                                           sv['o_dn'], sv['states'], sv['dn_loc'], dys[2], f'dn_bwd_{tag}')
    g['dn_conv_w'] = jnp.concatenate([dwq, dwk, dwv_], axis=1)
    g['dn_norm_gain'] = jnp.sum(dng, axis=0)
    def sc_bwd_tile(sb, sc, svv, dy, wv):
        u = sc * svv
        dsb = dy * _conv(u, wv)
        du, dws = _conv_bwd(u, wv, dy * sb)
        return dsb, du * svv, du * sc, jnp.concatenate(dws, axis=0)
    dsb, dsc, dsv, g['sc_conv_w'] = _cb(sc_bwd_tile, [(sv['proj'], C_SB), (sv['proj'], C_SC), (sv['proj'], C_SV), (dys[1], 0)],
                                        [(w['sc_conv_w'], 0)], [BF16, BF16, BF16], [3], tc=256, nblk=2, name=f'sc_bwd_{tag}')
    dfq, dfk, dfv, dcfc, dcfr, dqg, dkg = _att_bwd(sv['proj'], sp['fox_q_gain'], sp['fox_k_gain'], sv['cfc'], sv['cfr'],
                                                   sv['lse'], sv['ys'][0], dys[0], f'fox_bwd_{tag}')
    g['fox_q_gain'] = jnp.sum(dqg, axis=0)
    g['fox_k_gain'] = jnp.sum(dkg, axis=0)
    dcf = (dcfc[:, :, 0] + dcfr[:, 0, :]).T
    daux = jnp.concatenate([dcf, dgb[:, :, 1].T, dgb[:, :, 0].T, jnp.zeros((t, 112), F32)], axis=1)
    dsmall, dbias, dalog = _prep_bwd(sv['proj'], sv['bias_row'], sv['alog_row'], daux, f'prep_bwd_{tag}')
    g['b_fox_f'] = dbias[:, 0:8]
    g['dn_dt_bias'] = dbias[:, 12:16]
    g['dn_a_log'] = dalog[:, 12:16]
    dproj = jnp.concatenate([dfq, dfk, dfv, dsb, dsc, dsv, ddq, ddk, ddv, ddz, dgate[0], dgate[1], dgate[2], dsmall], axis=1)
    g['w_in'] = _mm(sv['hn'], dproj, 'tn', [F32], name=f'd_w_in_{tag}')
    if last_grad is not None:
        sp = last_grad(g, sp)
    dhn = _mm(dproj, w['w_in'], 'nt', [F32], name=f'd_hn_{tag}')
    dx, g['g_mix'] = _rms_bwd(sv['x'], sp['g_mix'], dhn, dx1, f'rms_mix_bwd_{tag}')
    return dx, g


def _loss_bwd(y, target, name):
    inv = 1.0 / y.shape[1]

    def fn(yv, tv):
        err = yv - tv
        return err * inv, jnp.zeros((8, 128), F32) + 0.5 * inv * jnp.sum(err * err)
    return _ew(fn, [y, target], [], [(y.shape[1], F32)], [(8, 128)], name=name)


PACK_W = 1024
FULL_SHAPE = {'w_in': (D_MODEL, IN_ORIG), 'w_branch': (3, BW, D_MODEL), 'w_o': (D_MODEL, D_MODEL), 'w_up': (2 * D_FF, D_MODEL),
              'w_down': (D_FF, D_MODEL), 'w_ple_gate': (D_MODEL, D_MODEL), 'w_ple': (PLE_DIM, D_MODEL),
              'sc_conv_w': (3, BW), 'dn_conv_w': (4, 3 * BW), 'ffn_conv_w': (3, 2 * D_FF)}
SMALL_SHAPE = {'g_mix': D_MODEL, 'b_fox_f': FOX_HEADS, 'fox_q_gain': FOX_DH, 'fox_k_gain': FOX_DH, 'dn_a_log': DN_HEADS,
               'dn_dt_bias': DN_HEADS, 'dn_norm_gain': DN_DH, 'g_ffn': D_MODEL, 'g_ple': D_MODEL}


def _shard_shape(name):
    s = list(FULL_SHAPE[name])
    s[SHARDED[name]] //= N_DEV
    return tuple(s)


def _full_from_gathered(g, name):
    sh, ax = _shard_shape(name), SHARDED[name]
    blocks = jnp.moveaxis(g, 0, ax)
    return blocks.reshape(sh[:ax] + (N_DEV * sh[ax],) + sh[ax + 1:])


def _by_dest(full, name):
    sh, ax = _shard_shape(name), SHARDED[name]
    return jnp.moveaxis(full.reshape(sh[:ax] + (N_DEV, sh[ax]) + sh[ax + 1:]), ax, 0)


def _flat_pack(arrs):
    flat = jnp.concatenate([a.reshape(-1).astype(F32) for a in arrs])
    rows = -(-flat.shape[0] // (8 * PACK_W)) * 8
    return jnp.pad(flat, (0, rows * PACK_W - flat.shape[0])).reshape(rows, PACK_W)


def _flat_unpack(pack, shapes):
    flat, out, off = pack.reshape(-1), [], 0
    for s in shapes:
        n = 1
        for d in s:
            n *= d
        out.append(flat[off:off + n].reshape(s))
        off += n
    return out


SMALL_SHAPES = [(DEPTH, SMALL_SHAPE[n]) for n in SMALL]
SMALL_LOSS_AT = sum(DEPTH * SMALL_SHAPE[n] for n in SMALL)
CONV_SHARD_SHAPES = [(DEPTH,) + _shard_shape(n) for n in CONVW]
CONV_FULL_SHAPES = [(DEPTH,) + FULL_SHAPE[n] for n in CONVW]


def _pick_tm(m, width):
    best = None
    for tm in range(16, m + 1, 16):
        if m % tm == 0 and tm * width * 4 <= (1 << 20):
            best = tm
    return best if best is not None else m


HBM_SPEC = pl.BlockSpec(memory_space=pltpu.HBM)


def _place():
    x, y, c = lax.axis_index("x"), lax.axis_index("y"), lax.axis_index("c")
    return x, y, c, [(1 - x, y), (x, 1 - y), (1 - x, 1 - y)]


def _all_gather(arrs, name):
    n = len(arrs)

    def body(*refs):
        ins, outs = refs[:n], refs[n:2 * n]
        send_sems, recv_sems = refs[2 * n:]
        x, y, c, chips = _place()
        me, sibling = (x, y, c), (x, y, 1 - c)

        def block(a, p):
            return outs[a].at[4 * p[0] + 2 * p[1] + p[2]]

        def copy(a, k, blk, to, src=None):
            return pltpu.make_async_remote_copy(src_ref=block(a, blk) if src is None else src, dst_ref=block(a, blk),
                                                send_sem=send_sems.at[a, k], recv_sem=recv_sems.at[a, k],
                                                device_id=to, device_id_type=MESH)

        first, passed = [], []
        for a in range(n):
            first.append(copy(a, 0, me, sibling, src=ins[a]))
            first += [copy(a, 1 + j, me, (*chip, c), src=ins[a]) for j, chip in enumerate(chips)]
        for cp in first:
            cp.start()
        for j, chip in enumerate(chips):
            for a in range(n):
                copy(a, 1 + j, (*chip, c), me).wait_recv()
                fwd = copy(a, 4 + j, (*chip, c), sibling)
                fwd.start()
                passed.append(fwd)
        for a in range(n):
            copy(a, 0, sibling, me).wait_recv()
            for j, chip in enumerate(chips):
                copy(a, 4 + j, (*chip, 1 - c), me).wait_recv()
        for cp in first + passed:
            cp.wait_send()

    gathered = pl.pallas_call(
        body, in_specs=[HBM_SPEC] * n, out_specs=[HBM_SPEC] * n,
        out_shape=[jax.ShapeDtypeStruct((N_DEV,) + a.shape, a.dtype) for a in arrs],
        scratch_shapes=[pltpu.SemaphoreType.DMA((n, 7)), pltpu.SemaphoreType.DMA((n, 7))], name=name,
    )(*arrs)
    me = 4 * lax.axis_index("x") + 2 * lax.axis_index("y") + lax.axis_index("c")
    return [lax.dynamic_update_index_in_dim(g, a, me, 0) for g, a in zip(gathered, arrs)]


SEM_SPEC = pl.BlockSpec(memory_space=pltpu.SEMAPHORE)
DATAFLOW = pltpu.SideEffectType.DATAFLOW_SIDE_EFFECTING
ALL_PEERS = (1, 2, 3, 4, 5, 6, 7)
NEAR_PEERS = (1, 4, 2, 6)


def _direct_copies(src_refs, zone_refs, send_sems, recv_sems, per_dest, peers):
    x, y, c, _ = _place()
    me = 4 * x + 2 * y + c
    cps = []
    for a, (src, zone) in enumerate(zip(src_refs, zone_refs)):
        for r, bits in enumerate(peers):
            px = 1 - x if bits & 4 else x
            py = 1 - y if bits & 2 else y
            pc = 1 - c if bits & 1 else c
            cps.append(pltpu.make_async_remote_copy(
                src_ref=src.at[4 * px + 2 * py + pc] if per_dest else src, dst_ref=zone.at[me],
                send_sem=send_sems.at[a * len(peers) + r], recv_sem=recv_sems.at[a * len(peers) + r],
                device_id=(px, py, pc), device_id_type=MESH))
    return cps


def _direct_start(srcs, per_dest, after, name, peers=ALL_PEERS):
    n = len(srcs)
    zones = [lax.empty((N_DEV,) + (s.shape[1:] if per_dest else s.shape), s.dtype) for s in srcs]

    def body(*refs):
        src_refs, zone_refs = refs[:n], refs[n:2 * n]
        send_sems, recv_sems = refs[2 * n + 1:2 * n + 3]
        for cp in _direct_copies(src_refs, zone_refs, send_sems, recv_sems, per_dest, peers):
            cp.start()
        refs[-1][...] = jnp.zeros_like(refs[-1])

    sems = pltpu.SemaphoreType.DMA((n * len(peers),))
    res = pl.pallas_call(
        body, name=name, in_specs=[HBM_SPEC] * (2 * n + 1),
        out_shape=[sems, sems] + [pltpu.HBM(s.shape, s.dtype) for s in srcs] + [pltpu.HBM(z.shape, z.dtype) for z in zones]
        + [jax.ShapeDtypeStruct((8, 128), F32)],
        out_specs=[SEM_SPEC, SEM_SPEC] + [HBM_SPEC] * (2 * n) + [pl.BlockSpec(memory_space=pltpu.VMEM)],
        input_output_aliases={i: 2 + i for i in range(2 * n)},
        compiler_params=pltpu.CompilerParams(has_side_effects=DATAFLOW),
    )(*[pltpu.with_memory_space_constraint(s, pltpu.HBM) for s in srcs],
      *[pltpu.with_memory_space_constraint(z, pltpu.HBM) for z in zones], after)
    return (res[0], res[1], list(res[2:2 + n]), list(res[2 + n:2 + 2 * n]), per_dest, peers), res[-1][0:1, 0:1]


def _direct_wait(started, after, name):
    send_sems, recv_sems, srcs, zones, per_dest, peers = started
    n = len(srcs)

    def body(*refs):
        src_refs, zone_refs = refs[:n], refs[n:2 * n]
        for cp in _direct_copies(src_refs, zone_refs, refs[2 * n], refs[2 * n + 1], per_dest, peers):
            cp.wait_send()
            cp.wait_recv()

    res = pl.pallas_call(
        body, name=name, in_specs=[HBM_SPEC] * (2 * n) + [SEM_SPEC, SEM_SPEC, HBM_SPEC],
        out_shape=[pltpu.HBM(s.shape, s.dtype) for s in srcs] + [pltpu.HBM(z.shape, z.dtype) for z in zones],
        out_specs=[HBM_SPEC] * (2 * n), input_output_aliases={i: i for i in range(2 * n)},
        compiler_params=pltpu.CompilerParams(has_side_effects=DATAFLOW),
    )(*srcs, *zones, send_sems, recv_sems, after)
    return list(res[n:])


def _gather_finish(zones, name):
    n = len(zones)

    def body(*refs):
        zone_in, zone_out = refs[:n], refs[n:2 * n]
        send_sems, recv_sems = refs[2 * n:]
        x, y, c, chips = _place()
        cps = []
        for a in range(n):
            for j, (cx, cy) in enumerate(chips):
                blk = 4 * cx + 2 * cy + c
                cp = pltpu.make_async_remote_copy(src_ref=zone_in[a].at[blk], dst_ref=zone_out[a].at[blk],
                                                  send_sem=send_sems.at[a, j], recv_sem=recv_sems.at[a, j],
                                                  device_id=(x, y, 1 - c), device_id_type=MESH)
                cp.start()
                cps.append(cp)
        for cp in cps:
            cp.wait()

    return pl.pallas_call(
        body, in_specs=[HBM_SPEC] * n, out_specs=[HBM_SPEC] * n,
        out_shape=[jax.ShapeDtypeStruct(z.shape, z.dtype) for z in zones], input_output_aliases={a: a for a in range(n)},
        scratch_shapes=[pltpu.SemaphoreType.DMA((n, 3)), pltpu.SemaphoreType.DMA((n, 3))], name=name,
    )(*zones)


def _adamw(w, g, m, v):
    m = ADAM_B1 * m + (1.0 - ADAM_B1) * g
    v = ADAM_B2 * v + (1.0 - ADAM_B2) * jnp.square(g)
    m_hat = m / (1.0 - ADAM_B1 ** ADAM_STEP)
    v_hat = v / (1.0 - ADAM_B2 ** ADAM_STEP)
    delta = -ADAM_LR * (m_hat / (jnp.sqrt(v_hat) + ADAM_EPS) + ADAM_WD * w)
    return delta, m, v


def _adamw_layer(w, m, v, own, parts, layer, prev, name):
    _, rows, c = w.shape
    tm = _pick_tm(rows, c)

    def body(w_ref, m_ref, v_ref, own_ref, parts_ref, *rest):
        g = own_ref[...]
        for j in range(parts_ref.shape[0]):
            g = g + parts_ref[j].astype(F32)
        delta, m2, v2 = _adamw(w_ref[...], g, m_ref[...], v_ref[...])
        g_ref, d_ref, m2_ref, v2_ref = rest[-4:]
        g_ref[...], d_ref[...], m2_ref[...], v2_ref[...] = g, delta, m2, v2

    wspec = pl.BlockSpec((None, tm, c), lambda i: (layer, i, 0))
    prev = list(prev) if prev is not None else []
    return pl.pallas_call(
        body, grid=(rows // tm,),
        in_specs=[wspec] * 3 + [pl.BlockSpec((tm, c), lambda i: (i, 0)), pl.BlockSpec((parts.shape[0], tm, c), lambda i: (0, i, 0))]
        + [HBM_SPEC] * len(prev),
        out_specs=[wspec] * 4, out_shape=[jax.ShapeDtypeStruct(w.shape, F32)] * 4,
        input_output_aliases={5 + k: k for k in range(len(prev))}, name=name,
    )(w, m, v, own, parts, *prev)


W_IN_ADAMW_TM = 128


def _adamw_w_in(w_t, m_t, v_t, grads, name):
    tm = W_IN_ADAMW_TM
    ng = [p.shape[0] for _, p in grads]

    def body(w_ref, m_ref, v_ref, *rest):
        outs = rest[2 * DEPTH:]
        for layer in range(DEPTH):
            own_ref, parts_ref = rest[2 * layer], rest[2 * layer + 1]
            g = own_ref[...]
            for j in range(ng[layer]):
                g = g + parts_ref[j].astype(F32)
            delta, m2, v2 = _adamw(w_ref[:, layer, :].T, g, m_ref[:, layer, :].T, v_ref[:, layer, :].T)
            for o_ref, val in zip(outs, (g, delta, m2, v2)):
                o_ref[:, layer, :] = val.T

    tspec = pl.BlockSpec((W_IN_SHARD, DEPTH, tm), lambda i: (0, 0, i))
    gspecs = []
    for _, parts in grads:
        gspecs += [pl.BlockSpec((tm, W_IN_SHARD), lambda i: (i, 0)), pl.BlockSpec((parts.shape[0], tm, W_IN_SHARD), lambda i: (0, i, 0))]
    return pl.pallas_call(
        body, grid=(D_MODEL // tm,), in_specs=[tspec] * 3 + gspecs, out_specs=[tspec] * 4,
        out_shape=[jax.ShapeDtypeStruct(w_t.shape, F32)] * 4, name=name,
    )(w_t, m_t, v_t, *[x for pair in grads for x in pair])


def _adamw_call(w, m, v, gparts, name):
    def fn(wv, mv, vv, *gs):
        g = gs[0].astype(F32)
        for gp in gs[1:]:
            g = g + gp.astype(F32)
        delta, m2, v2 = _adamw(wv, g, mv, vv)
        return g, delta, m2, v2
    rows, c = w.shape
    return _ew(fn, [w, m, v] + list(gparts), [], [(c, F32)] * 4, tm=_pick_tm(rows, c), name=name)


def kernel(x, p, g_mix, w_in, b_fox_f, fox_q_gain, fox_k_gain, sc_conv_w, dn_conv_w, dn_a_log, dn_dt_bias,
           dn_norm_gain, w_branch, w_o, g_ffn, w_up, ffn_conv_w, w_down, g_ple, w_ple_gate, w_ple, loss_target,
           m_g_mix, m_w_in, m_b_fox_f, m_fox_q_gain, m_fox_k_gain, m_sc_conv_w, m_dn_conv_w, m_dn_a_log,
           m_dn_dt_bias, m_dn_norm_gain, m_w_branch, m_w_o, m_g_ffn, m_w_up, m_ffn_conv_w, m_w_down, m_g_ple,
           m_w_ple_gate, m_w_ple, v_g_mix, v_w_in, v_b_fox_f, v_fox_q_gain, v_fox_k_gain, v_sc_conv_w, v_dn_conv_w,
           v_dn_a_log, v_dn_dt_bias, v_dn_norm_gain, v_w_branch, v_w_o, v_g_ffn, v_w_up, v_ffn_conv_w, v_w_down,
           v_g_ple, v_w_ple_gate, v_w_ple):
    return _step(x, p, g_mix, w_in, b_fox_f, fox_q_gain, fox_k_gain, sc_conv_w, dn_conv_w, dn_a_log, dn_dt_bias,
                 dn_norm_gain, w_branch, w_o, g_ffn, w_up, ffn_conv_w, w_down, g_ple, w_ple_gate, w_ple, loss_target,
                 m_g_mix, m_w_in, m_b_fox_f, m_fox_q_gain, m_fox_k_gain, m_sc_conv_w, m_dn_conv_w, m_dn_a_log,
                 m_dn_dt_bias, m_dn_norm_gain, m_w_branch, m_w_o, m_g_ffn, m_w_up, m_ffn_conv_w, m_w_down, m_g_ple,
                 m_w_ple_gate, m_w_ple, v_g_mix, v_w_in, v_b_fox_f, v_fox_q_gain, v_fox_k_gain, v_sc_conv_w,
                 v_dn_conv_w, v_dn_a_log, v_dn_dt_bias, v_dn_norm_gain, v_w_branch, v_w_o, v_g_ffn, v_w_up,
                 v_ffn_conv_w, v_w_down, v_g_ple, v_w_ple_gate, v_w_ple)


def _step(*args):
    names = ['x', 'p'] + WEIGHTS + ['loss_target'] + ['m_' + n for n in WEIGHTS] + ['v_' + n for n in WEIGHTS]
    assert len(args) == len(names)
    a = dict(zip(names, args))
    for n, perm in HELD_TRANSPOSED.items():
        for k in (n, 'm_' + n, 'v_' + n):
            a[k] = jnp.transpose(a[k], perm)
    x, target = a['x'][0], a['loss_target'][0]
    p = a['p'][:, 0]
    dev = 4 * lax.axis_index("x") + 2 * lax.axis_index("y") + lax.axis_index("c")

    LATE = [n for n in BIG if n != 'w_in']
    sps = [{n: a[n][layer][None, :] for n in SMALL} for layer in range(DEPTH)]
    shards = [{n: a[n][layer].astype(BF16) for n in BIG} for layer in range(DEPTH)]

    def by_device(full, n):
        by_dest = _by_dest(full, n)
        return lax.dynamic_index_in_dim(by_dest, dev, axis=0, keepdims=False), by_dest.astype(BF16)

    def blank(zone):
        return lax.dynamic_update_index_in_dim(zone, jnp.zeros(zone.shape[1:], zone.dtype), dev, 0)

    w_in_all, conv_all = _all_gather([shards[0]['w_in'], _flat_pack([a[n] for n in CONVW])], 'gather_w_in_l0')
    conv_by_dev = [_flat_unpack(conv_all[d], CONV_SHARD_SHAPES) for d in range(N_DEV)]
    conv_full = {n: jnp.concatenate([conv_by_dev[d][i] for d in range(N_DEV)], axis=2) for i, n in enumerate(CONVW)}
    late0_started, zero = _direct_start([shards[0][n] for n in LATE], False, w_in_all, 'weights_rest_l0_start', NEAR_PEERS)
    w1_started, zero = _direct_start([shards[1][n] for n in BIG], False, late0_started[2][0], 'weights_l1_start', NEAR_PEERS)
    sps[0]['g_mix'] = sps[0]['g_mix'] + zero

    def late_weights0(after):
        zones = _gather_finish(_direct_wait(late0_started, after, 'weights_rest_l0_wait'), 'weights_rest_l0_finish')
        return {n: _full_from_gathered(lax.dynamic_update_index_in_dim(z, shards[0][n], dev, 0), n) for n, z in zip(LATE, zones)}

    w0 = {n: conv_full[n][0] for n in CONVW}
    w0['w_in'] = _w_in_assemble(w_in_all, 'w_in_assemble_l0')
    h, sv0 = _layer_fwd(x, p[0], w0, sps[0], 'l0', late_weights0)
    zones = [lax.dynamic_update_index_in_dim(z, shards[1][n], dev, 0)
             for n, z in zip(BIG, _gather_finish(_direct_wait(w1_started, h, 'weights_l1_wait'), 'weights_l1_finish'))]
    w1 = {n: _full_from_gathered(z, n) for n, z in zip(BIG, zones) if n != 'w_in'}
    w1.update({n: conv_full[n][1] for n in CONVW})
    w1['w_in'] = _w_in_assemble(zones[list(BIG).index('w_in')], 'w_in_assemble_l1')
    h, sv1 = _layer_fwd(h, p[1], w1, sps[1], 'l1')
    dh, loss_part = _loss_bwd(h, target, 'loss')
    grads = [None] * DEPTH
    dh, grads[1] = _layer_bwd(dh, sv1, sps[1], 'l1')
    owns1, sends1 = {}, {}
    for n in BIG:
        owns1[n], sends1[n] = _w_in_split_by_device(grads[1][n], 'w_in_grad_split_l1') if n == 'w_in' else by_device(grads[1][n], n)
    g1_started, zero = _direct_start([sends1[n] for n in BIG], True, dh, 'grads_l1_start')
    early = {}

    def early_grads0(g, sp):
        owns, sends = zip(*[by_device(g[n], n) for n in LATE])
        early['started'], zero = _direct_start(list(sends), True, g['w_branch'], 'grads_rest_l0_start')
        early['owns'] = dict(zip(LATE, owns))
        return {**sp, 'dn_norm_gain': sp['dn_norm_gain'] + zero}

    def last_grad0(g, sp):
        early['own_in'], send = _w_in_split_by_device(g['w_in'], 'w_in_grad_split_l0')
        early['in_started'], zero = _direct_start([send], True, g['w_in'], 'grads_w_in_l0_start')
        return {**sp, 'g_mix': sp['g_mix'] + zero}

    dh, grads[0] = _layer_bwd(dh, sv0, {**sps[0], 'g_ple': sps[0]['g_ple'] + zero}, 'l0', early_grads0, last_grad0)
    grad_x = dh[None]
    own_in0, in0_started = early['own_in'], early['in_started']
    sent = in0_started[2][0]
    zones1 = dict(zip(BIG, _direct_wait(g1_started, sent, 'grads_l1_wait')))
    zones0 = dict(zip(LATE, _direct_wait(early['started'], sent, 'grads_rest_l0_wait')))

    def adamw(n, layer, own, zone, prev):
        rows = (-1, a[n].shape[-1])
        own = own.reshape(rows)
        view = lambda t: t.reshape((DEPTH,) + own.shape)
        return _adamw_layer(view(a[n]), view(a['m_' + n]), view(a['v_' + n]), own, blank(zone).reshape((N_DEV,) + own.shape),
                            layer, prev, f'adamw_{n}_l{layer}')

    out, done = {}, []
    for n in LATE:
        res = adamw(n, 0, early['owns'][n], zones0[n], adamw(n, 1, owns1[n], zones1[n], None))
        out[n] = [r.reshape(a[n].shape) for r in res]
        done.append(res[0][0, 0:1, 0:1])
    (zone_in0,) = _direct_wait(in0_started, jnp.concatenate(done + [dh[0:1, 0:1]], axis=1), 'grads_w_in_l0_wait')
    res = _adamw_w_in(*[jnp.transpose(a[k], (2, 0, 1)) for k in ('w_in', 'm_w_in', 'v_w_in')],
                      [(own_in0, blank(zone_in0)), (owns1['w_in'], blank(zones1['w_in']))], 'adamw_w_in')
    out['w_in'] = [jnp.transpose(r, (1, 2, 0)) for r in res]

    small_pack = _flat_pack([jnp.stack([grads[layer][n].reshape(-1) for layer in range(DEPTH)]) for n in SMALL] + [loss_part[0, 0]])
    conv_pack = _flat_pack([jnp.stack([grads[layer][n] for layer in range(DEPTH)]) for n in CONVW])
    small_started, _ = _direct_start([small_pack, conv_pack], False, small_pack, 'small_grads_start')
    small_all, conv_all = [lax.dynamic_update_index_in_dim(z, s, dev, 0) for z, s in
                           zip(_direct_wait(small_started, out['w_in'][0], 'small_grads_wait'), (small_pack, conv_pack))]
    res = _adamw_call(_flat_pack([a[n] for n in SMALL]), _flat_pack([a['m_' + n] for n in SMALL]),
                      _flat_pack([a['v_' + n] for n in SMALL]), [small_all[d] for d in range(N_DEV)], 'adamw_replicated')
    loss = res[0].reshape(-1)[SMALL_LOSS_AT]
    for k, r in enumerate(res):
        for n, val in zip(SMALL, _flat_unpack(r, SMALL_SHAPES)):
            out.setdefault(n, [None] * 4)[k] = val
    (conv_sum,) = _ew(lambda *gs: (functools.reduce(lambda s, t: s + t, gs),), [conv_all[d] for d in range(N_DEV)], [],
                      [(PACK_W, F32)], tm=conv_pack.shape[0], name='conv_grad_sum')
    conv_own = [lax.dynamic_slice_in_dim(g, dev * (g.shape[2] // N_DEV), g.shape[2] // N_DEV, axis=2)
                for g in _flat_unpack(conv_sum, CONV_FULL_SHAPES)]
    res = _adamw_call(_flat_pack([a[n] for n in CONVW]), _flat_pack([a['m_' + n] for n in CONVW]),
                      _flat_pack([a['v_' + n] for n in CONVW]), [_flat_pack(conv_own)], 'adamw_conv')
    for k, r in enumerate(res):
        for n, val in zip(CONVW, _flat_unpack(r, CONV_SHARD_SHAPES)):
            out.setdefault(n, [None] * 4)[k] = val

    for n, perm in HELD_TRANSPOSED.items():
        out[n] = [jnp.transpose(r, perm) for r in out[n]]
    outs = [loss, grad_x]
    for k in range(4):
        outs += [out[n][k] for n in WEIGHTS]
    return tuple(outs)
```

```python
import functools

import jax
import jax.numpy as jnp
from jax import lax
from jax.experimental import pallas as pl
from jax.experimental.pallas import tpu as pltpu

F32 = jnp.float32
BF16 = jnp.bfloat16
HI = lax.Precision.HIGHEST

D_MODEL = 1024
DEPTH = 2
N_DEV = 8
PLE_DIM = 256
BW = 512
FOX_HEADS, FOX_DH = 8, 64
DN_HEADS, DN_DH = 4, 128
DN_CHUNK = 64
D_FF = 2816
EPS = 1e-6
NEG = -1e30

ADAM_LR, ADAM_B1, ADAM_B2, ADAM_EPS, ADAM_WD, ADAM_STEP = 0.001, 0.9, 0.999, 1e-08, 0.01, 10

C_FQ, C_FK, C_FV = 0, 512, 1024
C_SB, C_SC, C_SV = 1536, 2048, 2560
C_DQ, C_DK, C_DV, C_DZ = 3072, 3584, 4096, 4608
C_GATE = 5120
C_SMALL = 8192
IN_P = 8320
IN_ORIG = 8208

WEIGHTS = ['g_mix', 'w_in', 'b_fox_f', 'fox_q_gain', 'fox_k_gain', 'sc_conv_w', 'dn_conv_w', 'dn_a_log',
           'dn_dt_bias', 'dn_norm_gain', 'w_branch', 'w_o', 'g_ffn', 'w_up', 'ffn_conv_w', 'w_down', 'g_ple',
           'w_ple_gate', 'w_ple']
BIG = {'w_in': 1, 'w_branch': 2, 'w_o': 0, 'w_up': 0, 'w_down': 0, 'w_ple_gate': 0, 'w_ple': 1}
HELD_TRANSPOSED = {'w_up': (0, 2, 1)}
CONVW = {'sc_conv_w': 1, 'dn_conv_w': 1, 'ffn_conv_w': 1}
SHARDED = {**BIG, **CONVW}
SMALL = [n for n in WEIGHTS if n not in SHARDED]
MESH = pl.DeviceIdType.MESH


def _sigmoid(x):
    return 0.5 * (jnp.tanh(0.5 * x) + 1.0)


def _silu(x):
    return x * _sigmoid(x)


def _log1pexp_negabs(z):
    return jnp.log(1.0 + jnp.exp(-jnp.abs(z)))


def _log_sigmoid(z):
    return jnp.minimum(z, 0.0) - _log1pexp_negabs(z)


def _softplus(z):
    return jnp.maximum(z, 0.0) + _log1pexp_negabs(z)


def _rms(x, g):
    return x * lax.rsqrt(jnp.mean(x * x, axis=-1, keepdims=True) + EPS) * g


def _l2(x):
    return x * lax.rsqrt(jnp.sum(x * x, axis=-1, keepdims=True) + EPS)


def _dot(a, b, dims, precision=None):
    return lax.dot_general(a, b, (dims, ((), ())), preferred_element_type=F32, precision=precision)


NN = ((1,), (0,))
NT = ((1,), (1,))
TN = ((0,), (0,))


def _shift_down(x, s):
    if s == 0:
        return x
    t = lax.broadcasted_iota(jnp.int32, x.shape, 0)
    return jnp.where(t >= s, pltpu.roll(x, s, 0), 0.0)


def _shift_up(x, s):
    if s == 0:
        return x
    n = x.shape[0]
    t = lax.broadcasted_iota(jnp.int32, x.shape, 0)
    return jnp.where(t < n - s, pltpu.roll(x, n - s, 0), 0.0)


def _conv(x, w):
    k = w.shape[0]
    y = w[k - 1:k] * x
    for j in range(k - 1):
        y = y + w[j:j + 1] * _shift_down(x, k - 1 - j)
    return y


def _conv_bwd(x, w, dy):
    k = w.shape[0]
    dx = w[k - 1:k] * dy
    dws = []
    for j in range(k - 1):
        dx = dx + w[j:j + 1] * _shift_up(dy, k - 1 - j)
        dws.append(jnp.sum(dy * _shift_down(x, k - 1 - j), axis=0, keepdims=True))
    dws.append(jnp.sum(dy * x, axis=0, keepdims=True))
    return dx, dws


MM_VMEM_BUDGET = 36 << 20
MM_STEP_BYTES = 1 << 20


def _mm_tiles(m, n, k, a_size, b_size, tile_size, cast_a):
    best = None
    for tm in [d for d in (2048, 1024, 512, 256, 128) if d <= m and m % d == 0] or [m]:
        for tn in [d for d in range(128, min(n, 2048) + 1, 128) if n % d == 0] or [n]:
            vmem = 2 * tm * k * a_size + (2 * tm * k if cast_a else 0) + 2 * k * tn * b_size + 2 * tm * tn * tile_size
            if vmem > MM_VMEM_BUDGET:
                continue
            steps = (m // tm) * (n // tn)
            cost = m * k * a_size + (m // tm) * k * n * b_size + m * n * tile_size + steps * MM_STEP_BYTES
            if best is None or cost < best[0]:
                best = (cost, tm, tn)
    assert best is not None, (m, n, k)
    return best[1], best[2]


def _mm(a, b, mode, outs, *, epi=None, epi_args=(), name):
    if mode == 'nn':
        (m, k), (k2, n) = a.shape, b.shape
    elif mode == 'nt':
        (m, k), (n, k2) = a.shape, b.shape
    else:
        (k, m), (k2, n) = a.shape, b.shape
    assert k == k2, (a.shape, b.shape, mode)
    tile_size = sum(jnp.dtype(dt).itemsize for dt in outs) + sum(e.dtype.itemsize for e in epi_args if e.shape[0] != 1)
    tm, tn = _mm_tiles(m, n, k, a.dtype.itemsize, b.dtype.itemsize, tile_size, a.dtype != BF16)
    dims = {'nn': NN, 'nt': NT, 'tn': TN}[mode]
    a_spec = pl.BlockSpec((k, tm), lambda i, j: (0, i)) if mode == 'tn' else pl.BlockSpec((tm, k), lambda i, j: (i, 0))
    b_spec = pl.BlockSpec((tn, k), lambda i, j: (j, 0)) if mode == 'nt' else pl.BlockSpec((k, tn), lambda i, j: (0, j))
    e_specs = [pl.BlockSpec((1, tn), lambda i, j: (0, j)) if e.shape[0] == 1 else pl.BlockSpec((tm, tn), lambda i, j: (i, j))
               for e in epi_args]
    ne, no = len(epi_args), len(outs)
    cast_a = a.dtype != BF16

    def body(a_ref, b_ref, *rest):
        if cast_a:
            a_sc = rest[-1]

            @pl.when(pl.program_id(1) == 0)
            def _():
                a_sc[...] = a_ref[...].astype(BF16)
            av = a_sc[...]
        else:
            av = a_ref[...]
        acc = _dot(av, b_ref[...].astype(BF16), dims)
        vals = epi(acc, *[e[...] for e in rest[:ne]]) if epi is not None else (acc,)
        for o_ref, v in zip(rest[ne:ne + no], vals):
            o_ref[...] = v.astype(o_ref.dtype)

    res = pl.pallas_call(
        body, grid=(m // tm, n // tn),
        in_specs=[a_spec, b_spec] + e_specs,
        out_specs=[pl.BlockSpec((tm, tn), lambda i, j: (i, j)) for _ in outs],
        out_shape=[jax.ShapeDtypeStruct((m, n), dt) for dt in outs],
        scratch_shapes=[pltpu.VMEM(a_spec.block_shape, BF16)] if cast_a else [],
        name=name,
    )(a, b, *epi_args)
    return res[0] if no == 1 else res


def _ew(fn, tiled, bcast, outs, reds=(), *, tm=256, name):
    secs = [(t, 0, t.shape[1]) if not isinstance(t, tuple) else t for t in tiled]
    m = secs[0][0].shape[0]
    tm = min(tm, m)
    assert m % tm == 0
    in_specs = []
    for arr, off, w in secs:
        assert off % w == 0
        in_specs.append(pl.BlockSpec((tm, w), functools.partial(lambda i, c: (i, c), c=off // w)))
    in_specs += [pl.BlockSpec(b.shape, lambda i: (0, 0)) for b in bcast]
    nin, no = len(in_specs), len(outs)

    def body(*refs):
        vals = fn(*[r[...] for r in refs[:nin]])
        for r, v in zip(refs[nin:nin + no], vals[:no]):
            r[...] = v.astype(r.dtype)
        i = pl.program_id(0)
        for r, v in zip(refs[nin + no:], vals[no:]):
            @pl.when(i == 0)
            def _():
                r[...] = v

            @pl.when(i > 0)
            def _():
                r[...] += v

    res = pl.pallas_call(
        body, grid=(m // tm,), in_specs=in_specs,
        out_specs=[pl.BlockSpec((tm, c), lambda i: (i, 0)) for c, _ in outs] + [pl.BlockSpec(s, lambda i: (0, 0)) for s in reds],
        out_shape=[jax.ShapeDtypeStruct((m, c), dt) for c, dt in outs] + [jax.ShapeDtypeStruct(s, F32) for s in reds],
        name=name,
    )(*[s[0] for s in secs], *bcast)
    return res


def _cb(fn, cols, params, outs, pouts, *, tc, nblk, name):
    t = cols[0][0].shape[0]
    in_specs = []
    for arr, off in cols:
        assert off % tc == 0
        in_specs.append(pl.BlockSpec((t, tc), functools.partial(lambda c, o: (0, o + c), o=off // tc)))
    for arr, off in params:
        in_specs.append(pl.BlockSpec((arr.shape[0], tc), functools.partial(lambda c, o: (0, o + c), o=off // tc)))
    nin, no = len(in_specs), len(outs)

    def body(*refs):
        vals = fn(*[r[...] for r in refs[:nin]])
        for r, v in zip(refs[nin:], vals):
            r[...] = v.astype(r.dtype)

    return pl.pallas_call(
        body, grid=(nblk,), in_specs=in_specs,
        out_specs=[pl.BlockSpec((t, tc), lambda c: (0, c)) for _ in outs] + [pl.BlockSpec((k, tc), lambda c: (0, c)) for k in pouts],
        out_shape=[jax.ShapeDtypeStruct((t, nblk * tc), dt) for dt in outs] + [jax.ShapeDtypeStruct((k, nblk * tc), F32) for k in pouts],
        name=name,
    )(*[c[0] for c in cols], *[p[0] for p in params])


CUM_BLK = 256


def _prep_point(s, bias, alog):
    lane = lax.broadcasted_iota(jnp.int32, s.shape, 1)
    z = s + bias
    return jnp.where(lane < 8, _log_sigmoid(z),
                     jnp.where(lane < 12, _sigmoid(s),
                               jnp.where(lane < 16, -jnp.exp(alog) * _softplus(z), 0.0)))


def _tri(n, upper):
    r = lax.broadcasted_iota(jnp.int32, (n, n), 0)
    c = lax.broadcasted_iota(jnp.int32, (n, n), 1)
    return (r <= c if upper else r >= c).astype(F32)


def _prep_fwd(proj, bias_row, alog_row, name):
    t = proj.shape[0]
    nb = t // CUM_BLK

    def body(s_ref, b_ref, a_ref, o_ref):
        pre = _prep_point(s_ref[...], b_ref[...], a_ref[...])
        lane = lax.broadcasted_iota(jnp.int32, (CUM_BLK, 128), 1)
        tri = _tri(CUM_BLK, False)
        carry = jnp.zeros((1, 128), F32)
        for blk in range(nb):
            xb = pre[blk * CUM_BLK:(blk + 1) * CUM_BLK]
            cb = _dot(tri, xb, NN, HI) + carry
            carry = cb[CUM_BLK - 1:CUM_BLK]
            o_ref[blk * CUM_BLK:(blk + 1) * CUM_BLK, :] = jnp.where(lane < 8, cb, xb)

    return pl.pallas_call(
        body, grid=(1,),
        in_specs=[pl.BlockSpec((t, 128), lambda i: (0, C_SMALL // 128)), pl.BlockSpec((1, 128), lambda i: (0, 0)),
                  pl.BlockSpec((1, 128), lambda i: (0, 0))],
        out_specs=pl.BlockSpec((t, 128), lambda i: (0, 0)),
        out_shape=jax.ShapeDtypeStruct((t, 128), F32), name=name,
    )(proj, bias_row, alog_row)


def _prep_bwd(proj, bias_row, alog_row, daux, name):
    t = proj.shape[0]
    nb = t // CUM_BLK

    def body(s_ref, b_ref, a_ref, d_ref, ds_ref, db_ref, da_ref, dpre_sc):
        lane = lax.broadcasted_iota(jnp.int32, (CUM_BLK, 128), 1)
        tri = _tri(CUM_BLK, True)
        carry = jnp.zeros((1, 128), F32)
        for blk in reversed(range(nb)):
            db = d_ref[blk * CUM_BLK:(blk + 1) * CUM_BLK, :]
            cb = _dot(tri, db, NN, HI) + carry
            carry = cb[0:1]
            dpre_sc[blk * CUM_BLK:(blk + 1) * CUM_BLK, :] = jnp.where(lane < 8, cb, db)
        _, vjp = jax.vjp(_prep_point, s_ref[...], b_ref[...], a_ref[...])
        ds, dbias, dalog = vjp(dpre_sc[...])
        ds_ref[...] = ds.astype(ds_ref.dtype)
        db_ref[...] = dbias
        da_ref[...] = dalog

    return pl.pallas_call(
        body, grid=(1,),
        in_specs=[pl.BlockSpec((t, 128), lambda i: (0, C_SMALL // 128)), pl.BlockSpec((1, 128), lambda i: (0, 0)),
                  pl.BlockSpec((1, 128), lambda i: (0, 0)), pl.BlockSpec((t, 128), lambda i: (0, 0))],
        out_specs=[pl.BlockSpec((t, 128), lambda i: (0, 0)), pl.BlockSpec((1, 128), lambda i: (0, 0)),
                   pl.BlockSpec((1, 128), lambda i: (0, 0))],
        out_shape=[jax.ShapeDtypeStruct((t, 128), BF16), jax.ShapeDtypeStruct((1, 128), F32), jax.ShapeDtypeStruct((1, 128), F32)],
        scratch_shapes=[pltpu.VMEM((t, 128), F32)], name=name,
    )(proj, bias_row, alog_row, daux)


ATT_TQ = 256
FOX_SCALE = FOX_DH ** -0.5


def _qnorm(q, g):
    return _rms(q, g) * FOX_SCALE


def _att_scores(qn_blk, kn, cfc_blk, cfr, qi, tq, kend):
    s = _dot(qn_blk.astype(BF16), kn[:kend].astype(BF16), NT) + cfc_blk - cfr[:, :kend]
    row = lax.broadcasted_iota(jnp.int32, (tq, kend), 0) + qi * tq
    col = lax.broadcasted_iota(jnp.int32, (tq, kend), 1)
    return s, row >= col


ATT_PAIR = 128 // FOX_DH


def _att_specs(t):
    pair = lambda off: pl.BlockSpec((t, 128), functools.partial(lambda i, o: (0, o + i), o=off // 128))
    gain = pl.BlockSpec((1, FOX_DH), lambda i: (0, 0))
    hd = lambda i: (i, 0, 0)
    col, row = pl.BlockSpec((ATT_PAIR, t, 1), hd), pl.BlockSpec((ATT_PAIR, 1, t), hd)
    return [pair(C_FQ), pair(C_FK), pair(C_FV), gain, gain, col, row], pl.BlockSpec((t, 128), lambda i: (0, i)), col, row


def _att_fwd(proj, qg, kg, cfc, cfr, name):
    t = proj.shape[0]
    tq = min(ATT_TQ, t)
    in_specs, pair_out, col, _ = _att_specs(t)

    def body(q_ref, k_ref, v_ref, qg_ref, kg_ref, cfc_ref, cfr_ref, o_ref, lse_ref):
        for e in range(ATT_PAIR):
            lanes = slice(e * FOX_DH, (e + 1) * FOX_DH)
            qn = _qnorm(q_ref[:, lanes], qg_ref[...])
            kn = _rms(k_ref[:, lanes], kg_ref[...])
            v = v_ref[:, lanes].astype(BF16)
            cfr = cfr_ref[e]
            for qi in range(t // tq):
                kend = (qi + 1) * tq
                rows = slice(qi * tq, kend)
                s, mask = _att_scores(qn[rows], kn, cfc_ref[e, rows, :], cfr, qi, tq, kend)
                s = jnp.where(mask, s, NEG)
                m = jnp.max(s, axis=1, keepdims=True)
                p = jnp.exp(s - m)
                l = jnp.sum(p, axis=1, keepdims=True)
                o_ref[rows, lanes] = _dot(p.astype(BF16), v[:kend], NN) / l
                lse_ref[e, rows, :] = m + jnp.log(l)

    return pl.pallas_call(
        body, grid=(FOX_HEADS // ATT_PAIR,), in_specs=in_specs, out_specs=[pair_out, col],
        out_shape=[jax.ShapeDtypeStruct((t, BW), F32), jax.ShapeDtypeStruct((FOX_HEADS, t, 1), F32)], name=name,
    )(proj, proj, proj, qg, kg, cfc, cfr)


def _att_bwd(proj, qg, kg, cfc, cfr, lse, o, do, name):
    t = proj.shape[0]
    tq = min(ATT_TQ, t)
    in_specs, pair_out, col, row = _att_specs(t)

    def body(q_ref, k_ref, v_ref, qg_ref, kg_ref, cfc_ref, cfr_ref, lse_ref, o_ref, do_ref,
             dq_ref, dk_ref, dv_ref, dcfc_ref, dcfr_ref, dqg_ref, dkg_ref, dqn_sc, dkn_sc, dv_sc, dcfr_sc):
        for e in range(ATT_PAIR):
            lanes = slice(e * FOX_DH, (e + 1) * FOX_DH)
            qn, vjp_q = jax.vjp(_qnorm, q_ref[:, lanes], qg_ref[...])
            kn, vjp_k = jax.vjp(_rms, k_ref[:, lanes], kg_ref[...])
            v = v_ref[:, lanes].astype(BF16)
            cfr = cfr_ref[e]
            do_e = do_ref[:, lanes]
            delta = jnp.sum(do_e * o_ref[:, lanes], axis=1, keepdims=True)
            dkn_sc[...] = jnp.zeros_like(dkn_sc)
            dv_sc[...] = jnp.zeros_like(dv_sc)
            dcfr_sc[...] = jnp.zeros_like(dcfr_sc)
            for qi in range(t // tq):
                kend = (qi + 1) * tq
                rows = slice(qi * tq, kend)
                s, mask = _att_scores(qn[rows], kn, cfc_ref[e, rows, :], cfr, qi, tq, kend)
                p = jnp.where(mask, jnp.exp(jnp.where(mask, s, NEG) - lse_ref[e, rows, :]), 0.0)
                do_b = do_e[rows].astype(BF16)
                dv_sc[0:kend, :] += _dot(p.astype(BF16), do_b, TN)
                dp = _dot(do_b, v[:kend], NT)
                ds = p * (dp - delta[rows])
                ds_b = ds.astype(BF16)
                dqn_sc[rows, :] = _dot(ds_b, kn[:kend].astype(BF16), NN)
                dkn_sc[0:kend, :] += _dot(ds_b, qn[rows].astype(BF16), TN)
                dcfc_ref[e, rows, :] = jnp.sum(ds, axis=1, keepdims=True)
                dcfr_sc[:, 0:kend] -= jnp.sum(ds, axis=0, keepdims=True)
            dq, dqg = vjp_q(dqn_sc[...])
            dk, dkg = vjp_k(dkn_sc[...])
            dq_ref[:, lanes] = dq.astype(dq_ref.dtype)
            dk_ref[:, lanes] = dk.astype(dk_ref.dtype)
            dv_ref[:, lanes] = dv_sc[...].astype(dv_ref.dtype)
            dcfr_ref[e] = dcfr_sc[...]
            dqg_ref[e] = dqg
            dkg_ref[e] = dkg

    gsp = pl.BlockSpec((ATT_PAIR, 1, FOX_DH), lambda i: (i, 0, 0))
    return pl.pallas_call(
        body, grid=(FOX_HEADS // ATT_PAIR,),
        in_specs=in_specs + [col, pair_out, pair_out],
        out_specs=[pair_out] * 3 + [col, row, gsp, gsp],
        out_shape=[jax.ShapeDtypeStruct((t, BW), BF16)] * 3
        + [jax.ShapeDtypeStruct((FOX_HEADS, t, 1), F32), jax.ShapeDtypeStruct((FOX_HEADS, 1, t), F32)]
        + [jax.ShapeDtypeStruct((FOX_HEADS, 1, FOX_DH), F32)] * 2,
        scratch_shapes=[pltpu.VMEM((t, FOX_DH), F32)] * 3 + [pltpu.VMEM((1, t), F32)], name=name,
    )(proj, proj, proj, qg, kg, cfc, cfr, lse, o, do)


DN_SCALE = DN_DH ** -0.5


DN_BATCH = 8


@functools.partial(jax.custom_vjp, nondiff_argnums=(2, 3))
def _mm3(a, b, dims, batch=False):
    return _mm3_passes(a, b, dims, batch)


def _mm3_fwd(a, b, dims, batch):
    return _mm3_passes(a, b, dims, batch), (a, b)


def _mm3_bwd(dims, batch, res, dc):
    a, b = res
    if dims == NN:
        return _mm3_passes(dc, b, NT, batch), _mm3_passes(a, dc, TN, batch)
    if dims == NT:
        return _mm3_passes(dc, b, NN, batch), _mm3_passes(dc, a, TN, batch)
    return _mm3_passes(b, dc, NT, batch), _mm3_passes(a, dc, NN, batch)


_mm3.defvjp(_mm3_fwd, _mm3_bwd)


def _mm3_passes(a, b, dims, batch):
    if batch:
        dn = (((dims[0][0] + 1,), (dims[1][0] + 1,)), ((0,), (0,)))
        dot = lambda p, q: lax.dot_general(p, q, dn, preferred_element_type=F32)
    else:
        dot = lambda p, q: _dot(p, q, dims)
    ah, bh = a.astype(BF16), b.astype(BF16)
    al, bl = (a - ah.astype(F32)).astype(BF16), (b - bh.astype(F32)).astype(BF16)
    return dot(ah, bh) + (dot(ah, bl) + dot(al, bh))


def _dn_local(qc, kc, vc, g, beta):
    nb, c, _ = qc.shape
    ii = lax.broadcasted_iota(jnp.int32, (c, c), 0)
    jj = lax.broadcasted_iota(jnp.int32, (c, c), 1)
    incl, strict = ii >= jj, ii > jj
    lower = jnp.broadcast_to(incl.astype(F32), (nb, c, c))
    eye = (ii == jj).astype(F32)
    mm = functools.partial(_mm3, batch=True)
    dm = mm(lower, jnp.where(strict, g, 0.0), NN)
    decay = jnp.where(incl, jnp.exp(jnp.where(incl, dm, 0.0)), 0.0)
    gcum = mm(lower, g * jnp.ones((1, 1, DN_DH), F32), NN)
    eg = jnp.exp(gcum)
    glast = gcum[:, c - 1:c]
    kb = kc * beta
    n1 = jnp.where(strict, mm(kb, kc, NT) * decay, 0.0)
    inv = eye - n1
    pw = n1
    for _ in range(5):
        pw = mm(pw, pw, NN)
        inv = inv + mm(pw, inv, NN)
    sol = mm(inv, jnp.concatenate([vc * beta, kb * eg], axis=2), NN)
    qk = jnp.where(incl, mm(qc, kc, NT) * decay, 0.0)
    return sol[:, :, :DN_DH], sol[:, :, DN_DH:], qk, qc * eg, kc * jnp.exp(glast - gcum), jnp.exp(glast)


@functools.partial(jax.custom_vjp, nondiff_argnums=(2,))
def _mm1(a, b, dims):
    return _dot(a.astype(BF16), b.astype(BF16), dims)


def _mm1_fwd(a, b, dims):
    return _mm1(a, b, dims), (a, b)


def _mm1_bwd(dims, res, dc):
    a, b = res
    if dims == NN:
        return _mm1(dc, b, NT), _mm1(a, dc, TN)
    if dims == NT:
        return _mm1(dc, b, NN), _mm1(dc, a, TN)
    return _mm1(b, dc, NT), _mm1(a, dc, NN)


_mm1.defvjp(_mm1_fwd, _mm1_bwd)


def _dn_state(u, kcum, qk, qdec, kdec, egl, state):
    v_new = u - _mm1(kcum, state, NN)
    out = _mm1(qdec, state, NN) + _mm1(qk, v_new, NN)
    return out, state * egl + _mm1(kdec, v_new, TN)


def _dn_pre_q(c):
    return _l2(_silu(c)) * DN_SCALE


def _dn_pre_k(c):
    return _l2(_silu(c))


def _dn_post(o, z, ng):
    return _rms(o, ng) * _silu(z)


def _dn_specs(t):
    cblk = lambda o: pl.BlockSpec((t, DN_DH), functools.partial(lambda h, o: (0, o + h), o=o // DN_DH))
    wblk = lambda o: pl.BlockSpec((4, DN_DH), functools.partial(lambda h, o: (0, o + h), o=o // DN_DH))
    proj_specs = [cblk(C_DQ), cblk(C_DK), cblk(C_DV), cblk(C_DZ)]
    w_specs = [wblk(0), wblk(BW), wblk(2 * BW)]
    gb_spec = pl.BlockSpec((1, t, 2), lambda h: (h, 0, 0))
    return proj_specs, w_specs, gb_spec


def _chunk_rows(n, count=1):
    return pl.ds(pl.multiple_of(n * DN_CHUNK, DN_CHUNK), count * DN_CHUNK)


def _egl_rows(n, count=1):
    return pl.ds(pl.multiple_of(n * 8, 8), count * 8)


def _dn_local_inputs(n, qn_sc, kn_sc, vv_sc, gb_ref):
    r = _chunk_rows(n, DN_BATCH)
    split = lambda v: v.reshape(DN_BATCH, DN_CHUNK, v.shape[-1])
    gbv = split(gb_ref[0, r, :])
    return split(qn_sc[r, :]), split(kn_sc[r, :]), split(vv_sc[r, :]), gbv[:, :, 0:1], gbv[:, :, 1:2]


def _dn_local_phase(nc, qn_sc, kn_sc, vv_sc, gb_ref, loc):
    def step(i, carry):
        n = i * DN_BATCH
        vals = _dn_local(*_dn_local_inputs(n, qn_sc, kn_sc, vv_sc, gb_ref))
        for sc, val in zip(loc[:5], vals[:5]):
            sc[_chunk_rows(n, DN_BATCH), :] = val.reshape(DN_BATCH * DN_CHUNK, val.shape[-1])
        loc[5][_egl_rows(n, DN_BATCH), :] = jnp.broadcast_to(vals[5], (DN_BATCH, 8, DN_DH)).reshape(DN_BATCH * 8, DN_DH)
        return carry

    lax.fori_loop(0, nc // DN_BATCH, step, 0)


def _dn_loc_scratch(t, nc):
    big = pltpu.VMEM((t, DN_DH), F32)
    return [big, big, pltpu.VMEM((t, DN_CHUNK), F32), big, big, pltpu.VMEM((nc * 8, DN_DH), F32)]


def _dn_loc_io(t, nc, **spec_args):
    head_cols = pl.BlockSpec((t, DN_DH), lambda h: (0, h), **spec_args)
    per_head = lambda rows, cols: pl.BlockSpec((1, rows, cols), lambda h: (h, 0, 0), **spec_args)
    wide = jax.ShapeDtypeStruct((t, BW), F32)
    shapes = [wide, wide, jax.ShapeDtypeStruct((DN_HEADS, t, DN_CHUNK), F32), wide, wide,
              jax.ShapeDtypeStruct((DN_HEADS, nc * 8, DN_DH), F32)]
    return shapes, [head_cols, head_cols, per_head(t, DN_CHUNK), head_cols, head_cols, per_head(nc * 8, DN_DH)]


def _dn_loc_refs(refs):
    return refs[0], refs[1], refs[2].at[0], refs[3], refs[4], refs[5].at[0]


def _dn_fwd(proj, conv_w, gb, ng, name):
    t = proj.shape[0]
    nc = t // DN_CHUNK
    assert nc % DN_BATCH == 0
    proj_specs, w_specs, gb_spec = _dn_specs(t)
    loc_shapes, loc_specs = _dn_loc_io(t, nc)

    def body(q_ref, k_ref, v_ref, z_ref, wq_ref, wk_ref, wv_ref, gb_ref, ng_ref, y_ref, o_ref, st_ref, *rest):
        loc = _dn_loc_refs(rest[:6])
        qn_sc, kn_sc, vv_sc = rest[6:]
        qn_sc[...] = _dn_pre_q(_conv(q_ref[...], wq_ref[...]))
        kn_sc[...] = _dn_pre_k(_conv(k_ref[...], wk_ref[...]))
        vv_sc[...] = _silu(_conv(v_ref[...], wv_ref[...]))
        _dn_local_phase(nc, qn_sc, kn_sc, vv_sc, gb_ref, loc)
        u_sc, kcum_sc, qk_sc, qdec_sc, kdec_sc, egl_sc = loc

        def chunk(n, state):
            r = _chunk_rows(n)
            egl = egl_sc[_egl_rows(n), :][0:1]
            out, new_state = _dn_state(u_sc[r, :], kcum_sc[r, :], qk_sc[r, :], qdec_sc[r, :], kdec_sc[r, :], egl, state)
            st_ref[0, n] = state
            o_ref[r, :] = out
            return new_state

        lax.fori_loop(0, nc, chunk, jnp.zeros((DN_DH, DN_DH), F32))
        y_ref[...] = _dn_post(o_ref[...], z_ref[...], ng_ref[...])

    hblk = pl.BlockSpec((t, DN_DH), lambda h: (0, h))
    return pl.pallas_call(
        body, grid=(DN_HEADS,),
        in_specs=proj_specs + w_specs + [gb_spec, pl.BlockSpec((1, DN_DH), lambda h: (0, 0))],
        out_specs=[hblk, hblk, pl.BlockSpec((1, nc, DN_DH, DN_DH), lambda h: (h, 0, 0, 0))] + loc_specs,
        out_shape=[jax.ShapeDtypeStruct((t, BW), F32), jax.ShapeDtypeStruct((t, BW), F32),
                   jax.ShapeDtypeStruct((DN_HEADS, nc, DN_DH, DN_DH), F32)] + loc_shapes,
        scratch_shapes=[pltpu.VMEM((t, DN_DH), F32)] * 3, name=name,
    )(proj, proj, proj, proj, conv_w, conv_w, conv_w, gb, ng)


def _dn_bwd(proj, conv_w, gb, ng, o, states, loc_saved, dy, name):
    t = proj.shape[0]
    nc = t // DN_CHUNK
    proj_specs, w_specs, gb_spec = _dn_specs(t)
    _, loc_specs = _dn_loc_io(t, nc, pipeline_mode=pl.Buffered(1))

    def body(q_ref, k_ref, v_ref, z_ref, wq_ref, wk_ref, wv_ref, gb_ref, ng_ref, o_ref, st_ref, dy_ref,
             u_ref, kcum_ref, qk_ref, qdec_ref, kdec_ref, egl_ref,
             dq_ref, dk_ref, dv_ref, dz_ref, dwq_ref, dwk_ref, dwv_ref, dgb_ref, dng_ref,
             qn_sc, kn_sc, vv_sc, do_sc, *dloc):
        loc = _dn_loc_refs((u_ref, kcum_ref, qk_ref, qdec_ref, kdec_ref, egl_ref))
        qn_sc[...] = _dn_pre_q(_conv(q_ref[...], wq_ref[...]))
        kn_sc[...] = _dn_pre_k(_conv(k_ref[...], wk_ref[...]))
        vv_sc[...] = _silu(_conv(v_ref[...], wv_ref[...]))
        _, vjp_y = jax.vjp(_dn_post, o_ref[...], z_ref[...], ng_ref[...])
        do, dz, dng = vjp_y(dy_ref[...])
        do_sc[...] = do
        dz_ref[...] = dz.astype(dz_ref.dtype)
        dng_ref[0] = dng
        u_sc, kcum_sc, qk_sc, qdec_sc, kdec_sc, egl_sc = loc

        def state_bwd(i, dstate):
            n = nc - 1 - i
            r = _chunk_rows(n)
            r8 = _egl_rows(n)
            _, vjp = jax.vjp(_dn_state, u_sc[r, :], kcum_sc[r, :], qk_sc[r, :], qdec_sc[r, :], kdec_sc[r, :],
                             egl_sc[r8, :][0:1], st_ref[0, n])
            du, dkcum, dqk, dqdec, dkdec, degl, dprev = vjp((do_sc[r, :], dstate))
            for d_sc, val in zip(dloc[:5], (du, dkcum, dqk, dqdec, dkdec)):
                d_sc[r, :] = val
            dloc[5][r8, :] = jnp.broadcast_to(degl, (8, DN_DH))
            return dprev

        lax.fori_loop(0, nc, state_bwd, jnp.zeros((DN_DH, DN_DH), F32))

        def local_bwd(i, carry):
            n = i * DN_BATCH
            r = _chunk_rows(n, DN_BATCH)
            _, vjp = jax.vjp(_dn_local, *_dn_local_inputs(n, qn_sc, kn_sc, vv_sc, gb_ref))
            cts = tuple(d_sc[r, :].reshape(DN_BATCH, DN_CHUNK, d_sc.shape[-1]) for d_sc in dloc[:5])
            cts += (dloc[5][_egl_rows(n, DN_BATCH), :].reshape(DN_BATCH, 8, DN_DH)[:, 0:1],)
            dqc, dkc, dvc, dg, dbeta = vjp(cts)
            for d_sc, val in zip((dloc[0], dloc[1], dloc[3]), (dqc, dkc, dvc)):
                d_sc[r, :] = val.reshape(DN_BATCH * DN_CHUNK, DN_DH)
            dgb_ref[0, r, :] = jnp.concatenate([dg, dbeta], axis=2).reshape(DN_BATCH * DN_CHUNK, 2)
            return carry

        lax.fori_loop(0, nc // DN_BATCH, local_bwd, 0)
        for x_ref, w_ref, pre, d_sc, dx_ref, dw_ref in ((q_ref, wq_ref, _dn_pre_q, dloc[0], dq_ref, dwq_ref),
                                                       (k_ref, wk_ref, _dn_pre_k, dloc[1], dk_ref, dwk_ref),
                                                       (v_ref, wv_ref, _silu, dloc[3], dv_ref, dwv_ref)):
            _, vjp = jax.vjp(pre, _conv(x_ref[...], w_ref[...]))
            (dc,) = vjp(d_sc[...])
            dx, dws = _conv_bwd(x_ref[...], w_ref[...], dc)
            dx_ref[...] = dx.astype(dx_ref.dtype)
            for j, dw in enumerate(dws):
                dw_ref[j:j + 1, :] = dw

    hblk = pl.BlockSpec((t, DN_DH), lambda h: (0, h))
    wout = pl.BlockSpec((4, DN_DH), lambda h: (0, h))
    return pl.pallas_call(
        body, grid=(DN_HEADS,),
        in_specs=proj_specs + w_specs + [gb_spec, pl.BlockSpec((1, DN_DH), lambda h: (0, 0)), hblk,
                                         pl.BlockSpec((1, nc, DN_DH, DN_DH), lambda h: (h, 0, 0, 0)), hblk] + loc_specs,
        out_specs=[hblk] * 4 + [wout] * 3 + [gb_spec, pl.BlockSpec((1, 1, DN_DH), lambda h: (h, 0, 0))],
        out_shape=[jax.ShapeDtypeStruct((t, BW), BF16)] * 4 + [jax.ShapeDtypeStruct((4, BW), F32)] * 3
        + [jax.ShapeDtypeStruct((DN_HEADS, t, 2), F32), jax.ShapeDtypeStruct((DN_HEADS, 1, DN_DH), F32)],
        scratch_shapes=[pltpu.VMEM((t, DN_DH), F32)] * 4 + _dn_loc_scratch(t, nc), name=name,
    )(proj, proj, proj, proj, conv_w, conv_w, conv_w, gb, ng, o, states, dy, *loc_saved)


MERGE_TM, MERGE_TN = 512, 512


def _merge_specs(t):
    tm, tn = min(MERGE_TM, t), MERGE_TN
    y_spec = pl.BlockSpec((tm, BW), lambda i, j: (i, 0))
    w_spec = pl.BlockSpec((3, BW, tn), lambda i, j: (0, 0, j))
    gate_specs = [pl.BlockSpec((tm, tn), functools.partial(lambda i, j, o: (i, o + j), o=(C_GATE + n * D_MODEL) // tn))
                  for n in range(3)]
    return tm, tn, [y_spec] * 3 + [w_spec] + gate_specs


def _merge_fwd(ys, wb, proj, name):
    t = proj.shape[0]
    tm, tn, in_specs = _merge_specs(t)

    def body(y0, y1, y2, w_ref, g0, g1, g2, o_ref):
        acc = jnp.zeros((tm, tn), F32)
        for n, (y, g) in enumerate(((y0, g0), (y1, g1), (y2, g2))):
            acc = acc + _dot(y[...].astype(BF16), w_ref[n], NN) * _sigmoid(g[...])
        o_ref[...] = acc.astype(o_ref.dtype)

    return pl.pallas_call(
        body, grid=(t // tm, D_MODEL // tn), in_specs=in_specs,
        out_specs=pl.BlockSpec((tm, tn), lambda i, j: (i, j)),
        out_shape=jax.ShapeDtypeStruct((t, D_MODEL), BF16), name=name,
    )(*ys, wb, proj, proj, proj)


def _merge_bwd(ys, wb, proj, dmerged, name):
    t = proj.shape[0]
    tm, tn, in_specs = _merge_specs(t)

    def body(y0, y1, y2, w_ref, g0, g1, g2, dm_ref, dg_ref, dt_ref):
        dm = dm_ref[...]
        for n, (y, g) in enumerate(((y0, g0), (y1, g1), (y2, g2))):
            tn_ = _dot(y[...].astype(BF16), w_ref[n], NN)
            sg = _sigmoid(g[...])
            dg_ref[n] = (dm * tn_ * sg * (1.0 - sg)).astype(dg_ref.dtype)
            dt_ref[n] = (dm * sg).astype(dt_ref.dtype)

    o3 = pl.BlockSpec((3, tm, tn), lambda i, j: (0, i, j))
    return pl.pallas_call(
        body, grid=(t // tm, D_MODEL // tn), in_specs=in_specs + [pl.BlockSpec((tm, tn), lambda i, j: (i, j))],
        out_specs=[o3, o3], out_shape=[jax.ShapeDtypeStruct((3, t, D_MODEL), BF16)] * 2, name=name,
    )(*ys, wb, proj, proj, proj, dmerged)


def _prep_rows(sp):
    z4, z112 = jnp.zeros((1, 4), F32), jnp.zeros((1, 112), F32)
    bias_row = jnp.concatenate([sp['b_fox_f'], z4, sp['dn_dt_bias'], z112], axis=1)
    alog_row = jnp.concatenate([jnp.zeros((1, 12), F32), sp['dn_a_log'], z112], axis=1)
    return bias_row, alog_row


def _sc_fwd_tile(sb, sc, sv, w):
    return (sb * _conv(sc * sv, w),)


def _ffn_act(ug, uv):
    return _silu(ug) * uv


W_IN_COLS = [(C_FQ, 0, 1536), (C_SB, 1544, 1536), (C_DQ, 3080, 1536), (C_DZ, 4624, 512), (C_GATE, 5136, 3072),
             (C_SMALL, 1536, 8), (C_SMALL + 8, 4616, 8)]
W_IN_SHARD = IN_ORIG // N_DEV


def _w_in_segments():
    out = []
    for d, o, w in W_IN_COLS:
        end = o + w
        while o < end:
            k = o // W_IN_SHARD
            n = min(end, (k + 1) * W_IN_SHARD) - o
            out.append((d, k, o - k * W_IN_SHARD, n))
            o, d = o + n, d + n
    return out


def _w_in_assemble(g, name):
    tm = 256

    def body(g_ref, o_ref):
        for d, k, s, n in _w_in_segments():
            o_ref[:, d:d + n] = g_ref[k, :, s:s + n]
        o_ref[:, IN_ORIG:IN_P] = jnp.zeros((tm, IN_P - IN_ORIG), o_ref.dtype)

    return pl.pallas_call(
        body, grid=(D_MODEL // tm,),
        in_specs=[pl.BlockSpec((N_DEV, tm, W_IN_SHARD), lambda i: (0, i, 0))],
        out_specs=pl.BlockSpec((tm, IN_P), lambda i: (i, 0)),
        out_shape=jax.ShapeDtypeStruct((D_MODEL, IN_P), g.dtype), name=name,
    )(g)


W_IN_SPLIT_TM = 128


def _w_in_split_by_device(g, name):
    tm = W_IN_SPLIT_TM

    def body(g_ref, own_ref, send_ref):
        me = 4 * lax.axis_index("x") + 2 * lax.axis_index("y") + lax.axis_index("c")
        for d, k, s, n in _w_in_segments():
            val = g_ref[:, d:d + n]
            send_ref[k, :, s:s + n] = val.astype(send_ref.dtype)

            @pl.when(me == k)
            def _():
                own_ref[:, s:s + n] = val

    return pl.pallas_call(
        body, grid=(D_MODEL // tm,), in_specs=[pl.BlockSpec((tm, IN_P), lambda i: (i, 0))],
        out_specs=[pl.BlockSpec((tm, W_IN_SHARD), lambda i: (i, 0)), pl.BlockSpec((N_DEV, tm, W_IN_SHARD), lambda i: (0, i, 0))],
        out_shape=[jax.ShapeDtypeStruct((D_MODEL, W_IN_SHARD), F32), jax.ShapeDtypeStruct((N_DEV, D_MODEL, W_IN_SHARD), BF16)],
        name=name,
    )(g)


def _layer_fwd(x, p_i, w, sp, tag, late_weights=None):
    t = x.shape[0]
    sv = {'x': x}
    (hn,) = _ew(lambda a, g: (_rms(a, g),), [x], [sp['g_mix']], [(D_MODEL, BF16)], name=f'rms_mix_{tag}')
    proj = _mm(hn, w['w_in'], 'nn', [F32], name=f'in_proj_{tag}')
    bias_row, alog_row = _prep_rows(sp)
    aux = _prep_fwd(proj, bias_row, alog_row, f'prep_{tag}')
    cf = aux[:, :FOX_HEADS].T
    cfc, cfr = cf[:, :, None], cf[:, None, :]
    y_fox, lse = _att_fwd(proj, sp['fox_q_gain'], sp['fox_k_gain'], cfc, cfr, f'fox_fwd_{tag}')
    (y_sc,) = _cb(_sc_fwd_tile, [(proj, C_SB), (proj, C_SC), (proj, C_SV)], [(w['sc_conv_w'], 0)], [F32], [],
                  tc=256, nblk=2, name=f'sc_fwd_{tag}')
    gb = jnp.stack([aux[:, 12:16].T, aux[:, 8:12].T], axis=-1)
    y_dn, o_dn, states, *dn_loc = _dn_fwd(proj, w['dn_conv_w'], gb, sp['dn_norm_gain'], f'dn_fwd_{tag}')
    ys = (y_fox, y_sc, y_dn)
    if late_weights is not None:
        w = {**w, **late_weights(y_dn)}
    merged = _merge_fwd(ys, w['w_branch'], proj, f'merge_fwd_{tag}')
    x1 = _mm(merged, w['w_o'], 'nn', [F32], epi=lambda acc, r: (acc + r,), epi_args=(x,), name=f'o_proj_{tag}')
    (hf,) = _ew(lambda a, g: (_rms(a, g),), [x1], [sp['g_ffn']], [(D_MODEL, BF16)], name=f'rms_ffn_{tag}')
    up = _mm(hf, w['w_up'], 'nt', [F32], name=f'up_proj_{tag}')
    (act,) = _cb(lambda ug, uv, wg, wv: (_ffn_act(_conv(ug, wg), _conv(uv, wv)),), [(up, 0), (up, D_FF)],
                 [(w['ffn_conv_w'], 0), (w['ffn_conv_w'], D_FF)], [BF16], [], tc=256, nblk=D_FF // 256, name=f'ffn_act_{tag}')
    x2 = _mm(act, w['w_down'], 'nn', [F32], epi=lambda acc, r: (acc + r,), epi_args=(x1,), name=f'down_proj_{tag}')
    (hp,) = _ew(lambda a, g: (_rms(a, g),), [x2], [sp['g_ple']], [(D_MODEL, BF16)], name=f'rms_ple_{tag}')
    gp = _mm(hp, w['w_ple_gate'], 'nn', [F32], name=f'ple_gate_{tag}')
    x3 = _mm(p_i, w['w_ple'], 'nn', [F32], epi=lambda acc, g, r: (r + _sigmoid(g) * acc,), epi_args=(gp, x2), name=f'ple_{tag}')
    sv.update(hn=hn, proj=proj, aux=aux, cfc=cfc, cfr=cfr, lse=lse, ys=ys, gb=gb, o_dn=o_dn,
              states=states, merged=merged, x1=x1, hf=hf, up=up, act=act, x2=x2, hp=hp, gp=gp, p=p_i,
              bias_row=bias_row, alog_row=alog_row, w=w, dn_loc=dn_loc)
    return x3, sv


def _rms_bwd(x, g, dh, dres, name):
    def fn(xv, dhv, dr, gv):
        _, vjp = jax.vjp(_rms, xv, gv)
        dx, dg = vjp(dhv)
        return dr + dx, dg
    return _ew(fn, [x, dh, dres], [g], [(D_MODEL, F32)], [(1, D_MODEL)], name=name)


def _layer_bwd(dx3, sv, sp, tag, early_grads=None, last_grad=None):
    t = dx3.shape[0]
    w = sv['w']
    g = {}
    def ple_epi(acc, gpv, d):
        s = _sigmoid(gpv)
        return d * acc * s * (1.0 - s), d * s
    dgp, de = _mm(sv['p'], w['w_ple'], 'nn', [BF16, BF16], epi=ple_epi, epi_args=(sv['gp'], dx3), name=f'ple_bwd_{tag}')
    g['w_ple'] = _mm(sv['p'], de, 'tn', [F32], name=f'd_w_ple_{tag}')
    g['w_ple_gate'] = _mm(sv['hp'], dgp, 'tn', [F32], name=f'd_w_ple_gate_{tag}')
    dhp = _mm(dgp, w['w_ple_gate'], 'nt', [F32], name=f'd_hp_{tag}')
    dx2, g['g_ple'] = _rms_bwd(sv['x2'], sp['g_ple'], dhp, dx3, f'rms_ple_bwd_{tag}')
    dact = _mm(dx2, w['w_down'], 'nt', [F32], name=f'd_act_{tag}')
    g['w_down'] = _mm(sv['act'], dx2, 'tn', [F32], name=f'd_w_down_{tag}')

    def ffn_bwd_tile(ug, uv, da, wg, wv):
        cg, cv = _conv(ug, wg), _conv(uv, wv)
        _, vjp = jax.vjp(_ffn_act, cg, cv)
        dcg, dcv = vjp(da)
        dug, dwg = _conv_bwd(ug, wg, dcg)
        duv, dwv = _conv_bwd(uv, wv, dcv)
        return dug, duv, jnp.concatenate(dwg, axis=0), jnp.concatenate(dwv, axis=0)
    dupg, dupv, dwg, dwv = _cb(ffn_bwd_tile, [(sv['up'], 0), (sv['up'], D_FF), (dact, 0)],
                               [(w['ffn_conv_w'], 0), (w['ffn_conv_w'], D_FF)], [BF16, BF16], [3, 3], tc=256,
                               nblk=D_FF // 256, name=f'ffn_act_bwd_{tag}')
    dup = jnp.concatenate([dupg, dupv], axis=1)
    g['ffn_conv_w'] = jnp.concatenate([dwg, dwv], axis=1)
    g['w_up'] = _mm(dup, sv['hf'], 'tn', [F32], name=f'd_w_up_{tag}')
    dhf = _mm(dup, w['w_up'], 'nn', [F32], name=f'd_hf_{tag}')
    dx1, g['g_ffn'] = _rms_bwd(sv['x1'], sp['g_ffn'], dhf, dx2, f'rms_ffn_bwd_{tag}')
    dmerged = _mm(dx1, w['w_o'], 'nt', [F32], name=f'd_merged_{tag}')
    g['w_o'] = _mm(sv['merged'], dx1, 'tn', [F32], name=f'd_w_o_{tag}')
    dgate, dtn = _merge_bwd(sv['ys'], w['w_branch'], sv['proj'], dmerged, f'merge_bwd_{tag}')
    dys, dwb = [], []
    for n in range(3):
        dys.append(_mm(dtn[n], w['w_branch'][n], 'nt', [F32], name=f'd_y{n}_{tag}'))
        dwb.append(_mm(sv['ys'][n], dtn[n], 'tn', [F32], name=f'd_w_branch{n}_{tag}'))
    g['w_branch'] = jnp.stack(dwb)
    if early_grads is not None:
        sp = early_grads(g, sp)
    ddq, ddk, ddv, ddz, dwq, dwk, dwv_, dgb, dng = _dn_bwd(sv['proj'], w['dn_conv_w'], sv['gb'], sp['dn_norm_gain'],
                                                           sv['o_dn'], sv['states'], sv['dn_loc'], dys[2], f'dn_bwd_{tag}')
    g['dn_conv_w'] = jnp.concatenate([dwq, dwk, dwv_], axis=1)
    g['dn_norm_gain'] = jnp.sum(dng, axis=0)
    def sc_bwd_tile(sb, sc, svv, dy, wv):
        u = sc * svv
        dsb = dy * _conv(u, wv)
        du, dws = _conv_bwd(u, wv, dy * sb)
        return dsb, du * svv, du * sc, jnp.concatenate(dws, axis=0)
    dsb, dsc, dsv, g['sc_conv_w'] = _cb(sc_bwd_tile, [(sv['proj'], C_SB), (sv['proj'], C_SC), (sv['proj'], C_SV), (dys[1], 0)],
                                        [(w['sc_conv_w'], 0)], [BF16, BF16, BF16], [3], tc=256, nblk=2, name=f'sc_bwd_{tag}')
    dfq, dfk, dfv, dcfc, dcfr, dqg, dkg = _att_bwd(sv['proj'], sp['fox_q_gain'], sp['fox_k_gain'], sv['cfc'], sv['cfr'],
                                                   sv['lse'], sv['ys'][0], dys[0], f'fox_bwd_{tag}')
    g['fox_q_gain'] = jnp.sum(dqg, axis=0)
    g['fox_k_gain'] = jnp.sum(dkg, axis=0)
    dcf = (dcfc[:, :, 0] + dcfr[:, 0, :]).T
    daux = jnp.concatenate([dcf, dgb[:, :, 1].T, dgb[:, :, 0].T, jnp.zeros((t, 112), F32)], axis=1)
    dsmall, dbias, dalog = _prep_bwd(sv['proj'], sv['bias_row'], sv['alog_row'], daux, f'prep_bwd_{tag}')
    g['b_fox_f'] = dbias[:, 0:8]
    g['dn_dt_bias'] = dbias[:, 12:16]
    g['dn_a_log'] = dalog[:, 12:16]
    dproj = jnp.concatenate([dfq, dfk, dfv, dsb, dsc, dsv, ddq, ddk, ddv, ddz, dgate[0], dgate[1], dgate[2], dsmall], axis=1)
    g['w_in'] = _mm(sv['hn'], dproj, 'tn', [F32], name=f'd_w_in_{tag}')
    if last_grad is not None:
        sp = last_grad(g, sp)
    dhn = _mm(dproj, w['w_in'], 'nt', [F32], epi=lambda acc, z: (acc + z,),
              epi_args=(sp.get('zero_row', jnp.zeros((1, D_MODEL), F32)),), name=f'd_hn_{tag}')
    dx, g['g_mix'] = _rms_bwd(sv['x'], sp['g_mix'], dhn, dx1, f'rms_mix_bwd_{tag}')
    return dx, g


def _loss_bwd(y, target, name):
    inv = 1.0 / y.shape[1]

    def fn(yv, tv):
        err = yv - tv
        return err * inv, jnp.zeros((8, 128), F32) + 0.5 * inv * jnp.sum(err * err)
    return _ew(fn, [y, target], [], [(y.shape[1], F32)], [(8, 128)], name=name)


PACK_W = 1024
FULL_SHAPE = {'w_in': (D_MODEL, IN_ORIG), 'w_branch': (3, BW, D_MODEL), 'w_o': (D_MODEL, D_MODEL), 'w_up': (2 * D_FF, D_MODEL),
              'w_down': (D_FF, D_MODEL), 'w_ple_gate': (D_MODEL, D_MODEL), 'w_ple': (PLE_DIM, D_MODEL),
              'sc_conv_w': (3, BW), 'dn_conv_w': (4, 3 * BW), 'ffn_conv_w': (3, 2 * D_FF)}
SMALL_SHAPE = {'g_mix': D_MODEL, 'b_fox_f': FOX_HEADS, 'fox_q_gain': FOX_DH, 'fox_k_gain': FOX_DH, 'dn_a_log': DN_HEADS,
               'dn_dt_bias': DN_HEADS, 'dn_norm_gain': DN_DH, 'g_ffn': D_MODEL, 'g_ple': D_MODEL}


def _shard_shape(name):
    s = list(FULL_SHAPE[name])
    s[SHARDED[name]] //= N_DEV
    return tuple(s)


def _full_from_gathered(g, name):
    sh, ax = _shard_shape(name), SHARDED[name]
    blocks = jnp.moveaxis(g, 0, ax)
    return blocks.reshape(sh[:ax] + (N_DEV * sh[ax],) + sh[ax + 1:])


def _by_dest(full, name):
    sh, ax = _shard_shape(name), SHARDED[name]
    return jnp.moveaxis(full.reshape(sh[:ax] + (N_DEV, sh[ax]) + sh[ax + 1:]), ax, 0)


def _flat_pack(arrs):
    flat = jnp.concatenate([a.reshape(-1).astype(F32) for a in arrs])
    rows = -(-flat.shape[0] // (8 * PACK_W)) * 8
    return jnp.pad(flat, (0, rows * PACK_W - flat.shape[0])).reshape(rows, PACK_W)


def _flat_unpack(pack, shapes):
    flat, out, off = pack.reshape(-1), [], 0
    for s in shapes:
        n = 1
        for d in s:
            n *= d
        out.append(flat[off:off + n].reshape(s))
        off += n
    return out


SMALL_SHAPES = [(DEPTH, SMALL_SHAPE[n]) for n in SMALL]
SMALL_LOSS_AT = sum(DEPTH * SMALL_SHAPE[n] for n in SMALL)
CONV_SHARD_SHAPES = [(DEPTH,) + _shard_shape(n) for n in CONVW]
CONV_FULL_SHAPES = [(DEPTH,) + FULL_SHAPE[n] for n in CONVW]


def _pick_tm(m, width):
    best = None
    for tm in range(16, m + 1, 16):
        if m % tm == 0 and tm * width * 4 <= (1 << 20):
            best = tm
    return best if best is not None else m


HBM_SPEC = pl.BlockSpec(memory_space=pltpu.HBM)


def _place():
    x, y, c = lax.axis_index("x"), lax.axis_index("y"), lax.axis_index("c")
    return x, y, c, [(1 - x, y), (x, 1 - y), (1 - x, 1 - y)]


def _all_gather(arrs, name):
    n = len(arrs)

    def body(*refs):
        ins, outs = refs[:n], refs[n:2 * n]
        send_sems, recv_sems = refs[2 * n:]
        x, y, c, chips = _place()
        me, sibling = (x, y, c), (x, y, 1 - c)

        def block(a, p):
            return outs[a].at[4 * p[0] + 2 * p[1] + p[2]]

        def copy(a, k, blk, to, src=None):
            return pltpu.make_async_remote_copy(src_ref=block(a, blk) if src is None else src, dst_ref=block(a, blk),
                                                send_sem=send_sems.at[a, k], recv_sem=recv_sems.at[a, k],
                                                device_id=to, device_id_type=MESH)

        first, passed = [], []
        for a in range(n):
            first.append(copy(a, 0, me, sibling, src=ins[a]))
            first += [copy(a, 1 + j, me, (*chip, c), src=ins[a]) for j, chip in enumerate(chips)]
        for cp in first:
            cp.start()
        for j, chip in enumerate(chips):
            for a in range(n):
                copy(a, 1 + j, (*chip, c), me).wait_recv()
                fwd = copy(a, 4 + j, (*chip, c), sibling)
                fwd.start()
                passed.append(fwd)
        for a in range(n):
            copy(a, 0, sibling, me).wait_recv()
            for j, chip in enumerate(chips):
                copy(a, 4 + j, (*chip, 1 - c), me).wait_recv()
        for cp in first + passed:
            cp.wait_send()

    gathered = pl.pallas_call(
        body, in_specs=[HBM_SPEC] * n, out_specs=[HBM_SPEC] * n,
        out_shape=[jax.ShapeDtypeStruct((N_DEV,) + a.shape, a.dtype) for a in arrs],
        scratch_shapes=[pltpu.SemaphoreType.DMA((n, 7)), pltpu.SemaphoreType.DMA((n, 7))], name=name,
    )(*arrs)
    me = 4 * lax.axis_index("x") + 2 * lax.axis_index("y") + lax.axis_index("c")
    return [lax.dynamic_update_index_in_dim(g, a, me, 0) for g, a in zip(gathered, arrs)]


SEM_SPEC = pl.BlockSpec(memory_space=pltpu.SEMAPHORE)
DATAFLOW = pltpu.SideEffectType.DATAFLOW_SIDE_EFFECTING
ALL_PEERS = (1, 2, 3, 4, 5, 6, 7)
NEAR_PEERS = (1, 4, 2, 6)


def _direct_copies(src_refs, zone_refs, send_sems, recv_sems, per_dest, peers):
    x, y, c, _ = _place()
    me = 4 * x + 2 * y + c
    cps = []
    for a, (src, zone) in enumerate(zip(src_refs, zone_refs)):
        for r, bits in enumerate(peers):
            px = 1 - x if bits & 4 else x
            py = 1 - y if bits & 2 else y
            pc = 1 - c if bits & 1 else c
            cps.append(pltpu.make_async_remote_copy(
                src_ref=src.at[4 * px + 2 * py + pc] if per_dest else src, dst_ref=zone.at[me],
                send_sem=send_sems.at[a * len(peers) + r], recv_sem=recv_sems.at[a * len(peers) + r],
                device_id=(px, py, pc), device_id_type=MESH))
    return cps


def _direct_start(srcs, per_dest, after, name, peers=ALL_PEERS):
    n = len(srcs)
    zones = [lax.empty((N_DEV,) + (s.shape[1:] if per_dest else s.shape), s.dtype) for s in srcs]

    def body(*refs):
        src_refs, zone_refs = refs[:n], refs[n:2 * n]
        send_sems, recv_sems = refs[2 * n + 1:2 * n + 3]
        for cp in _direct_copies(src_refs, zone_refs, send_sems, recv_sems, per_dest, peers):
            cp.start()
        refs[-1][...] = jnp.zeros_like(refs[-1])

    sems = pltpu.SemaphoreType.DMA((n * len(peers),))
    res = pl.pallas_call(
        body, name=name, in_specs=[HBM_SPEC] * (2 * n + 1),
        out_shape=[sems, sems] + [pltpu.HBM(s.shape, s.dtype) for s in srcs] + [pltpu.HBM(z.shape, z.dtype) for z in zones]
        + [jax.ShapeDtypeStruct((8, 128), F32)],
        out_specs=[SEM_SPEC, SEM_SPEC] + [HBM_SPEC] * (2 * n) + [pl.BlockSpec(memory_space=pltpu.VMEM)],
        input_output_aliases={i: 2 + i for i in range(2 * n)},
        compiler_params=pltpu.CompilerParams(has_side_effects=DATAFLOW),
    )(*[pltpu.with_memory_space_constraint(s, pltpu.HBM) for s in srcs],
      *[pltpu.with_memory_space_constraint(z, pltpu.HBM) for z in zones], after)
    return (res[0], res[1], list(res[2:2 + n]), list(res[2 + n:2 + 2 * n]), per_dest, peers), res[-1][0:1, 0:1]


def _direct_wait(started, after, name):
    send_sems, recv_sems, srcs, zones, per_dest, peers = started
    n = len(srcs)

    def body(*refs):
        src_refs, zone_refs = refs[:n], refs[n:2 * n]
        for cp in _direct_copies(src_refs, zone_refs, refs[2 * n], refs[2 * n + 1], per_dest, peers):
            cp.wait_send()
            cp.wait_recv()

    res = pl.pallas_call(
        body, name=name, in_specs=[HBM_SPEC] * (2 * n) + [SEM_SPEC, SEM_SPEC, HBM_SPEC],
        out_shape=[pltpu.HBM(s.shape, s.dtype) for s in srcs] + [pltpu.HBM(z.shape, z.dtype) for z in zones],
        out_specs=[HBM_SPEC] * (2 * n), input_output_aliases={i: i for i in range(2 * n)},
        compiler_params=pltpu.CompilerParams(has_side_effects=DATAFLOW),
    )(*srcs, *zones, send_sems, recv_sems, after)
    return list(res[n:])


def _gather_finish(zones, name):
    n = len(zones)

    def body(*refs):
        zone_in, zone_out = refs[:n], refs[n:2 * n]
        send_sems, recv_sems = refs[2 * n:]
        x, y, c, chips = _place()
        cps = []
        for a in range(n):
            for j, (cx, cy) in enumerate(chips):
                blk = 4 * cx + 2 * cy + c
                cp = pltpu.make_async_remote_copy(src_ref=zone_in[a].at[blk], dst_ref=zone_out[a].at[blk],
                                                  send_sem=send_sems.at[a, j], recv_sem=recv_sems.at[a, j],
                                                  device_id=(x, y, 1 - c), device_id_type=MESH)
                cp.start()
                cps.append(cp)
        for cp in cps:
            cp.wait()

    return pl.pallas_call(
        body, in_specs=[HBM_SPEC] * n, out_specs=[HBM_SPEC] * n,
        out_shape=[jax.ShapeDtypeStruct(z.shape, z.dtype) for z in zones], input_output_aliases={a: a for a in range(n)},
        scratch_shapes=[pltpu.SemaphoreType.DMA((n, 3)), pltpu.SemaphoreType.DMA((n, 3))], name=name,
    )(*zones)


def _adamw(w, g, m, v):
    m = ADAM_B1 * m + (1.0 - ADAM_B1) * g
    v = ADAM_B2 * v + (1.0 - ADAM_B2) * jnp.square(g)
    m_hat = m / (1.0 - ADAM_B1 ** ADAM_STEP)
    v_hat = v / (1.0 - ADAM_B2 ** ADAM_STEP)
    delta = -ADAM_LR * (m_hat / (jnp.sqrt(v_hat) + ADAM_EPS) + ADAM_WD * w)
    return delta, m, v


def _adamw_layer(w, m, v, own, parts, layer, prev, name):
    _, rows, c = w.shape
    tm = _pick_tm(rows, c)

    def body(w_ref, m_ref, v_ref, own_ref, parts_ref, *rest):
        g = own_ref[...]
        for j in range(parts_ref.shape[0]):
            g = g + parts_ref[j].astype(F32)
        delta, m2, v2 = _adamw(w_ref[...], g, m_ref[...], v_ref[...])
        g_ref, d_ref, m2_ref, v2_ref = rest[-4:]
        g_ref[...], d_ref[...], m2_ref[...], v2_ref[...] = g, delta, m2, v2

    wspec = pl.BlockSpec((None, tm, c), lambda i: (layer, i, 0))
    prev = list(prev) if prev is not None else []
    return pl.pallas_call(
        body, grid=(rows // tm,),
        in_specs=[wspec] * 3 + [pl.BlockSpec((tm, c), lambda i: (i, 0)), pl.BlockSpec((parts.shape[0], tm, c), lambda i: (0, i, 0))]
        + [HBM_SPEC] * len(prev),
        out_specs=[wspec] * 4, out_shape=[jax.ShapeDtypeStruct(w.shape, F32)] * 4,
        input_output_aliases={5 + k: k for k in range(len(prev))}, name=name,
    )(w, m, v, own, parts, *prev)


W_IN_ADAMW_TM = 128


def _adamw_w_in(w_t, m_t, v_t, grads, name):
    tm = W_IN_ADAMW_TM
    ng = [p.shape[0] for _, p in grads]

    def body(w_ref, m_ref, v_ref, *rest):
        outs = rest[2 * DEPTH:]
        for layer in range(DEPTH):
            own_ref, parts_ref = rest[2 * layer], rest[2 * layer + 1]
            g = own_ref[...]
            for j in range(ng[layer]):
                g = g + parts_ref[j].astype(F32)
            delta, m2, v2 = _adamw(w_ref[:, layer, :].T, g, m_ref[:, layer, :].T, v_ref[:, layer, :].T)
            for o_ref, val in zip(outs, (g, delta, m2, v2)):
                o_ref[:, layer, :] = val.T

    tspec = pl.BlockSpec((W_IN_SHARD, DEPTH, tm), lambda i: (0, 0, i))
    gspecs = []
    for _, parts in grads:
        gspecs += [pl.BlockSpec((tm, W_IN_SHARD), lambda i: (i, 0)), pl.BlockSpec((parts.shape[0], tm, W_IN_SHARD), lambda i: (0, i, 0))]
    return pl.pallas_call(
        body, grid=(D_MODEL // tm,), in_specs=[tspec] * 3 + gspecs, out_specs=[tspec] * 4,
        out_shape=[jax.ShapeDtypeStruct(w_t.shape, F32)] * 4, name=name,
    )(w_t, m_t, v_t, *[x for pair in grads for x in pair])


def _adamw_call(w, m, v, gparts, name):
    def fn(wv, mv, vv, *gs):
        g = gs[0].astype(F32)
        for gp in gs[1:]:
            g = g + gp.astype(F32)
        delta, m2, v2 = _adamw(wv, g, mv, vv)
        return g, delta, m2, v2
    rows, c = w.shape
    return _ew(fn, [w, m, v] + list(gparts), [], [(c, F32)] * 4, tm=_pick_tm(rows, c), name=name)


def kernel(x, p, g_mix, w_in, b_fox_f, fox_q_gain, fox_k_gain, sc_conv_w, dn_conv_w, dn_a_log, dn_dt_bias,
           dn_norm_gain, w_branch, w_o, g_ffn, w_up, ffn_conv_w, w_down, g_ple, w_ple_gate, w_ple, loss_target,
           m_g_mix, m_w_in, m_b_fox_f, m_fox_q_gain, m_fox_k_gain, m_sc_conv_w, m_dn_conv_w, m_dn_a_log,
           m_dn_dt_bias, m_dn_norm_gain, m_w_branch, m_w_o, m_g_ffn, m_w_up, m_ffn_conv_w, m_w_down, m_g_ple,
           m_w_ple_gate, m_w_ple, v_g_mix, v_w_in, v_b_fox_f, v_fox_q_gain, v_fox_k_gain, v_sc_conv_w, v_dn_conv_w,
           v_dn_a_log, v_dn_dt_bias, v_dn_norm_gain, v_w_branch, v_w_o, v_g_ffn, v_w_up, v_ffn_conv_w, v_w_down,
           v_g_ple, v_w_ple_gate, v_w_ple):
    return _step(x, p, g_mix, w_in, b_fox_f, fox_q_gain, fox_k_gain, sc_conv_w, dn_conv_w, dn_a_log, dn_dt_bias,
                 dn_norm_gain, w_branch, w_o, g_ffn, w_up, ffn_conv_w, w_down, g_ple, w_ple_gate, w_ple, loss_target,
                 m_g_mix, m_w_in, m_b_fox_f, m_fox_q_gain, m_fox_k_gain, m_sc_conv_w, m_dn_conv_w, m_dn_a_log,
                 m_dn_dt_bias, m_dn_norm_gain, m_w_branch, m_w_o, m_g_ffn, m_w_up, m_ffn_conv_w, m_w_down, m_g_ple,
                 m_w_ple_gate, m_w_ple, v_g_mix, v_w_in, v_b_fox_f, v_fox_q_gain, v_fox_k_gain, v_sc_conv_w,
                 v_dn_conv_w, v_dn_a_log, v_dn_dt_bias, v_dn_norm_gain, v_w_branch, v_w_o, v_g_ffn, v_w_up,
                 v_ffn_conv_w, v_w_down, v_g_ple, v_w_ple_gate, v_w_ple)


def _step(*args):
    names = ['x', 'p'] + WEIGHTS + ['loss_target'] + ['m_' + n for n in WEIGHTS] + ['v_' + n for n in WEIGHTS]
    assert len(args) == len(names)
    a = dict(zip(names, args))
    for n, perm in HELD_TRANSPOSED.items():
        for k in (n, 'm_' + n, 'v_' + n):
            a[k] = jnp.transpose(a[k], perm)
    x, target = a['x'][0], a['loss_target'][0]
    p = a['p'][:, 0]
    dev = 4 * lax.axis_index("x") + 2 * lax.axis_index("y") + lax.axis_index("c")

    LATE = [n for n in BIG if n != 'w_in']
    sps = [{n: a[n][layer][None, :] for n in SMALL} for layer in range(DEPTH)]
    shards = [{n: a[n][layer].astype(BF16) for n in BIG} for layer in range(DEPTH)]

    def by_device(full, n):
        by_dest = _by_dest(full, n)
        return lax.dynamic_index_in_dim(by_dest, dev, axis=0, keepdims=False), by_dest.astype(BF16)

    def blank(zone):
        return lax.dynamic_update_index_in_dim(zone, jnp.zeros(zone.shape[1:], zone.dtype), dev, 0)

    w_in_all, conv_all = _all_gather([shards[0]['w_in'], _flat_pack([a[n] for n in CONVW])], 'gather_w_in_l0')
    conv_by_dev = [_flat_unpack(conv_all[d], CONV_SHARD_SHAPES) for d in range(N_DEV)]
    conv_full = {n: jnp.concatenate([conv_by_dev[d][i] for d in range(N_DEV)], axis=2) for i, n in enumerate(CONVW)}
    late0_started, zero = _direct_start([shards[0][n] for n in LATE], False, w_in_all, 'weights_rest_l0_start', NEAR_PEERS)
    w1_started, zero = _direct_start([shards[1][n] for n in BIG], False, late0_started[2][0], 'weights_l1_start', NEAR_PEERS)
    sps[0]['g_mix'] = sps[0]['g_mix'] + zero

    def late_weights0(after):
        zones = _gather_finish(_direct_wait(late0_started, after, 'weights_rest_l0_wait'), 'weights_rest_l0_finish')
        return {n: _full_from_gathered(lax.dynamic_update_index_in_dim(z, shards[0][n], dev, 0), n) for n, z in zip(LATE, zones)}

    w0 = {n: conv_full[n][0] for n in CONVW}
    w0['w_in'] = _w_in_assemble(w_in_all, 'w_in_assemble_l0')
    h, sv0 = _layer_fwd(x, p[0], w0, sps[0], 'l0', late_weights0)
    zones = [lax.dynamic_update_index_in_dim(z, shards[1][n], dev, 0)
             for n, z in zip(BIG, _gather_finish(_direct_wait(w1_started, h, 'weights_l1_wait'), 'weights_l1_finish'))]
    w1 = {n: _full_from_gathered(z, n) for n, z in zip(BIG, zones) if n != 'w_in'}
    w1.update({n: conv_full[n][1] for n in CONVW})
    w1['w_in'] = _w_in_assemble(zones[list(BIG).index('w_in')], 'w_in_assemble_l1')
    h, sv1 = _layer_fwd(h, p[1], w1, sps[1], 'l1')
    dh, loss_part = _loss_bwd(h, target, 'loss')
    grads = [None] * DEPTH
    dh, grads[1] = _layer_bwd(dh, sv1, sps[1], 'l1')
    owns1, sends1 = {}, {}
    for n in BIG:
        owns1[n], sends1[n] = _w_in_split_by_device(grads[1][n], 'w_in_grad_split_l1') if n == 'w_in' else by_device(grads[1][n], n)
    g1_started, zero = _direct_start([sends1[n] for n in BIG], True, dh, 'grads_l1_start')
    early = {}

    def early_grads0(g, sp):
        owns, sends = zip(*[by_device(g[n], n) for n in LATE])
        early['started'], zero = _direct_start(list(sends), True, g['w_branch'], 'grads_rest_l0_start')
        early['owns'] = dict(zip(LATE, owns))
        return {**sp, 'dn_norm_gain': sp['dn_norm_gain'] + zero}

    def last_grad0(g, sp):
        early['own_in'], send = _w_in_split_by_device(g['w_in'], 'w_in_grad_split_l0')
        early['in_started'], zero = _direct_start([send], True, g['w_in'], 'grads_w_in_l0_start')
        return {**sp, 'g_mix': sp['g_mix'] + zero, 'zero_row': jnp.zeros((1, D_MODEL), F32) + zero}

    dh, grads[0] = _layer_bwd(dh, sv0, {**sps[0], 'g_ple': sps[0]['g_ple'] + zero}, 'l0', early_grads0, last_grad0)
    grad_x = dh[None]
    own_in0, in0_started = early['own_in'], early['in_started']
    sent = in0_started[2][0]
    zones1 = dict(zip(BIG, _direct_wait(g1_started, sent, 'grads_l1_wait')))
    zones0 = dict(zip(LATE, _direct_wait(early['started'], sent, 'grads_rest_l0_wait')))

    def adamw(n, layer, own, zone, prev):
        rows = (-1, a[n].shape[-1])
        own = own.reshape(rows)
        view = lambda t: t.reshape((DEPTH,) + own.shape)
        return _adamw_layer(view(a[n]), view(a['m_' + n]), view(a['v_' + n]), own, blank(zone).reshape((N_DEV,) + own.shape),
                            layer, prev, f'adamw_{n}_l{layer}')

    out, done = {}, []
    for n in LATE:
        res = adamw(n, 0, early['owns'][n], zones0[n], adamw(n, 1, owns1[n], zones1[n], None))
        out[n] = [r.reshape(a[n].shape) for r in res]
        done.append(res[0][0, 0:1, 0:1])
    (zone_in0,) = _direct_wait(in0_started, jnp.concatenate(done + [dh[0:1, 0:1]], axis=1), 'grads_w_in_l0_wait')
    res = _adamw_w_in(*[jnp.transpose(a[k], (2, 0, 1)) for k in ('w_in', 'm_w_in', 'v_w_in')],
                      [(own_in0, blank(zone_in0)), (owns1['w_in'], blank(zones1['w_in']))], 'adamw_w_in')
    out['w_in'] = [jnp.transpose(r, (1, 2, 0)) for r in res]

    small_pack = _flat_pack([jnp.stack([grads[layer][n].reshape(-1) for layer in range(DEPTH)]) for n in SMALL] + [loss_part[0, 0]])
    conv_pack = _flat_pack([jnp.stack([grads[layer][n] for layer in range(DEPTH)]) for n in CONVW])
    small_started, _ = _direct_start([small_pack, conv_pack], False, small_pack, 'small_grads_start')
    small_all, conv_all = [lax.dynamic_update_index_in_dim(z, s, dev, 0) for z, s in
                           zip(_direct_wait(small_started, out['w_in'][0], 'small_grads_wait'), (small_pack, conv_pack))]
    res = _adamw_call(_flat_pack([a[n] for n in SMALL]), _flat_pack([a['m_' + n] for n in SMALL]),
                      _flat_pack([a['v_' + n] for n in SMALL]), [small_all[d] for d in range(N_DEV)], 'adamw_replicated')
    loss = res[0].reshape(-1)[SMALL_LOSS_AT]
    for k, r in enumerate(res):
        for n, val in zip(SMALL, _flat_unpack(r, SMALL_SHAPES)):
            out.setdefault(n, [None] * 4)[k] = val
    (conv_sum,) = _ew(lambda *gs: (functools.reduce(lambda s, t: s + t, gs),), [conv_all[d] for d in range(N_DEV)], [],
                      [(PACK_W, F32)], tm=conv_pack.shape[0], name='conv_grad_sum')
    conv_own = [lax.dynamic_slice_in_dim(g, dev * (g.shape[2] // N_DEV), g.shape[2] // N_DEV, axis=2)
                for g in _flat_unpack(conv_sum, CONV_FULL_SHAPES)]
    res = _adamw_call(_flat_pack([a[n] for n in CONVW]), _flat_pack([a['m_' + n] for n in CONVW]),
                      _flat_pack([a['v_' + n] for n in CONVW]), [_flat_pack(conv_own)], 'adamw_conv')
    for k, r in enumerate(res):
        for n, val in zip(CONVW, _flat_unpack(r, CONV_SHARD_SHAPES)):
            out.setdefault(n, [None] * 4)[k] = val

    for n, perm in HELD_TRANSPOSED.items():
        out[n] = [jnp.transpose(r, perm) for r in out[n]]
    outs = [loss, grad_x]
    for k in range(4):
        outs += [out[n][k] for n in WEIGHTS]
    return tuple(outs)
```

```python
import functools

import jax
import jax.numpy as jnp
from jax import lax
from jax.experimental import pallas as pl
from jax.experimental.pallas import tpu as pltpu

F32 = jnp.float32
BF16 = jnp.bfloat16
HI = lax.Precision.HIGHEST

D_MODEL = 1024
DEPTH = 2
N_DEV = 8
PLE_DIM = 256
BW = 512
FOX_HEADS, FOX_DH = 8, 64
DN_HEADS, DN_DH = 4, 128
DN_CHUNK = 64
D_FF = 2816
EPS = 1e-6
NEG = -1e30

ADAM_LR, ADAM_B1, ADAM_B2, ADAM_EPS, ADAM_WD, ADAM_STEP = 0.001, 0.9, 0.999, 1e-08, 0.01, 10

C_FQ, C_FK, C_FV = 0, 512, 1024
C_SB, C_SC, C_SV = 1536, 2048, 2560
C_DQ, C_DK, C_DV, C_DZ = 3072, 3584, 4096, 4608
C_GATE = 5120
C_SMALL = 8192
IN_P = 8320
IN_ORIG = 8208

WEIGHTS = ['g_mix', 'w_in', 'b_fox_f', 'fox_q_gain', 'fox_k_gain', 'sc_conv_w', 'dn_conv_w', 'dn_a_log',
           'dn_dt_bias', 'dn_norm_gain', 'w_branch', 'w_o', 'g_ffn', 'w_up', 'ffn_conv_w', 'w_down', 'g_ple',
           'w_ple_gate', 'w_ple']
BIG = {'w_in': 1, 'w_branch': 2, 'w_o': 0, 'w_up': 0, 'w_down': 0, 'w_ple_gate': 0, 'w_ple': 1}
HELD_TRANSPOSED = {'w_up': (0, 2, 1)}
CONVW = {'sc_conv_w': 1, 'dn_conv_w': 1, 'ffn_conv_w': 1}
SHARDED = {**BIG, **CONVW}
SMALL = [n for n in WEIGHTS if n not in SHARDED]
MESH = pl.DeviceIdType.MESH


def _sigmoid(x):
    return 0.5 * (jnp.tanh(0.5 * x) + 1.0)


def _silu(x):
    return x * _sigmoid(x)


def _log1pexp_negabs(z):
    return jnp.log(1.0 + jnp.exp(-jnp.abs(z)))


def _log_sigmoid(z):
    return jnp.minimum(z, 0.0) - _log1pexp_negabs(z)


def _softplus(z):
    return jnp.maximum(z, 0.0) + _log1pexp_negabs(z)


def _rms(x, g):
    return x * lax.rsqrt(jnp.mean(x * x, axis=-1, keepdims=True) + EPS) * g


def _l2(x):
    return x * lax.rsqrt(jnp.sum(x * x, axis=-1, keepdims=True) + EPS)


def _dot(a, b, dims, precision=None):
    return lax.dot_general(a, b, (dims, ((), ())), preferred_element_type=F32, precision=precision)


NN = ((1,), (0,))
NT = ((1,), (1,))
TN = ((0,), (0,))


def _shift_down(x, s):
    if s == 0:
        return x
    t = lax.broadcasted_iota(jnp.int32, x.shape, 0)
    return jnp.where(t >= s, pltpu.roll(x, s, 0), 0.0)


def _shift_up(x, s):
    if s == 0:
        return x
    n = x.shape[0]
    t = lax.broadcasted_iota(jnp.int32, x.shape, 0)
    return jnp.where(t < n - s, pltpu.roll(x, n - s, 0), 0.0)


def _conv(x, w):
    k = w.shape[0]
    y = w[k - 1:k] * x
    for j in range(k - 1):
        y = y + w[j:j + 1] * _shift_down(x, k - 1 - j)
    return y


def _conv_bwd(x, w, dy):
    k = w.shape[0]
    dx = w[k - 1:k] * dy
    dws = []
    for j in range(k - 1):
        dx = dx + w[j:j + 1] * _shift_up(dy, k - 1 - j)
        dws.append(jnp.sum(dy * _shift_down(x, k - 1 - j), axis=0, keepdims=True))
    dws.append(jnp.sum(dy * x, axis=0, keepdims=True))
    return dx, dws


MM_VMEM_BUDGET = 36 << 20
MM_STEP_BYTES = 1 << 20


def _mm_tiles(m, n, k, a_size, b_size, tile_size, cast_a):
    best = None
    for tm in [d for d in (2048, 1024, 512, 256, 128) if d <= m and m % d == 0] or [m]:
        for tn in [d for d in range(128, min(n, 2048) + 1, 128) if n % d == 0] or [n]:
            vmem = 2 * tm * k * a_size + (2 * tm * k if cast_a else 0) + 2 * k * tn * b_size + 2 * tm * tn * tile_size
            if vmem > MM_VMEM_BUDGET:
                continue
            steps = (m // tm) * (n // tn)
            cost = m * k * a_size + (m // tm) * k * n * b_size + m * n * tile_size + steps * MM_STEP_BYTES
            if best is None or cost < best[0]:
                best = (cost, tm, tn)
    assert best is not None, (m, n, k)
    return best[1], best[2]


def _mm(a, b, mode, outs, *, epi=None, epi_args=(), name):
    if mode == 'nn':
        (m, k), (k2, n) = a.shape, b.shape
    elif mode == 'nt':
        (m, k), (n, k2) = a.shape, b.shape
    else:
        (k, m), (k2, n) = a.shape, b.shape
    assert k == k2, (a.shape, b.shape, mode)
    tile_size = sum(jnp.dtype(dt).itemsize for dt in outs) + sum(e.dtype.itemsize for e in epi_args if e.shape[0] != 1)
    tm, tn = _mm_tiles(m, n, k, a.dtype.itemsize, b.dtype.itemsize, tile_size, a.dtype != BF16)
    dims = {'nn': NN, 'nt': NT, 'tn': TN}[mode]
    a_spec = pl.BlockSpec((k, tm), lambda i, j: (0, i)) if mode == 'tn' else pl.BlockSpec((tm, k), lambda i, j: (i, 0))
    b_spec = pl.BlockSpec((tn, k), lambda i, j: (j, 0)) if mode == 'nt' else pl.BlockSpec((k, tn), lambda i, j: (0, j))
    e_specs = [pl.BlockSpec((1, tn), lambda i, j: (0, j)) if e.shape[0] == 1 else pl.BlockSpec((tm, tn), lambda i, j: (i, j))
               for e in epi_args]
    ne, no = len(epi_args), len(outs)
    cast_a = a.dtype != BF16

    def body(a_ref, b_ref, *rest):
        if cast_a:
            a_sc = rest[-1]

            @pl.when(pl.program_id(1) == 0)
            def _():
                a_sc[...] = a_ref[...].astype(BF16)
            av = a_sc[...]
        else:
            av = a_ref[...]
        acc = _dot(av, b_ref[...].astype(BF16), dims)
        vals = epi(acc, *[e[...] for e in rest[:ne]]) if epi is not None else (acc,)
        for o_ref, v in zip(rest[ne:ne + no], vals):
            o_ref[...] = v.astype(o_ref.dtype)

    res = pl.pallas_call(
        body, grid=(m // tm, n // tn),
        in_specs=[a_spec, b_spec] + e_specs,
        out_specs=[pl.BlockSpec((tm, tn), lambda i, j: (i, j)) for _ in outs],
        out_shape=[jax.ShapeDtypeStruct((m, n), dt) for dt in outs],
        scratch_shapes=[pltpu.VMEM(a_spec.block_shape, BF16)] if cast_a else [],
        name=name,
    )(a, b, *epi_args)
    return res[0] if no == 1 else res


def _ew(fn, tiled, bcast, outs, reds=(), *, tm=256, name):
    secs = [(t, 0, t.shape[1]) if not isinstance(t, tuple) else t for t in tiled]
    m = secs[0][0].shape[0]
    tm = min(tm, m)
    assert m % tm == 0
    in_specs = []
    for arr, off, w in secs:
        assert off % w == 0
        in_specs.append(pl.BlockSpec((tm, w), functools.partial(lambda i, c: (i, c), c=off // w)))
    in_specs += [pl.BlockSpec(b.shape, lambda i: (0, 0)) for b in bcast]
    nin, no = len(in_specs), len(outs)

    def body(*refs):
        vals = fn(*[r[...] for r in refs[:nin]])
        for r, v in zip(refs[nin:nin + no], vals[:no]):
            r[...] = v.astype(r.dtype)
        i = pl.program_id(0)
        for r, v in zip(refs[nin + no:], vals[no:]):
            @pl.when(i == 0)
            def _():
                r[...] = v

            @pl.when(i > 0)
            def _():
                r[...] += v

    res = pl.pallas_call(
        body, grid=(m // tm,), in_specs=in_specs,
        out_specs=[pl.BlockSpec((tm, c), lambda i: (i, 0)) for c, _ in outs] + [pl.BlockSpec(s, lambda i: (0, 0)) for s in reds],
        out_shape=[jax.ShapeDtypeStruct((m, c), dt) for c, dt in outs] + [jax.ShapeDtypeStruct(s, F32) for s in reds],
        name=name,
    )(*[s[0] for s in secs], *bcast)
    return res


def _cb(fn, cols, params, outs, pouts, *, tc, nblk, name):
    t = cols[0][0].shape[0]
    in_specs = []
    for arr, off in cols:
        assert off % tc == 0
        in_specs.append(pl.BlockSpec((t, tc), functools.partial(lambda c, o: (0, o + c), o=off // tc)))
    for arr, off in params:
        in_specs.append(pl.BlockSpec((arr.shape[0], tc), functools.partial(lambda c, o: (0, o + c), o=off // tc)))
    nin, no = len(in_specs), len(outs)

    def body(*refs):
        vals = fn(*[r[...] for r in refs[:nin]])
        for r, v in zip(refs[nin:], vals):
            r[...] = v.astype(r.dtype)

    return pl.pallas_call(
        body, grid=(nblk,), in_specs=in_specs,
        out_specs=[pl.BlockSpec((t, tc), lambda c: (0, c)) for _ in outs] + [pl.BlockSpec((k, tc), lambda c: (0, c)) for k in pouts],
        out_shape=[jax.ShapeDtypeStruct((t, nblk * tc), dt) for dt in outs] + [jax.ShapeDtypeStruct((k, nblk * tc), F32) for k in pouts],
        name=name,
    )(*[c[0] for c in cols], *[p[0] for p in params])


CUM_BLK = 256


def _prep_point(s, bias, alog):
    lane = lax.broadcasted_iota(jnp.int32, s.shape, 1)
    z = s + bias
    return jnp.where(lane < 8, _log_sigmoid(z),
                     jnp.where(lane < 12, _sigmoid(s),
                               jnp.where(lane < 16, -jnp.exp(alog) * _softplus(z), 0.0)))


def _tri(n, upper):
    r = lax.broadcasted_iota(jnp.int32, (n, n), 0)
    c = lax.broadcasted_iota(jnp.int32, (n, n), 1)
    return (r <= c if upper else r >= c).astype(F32)


def _prep_fwd(proj, bias_row, alog_row, name):
    t = proj.shape[0]
    nb = t // CUM_BLK

    def body(s_ref, b_ref, a_ref, o_ref):
        pre = _prep_point(s_ref[...], b_ref[...], a_ref[...])
        lane = lax.broadcasted_iota(jnp.int32, (CUM_BLK, 128), 1)
        tri = _tri(CUM_BLK, False)
        carry = jnp.zeros((1, 128), F32)
        for blk in range(nb):
            xb = pre[blk * CUM_BLK:(blk + 1) * CUM_BLK]
            cb = _dot(tri, xb, NN, HI) + carry
            carry = cb[CUM_BLK - 1:CUM_BLK]
            o_ref[blk * CUM_BLK:(blk + 1) * CUM_BLK, :] = jnp.where(lane < 8, cb, xb)

    return pl.pallas_call(
        body, grid=(1,),
        in_specs=[pl.BlockSpec((t, 128), lambda i: (0, C_SMALL // 128)), pl.BlockSpec((1, 128), lambda i: (0, 0)),
                  pl.BlockSpec((1, 128), lambda i: (0, 0))],
        out_specs=pl.BlockSpec((t, 128), lambda i: (0, 0)),
        out_shape=jax.ShapeDtypeStruct((t, 128), F32), name=name,
    )(proj, bias_row, alog_row)


def _prep_bwd(proj, bias_row, alog_row, daux, name):
    t = proj.shape[0]
    nb = t // CUM_BLK

    def body(s_ref, b_ref, a_ref, d_ref, ds_ref, db_ref, da_ref, dpre_sc):
        lane = lax.broadcasted_iota(jnp.int32, (CUM_BLK, 128), 1)
        tri = _tri(CUM_BLK, True)
        carry = jnp.zeros((1, 128), F32)
        for blk in reversed(range(nb)):
            db = d_ref[blk * CUM_BLK:(blk + 1) * CUM_BLK, :]
            cb = _dot(tri, db, NN, HI) + carry
            carry = cb[0:1]
            dpre_sc[blk * CUM_BLK:(blk + 1) * CUM_BLK, :] = jnp.where(lane < 8, cb, db)
        _, vjp = jax.vjp(_prep_point, s_ref[...], b_ref[...], a_ref[...])
        ds, dbias, dalog = vjp(dpre_sc[...])
        ds_ref[...] = ds.astype(ds_ref.dtype)
        db_ref[...] = dbias
        da_ref[...] = dalog

    return pl.pallas_call(
        body, grid=(1,),
        in_specs=[pl.BlockSpec((t, 128), lambda i: (0, C_SMALL // 128)), pl.BlockSpec((1, 128), lambda i: (0, 0)),
                  pl.BlockSpec((1, 128), lambda i: (0, 0)), pl.BlockSpec((t, 128), lambda i: (0, 0))],
        out_specs=[pl.BlockSpec((t, 128), lambda i: (0, 0)), pl.BlockSpec((1, 128), lambda i: (0, 0)),
                   pl.BlockSpec((1, 128), lambda i: (0, 0))],
        out_shape=[jax.ShapeDtypeStruct((t, 128), BF16), jax.ShapeDtypeStruct((1, 128), F32), jax.ShapeDtypeStruct((1, 128), F32)],
        scratch_shapes=[pltpu.VMEM((t, 128), F32)], name=name,
    )(proj, bias_row, alog_row, daux)


ATT_TQ = 256
FOX_SCALE = FOX_DH ** -0.5


def _qnorm(q, g):
    return _rms(q, g) * FOX_SCALE


def _att_scores(qn_blk, kn, cfc_blk, cfr, qi, tq, kend):
    s = _dot(qn_blk.astype(BF16), kn[:kend].astype(BF16), NT) + cfc_blk - cfr[:, :kend]
    row = lax.broadcasted_iota(jnp.int32, (tq, kend), 0) + qi * tq
    col = lax.broadcasted_iota(jnp.int32, (tq, kend), 1)
    return s, row >= col


ATT_PAIR = 128 // FOX_DH


def _att_specs(t):
    pair = lambda off: pl.BlockSpec((t, 128), functools.partial(lambda i, o: (0, o + i), o=off // 128))
    gain = pl.BlockSpec((1, FOX_DH), lambda i: (0, 0))
    hd = lambda i: (i, 0, 0)
    col, row = pl.BlockSpec((ATT_PAIR, t, 1), hd), pl.BlockSpec((ATT_PAIR, 1, t), hd)
    return [pair(C_FQ), pair(C_FK), pair(C_FV), gain, gain, col, row], pl.BlockSpec((t, 128), lambda i: (0, i)), col, row


def _att_fwd(proj, qg, kg, cfc, cfr, name):
    t = proj.shape[0]
    tq = min(ATT_TQ, t)
    in_specs, pair_out, col, _ = _att_specs(t)

    def body(q_ref, k_ref, v_ref, qg_ref, kg_ref, cfc_ref, cfr_ref, o_ref, lse_ref):
        for e in range(ATT_PAIR):
            lanes = slice(e * FOX_DH, (e + 1) * FOX_DH)
            qn = _qnorm(q_ref[:, lanes], qg_ref[...])
            kn = _rms(k_ref[:, lanes], kg_ref[...])
            v = v_ref[:, lanes].astype(BF16)
            cfr = cfr_ref[e]
            for qi in range(t // tq):
                kend = (qi + 1) * tq
                rows = slice(qi * tq, kend)
                s, mask = _att_scores(qn[rows], kn, cfc_ref[e, rows, :], cfr, qi, tq, kend)
                s = jnp.where(mask, s, NEG)
                m = jnp.max(s, axis=1, keepdims=True)
                p = jnp.exp(s - m)
                l = jnp.sum(p, axis=1, keepdims=True)
                o_ref[rows, lanes] = _dot(p.astype(BF16), v[:kend], NN) / l
                lse_ref[e, rows, :] = m + jnp.log(l)

    return pl.pallas_call(
        body, grid=(FOX_HEADS // ATT_PAIR,), in_specs=in_specs, out_specs=[pair_out, col],
        out_shape=[jax.ShapeDtypeStruct((t, BW), F32), jax.ShapeDtypeStruct((FOX_HEADS, t, 1), F32)], name=name,
    )(proj, proj, proj, qg, kg, cfc, cfr)


def _att_bwd(proj, qg, kg, cfc, cfr, lse, o, do, name):
    t = proj.shape[0]
    tq = min(ATT_TQ, t)
    in_specs, pair_out, col, row = _att_specs(t)

    def body(q_ref, k_ref, v_ref, qg_ref, kg_ref, cfc_ref, cfr_ref, lse_ref, o_ref, do_ref,
             dq_ref, dk_ref, dv_ref, dcfc_ref, dcfr_ref, dqg_ref, dkg_ref, dqn_sc, dkn_sc, dv_sc, dcfr_sc):
        for e in range(ATT_PAIR):
            lanes = slice(e * FOX_DH, (e + 1) * FOX_DH)
            qn, vjp_q = jax.vjp(_qnorm, q_ref[:, lanes], qg_ref[...])
            kn, vjp_k = jax.vjp(_rms, k_ref[:, lanes], kg_ref[...])
            v = v_ref[:, lanes].astype(BF16)
            cfr = cfr_ref[e]
            do_e = do_ref[:, lanes]
            delta = jnp.sum(do_e * o_ref[:, lanes], axis=1, keepdims=True)
            dkn_sc[...] = jnp.zeros_like(dkn_sc)
            dv_sc[...] = jnp.zeros_like(dv_sc)
            dcfr_sc[...] = jnp.zeros_like(dcfr_sc)
            for qi in range(t // tq):
                kend = (qi + 1) * tq
                rows = slice(qi * tq, kend)
                s, mask = _att_scores(qn[rows], kn, cfc_ref[e, rows, :], cfr, qi, tq, kend)
                p = jnp.where(mask, jnp.exp(jnp.where(mask, s, NEG) - lse_ref[e, rows, :]), 0.0)
                do_b = do_e[rows].astype(BF16)
                dv_sc[0:kend, :] += _dot(p.astype(BF16), do_b, TN)
                dp = _dot(do_b, v[:kend], NT)
                ds = p * (dp - delta[rows])
                ds_b = ds.astype(BF16)
                dqn_sc[rows, :] = _dot(ds_b, kn[:kend].astype(BF16), NN)
                dkn_sc[0:kend, :] += _dot(ds_b, qn[rows].astype(BF16), TN)
                dcfc_ref[e, rows, :] = jnp.sum(ds, axis=1, keepdims=True)
                dcfr_sc[:, 0:kend] -= jnp.sum(ds, axis=0, keepdims=True)
            dq, dqg = vjp_q(dqn_sc[...])
            dk, dkg = vjp_k(dkn_sc[...])
            dq_ref[:, lanes] = dq.astype(dq_ref.dtype)
            dk_ref[:, lanes] = dk.astype(dk_ref.dtype)
            dv_ref[:, lanes] = dv_sc[...].astype(dv_ref.dtype)
            dcfr_ref[e] = dcfr_sc[...]
            dqg_ref[e] = dqg
            dkg_ref[e] = dkg

    gsp = pl.BlockSpec((ATT_PAIR, 1, FOX_DH), lambda i: (i, 0, 0))
    return pl.pallas_call(
        body, grid=(FOX_HEADS // ATT_PAIR,),
        in_specs=in_specs + [col, pair_out, pair_out],
        out_specs=[pair_out] * 3 + [col, row, gsp, gsp],
        out_shape=[jax.ShapeDtypeStruct((t, BW), BF16)] * 3
        + [jax.ShapeDtypeStruct((FOX_HEADS, t, 1), F32), jax.ShapeDtypeStruct((FOX_HEADS, 1, t), F32)]
        + [jax.ShapeDtypeStruct((FOX_HEADS, 1, FOX_DH), F32)] * 2,
        scratch_shapes=[pltpu.VMEM((t, FOX_DH), F32)] * 3 + [pltpu.VMEM((1, t), F32)], name=name,
    )(proj, proj, proj, qg, kg, cfc, cfr, lse, o, do)


DN_SCALE = DN_DH ** -0.5


DN_BATCH = 8


@functools.partial(jax.custom_vjp, nondiff_argnums=(2, 3))
def _mm3(a, b, dims, batch=False):
    return _mm3_passes(a, b, dims, batch)


def _mm3_fwd(a, b, dims, batch):
    return _mm3_passes(a, b, dims, batch), (a, b)


def _mm3_bwd(dims, batch, res, dc):
    a, b = res
    if dims == NN:
        return _mm3_passes(dc, b, NT, batch), _mm3_passes(a, dc, TN, batch)
    if dims == NT:
        return _mm3_passes(dc, b, NN, batch), _mm3_passes(dc, a, TN, batch)
    return _mm3_passes(b, dc, NT, batch), _mm3_passes(a, dc, NN, batch)


_mm3.defvjp(_mm3_fwd, _mm3_bwd)


def _mm3_passes(a, b, dims, batch):
    if batch:
        dn = (((dims[0][0] + 1,), (dims[1][0] + 1,)), ((0,), (0,)))
        dot = lambda p, q: lax.dot_general(p, q, dn, preferred_element_type=F32)
    else:
        dot = lambda p, q: _dot(p, q, dims)
    ah, bh = a.astype(BF16), b.astype(BF16)
    al, bl = (a - ah.astype(F32)).astype(BF16), (b - bh.astype(F32)).astype(BF16)
    return dot(ah, bh) + (dot(ah, bl) + dot(al, bh))


def _dn_local(qc, kc, vc, g, beta):
    nb, c, _ = qc.shape
    ii = lax.broadcasted_iota(jnp.int32, (c, c), 0)
    jj = lax.broadcasted_iota(jnp.int32, (c, c), 1)
    incl, strict = ii >= jj, ii > jj
    lower = jnp.broadcast_to(incl.astype(F32), (nb, c, c))
    eye = (ii == jj).astype(F32)
    mm = functools.partial(_mm3, batch=True)
    dm = mm(lower, jnp.where(strict, g, 0.0), NN)
    decay = jnp.where(incl, jnp.exp(jnp.where(incl, dm, 0.0)), 0.0)
    gcum = mm(lower, g * jnp.ones((1, 1, DN_DH), F32), NN)
    eg = jnp.exp(gcum)
    glast = gcum[:, c - 1:c]
    kb = kc * beta
    n1 = jnp.where(strict, mm(kb, kc, NT) * decay, 0.0)
    inv = eye - n1
    pw = n1
    for _ in range(5):
        pw = mm(pw, pw, NN)
        inv = inv + mm(pw, inv, NN)
    sol = mm(inv, jnp.concatenate([vc * beta, kb * eg], axis=2), NN)
    qk = jnp.where(incl, mm(qc, kc, NT) * decay, 0.0)
    return sol[:, :, :DN_DH], sol[:, :, DN_DH:], qk, qc * eg, kc * jnp.exp(glast - gcum), jnp.exp(glast)


@functools.partial(jax.custom_vjp, nondiff_argnums=(2,))
def _mm1(a, b, dims):
    return _dot(a.astype(BF16), b.astype(BF16), dims)


def _mm1_fwd(a, b, dims):
    return _mm1(a, b, dims), (a, b)


def _mm1_bwd(dims, res, dc):
    a, b = res
    if dims == NN:
        return _mm1(dc, b, NT), _mm1(a, dc, TN)
    if dims == NT:
        return _mm1(dc, b, NN), _mm1(dc, a, TN)
    return _mm1(b, dc, NT), _mm1(a, dc, NN)


_mm1.defvjp(_mm1_fwd, _mm1_bwd)


def _dn_state(u, kcum, qk, qdec, kdec, egl, state):
    v_new = u - _mm1(kcum, state, NN)
    out = _mm1(qdec, state, NN) + _mm1(qk, v_new, NN)
    return out, state * egl + _mm1(kdec, v_new, TN)


def _dn_pre_q(c):
    return _l2(_silu(c)) * DN_SCALE


def _dn_pre_k(c):
    return _l2(_silu(c))


def _dn_post(o, z, ng):
    return _rms(o, ng) * _silu(z)


def _dn_specs(t):
    cblk = lambda o: pl.BlockSpec((t, DN_DH), functools.partial(lambda h, o: (0, o + h), o=o // DN_DH))
    wblk = lambda o: pl.BlockSpec((4, DN_DH), functools.partial(lambda h, o: (0, o + h), o=o // DN_DH))
    proj_specs = [cblk(C_DQ), cblk(C_DK), cblk(C_DV), cblk(C_DZ)]
    w_specs = [wblk(0), wblk(BW), wblk(2 * BW)]
    gb_spec = pl.BlockSpec((1, t, 2), lambda h: (h, 0, 0))
    return proj_specs, w_specs, gb_spec


def _chunk_rows(n, count=1):
    return pl.ds(pl.multiple_of(n * DN_CHUNK, DN_CHUNK), count * DN_CHUNK)


def _egl_rows(n, count=1):
    return pl.ds(pl.multiple_of(n * 8, 8), count * 8)


def _dn_local_inputs(n, qn_sc, kn_sc, vv_sc, gb_ref):
    r = _chunk_rows(n, DN_BATCH)
    split = lambda v: v.reshape(DN_BATCH, DN_CHUNK, v.shape[-1])
    gbv = split(gb_ref[0, r, :])
    return split(qn_sc[r, :]), split(kn_sc[r, :]), split(vv_sc[r, :]), gbv[:, :, 0:1], gbv[:, :, 1:2]


def _dn_local_phase(nc, qn_sc, kn_sc, vv_sc, gb_ref, loc):
    def step(i, carry):
        n = i * DN_BATCH
        vals = _dn_local(*_dn_local_inputs(n, qn_sc, kn_sc, vv_sc, gb_ref))
        for sc, val in zip(loc[:5], vals[:5]):
            sc[_chunk_rows(n, DN_BATCH), :] = val.reshape(DN_BATCH * DN_CHUNK, val.shape[-1])
        loc[5][_egl_rows(n, DN_BATCH), :] = jnp.broadcast_to(vals[5], (DN_BATCH, 8, DN_DH)).reshape(DN_BATCH * 8, DN_DH)
        return carry

    lax.fori_loop(0, nc // DN_BATCH, step, 0)


def _dn_loc_scratch(t, nc):
    big = pltpu.VMEM((t, DN_DH), F32)
    return [big, big, pltpu.VMEM((t, DN_CHUNK), F32), big, big, pltpu.VMEM((nc * 8, DN_DH), F32)]


def _dn_loc_io(t, nc, **spec_args):
    head_cols = pl.BlockSpec((t, DN_DH), lambda h: (0, h), **spec_args)
    per_head = lambda rows, cols: pl.BlockSpec((1, rows, cols), lambda h: (h, 0, 0), **spec_args)
    wide = jax.ShapeDtypeStruct((t, BW), F32)
    shapes = [wide, wide, jax.ShapeDtypeStruct((DN_HEADS, t, DN_CHUNK), F32), wide, wide,
              jax.ShapeDtypeStruct((DN_HEADS, nc * 8, DN_DH), F32)]
    return shapes, [head_cols, head_cols, per_head(t, DN_CHUNK), head_cols, head_cols, per_head(nc * 8, DN_DH)]


def _dn_loc_refs(refs):
    return refs[0], refs[1], refs[2].at[0], refs[3], refs[4], refs[5].at[0]


def _dn_fwd(proj, conv_w, gb, ng, name):
    t = proj.shape[0]
    nc = t // DN_CHUNK
    assert nc % DN_BATCH == 0
    proj_specs, w_specs, gb_spec = _dn_specs(t)
    loc_shapes, loc_specs = _dn_loc_io(t, nc)

    def body(q_ref, k_ref, v_ref, z_ref, wq_ref, wk_ref, wv_ref, gb_ref, ng_ref, y_ref, o_ref, st_ref, *rest):
        loc = _dn_loc_refs(rest[:6])
        qn_sc, kn_sc, vv_sc = rest[6:]
        qn_sc[...] = _dn_pre_q(_conv(q_ref[...], wq_ref[...]))
        kn_sc[...] = _dn_pre_k(_conv(k_ref[...], wk_ref[...]))
        vv_sc[...] = _silu(_conv(v_ref[...], wv_ref[...]))
        _dn_local_phase(nc, qn_sc, kn_sc, vv_sc, gb_ref, loc)
        u_sc, kcum_sc, qk_sc, qdec_sc, kdec_sc, egl_sc = loc

        def chunk(n, state):
            r = _chunk_rows(n)
            egl = egl_sc[_egl_rows(n), :][0:1]
            out, new_state = _dn_state(u_sc[r, :], kcum_sc[r, :], qk_sc[r, :], qdec_sc[r, :], kdec_sc[r, :], egl, state)
            st_ref[0, n] = state
            o_ref[r, :] = out
            return new_state

        lax.fori_loop(0, nc, chunk, jnp.zeros((DN_DH, DN_DH), F32))
        y_ref[...] = _dn_post(o_ref[...], z_ref[...], ng_ref[...])

    hblk = pl.BlockSpec((t, DN_DH), lambda h: (0, h))
    return pl.pallas_call(
        body, grid=(DN_HEADS,),
        in_specs=proj_specs + w_specs + [gb_spec, pl.BlockSpec((1, DN_DH), lambda h: (0, 0))],
        out_specs=[hblk, hblk, pl.BlockSpec((1, nc, DN_DH, DN_DH), lambda h: (h, 0, 0, 0))] + loc_specs,
        out_shape=[jax.ShapeDtypeStruct((t, BW), F32), jax.ShapeDtypeStruct((t, BW), F32),
                   jax.ShapeDtypeStruct((DN_HEADS, nc, DN_DH, DN_DH), F32)] + loc_shapes,
        scratch_shapes=[pltpu.VMEM((t, DN_DH), F32)] * 3, name=name,
    )(proj, proj, proj, proj, conv_w, conv_w, conv_w, gb, ng)


def _dn_bwd(proj, conv_w, gb, ng, o, states, loc_saved, dy, name):
    t = proj.shape[0]
    nc = t // DN_CHUNK
    proj_specs, w_specs, gb_spec = _dn_specs(t)
    _, loc_specs = _dn_loc_io(t, nc, pipeline_mode=pl.Buffered(1))

    def body(q_ref, k_ref, v_ref, z_ref, wq_ref, wk_ref, wv_ref, gb_ref, ng_ref, o_ref, st_ref, dy_ref,
             u_ref, kcum_ref, qk_ref, qdec_ref, kdec_ref, egl_ref,
             dq_ref, dk_ref, dv_ref, dz_ref, dwq_ref, dwk_ref, dwv_ref, dgb_ref, dng_ref,
             qn_sc, kn_sc, vv_sc, do_sc, *dloc):
        loc = _dn_loc_refs((u_ref, kcum_ref, qk_ref, qdec_ref, kdec_ref, egl_ref))
        qn_sc[...] = _dn_pre_q(_conv(q_ref[...], wq_ref[...]))
        kn_sc[...] = _dn_pre_k(_conv(k_ref[...], wk_ref[...]))
        vv_sc[...] = _silu(_conv(v_ref[...], wv_ref[...]))
        _, vjp_y = jax.vjp(_dn_post, o_ref[...], z_ref[...], ng_ref[...])
        do, dz, dng = vjp_y(dy_ref[...])
        do_sc[...] = do
        dz_ref[...] = dz.astype(dz_ref.dtype)
        dng_ref[0] = dng
        u_sc, kcum_sc, qk_sc, qdec_sc, kdec_sc, egl_sc = loc

        def state_bwd(i, dstate):
            n = nc - 1 - i
            r = _chunk_rows(n)
            r8 = _egl_rows(n)
            _, vjp = jax.vjp(_dn_state, u_sc[r, :], kcum_sc[r, :], qk_sc[r, :], qdec_sc[r, :], kdec_sc[r, :],
                             egl_sc[r8, :][0:1], st_ref[0, n])
            du, dkcum, dqk, dqdec, dkdec, degl, dprev = vjp((do_sc[r, :], dstate))
            for d_sc, val in zip(dloc[:5], (du, dkcum, dqk, dqdec, dkdec)):
                d_sc[r, :] = val
            dloc[5][r8, :] = jnp.broadcast_to(degl, (8, DN_DH))
            return dprev

        lax.fori_loop(0, nc, state_bwd, jnp.zeros((DN_DH, DN_DH), F32))

        def local_bwd(i, carry):
            n = i * DN_BATCH
            r = _chunk_rows(n, DN_BATCH)
            _, vjp = jax.vjp(_dn_local, *_dn_local_inputs(n, qn_sc, kn_sc, vv_sc, gb_ref))
            cts = tuple(d_sc[r, :].reshape(DN_BATCH, DN_CHUNK, d_sc.shape[-1]) for d_sc in dloc[:5])
            cts += (dloc[5][_egl_rows(n, DN_BATCH), :].reshape(DN_BATCH, 8, DN_DH)[:, 0:1],)
            dqc, dkc, dvc, dg, dbeta = vjp(cts)
            for d_sc, val in zip((dloc[0], dloc[1], dloc[3]), (dqc, dkc, dvc)):
                d_sc[r, :] = val.reshape(DN_BATCH * DN_CHUNK, DN_DH)
            dgb_ref[0, r, :] = jnp.concatenate([dg, dbeta], axis=2).reshape(DN_BATCH * DN_CHUNK, 2)
            return carry

        lax.fori_loop(0, nc // DN_BATCH, local_bwd, 0)
        for x_ref, w_ref, pre, d_sc, dx_ref, dw_ref in ((q_ref, wq_ref, _dn_pre_q, dloc[0], dq_ref, dwq_ref),
                                                       (k_ref, wk_ref, _dn_pre_k, dloc[1], dk_ref, dwk_ref),
                                                       (v_ref, wv_ref, _silu, dloc[3], dv_ref, dwv_ref)):
            _, vjp = jax.vjp(pre, _conv(x_ref[...], w_ref[...]))
            (dc,) = vjp(d_sc[...])
            dx, dws = _conv_bwd(x_ref[...], w_ref[...], dc)
            dx_ref[...] = dx.astype(dx_ref.dtype)
            for j, dw in enumerate(dws):
                dw_ref[j:j + 1, :] = dw

    hblk = pl.BlockSpec((t, DN_DH), lambda h: (0, h))
    wout = pl.BlockSpec((4, DN_DH), lambda h: (0, h))
    return pl.pallas_call(
        body, grid=(DN_HEADS,),
        in_specs=proj_specs + w_specs + [gb_spec, pl.BlockSpec((1, DN_DH), lambda h: (0, 0)), hblk,
                                         pl.BlockSpec((1, nc, DN_DH, DN_DH), lambda h: (h, 0, 0, 0)), hblk] + loc_specs,
        out_specs=[hblk] * 4 + [wout] * 3 + [gb_spec, pl.BlockSpec((1, 1, DN_DH), lambda h: (h, 0, 0))],
        out_shape=[jax.ShapeDtypeStruct((t, BW), BF16)] * 4 + [jax.ShapeDtypeStruct((4, BW), F32)] * 3
        + [jax.ShapeDtypeStruct((DN_HEADS, t, 2), F32), jax.ShapeDtypeStruct((DN_HEADS, 1, DN_DH), F32)],
        scratch_shapes=[pltpu.VMEM((t, DN_DH), F32)] * 4 + _dn_loc_scratch(t, nc), name=name,
    )(proj, proj, proj, proj, conv_w, conv_w, conv_w, gb, ng, o, states, dy, *loc_saved)


MERGE_TM, MERGE_TN = 512, 512


def _merge_specs(t):
    tm, tn = min(MERGE_TM, t), MERGE_TN
    y_spec = pl.BlockSpec((tm, BW), lambda i, j: (i, 0))
    w_spec = pl.BlockSpec((3, BW, tn), lambda i, j: (0, 0, j))
    gate_specs = [pl.BlockSpec((tm, tn), functools.partial(lambda i, j, o: (i, o + j), o=(C_GATE + n * D_MODEL) // tn))
                  for n in range(3)]
    return tm, tn, [y_spec] * 3 + [w_spec] + gate_specs


def _merge_fwd(ys, wb, proj, name):
    t = proj.shape[0]
    tm, tn, in_specs = _merge_specs(t)

    def body(y0, y1, y2, w_ref, g0, g1, g2, o_ref):
        acc = jnp.zeros((tm, tn), F32)
        for n, (y, g) in enumerate(((y0, g0), (y1, g1), (y2, g2))):
            acc = acc + _dot(y[...].astype(BF16), w_ref[n], NN) * _sigmoid(g[...])
        o_ref[...] = acc.astype(o_ref.dtype)

    return pl.pallas_call(
        body, grid=(t // tm, D_MODEL // tn), in_specs=in_specs,
        out_specs=pl.BlockSpec((tm, tn), lambda i, j: (i, j)),
        out_shape=jax.ShapeDtypeStruct((t, D_MODEL), BF16), name=name,
    )(*ys, wb, proj, proj, proj)


def _merge_bwd(ys, wb, proj, dmerged, name):
    t = proj.shape[0]
    tm, tn, in_specs = _merge_specs(t)

    def body(y0, y1, y2, w_ref, g0, g1, g2, dm_ref, dg_ref, dt_ref):
        dm = dm_ref[...]
        for n, (y, g) in enumerate(((y0, g0), (y1, g1), (y2, g2))):
            tn_ = _dot(y[...].astype(BF16), w_ref[n], NN)
            sg = _sigmoid(g[...])
            dg_ref[n] = (dm * tn_ * sg * (1.0 - sg)).astype(dg_ref.dtype)
            dt_ref[n] = (dm * sg).astype(dt_ref.dtype)

    o3 = pl.BlockSpec((3, tm, tn), lambda i, j: (0, i, j))
    return pl.pallas_call(
        body, grid=(t // tm, D_MODEL // tn), in_specs=in_specs + [pl.BlockSpec((tm, tn), lambda i, j: (i, j))],
        out_specs=[o3, o3], out_shape=[jax.ShapeDtypeStruct((3, t, D_MODEL), BF16)] * 2, name=name,
    )(*ys, wb, proj, proj, proj, dmerged)


def _prep_rows(sp):
    z4, z112 = jnp.zeros((1, 4), F32), jnp.zeros((1, 112), F32)
    bias_row = jnp.concatenate([sp['b_fox_f'], z4, sp['dn_dt_bias'], z112], axis=1)
    alog_row = jnp.concatenate([jnp.zeros((1, 12), F32), sp['dn_a_log'], z112], axis=1)
    return bias_row, alog_row


def _sc_fwd_tile(sb, sc, sv, w):
    return (sb * _conv(sc * sv, w),)


def _ffn_act(ug, uv):
    return _silu(ug) * uv


W_IN_COLS = [(C_FQ, 0, 1536), (C_SB, 1544, 1536), (C_DQ, 3080, 1536), (C_DZ, 4624, 512), (C_GATE, 5136, 3072),
             (C_SMALL, 1536, 8), (C_SMALL + 8, 4616, 8)]
W_IN_SHARD = IN_ORIG // N_DEV


def _w_in_segments():
    out = []
    for d, o, w in W_IN_COLS:
        end = o + w
        while o < end:
            k = o // W_IN_SHARD
            n = min(end, (k + 1) * W_IN_SHARD) - o
            out.append((d, k, o - k * W_IN_SHARD, n))
            o, d = o + n, d + n
    return out


def _w_in_assemble(g, name):
    tm = 256

    def body(g_ref, o_ref):
        for d, k, s, n in _w_in_segments():
            o_ref[:, d:d + n] = g_ref[k, :, s:s + n]
        o_ref[:, IN_ORIG:IN_P] = jnp.zeros((tm, IN_P - IN_ORIG), o_ref.dtype)

    return pl.pallas_call(
        body, grid=(D_MODEL // tm,),
        in_specs=[pl.BlockSpec((N_DEV, tm, W_IN_SHARD), lambda i: (0, i, 0))],
        out_specs=pl.BlockSpec((tm, IN_P), lambda i: (i, 0)),
        out_shape=jax.ShapeDtypeStruct((D_MODEL, IN_P), g.dtype), name=name,
    )(g)


W_IN_SPLIT_TM = 128


def _w_in_split_by_device(g, name):
    tm = W_IN_SPLIT_TM

    def body(g_ref, own_ref, send_ref):
        me = 4 * lax.axis_index("x") + 2 * lax.axis_index("y") + lax.axis_index("c")
        for d, k, s, n in _w_in_segments():
            val = g_ref[:, d:d + n]
            send_ref[k, :, s:s + n] = val.astype(send_ref.dtype)

            @pl.when(me == k)
            def _():
                own_ref[:, s:s + n] = val

    return pl.pallas_call(
        body, grid=(D_MODEL // tm,), in_specs=[pl.BlockSpec((tm, IN_P), lambda i: (i, 0))],
        out_specs=[pl.BlockSpec((tm, W_IN_SHARD), lambda i: (i, 0)), pl.BlockSpec((N_DEV, tm, W_IN_SHARD), lambda i: (0, i, 0))],
        out_shape=[jax.ShapeDtypeStruct((D_MODEL, W_IN_SHARD), F32), jax.ShapeDtypeStruct((N_DEV, D_MODEL, W_IN_SHARD), BF16)],
        name=name,
    )(g)


def _layer_fwd(x, p_i, w, sp, tag, late_weights=None):
    t = x.shape[0]
    sv = {'x': x}
    (hn,) = _ew(lambda a, g: (_rms(a, g),), [x], [sp['g_mix']], [(D_MODEL, BF16)], name=f'rms_mix_{tag}')
    proj = _mm(hn, w['w_in'], 'nn', [F32], name=f'in_proj_{tag}')
    bias_row, alog_row = _prep_rows(sp)
    aux = _prep_fwd(proj, bias_row, alog_row, f'prep_{tag}')
    cf = aux[:, :FOX_HEADS].T
    cfc, cfr = cf[:, :, None], cf[:, None, :]
    y_fox, lse = _att_fwd(proj, sp['fox_q_gain'], sp['fox_k_gain'], cfc, cfr, f'fox_fwd_{tag}')
    (y_sc,) = _cb(_sc_fwd_tile, [(proj, C_SB), (proj, C_SC), (proj, C_SV)], [(w['sc_conv_w'], 0)], [F32], [],
                  tc=256, nblk=2, name=f'sc_fwd_{tag}')
    gb = jnp.stack([aux[:, 12:16].T, aux[:, 8:12].T], axis=-1)
    y_dn, o_dn, states, *dn_loc = _dn_fwd(proj, w['dn_conv_w'], gb, sp['dn_norm_gain'], f'dn_fwd_{tag}')
    ys = (y_fox, y_sc, y_dn)
    if late_weights is not None:
        w = {**w, **late_weights(y_dn)}
    merged = _merge_fwd(ys, w['w_branch'], proj, f'merge_fwd_{tag}')
    x1 = _mm(merged, w['w_o'], 'nn', [F32], epi=lambda acc, r: (acc + r,), epi_args=(x,), name=f'o_proj_{tag}')
    (hf,) = _ew(lambda a, g: (_rms(a, g),), [x1], [sp['g_ffn']], [(D_MODEL, BF16)], name=f'rms_ffn_{tag}')
    up = _mm(hf, w['w_up'], 'nt', [F32], name=f'up_proj_{tag}')
    (act,) = _cb(lambda ug, uv, wg, wv: (_ffn_act(_conv(ug, wg), _conv(uv, wv)),), [(up, 0), (up, D_FF)],
                 [(w['ffn_conv_w'], 0), (w['ffn_conv_w'], D_FF)], [BF16], [], tc=256, nblk=D_FF // 256, name=f'ffn_act_{tag}')
    x2 = _mm(act, w['w_down'], 'nn', [F32], epi=lambda acc, r: (acc + r,), epi_args=(x1,), name=f'down_proj_{tag}')
    (hp,) = _ew(lambda a, g: (_rms(a, g),), [x2], [sp['g_ple']], [(D_MODEL, BF16)], name=f'rms_ple_{tag}')
    gp = _mm(hp, w['w_ple_gate'], 'nn', [F32], name=f'ple_gate_{tag}')
    x3 = _mm(p_i, w['w_ple'], 'nn', [F32], epi=lambda acc, g, r: (r + _sigmoid(g) * acc,), epi_args=(gp, x2), name=f'ple_{tag}')
    sv.update(hn=hn, proj=proj, aux=aux, cfc=cfc, cfr=cfr, lse=lse, ys=ys, gb=gb, o_dn=o_dn,
              states=states, merged=merged, x1=x1, hf=hf, up=up, act=act, x2=x2, hp=hp, gp=gp, p=p_i,
              bias_row=bias_row, alog_row=alog_row, w=w, dn_loc=dn_loc)
    return x3, sv


def _rms_bwd(x, g, dh, dres, name):
    def fn(xv, dhv, dr, gv):
        _, vjp = jax.vjp(_rms, xv, gv)
        dx, dg = vjp(dhv)
        return dr + dx, dg
    return _ew(fn, [x, dh, dres], [g], [(D_MODEL, F32)], [(1, D_MODEL)], name=name)


def _layer_bwd(dx3, sv, sp, tag, early_grads=None, last_grad=None):
    t = dx3.shape[0]
    w = sv['w']
    g = {}
    def ple_epi(acc, gpv, d):
        s = _sigmoid(gpv)
        return d * acc * s * (1.0 - s), d * s
    dgp, de = _mm(sv['p'], w['w_ple'], 'nn', [BF16, BF16], epi=ple_epi, epi_args=(sv['gp'], dx3), name=f'ple_bwd_{tag}')
    g['w_ple'] = _mm(sv['p'], de, 'tn', [F32], name=f'd_w_ple_{tag}')
    g['w_ple_gate'] = _mm(sv['hp'], dgp, 'tn', [F32], name=f'd_w_ple_gate_{tag}')
    dhp = _mm(dgp, w['w_ple_gate'], 'nt', [F32], name=f'd_hp_{tag}')
    dx2, g['g_ple'] = _rms_bwd(sv['x2'], sp['g_ple'], dhp, dx3, f'rms_ple_bwd_{tag}')
    dact = _mm(dx2, w['w_down'], 'nt', [F32], name=f'd_act_{tag}')
    g['w_down'] = _mm(sv['act'], dx2, 'tn', [F32], name=f'd_w_down_{tag}')

    def ffn_bwd_tile(ug, uv, da, wg, wv):
        cg, cv = _conv(ug, wg), _conv(uv, wv)
        _, vjp = jax.vjp(_ffn_act, cg, cv)
        dcg, dcv = vjp(da)
        dug, dwg = _conv_bwd(ug, wg, dcg)
        duv, dwv = _conv_bwd(uv, wv, dcv)
        return dug, duv, jnp.concatenate(dwg, axis=0), jnp.concatenate(dwv, axis=0)
    dupg, dupv, dwg, dwv = _cb(ffn_bwd_tile, [(sv['up'], 0), (sv['up'], D_FF), (dact, 0)],
                               [(w['ffn_conv_w'], 0), (w['ffn_conv_w'], D_FF)], [BF16, BF16], [3, 3], tc=256,
                               nblk=D_FF // 256, name=f'ffn_act_bwd_{tag}')
    dup = jnp.concatenate([dupg, dupv], axis=1)
    g['ffn_conv_w'] = jnp.concatenate([dwg, dwv], axis=1)
    g['w_up'] = _mm(dup, sv['hf'], 'tn', [F32], name=f'd_w_up_{tag}')
    dhf = _mm(dup, w['w_up'], 'nn', [F32], name=f'd_hf_{tag}')
    dx1, g['g_ffn'] = _rms_bwd(sv['x1'], sp['g_ffn'], dhf, dx2, f'rms_ffn_bwd_{tag}')
    dmerged = _mm(dx1, w['w_o'], 'nt', [F32], name=f'd_merged_{tag}')
    g['w_o'] = _mm(sv['merged'], dx1, 'tn', [F32], name=f'd_w_o_{tag}')
    dgate, dtn = _merge_bwd(sv['ys'], w['w_branch'], sv['proj'], dmerged, f'merge_bwd_{tag}')
    dys, dwb = [], []
    for n in range(3):
        dys.append(_mm(dtn[n], w['w_branch'][n], 'nt', [F32], name=f'd_y{n}_{tag}'))
        dwb.append(_mm(sv['ys'][n], dtn[n], 'tn', [F32], name=f'd_w_branch{n}_{tag}'))
    g['w_branch'] = jnp.stack(dwb)
    if early_grads is not None:
        sp = early_grads(g, sp)
    ddq, ddk, ddv, ddz, dwq, dwk, dwv_, dgb, dng = _dn_bwd(sv['proj'], w['dn_conv_w'], sv['gb'], sp['dn_norm_gain'],
                                                           sv['o_dn'], sv['states'], sv['dn_loc'], dys[2], f'dn_bwd_{tag}')
    g['dn_conv_w'] = jnp.concatenate([dwq, dwk, dwv_], axis=1)
    g['dn_norm_gain'] = jnp.sum(dng, axis=0)
    def sc_bwd_tile(sb, sc, svv, dy, wv):
        u = sc * svv
        dsb = dy * _conv(u, wv)
        du, dws = _conv_bwd(u, wv, dy * sb)
        return dsb, du * svv, du * sc, jnp.concatenate(dws, axis=0)
    dsb, dsc, dsv, g['sc_conv_w'] = _cb(sc_bwd_tile, [(sv['proj'], C_SB), (sv['proj'], C_SC), (sv['proj'], C_SV), (dys[1], 0)],
                                        [(w['sc_conv_w'], 0)], [BF16, BF16, BF16], [3], tc=256, nblk=2, name=f'sc_bwd_{tag}')
    dfq, dfk, dfv, dcfc, dcfr, dqg, dkg = _att_bwd(sv['proj'], sp['fox_q_gain'], sp['fox_k_gain'], sv['cfc'], sv['cfr'],
                                                   sv['lse'], sv['ys'][0], dys[0], f'fox_bwd_{tag}')
    g['fox_q_gain'] = jnp.sum(dqg, axis=0)
    g['fox_k_gain'] = jnp.sum(dkg, axis=0)
    dcf = (dcfc[:, :, 0] + dcfr[:, 0, :]).T
    daux = jnp.concatenate([dcf, dgb[:, :, 1].T, dgb[:, :, 0].T, jnp.zeros((t, 112), F32)], axis=1)
    dsmall, dbias, dalog = _prep_bwd(sv['proj'], sv['bias_row'], sv['alog_row'], daux, f'prep_bwd_{tag}')
    g['b_fox_f'] = dbias[:, 0:8]
    g['dn_dt_bias'] = dbias[:, 12:16]
    g['dn_a_log'] = dalog[:, 12:16]
    dproj = jnp.concatenate([dfq, dfk, dfv, dsb, dsc, dsv, ddq, ddk, ddv, ddz, dgate[0], dgate[1], dgate[2], dsmall], axis=1)
    g['w_in'] = _mm(sv['hn'], dproj, 'tn', [F32], name=f'd_w_in_{tag}')
    if last_grad is not None:
        sp = last_grad(g, sp)
    dhn = _mm(dproj, w['w_in'], 'nt', [F32], epi=lambda acc, z: (acc + z,),
              epi_args=(sp.get('zero_row', jnp.zeros((1, D_MODEL), F32)),), name=f'd_hn_{tag}')
    dx, g['g_mix'] = _rms_bwd(sv['x'], sp['g_mix'], dhn, dx1, f'rms_mix_bwd_{tag}')
    return dx, g


def _loss_bwd(y, target, name):
    inv = 1.0 / y.shape[1]

    def fn(yv, tv):
        err = yv - tv
        return err * inv, jnp.zeros((8, 128), F32) + 0.5 * inv * jnp.sum(err * err)
    return _ew(fn, [y, target], [], [(y.shape[1], F32)], [(8, 128)], name=name)


PACK_W = 1024
FULL_SHAPE = {'w_in': (D_MODEL, IN_ORIG), 'w_branch': (3, BW, D_MODEL), 'w_o': (D_MODEL, D_MODEL), 'w_up': (2 * D_FF, D_MODEL),
              'w_down': (D_FF, D_MODEL), 'w_ple_gate': (D_MODEL, D_MODEL), 'w_ple': (PLE_DIM, D_MODEL),
              'sc_conv_w': (3, BW), 'dn_conv_w': (4, 3 * BW), 'ffn_conv_w': (3, 2 * D_FF)}
SMALL_SHAPE = {'g_mix': D_MODEL, 'b_fox_f': FOX_HEADS, 'fox_q_gain': FOX_DH, 'fox_k_gain': FOX_DH, 'dn_a_log': DN_HEADS,
               'dn_dt_bias': DN_HEADS, 'dn_norm_gain': DN_DH, 'g_ffn': D_MODEL, 'g_ple': D_MODEL}


def _shard_shape(name):
    s = list(FULL_SHAPE[name])
    s[SHARDED[name]] //= N_DEV
    return tuple(s)


def _full_from_gathered(g, name):
    sh, ax = _shard_shape(name), SHARDED[name]
    blocks = jnp.moveaxis(g, 0, ax)
    return blocks.reshape(sh[:ax] + (N_DEV * sh[ax],) + sh[ax + 1:])


def _by_dest(full, name):
    sh, ax = _shard_shape(name), SHARDED[name]
    return jnp.moveaxis(full.reshape(sh[:ax] + (N_DEV, sh[ax]) + sh[ax + 1:]), ax, 0)


def _flat_pack(arrs):
    flat = jnp.concatenate([a.reshape(-1).astype(F32) for a in arrs])
    rows = -(-flat.shape[0] // (8 * PACK_W)) * 8
    return jnp.pad(flat, (0, rows * PACK_W - flat.shape[0])).reshape(rows, PACK_W)


def _flat_unpack(pack, shapes):
    flat, out, off = pack.reshape(-1), [], 0
    for s in shapes:
        n = 1
        for d in s:
            n *= d
        out.append(flat[off:off + n].reshape(s))
        off += n
    return out


SMALL_SHAPES = [(DEPTH, SMALL_SHAPE[n]) for n in SMALL]
SMALL_LOSS_AT = sum(DEPTH * SMALL_SHAPE[n] for n in SMALL)
CONV_SHARD_SHAPES = [(DEPTH,) + _shard_shape(n) for n in CONVW]
CONV_FULL_SHAPES = [(DEPTH,) + FULL_SHAPE[n] for n in CONVW]


def _pick_tm(m, width):
    best = None
    for tm in range(16, m + 1, 16):
        if m % tm == 0 and tm * width * 4 <= (1 << 20):
            best = tm
    return best if best is not None else m


HBM_SPEC = pl.BlockSpec(memory_space=pltpu.HBM)


def _place():
    x, y, c = lax.axis_index("x"), lax.axis_index("y"), lax.axis_index("c")
    return x, y, c, [(1 - x, y), (x, 1 - y), (1 - x, 1 - y)]


def _all_gather(arrs, name):
    n = len(arrs)

    def body(*refs):
        ins, outs = refs[:n], refs[n:2 * n]
        send_sems, recv_sems = refs[2 * n:]
        x, y, c, chips = _place()
        me, sibling = (x, y, c), (x, y, 1 - c)

        def block(a, p):
            return outs[a].at[4 * p[0] + 2 * p[1] + p[2]]

        def copy(a, k, blk, to, src=None):
            return pltpu.make_async_remote_copy(src_ref=block(a, blk) if src is None else src, dst_ref=block(a, blk),
                                                send_sem=send_sems.at[a, k], recv_sem=recv_sems.at[a, k],
                                                device_id=to, device_id_type=MESH)

        first, passed = [], []
        for a in range(n):
            first.append(copy(a, 0, me, sibling, src=ins[a]))
            first += [copy(a, 1 + j, me, (*chip, c), src=ins[a]) for j, chip in enumerate(chips)]
        for cp in first:
            cp.start()
        for j, chip in enumerate(chips):
            for a in range(n):
                copy(a, 1 + j, (*chip, c), me).wait_recv()
                fwd = copy(a, 4 + j, (*chip, c), sibling)
                fwd.start()
                passed.append(fwd)
        for a in range(n):
            copy(a, 0, sibling, me).wait_recv()
            for j, chip in enumerate(chips):
                copy(a, 4 + j, (*chip, 1 - c), me).wait_recv()
        for cp in first + passed:
            cp.wait_send()

    gathered = pl.pallas_call(
        body, in_specs=[HBM_SPEC] * n, out_specs=[HBM_SPEC] * n,
        out_shape=[jax.ShapeDtypeStruct((N_DEV,) + a.shape, a.dtype) for a in arrs],
        scratch_shapes=[pltpu.SemaphoreType.DMA((n, 7)), pltpu.SemaphoreType.DMA((n, 7))], name=name,
    )(*arrs)
    me = 4 * lax.axis_index("x") + 2 * lax.axis_index("y") + lax.axis_index("c")
    return [lax.dynamic_update_index_in_dim(g, a, me, 0) for g, a in zip(gathered, arrs)]


SEM_SPEC = pl.BlockSpec(memory_space=pltpu.SEMAPHORE)
DATAFLOW = pltpu.SideEffectType.DATAFLOW_SIDE_EFFECTING
ALL_PEERS = (1, 2, 3, 4, 5, 6, 7)
NEAR_PEERS = (1, 4, 2, 6)


def _direct_copies(src_refs, zone_refs, send_sems, recv_sems, per_dest, peers):
    x, y, c, _ = _place()
    me = 4 * x + 2 * y + c
    cps = []
    for a, (src, zone) in enumerate(zip(src_refs, zone_refs)):
        for r, bits in enumerate(peers):
            px = 1 - x if bits & 4 else x
            py = 1 - y if bits & 2 else y
            pc = 1 - c if bits & 1 else c
            cps.append(pltpu.make_async_remote_copy(
                src_ref=src.at[4 * px + 2 * py + pc] if per_dest else src, dst_ref=zone.at[me],
                send_sem=send_sems.at[a * len(peers) + r], recv_sem=recv_sems.at[a * len(peers) + r],
                device_id=(px, py, pc), device_id_type=MESH))
    return cps


def _direct_start(srcs, per_dest, after, name, peers=ALL_PEERS):
    n = len(srcs)
    zones = [lax.empty((N_DEV,) + (s.shape[1:] if per_dest else s.shape), s.dtype) for s in srcs]

    def body(*refs):
        src_refs, zone_refs = refs[:n], refs[n:2 * n]
        send_sems, recv_sems = refs[2 * n + 1:2 * n + 3]
        for cp in _direct_copies(src_refs, zone_refs, send_sems, recv_sems, per_dest, peers):
            cp.start()
        refs[-1][...] = jnp.zeros_like(refs[-1])

    sems = pltpu.SemaphoreType.DMA((n * len(peers),))
    res = pl.pallas_call(
        body, name=name, in_specs=[HBM_SPEC] * (2 * n + 1),
        out_shape=[sems, sems] + [pltpu.HBM(s.shape, s.dtype) for s in srcs] + [pltpu.HBM(z.shape, z.dtype) for z in zones]
        + [jax.ShapeDtypeStruct((8, 128), F32)],
        out_specs=[SEM_SPEC, SEM_SPEC] + [HBM_SPEC] * (2 * n) + [pl.BlockSpec(memory_space=pltpu.VMEM)],
        input_output_aliases={i: 2 + i for i in range(2 * n)},
        compiler_params=pltpu.CompilerParams(has_side_effects=DATAFLOW),
    )(*[pltpu.with_memory_space_constraint(s, pltpu.HBM) for s in srcs],
      *[pltpu.with_memory_space_constraint(z, pltpu.HBM) for z in zones], after)
    return (res[0], res[1], list(res[2:2 + n]), list(res[2 + n:2 + 2 * n]), per_dest, peers), res[-1][0:1, 0:1]


def _direct_wait(started, after, name):
    send_sems, recv_sems, srcs, zones, per_dest, peers = started
    n = len(srcs)

    def body(*refs):
        src_refs, zone_refs = refs[:n], refs[n:2 * n]
        for cp in _direct_copies(src_refs, zone_refs, refs[2 * n], refs[2 * n + 1], per_dest, peers):
            cp.wait_send()
            cp.wait_recv()

    res = pl.pallas_call(
        body, name=name, in_specs=[HBM_SPEC] * (2 * n) + [SEM_SPEC, SEM_SPEC, HBM_SPEC],
        out_shape=[pltpu.HBM(s.shape, s.dtype) for s in srcs] + [pltpu.HBM(z.shape, z.dtype) for z in zones],
        out_specs=[HBM_SPEC] * (2 * n), input_output_aliases={i: i for i in range(2 * n)},
        compiler_params=pltpu.CompilerParams(has_side_effects=DATAFLOW),
    )(*srcs, *zones, send_sems, recv_sems, after)
    return list(res[n:])


def _gather_finish(zones, name):
    n = len(zones)

    def body(*refs):
        zone_in, zone_out = refs[:n], refs[n:2 * n]
        send_sems, recv_sems = refs[2 * n:]
        x, y, c, chips = _place()
        cps = []
        for a in range(n):
            for j, (cx, cy) in enumerate(chips):
                blk = 4 * cx + 2 * cy + c
                cp = pltpu.make_async_remote_copy(src_ref=zone_in[a].at[blk], dst_ref=zone_out[a].at[blk],
                                                  send_sem=send_sems.at[a, j], recv_sem=recv_sems.at[a, j],
                                                  device_id=(x, y, 1 - c), device_id_type=MESH)
                cp.start()
                cps.append(cp)
        for cp in cps:
            cp.wait()

    return pl.pallas_call(
        body, in_specs=[HBM_SPEC] * n, out_specs=[HBM_SPEC] * n,
        out_shape=[jax.ShapeDtypeStruct(z.shape, z.dtype) for z in zones], input_output_aliases={a: a for a in range(n)},
        scratch_shapes=[pltpu.SemaphoreType.DMA((n, 3)), pltpu.SemaphoreType.DMA((n, 3))], name=name,
    )(*zones)


def _adamw(w, g, m, v):
    m = ADAM_B1 * m + (1.0 - ADAM_B1) * g
    v = ADAM_B2 * v + (1.0 - ADAM_B2) * jnp.square(g)
    m_hat = m / (1.0 - ADAM_B1 ** ADAM_STEP)
    v_hat = v / (1.0 - ADAM_B2 ** ADAM_STEP)
    delta = -ADAM_LR * (m_hat / (jnp.sqrt(v_hat) + ADAM_EPS) + ADAM_WD * w)
    return delta, m, v


def _adamw_layer(w, m, v, own, parts, layer, prev, name):
    _, rows, c = w.shape
    tm = _pick_tm(rows, c)

    def body(w_ref, m_ref, v_ref, own_ref, parts_ref, *rest):
        g = own_ref[...]
        for j in range(parts_ref.shape[0]):
            g = g + parts_ref[j].astype(F32)
        delta, m2, v2 = _adamw(w_ref[...], g, m_ref[...], v_ref[...])
        g_ref, d_ref, m2_ref, v2_ref = rest[-4:]
        g_ref[...], d_ref[...], m2_ref[...], v2_ref[...] = g, delta, m2, v2

    wspec = pl.BlockSpec((None, tm, c), lambda i: (layer, i, 0))
    prev = list(prev) if prev is not None else []
    return pl.pallas_call(
        body, grid=(rows // tm,),
        in_specs=[wspec] * 3 + [pl.BlockSpec((tm, c), lambda i: (i, 0)), pl.BlockSpec((parts.shape[0], tm, c), lambda i: (0, i, 0))]
        + [HBM_SPEC] * len(prev),
        out_specs=[wspec] * 4, out_shape=[jax.ShapeDtypeStruct(w.shape, F32)] * 4,
        input_output_aliases={5 + k: k for k in range(len(prev))}, name=name,
    )(w, m, v, own, parts, *prev)


W_IN_ADAMW_TM = 128


def _adamw_w_in(w_t, m_t, v_t, grads, name):
    tm = W_IN_ADAMW_TM
    ng = [p.shape[0] for _, p in grads]

    def body(w_ref, m_ref, v_ref, *rest):
        outs = rest[2 * DEPTH:]
        for layer in range(DEPTH):
            own_ref, parts_ref = rest[2 * layer], rest[2 * layer + 1]
            g = own_ref[...]
            for j in range(ng[layer]):
                g = g + parts_ref[j].astype(F32)
            delta, m2, v2 = _adamw(w_ref[:, layer, :].T, g, m_ref[:, layer, :].T, v_ref[:, layer, :].T)
            for o_ref, val in zip(outs, (g, delta, m2, v2)):
                o_ref[:, layer, :] = val.T

    tspec = pl.BlockSpec((W_IN_SHARD, DEPTH, tm), lambda i: (0, 0, i))
    gspecs = []
    for _, parts in grads:
        gspecs += [pl.BlockSpec((tm, W_IN_SHARD), lambda i: (i, 0)), pl.BlockSpec((parts.shape[0], tm, W_IN_SHARD), lambda i: (0, i, 0))]
    return pl.pallas_call(
        body, grid=(D_MODEL // tm,), in_specs=[tspec] * 3 + gspecs, out_specs=[tspec] * 4,
        out_shape=[jax.ShapeDtypeStruct(w_t.shape, F32)] * 4, name=name,
    )(w_t, m_t, v_t, *[x for pair in grads for x in pair])


def _adamw_call(w, m, v, gparts, name):
    def fn(wv, mv, vv, *gs):
        g = gs[0].astype(F32)
        for gp in gs[1:]:
            g = g + gp.astype(F32)
        delta, m2, v2 = _adamw(wv, g, mv, vv)
        return g, delta, m2, v2
    rows, c = w.shape
    return _ew(fn, [w, m, v] + list(gparts), [], [(c, F32)] * 4, tm=_pick_tm(rows, c), name=name)


def kernel(x, p, g_mix, w_in, b_fox_f, fox_q_gain, fox_k_gain, sc_conv_w, dn_conv_w, dn_a_log, dn_dt_bias,
           dn_norm_gain, w_branch, w_o, g_ffn, w_up, ffn_conv_w, w_down, g_ple, w_ple_gate, w_ple, loss_target,
           m_g_mix, m_w_in, m_b_fox_f, m_fox_q_gain, m_fox_k_gain, m_sc_conv_w, m_dn_conv_w, m_dn_a_log,
           m_dn_dt_bias, m_dn_norm_gain, m_w_branch, m_w_o, m_g_ffn, m_w_up, m_ffn_conv_w, m_w_down, m_g_ple,
           m_w_ple_gate, m_w_ple, v_g_mix, v_w_in, v_b_fox_f, v_fox_q_gain, v_fox_k_gain, v_sc_conv_w, v_dn_conv_w,
           v_dn_a_log, v_dn_dt_bias, v_dn_norm_gain, v_w_branch, v_w_o, v_g_ffn, v_w_up, v_ffn_conv_w, v_w_down,
           v_g_ple, v_w_ple_gate, v_w_ple):
    return _step(x, p, g_mix, w_in, b_fox_f, fox_q_gain, fox_k_gain, sc_conv_w, dn_conv_w, dn_a_log, dn_dt_bias,
                 dn_norm_gain, w_branch, w_o, g_ffn, w_up, ffn_conv_w, w_down, g_ple, w_ple_gate, w_ple, loss_target,
                 m_g_mix, m_w_in, m_b_fox_f, m_fox_q_gain, m_fox_k_gain, m_sc_conv_w, m_dn_conv_w, m_dn_a_log,
                 m_dn_dt_bias, m_dn_norm_gain, m_w_branch, m_w_o, m_g_ffn, m_w_up, m_ffn_conv_w, m_w_down, m_g_ple,
                 m_w_ple_gate, m_w_ple, v_g_mix, v_w_in, v_b_fox_f, v_fox_q_gain, v_fox_k_gain, v_sc_conv_w,
                 v_dn_conv_w, v_dn_a_log, v_dn_dt_bias, v_dn_norm_gain, v_w_branch, v_w_o, v_g_ffn, v_w_up,
                 v_ffn_conv_w, v_w_down, v_g_ple, v_w_ple_gate, v_w_ple)


def _step(*args):
    names = ['x', 'p'] + WEIGHTS + ['loss_target'] + ['m_' + n for n in WEIGHTS] + ['v_' + n for n in WEIGHTS]
    assert len(args) == len(names)
    a = dict(zip(names, args))
    for n, perm in HELD_TRANSPOSED.items():
        for k in (n, 'm_' + n, 'v_' + n):
            a[k] = jnp.transpose(a[k], perm)
    x, target = a['x'][0], a['loss_target'][0]
    p = a['p'][:, 0]
    dev = 4 * lax.axis_index("x") + 2 * lax.axis_index("y") + lax.axis_index("c")

    LATE = [n for n in BIG if n != 'w_in']
    sps = [{n: a[n][layer][None, :] for n in SMALL} for layer in range(DEPTH)]
    shards = [{n: a[n][layer].astype(BF16) for n in BIG} for layer in range(DEPTH)]

    def by_device(full, n):
        by_dest = _by_dest(full, n)
        return lax.dynamic_index_in_dim(by_dest, dev, axis=0, keepdims=False), by_dest.astype(BF16)

    def blank(zone):
        return lax.dynamic_update_index_in_dim(zone, jnp.zeros(zone.shape[1:], zone.dtype), dev, 0)

    w_in_all, conv_all = _all_gather([shards[0]['w_in'], _flat_pack([a[n] for n in CONVW])], 'gather_w_in_l0')
    conv_by_dev = [_flat_unpack(conv_all[d], CONV_SHARD_SHAPES) for d in range(N_DEV)]
    conv_full = {n: jnp.concatenate([conv_by_dev[d][i] for d in range(N_DEV)], axis=2) for i, n in enumerate(CONVW)}
    late0_started, zero = _direct_start([shards[0][n] for n in LATE], False, w_in_all, 'weights_rest_l0_start', NEAR_PEERS)
    w1_started, zero = _direct_start([shards[1][n] for n in BIG], False, late0_started[2][0], 'weights_l1_start', NEAR_PEERS)
    sps[0]['g_mix'] = sps[0]['g_mix'] + zero

    def late_weights0(after):
        zones = _gather_finish(_direct_wait(late0_started, after, 'weights_rest_l0_wait'), 'weights_rest_l0_finish')
        return {n: _full_from_gathered(lax.dynamic_update_index_in_dim(z, shards[0][n], dev, 0), n) for n, z in zip(LATE, zones)}

    w0 = {n: conv_full[n][0] for n in CONVW}
    w0['w_in'] = _w_in_assemble(w_in_all, 'w_in_assemble_l0')
    h, sv0 = _layer_fwd(x, p[0], w0, sps[0], 'l0', late_weights0)
    zones = [lax.dynamic_update_index_in_dim(z, shards[1][n], dev, 0)
             for n, z in zip(BIG, _gather_finish(_direct_wait(w1_started, h, 'weights_l1_wait'), 'weights_l1_finish'))]
    w1 = {n: _full_from_gathered(z, n) for n, z in zip(BIG, zones) if n != 'w_in'}
    w1.update({n: conv_full[n][1] for n in CONVW})
    w1['w_in'] = _w_in_assemble(zones[list(BIG).index('w_in')], 'w_in_assemble_l1')
    h, sv1 = _layer_fwd(h, p[1], w1, sps[1], 'l1')
    dh, loss_part = _loss_bwd(h, target, 'loss')
    grads = [None] * DEPTH
    dh, grads[1] = _layer_bwd(dh, sv1, sps[1], 'l1')
    owns1, sends1 = {}, {}
    for n in BIG:
        owns1[n], sends1[n] = _w_in_split_by_device(grads[1][n], 'w_in_grad_split_l1') if n == 'w_in' else by_device(grads[1][n], n)
    g1_started, zero = _direct_start([sends1[n] for n in BIG], True, dh, 'grads_l1_start')
    early = {}

    def early_grads0(g, sp):
        owns, sends = zip(*[by_device(g[n], n) for n in LATE])
        early['started'], zero = _direct_start(list(sends), True, g['w_branch'], 'grads_rest_l0_start')
        early['owns'] = dict(zip(LATE, owns))
        return {**sp, 'dn_norm_gain': sp['dn_norm_gain'] + zero}

    def last_grad0(g, sp):
        early['own_in'], send = _w_in_split_by_device(g['w_in'], 'w_in_grad_split_l0')
        early['in_started'], zero = _direct_start([send], True, g['w_in'], 'grads_w_in_l0_start')
        return {**sp, 'g_mix': sp['g_mix'] + zero, 'zero_row': jnp.zeros((1, D_MODEL), F32) + zero}

    dh, grads[0] = _layer_bwd(dh, sv0, {**sps[0], 'g_ple': sps[0]['g_ple'] + zero}, 'l0', early_grads0, last_grad0)
    grad_x = dh[None]
    own_in0, in0_started = early['own_in'], early['in_started']
    sent = in0_started[2][0]
    zones1 = dict(zip(BIG, _direct_wait(g1_started, sent, 'grads_l1_wait')))
    zones0 = dict(zip(LATE, _direct_wait(early['started'], sent, 'grads_rest_l0_wait')))

    def adamw(n, layer, own, zone, prev):
        rows = (-1, a[n].shape[-1])
        own = own.reshape(rows)
        view = lambda t: t.reshape((DEPTH,) + own.shape)
        return _adamw_layer(view(a[n]), view(a['m_' + n]), view(a['v_' + n]), own, blank(zone).reshape((N_DEV,) + own.shape),
                            layer, prev, f'adamw_{n}_l{layer}')

    out, done = {}, []
    for n in LATE:
        res = adamw(n, 0, early['owns'][n], zones0[n], adamw(n, 1, owns1[n], zones1[n], None))
        out[n] = [r.reshape(a[n].shape) for r in res]
        done.append(res[0][0, 0:1, 0:1])
    (zone_in0,) = _direct_wait(in0_started, jnp.concatenate(done + [dh[0:1, 0:1]], axis=1), 'grads_w_in_l0_wait')
    res = _adamw_w_in(*[jnp.transpose(a[k], (2, 0, 1)) for k in ('w_in', 'm_w_in', 'v_w_in')],
                      [(own_in0, blank(zone_in0)), (owns1['w_in'], blank(zones1['w_in']))], 'adamw_w_in')
    out['w_in'] = [jnp.transpose(r, (1, 2, 0)) for r in res]
    last_update = res[0]

    small_pack = _flat_pack([jnp.stack([grads[layer][n].reshape(-1) for layer in range(DEPTH)]) for n in SMALL] + [loss_part[0, 0]])
    conv_pack = _flat_pack([jnp.stack([grads[layer][n] for layer in range(DEPTH)]) for n in CONVW])
    small_started, _ = _direct_start([small_pack, conv_pack], False, small_pack, 'small_grads_start')
    small_all, conv_all = [lax.dynamic_update_index_in_dim(z, s, dev, 0) for z, s in
                           zip(_direct_wait(small_started, last_update, 'small_grads_wait'), (small_pack, conv_pack))]
    res = _adamw_call(_flat_pack([a[n] for n in SMALL]), _flat_pack([a['m_' + n] for n in SMALL]),
                      _flat_pack([a['v_' + n] for n in SMALL]), [small_all[d] for d in range(N_DEV)], 'adamw_replicated')
    loss = res[0].reshape(-1)[SMALL_LOSS_AT]
    for k, r in enumerate(res):
        for n, val in zip(SMALL, _flat_unpack(r, SMALL_SHAPES)):
            out.setdefault(n, [None] * 4)[k] = val
    (conv_sum,) = _ew(lambda *gs: (functools.reduce(lambda s, t: s + t, gs),), [conv_all[d] for d in range(N_DEV)], [],
                      [(PACK_W, F32)], tm=conv_pack.shape[0], name='conv_grad_sum')
    conv_own = [lax.dynamic_slice_in_dim(g, dev * (g.shape[2] // N_DEV), g.shape[2] // N_DEV, axis=2)
                for g in _flat_unpack(conv_sum, CONV_FULL_SHAPES)]
    res = _adamw_call(_flat_pack([a[n] for n in CONVW]), _flat_pack([a['m_' + n] for n in CONVW]),
                      _flat_pack([a['v_' + n] for n in CONVW]), [_flat_pack(conv_own)], 'adamw_conv')
    for k, r in enumerate(res):
        for n, val in zip(CONVW, _flat_unpack(r, CONV_SHARD_SHAPES)):
            out.setdefault(n, [None] * 4)[k] = val

    for n, perm in HELD_TRANSPOSED.items():
        out[n] = [jnp.transpose(r, perm) for r in out[n]]
    outs = [loss, grad_x]
    for k in range(4):
        outs += [out[n][k] for n in WEIGHTS]
    return tuple(outs)
```

```python
import functools

import jax
import jax.numpy as jnp
from jax import lax
from jax.experimental import pallas as pl
from jax.experimental.pallas import tpu as pltpu

F32 = jnp.float32
BF16 = jnp.bfloat16
HI = lax.Precision.HIGHEST

D_MODEL = 1024
DEPTH = 2
N_DEV = 8
PLE_DIM = 256
BW = 512
FOX_HEADS, FOX_DH = 8, 64
DN_HEADS, DN_DH = 4, 128
DN_CHUNK = 64
D_FF = 2816
EPS = 1e-6
NEG = -1e30

ADAM_LR, ADAM_B1, ADAM_B2, ADAM_EPS, ADAM_WD, ADAM_STEP = 0.001, 0.9, 0.999, 1e-08, 0.01, 10

C_FQ, C_FK, C_FV = 0, 512, 1024
C_SB, C_SC, C_SV = 1536, 2048, 2560
C_DQ, C_DK, C_DV, C_DZ = 3072, 3584, 4096, 4608
C_GATE = 5120
C_SMALL = 8192
IN_P = 8320
IN_ORIG = 8208

WEIGHTS = ['g_mix', 'w_in', 'b_fox_f', 'fox_q_gain', 'fox_k_gain', 'sc_conv_w', 'dn_conv_w', 'dn_a_log',
           'dn_dt_bias', 'dn_norm_gain', 'w_branch', 'w_o', 'g_ffn', 'w_up', 'ffn_conv_w', 'w_down', 'g_ple',
           'w_ple_gate', 'w_ple']
BIG = {'w_in': 1, 'w_branch': 2, 'w_o': 0, 'w_up': 0, 'w_down': 0, 'w_ple_gate': 0, 'w_ple': 1}
HELD_TRANSPOSED = {'w_up': (0, 2, 1)}
CONVW = {'sc_conv_w': 1, 'dn_conv_w': 1, 'ffn_conv_w': 1}
SHARDED = {**BIG, **CONVW}
SMALL = [n for n in WEIGHTS if n not in SHARDED]
MESH = pl.DeviceIdType.MESH


def _sigmoid(x):
    return 0.5 * (jnp.tanh(0.5 * x) + 1.0)


def _silu(x):
    return x * _sigmoid(x)


def _log1pexp_negabs(z):
    return jnp.log(1.0 + jnp.exp(-jnp.abs(z)))


def _log_sigmoid(z):
    return jnp.minimum(z, 0.0) - _log1pexp_negabs(z)


def _softplus(z):
    return jnp.maximum(z, 0.0) + _log1pexp_negabs(z)


def _rms(x, g):
    return x * lax.rsqrt(jnp.mean(x * x, axis=-1, keepdims=True) + EPS) * g


def _l2(x):
    return x * lax.rsqrt(jnp.sum(x * x, axis=-1, keepdims=True) + EPS)


def _dot(a, b, dims, precision=None):
    return lax.dot_general(a, b, (dims, ((), ())), preferred_element_type=F32, precision=precision)


NN = ((1,), (0,))
NT = ((1,), (1,))
TN = ((0,), (0,))


def _shift_down(x, s):
    if s == 0:
        return x
    t = lax.broadcasted_iota(jnp.int32, x.shape, 0)
    return jnp.where(t >= s, pltpu.roll(x, s, 0), 0.0)


def _shift_up(x, s):
    if s == 0:
        return x
    n = x.shape[0]
    t = lax.broadcasted_iota(jnp.int32, x.shape, 0)
    return jnp.where(t < n - s, pltpu.roll(x, n - s, 0), 0.0)


def _conv(x, w):
    k = w.shape[0]
    y = w[k - 1:k] * x
    for j in range(k - 1):
        y = y + w[j:j + 1] * _shift_down(x, k - 1 - j)
    return y


def _conv_bwd(x, w, dy):
    k = w.shape[0]
    dx = w[k - 1:k] * dy
    dws = []
    for j in range(k - 1):
        dx = dx + w[j:j + 1] * _shift_up(dy, k - 1 - j)
        dws.append(jnp.sum(dy * _shift_down(x, k - 1 - j), axis=0, keepdims=True))
    dws.append(jnp.sum(dy * x, axis=0, keepdims=True))
    return dx, dws


MM_VMEM_BUDGET = 36 << 20
MM_STEP_BYTES = 1 << 20


def _mm_tiles(m, n, k, a_size, b_size, tile_size, cast_a):
    best = None
    for tm in [d for d in (2048, 1024, 512, 256, 128) if d <= m and m % d == 0] or [m]:
        for tn in [d for d in range(128, min(n, 2048) + 1, 128) if n % d == 0] or [n]:
            vmem = 2 * tm * k * a_size + (2 * tm * k if cast_a else 0) + 2 * k * tn * b_size + 2 * tm * tn * tile_size
            if vmem > MM_VMEM_BUDGET:
                continue
            steps = (m // tm) * (n // tn)
            cost = m * k * a_size + (m // tm) * k * n * b_size + m * n * tile_size + steps * MM_STEP_BYTES
            if best is None or cost < best[0]:
                best = (cost, tm, tn)
    assert best is not None, (m, n, k)
    return best[1], best[2]


def _mm(a, b, mode, outs, *, epi=None, epi_args=(), name):
    if mode == 'nn':
        (m, k), (k2, n) = a.shape, b.shape
    elif mode == 'nt':
        (m, k), (n, k2) = a.shape, b.shape
    else:
        (k, m), (k2, n) = a.shape, b.shape
    assert k == k2, (a.shape, b.shape, mode)
    tile_size = sum(jnp.dtype(dt).itemsize for dt in outs) + sum(e.dtype.itemsize for e in epi_args if e.shape[0] != 1)
    tm, tn = _mm_tiles(m, n, k, a.dtype.itemsize, b.dtype.itemsize, tile_size, a.dtype != BF16)
    dims = {'nn': NN, 'nt': NT, 'tn': TN}[mode]
    a_spec = pl.BlockSpec((k, tm), lambda i, j: (0, i)) if mode == 'tn' else pl.BlockSpec((tm, k), lambda i, j: (i, 0))
    b_spec = pl.BlockSpec((tn, k), lambda i, j: (j, 0)) if mode == 'nt' else pl.BlockSpec((k, tn), lambda i, j: (0, j))
    e_specs = [pl.BlockSpec((1, tn), lambda i, j: (0, j)) if e.shape[0] == 1 else pl.BlockSpec((tm, tn), lambda i, j: (i, j))
               for e in epi_args]
    ne, no = len(epi_args), len(outs)
    cast_a = a.dtype != BF16

    def body(a_ref, b_ref, *rest):
        if cast_a:
            a_sc = rest[-1]

            @pl.when(pl.program_id(1) == 0)
            def _():
                a_sc[...] = a_ref[...].astype(BF16)
            av = a_sc[...]
        else:
            av = a_ref[...]
        acc = _dot(av, b_ref[...].astype(BF16), dims)
        vals = epi(acc, *[e[...] for e in rest[:ne]]) if epi is not None else (acc,)
        for o_ref, v in zip(rest[ne:ne + no], vals):
            o_ref[...] = v.astype(o_ref.dtype)

    res = pl.pallas_call(
        body, grid=(m // tm, n // tn),
        in_specs=[a_spec, b_spec] + e_specs,
        out_specs=[pl.BlockSpec((tm, tn), lambda i, j: (i, j)) for _ in outs],
        out_shape=[jax.ShapeDtypeStruct((m, n), dt) for dt in outs],
        scratch_shapes=[pltpu.VMEM(a_spec.block_shape, BF16)] if cast_a else [],
        name=name,
    )(a, b, *epi_args)
    return res[0] if no == 1 else res


def _ew(fn, tiled, bcast, outs, reds=(), *, tm=256, name):
    secs = [(t, 0, t.shape[1]) if not isinstance(t, tuple) else t for t in tiled]
    m = secs[0][0].shape[0]
    tm = min(tm, m)
    assert m % tm == 0
    in_specs = []
    for arr, off, w in secs:
        assert off % w == 0
        in_specs.append(pl.BlockSpec((tm, w), functools.partial(lambda i, c: (i, c), c=off // w)))
    in_specs += [pl.BlockSpec(b.shape, lambda i: (0, 0)) for b in bcast]
    nin, no = len(in_specs), len(outs)

    def body(*refs):
        vals = fn(*[r[...] for r in refs[:nin]])
        for r, v in zip(refs[nin:nin + no], vals[:no]):
            r[...] = v.astype(r.dtype)
        i = pl.program_id(0)
        for r, v in zip(refs[nin + no:], vals[no:]):
            @pl.when(i == 0)
            def _():
                r[...] = v

            @pl.when(i > 0)
            def _():
                r[...] += v

    res = pl.pallas_call(
        body, grid=(m // tm,), in_specs=in_specs,
        out_specs=[pl.BlockSpec((tm, c), lambda i: (i, 0)) for c, _ in outs] + [pl.BlockSpec(s, lambda i: (0, 0)) for s in reds],
        out_shape=[jax.ShapeDtypeStruct((m, c), dt) for c, dt in outs] + [jax.ShapeDtypeStruct(s, F32) for s in reds],
        name=name,
    )(*[s[0] for s in secs], *bcast)
    return res


def _cb(fn, cols, params, outs, pouts, *, tc, nblk, name):
    t = cols[0][0].shape[0]
    in_specs = []
    for arr, off in cols:
        assert off % tc == 0
        in_specs.append(pl.BlockSpec((t, tc), functools.partial(lambda c, o: (0, o + c), o=off // tc)))
    for arr, off in params:
        in_specs.append(pl.BlockSpec((arr.shape[0], tc), functools.partial(lambda c, o: (0, o + c), o=off // tc)))
    nin, no = len(in_specs), len(outs)

    def body(*refs):
        vals = fn(*[r[...] for r in refs[:nin]])
        for r, v in zip(refs[nin:], vals):
            r[...] = v.astype(r.dtype)

    return pl.pallas_call(
        body, grid=(nblk,), in_specs=in_specs,
        out_specs=[pl.BlockSpec((t, tc), lambda c: (0, c)) for _ in outs] + [pl.BlockSpec((k, tc), lambda c: (0, c)) for k in pouts],
        out_shape=[jax.ShapeDtypeStruct((t, nblk * tc), dt) for dt in outs] + [jax.ShapeDtypeStruct((k, nblk * tc), F32) for k in pouts],
        name=name,
    )(*[c[0] for c in cols], *[p[0] for p in params])


CUM_BLK = 256


def _prep_point(s, bias, alog):
    lane = lax.broadcasted_iota(jnp.int32, s.shape, 1)
    z = s + bias
    return jnp.where(lane < 8, _log_sigmoid(z),
                     jnp.where(lane < 12, _sigmoid(s),
                               jnp.where(lane < 16, -jnp.exp(alog) * _softplus(z), 0.0)))


def _tri(n, upper):
    r = lax.broadcasted_iota(jnp.int32, (n, n), 0)
    c = lax.broadcasted_iota(jnp.int32, (n, n), 1)
    return (r <= c if upper else r >= c).astype(F32)


def _prep_fwd(proj, bias_row, alog_row, name):
    t = proj.shape[0]
    nb = t // CUM_BLK

    def body(s_ref, b_ref, a_ref, o_ref):
        pre = _prep_point(s_ref[...], b_ref[...], a_ref[...])
        lane = lax.broadcasted_iota(jnp.int32, (CUM_BLK, 128), 1)
        tri = _tri(CUM_BLK, False)
        carry = jnp.zeros((1, 128), F32)
        for blk in range(nb):
            xb = pre[blk * CUM_BLK:(blk + 1) * CUM_BLK]
            cb = _dot(tri, xb, NN, HI) + carry
            carry = cb[CUM_BLK - 1:CUM_BLK]
            o_ref[blk * CUM_BLK:(blk + 1) * CUM_BLK, :] = jnp.where(lane < 8, cb, xb)

    return pl.pallas_call(
        body, grid=(1,),
        in_specs=[pl.BlockSpec((t, 128), lambda i: (0, 0)), pl.BlockSpec((1, 128), lambda i: (0, 0)),
                  pl.BlockSpec((1, 128), lambda i: (0, 0))],
        out_specs=pl.BlockSpec((t, 128), lambda i: (0, 0)),
        out_shape=jax.ShapeDtypeStruct((t, 128), F32), name=name,
    )(proj, bias_row, alog_row)


def _prep_bwd(proj, bias_row, alog_row, daux, name):
    t = proj.shape[0]
    nb = t // CUM_BLK

    def body(s_ref, b_ref, a_ref, d_ref, ds_ref, db_ref, da_ref, dpre_sc):
        lane = lax.broadcasted_iota(jnp.int32, (CUM_BLK, 128), 1)
        tri = _tri(CUM_BLK, True)
        carry = jnp.zeros((1, 128), F32)
        for blk in reversed(range(nb)):
            db = d_ref[blk * CUM_BLK:(blk + 1) * CUM_BLK, :]
            cb = _dot(tri, db, NN, HI) + carry
            carry = cb[0:1]
            dpre_sc[blk * CUM_BLK:(blk + 1) * CUM_BLK, :] = jnp.where(lane < 8, cb, db)
        _, vjp = jax.vjp(_prep_point, s_ref[...], b_ref[...], a_ref[...])
        ds, dbias, dalog = vjp(dpre_sc[...])
        ds_ref[...] = ds.astype(ds_ref.dtype)
        db_ref[...] = dbias
        da_ref[...] = dalog

    return pl.pallas_call(
        body, grid=(1,),
        in_specs=[pl.BlockSpec((t, 128), lambda i: (0, 0)), pl.BlockSpec((1, 128), lambda i: (0, 0)),
                  pl.BlockSpec((1, 128), lambda i: (0, 0)), pl.BlockSpec((t, 128), lambda i: (0, 0))],
        out_specs=[pl.BlockSpec((t, 128), lambda i: (0, 0)), pl.BlockSpec((1, 128), lambda i: (0, 0)),
                   pl.BlockSpec((1, 128), lambda i: (0, 0))],
        out_shape=[jax.ShapeDtypeStruct((t, 128), BF16), jax.ShapeDtypeStruct((1, 128), F32), jax.ShapeDtypeStruct((1, 128), F32)],
        scratch_shapes=[pltpu.VMEM((t, 128), F32)], name=name,
    )(proj, bias_row, alog_row, daux)


ATT_TQ = 256
FOX_SCALE = FOX_DH ** -0.5


def _qnorm(q, g):
    return _rms(q, g) * FOX_SCALE


def _att_scores(qn_blk, kn, cfc_blk, cfr, qi, tq, kend):
    s = _dot(qn_blk.astype(BF16), kn[:kend].astype(BF16), NT) + cfc_blk - cfr[:, :kend]
    row = lax.broadcasted_iota(jnp.int32, (tq, kend), 0) + qi * tq
    col = lax.broadcasted_iota(jnp.int32, (tq, kend), 1)
    return s, row >= col


ATT_PAIR = 128 // FOX_DH


def _att_specs(t):
    pair = lambda off: pl.BlockSpec((t, 128), functools.partial(lambda i, o: (0, o + i), o=off // 128))
    gain = pl.BlockSpec((1, FOX_DH), lambda i: (0, 0))
    hd = lambda i: (i, 0, 0)
    col, row = pl.BlockSpec((ATT_PAIR, t, 1), hd), pl.BlockSpec((ATT_PAIR, 1, t), hd)
    return [pair(C_FQ), pair(C_FK), pair(C_FV), gain, gain, col, row], pl.BlockSpec((t, 128), lambda i: (0, i)), col, row


def _att_fwd(proj, qg, kg, cfc, cfr, name):
    t = proj.shape[0]
    tq = min(ATT_TQ, t)
    in_specs, pair_out, col, _ = _att_specs(t)

    def body(q_ref, k_ref, v_ref, qg_ref, kg_ref, cfc_ref, cfr_ref, o_ref, lse_ref):
        for e in range(ATT_PAIR):
            lanes = slice(e * FOX_DH, (e + 1) * FOX_DH)
            qn = _qnorm(q_ref[:, lanes].astype(F32), qg_ref[...])
            kn = _rms(k_ref[:, lanes].astype(F32), kg_ref[...])
            v = v_ref[:, lanes].astype(BF16)
            cfr = cfr_ref[e]
            for qi in range(t // tq):
                kend = (qi + 1) * tq
                rows = slice(qi * tq, kend)
                s, mask = _att_scores(qn[rows], kn, cfc_ref[e, rows, :], cfr, qi, tq, kend)
                s = jnp.where(mask, s, NEG)
                m = jnp.max(s, axis=1, keepdims=True)
                p = jnp.exp(s - m)
                l = jnp.sum(p, axis=1, keepdims=True)
                o_ref[rows, lanes] = _dot(p.astype(BF16), v[:kend], NN) / l
                lse_ref[e, rows, :] = m + jnp.log(l)

    return pl.pallas_call(
        body, grid=(FOX_HEADS // ATT_PAIR,), in_specs=in_specs, out_specs=[pair_out, col],
        out_shape=[jax.ShapeDtypeStruct((t, BW), F32), jax.ShapeDtypeStruct((FOX_HEADS, t, 1), F32)], name=name,
    )(proj, proj, proj, qg, kg, cfc, cfr)


def _att_bwd(proj, qg, kg, cfc, cfr, lse, o, do, name):
    t = proj.shape[0]
    tq = min(ATT_TQ, t)
    in_specs, pair_out, col, row = _att_specs(t)

    def body(q_ref, k_ref, v_ref, qg_ref, kg_ref, cfc_ref, cfr_ref, lse_ref, o_ref, do_ref,
             dq_ref, dk_ref, dv_ref, dcfc_ref, dcfr_ref, dqg_ref, dkg_ref, dqn_sc, dkn_sc, dv_sc, dcfr_sc):
        for e in range(ATT_PAIR):
            lanes = slice(e * FOX_DH, (e + 1) * FOX_DH)
            qn, vjp_q = jax.vjp(_qnorm, q_ref[:, lanes].astype(F32), qg_ref[...])
            kn, vjp_k = jax.vjp(_rms, k_ref[:, lanes].astype(F32), kg_ref[...])
            v = v_ref[:, lanes].astype(BF16)
            cfr = cfr_ref[e]
            do_e = do_ref[:, lanes]
            delta = jnp.sum(do_e * o_ref[:, lanes], axis=1, keepdims=True)
            dkn_sc[...] = jnp.zeros_like(dkn_sc)
            dv_sc[...] = jnp.zeros_like(dv_sc)
            dcfr_sc[...] = jnp.zeros_like(dcfr_sc)
            for qi in range(t // tq):
                kend = (qi + 1) * tq
                rows = slice(qi * tq, kend)
                s, mask = _att_scores(qn[rows], kn, cfc_ref[e, rows, :], cfr, qi, tq, kend)
                p = jnp.where(mask, jnp.exp(jnp.where(mask, s, NEG) - lse_ref[e, rows, :]), 0.0)
                do_b = do_e[rows].astype(BF16)
                dv_sc[0:kend, :] += _dot(p.astype(BF16), do_b, TN)
                dp = _dot(do_b, v[:kend], NT)
                ds = p * (dp - delta[rows])
                ds_b = ds.astype(BF16)
                dqn_sc[rows, :] = _dot(ds_b, kn[:kend].astype(BF16), NN)
                dkn_sc[0:kend, :] += _dot(ds_b, qn[rows].astype(BF16), TN)
                dcfc_ref[e, rows, :] = jnp.sum(ds, axis=1, keepdims=True)
                dcfr_sc[:, 0:kend] -= jnp.sum(ds, axis=0, keepdims=True)
            dq, dqg = vjp_q(dqn_sc[...])
            dk, dkg = vjp_k(dkn_sc[...])
            dq_ref[:, lanes] = dq.astype(dq_ref.dtype)
            dk_ref[:, lanes] = dk.astype(dk_ref.dtype)
            dv_ref[:, lanes] = dv_sc[...].astype(dv_ref.dtype)
            dcfr_ref[e] = dcfr_sc[...]
            dqg_ref[e] = dqg
            dkg_ref[e] = dkg

    gsp = pl.BlockSpec((ATT_PAIR, 1, FOX_DH), lambda i: (i, 0, 0))
    return pl.pallas_call(
        body, grid=(FOX_HEADS // ATT_PAIR,),
        in_specs=in_specs + [col, pair_out, pair_out],
        out_specs=[pair_out] * 3 + [col, row, gsp, gsp],
        out_shape=[jax.ShapeDtypeStruct((t, BW), BF16)] * 3
        + [jax.ShapeDtypeStruct((FOX_HEADS, t, 1), F32), jax.ShapeDtypeStruct((FOX_HEADS, 1, t), F32)]
        + [jax.ShapeDtypeStruct((FOX_HEADS, 1, FOX_DH), F32)] * 2,
        scratch_shapes=[pltpu.VMEM((t, FOX_DH), F32)] * 3 + [pltpu.VMEM((1, t), F32)], name=name,
    )(proj, proj, proj, qg, kg, cfc, cfr, lse, o, do)


DN_SCALE = DN_DH ** -0.5


DN_BATCH = 8


@functools.partial(jax.custom_vjp, nondiff_argnums=(2, 3))
def _mm3(a, b, dims, batch=False):
    return _mm3_passes(a, b, dims, batch)


def _mm3_fwd(a, b, dims, batch):
    return _mm3_passes(a, b, dims, batch), (a, b)


def _mm3_bwd(dims, batch, res, dc):
    a, b = res
    if dims == NN:
        return _mm3_passes(dc, b, NT, batch), _mm3_passes(a, dc, TN, batch)
    if dims == NT:
        return _mm3_passes(dc, b, NN, batch), _mm3_passes(dc, a, TN, batch)
    return _mm3_passes(b, dc, NT, batch), _mm3_passes(a, dc, NN, batch)


_mm3.defvjp(_mm3_fwd, _mm3_bwd)


def _mm3_passes(a, b, dims, batch):
    if batch:
        dn = (((dims[0][0] + 1,), (dims[1][0] + 1,)), ((0,), (0,)))
        dot = lambda p, q: lax.dot_general(p, q, dn, preferred_element_type=F32)
    else:
        dot = lambda p, q: _dot(p, q, dims)
    ah, bh = a.astype(BF16), b.astype(BF16)
    al, bl = (a - ah.astype(F32)).astype(BF16), (b - bh.astype(F32)).astype(BF16)
    return dot(ah, bh) + (dot(ah, bl) + dot(al, bh))


def _dn_local(qc, kc, vc, g, beta):
    nb, c, _ = qc.shape
    ii = lax.broadcasted_iota(jnp.int32, (c, c), 0)
    jj = lax.broadcasted_iota(jnp.int32, (c, c), 1)
    incl, strict = ii >= jj, ii > jj
    lower = jnp.broadcast_to(incl.astype(F32), (nb, c, c))
    eye = (ii == jj).astype(F32)
    mm = functools.partial(_mm3, batch=True)
    dm = mm(lower, jnp.where(strict, g, 0.0), NN)
    decay = jnp.where(incl, jnp.exp(jnp.where(incl, dm, 0.0)), 0.0)
    gcum = mm(lower, g * jnp.ones((1, 1, DN_DH), F32), NN)
    eg = jnp.exp(gcum)
    glast = gcum[:, c - 1:c]
    kb = kc * beta
    n1 = jnp.where(strict, mm(kb, kc, NT) * decay, 0.0)
    inv = eye - n1
    pw = n1
    for _ in range(5):
        pw = mm(pw, pw, NN)
        inv = inv + mm(pw, inv, NN)
    sol = mm(inv, jnp.concatenate([vc * beta, kb * eg], axis=2), NN)
    qk = jnp.where(incl, mm(qc, kc, NT) * decay, 0.0)
    return sol[:, :, :DN_DH], sol[:, :, DN_DH:], qk, qc * eg, kc * jnp.exp(glast - gcum), jnp.exp(glast)


@functools.partial(jax.custom_vjp, nondiff_argnums=(2,))
def _mm1(a, b, dims):
    return _dot(a.astype(BF16), b.astype(BF16), dims)


def _mm1_fwd(a, b, dims):
    return _mm1(a, b, dims), (a, b)


def _mm1_bwd(dims, res, dc):
    a, b = res
    if dims == NN:
        return _mm1(dc, b, NT), _mm1(a, dc, TN)
    if dims == NT:
        return _mm1(dc, b, NN), _mm1(dc, a, TN)
    return _mm1(b, dc, NT), _mm1(a, dc, NN)


_mm1.defvjp(_mm1_fwd, _mm1_bwd)


def _dn_state(u, kcum, qk, qdec, kdec, egl, state):
    v_new = u - _mm1(kcum, state, NN)
    out = _mm1(qdec, state, NN) + _mm1(qk, v_new, NN)
    return out, state * egl + _mm1(kdec, v_new, TN)


def _dn_pre_q(c):
    return _l2(_silu(c)) * DN_SCALE


def _dn_pre_k(c):
    return _l2(_silu(c))


def _dn_post(o, z, ng):
    return _rms(o, ng) * _silu(z)


def _dn_specs(t):
    cblk = lambda o: pl.BlockSpec((t, DN_DH), functools.partial(lambda h, o: (0, o + h), o=o // DN_DH))
    wblk = lambda o: pl.BlockSpec((4, DN_DH), functools.partial(lambda h, o: (0, o + h), o=o // DN_DH))
    proj_specs = [cblk(C_DQ), cblk(C_DK), cblk(C_DV), cblk(C_DZ)]
    w_specs = [wblk(0), wblk(BW), wblk(2 * BW)]
    gb_spec = pl.BlockSpec((1, t, 2), lambda h: (h, 0, 0))
    return proj_specs, w_specs, gb_spec


def _chunk_rows(n, count=1):
    return pl.ds(pl.multiple_of(n * DN_CHUNK, DN_CHUNK), count * DN_CHUNK)


def _egl_rows(n, count=1):
    return pl.ds(pl.multiple_of(n * 8, 8), count * 8)


def _dn_local_inputs(n, qn_sc, kn_sc, vv_sc, gb_ref):
    r = _chunk_rows(n, DN_BATCH)
    split = lambda v: v.reshape(DN_BATCH, DN_CHUNK, v.shape[-1])
    gbv = split(gb_ref[0, r, :])
    return split(qn_sc[r, :]), split(kn_sc[r, :]), split(vv_sc[r, :]), gbv[:, :, 0:1], gbv[:, :, 1:2]


def _dn_local_phase(nc, qn_sc, kn_sc, vv_sc, gb_ref, loc):
    def step(i, carry):
        n = i * DN_BATCH
        vals = _dn_local(*_dn_local_inputs(n, qn_sc, kn_sc, vv_sc, gb_ref))
        for sc, val in zip(loc[:5], vals[:5]):
            sc[_chunk_rows(n, DN_BATCH), :] = val.reshape(DN_BATCH * DN_CHUNK, val.shape[-1])
        loc[5][_egl_rows(n, DN_BATCH), :] = jnp.broadcast_to(vals[5], (DN_BATCH, 8, DN_DH)).reshape(DN_BATCH * 8, DN_DH)
        return carry

    lax.fori_loop(0, nc // DN_BATCH, step, 0)


def _dn_loc_scratch(t, nc):
    big = pltpu.VMEM((t, DN_DH), F32)
    return [big, big, pltpu.VMEM((t, DN_CHUNK), F32), big, big, pltpu.VMEM((nc * 8, DN_DH), F32)]


def _dn_loc_io(t, nc, **spec_args):
    head_cols = pl.BlockSpec((t, DN_DH), lambda h: (0, h), **spec_args)
    per_head = lambda rows, cols: pl.BlockSpec((1, rows, cols), lambda h: (h, 0, 0), **spec_args)
    wide = jax.ShapeDtypeStruct((t, BW), F32)
    shapes = [wide, wide, jax.ShapeDtypeStruct((DN_HEADS, t, DN_CHUNK), F32), wide, wide,
              jax.ShapeDtypeStruct((DN_HEADS, nc * 8, DN_DH), F32)]
    return shapes, [head_cols, head_cols, per_head(t, DN_CHUNK), head_cols, head_cols, per_head(nc * 8, DN_DH)]


def _dn_loc_refs(refs):
    return refs[0], refs[1], refs[2].at[0], refs[3], refs[4], refs[5].at[0]


def _dn_fwd(proj, conv_w, gb, ng, name):
    t = proj.shape[0]
    nc = t // DN_CHUNK
    assert nc % DN_BATCH == 0
    proj_specs, w_specs, gb_spec = _dn_specs(t)
    loc_shapes, loc_specs = _dn_loc_io(t, nc)

    def body(q_ref, k_ref, v_ref, z_ref, wq_ref, wk_ref, wv_ref, gb_ref, ng_ref, y_ref, o_ref, st_ref, *rest):
        loc = _dn_loc_refs(rest[:6])
        qn_sc, kn_sc, vv_sc = rest[6:]
        qn_sc[...] = _dn_pre_q(_conv(q_ref[...].astype(F32), wq_ref[...]))
        kn_sc[...] = _dn_pre_k(_conv(k_ref[...].astype(F32), wk_ref[...]))
        vv_sc[...] = _silu(_conv(v_ref[...].astype(F32), wv_ref[...]))
        _dn_local_phase(nc, qn_sc, kn_sc, vv_sc, gb_ref, loc)
        u_sc, kcum_sc, qk_sc, qdec_sc, kdec_sc, egl_sc = loc

        def chunk(n, state):
            r = _chunk_rows(n)
            egl = egl_sc[_egl_rows(n), :][0:1]
            out, new_state = _dn_state(u_sc[r, :], kcum_sc[r, :], qk_sc[r, :], qdec_sc[r, :], kdec_sc[r, :], egl, state)
            st_ref[0, n] = state
            o_ref[r, :] = out
            return new_state

        lax.fori_loop(0, nc, chunk, jnp.zeros((DN_DH, DN_DH), F32))
        y_ref[...] = _dn_post(o_ref[...], z_ref[...].astype(F32), ng_ref[...])

    hblk = pl.BlockSpec((t, DN_DH), lambda h: (0, h))
    return pl.pallas_call(
        body, grid=(DN_HEADS,),
        in_specs=proj_specs + w_specs + [gb_spec, pl.BlockSpec((1, DN_DH), lambda h: (0, 0))],
        out_specs=[hblk, hblk, pl.BlockSpec((1, nc, DN_DH, DN_DH), lambda h: (h, 0, 0, 0))] + loc_specs,
        out_shape=[jax.ShapeDtypeStruct((t, BW), F32), jax.ShapeDtypeStruct((t, BW), F32),
                   jax.ShapeDtypeStruct((DN_HEADS, nc, DN_DH, DN_DH), F32)] + loc_shapes,
        scratch_shapes=[pltpu.VMEM((t, DN_DH), F32)] * 3, name=name,
    )(proj, proj, proj, proj, conv_w, conv_w, conv_w, gb, ng)


def _dn_bwd(proj, conv_w, gb, ng, o, states, loc_saved, dy, name):
    t = proj.shape[0]
    nc = t // DN_CHUNK
    proj_specs, w_specs, gb_spec = _dn_specs(t)
    _, loc_specs = _dn_loc_io(t, nc, pipeline_mode=pl.Buffered(1))

    def body(q_ref, k_ref, v_ref, z_ref, wq_ref, wk_ref, wv_ref, gb_ref, ng_ref, o_ref, st_ref, dy_ref,
             u_ref, kcum_ref, qk_ref, qdec_ref, kdec_ref, egl_ref,
             dq_ref, dk_ref, dv_ref, dz_ref, dwq_ref, dwk_ref, dwv_ref, dgb_ref, dng_ref,
             qn_sc, kn_sc, vv_sc, do_sc, *dloc):
        loc = _dn_loc_refs((u_ref, kcum_ref, qk_ref, qdec_ref, kdec_ref, egl_ref))
        qn_sc[...] = _dn_pre_q(_conv(q_ref[...].astype(F32), wq_ref[...]))
        kn_sc[...] = _dn_pre_k(_conv(k_ref[...].astype(F32), wk_ref[...]))
        vv_sc[...] = _silu(_conv(v_ref[...].astype(F32), wv_ref[...]))
        _, vjp_y = jax.vjp(_dn_post, o_ref[...], z_ref[...].astype(F32), ng_ref[...])
        do, dz, dng = vjp_y(dy_ref[...])
        do_sc[...] = do
        dz_ref[...] = dz.astype(dz_ref.dtype)
        dng_ref[0] = dng
        u_sc, kcum_sc, qk_sc, qdec_sc, kdec_sc, egl_sc = loc

        def state_bwd(i, dstate):
            n = nc - 1 - i
            r = _chunk_rows(n)
            r8 = _egl_rows(n)
            _, vjp = jax.vjp(_dn_state, u_sc[r, :], kcum_sc[r, :], qk_sc[r, :], qdec_sc[r, :], kdec_sc[r, :],
                             egl_sc[r8, :][0:1], st_ref[0, n])
            du, dkcum, dqk, dqdec, dkdec, degl, dprev = vjp((do_sc[r, :], dstate))
            for d_sc, val in zip(dloc[:5], (du, dkcum, dqk, dqdec, dkdec)):
                d_sc[r, :] = val
            dloc[5][r8, :] = jnp.broadcast_to(degl, (8, DN_DH))
            return dprev

        lax.fori_loop(0, nc, state_bwd, jnp.zeros((DN_DH, DN_DH), F32))

        def local_bwd(i, carry):
            n = i * DN_BATCH
            r = _chunk_rows(n, DN_BATCH)
            _, vjp = jax.vjp(_dn_local, *_dn_local_inputs(n, qn_sc, kn_sc, vv_sc, gb_ref))
            cts = tuple(d_sc[r, :].reshape(DN_BATCH, DN_CHUNK, d_sc.shape[-1]) for d_sc in dloc[:5])
            cts += (dloc[5][_egl_rows(n, DN_BATCH), :].reshape(DN_BATCH, 8, DN_DH)[:, 0:1],)
            dqc, dkc, dvc, dg, dbeta = vjp(cts)
            for d_sc, val in zip((dloc[0], dloc[1], dloc[3]), (dqc, dkc, dvc)):
                d_sc[r, :] = val.reshape(DN_BATCH * DN_CHUNK, DN_DH)
            dgb_ref[0, r, :] = jnp.concatenate([dg, dbeta], axis=2).reshape(DN_BATCH * DN_CHUNK, 2)
            return carry

        lax.fori_loop(0, nc // DN_BATCH, local_bwd, 0)
        for x_ref, w_ref, pre, d_sc, dx_ref, dw_ref in ((q_ref, wq_ref, _dn_pre_q, dloc[0], dq_ref, dwq_ref),
                                                       (k_ref, wk_ref, _dn_pre_k, dloc[1], dk_ref, dwk_ref),
                                                       (v_ref, wv_ref, _silu, dloc[3], dv_ref, dwv_ref)):
            x_in = x_ref[...].astype(F32)
            _, vjp = jax.vjp(pre, _conv(x_in, w_ref[...]))
            (dc,) = vjp(d_sc[...])
            dx, dws = _conv_bwd(x_in, w_ref[...], dc)
            dx_ref[...] = dx.astype(dx_ref.dtype)
            for j, dw in enumerate(dws):
                dw_ref[j:j + 1, :] = dw

    hblk = pl.BlockSpec((t, DN_DH), lambda h: (0, h))
    wout = pl.BlockSpec((4, DN_DH), lambda h: (0, h))
    return pl.pallas_call(
        body, grid=(DN_HEADS,),
        in_specs=proj_specs + w_specs + [gb_spec, pl.BlockSpec((1, DN_DH), lambda h: (0, 0)), hblk,
                                         pl.BlockSpec((1, nc, DN_DH, DN_DH), lambda h: (h, 0, 0, 0)), hblk] + loc_specs,
        out_specs=[hblk] * 4 + [wout] * 3 + [gb_spec, pl.BlockSpec((1, 1, DN_DH), lambda h: (h, 0, 0))],
        out_shape=[jax.ShapeDtypeStruct((t, BW), BF16)] * 4 + [jax.ShapeDtypeStruct((4, BW), F32)] * 3
        + [jax.ShapeDtypeStruct((DN_HEADS, t, 2), F32), jax.ShapeDtypeStruct((DN_HEADS, 1, DN_DH), F32)],
        scratch_shapes=[pltpu.VMEM((t, DN_DH), F32)] * 4 + _dn_loc_scratch(t, nc), name=name,
    )(proj, proj, proj, proj, conv_w, conv_w, conv_w, gb, ng, o, states, dy, *loc_saved)


MERGE_TM, MERGE_TN = 512, 512


def _merge_specs(t):
    tm, tn = min(MERGE_TM, t), MERGE_TN
    y_spec = pl.BlockSpec((tm, BW), lambda i, j: (i, 0))
    w_spec = pl.BlockSpec((3, BW, tn), lambda i, j: (0, 0, j))
    gate_specs = [pl.BlockSpec((tm, tn), functools.partial(lambda i, j, o: (i, o + j), o=(C_GATE + n * D_MODEL) // tn))
                  for n in range(3)]
    return tm, tn, [y_spec] * 3 + [w_spec] + gate_specs


def _merge_fwd(ys, wb, proj, name):
    t = proj.shape[0]
    tm, tn, in_specs = _merge_specs(t)

    def body(y0, y1, y2, w_ref, g0, g1, g2, o_ref):
        acc = jnp.zeros((tm, tn), F32)
        for n, (y, g) in enumerate(((y0, g0), (y1, g1), (y2, g2))):
            acc = acc + _dot(y[...].astype(BF16), w_ref[n], NN) * _sigmoid(g[...].astype(F32))
        o_ref[...] = acc.astype(o_ref.dtype)

    return pl.pallas_call(
        body, grid=(t // tm, D_MODEL // tn), in_specs=in_specs,
        out_specs=pl.BlockSpec((tm, tn), lambda i, j: (i, j)),
        out_shape=jax.ShapeDtypeStruct((t, D_MODEL), BF16), name=name,
    )(*ys, wb, proj, proj, proj)


def _merge_bwd(ys, wb, proj, dmerged, name):
    t = proj.shape[0]
    tm, tn, in_specs = _merge_specs(t)

    def body(y0, y1, y2, w_ref, g0, g1, g2, dm_ref, dg_ref, dt_ref):
        dm = dm_ref[...]
        for n, (y, g) in enumerate(((y0, g0), (y1, g1), (y2, g2))):
            tn_ = _dot(y[...].astype(BF16), w_ref[n], NN)
            sg = _sigmoid(g[...].astype(F32))
            dg_ref[n] = (dm * tn_ * sg * (1.0 - sg)).astype(dg_ref.dtype)
            dt_ref[n] = (dm * sg).astype(dt_ref.dtype)

    o3 = pl.BlockSpec((3, tm, tn), lambda i, j: (0, i, j))
    return pl.pallas_call(
        body, grid=(t // tm, D_MODEL // tn), in_specs=in_specs + [pl.BlockSpec((tm, tn), lambda i, j: (i, j))],
        out_specs=[o3, o3], out_shape=[jax.ShapeDtypeStruct((3, t, D_MODEL), BF16)] * 2, name=name,
    )(*ys, wb, proj, proj, proj, dmerged)


def _prep_rows(sp):
    z4, z112 = jnp.zeros((1, 4), F32), jnp.zeros((1, 112), F32)
    bias_row = jnp.concatenate([sp['b_fox_f'], z4, sp['dn_dt_bias'], z112], axis=1)
    alog_row = jnp.concatenate([jnp.zeros((1, 12), F32), sp['dn_a_log'], z112], axis=1)
    return bias_row, alog_row


def _sc_fwd_tile(sb, sc, sv, w):
    sb, sc, sv = sb.astype(F32), sc.astype(F32), sv.astype(F32)
    return (sb * _conv(sc * sv, w),)


def _ffn_act(ug, uv):
    return _silu(ug) * uv


W_IN_COLS = [(C_FQ, 0, 1536), (C_SB, 1544, 1536), (C_DQ, 3080, 1536), (C_DZ, 4624, 512), (C_GATE, 5136, 3072),
             (C_SMALL, 1536, 8), (C_SMALL + 8, 4616, 8)]
W_IN_SHARD = IN_ORIG // N_DEV


def _w_in_segments():
    out = []
    for d, o, w in W_IN_COLS:
        end = o + w
        while o < end:
            k = o // W_IN_SHARD
            n = min(end, (k + 1) * W_IN_SHARD) - o
            out.append((d, k, o - k * W_IN_SHARD, n))
            o, d = o + n, d + n
    return out


def _w_in_assemble(g, name):
    tm = 256

    def body(g_ref, o_ref):
        for d, k, s, n in _w_in_segments():
            o_ref[:, d:d + n] = g_ref[k, :, s:s + n]
        o_ref[:, IN_ORIG:IN_P] = jnp.zeros((tm, IN_P - IN_ORIG), o_ref.dtype)

    return pl.pallas_call(
        body, grid=(D_MODEL // tm,),
        in_specs=[pl.BlockSpec((N_DEV, tm, W_IN_SHARD), lambda i: (0, i, 0))],
        out_specs=pl.BlockSpec((tm, IN_P), lambda i: (i, 0)),
        out_shape=jax.ShapeDtypeStruct((D_MODEL, IN_P), g.dtype), name=name,
    )(g)


W_IN_SPLIT_TM = 128


def _w_in_split_by_device(g, name):
    tm = W_IN_SPLIT_TM

    def body(g_ref, own_ref, send_ref):
        me = 4 * lax.axis_index("x") + 2 * lax.axis_index("y") + lax.axis_index("c")
        for d, k, s, n in _w_in_segments():
            val = g_ref[:, d:d + n]
            send_ref[k, :, s:s + n] = val.astype(send_ref.dtype)

            @pl.when(me == k)
            def _():
                own_ref[:, s:s + n] = val

    return pl.pallas_call(
        body, grid=(D_MODEL // tm,), in_specs=[pl.BlockSpec((tm, IN_P), lambda i: (i, 0))],
        out_specs=[pl.BlockSpec((tm, W_IN_SHARD), lambda i: (i, 0)), pl.BlockSpec((N_DEV, tm, W_IN_SHARD), lambda i: (0, i, 0))],
        out_shape=[jax.ShapeDtypeStruct((D_MODEL, W_IN_SHARD), F32), jax.ShapeDtypeStruct((N_DEV, D_MODEL, W_IN_SHARD), BF16)],
        name=name,
    )(g)


def _layer_fwd(x, p_i, w, sp, tag, late_weights=None):
    t = x.shape[0]
    sv = {'x': x}
    (hn,) = _ew(lambda a, g: (_rms(a, g),), [x], [sp['g_mix']], [(D_MODEL, BF16)], name=f'rms_mix_{tag}')
    proj = _mm(hn, w['w_in'], 'nn', [BF16], name=f'in_proj_{tag}')
    small = _mm(hn, w['w_in'][:, C_SMALL:], 'nn', [F32], name=f'in_small_{tag}')
    bias_row, alog_row = _prep_rows(sp)
    aux = _prep_fwd(small, bias_row, alog_row, f'prep_{tag}')
    cf = aux[:, :FOX_HEADS].T
    cfc, cfr = cf[:, :, None], cf[:, None, :]
    y_fox, lse = _att_fwd(proj, sp['fox_q_gain'], sp['fox_k_gain'], cfc, cfr, f'fox_fwd_{tag}')
    (y_sc,) = _cb(_sc_fwd_tile, [(proj, C_SB), (proj, C_SC), (proj, C_SV)], [(w['sc_conv_w'], 0)], [F32], [],
                  tc=256, nblk=2, name=f'sc_fwd_{tag}')
    gb = jnp.stack([aux[:, 12:16].T, aux[:, 8:12].T], axis=-1)
    y_dn, o_dn, states, *dn_loc = _dn_fwd(proj, w['dn_conv_w'], gb, sp['dn_norm_gain'], f'dn_fwd_{tag}')
    ys = (y_fox, y_sc, y_dn)
    if late_weights is not None:
        w = {**w, **late_weights(y_dn)}
    merged = _merge_fwd(ys, w['w_branch'], proj, f'merge_fwd_{tag}')
    x1 = _mm(merged, w['w_o'], 'nn', [F32], epi=lambda acc, r: (acc + r,), epi_args=(x,), name=f'o_proj_{tag}')
    (hf,) = _ew(lambda a, g: (_rms(a, g),), [x1], [sp['g_ffn']], [(D_MODEL, BF16)], name=f'rms_ffn_{tag}')
    up = _mm(hf, w['w_up'], 'nt', [F32], name=f'up_proj_{tag}')
    (act,) = _cb(lambda ug, uv, wg, wv: (_ffn_act(_conv(ug, wg), _conv(uv, wv)),), [(up, 0), (up, D_FF)],
                 [(w['ffn_conv_w'], 0), (w['ffn_conv_w'], D_FF)], [BF16], [], tc=256, nblk=D_FF // 256, name=f'ffn_act_{tag}')
    x2 = _mm(act, w['w_down'], 'nn', [F32], epi=lambda acc, r: (acc + r,), epi_args=(x1,), name=f'down_proj_{tag}')
    (hp,) = _ew(lambda a, g: (_rms(a, g),), [x2], [sp['g_ple']], [(D_MODEL, BF16)], name=f'rms_ple_{tag}')
    gp = _mm(hp, w['w_ple_gate'], 'nn', [F32], name=f'ple_gate_{tag}')
    x3 = _mm(p_i, w['w_ple'], 'nn', [F32], epi=lambda acc, g, r: (r + _sigmoid(g) * acc,), epi_args=(gp, x2), name=f'ple_{tag}')
    sv.update(hn=hn, proj=proj, small=small, aux=aux, cfc=cfc, cfr=cfr, lse=lse, ys=ys, gb=gb, o_dn=o_dn,
              states=states, merged=merged, x1=x1, hf=hf, up=up, act=act, x2=x2, hp=hp, gp=gp, p=p_i,
              bias_row=bias_row, alog_row=alog_row, w=w, dn_loc=dn_loc)
    return x3, sv


def _rms_bwd(x, g, dh, dres, name):
    def fn(xv, dhv, dr, gv):
        _, vjp = jax.vjp(_rms, xv, gv)
        dx, dg = vjp(dhv)
        return dr + dx, dg
    return _ew(fn, [x, dh, dres], [g], [(D_MODEL, F32)], [(1, D_MODEL)], name=name)


def _layer_bwd(dx3, sv, sp, tag, early_grads=None, last_grad=None):
    t = dx3.shape[0]
    w = sv['w']
    g = {}
    def ple_epi(acc, gpv, d):
        s = _sigmoid(gpv)
        return d * acc * s * (1.0 - s), d * s
    dgp, de = _mm(sv['p'], w['w_ple'], 'nn', [BF16, BF16], epi=ple_epi, epi_args=(sv['gp'], dx3), name=f'ple_bwd_{tag}')
    g['w_ple'] = _mm(sv['p'], de, 'tn', [F32], name=f'd_w_ple_{tag}')
    g['w_ple_gate'] = _mm(sv['hp'], dgp, 'tn', [F32], name=f'd_w_ple_gate_{tag}')
    dhp = _mm(dgp, w['w_ple_gate'], 'nt', [F32], name=f'd_hp_{tag}')
    dx2, g['g_ple'] = _rms_bwd(sv['x2'], sp['g_ple'], dhp, dx3, f'rms_ple_bwd_{tag}')
    dact = _mm(dx2, w['w_down'], 'nt', [F32], name=f'd_act_{tag}')
    g['w_down'] = _mm(sv['act'], dx2, 'tn', [F32], name=f'd_w_down_{tag}')

    def ffn_bwd_tile(ug, uv, da, wg, wv):
        cg, cv = _conv(ug, wg), _conv(uv, wv)
        _, vjp = jax.vjp(_ffn_act, cg, cv)
        dcg, dcv = vjp(da)
        dug, dwg = _conv_bwd(ug, wg, dcg)
        duv, dwv = _conv_bwd(uv, wv, dcv)
        return dug, duv, jnp.concatenate(dwg, axis=0), jnp.concatenate(dwv, axis=0)
    dupg, dupv, dwg, dwv = _cb(ffn_bwd_tile, [(sv['up'], 0), (sv['up'], D_FF), (dact, 0)],
                               [(w['ffn_conv_w'], 0), (w['ffn_conv_w'], D_FF)], [BF16, BF16], [3, 3], tc=256,
                               nblk=D_FF // 256, name=f'ffn_act_bwd_{tag}')
    dup = jnp.concatenate([dupg, dupv], axis=1)
    g['ffn_conv_w'] = jnp.concatenate([dwg, dwv], axis=1)
    g['w_up'] = _mm(dup, sv['hf'], 'tn', [F32], name=f'd_w_up_{tag}')
    dhf = _mm(dup, w['w_up'], 'nn', [F32], name=f'd_hf_{tag}')
    dx1, g['g_ffn'] = _rms_bwd(sv['x1'], sp['g_ffn'], dhf, dx2, f'rms_ffn_bwd_{tag}')
    dmerged = _mm(dx1, w['w_o'], 'nt', [F32], name=f'd_merged_{tag}')
    g['w_o'] = _mm(sv['merged'], dx1, 'tn', [F32], name=f'd_w_o_{tag}')
    dgate, dtn = _merge_bwd(sv['ys'], w['w_branch'], sv['proj'], dmerged, f'merge_bwd_{tag}')
    dys, dwb = [], []
    for n in range(3):
        dys.append(_mm(dtn[n], w['w_branch'][n], 'nt', [F32], name=f'd_y{n}_{tag}'))
        dwb.append(_mm(sv['ys'][n], dtn[n], 'tn', [F32], name=f'd_w_branch{n}_{tag}'))
    g['w_branch'] = jnp.stack(dwb)
    if early_grads is not None:
        sp = early_grads(g, sp)
    ddq, ddk, ddv, ddz, dwq, dwk, dwv_, dgb, dng = _dn_bwd(sv['proj'], w['dn_conv_w'], sv['gb'], sp['dn_norm_gain'],
                                                           sv['o_dn'], sv['states'], sv['dn_loc'], dys[2], f'dn_bwd_{tag}')
    g['dn_conv_w'] = jnp.concatenate([dwq, dwk, dwv_], axis=1)
    g['dn_norm_gain'] = jnp.sum(dng, axis=0)
    def sc_bwd_tile(sb, sc, svv, dy, wv):
        sb, sc, svv = sb.astype(F32), sc.astype(F32), svv.astype(F32)
        u = sc * svv
        dsb = dy * _conv(u, wv)
        du, dws = _conv_bwd(u, wv, dy * sb)
        return dsb, du * svv, du * sc, jnp.concatenate(dws, axis=0)
    dsb, dsc, dsv, g['sc_conv_w'] = _cb(sc_bwd_tile, [(sv['proj'], C_SB), (sv['proj'], C_SC), (sv['proj'], C_SV), (dys[1], 0)],
                                        [(w['sc_conv_w'], 0)], [BF16, BF16, BF16], [3], tc=256, nblk=2, name=f'sc_bwd_{tag}')
    dfq, dfk, dfv, dcfc, dcfr, dqg, dkg = _att_bwd(sv['proj'], sp['fox_q_gain'], sp['fox_k_gain'], sv['cfc'], sv['cfr'],
                                                   sv['lse'], sv['ys'][0], dys[0], f'fox_bwd_{tag}')
    g['fox_q_gain'] = jnp.sum(dqg, axis=0)
    g['fox_k_gain'] = jnp.sum(dkg, axis=0)
    dcf = (dcfc[:, :, 0] + dcfr[:, 0, :]).T
    daux = jnp.concatenate([dcf, dgb[:, :, 1].T, dgb[:, :, 0].T, jnp.zeros((t, 112), F32)], axis=1)
    dsmall, dbias, dalog = _prep_bwd(sv['small'], sv['bias_row'], sv['alog_row'], daux, f'prep_bwd_{tag}')
    g['b_fox_f'] = dbias[:, 0:8]
    g['dn_dt_bias'] = dbias[:, 12:16]
    g['dn_a_log'] = dalog[:, 12:16]
    dproj = jnp.concatenate([dfq, dfk, dfv, dsb, dsc, dsv, ddq, ddk, ddv, ddz, dgate[0], dgate[1], dgate[2], dsmall], axis=1)
    g['w_in'] = _mm(sv['hn'], dproj, 'tn', [F32], name=f'd_w_in_{tag}')
    if last_grad is not None:
        sp = last_grad(g, sp)
    dhn = _mm(dproj, w['w_in'], 'nt', [F32], epi=lambda acc, z: (acc + z,),
              epi_args=(sp.get('zero_row', jnp.zeros((1, D_MODEL), F32)),), name=f'd_hn_{tag}')
    dx, g['g_mix'] = _rms_bwd(sv['x'], sp['g_mix'], dhn, dx1, f'rms_mix_bwd_{tag}')
    return dx, g


def _loss_bwd(y, target, name):
    inv = 1.0 / y.shape[1]

    def fn(yv, tv):
        err = yv - tv
        return err * inv, jnp.zeros((8, 128), F32) + 0.5 * inv * jnp.sum(err * err)
    return _ew(fn, [y, target], [], [(y.shape[1], F32)], [(8, 128)], name=name)


PACK_W = 1024
FULL_SHAPE = {'w_in': (D_MODEL, IN_ORIG), 'w_branch': (3, BW, D_MODEL), 'w_o': (D_MODEL, D_MODEL), 'w_up': (2 * D_FF, D_MODEL),
              'w_down': (D_FF, D_MODEL), 'w_ple_gate': (D_MODEL, D_MODEL), 'w_ple': (PLE_DIM, D_MODEL),
              'sc_conv_w': (3, BW), 'dn_conv_w': (4, 3 * BW), 'ffn_conv_w': (3, 2 * D_FF)}
SMALL_SHAPE = {'g_mix': D_MODEL, 'b_fox_f': FOX_HEADS, 'fox_q_gain': FOX_DH, 'fox_k_gain': FOX_DH, 'dn_a_log': DN_HEADS,
               'dn_dt_bias': DN_HEADS, 'dn_norm_gain': DN_DH, 'g_ffn': D_MODEL, 'g_ple': D_MODEL}


def _shard_shape(name):
    s = list(FULL_SHAPE[name])
    s[SHARDED[name]] //= N_DEV
    return tuple(s)


def _full_from_gathered(g, name):
    sh, ax = _shard_shape(name), SHARDED[name]
    blocks = jnp.moveaxis(g, 0, ax)
    return blocks.reshape(sh[:ax] + (N_DEV * sh[ax],) + sh[ax + 1:])


def _by_dest(full, name):
    sh, ax = _shard_shape(name), SHARDED[name]
    return jnp.moveaxis(full.reshape(sh[:ax] + (N_DEV, sh[ax]) + sh[ax + 1:]), ax, 0)


def _flat_pack(arrs):
    flat = jnp.concatenate([a.reshape(-1).astype(F32) for a in arrs])
    rows = -(-flat.shape[0] // (8 * PACK_W)) * 8
    return jnp.pad(flat, (0, rows * PACK_W - flat.shape[0])).reshape(rows, PACK_W)


def _flat_unpack(pack, shapes):
    flat, out, off = pack.reshape(-1), [], 0
    for s in shapes:
        n = 1
        for d in s:
            n *= d
        out.append(flat[off:off + n].reshape(s))
        off += n
    return out


SMALL_SHAPES = [(DEPTH, SMALL_SHAPE[n]) for n in SMALL]
SMALL_LOSS_AT = sum(DEPTH * SMALL_SHAPE[n] for n in SMALL)
CONV_SHARD_SHAPES = [(DEPTH,) + _shard_shape(n) for n in CONVW]
CONV_FULL_SHAPES = [(DEPTH,) + FULL_SHAPE[n] for n in CONVW]


def _pick_tm(m, width):
    best = None
    for tm in range(16, m + 1, 16):
        if m % tm == 0 and tm * width * 4 <= (1 << 20):
            best = tm
    return best if best is not None else m


HBM_SPEC = pl.BlockSpec(memory_space=pltpu.HBM)


def _place():
    x, y, c = lax.axis_index("x"), lax.axis_index("y"), lax.axis_index("c")
    return x, y, c, [(1 - x, y), (x, 1 - y), (1 - x, 1 - y)]


def _all_gather(arrs, name):
    n = len(arrs)

    def body(*refs):
        ins, outs = refs[:n], refs[n:2 * n]
        send_sems, recv_sems = refs[2 * n:]
        x, y, c, chips = _place()
        me, sibling = (x, y, c), (x, y, 1 - c)

        def block(a, p):
            return outs[a].at[4 * p[0] + 2 * p[1] + p[2]]

        def copy(a, k, blk, to, src=None):
            return pltpu.make_async_remote_copy(src_ref=block(a, blk) if src is None else src, dst_ref=block(a, blk),
                                                send_sem=send_sems.at[a, k], recv_sem=recv_sems.at[a, k],
                                                device_id=to, device_id_type=MESH)

        first, passed = [], []
        for a in range(n):
            first.append(copy(a, 0, me, sibling, src=ins[a]))
            first += [copy(a, 1 + j, me, (*chip, c), src=ins[a]) for j, chip in enumerate(chips)]
        for cp in first:
            cp.start()
        for j, chip in enumerate(chips):
            for a in range(n):
                copy(a, 1 + j, (*chip, c), me).wait_recv()
                fwd = copy(a, 4 + j, (*chip, c), sibling)
                fwd.start()
                passed.append(fwd)
        for a in range(n):
            copy(a, 0, sibling, me).wait_recv()
            for j, chip in enumerate(chips):
                copy(a, 4 + j, (*chip, 1 - c), me).wait_recv()
        for cp in first + passed:
            cp.wait_send()

    gathered = pl.pallas_call(
        body, in_specs=[HBM_SPEC] * n, out_specs=[HBM_SPEC] * n,
        out_shape=[jax.ShapeDtypeStruct((N_DEV,) + a.shape, a.dtype) for a in arrs],
        scratch_shapes=[pltpu.SemaphoreType.DMA((n, 7)), pltpu.SemaphoreType.DMA((n, 7))], name=name,
    )(*arrs)
    me = 4 * lax.axis_index("x") + 2 * lax.axis_index("y") + lax.axis_index("c")
    return [lax.dynamic_update_index_in_dim(g, a, me, 0) for g, a in zip(gathered, arrs)]


SEM_SPEC = pl.BlockSpec(memory_space=pltpu.SEMAPHORE)
DATAFLOW = pltpu.SideEffectType.DATAFLOW_SIDE_EFFECTING
ALL_PEERS = (1, 2, 3, 4, 5, 6, 7)
NEAR_PEERS = (1, 4, 2, 6)


def _direct_copies(src_refs, zone_refs, send_sems, recv_sems, per_dest, peers):
    x, y, c, _ = _place()
    me = 4 * x + 2 * y + c
    cps = []
    for a, (src, zone) in enumerate(zip(src_refs, zone_refs)):
        for r, bits in enumerate(peers):
            px = 1 - x if bits & 4 else x
            py = 1 - y if bits & 2 else y
            pc = 1 - c if bits & 1 else c
            cps.append(pltpu.make_async_remote_copy(
                src_ref=src.at[4 * px + 2 * py + pc] if per_dest else src, dst_ref=zone.at[me],
                send_sem=send_sems.at[a * len(peers) + r], recv_sem=recv_sems.at[a * len(peers) + r],
                device_id=(px, py, pc), device_id_type=MESH))
    return cps


def _direct_start(srcs, per_dest, after, name, peers=ALL_PEERS):
    n = len(srcs)
    zones = [lax.empty((N_DEV,) + (s.shape[1:] if per_dest else s.shape), s.dtype) for s in srcs]

    def body(*refs):
        src_refs, zone_refs = refs[:n], refs[n:2 * n]
        send_sems, recv_sems = refs[2 * n + 1:2 * n + 3]
        for cp in _direct_copies(src_refs, zone_refs, send_sems, recv_sems, per_dest, peers):
            cp.start()
        refs[-1][...] = jnp.zeros_like(refs[-1])

    sems = pltpu.SemaphoreType.DMA((n * len(peers),))
    res = pl.pallas_call(
        body, name=name, in_specs=[HBM_SPEC] * (2 * n + 1),
        out_shape=[sems, sems] + [pltpu.HBM(s.shape, s.dtype) for s in srcs] + [pltpu.HBM(z.shape, z.dtype) for z in zones]
        + [jax.ShapeDtypeStruct((8, 128), F32)],
        out_specs=[SEM_SPEC, SEM_SPEC] + [HBM_SPEC] * (2 * n) + [pl.BlockSpec(memory_space=pltpu.VMEM)],
        input_output_aliases={i: 2 + i for i in range(2 * n)},
        compiler_params=pltpu.CompilerParams(has_side_effects=DATAFLOW),
    )(*[pltpu.with_memory_space_constraint(s, pltpu.HBM) for s in srcs],
      *[pltpu.with_memory_space_constraint(z, pltpu.HBM) for z in zones], after)
    return (res[0], res[1], list(res[2:2 + n]), list(res[2 + n:2 + 2 * n]), per_dest, peers), res[-1][0:1, 0:1]


def _direct_wait(started, after, name):
    send_sems, recv_sems, srcs, zones, per_dest, peers = started
    n = len(srcs)

    def body(*refs):
        src_refs, zone_refs = refs[:n], refs[n:2 * n]
        for cp in _direct_copies(src_refs, zone_refs, refs[2 * n], refs[2 * n + 1], per_dest, peers):
            cp.wait_send()
            cp.wait_recv()

    res = pl.pallas_call(
        body, name=name, in_specs=[HBM_SPEC] * (2 * n) + [SEM_SPEC, SEM_SPEC, HBM_SPEC],
        out_shape=[pltpu.HBM(s.shape, s.dtype) for s in srcs] + [pltpu.HBM(z.shape, z.dtype) for z in zones],
        out_specs=[HBM_SPEC] * (2 * n), input_output_aliases={i: i for i in range(2 * n)},
        compiler_params=pltpu.CompilerParams(has_side_effects=DATAFLOW),
    )(*srcs, *zones, send_sems, recv_sems, after)
    return list(res[n:])


def _gather_finish(zones, name):
    n = len(zones)

    def body(*refs):
        zone_in, zone_out = refs[:n], refs[n:2 * n]
        send_sems, recv_sems = refs[2 * n:]
        x, y, c, chips = _place()
        cps = []
        for a in range(n):
            for j, (cx, cy) in enumerate(chips):
                blk = 4 * cx + 2 * cy + c
                cp = pltpu.make_async_remote_copy(src_ref=zone_in[a].at[blk], dst_ref=zone_out[a].at[blk],
                                                  send_sem=send_sems.at[a, j], recv_sem=recv_sems.at[a, j],
                                                  device_id=(x, y, 1 - c), device_id_type=MESH)
                cp.start()
                cps.append(cp)
        for cp in cps:
            cp.wait()

    return pl.pallas_call(
        body, in_specs=[HBM_SPEC] * n, out_specs=[HBM_SPEC] * n,
        out_shape=[jax.ShapeDtypeStruct(z.shape, z.dtype) for z in zones], input_output_aliases={a: a for a in range(n)},
        scratch_shapes=[pltpu.SemaphoreType.DMA((n, 3)), pltpu.SemaphoreType.DMA((n, 3))], name=name,
    )(*zones)


def _adamw(w, g, m, v):
    m = ADAM_B1 * m + (1.0 - ADAM_B1) * g
    v = ADAM_B2 * v + (1.0 - ADAM_B2) * jnp.square(g)
    m_hat = m / (1.0 - ADAM_B1 ** ADAM_STEP)
    v_hat = v / (1.0 - ADAM_B2 ** ADAM_STEP)
    delta = -ADAM_LR * (m_hat / (jnp.sqrt(v_hat) + ADAM_EPS) + ADAM_WD * w)
    return delta, m, v


def _adamw_layer(w, m, v, own, parts, layer, prev, name):
    _, rows, c = w.shape
    tm = _pick_tm(rows, c)

    def body(w_ref, m_ref, v_ref, own_ref, parts_ref, *rest):
        g = own_ref[...]
        for j in range(parts_ref.shape[0]):
            g = g + parts_ref[j].astype(F32)
        delta, m2, v2 = _adamw(w_ref[...], g, m_ref[...], v_ref[...])
        g_ref, d_ref, m2_ref, v2_ref = rest[-4:]
        g_ref[...], d_ref[...], m2_ref[...], v2_ref[...] = g, delta, m2, v2

    wspec = pl.BlockSpec((None, tm, c), lambda i: (layer, i, 0))
    prev = list(prev) if prev is not None else []
    return pl.pallas_call(
        body, grid=(rows // tm,),
        in_specs=[wspec] * 3 + [pl.BlockSpec((tm, c), lambda i: (i, 0)), pl.BlockSpec((parts.shape[0], tm, c), lambda i: (0, i, 0))]
        + [HBM_SPEC] * len(prev),
        out_specs=[wspec] * 4, out_shape=[jax.ShapeDtypeStruct(w.shape, F32)] * 4,
        input_output_aliases={5 + k: k for k in range(len(prev))}, name=name,
    )(w, m, v, own, parts, *prev)


W_IN_ADAMW_TM = 128


def _adamw_w_in(w_t, m_t, v_t, grads, name):
    tm = W_IN_ADAMW_TM
    ng = [p.shape[0] for _, p in grads]

    def body(w_ref, m_ref, v_ref, *rest):
        outs = rest[2 * DEPTH:]
        for layer in range(DEPTH):
            own_ref, parts_ref = rest[2 * layer], rest[2 * layer + 1]
            g = own_ref[...]
            for j in range(ng[layer]):
                g = g + parts_ref[j].astype(F32)
            delta, m2, v2 = _adamw(w_ref[:, layer, :].T, g, m_ref[:, layer, :].T, v_ref[:, layer, :].T)
            for o_ref, val in zip(outs, (g, delta, m2, v2)):
                o_ref[:, layer, :] = val.T

    tspec = pl.BlockSpec((W_IN_SHARD, DEPTH, tm), lambda i: (0, 0, i))
    gspecs = []
    for _, parts in grads:
        gspecs += [pl.BlockSpec((tm, W_IN_SHARD), lambda i: (i, 0)), pl.BlockSpec((parts.shape[0], tm, W_IN_SHARD), lambda i: (0, i, 0))]
    return pl.pallas_call(
        body, grid=(D_MODEL // tm,), in_specs=[tspec] * 3 + gspecs, out_specs=[tspec] * 4,
        out_shape=[jax.ShapeDtypeStruct(w_t.shape, F32)] * 4, name=name,
    )(w_t, m_t, v_t, *[x for pair in grads for x in pair])


def _adamw_call(w, m, v, gparts, name):
    def fn(wv, mv, vv, *gs):
        g = gs[0].astype(F32)
        for gp in gs[1:]:
            g = g + gp.astype(F32)
        delta, m2, v2 = _adamw(wv, g, mv, vv)
        return g, delta, m2, v2
    rows, c = w.shape
    return _ew(fn, [w, m, v] + list(gparts), [], [(c, F32)] * 4, tm=_pick_tm(rows, c), name=name)


def kernel(x, p, g_mix, w_in, b_fox_f, fox_q_gain, fox_k_gain, sc_conv_w, dn_conv_w, dn_a_log, dn_dt_bias,
           dn_norm_gain, w_branch, w_o, g_ffn, w_up, ffn_conv_w, w_down, g_ple, w_ple_gate, w_ple, loss_target,
           m_g_mix, m_w_in, m_b_fox_f, m_fox_q_gain, m_fox_k_gain, m_sc_conv_w, m_dn_conv_w, m_dn_a_log,
           m_dn_dt_bias, m_dn_norm_gain, m_w_branch, m_w_o, m_g_ffn, m_w_up, m_ffn_conv_w, m_w_down, m_g_ple,
           m_w_ple_gate, m_w_ple, v_g_mix, v_w_in, v_b_fox_f, v_fox_q_gain, v_fox_k_gain, v_sc_conv_w, v_dn_conv_w,
           v_dn_a_log, v_dn_dt_bias, v_dn_norm_gain, v_w_branch, v_w_o, v_g_ffn, v_w_up, v_ffn_conv_w, v_w_down,
           v_g_ple, v_w_ple_gate, v_w_ple):
    return _step(x, p, g_mix, w_in, b_fox_f, fox_q_gain, fox_k_gain, sc_conv_w, dn_conv_w, dn_a_log, dn_dt_bias,
                 dn_norm_gain, w_branch, w_o, g_ffn, w_up, ffn_conv_w, w_down, g_ple, w_ple_gate, w_ple, loss_target,
                 m_g_mix, m_w_in, m_b_fox_f, m_fox_q_gain, m_fox_k_gain, m_sc_conv_w, m_dn_conv_w, m_dn_a_log,
                 m_dn_dt_bias, m_dn_norm_gain, m_w_branch, m_w_o, m_g_ffn, m_w_up, m_ffn_conv_w, m_w_down, m_g_ple,
                 m_w_ple_gate, m_w_ple, v_g_mix, v_w_in, v_b_fox_f, v_fox_q_gain, v_fox_k_gain, v_sc_conv_w,
                 v_dn_conv_w, v_dn_a_log, v_dn_dt_bias, v_dn_norm_gain, v_w_branch, v_w_o, v_g_ffn, v_w_up,
                 v_ffn_conv_w, v_w_down, v_g_ple, v_w_ple_gate, v_w_ple)


def _step(*args):
    names = ['x', 'p'] + WEIGHTS + ['loss_target'] + ['m_' + n for n in WEIGHTS] + ['v_' + n for n in WEIGHTS]
    assert len(args) == len(names)
    a = dict(zip(names, args))
    for n, perm in HELD_TRANSPOSED.items():
        for k in (n, 'm_' + n, 'v_' + n):
            a[k] = jnp.transpose(a[k], perm)
    x, target = a['x'][0], a['loss_target'][0]
    p = a['p'][:, 0]
    dev = 4 * lax.axis_index("x") + 2 * lax.axis_index("y") + lax.axis_index("c")

    LATE = [n for n in BIG if n != 'w_in']
    sps = [{n: a[n][layer][None, :] for n in SMALL} for layer in range(DEPTH)]
    shards = [{n: a[n][layer].astype(BF16) for n in BIG} for layer in range(DEPTH)]

    def by_device(full, n):
        by_dest = _by_dest(full, n)
        return lax.dynamic_index_in_dim(by_dest, dev, axis=0, keepdims=False), by_dest.astype(BF16)

    def blank(zone):
        return lax.dynamic_update_index_in_dim(zone, jnp.zeros(zone.shape[1:], zone.dtype), dev, 0)

    w_in_all, conv_all = _all_gather([shards[0]['w_in'], _flat_pack([a[n] for n in CONVW])], 'gather_w_in_l0')
    conv_by_dev = [_flat_unpack(conv_all[d], CONV_SHARD_SHAPES) for d in range(N_DEV)]
    conv_full = {n: jnp.concatenate([conv_by_dev[d][i] for d in range(N_DEV)], axis=2) for i, n in enumerate(CONVW)}
    late0_started, zero = _direct_start([shards[0][n] for n in LATE], False, w_in_all, 'weights_rest_l0_start', NEAR_PEERS)
    w1_started, zero = _direct_start([shards[1][n] for n in BIG], False, late0_started[2][0], 'weights_l1_start', NEAR_PEERS)
    sps[0]['g_mix'] = sps[0]['g_mix'] + zero

    def late_weights0(after):
        zones = _gather_finish(_direct_wait(late0_started, after, 'weights_rest_l0_wait'), 'weights_rest_l0_finish')
        return {n: _full_from_gathered(lax.dynamic_update_index_in_dim(z, shards[0][n], dev, 0), n) for n, z in zip(LATE, zones)}

    w0 = {n: conv_full[n][0] for n in CONVW}
    w0['w_in'] = _w_in_assemble(w_in_all, 'w_in_assemble_l0')
    h, sv0 = _layer_fwd(x, p[0], w0, sps[0], 'l0', late_weights0)
    zones = [lax.dynamic_update_index_in_dim(z, shards[1][n], dev, 0)
             for n, z in zip(BIG, _gather_finish(_direct_wait(w1_started, h, 'weights_l1_wait'), 'weights_l1_finish'))]
    w1 = {n: _full_from_gathered(z, n) for n, z in zip(BIG, zones) if n != 'w_in'}
    w1.update({n: conv_full[n][1] for n in CONVW})
    w1['w_in'] = _w_in_assemble(zones[list(BIG).index('w_in')], 'w_in_assemble_l1')
    h, sv1 = _layer_fwd(h, p[1], w1, sps[1], 'l1')
    dh, loss_part = _loss_bwd(h, target, 'loss')
    grads = [None] * DEPTH
    dh, grads[1] = _layer_bwd(dh, sv1, sps[1], 'l1')
    owns1, sends1 = {}, {}
    for n in BIG:
        owns1[n], sends1[n] = _w_in_split_by_device(grads[1][n], 'w_in_grad_split_l1') if n == 'w_in' else by_device(grads[1][n], n)
    g1_started, zero = _direct_start([sends1[n] for n in BIG], True, dh, 'grads_l1_start')
    early = {}

    def early_grads0(g, sp):
        owns, sends = zip(*[by_device(g[n], n) for n in LATE])
        early['started'], zero = _direct_start(list(sends), True, g['w_branch'], 'grads_rest_l0_start')
        early['owns'] = dict(zip(LATE, owns))
        return {**sp, 'dn_norm_gain': sp['dn_norm_gain'] + zero}

    def last_grad0(g, sp):
        early['own_in'], send = _w_in_split_by_device(g['w_in'], 'w_in_grad_split_l0')
        early['in_started'], zero = _direct_start([send], True, g['w_in'], 'grads_w_in_l0_start')
        return {**sp, 'g_mix': sp['g_mix'] + zero, 'zero_row': jnp.zeros((1, D_MODEL), F32) + zero}

    dh, grads[0] = _layer_bwd(dh, sv0, {**sps[0], 'g_ple': sps[0]['g_ple'] + zero}, 'l0', early_grads0, last_grad0)
    grad_x = dh[None]
    own_in0, in0_started = early['own_in'], early['in_started']
    sent = in0_started[2][0]
    zones1 = dict(zip(BIG, _direct_wait(g1_started, sent, 'grads_l1_wait')))
    zones0 = dict(zip(LATE, _direct_wait(early['started'], sent, 'grads_rest_l0_wait')))

    def adamw(n, layer, own, zone, prev):
        rows = (-1, a[n].shape[-1])
        own = own.reshape(rows)
        view = lambda t: t.reshape((DEPTH,) + own.shape)
        return _adamw_layer(view(a[n]), view(a['m_' + n]), view(a['v_' + n]), own, blank(zone).reshape((N_DEV,) + own.shape),
                            layer, prev, f'adamw_{n}_l{layer}')

    out, done = {}, []
    for n in LATE:
        res = adamw(n, 0, early['owns'][n], zones0[n], adamw(n, 1, owns1[n], zones1[n], None))
        out[n] = [r.reshape(a[n].shape) for r in res]
        done.append(res[0][0, 0:1, 0:1])
    (zone_in0,) = _direct_wait(in0_started, jnp.concatenate(done + [dh[0:1, 0:1]], axis=1), 'grads_w_in_l0_wait')
    res = _adamw_w_in(*[jnp.transpose(a[k], (2, 0, 1)) for k in ('w_in', 'm_w_in', 'v_w_in')],
                      [(own_in0, blank(zone_in0)), (owns1['w_in'], blank(zones1['w_in']))], 'adamw_w_in')
    out['w_in'] = [jnp.transpose(r, (1, 2, 0)) for r in res]
    last_update = res[0]

    small_pack = _flat_pack([jnp.stack([grads[layer][n].reshape(-1) for layer in range(DEPTH)]) for n in SMALL] + [loss_part[0, 0]])
    conv_pack = _flat_pack([jnp.stack([grads[layer][n] for layer in range(DEPTH)]) for n in CONVW])
    small_started, _ = _direct_start([small_pack, conv_pack], False, small_pack, 'small_grads_start')
    small_all, conv_all = [lax.dynamic_update_index_in_dim(z, s, dev, 0) for z, s in
                           zip(_direct_wait(small_started, last_update, 'small_grads_wait'), (small_pack, conv_pack))]
    res = _adamw_call(_flat_pack([a[n] for n in SMALL]), _flat_pack([a['m_' + n] for n in SMALL]),
                      _flat_pack([a['v_' + n] for n in SMALL]), [small_all[d] for d in range(N_DEV)], 'adamw_replicated')
    loss = res[0].reshape(-1)[SMALL_LOSS_AT]
    for k, r in enumerate(res):
        for n, val in zip(SMALL, _flat_unpack(r, SMALL_SHAPES)):
            out.setdefault(n, [None] * 4)[k] = val
    (conv_sum,) = _ew(lambda *gs: (functools.reduce(lambda s, t: s + t, gs),), [conv_all[d] for d in range(N_DEV)], [],
                      [(PACK_W, F32)], tm=conv_pack.shape[0], name='conv_grad_sum')
    conv_own = [lax.dynamic_slice_in_dim(g, dev * (g.shape[2] // N_DEV), g.shape[2] // N_DEV, axis=2)
                for g in _flat_unpack(conv_sum, CONV_FULL_SHAPES)]
    res = _adamw_call(_flat_pack([a[n] for n in CONVW]), _flat_pack([a['m_' + n] for n in CONVW]),
                      _flat_pack([a['v_' + n] for n in CONVW]), [_flat_pack(conv_own)], 'adamw_conv')
    for k, r in enumerate(res):
        for n, val in zip(CONVW, _flat_unpack(r, CONV_SHARD_SHAPES)):
            out.setdefault(n, [None] * 4)[k] = val

    for n, perm in HELD_TRANSPOSED.items():
        out[n] = [jnp.transpose(r, perm) for r in out[n]]
    outs = [loss, grad_x]
    for k in range(4):
        outs += [out[n][k] for n in WEIGHTS]
    return tuple(outs)
```

```python
import functools

import jax
import jax.numpy as jnp
from jax import lax
from jax.experimental import pallas as pl
from jax.experimental.pallas import tpu as pltpu

F32 = jnp.float32
BF16 = jnp.bfloat16
HI = lax.Precision.HIGHEST

D_MODEL = 1024
DEPTH = 2
N_DEV = 8
PLE_DIM = 256
BW = 512
FOX_HEADS, FOX_DH = 8, 64
DN_HEADS, DN_DH = 4, 128
DN_CHUNK = 64
D_FF = 2816
EPS = 1e-6
NEG = -1e30

ADAM_LR, ADAM_B1, ADAM_B2, ADAM_EPS, ADAM_WD, ADAM_STEP = 0.001, 0.9, 0.999, 1e-08, 0.01, 10

C_FQ, C_FK, C_FV = 0, 512, 1024
C_SB, C_SC, C_SV = 1536, 2048, 2560
C_DQ, C_DK, C_DV, C_DZ = 3072, 3584, 4096, 4608
C_GATE = 5120
C_SMALL = 8192
IN_P = 8320
IN_ORIG = 8208

WEIGHTS = ['g_mix', 'w_in', 'b_fox_f', 'fox_q_gain', 'fox_k_gain', 'sc_conv_w', 'dn_conv_w', 'dn_a_log',
           'dn_dt_bias', 'dn_norm_gain', 'w_branch', 'w_o', 'g_ffn', 'w_up', 'ffn_conv_w', 'w_down', 'g_ple',
           'w_ple_gate', 'w_ple']
BIG = {'w_in': 1, 'w_branch': 2, 'w_o': 0, 'w_up': 0, 'w_down': 0, 'w_ple_gate': 0, 'w_ple': 1}
HELD_TRANSPOSED = {'w_up': (0, 2, 1)}
CONVW = {'sc_conv_w': 1, 'dn_conv_w': 1, 'ffn_conv_w': 1}
SHARDED = {**BIG, **CONVW}
SMALL = [n for n in WEIGHTS if n not in SHARDED]
MESH = pl.DeviceIdType.MESH


def _sigmoid(x):
    return 0.5 * (jnp.tanh(0.5 * x) + 1.0)


def _silu(x):
    return x * _sigmoid(x)


def _log1pexp_negabs(z):
    return jnp.log(1.0 + jnp.exp(-jnp.abs(z)))


def _log_sigmoid(z):
    return jnp.minimum(z, 0.0) - _log1pexp_negabs(z)


def _softplus(z):
    return jnp.maximum(z, 0.0) + _log1pexp_negabs(z)


def _rms(x, g):
    return x * lax.rsqrt(jnp.mean(x * x, axis=-1, keepdims=True) + EPS) * g


def _l2(x):
    return x * lax.rsqrt(jnp.sum(x * x, axis=-1, keepdims=True) + EPS)


def _dot(a, b, dims, precision=None):
    return lax.dot_general(a, b, (dims, ((), ())), preferred_element_type=F32, precision=precision)


NN = ((1,), (0,))
NT = ((1,), (1,))
TN = ((0,), (0,))


def _shift_down(x, s):
    if s == 0:
        return x
    t = lax.broadcasted_iota(jnp.int32, x.shape, 0)
    return jnp.where(t >= s, pltpu.roll(x, s, 0), 0.0)


def _shift_up(x, s):
    if s == 0:
        return x
    n = x.shape[0]
    t = lax.broadcasted_iota(jnp.int32, x.shape, 0)
    return jnp.where(t < n - s, pltpu.roll(x, n - s, 0), 0.0)


def _conv(x, w):
    k = w.shape[0]
    y = w[k - 1:k] * x
    for j in range(k - 1):
        y = y + w[j:j + 1] * _shift_down(x, k - 1 - j)
    return y


def _conv_bwd(x, w, dy):
    k = w.shape[0]
    dx = w[k - 1:k] * dy
    dws = []
    for j in range(k - 1):
        dx = dx + w[j:j + 1] * _shift_up(dy, k - 1 - j)
        dws.append(jnp.sum(dy * _shift_down(x, k - 1 - j), axis=0, keepdims=True))
    dws.append(jnp.sum(dy * x, axis=0, keepdims=True))
    return dx, dws


MM_VMEM_BUDGET = 36 << 20
MM_STEP_BYTES = 1 << 20


def _mm_tiles(m, n, k, a_size, b_size, tile_size, cast_a):
    best = None
    for tm in [d for d in (2048, 1024, 512, 256, 128) if d <= m and m % d == 0] or [m]:
        for tn in [d for d in range(128, min(n, 2048) + 1, 128) if n % d == 0] or [n]:
            vmem = 2 * tm * k * a_size + (2 * tm * k if cast_a else 0) + 2 * k * tn * b_size + 2 * tm * tn * tile_size
            if vmem > MM_VMEM_BUDGET:
                continue
            steps = (m // tm) * (n // tn)
            cost = m * k * a_size + (m // tm) * k * n * b_size + m * n * tile_size + steps * MM_STEP_BYTES
            if best is None or cost < best[0]:
                best = (cost, tm, tn)
    assert best is not None, (m, n, k)
    return best[1], best[2]


def _mm(a, b, mode, outs, *, epi=None, epi_args=(), name):
    if mode == 'nn':
        (m, k), (k2, n) = a.shape, b.shape
    elif mode == 'nt':
        (m, k), (n, k2) = a.shape, b.shape
    else:
        (k, m), (k2, n) = a.shape, b.shape
    assert k == k2, (a.shape, b.shape, mode)
    tile_size = sum(jnp.dtype(dt).itemsize for dt in outs) + sum(e.dtype.itemsize for e in epi_args if e.shape[0] != 1)
    tm, tn = _mm_tiles(m, n, k, a.dtype.itemsize, b.dtype.itemsize, tile_size, a.dtype != BF16)
    dims = {'nn': NN, 'nt': NT, 'tn': TN}[mode]
    a_spec = pl.BlockSpec((k, tm), lambda i, j: (0, i)) if mode == 'tn' else pl.BlockSpec((tm, k), lambda i, j: (i, 0))
    b_spec = pl.BlockSpec((tn, k), lambda i, j: (j, 0)) if mode == 'nt' else pl.BlockSpec((k, tn), lambda i, j: (0, j))
    e_specs = [pl.BlockSpec((1, tn), lambda i, j: (0, j)) if e.shape[0] == 1 else pl.BlockSpec((tm, tn), lambda i, j: (i, j))
               for e in epi_args]
    ne, no = len(epi_args), len(outs)
    cast_a = a.dtype != BF16

    def body(a_ref, b_ref, *rest):
        if cast_a:
            a_sc = rest[-1]

            @pl.when(pl.program_id(1) == 0)
            def _():
                a_sc[...] = a_ref[...].astype(BF16)
            av = a_sc[...]
        else:
            av = a_ref[...]
        acc = _dot(av, b_ref[...].astype(BF16), dims)
        vals = epi(acc, *[e[...] for e in rest[:ne]]) if epi is not None else (acc,)
        for o_ref, v in zip(rest[ne:ne + no], vals):
            o_ref[...] = v.astype(o_ref.dtype)

    res = pl.pallas_call(
        body, grid=(m // tm, n // tn),
        in_specs=[a_spec, b_spec] + e_specs,
        out_specs=[pl.BlockSpec((tm, tn), lambda i, j: (i, j)) for _ in outs],
        out_shape=[jax.ShapeDtypeStruct((m, n), dt) for dt in outs],
        scratch_shapes=[pltpu.VMEM(a_spec.block_shape, BF16)] if cast_a else [],
        name=name,
    )(a, b, *epi_args)
    return res[0] if no == 1 else res


def _ew(fn, tiled, bcast, outs, reds=(), *, tm=256, name):
    secs = [(t, 0, t.shape[1]) if not isinstance(t, tuple) else t for t in tiled]
    m = secs[0][0].shape[0]
    tm = min(tm, m)
    assert m % tm == 0
    in_specs = []
    for arr, off, w in secs:
        assert off % w == 0
        in_specs.append(pl.BlockSpec((tm, w), functools.partial(lambda i, c: (i, c), c=off // w)))
    in_specs += [pl.BlockSpec(b.shape, lambda i: (0, 0)) for b in bcast]
    nin, no = len(in_specs), len(outs)

    def body(*refs):
        vals = fn(*[r[...] for r in refs[:nin]])
        for r, v in zip(refs[nin:nin + no], vals[:no]):
            r[...] = v.astype(r.dtype)
        i = pl.program_id(0)
        for r, v in zip(refs[nin + no:], vals[no:]):
            @pl.when(i == 0)
            def _():
                r[...] = v

            @pl.when(i > 0)
            def _():
                r[...] += v

    res = pl.pallas_call(
        body, grid=(m // tm,), in_specs=in_specs,
        out_specs=[pl.BlockSpec((tm, c), lambda i: (i, 0)) for c, _ in outs] + [pl.BlockSpec(s, lambda i: (0, 0)) for s in reds],
        out_shape=[jax.ShapeDtypeStruct((m, c), dt) for c, dt in outs] + [jax.ShapeDtypeStruct(s, F32) for s in reds],
        name=name,
    )(*[s[0] for s in secs], *bcast)
    return res


def _cb(fn, cols, params, outs, pouts, *, tc, nblk, name):
    t = cols[0][0].shape[0]
    in_specs = []
    for arr, off in cols:
        assert off % tc == 0
        in_specs.append(pl.BlockSpec((t, tc), functools.partial(lambda c, o: (0, o + c), o=off // tc)))
    for arr, off in params:
        in_specs.append(pl.BlockSpec((arr.shape[0], tc), functools.partial(lambda c, o: (0, o + c), o=off // tc)))
    nin, no = len(in_specs), len(outs)

    def body(*refs):
        vals = fn(*[r[...] for r in refs[:nin]])
        for r, v in zip(refs[nin:], vals):
            r[...] = v.astype(r.dtype)

    return pl.pallas_call(
        body, grid=(nblk,), in_specs=in_specs,
        out_specs=[pl.BlockSpec((t, tc), lambda c: (0, c)) for _ in outs] + [pl.BlockSpec((k, tc), lambda c: (0, c)) for k in pouts],
        out_shape=[jax.ShapeDtypeStruct((t, nblk * tc), dt) for dt in outs] + [jax.ShapeDtypeStruct((k, nblk * tc), F32) for k in pouts],
        name=name,
    )(*[c[0] for c in cols], *[p[0] for p in params])


CUM_BLK = 256


def _prep_point(s, bias, alog):
    lane = lax.broadcasted_iota(jnp.int32, s.shape, 1)
    z = s + bias
    return jnp.where(lane < 8, _log_sigmoid(z),
                     jnp.where(lane < 12, _sigmoid(s),
                               jnp.where(lane < 16, -jnp.exp(alog) * _softplus(z), 0.0)))


def _tri(n, upper):
    r = lax.broadcasted_iota(jnp.int32, (n, n), 0)
    c = lax.broadcasted_iota(jnp.int32, (n, n), 1)
    return (r <= c if upper else r >= c).astype(F32)


def _prep_fwd(proj, bias_row, alog_row, name):
    t = proj.shape[0]
    nb = t // CUM_BLK

    def body(s_ref, b_ref, a_ref, o_ref):
        pre = _prep_point(s_ref[...], b_ref[...], a_ref[...])
        lane = lax.broadcasted_iota(jnp.int32, (CUM_BLK, 128), 1)
        tri = _tri(CUM_BLK, False)
        carry = jnp.zeros((1, 128), F32)
        for blk in range(nb):
            xb = pre[blk * CUM_BLK:(blk + 1) * CUM_BLK]
            cb = _dot(tri, xb, NN, HI) + carry
            carry = cb[CUM_BLK - 1:CUM_BLK]
            o_ref[blk * CUM_BLK:(blk + 1) * CUM_BLK, :] = jnp.where(lane < 8, cb, xb)

    return pl.pallas_call(
        body, grid=(1,),
        in_specs=[pl.BlockSpec((t, 128), lambda i: (0, 0)), pl.BlockSpec((1, 128), lambda i: (0, 0)),
                  pl.BlockSpec((1, 128), lambda i: (0, 0))],
        out_specs=pl.BlockSpec((t, 128), lambda i: (0, 0)),
        out_shape=jax.ShapeDtypeStruct((t, 128), F32), name=name,
    )(proj, bias_row, alog_row)


def _prep_bwd(proj, bias_row, alog_row, daux, name):
    t = proj.shape[0]
    nb = t // CUM_BLK

    def body(s_ref, b_ref, a_ref, d_ref, ds_ref, db_ref, da_ref, dpre_sc):
        lane = lax.broadcasted_iota(jnp.int32, (CUM_BLK, 128), 1)
        tri = _tri(CUM_BLK, True)
        carry = jnp.zeros((1, 128), F32)
        for blk in reversed(range(nb)):
            db = d_ref[blk * CUM_BLK:(blk + 1) * CUM_BLK, :]
            cb = _dot(tri, db, NN, HI) + carry
            carry = cb[0:1]
            dpre_sc[blk * CUM_BLK:(blk + 1) * CUM_BLK, :] = jnp.where(lane < 8, cb, db)
        _, vjp = jax.vjp(_prep_point, s_ref[...], b_ref[...], a_ref[...])
        ds, dbias, dalog = vjp(dpre_sc[...])
        ds_ref[...] = ds.astype(ds_ref.dtype)
        db_ref[...] = dbias
        da_ref[...] = dalog

    return pl.pallas_call(
        body, grid=(1,),
        in_specs=[pl.BlockSpec((t, 128), lambda i: (0, 0)), pl.BlockSpec((1, 128), lambda i: (0, 0)),
                  pl.BlockSpec((1, 128), lambda i: (0, 0)), pl.BlockSpec((t, 128), lambda i: (0, 0))],
        out_specs=[pl.BlockSpec((t, 128), lambda i: (0, 0)), pl.BlockSpec((1, 128), lambda i: (0, 0)),
                   pl.BlockSpec((1, 128), lambda i: (0, 0))],
        out_shape=[jax.ShapeDtypeStruct((t, 128), BF16), jax.ShapeDtypeStruct((1, 128), F32), jax.ShapeDtypeStruct((1, 128), F32)],
        scratch_shapes=[pltpu.VMEM((t, 128), F32)], name=name,
    )(proj, bias_row, alog_row, daux)


ATT_TQ = 256
FOX_SCALE = FOX_DH ** -0.5


def _qnorm(q, g):
    return _rms(q, g) * FOX_SCALE


def _att_scores(qn_blk, kn, cfc_blk, cfr, qi, tq, kend):
    s = _dot(qn_blk.astype(BF16), kn[:kend].astype(BF16), NT) + cfc_blk - cfr[:, :kend]
    row = lax.broadcasted_iota(jnp.int32, (tq, kend), 0) + qi * tq
    col = lax.broadcasted_iota(jnp.int32, (tq, kend), 1)
    return s, row >= col


ATT_PAIR = 128 // FOX_DH


def _att_specs(t):
    pair = lambda off: pl.BlockSpec((t, 128), functools.partial(lambda i, o: (0, o + i), o=off // 128))
    gain = pl.BlockSpec((1, FOX_DH), lambda i: (0, 0))
    hd = lambda i: (i, 0, 0)
    col, row = pl.BlockSpec((ATT_PAIR, t, 1), hd), pl.BlockSpec((ATT_PAIR, 1, t), hd)
    return [pair(C_FQ), pair(C_FK), pair(C_FV), gain, gain, col, row], pl.BlockSpec((t, 128), lambda i: (0, i)), col, row


def _att_fwd(proj, qg, kg, cfc, cfr, name):
    t = proj.shape[0]
    tq = min(ATT_TQ, t)
    in_specs, pair_out, col, _ = _att_specs(t)

    def body(q_ref, k_ref, v_ref, qg_ref, kg_ref, cfc_ref, cfr_ref, o_ref, lse_ref):
        for e in range(ATT_PAIR):
            lanes = slice(e * FOX_DH, (e + 1) * FOX_DH)
            qn = _qnorm(q_ref[:, lanes].astype(F32), qg_ref[...])
            kn = _rms(k_ref[:, lanes].astype(F32), kg_ref[...])
            v = v_ref[:, lanes].astype(BF16)
            cfr = cfr_ref[e]
            for qi in range(t // tq):
                kend = (qi + 1) * tq
                rows = slice(qi * tq, kend)
                s, mask = _att_scores(qn[rows], kn, cfc_ref[e, rows, :], cfr, qi, tq, kend)
                s = jnp.where(mask, s, NEG)
                m = jnp.max(s, axis=1, keepdims=True)
                p = jnp.exp(s - m)
                l = jnp.sum(p, axis=1, keepdims=True)
                o_ref[rows, lanes] = _dot(p.astype(BF16), v[:kend], NN) / l
                lse_ref[e, rows, :] = m + jnp.log(l)

    return pl.pallas_call(
        body, grid=(FOX_HEADS // ATT_PAIR,), in_specs=in_specs, out_specs=[pair_out, col],
        out_shape=[jax.ShapeDtypeStruct((t, BW), F32), jax.ShapeDtypeStruct((FOX_HEADS, t, 1), F32)], name=name,
    )(proj, proj, proj, qg, kg, cfc, cfr)


def _att_bwd(proj, qg, kg, cfc, cfr, lse, o, do, name):
    t = proj.shape[0]
    tq = min(ATT_TQ, t)
    in_specs, pair_out, col, row = _att_specs(t)

    def body(q_ref, k_ref, v_ref, qg_ref, kg_ref, cfc_ref, cfr_ref, lse_ref, o_ref, do_ref,
             dq_ref, dk_ref, dv_ref, dcfc_ref, dcfr_ref, dqg_ref, dkg_ref, dqn_sc, dkn_sc, dv_sc, dcfr_sc):
        for e in range(ATT_PAIR):
            lanes = slice(e * FOX_DH, (e + 1) * FOX_DH)
            qn, vjp_q = jax.vjp(_qnorm, q_ref[:, lanes].astype(F32), qg_ref[...])
            kn, vjp_k = jax.vjp(_rms, k_ref[:, lanes].astype(F32), kg_ref[...])
            v = v_ref[:, lanes].astype(BF16)
            cfr = cfr_ref[e]
            do_e = do_ref[:, lanes]
            delta = jnp.sum(do_e * o_ref[:, lanes], axis=1, keepdims=True)
            dkn_sc[...] = jnp.zeros_like(dkn_sc)
            dv_sc[...] = jnp.zeros_like(dv_sc)
            dcfr_sc[...] = jnp.zeros_like(dcfr_sc)
            for qi in range(t // tq):
                kend = (qi + 1) * tq
                rows = slice(qi * tq, kend)
                s, mask = _att_scores(qn[rows], kn, cfc_ref[e, rows, :], cfr, qi, tq, kend)
                p = jnp.where(mask, jnp.exp(jnp.where(mask, s, NEG) - lse_ref[e, rows, :]), 0.0)
                do_b = do_e[rows].astype(BF16)
                dv_sc[0:kend, :] += _dot(p.astype(BF16), do_b, TN)
                dp = _dot(do_b, v[:kend], NT)
                ds = p * (dp - delta[rows])
                ds_b = ds.astype(BF16)
                dqn_sc[rows, :] = _dot(ds_b, kn[:kend].astype(BF16), NN)
                dkn_sc[0:kend, :] += _dot(ds_b, qn[rows].astype(BF16), TN)
                dcfc_ref[e, rows, :] = jnp.sum(ds, axis=1, keepdims=True)
                dcfr_sc[:, 0:kend] -= jnp.sum(ds, axis=0, keepdims=True)
            dq, dqg = vjp_q(dqn_sc[...])
            dk, dkg = vjp_k(dkn_sc[...])
            dq_ref[:, lanes] = dq.astype(dq_ref.dtype)
            dk_ref[:, lanes] = dk.astype(dk_ref.dtype)
            dv_ref[:, lanes] = dv_sc[...].astype(dv_ref.dtype)
            dcfr_ref[e] = dcfr_sc[...]
            dqg_ref[e] = dqg
            dkg_ref[e] = dkg

    gsp = pl.BlockSpec((ATT_PAIR, 1, FOX_DH), lambda i: (i, 0, 0))
    return pl.pallas_call(
        body, grid=(FOX_HEADS // ATT_PAIR,),
        in_specs=in_specs + [col, pair_out, pair_out],
        out_specs=[pair_out] * 3 + [col, row, gsp, gsp],
        out_shape=[jax.ShapeDtypeStruct((t, BW), BF16)] * 3
        + [jax.ShapeDtypeStruct((FOX_HEADS, t, 1), F32), jax.ShapeDtypeStruct((FOX_HEADS, 1, t), F32)]
        + [jax.ShapeDtypeStruct((FOX_HEADS, 1, FOX_DH), F32)] * 2,
        scratch_shapes=[pltpu.VMEM((t, FOX_DH), F32)] * 3 + [pltpu.VMEM((1, t), F32)], name=name,
    )(proj, proj, proj, qg, kg, cfc, cfr, lse, o, do)


DN_SCALE = DN_DH ** -0.5


DN_BATCH = 8


@functools.partial(jax.custom_vjp, nondiff_argnums=(2, 3))
def _mm3(a, b, dims, batch=False):
    return _mm3_passes(a, b, dims, batch)


def _mm3_fwd(a, b, dims, batch):
    return _mm3_passes(a, b, dims, batch), (a, b)


def _mm3_bwd(dims, batch, res, dc):
    a, b = res
    if dims == NN:
        return _mm3_passes(dc, b, NT, batch), _mm3_passes(a, dc, TN, batch)
    if dims == NT:
        return _mm3_passes(dc, b, NN, batch), _mm3_passes(dc, a, TN, batch)
    return _mm3_passes(b, dc, NT, batch), _mm3_passes(a, dc, NN, batch)


_mm3.defvjp(_mm3_fwd, _mm3_bwd)


def _mm3_passes(a, b, dims, batch):
    if batch:
        dn = (((dims[0][0] + 1,), (dims[1][0] + 1,)), ((0,), (0,)))
        dot = lambda p, q: lax.dot_general(p, q, dn, preferred_element_type=F32)
    else:
        dot = lambda p, q: _dot(p, q, dims)
    ah, bh = a.astype(BF16), b.astype(BF16)
    al, bl = (a - ah.astype(F32)).astype(BF16), (b - bh.astype(F32)).astype(BF16)
    return dot(ah, bh) + (dot(ah, bl) + dot(al, bh))


def _dn_local(qc, kc, vc, g, beta):
    nb, c, _ = qc.shape
    ii = lax.broadcasted_iota(jnp.int32, (c, c), 0)
    jj = lax.broadcasted_iota(jnp.int32, (c, c), 1)
    incl, strict = ii >= jj, ii > jj
    lower = jnp.broadcast_to(incl.astype(F32), (nb, c, c))
    eye = (ii == jj).astype(F32)
    mm = functools.partial(_mm3, batch=True)
    dm = mm(lower, jnp.where(strict, g, 0.0), NN)
    decay = jnp.where(incl, jnp.exp(jnp.where(incl, dm, 0.0)), 0.0)
    gcum = mm(lower, g * jnp.ones((1, 1, DN_DH), F32), NN)
    eg = jnp.exp(gcum)
    glast = gcum[:, c - 1:c]
    kb = kc * beta
    n1 = jnp.where(strict, mm(kb, kc, NT) * decay, 0.0)
    inv = eye - n1
    pw = n1
    for _ in range(5):
        pw = mm(pw, pw, NN)
        inv = inv + mm(pw, inv, NN)
    sol = mm(inv, jnp.concatenate([vc * beta, kb * eg], axis=2), NN)
    qk = jnp.where(incl, mm(qc, kc, NT) * decay, 0.0)
    return sol[:, :, :DN_DH], sol[:, :, DN_DH:], qk, qc * eg, kc * jnp.exp(glast - gcum), jnp.exp(glast)


@functools.partial(jax.custom_vjp, nondiff_argnums=(2,))
def _mm1(a, b, dims):
    return _dot(a.astype(BF16), b.astype(BF16), dims)


def _mm1_fwd(a, b, dims):
    return _mm1(a, b, dims), (a, b)


def _mm1_bwd(dims, res, dc):
    a, b = res
    if dims == NN:
        return _mm1(dc, b, NT), _mm1(a, dc, TN)
    if dims == NT:
        return _mm1(dc, b, NN), _mm1(dc, a, TN)
    return _mm1(b, dc, NT), _mm1(a, dc, NN)


_mm1.defvjp(_mm1_fwd, _mm1_bwd)


def _dn_state(u, kcum, qk, qdec, kdec, egl, state):
    v_new = u - _mm1(kcum, state, NN)
    out = _mm1(qdec, state, NN) + _mm1(qk, v_new, NN)
    return out, state * egl + _mm1(kdec, v_new, TN)


def _dn_pre_q(c):
    return _l2(_silu(c)) * DN_SCALE


def _dn_pre_k(c):
    return _l2(_silu(c))


def _dn_post(o, z, ng):
    return _rms(o, ng) * _silu(z)


def _dn_specs(t):
    cblk = lambda o: pl.BlockSpec((t, DN_DH), functools.partial(lambda h, o: (0, o + h), o=o // DN_DH))
    wblk = lambda o: pl.BlockSpec((4, DN_DH), functools.partial(lambda h, o: (0, o + h), o=o // DN_DH))
    proj_specs = [cblk(C_DQ), cblk(C_DK), cblk(C_DV), cblk(C_DZ)]
    w_specs = [wblk(0), wblk(BW), wblk(2 * BW)]
    gb_spec = pl.BlockSpec((1, t, 2), lambda h: (h, 0, 0))
    return proj_specs, w_specs, gb_spec


def _chunk_rows(n, count=1):
    return pl.ds(pl.multiple_of(n * DN_CHUNK, DN_CHUNK), count * DN_CHUNK)


def _egl_rows(n, count=1):
    return pl.ds(pl.multiple_of(n * 8, 8), count * 8)


def _dn_local_inputs(n, qn_sc, kn_sc, vv_sc, gb_ref):
    r = _chunk_rows(n, DN_BATCH)
    split = lambda v: v.reshape(DN_BATCH, DN_CHUNK, v.shape[-1])
    gbv = split(gb_ref[0, r, :])
    return split(qn_sc[r, :]), split(kn_sc[r, :]), split(vv_sc[r, :]), gbv[:, :, 0:1], gbv[:, :, 1:2]


def _dn_local_phase(nc, qn_sc, kn_sc, vv_sc, gb_ref, loc):
    def step(i, carry):
        n = i * DN_BATCH
        vals = _dn_local(*_dn_local_inputs(n, qn_sc, kn_sc, vv_sc, gb_ref))
        for sc, val in zip(loc[:5], vals[:5]):
            sc[_chunk_rows(n, DN_BATCH), :] = val.reshape(DN_BATCH * DN_CHUNK, val.shape[-1])
        loc[5][_egl_rows(n, DN_BATCH), :] = jnp.broadcast_to(vals[5], (DN_BATCH, 8, DN_DH)).reshape(DN_BATCH * 8, DN_DH)
        return carry

    lax.fori_loop(0, nc // DN_BATCH, step, 0)


def _dn_loc_scratch(t, nc):
    big = pltpu.VMEM((t, DN_DH), F32)
    return [big, big, pltpu.VMEM((t, DN_CHUNK), F32), big, big, pltpu.VMEM((nc * 8, DN_DH), F32)]


def _dn_loc_io(t, nc, **spec_args):
    head_cols = pl.BlockSpec((t, DN_DH), lambda h: (0, h), **spec_args)
    per_head = lambda rows, cols: pl.BlockSpec((1, rows, cols), lambda h: (h, 0, 0), **spec_args)
    wide = jax.ShapeDtypeStruct((t, BW), F32)
    shapes = [wide, wide, jax.ShapeDtypeStruct((DN_HEADS, t, DN_CHUNK), F32), wide, wide,
              jax.ShapeDtypeStruct((DN_HEADS, nc * 8, DN_DH), F32)]
    return shapes, [head_cols, head_cols, per_head(t, DN_CHUNK), head_cols, head_cols, per_head(nc * 8, DN_DH)]


def _dn_loc_refs(refs):
    return refs[0], refs[1], refs[2].at[0], refs[3], refs[4], refs[5].at[0]


def _dn_fwd(proj, conv_w, gb, ng, name):
    t = proj.shape[0]
    nc = t // DN_CHUNK
    assert nc % DN_BATCH == 0
    proj_specs, w_specs, gb_spec = _dn_specs(t)
    loc_shapes, loc_specs = _dn_loc_io(t, nc)

    def body(q_ref, k_ref, v_ref, z_ref, wq_ref, wk_ref, wv_ref, gb_ref, ng_ref, y_ref, o_ref, st_ref, *rest):
        loc = _dn_loc_refs(rest[:6])
        qn_sc, kn_sc, vv_sc = rest[6:]
        qn_sc[...] = _dn_pre_q(_conv(q_ref[...].astype(F32), wq_ref[...]))
        kn_sc[...] = _dn_pre_k(_conv(k_ref[...].astype(F32), wk_ref[...]))
        vv_sc[...] = _silu(_conv(v_ref[...].astype(F32), wv_ref[...]))
        _dn_local_phase(nc, qn_sc, kn_sc, vv_sc, gb_ref, loc)
        u_sc, kcum_sc, qk_sc, qdec_sc, kdec_sc, egl_sc = loc

        def chunk(n, state):
            r = _chunk_rows(n)
            egl = egl_sc[_egl_rows(n), :][0:1]
            out, new_state = _dn_state(u_sc[r, :], kcum_sc[r, :], qk_sc[r, :], qdec_sc[r, :], kdec_sc[r, :], egl, state)
            st_ref[0, n] = state
            o_ref[r, :] = out
            return new_state

        lax.fori_loop(0, nc, chunk, jnp.zeros((DN_DH, DN_DH), F32))
        y_ref[...] = _dn_post(o_ref[...], z_ref[...].astype(F32), ng_ref[...])

    hblk = pl.BlockSpec((t, DN_DH), lambda h: (0, h))
    return pl.pallas_call(
        body, grid=(DN_HEADS,),
        in_specs=proj_specs + w_specs + [gb_spec, pl.BlockSpec((1, DN_DH), lambda h: (0, 0))],
        out_specs=[hblk, hblk, pl.BlockSpec((1, nc, DN_DH, DN_DH), lambda h: (h, 0, 0, 0))] + loc_specs,
        out_shape=[jax.ShapeDtypeStruct((t, BW), F32), jax.ShapeDtypeStruct((t, BW), F32),
                   jax.ShapeDtypeStruct((DN_HEADS, nc, DN_DH, DN_DH), F32)] + loc_shapes,
        scratch_shapes=[pltpu.VMEM((t, DN_DH), F32)] * 3, name=name,
    )(proj, proj, proj, proj, conv_w, conv_w, conv_w, gb, ng)


def _dn_bwd(proj, conv_w, gb, ng, o, states, loc_saved, dy, name):
    t = proj.shape[0]
    nc = t // DN_CHUNK
    proj_specs, w_specs, gb_spec = _dn_specs(t)
    _, loc_specs = _dn_loc_io(t, nc, pipeline_mode=pl.Buffered(1))

    def body(q_ref, k_ref, v_ref, z_ref, wq_ref, wk_ref, wv_ref, gb_ref, ng_ref, o_ref, st_ref, dy_ref,
             u_ref, kcum_ref, qk_ref, qdec_ref, kdec_ref, egl_ref,
             dq_ref, dk_ref, dv_ref, dz_ref, dwq_ref, dwk_ref, dwv_ref, dgb_ref, dng_ref,
             qn_sc, kn_sc, vv_sc, do_sc, *dloc):
        loc = _dn_loc_refs((u_ref, kcum_ref, qk_ref, qdec_ref, kdec_ref, egl_ref))
        qn_sc[...] = _dn_pre_q(_conv(q_ref[...].astype(F32), wq_ref[...]))
        kn_sc[...] = _dn_pre_k(_conv(k_ref[...].astype(F32), wk_ref[...]))
        vv_sc[...] = _silu(_conv(v_ref[...].astype(F32), wv_ref[...]))
        _, vjp_y = jax.vjp(_dn_post, o_ref[...], z_ref[...].astype(F32), ng_ref[...])
        do, dz, dng = vjp_y(dy_ref[...])
        do_sc[...] = do
        dz_ref[...] = dz.astype(dz_ref.dtype)
        dng_ref[0] = dng
        u_sc, kcum_sc, qk_sc, qdec_sc, kdec_sc, egl_sc = loc

        def state_bwd(i, dstate):
            n = nc - 1 - i
            r = _chunk_rows(n)
            r8 = _egl_rows(n)
            _, vjp = jax.vjp(_dn_state, u_sc[r, :], kcum_sc[r, :], qk_sc[r, :], qdec_sc[r, :], kdec_sc[r, :],
                             egl_sc[r8, :][0:1], st_ref[0, n])
            du, dkcum, dqk, dqdec, dkdec, degl, dprev = vjp((do_sc[r, :], dstate))
            for d_sc, val in zip(dloc[:5], (du, dkcum, dqk, dqdec, dkdec)):
                d_sc[r, :] = val
            dloc[5][r8, :] = jnp.broadcast_to(degl, (8, DN_DH))
            return dprev

        lax.fori_loop(0, nc, state_bwd, jnp.zeros((DN_DH, DN_DH), F32))

        def local_bwd(i, carry):
            n = i * DN_BATCH
            r = _chunk_rows(n, DN_BATCH)
            _, vjp = jax.vjp(_dn_local, *_dn_local_inputs(n, qn_sc, kn_sc, vv_sc, gb_ref))
            cts = tuple(d_sc[r, :].reshape(DN_BATCH, DN_CHUNK, d_sc.shape[-1]) for d_sc in dloc[:5])
            cts += (dloc[5][_egl_rows(n, DN_BATCH), :].reshape(DN_BATCH, 8, DN_DH)[:, 0:1],)
            dqc, dkc, dvc, dg, dbeta = vjp(cts)
            for d_sc, val in zip((dloc[0], dloc[1], dloc[3]), (dqc, dkc, dvc)):
                d_sc[r, :] = val.reshape(DN_BATCH * DN_CHUNK, DN_DH)
            dgb_ref[0, r, :] = jnp.concatenate([dg, dbeta], axis=2).reshape(DN_BATCH * DN_CHUNK, 2)
            return carry

        lax.fori_loop(0, nc // DN_BATCH, local_bwd, 0)
        for x_ref, w_ref, pre, d_sc, dx_ref, dw_ref in ((q_ref, wq_ref, _dn_pre_q, dloc[0], dq_ref, dwq_ref),
                                                       (k_ref, wk_ref, _dn_pre_k, dloc[1], dk_ref, dwk_ref),
                                                       (v_ref, wv_ref, _silu, dloc[3], dv_ref, dwv_ref)):
            x_in = x_ref[...].astype(F32)
            _, vjp = jax.vjp(pre, _conv(x_in, w_ref[...]))
            (dc,) = vjp(d_sc[...])
            dx, dws = _conv_bwd(x_in, w_ref[...], dc)
            dx_ref[...] = dx.astype(dx_ref.dtype)
            for j, dw in enumerate(dws):
                dw_ref[j:j + 1, :] = dw

    hblk = pl.BlockSpec((t, DN_DH), lambda h: (0, h))
    wout = pl.BlockSpec((4, DN_DH), lambda h: (0, h))
    return pl.pallas_call(
        body, grid=(DN_HEADS,),
        in_specs=proj_specs + w_specs + [gb_spec, pl.BlockSpec((1, DN_DH), lambda h: (0, 0)), hblk,
                                         pl.BlockSpec((1, nc, DN_DH, DN_DH), lambda h: (h, 0, 0, 0)), hblk] + loc_specs,
        out_specs=[hblk] * 4 + [wout] * 3 + [gb_spec, pl.BlockSpec((1, 1, DN_DH), lambda h: (h, 0, 0))],
        out_shape=[jax.ShapeDtypeStruct((t, BW), BF16)] * 4 + [jax.ShapeDtypeStruct((4, BW), F32)] * 3
        + [jax.ShapeDtypeStruct((DN_HEADS, t, 2), F32), jax.ShapeDtypeStruct((DN_HEADS, 1, DN_DH), F32)],
        scratch_shapes=[pltpu.VMEM((t, DN_DH), F32)] * 4 + _dn_loc_scratch(t, nc), name=name,
    )(proj, proj, proj, proj, conv_w, conv_w, conv_w, gb, ng, o, states, dy, *loc_saved)


MERGE_TM, MERGE_TN = 512, 512


def _merge_specs(t):
    tm, tn = min(MERGE_TM, t), MERGE_TN
    y_spec = pl.BlockSpec((tm, BW), lambda i, j: (i, 0))
    w_spec = pl.BlockSpec((3, BW, tn), lambda i, j: (0, 0, j))
    gate_specs = [pl.BlockSpec((tm, tn), functools.partial(lambda i, j, o: (i, o + j), o=(C_GATE + n * D_MODEL) // tn))
                  for n in range(3)]
    return tm, tn, [y_spec] * 3 + [w_spec] + gate_specs


def _merge_fwd(ys, wb, proj, name):
    t = proj.shape[0]
    tm, tn, in_specs = _merge_specs(t)

    def body(y0, y1, y2, w_ref, g0, g1, g2, o_ref):
        acc = jnp.zeros((tm, tn), F32)
        for n, (y, g) in enumerate(((y0, g0), (y1, g1), (y2, g2))):
            acc = acc + _dot(y[...].astype(BF16), w_ref[n], NN) * _sigmoid(g[...].astype(F32))
        o_ref[...] = acc.astype(o_ref.dtype)

    return pl.pallas_call(
        body, grid=(t // tm, D_MODEL // tn), in_specs=in_specs,
        out_specs=pl.BlockSpec((tm, tn), lambda i, j: (i, j)),
        out_shape=jax.ShapeDtypeStruct((t, D_MODEL), BF16), name=name,
    )(*ys, wb, proj, proj, proj)


def _merge_bwd(ys, wb, proj, dmerged, name):
    t = proj.shape[0]
    tm, tn, in_specs = _merge_specs(t)

    def body(y0, y1, y2, w_ref, g0, g1, g2, dm_ref, dg_ref, dt_ref):
        dm = dm_ref[...]
        for n, (y, g) in enumerate(((y0, g0), (y1, g1), (y2, g2))):
            tn_ = _dot(y[...].astype(BF16), w_ref[n], NN)
            sg = _sigmoid(g[...].astype(F32))
            dg_ref[n] = (dm * tn_ * sg * (1.0 - sg)).astype(dg_ref.dtype)
            dt_ref[n] = (dm * sg).astype(dt_ref.dtype)

    o3 = pl.BlockSpec((3, tm, tn), lambda i, j: (0, i, j))
    return pl.pallas_call(
        body, grid=(t // tm, D_MODEL // tn), in_specs=in_specs + [pl.BlockSpec((tm, tn), lambda i, j: (i, j))],
        out_specs=[o3, o3], out_shape=[jax.ShapeDtypeStruct((3, t, D_MODEL), BF16)] * 2, name=name,
    )(*ys, wb, proj, proj, proj, dmerged)


def _prep_rows(sp):
    z4, z112 = jnp.zeros((1, 4), F32), jnp.zeros((1, 112), F32)
    bias_row = jnp.concatenate([sp['b_fox_f'], z4, sp['dn_dt_bias'], z112], axis=1)
    alog_row = jnp.concatenate([jnp.zeros((1, 12), F32), sp['dn_a_log'], z112], axis=1)
    return bias_row, alog_row


def _sc_fwd_tile(sb, sc, sv, w):
    sb, sc, sv = sb.astype(F32), sc.astype(F32), sv.astype(F32)
    return (sb * _conv(sc * sv, w),)


def _ffn_act(ug, uv):
    return _silu(ug) * uv


W_IN_COLS = [(C_FQ, 0, 1536), (C_SB, 1544, 1536), (C_DQ, 3080, 1536), (C_DZ, 4624, 512), (C_GATE, 5136, 3072),
             (C_SMALL, 1536, 8), (C_SMALL + 8, 4616, 8)]
W_IN_SHARD = IN_ORIG // N_DEV


def _w_in_segments():
    out = []
    for d, o, w in W_IN_COLS:
        end = o + w
        while o < end:
            k = o // W_IN_SHARD
            n = min(end, (k + 1) * W_IN_SHARD) - o
            out.append((d, k, o - k * W_IN_SHARD, n))
            o, d = o + n, d + n
    return out


def _w_in_assemble(g, name):
    tm = 256

    def body(g_ref, o_ref):
        for d, k, s, n in _w_in_segments():
            o_ref[:, d:d + n] = g_ref[k, :, s:s + n]
        o_ref[:, IN_ORIG:IN_P] = jnp.zeros((tm, IN_P - IN_ORIG), o_ref.dtype)

    return pl.pallas_call(
        body, grid=(D_MODEL // tm,),
        in_specs=[pl.BlockSpec((N_DEV, tm, W_IN_SHARD), lambda i: (0, i, 0))],
        out_specs=pl.BlockSpec((tm, IN_P), lambda i: (i, 0)),
        out_shape=jax.ShapeDtypeStruct((D_MODEL, IN_P), g.dtype), name=name,
    )(g)


W_IN_SPLIT_TM = 128


def _w_in_split_by_device(g, name):
    tm = W_IN_SPLIT_TM

    def body(g_ref, own_ref, send_ref):
        me = 4 * lax.axis_index("x") + 2 * lax.axis_index("y") + lax.axis_index("c")
        for d, k, s, n in _w_in_segments():
            val = g_ref[:, d:d + n]
            send_ref[k, :, s:s + n] = val.astype(send_ref.dtype)

            @pl.when(me == k)
            def _():
                own_ref[:, s:s + n] = val

    return pl.pallas_call(
        body, grid=(D_MODEL // tm,), in_specs=[pl.BlockSpec((tm, IN_P), lambda i: (i, 0))],
        out_specs=[pl.BlockSpec((tm, W_IN_SHARD), lambda i: (i, 0)), pl.BlockSpec((N_DEV, tm, W_IN_SHARD), lambda i: (0, i, 0))],
        out_shape=[jax.ShapeDtypeStruct((D_MODEL, W_IN_SHARD), F32), jax.ShapeDtypeStruct((N_DEV, D_MODEL, W_IN_SHARD), BF16)],
        name=name,
    )(g)


def _layer_fwd(x, p_i, w, sp, tag, late_weights=None):
    t = x.shape[0]
    sv = {'x': x}
    (hn,) = _ew(lambda a, g: (_rms(a, g),), [x], [sp['g_mix']], [(D_MODEL, BF16)], name=f'rms_mix_{tag}')
    proj = _mm(hn, w['w_in'], 'nn', [BF16], name=f'in_proj_{tag}')
    small = _mm(hn, w['w_in'][:, C_SMALL:], 'nn', [F32], name=f'in_small_{tag}')
    bias_row, alog_row = _prep_rows(sp)
    aux = _prep_fwd(small, bias_row, alog_row, f'prep_{tag}')
    cf = aux[:, :FOX_HEADS].T
    cfc, cfr = cf[:, :, None], cf[:, None, :]
    y_fox, lse = _att_fwd(proj, sp['fox_q_gain'], sp['fox_k_gain'], cfc, cfr, f'fox_fwd_{tag}')
    (y_sc,) = _cb(_sc_fwd_tile, [(proj, C_SB), (proj, C_SC), (proj, C_SV)], [(w['sc_conv_w'], 0)], [F32], [],
                  tc=256, nblk=2, name=f'sc_fwd_{tag}')
    gb = jnp.stack([aux[:, 12:16].T, aux[:, 8:12].T], axis=-1)
    y_dn, o_dn, states, *dn_loc = _dn_fwd(proj, w['dn_conv_w'], gb, sp['dn_norm_gain'], f'dn_fwd_{tag}')
    ys = (y_fox, y_sc, y_dn)
    if late_weights is not None:
        w = {**w, **late_weights(y_dn)}
    merged = _merge_fwd(ys, w['w_branch'], proj, f'merge_fwd_{tag}')
    x1 = _mm(merged, w['w_o'], 'nn', [F32], epi=lambda acc, r: (acc + r,), epi_args=(x,), name=f'o_proj_{tag}')
    (hf,) = _ew(lambda a, g: (_rms(a, g),), [x1], [sp['g_ffn']], [(D_MODEL, BF16)], name=f'rms_ffn_{tag}')
    up = _mm(hf, w['w_up'], 'nt', [BF16], name=f'up_proj_{tag}')
    (act,) = _cb(lambda ug, uv, wg, wv: (_ffn_act(_conv(ug.astype(F32), wg), _conv(uv.astype(F32), wv)),), [(up, 0), (up, D_FF)],
                 [(w['ffn_conv_w'], 0), (w['ffn_conv_w'], D_FF)], [BF16], [], tc=256, nblk=D_FF // 256, name=f'ffn_act_{tag}')
    x2 = _mm(act, w['w_down'], 'nn', [F32], epi=lambda acc, r: (acc + r,), epi_args=(x1,), name=f'down_proj_{tag}')
    (hp,) = _ew(lambda a, g: (_rms(a, g),), [x2], [sp['g_ple']], [(D_MODEL, BF16)], name=f'rms_ple_{tag}')
    gp = _mm(hp, w['w_ple_gate'], 'nn', [F32], name=f'ple_gate_{tag}')
    x3 = _mm(p_i, w['w_ple'], 'nn', [F32], epi=lambda acc, g, r: (r + _sigmoid(g) * acc,), epi_args=(gp, x2), name=f'ple_{tag}')
    sv.update(hn=hn, proj=proj, small=small, aux=aux, cfc=cfc, cfr=cfr, lse=lse, ys=ys, gb=gb, o_dn=o_dn,
              states=states, merged=merged, x1=x1, hf=hf, up=up, act=act, x2=x2, hp=hp, gp=gp, p=p_i,
              bias_row=bias_row, alog_row=alog_row, w=w, dn_loc=dn_loc)
    return x3, sv


def _rms_bwd(x, g, dh, dres, name):
    def fn(xv, dhv, dr, gv):
        _, vjp = jax.vjp(_rms, xv, gv)
        dx, dg = vjp(dhv)
        return dr + dx, dg
    return _ew(fn, [x, dh, dres], [g], [(D_MODEL, F32)], [(1, D_MODEL)], name=name)


def _layer_bwd(dx3, sv, sp, tag, early_grads=None, last_grad=None):
    t = dx3.shape[0]
    w = sv['w']
    g = {}
    def ple_epi(acc, gpv, d):
        s = _sigmoid(gpv)
        return d * acc * s * (1.0 - s), d * s
    dgp, de = _mm(sv['p'], w['w_ple'], 'nn', [BF16, BF16], epi=ple_epi, epi_args=(sv['gp'], dx3), name=f'ple_bwd_{tag}')
    g['w_ple'] = _mm(sv['p'], de, 'tn', [F32], name=f'd_w_ple_{tag}')
    g['w_ple_gate'] = _mm(sv['hp'], dgp, 'tn', [F32], name=f'd_w_ple_gate_{tag}')
    dhp = _mm(dgp, w['w_ple_gate'], 'nt', [F32], name=f'd_hp_{tag}')
    dx2, g['g_ple'] = _rms_bwd(sv['x2'], sp['g_ple'], dhp, dx3, f'rms_ple_bwd_{tag}')
    dact = _mm(dx2, w['w_down'], 'nt', [F32], name=f'd_act_{tag}')
    g['w_down'] = _mm(sv['act'], dx2, 'tn', [F32], name=f'd_w_down_{tag}')

    def ffn_bwd_tile(ug, uv, da, wg, wv):
        ug, uv = ug.astype(F32), uv.astype(F32)
        cg, cv = _conv(ug, wg), _conv(uv, wv)
        _, vjp = jax.vjp(_ffn_act, cg, cv)
        dcg, dcv = vjp(da)
        dug, dwg = _conv_bwd(ug, wg, dcg)
        duv, dwv = _conv_bwd(uv, wv, dcv)
        return dug, duv, jnp.concatenate(dwg, axis=0), jnp.concatenate(dwv, axis=0)
    dupg, dupv, dwg, dwv = _cb(ffn_bwd_tile, [(sv['up'], 0), (sv['up'], D_FF), (dact, 0)],
                               [(w['ffn_conv_w'], 0), (w['ffn_conv_w'], D_FF)], [BF16, BF16], [3, 3], tc=256,
                               nblk=D_FF // 256, name=f'ffn_act_bwd_{tag}')
    dup = jnp.concatenate([dupg, dupv], axis=1)
    g['ffn_conv_w'] = jnp.concatenate([dwg, dwv], axis=1)
    g['w_up'] = _mm(dup, sv['hf'], 'tn', [F32], name=f'd_w_up_{tag}')
    dhf = _mm(dup, w['w_up'], 'nn', [F32], name=f'd_hf_{tag}')
    dx1, g['g_ffn'] = _rms_bwd(sv['x1'], sp['g_ffn'], dhf, dx2, f'rms_ffn_bwd_{tag}')
    dmerged = _mm(dx1, w['w_o'], 'nt', [F32], name=f'd_merged_{tag}')
    g['w_o'] = _mm(sv['merged'], dx1, 'tn', [F32], name=f'd_w_o_{tag}')
    dgate, dtn = _merge_bwd(sv['ys'], w['w_branch'], sv['proj'], dmerged, f'merge_bwd_{tag}')
    dys, dwb = [], []
    for n in range(3):
        dys.append(_mm(dtn[n], w['w_branch'][n], 'nt', [F32], name=f'd_y{n}_{tag}'))
        dwb.append(_mm(sv['ys'][n], dtn[n], 'tn', [F32], name=f'd_w_branch{n}_{tag}'))
    g['w_branch'] = jnp.stack(dwb)
    if early_grads is not None:
        sp = early_grads(g, sp)
    ddq, ddk, ddv, ddz, dwq, dwk, dwv_, dgb, dng = _dn_bwd(sv['proj'], w['dn_conv_w'], sv['gb'], sp['dn_norm_gain'],
                                                           sv['o_dn'], sv['states'], sv['dn_loc'], dys[2], f'dn_bwd_{tag}')
    g['dn_conv_w'] = jnp.concatenate([dwq, dwk, dwv_], axis=1)
    g['dn_norm_gain'] = jnp.sum(dng, axis=0)
    def sc_bwd_tile(sb, sc, svv, dy, wv):
        sb, sc, svv = sb.astype(F32), sc.astype(F32), svv.astype(F32)
        u = sc * svv
        dsb = dy * _conv(u, wv)
        du, dws = _conv_bwd(u, wv, dy * sb)
        return dsb, du * svv, du * sc, jnp.concatenate(dws, axis=0)
    dsb, dsc, dsv, g['sc_conv_w'] = _cb(sc_bwd_tile, [(sv['proj'], C_SB), (sv['proj'], C_SC), (sv['proj'], C_SV), (dys[1], 0)],
                                        [(w['sc_conv_w'], 0)], [BF16, BF16, BF16], [3], tc=256, nblk=2, name=f'sc_bwd_{tag}')
    dfq, dfk, dfv, dcfc, dcfr, dqg, dkg = _att_bwd(sv['proj'], sp['fox_q_gain'], sp['fox_k_gain'], sv['cfc'], sv['cfr'],
                                                   sv['lse'], sv['ys'][0], dys[0], f'fox_bwd_{tag}')
    g['fox_q_gain'] = jnp.sum(dqg, axis=0)
    g['fox_k_gain'] = jnp.sum(dkg, axis=0)
    dcf = (dcfc[:, :, 0] + dcfr[:, 0, :]).T
    daux = jnp.concatenate([dcf, dgb[:, :, 1].T, dgb[:, :, 0].T, jnp.zeros((t, 112), F32)], axis=1)
    dsmall, dbias, dalog = _prep_bwd(sv['small'], sv['bias_row'], sv['alog_row'], daux, f'prep_bwd_{tag}')
    g['b_fox_f'] = dbias[:, 0:8]
    g['dn_dt_bias'] = dbias[:, 12:16]
    g['dn_a_log'] = dalog[:, 12:16]
    dproj = jnp.concatenate([dfq, dfk, dfv, dsb, dsc, dsv, ddq, ddk, ddv, ddz, dgate[0], dgate[1], dgate[2], dsmall], axis=1)
    g['w_in'] = _mm(sv['hn'], dproj, 'tn', [F32], name=f'd_w_in_{tag}')
    if last_grad is not None:
        sp = last_grad(g, sp)
    dhn = _mm(dproj, w['w_in'], 'nt', [F32], epi=lambda acc, z: (acc + z,),
              epi_args=(sp.get('zero_row', jnp.zeros((1, D_MODEL), F32)),), name=f'd_hn_{tag}')
    dx, g['g_mix'] = _rms_bwd(sv['x'], sp['g_mix'], dhn, dx1, f'rms_mix_bwd_{tag}')
    return dx, g


def _loss_bwd(y, target, name):
    inv = 1.0 / y.shape[1]

    def fn(yv, tv):
        err = yv - tv
        return err * inv, jnp.zeros((8, 128), F32) + 0.5 * inv * jnp.sum(err * err)
    return _ew(fn, [y, target], [], [(y.shape[1], F32)], [(8, 128)], name=name)


PACK_W = 1024
FULL_SHAPE = {'w_in': (D_MODEL, IN_ORIG), 'w_branch': (3, BW, D_MODEL), 'w_o': (D_MODEL, D_MODEL), 'w_up': (2 * D_FF, D_MODEL),
              'w_down': (D_FF, D_MODEL), 'w_ple_gate': (D_MODEL, D_MODEL), 'w_ple': (PLE_DIM, D_MODEL),
              'sc_conv_w': (3, BW), 'dn_conv_w': (4, 3 * BW), 'ffn_conv_w': (3, 2 * D_FF)}
SMALL_SHAPE = {'g_mix': D_MODEL, 'b_fox_f': FOX_HEADS, 'fox_q_gain': FOX_DH, 'fox_k_gain': FOX_DH, 'dn_a_log': DN_HEADS,
               'dn_dt_bias': DN_HEADS, 'dn_norm_gain': DN_DH, 'g_ffn': D_MODEL, 'g_ple': D_MODEL}


def _shard_shape(name):
    s = list(FULL_SHAPE[name])
    s[SHARDED[name]] //= N_DEV
    return tuple(s)


def _full_from_gathered(g, name):
    sh, ax = _shard_shape(name), SHARDED[name]
    blocks = jnp.moveaxis(g, 0, ax)
    return blocks.reshape(sh[:ax] + (N_DEV * sh[ax],) + sh[ax + 1:])


def _by_dest(full, name):
    sh, ax = _shard_shape(name), SHARDED[name]
    return jnp.moveaxis(full.reshape(sh[:ax] + (N_DEV, sh[ax]) + sh[ax + 1:]), ax, 0)


def _flat_pack(arrs):
    flat = jnp.concatenate([a.reshape(-1).astype(F32) for a in arrs])
    rows = -(-flat.shape[0] // (8 * PACK_W)) * 8
    return jnp.pad(flat, (0, rows * PACK_W - flat.shape[0])).reshape(rows, PACK_W)


def _flat_unpack(pack, shapes):
    flat, out, off = pack.reshape(-1), [], 0
    for s in shapes:
        n = 1
        for d in s:
            n *= d
        out.append(flat[off:off + n].reshape(s))
        off += n
    return out


SMALL_SHAPES = [(DEPTH, SMALL_SHAPE[n]) for n in SMALL]
SMALL_LOSS_AT = sum(DEPTH * SMALL_SHAPE[n] for n in SMALL)
CONV_SHARD_SHAPES = [(DEPTH,) + _shard_shape(n) for n in CONVW]
CONV_FULL_SHAPES = [(DEPTH,) + FULL_SHAPE[n] for n in CONVW]


def _pick_tm(m, width):
    best = None
    for tm in range(16, m + 1, 16):
        if m % tm == 0 and tm * width * 4 <= (1 << 20):
            best = tm
    return best if best is not None else m


HBM_SPEC = pl.BlockSpec(memory_space=pltpu.HBM)


def _place():
    x, y, c = lax.axis_index("x"), lax.axis_index("y"), lax.axis_index("c")
    return x, y, c, [(1 - x, y), (x, 1 - y), (1 - x, 1 - y)]


def _all_gather(arrs, name):
    n = len(arrs)

    def body(*refs):
        ins, outs = refs[:n], refs[n:2 * n]
        send_sems, recv_sems = refs[2 * n:]
        x, y, c, chips = _place()
        me, sibling = (x, y, c), (x, y, 1 - c)

        def block(a, p):
            return outs[a].at[4 * p[0] + 2 * p[1] + p[2]]

        def copy(a, k, blk, to, src=None):
            return pltpu.make_async_remote_copy(src_ref=block(a, blk) if src is None else src, dst_ref=block(a, blk),
                                                send_sem=send_sems.at[a, k], recv_sem=recv_sems.at[a, k],
                                                device_id=to, device_id_type=MESH)

        first, passed = [], []
        for a in range(n):
            first.append(copy(a, 0, me, sibling, src=ins[a]))
            first += [copy(a, 1 + j, me, (*chip, c), src=ins[a]) for j, chip in enumerate(chips)]
        for cp in first:
            cp.start()
        for j, chip in enumerate(chips):
            for a in range(n):
                copy(a, 1 + j, (*chip, c), me).wait_recv()
                fwd = copy(a, 4 + j, (*chip, c), sibling)
                fwd.start()
                passed.append(fwd)
        for a in range(n):
            copy(a, 0, sibling, me).wait_recv()
            for j, chip in enumerate(chips):
                copy(a, 4 + j, (*chip, 1 - c), me).wait_recv()
        for cp in first + passed:
            cp.wait_send()

    gathered = pl.pallas_call(
        body, in_specs=[HBM_SPEC] * n, out_specs=[HBM_SPEC] * n,
        out_shape=[jax.ShapeDtypeStruct((N_DEV,) + a.shape, a.dtype) for a in arrs],
        scratch_shapes=[pltpu.SemaphoreType.DMA((n, 7)), pltpu.SemaphoreType.DMA((n, 7))], name=name,
    )(*arrs)
    me = 4 * lax.axis_index("x") + 2 * lax.axis_index("y") + lax.axis_index("c")
    return [lax.dynamic_update_index_in_dim(g, a, me, 0) for g, a in zip(gathered, arrs)]


SEM_SPEC = pl.BlockSpec(memory_space=pltpu.SEMAPHORE)
DATAFLOW = pltpu.SideEffectType.DATAFLOW_SIDE_EFFECTING
ALL_PEERS = (1, 2, 3, 4, 5, 6, 7)
NEAR_PEERS = (1, 4, 2, 6)


def _direct_copies(src_refs, zone_refs, send_sems, recv_sems, per_dest, peers):
    x, y, c, _ = _place()
    me = 4 * x + 2 * y + c
    cps = []
    for a, (src, zone) in enumerate(zip(src_refs, zone_refs)):
        for r, bits in enumerate(peers):
            px = 1 - x if bits & 4 else x
            py = 1 - y if bits & 2 else y
            pc = 1 - c if bits & 1 else c
            cps.append(pltpu.make_async_remote_copy(
                src_ref=src.at[4 * px + 2 * py + pc] if per_dest else src, dst_ref=zone.at[me],
                send_sem=send_sems.at[a * len(peers) + r], recv_sem=recv_sems.at[a * len(peers) + r],
                device_id=(px, py, pc), device_id_type=MESH))
    return cps


def _direct_start(srcs, per_dest, after, name, peers=ALL_PEERS):
    n = len(srcs)
    zones = [lax.empty((N_DEV,) + (s.shape[1:] if per_dest else s.shape), s.dtype) for s in srcs]

    def body(*refs):
        src_refs, zone_refs = refs[:n], refs[n:2 * n]
        send_sems, recv_sems = refs[2 * n + 1:2 * n + 3]
        for cp in _direct_copies(src_refs, zone_refs, send_sems, recv_sems, per_dest, peers):
            cp.start()
        refs[-1][...] = jnp.zeros_like(refs[-1])

    sems = pltpu.SemaphoreType.DMA((n * len(peers),))
    res = pl.pallas_call(
        body, name=name, in_specs=[HBM_SPEC] * (2 * n + 1),
        out_shape=[sems, sems] + [pltpu.HBM(s.shape, s.dtype) for s in srcs] + [pltpu.HBM(z.shape, z.dtype) for z in zones]
        + [jax.ShapeDtypeStruct((8, 128), F32)],
        out_specs=[SEM_SPEC, SEM_SPEC] + [HBM_SPEC] * (2 * n) + [pl.BlockSpec(memory_space=pltpu.VMEM)],
        input_output_aliases={i: 2 + i for i in range(2 * n)},
        compiler_params=pltpu.CompilerParams(has_side_effects=DATAFLOW),
    )(*[pltpu.with_memory_space_constraint(s, pltpu.HBM) for s in srcs],
      *[pltpu.with_memory_space_constraint(z, pltpu.HBM) for z in zones], after)
    return (res[0], res[1], list(res[2:2 + n]), list(res[2 + n:2 + 2 * n]), per_dest, peers), res[-1][0:1, 0:1]


def _direct_wait(started, after, name):
    send_sems, recv_sems, srcs, zones, per_dest, peers = started
    n = len(srcs)

    def body(*refs):
        src_refs, zone_refs = refs[:n], refs[n:2 * n]
        for cp in _direct_copies(src_refs, zone_refs, refs[2 * n], refs[2 * n + 1], per_dest, peers):
            cp.wait_send()
            cp.wait_recv()

    res = pl.pallas_call(
        body, name=name, in_specs=[HBM_SPEC] * (2 * n) + [SEM_SPEC, SEM_SPEC, HBM_SPEC],
        out_shape=[pltpu.HBM(s.shape, s.dtype) for s in srcs] + [pltpu.HBM(z.shape, z.dtype) for z in zones],
        out_specs=[HBM_SPEC] * (2 * n), input_output_aliases={i: i for i in range(2 * n)},
        compiler_params=pltpu.CompilerParams(has_side_effects=DATAFLOW),
    )(*srcs, *zones, send_sems, recv_sems, after)
    return list(res[n:])


def _gather_finish(zones, name):
    n = len(zones)

    def body(*refs):
        zone_in, zone_out = refs[:n], refs[n:2 * n]
        send_sems, recv_sems = refs[2 * n:]
        x, y, c, chips = _place()
        cps = []
        for a in range(n):
            for j, (cx, cy) in enumerate(chips):
                blk = 4 * cx + 2 * cy + c
                cp = pltpu.make_async_remote_copy(src_ref=zone_in[a].at[blk], dst_ref=zone_out[a].at[blk],
                                                  send_sem=send_sems.at[a, j], recv_sem=recv_sems.at[a, j],
                                                  device_id=(x, y, 1 - c), device_id_type=MESH)
                cp.start()
                cps.append(cp)
        for cp in cps:
            cp.wait()

    return pl.pallas_call(
        body, in_specs=[HBM_SPEC] * n, out_specs=[HBM_SPEC] * n,
        out_shape=[jax.ShapeDtypeStruct(z.shape, z.dtype) for z in zones], input_output_aliases={a: a for a in range(n)},
        scratch_shapes=[pltpu.SemaphoreType.DMA((n, 3)), pltpu.SemaphoreType.DMA((n, 3))], name=name,
    )(*zones)


def _adamw(w, g, m, v):
    m = ADAM_B1 * m + (1.0 - ADAM_B1) * g
    v = ADAM_B2 * v + (1.0 - ADAM_B2) * jnp.square(g)
    m_hat = m / (1.0 - ADAM_B1 ** ADAM_STEP)
    v_hat = v / (1.0 - ADAM_B2 ** ADAM_STEP)
    delta = -ADAM_LR * (m_hat / (jnp.sqrt(v_hat) + ADAM_EPS) + ADAM_WD * w)
    return delta, m, v


def _adamw_layer(w, m, v, own, parts, layer, prev, name):
    _, rows, c = w.shape
    tm = _pick_tm(rows, c)

    def body(w_ref, m_ref, v_ref, own_ref, parts_ref, *rest):
        g = own_ref[...]
        for j in range(parts_ref.shape[0]):
            g = g + parts_ref[j].astype(F32)
        delta, m2, v2 = _adamw(w_ref[...], g, m_ref[...], v_ref[...])
        g_ref, d_ref, m2_ref, v2_ref = rest[-4:]
        g_ref[...], d_ref[...], m2_ref[...], v2_ref[...] = g, delta, m2, v2

    wspec = pl.BlockSpec((None, tm, c), lambda i: (layer, i, 0))
    prev = list(prev) if prev is not None else []
    return pl.pallas_call(
        body, grid=(rows // tm,),
        in_specs=[wspec] * 3 + [pl.BlockSpec((tm, c), lambda i: (i, 0)), pl.BlockSpec((parts.shape[0], tm, c), lambda i: (0, i, 0))]
        + [HBM_SPEC] * len(prev),
        out_specs=[wspec] * 4, out_shape=[jax.ShapeDtypeStruct(w.shape, F32)] * 4,
        input_output_aliases={5 + k: k for k in range(len(prev))}, name=name,
    )(w, m, v, own, parts, *prev)


W_IN_ADAMW_TM = 128


def _adamw_w_in(w_t, m_t, v_t, grads, name):
    tm = W_IN_ADAMW_TM
    ng = [p.shape[0] for _, p in grads]

    def body(w_ref, m_ref, v_ref, *rest):
        outs = rest[2 * DEPTH:]
        for layer in range(DEPTH):
            own_ref, parts_ref = rest[2 * layer], rest[2 * layer + 1]
            g = own_ref[...]
            for j in range(ng[layer]):
                g = g + parts_ref[j].astype(F32)
            delta, m2, v2 = _adamw(w_ref[:, layer, :].T, g, m_ref[:, layer, :].T, v_ref[:, layer, :].T)
            for o_ref, val in zip(outs, (g, delta, m2, v2)):
                o_ref[:, layer, :] = val.T

    tspec = pl.BlockSpec((W_IN_SHARD, DEPTH, tm), lambda i: (0, 0, i))
    gspecs = []
    for _, parts in grads:
        gspecs += [pl.BlockSpec((tm, W_IN_SHARD), lambda i: (i, 0)), pl.BlockSpec((parts.shape[0], tm, W_IN_SHARD), lambda i: (0, i, 0))]
    return pl.pallas_call(
        body, grid=(D_MODEL // tm,), in_specs=[tspec] * 3 + gspecs, out_specs=[tspec] * 4,
        out_shape=[jax.ShapeDtypeStruct(w_t.shape, F32)] * 4, name=name,
    )(w_t, m_t, v_t, *[x for pair in grads for x in pair])


def _adamw_call(w, m, v, gparts, name):
    def fn(wv, mv, vv, *gs):
        g = gs[0].astype(F32)
        for gp in gs[1:]:
            g = g + gp.astype(F32)
        delta, m2, v2 = _adamw(wv, g, mv, vv)
        return g, delta, m2, v2
    rows, c = w.shape
    return _ew(fn, [w, m, v] + list(gparts), [], [(c, F32)] * 4, tm=_pick_tm(rows, c), name=name)


def kernel(x, p, g_mix, w_in, b_fox_f, fox_q_gain, fox_k_gain, sc_conv_w, dn_conv_w, dn_a_log, dn_dt_bias,
           dn_norm_gain, w_branch, w_o, g_ffn, w_up, ffn_conv_w, w_down, g_ple, w_ple_gate, w_ple, loss_target,
           m_g_mix, m_w_in, m_b_fox_f, m_fox_q_gain, m_fox_k_gain, m_sc_conv_w, m_dn_conv_w, m_dn_a_log,
           m_dn_dt_bias, m_dn_norm_gain, m_w_branch, m_w_o, m_g_ffn, m_w_up, m_ffn_conv_w, m_w_down, m_g_ple,
           m_w_ple_gate, m_w_ple, v_g_mix, v_w_in, v_b_fox_f, v_fox_q_gain, v_fox_k_gain, v_sc_conv_w, v_dn_conv_w,
           v_dn_a_log, v_dn_dt_bias, v_dn_norm_gain, v_w_branch, v_w_o, v_g_ffn, v_w_up, v_ffn_conv_w, v_w_down,
           v_g_ple, v_w_ple_gate, v_w_ple):
    return _step(x, p, g_mix, w_in, b_fox_f, fox_q_gain, fox_k_gain, sc_conv_w, dn_conv_w, dn_a_log, dn_dt_bias,
                 dn_norm_gain, w_branch, w_o, g_ffn, w_up, ffn_conv_w, w_down, g_ple, w_ple_gate, w_ple, loss_target,
                 m_g_mix, m_w_in, m_b_fox_f, m_fox_q_gain, m_fox_k_gain, m_sc_conv_w, m_dn_conv_w, m_dn_a_log,
                 m_dn_dt_bias, m_dn_norm_gain, m_w_branch, m_w_o, m_g_ffn, m_w_up, m_ffn_conv_w, m_w_down, m_g_ple,
                 m_w_ple_gate, m_w_ple, v_g_mix, v_w_in, v_b_fox_f, v_fox_q_gain, v_fox_k_gain, v_sc_conv_w,
                 v_dn_conv_w, v_dn_a_log, v_dn_dt_bias, v_dn_norm_gain, v_w_branch, v_w_o, v_g_ffn, v_w_up,
                 v_ffn_conv_w, v_w_down, v_g_ple, v_w_ple_gate, v_w_ple)


def _step(*args):
    names = ['x', 'p'] + WEIGHTS + ['loss_target'] + ['m_' + n for n in WEIGHTS] + ['v_' + n for n in WEIGHTS]
    assert len(args) == len(names)
    a = dict(zip(names, args))
    for n, perm in HELD_TRANSPOSED.items():
        for k in (n, 'm_' + n, 'v_' + n):
            a[k] = jnp.transpose(a[k], perm)
    x, target = a['x'][0], a['loss_target'][0]
    p = a['p'][:, 0]
    dev = 4 * lax.axis_index("x") + 2 * lax.axis_index("y") + lax.axis_index("c")

    LATE = [n for n in BIG if n != 'w_in']
    sps = [{n: a[n][layer][None, :] for n in SMALL} for layer in range(DEPTH)]
    shards = [{n: a[n][layer].astype(BF16) for n in BIG} for layer in range(DEPTH)]

    def by_device(full, n):
        by_dest = _by_dest(full, n)
        return lax.dynamic_index_in_dim(by_dest, dev, axis=0, keepdims=False), by_dest.astype(BF16)

    def blank(zone):
        return lax.dynamic_update_index_in_dim(zone, jnp.zeros(zone.shape[1:], zone.dtype), dev, 0)

    w_in_all, conv_all = _all_gather([shards[0]['w_in'], _flat_pack([a[n] for n in CONVW])], 'gather_w_in_l0')
    conv_by_dev = [_flat_unpack(conv_all[d], CONV_SHARD_SHAPES) for d in range(N_DEV)]
    conv_full = {n: jnp.concatenate([conv_by_dev[d][i] for d in range(N_DEV)], axis=2) for i, n in enumerate(CONVW)}
    late0_started, zero = _direct_start([shards[0][n] for n in LATE], False, w_in_all, 'weights_rest_l0_start', NEAR_PEERS)
    w1_started, zero = _direct_start([shards[1][n] for n in BIG], False, late0_started[2][0], 'weights_l1_start', NEAR_PEERS)
    sps[0]['g_mix'] = sps[0]['g_mix'] + zero

    def late_weights0(after):
        zones = _gather_finish(_direct_wait(late0_started, after, 'weights_rest_l0_wait'), 'weights_rest_l0_finish')
        return {n: _full_from_gathered(lax.dynamic_update_index_in_dim(z, shards[0][n], dev, 0), n) for n, z in zip(LATE, zones)}

    w0 = {n: conv_full[n][0] for n in CONVW}
    w0['w_in'] = _w_in_assemble(w_in_all, 'w_in_assemble_l0')
    h, sv0 = _layer_fwd(x, p[0], w0, sps[0], 'l0', late_weights0)
    zones = [lax.dynamic_update_index_in_dim(z, shards[1][n], dev, 0)
             for n, z in zip(BIG, _gather_finish(_direct_wait(w1_started, h, 'weights_l1_wait'), 'weights_l1_finish'))]
    w1 = {n: _full_from_gathered(z, n) for n, z in zip(BIG, zones) if n != 'w_in'}
    w1.update({n: conv_full[n][1] for n in CONVW})
    w1['w_in'] = _w_in_assemble(zones[list(BIG).index('w_in')], 'w_in_assemble_l1')
    h, sv1 = _layer_fwd(h, p[1], w1, sps[1], 'l1')
    dh, loss_part = _loss_bwd(h, target, 'loss')
    grads = [None] * DEPTH
    dh, grads[1] = _layer_bwd(dh, sv1, sps[1], 'l1')
    owns1, sends1 = {}, {}
    for n in BIG:
        owns1[n], sends1[n] = _w_in_split_by_device(grads[1][n], 'w_in_grad_split_l1') if n == 'w_in' else by_device(grads[1][n], n)
    g1_started, zero = _direct_start([sends1[n] for n in BIG], True, dh, 'grads_l1_start')
    early = {}

    def early_grads0(g, sp):
        owns, sends = zip(*[by_device(g[n], n) for n in LATE])
        early['started'], zero = _direct_start(list(sends), True, g['w_branch'], 'grads_rest_l0_start')
        early['owns'] = dict(zip(LATE, owns))
        return {**sp, 'dn_norm_gain': sp['dn_norm_gain'] + zero}

    def last_grad0(g, sp):
        early['own_in'], send = _w_in_split_by_device(g['w_in'], 'w_in_grad_split_l0')
        early['in_started'], zero = _direct_start([send], True, g['w_in'], 'grads_w_in_l0_start')
        return {**sp, 'g_mix': sp['g_mix'] + zero, 'zero_row': jnp.zeros((1, D_MODEL), F32) + zero}

    dh, grads[0] = _layer_bwd(dh, sv0, {**sps[0], 'g_ple': sps[0]['g_ple'] + zero}, 'l0', early_grads0, last_grad0)
    grad_x = dh[None]
    own_in0, in0_started = early['own_in'], early['in_started']
    sent = in0_started[2][0]
    zones1 = dict(zip(BIG, _direct_wait(g1_started, sent, 'grads_l1_wait')))
    zones0 = dict(zip(LATE, _direct_wait(early['started'], sent, 'grads_rest_l0_wait')))

    def adamw(n, layer, own, zone, prev):
        rows = (-1, a[n].shape[-1])
        own = own.reshape(rows)
        view = lambda t: t.reshape((DEPTH,) + own.shape)
        return _adamw_layer(view(a[n]), view(a['m_' + n]), view(a['v_' + n]), own, blank(zone).reshape((N_DEV,) + own.shape),
                            layer, prev, f'adamw_{n}_l{layer}')

    out, done = {}, []
    for n in LATE:
        res = adamw(n, 0, early['owns'][n], zones0[n], adamw(n, 1, owns1[n], zones1[n], None))
        out[n] = [r.reshape(a[n].shape) for r in res]
        done.append(res[0][0, 0:1, 0:1])
    (zone_in0,) = _direct_wait(in0_started, jnp.concatenate(done + [dh[0:1, 0:1]], axis=1), 'grads_w_in_l0_wait')
    res = _adamw_w_in(*[jnp.transpose(a[k], (2, 0, 1)) for k in ('w_in', 'm_w_in', 'v_w_in')],
                      [(own_in0, blank(zone_in0)), (owns1['w_in'], blank(zones1['w_in']))], 'adamw_w_in')
    out['w_in'] = [jnp.transpose(r, (1, 2, 0)) for r in res]
    last_update = res[0]

    small_pack = _flat_pack([jnp.stack([grads[layer][n].reshape(-1) for layer in range(DEPTH)]) for n in SMALL] + [loss_part[0, 0]])
    conv_pack = _flat_pack([jnp.stack([grads[layer][n] for layer in range(DEPTH)]) for n in CONVW])
    small_started, _ = _direct_start([small_pack, conv_pack], False, small_pack, 'small_grads_start')
    small_all, conv_all = [lax.dynamic_update_index_in_dim(z, s, dev, 0) for z, s in
                           zip(_direct_wait(small_started, last_update, 'small_grads_wait'), (small_pack, conv_pack))]
    res = _adamw_call(_flat_pack([a[n] for n in SMALL]), _flat_pack([a['m_' + n] for n in SMALL]),
                      _flat_pack([a['v_' + n] for n in SMALL]), [small_all[d] for d in range(N_DEV)], 'adamw_replicated')
    loss = res[0].reshape(-1)[SMALL_LOSS_AT]
    for k, r in enumerate(res):
        for n, val in zip(SMALL, _flat_unpack(r, SMALL_SHAPES)):
            out.setdefault(n, [None] * 4)[k] = val
    (conv_sum,) = _ew(lambda *gs: (functools.reduce(lambda s, t: s + t, gs),), [conv_all[d] for d in range(N_DEV)], [],
                      [(PACK_W, F32)], tm=conv_pack.shape[0], name='conv_grad_sum')
    conv_own = [lax.dynamic_slice_in_dim(g, dev * (g.shape[2] // N_DEV), g.shape[2] // N_DEV, axis=2)
                for g in _flat_unpack(conv_sum, CONV_FULL_SHAPES)]
    res = _adamw_call(_flat_pack([a[n] for n in CONVW]), _flat_pack([a['m_' + n] for n in CONVW]),
                      _flat_pack([a['v_' + n] for n in CONVW]), [_flat_pack(conv_own)], 'adamw_conv')
    for k, r in enumerate(res):
        for n, val in zip(CONVW, _flat_unpack(r, CONV_SHARD_SHAPES)):
            out.setdefault(n, [None] * 4)[k] = val

    for n, perm in HELD_TRANSPOSED.items():
        out[n] = [jnp.transpose(r, perm) for r in out[n]]
    outs = [loss, grad_x]
    for k in range(4):
        outs += [out[n][k] for n in WEIGHTS]
    return tuple(outs)
```

```python
import functools

import jax
import jax.numpy as jnp
from jax import lax
from jax.experimental import pallas as pl
from jax.experimental.pallas import tpu as pltpu

F32 = jnp.float32
BF16 = jnp.bfloat16
HI = lax.Precision.HIGHEST

D_MODEL = 1024
DEPTH = 2
N_DEV = 8
PLE_DIM = 256
BW = 512
FOX_HEADS, FOX_DH = 8, 64
DN_HEADS, DN_DH = 4, 128
DN_CHUNK = 64
D_FF = 2816
EPS = 1e-6
NEG = -1e30

ADAM_LR, ADAM_B1, ADAM_B2, ADAM_EPS, ADAM_WD, ADAM_STEP = 0.001, 0.9, 0.999, 1e-08, 0.01, 10

C_FQ, C_FK, C_FV = 0, 512, 1024
C_SB, C_SC, C_SV = 1536, 2048, 2560
C_DQ, C_DK, C_DV, C_DZ = 3072, 3584, 4096, 4608
C_GATE = 5120
C_SMALL = 8192
IN_P = 8320
IN_ORIG = 8208

WEIGHTS = ['g_mix', 'w_in', 'b_fox_f', 'fox_q_gain', 'fox_k_gain', 'sc_conv_w', 'dn_conv_w', 'dn_a_log',
           'dn_dt_bias', 'dn_norm_gain', 'w_branch', 'w_o', 'g_ffn', 'w_up', 'ffn_conv_w', 'w_down', 'g_ple',
           'w_ple_gate', 'w_ple']
BIG = {'w_in': 1, 'w_branch': 2, 'w_o': 0, 'w_up': 0, 'w_down': 0, 'w_ple_gate': 0, 'w_ple': 1}
HELD_TRANSPOSED = {'w_up': (0, 2, 1)}
CONVW = {'sc_conv_w': 1, 'dn_conv_w': 1, 'ffn_conv_w': 1}
SHARDED = {**BIG, **CONVW}
SMALL = [n for n in WEIGHTS if n not in SHARDED]
MESH = pl.DeviceIdType.MESH


def _sigmoid(x):
    return 0.5 * (jnp.tanh(0.5 * x) + 1.0)


def _silu(x):
    return x * _sigmoid(x)


def _log1pexp_negabs(z):
    return jnp.log(1.0 + jnp.exp(-jnp.abs(z)))


def _log_sigmoid(z):
    return jnp.minimum(z, 0.0) - _log1pexp_negabs(z)


def _softplus(z):
    return jnp.maximum(z, 0.0) + _log1pexp_negabs(z)


def _rms(x, g):
    return x * lax.rsqrt(jnp.mean(x * x, axis=-1, keepdims=True) + EPS) * g


def _l2(x):
    return x * lax.rsqrt(jnp.sum(x * x, axis=-1, keepdims=True) + EPS)


def _dot(a, b, dims, precision=None):
    return lax.dot_general(a, b, (dims, ((), ())), preferred_element_type=F32, precision=precision)


NN = ((1,), (0,))
NT = ((1,), (1,))
TN = ((0,), (0,))


def _shift_down(x, s):
    if s == 0:
        return x
    t = lax.broadcasted_iota(jnp.int32, x.shape, 0)
    return jnp.where(t >= s, pltpu.roll(x, s, 0), 0.0)


def _shift_up(x, s):
    if s == 0:
        return x
    n = x.shape[0]
    t = lax.broadcasted_iota(jnp.int32, x.shape, 0)
    return jnp.where(t < n - s, pltpu.roll(x, n - s, 0), 0.0)


def _conv(x, w):
    k = w.shape[0]
    y = w[k - 1:k] * x
    for j in range(k - 1):
        y = y + w[j:j + 1] * _shift_down(x, k - 1 - j)
    return y


def _conv_bwd(x, w, dy):
    k = w.shape[0]
    dx = w[k - 1:k] * dy
    dws = []
    for j in range(k - 1):
        dx = dx + w[j:j + 1] * _shift_up(dy, k - 1 - j)
        dws.append(jnp.sum(dy * _shift_down(x, k - 1 - j), axis=0, keepdims=True))
    dws.append(jnp.sum(dy * x, axis=0, keepdims=True))
    return dx, dws


MM_VMEM_BUDGET = 36 << 20
MM_STEP_BYTES = 1 << 20


def _mm_tiles(m, n, k, a_size, b_size, tile_size, cast_a):
    best = None
    for tm in [d for d in (2048, 1024, 512, 256, 128) if d <= m and m % d == 0] or [m]:
        for tn in [d for d in range(128, min(n, 2048) + 1, 128) if n % d == 0] or [n]:
            vmem = 2 * tm * k * a_size + (2 * tm * k if cast_a else 0) + 2 * k * tn * b_size + 2 * tm * tn * tile_size
            if vmem > MM_VMEM_BUDGET:
                continue
            steps = (m // tm) * (n // tn)
            cost = m * k * a_size + (m // tm) * k * n * b_size + m * n * tile_size + steps * MM_STEP_BYTES
            if best is None or cost < best[0]:
                best = (cost, tm, tn)
    assert best is not None, (m, n, k)
    return best[1], best[2]


def _mm(a, b, mode, outs, *, epi=None, epi_args=(), name):
    if mode == 'nn':
        (m, k), (k2, n) = a.shape, b.shape
    elif mode == 'nt':
        (m, k), (n, k2) = a.shape, b.shape
    else:
        (k, m), (k2, n) = a.shape, b.shape
    assert k == k2, (a.shape, b.shape, mode)
    tile_size = sum(jnp.dtype(dt).itemsize for dt in outs) + sum(e.dtype.itemsize for e in epi_args if e.shape[0] != 1)
    tm, tn = _mm_tiles(m, n, k, a.dtype.itemsize, b.dtype.itemsize, tile_size, a.dtype != BF16)
    dims = {'nn': NN, 'nt': NT, 'tn': TN}[mode]
    a_spec = pl.BlockSpec((k, tm), lambda i, j: (0, i)) if mode == 'tn' else pl.BlockSpec((tm, k), lambda i, j: (i, 0))
    b_spec = pl.BlockSpec((tn, k), lambda i, j: (j, 0)) if mode == 'nt' else pl.BlockSpec((k, tn), lambda i, j: (0, j))
    e_specs = [pl.BlockSpec((1, tn), lambda i, j: (0, j)) if e.shape[0] == 1 else pl.BlockSpec((tm, tn), lambda i, j: (i, j))
               for e in epi_args]
    ne, no = len(epi_args), len(outs)
    cast_a = a.dtype != BF16

    def body(a_ref, b_ref, *rest):
        if cast_a:
            a_sc = rest[-1]

            @pl.when(pl.program_id(1) == 0)
            def _():
                a_sc[...] = a_ref[...].astype(BF16)
            av = a_sc[...]
        else:
            av = a_ref[...]
        acc = _dot(av, b_ref[...].astype(BF16), dims)
        vals = epi(acc, *[e[...] for e in rest[:ne]]) if epi is not None else (acc,)
        for o_ref, v in zip(rest[ne:ne + no], vals):
            o_ref[...] = v.astype(o_ref.dtype)

    res = pl.pallas_call(
        body, grid=(m // tm, n // tn),
        in_specs=[a_spec, b_spec] + e_specs,
        out_specs=[pl.BlockSpec((tm, tn), lambda i, j: (i, j)) for _ in outs],
        out_shape=[jax.ShapeDtypeStruct((m, n), dt) for dt in outs],
        scratch_shapes=[pltpu.VMEM(a_spec.block_shape, BF16)] if cast_a else [],
        name=name,
    )(a, b, *epi_args)
    return res[0] if no == 1 else res


def _ew(fn, tiled, bcast, outs, reds=(), *, tm=512, name):
    secs = [(t, 0, t.shape[1]) if not isinstance(t, tuple) else t for t in tiled]
    m = secs[0][0].shape[0]
    tm = min(tm, m)
    assert m % tm == 0
    in_specs = []
    for arr, off, w in secs:
        assert off % w == 0
        in_specs.append(pl.BlockSpec((tm, w), functools.partial(lambda i, c: (i, c), c=off // w)))
    in_specs += [pl.BlockSpec(b.shape, lambda i: (0, 0)) for b in bcast]
    nin, no = len(in_specs), len(outs)

    def body(*refs):
        vals = fn(*[r[...] for r in refs[:nin]])
        for r, v in zip(refs[nin:nin + no], vals[:no]):
            r[...] = v.astype(r.dtype)
        i = pl.program_id(0)
        for r, v in zip(refs[nin + no:], vals[no:]):
            @pl.when(i == 0)
            def _():
                r[...] = v

            @pl.when(i > 0)
            def _():
                r[...] += v

    res = pl.pallas_call(
        body, grid=(m // tm,), in_specs=in_specs,
        out_specs=[pl.BlockSpec((tm, c), lambda i: (i, 0)) for c, _ in outs] + [pl.BlockSpec(s, lambda i: (0, 0)) for s in reds],
        out_shape=[jax.ShapeDtypeStruct((m, c), dt) for c, dt in outs] + [jax.ShapeDtypeStruct(s, F32) for s in reds],
        name=name,
    )(*[s[0] for s in secs], *bcast)
    return res


def _cb(fn, cols, params, outs, pouts, *, tc, nblk, name):
    t = cols[0][0].shape[0]
    in_specs = []
    for arr, off in cols:
        assert off % tc == 0
        in_specs.append(pl.BlockSpec((t, tc), functools.partial(lambda c, o: (0, o + c), o=off // tc)))
    for arr, off in params:
        in_specs.append(pl.BlockSpec((arr.shape[0], tc), functools.partial(lambda c, o: (0, o + c), o=off // tc)))
    nin, no = len(in_specs), len(outs)

    def body(*refs):
        vals = fn(*[r[...] for r in refs[:nin]])
        for r, v in zip(refs[nin:], vals):
            r[...] = v.astype(r.dtype)

    return pl.pallas_call(
        body, grid=(nblk,), in_specs=in_specs,
        out_specs=[pl.BlockSpec((t, tc), lambda c: (0, c)) for _ in outs] + [pl.BlockSpec((k, tc), lambda c: (0, c)) for k in pouts],
        out_shape=[jax.ShapeDtypeStruct((t, nblk * tc), dt) for dt in outs] + [jax.ShapeDtypeStruct((k, nblk * tc), F32) for k in pouts],
        name=name,
    )(*[c[0] for c in cols], *[p[0] for p in params])


CUM_BLK = 256


def _prep_point(s, bias, alog):
    lane = lax.broadcasted_iota(jnp.int32, s.shape, 1)
    z = s + bias
    return jnp.where(lane < 8, _log_sigmoid(z),
                     jnp.where(lane < 12, _sigmoid(s),
                               jnp.where(lane < 16, -jnp.exp(alog) * _softplus(z), 0.0)))


def _tri(n, upper):
    r = lax.broadcasted_iota(jnp.int32, (n, n), 0)
    c = lax.broadcasted_iota(jnp.int32, (n, n), 1)
    return (r <= c if upper else r >= c).astype(F32)


def _prep_fwd(proj, bias_row, alog_row, name):
    t = proj.shape[0]
    nb = t // CUM_BLK

    def body(s_ref, b_ref, a_ref, o_ref):
        pre = _prep_point(s_ref[...], b_ref[...], a_ref[...])
        lane = lax.broadcasted_iota(jnp.int32, (CUM_BLK, 128), 1)
        tri = _tri(CUM_BLK, False)
        carry = jnp.zeros((1, 128), F32)
        for blk in range(nb):
            xb = pre[blk * CUM_BLK:(blk + 1) * CUM_BLK]
            cb = _dot(tri, xb, NN, HI) + carry
            carry = cb[CUM_BLK - 1:CUM_BLK]
            o_ref[blk * CUM_BLK:(blk + 1) * CUM_BLK, :] = jnp.where(lane < 8, cb, xb)

    return pl.pallas_call(
        body, grid=(1,),
        in_specs=[pl.BlockSpec((t, 128), lambda i: (0, 0)), pl.BlockSpec((1, 128), lambda i: (0, 0)),
                  pl.BlockSpec((1, 128), lambda i: (0, 0))],
        out_specs=pl.BlockSpec((t, 128), lambda i: (0, 0)),
        out_shape=jax.ShapeDtypeStruct((t, 128), F32), name=name,
    )(proj, bias_row, alog_row)


def _prep_bwd(proj, bias_row, alog_row, daux, name):
    t = proj.shape[0]
    nb = t // CUM_BLK

    def body(s_ref, b_ref, a_ref, d_ref, ds_ref, db_ref, da_ref, dpre_sc):
        lane = lax.broadcasted_iota(jnp.int32, (CUM_BLK, 128), 1)
        tri = _tri(CUM_BLK, True)
        carry = jnp.zeros((1, 128), F32)
        for blk in reversed(range(nb)):
            db = d_ref[blk * CUM_BLK:(blk + 1) * CUM_BLK, :]
            cb = _dot(tri, db, NN, HI) + carry
            carry = cb[0:1]
            dpre_sc[blk * CUM_BLK:(blk + 1) * CUM_BLK, :] = jnp.where(lane < 8, cb, db)
        _, vjp = jax.vjp(_prep_point, s_ref[...], b_ref[...], a_ref[...])
        ds, dbias, dalog = vjp(dpre_sc[...])
        ds_ref[...] = ds.astype(ds_ref.dtype)
        db_ref[...] = dbias
        da_ref[...] = dalog

    return pl.pallas_call(
        body, grid=(1,),
        in_specs=[pl.BlockSpec((t, 128), lambda i: (0, 0)), pl.BlockSpec((1, 128), lambda i: (0, 0)),
                  pl.BlockSpec((1, 128), lambda i: (0, 0)), pl.BlockSpec((t, 128), lambda i: (0, 0))],
        out_specs=[pl.BlockSpec((t, 128), lambda i: (0, 0)), pl.BlockSpec((1, 128), lambda i: (0, 0)),
                   pl.BlockSpec((1, 128), lambda i: (0, 0))],
        out_shape=[jax.ShapeDtypeStruct((t, 128), BF16), jax.ShapeDtypeStruct((1, 128), F32), jax.ShapeDtypeStruct((1, 128), F32)],
        scratch_shapes=[pltpu.VMEM((t, 128), F32)], name=name,
    )(proj, bias_row, alog_row, daux)


ATT_TQ = 256
FOX_SCALE = FOX_DH ** -0.5


def _qnorm(q, g):
    return _rms(q, g) * FOX_SCALE


def _att_scores(qn_blk, kn, cfc_blk, cfr, qi, tq, kend):
    s = _dot(qn_blk.astype(BF16), kn[:kend].astype(BF16), NT) + cfc_blk - cfr[:, :kend]
    row = lax.broadcasted_iota(jnp.int32, (tq, kend), 0) + qi * tq
    col = lax.broadcasted_iota(jnp.int32, (tq, kend), 1)
    return s, row >= col


ATT_PAIR = 128 // FOX_DH


def _att_specs(t):
    pair = lambda off: pl.BlockSpec((t, 128), functools.partial(lambda i, o: (0, o + i), o=off // 128))
    gain = pl.BlockSpec((1, FOX_DH), lambda i: (0, 0))
    hd = lambda i: (i, 0, 0)
    col, row = pl.BlockSpec((ATT_PAIR, t, 1), hd), pl.BlockSpec((ATT_PAIR, 1, t), hd)
    return [pair(C_FQ), pair(C_FK), pair(C_FV), gain, gain, col, row], pl.BlockSpec((t, 128), lambda i: (0, i)), col, row


def _att_fwd(proj, qg, kg, cfc, cfr, name):
    t = proj.shape[0]
    tq = min(ATT_TQ, t)
    in_specs, pair_out, col, _ = _att_specs(t)

    def body(q_ref, k_ref, v_ref, qg_ref, kg_ref, cfc_ref, cfr_ref, o_ref, lse_ref):
        for e in range(ATT_PAIR):
            lanes = slice(e * FOX_DH, (e + 1) * FOX_DH)
            qn = _qnorm(q_ref[:, lanes].astype(F32), qg_ref[...])
            kn = _rms(k_ref[:, lanes].astype(F32), kg_ref[...])
            v = v_ref[:, lanes].astype(BF16)
            cfr = cfr_ref[e]
            for qi in range(t // tq):
                kend = (qi + 1) * tq
                rows = slice(qi * tq, kend)
                s, mask = _att_scores(qn[rows], kn, cfc_ref[e, rows, :], cfr, qi, tq, kend)
                s = jnp.where(mask, s, NEG)
                m = jnp.max(s, axis=1, keepdims=True)
                p = jnp.exp(s - m)
                l = jnp.sum(p, axis=1, keepdims=True)
                o_ref[rows, lanes] = _dot(p.astype(BF16), v[:kend], NN) / l
                lse_ref[e, rows, :] = m + jnp.log(l)

    return pl.pallas_call(
        body, grid=(FOX_HEADS // ATT_PAIR,), in_specs=in_specs, out_specs=[pair_out, col],
        out_shape=[jax.ShapeDtypeStruct((t, BW), F32), jax.ShapeDtypeStruct((FOX_HEADS, t, 1), F32)], name=name,
    )(proj, proj, proj, qg, kg, cfc, cfr)


def _att_bwd(proj, qg, kg, cfc, cfr, lse, o, do, name):
    t = proj.shape[0]
    tq = min(ATT_TQ, t)
    in_specs, pair_out, col, row = _att_specs(t)

    def body(q_ref, k_ref, v_ref, qg_ref, kg_ref, cfc_ref, cfr_ref, lse_ref, o_ref, do_ref,
             dq_ref, dk_ref, dv_ref, dcfc_ref, dcfr_ref, dqg_ref, dkg_ref, dqn_sc, dkn_sc, dv_sc, dcfr_sc):
        for e in range(ATT_PAIR):
            lanes = slice(e * FOX_DH, (e + 1) * FOX_DH)
            qn, vjp_q = jax.vjp(_qnorm, q_ref[:, lanes].astype(F32), qg_ref[...])
            kn, vjp_k = jax.vjp(_rms, k_ref[:, lanes].astype(F32), kg_ref[...])
            v = v_ref[:, lanes].astype(BF16)
            cfr = cfr_ref[e]
            do_e = do_ref[:, lanes]
            delta = jnp.sum(do_e * o_ref[:, lanes], axis=1, keepdims=True)
            dkn_sc[...] = jnp.zeros_like(dkn_sc)
            dv_sc[...] = jnp.zeros_like(dv_sc)
            dcfr_sc[...] = jnp.zeros_like(dcfr_sc)
            for qi in range(t // tq):
                kend = (qi + 1) * tq
                rows = slice(qi * tq, kend)
                s, mask = _att_scores(qn[rows], kn, cfc_ref[e, rows, :], cfr, qi, tq, kend)
                p = jnp.where(mask, jnp.exp(jnp.where(mask, s, NEG) - lse_ref[e, rows, :]), 0.0)
                do_b = do_e[rows].astype(BF16)
                dv_sc[0:kend, :] += _dot(p.astype(BF16), do_b, TN)
                dp = _dot(do_b, v[:kend], NT)
                ds = p * (dp - delta[rows])
                ds_b = ds.astype(BF16)
                dqn_sc[rows, :] = _dot(ds_b, kn[:kend].astype(BF16), NN)
                dkn_sc[0:kend, :] += _dot(ds_b, qn[rows].astype(BF16), TN)
                dcfc_ref[e, rows, :] = jnp.sum(ds, axis=1, keepdims=True)
                dcfr_sc[:, 0:kend] -= jnp.sum(ds, axis=0, keepdims=True)
            dq, dqg = vjp_q(dqn_sc[...])
            dk, dkg = vjp_k(dkn_sc[...])
            dq_ref[:, lanes] = dq.astype(dq_ref.dtype)
            dk_ref[:, lanes] = dk.astype(dk_ref.dtype)
            dv_ref[:, lanes] = dv_sc[...].astype(dv_ref.dtype)
            dcfr_ref[e] = dcfr_sc[...]
            dqg_ref[e] = dqg
            dkg_ref[e] = dkg

    gsp = pl.BlockSpec((ATT_PAIR, 1, FOX_DH), lambda i: (i, 0, 0))
    return pl.pallas_call(
        body, grid=(FOX_HEADS // ATT_PAIR,),
        in_specs=in_specs + [col, pair_out, pair_out],
        out_specs=[pair_out] * 3 + [col, row, gsp, gsp],
        out_shape=[jax.ShapeDtypeStruct((t, BW), BF16)] * 3
        + [jax.ShapeDtypeStruct((FOX_HEADS, t, 1), F32), jax.ShapeDtypeStruct((FOX_HEADS, 1, t), F32)]
        + [jax.ShapeDtypeStruct((FOX_HEADS, 1, FOX_DH), F32)] * 2,
        scratch_shapes=[pltpu.VMEM((t, FOX_DH), F32)] * 3 + [pltpu.VMEM((1, t), F32)], name=name,
    )(proj, proj, proj, qg, kg, cfc, cfr, lse, o, do)


DN_SCALE = DN_DH ** -0.5


DN_BATCH = 8


@functools.partial(jax.custom_vjp, nondiff_argnums=(2, 3))
def _mm3(a, b, dims, batch=False):
    return _mm3_passes(a, b, dims, batch)


def _mm3_fwd(a, b, dims, batch):
    return _mm3_passes(a, b, dims, batch), (a, b)


def _mm3_bwd(dims, batch, res, dc):
    a, b = res
    if dims == NN:
        return _mm3_passes(dc, b, NT, batch), _mm3_passes(a, dc, TN, batch)
    if dims == NT:
        return _mm3_passes(dc, b, NN, batch), _mm3_passes(dc, a, TN, batch)
    return _mm3_passes(b, dc, NT, batch), _mm3_passes(a, dc, NN, batch)


_mm3.defvjp(_mm3_fwd, _mm3_bwd)


def _mm3_passes(a, b, dims, batch):
    if batch:
        dn = (((dims[0][0] + 1,), (dims[1][0] + 1,)), ((0,), (0,)))
        dot = lambda p, q: lax.dot_general(p, q, dn, preferred_element_type=F32)
    else:
        dot = lambda p, q: _dot(p, q, dims)
    ah, bh = a.astype(BF16), b.astype(BF16)
    al, bl = (a - ah.astype(F32)).astype(BF16), (b - bh.astype(F32)).astype(BF16)
    return dot(ah, bh) + (dot(ah, bl) + dot(al, bh))


def _dn_local(qc, kc, vc, g, beta):
    nb, c, _ = qc.shape
    ii = lax.broadcasted_iota(jnp.int32, (c, c), 0)
    jj = lax.broadcasted_iota(jnp.int32, (c, c), 1)
    incl, strict = ii >= jj, ii > jj
    lower = jnp.broadcast_to(incl.astype(F32), (nb, c, c))
    eye = (ii == jj).astype(F32)
    mm = functools.partial(_mm3, batch=True)
    dm = mm(lower, jnp.where(strict, g, 0.0), NN)
    decay = jnp.where(incl, jnp.exp(jnp.where(incl, dm, 0.0)), 0.0)
    gcum = mm(lower, g * jnp.ones((1, 1, DN_DH), F32), NN)
    eg = jnp.exp(gcum)
    glast = gcum[:, c - 1:c]
    kb = kc * beta
    n1 = jnp.where(strict, mm(kb, kc, NT) * decay, 0.0)
    inv = eye - n1
    pw = n1
    for _ in range(5):
        pw = mm(pw, pw, NN)
        inv = inv + mm(pw, inv, NN)
    sol = mm(inv, jnp.concatenate([vc * beta, kb * eg], axis=2), NN)
    qk = jnp.where(incl, mm(qc, kc, NT) * decay, 0.0)
    return sol[:, :, :DN_DH], sol[:, :, DN_DH:], qk, qc * eg, kc * jnp.exp(glast - gcum), jnp.exp(glast)


@functools.partial(jax.custom_vjp, nondiff_argnums=(2,))
def _mm1(a, b, dims):
    return _dot(a.astype(BF16), b.astype(BF16), dims)


def _mm1_fwd(a, b, dims):
    return _mm1(a, b, dims), (a, b)


def _mm1_bwd(dims, res, dc):
    a, b = res
    if dims == NN:
        return _mm1(dc, b, NT), _mm1(a, dc, TN)
    if dims == NT:
        return _mm1(dc, b, NN), _mm1(dc, a, TN)
    return _mm1(b, dc, NT), _mm1(a, dc, NN)


_mm1.defvjp(_mm1_fwd, _mm1_bwd)


def _dn_state(u, kcum, qk, qdec, kdec, egl, state):
    v_new = u - _mm1(kcum, state, NN)
    out = _mm1(qdec, state, NN) + _mm1(qk, v_new, NN)
    return out, state * egl + _mm1(kdec, v_new, TN)


def _dn_pre_q(c):
    return _l2(_silu(c)) * DN_SCALE


def _dn_pre_k(c):
    return _l2(_silu(c))


def _dn_post(o, z, ng):
    return _rms(o, ng) * _silu(z)


def _dn_specs(t):
    cblk = lambda o: pl.BlockSpec((t, DN_DH), functools.partial(lambda h, o: (0, o + h), o=o // DN_DH))
    wblk = lambda o: pl.BlockSpec((4, DN_DH), functools.partial(lambda h, o: (0, o + h), o=o // DN_DH))
    proj_specs = [cblk(C_DQ), cblk(C_DK), cblk(C_DV), cblk(C_DZ)]
    w_specs = [wblk(0), wblk(BW), wblk(2 * BW)]
    gb_spec = pl.BlockSpec((1, t, 2), lambda h: (h, 0, 0))
    return proj_specs, w_specs, gb_spec


def _chunk_rows(n, count=1):
    return pl.ds(pl.multiple_of(n * DN_CHUNK, DN_CHUNK), count * DN_CHUNK)


def _egl_rows(n, count=1):
    return pl.ds(pl.multiple_of(n * 8, 8), count * 8)


def _dn_local_inputs(n, qn_sc, kn_sc, vv_sc, gb_ref):
    r = _chunk_rows(n, DN_BATCH)
    split = lambda v: v.reshape(DN_BATCH, DN_CHUNK, v.shape[-1])
    gbv = split(gb_ref[0, r, :])
    return split(qn_sc[r, :]), split(kn_sc[r, :]), split(vv_sc[r, :]), gbv[:, :, 0:1], gbv[:, :, 1:2]


def _dn_local_phase(nc, qn_sc, kn_sc, vv_sc, gb_ref, loc):
    def step(i, carry):
        n = i * DN_BATCH
        vals = _dn_local(*_dn_local_inputs(n, qn_sc, kn_sc, vv_sc, gb_ref))
        for sc, val in zip(loc[:5], vals[:5]):
            sc[_chunk_rows(n, DN_BATCH), :] = val.reshape(DN_BATCH * DN_CHUNK, val.shape[-1])
        loc[5][_egl_rows(n, DN_BATCH), :] = jnp.broadcast_to(vals[5], (DN_BATCH, 8, DN_DH)).reshape(DN_BATCH * 8, DN_DH)
        return carry

    lax.fori_loop(0, nc // DN_BATCH, step, 0)


def _dn_loc_scratch(t, nc):
    big = pltpu.VMEM((t, DN_DH), F32)
    return [big, big, pltpu.VMEM((t, DN_CHUNK), F32), big, big, pltpu.VMEM((nc * 8, DN_DH), F32)]


def _dn_loc_io(t, nc, **spec_args):
    head_cols = pl.BlockSpec((t, DN_DH), lambda h: (0, h), **spec_args)
    per_head = lambda rows, cols: pl.BlockSpec((1, rows, cols), lambda h: (h, 0, 0), **spec_args)
    wide = jax.ShapeDtypeStruct((t, BW), F32)
    shapes = [wide, wide, jax.ShapeDtypeStruct((DN_HEADS, t, DN_CHUNK), F32), wide, wide,
              jax.ShapeDtypeStruct((DN_HEADS, nc * 8, DN_DH), F32)]
    return shapes, [head_cols, head_cols, per_head(t, DN_CHUNK), head_cols, head_cols, per_head(nc * 8, DN_DH)]


def _dn_loc_refs(refs):
    return refs[0], refs[1], refs[2].at[0], refs[3], refs[4], refs[5].at[0]


def _dn_fwd(proj, conv_w, gb, ng, name):
    t = proj.shape[0]
    nc = t // DN_CHUNK
    assert nc % DN_BATCH == 0
    proj_specs, w_specs, gb_spec = _dn_specs(t)
    loc_shapes, loc_specs = _dn_loc_io(t, nc)

    def body(q_ref, k_ref, v_ref, z_ref, wq_ref, wk_ref, wv_ref, gb_ref, ng_ref, y_ref, o_ref, st_ref, *rest):
        loc = _dn_loc_refs(rest[:6])
        qn_sc, kn_sc, vv_sc = rest[6:]
        qn_sc[...] = _dn_pre_q(_conv(q_ref[...].astype(F32), wq_ref[...]))
        kn_sc[...] = _dn_pre_k(_conv(k_ref[...].astype(F32), wk_ref[...]))
        vv_sc[...] = _silu(_conv(v_ref[...].astype(F32), wv_ref[...]))
        _dn_local_phase(nc, qn_sc, kn_sc, vv_sc, gb_ref, loc)
        u_sc, kcum_sc, qk_sc, qdec_sc, kdec_sc, egl_sc = loc

        def chunk(n, state):
            r = _chunk_rows(n)
            egl = egl_sc[_egl_rows(n), :][0:1]
            out, new_state = _dn_state(u_sc[r, :], kcum_sc[r, :], qk_sc[r, :], qdec_sc[r, :], kdec_sc[r, :], egl, state)
            st_ref[0, n] = state
            o_ref[r, :] = out
            return new_state

        lax.fori_loop(0, nc, chunk, jnp.zeros((DN_DH, DN_DH), F32))
        y_ref[...] = _dn_post(o_ref[...], z_ref[...].astype(F32), ng_ref[...])

    hblk = pl.BlockSpec((t, DN_DH), lambda h: (0, h))
    return pl.pallas_call(
        body, grid=(DN_HEADS,),
        in_specs=proj_specs + w_specs + [gb_spec, pl.BlockSpec((1, DN_DH), lambda h: (0, 0))],
        out_specs=[hblk, hblk, pl.BlockSpec((1, nc, DN_DH, DN_DH), lambda h: (h, 0, 0, 0))] + loc_specs,
        out_shape=[jax.ShapeDtypeStruct((t, BW), F32), jax.ShapeDtypeStruct((t, BW), F32),
                   jax.ShapeDtypeStruct((DN_HEADS, nc, DN_DH, DN_DH), F32)] + loc_shapes,
        scratch_shapes=[pltpu.VMEM((t, DN_DH), F32)] * 3, name=name,
    )(proj, proj, proj, proj, conv_w, conv_w, conv_w, gb, ng)


def _dn_bwd(proj, conv_w, gb, ng, o, states, loc_saved, dy, name):
    t = proj.shape[0]
    nc = t // DN_CHUNK
    proj_specs, w_specs, gb_spec = _dn_specs(t)
    _, loc_specs = _dn_loc_io(t, nc, pipeline_mode=pl.Buffered(1))

    def body(q_ref, k_ref, v_ref, z_ref, wq_ref, wk_ref, wv_ref, gb_ref, ng_ref, o_ref, st_ref, dy_ref,
             u_ref, kcum_ref, qk_ref, qdec_ref, kdec_ref, egl_ref,
             dq_ref, dk_ref, dv_ref, dz_ref, dwq_ref, dwk_ref, dwv_ref, dgb_ref, dng_ref,
             qn_sc, kn_sc, vv_sc, do_sc, *dloc):
        loc = _dn_loc_refs((u_ref, kcum_ref, qk_ref, qdec_ref, kdec_ref, egl_ref))
        qn_sc[...] = _dn_pre_q(_conv(q_ref[...].astype(F32), wq_ref[...]))
        kn_sc[...] = _dn_pre_k(_conv(k_ref[...].astype(F32), wk_ref[...]))
        vv_sc[...] = _silu(_conv(v_ref[...].astype(F32), wv_ref[...]))
        _, vjp_y = jax.vjp(_dn_post, o_ref[...], z_ref[...].astype(F32), ng_ref[...])
        do, dz, dng = vjp_y(dy_ref[...])
        do_sc[...] = do
        dz_ref[...] = dz.astype(dz_ref.dtype)
        dng_ref[0] = dng
        u_sc, kcum_sc, qk_sc, qdec_sc, kdec_sc, egl_sc = loc

        def state_bwd(i, dstate):
            n = nc - 1 - i
            r = _chunk_rows(n)
            r8 = _egl_rows(n)
            _, vjp = jax.vjp(_dn_state, u_sc[r, :], kcum_sc[r, :], qk_sc[r, :], qdec_sc[r, :], kdec_sc[r, :],
                             egl_sc[r8, :][0:1], st_ref[0, n])
            du, dkcum, dqk, dqdec, dkdec, degl, dprev = vjp((do_sc[r, :], dstate))
            for d_sc, val in zip(dloc[:5], (du, dkcum, dqk, dqdec, dkdec)):
                d_sc[r, :] = val
            dloc[5][r8, :] = jnp.broadcast_to(degl, (8, DN_DH))
            return dprev

        lax.fori_loop(0, nc, state_bwd, jnp.zeros((DN_DH, DN_DH), F32))

        def local_bwd(i, carry):
            n = i * DN_BATCH
            r = _chunk_rows(n, DN_BATCH)
            _, vjp = jax.vjp(_dn_local, *_dn_local_inputs(n, qn_sc, kn_sc, vv_sc, gb_ref))
            cts = tuple(d_sc[r, :].reshape(DN_BATCH, DN_CHUNK, d_sc.shape[-1]) for d_sc in dloc[:5])
            cts += (dloc[5][_egl_rows(n, DN_BATCH), :].reshape(DN_BATCH, 8, DN_DH)[:, 0:1],)
            dqc, dkc, dvc, dg, dbeta = vjp(cts)
            for d_sc, val in zip((dloc[0], dloc[1], dloc[3]), (dqc, dkc, dvc)):
                d_sc[r, :] = val.reshape(DN_BATCH * DN_CHUNK, DN_DH)
            dgb_ref[0, r, :] = jnp.concatenate([dg, dbeta], axis=2).reshape(DN_BATCH * DN_CHUNK, 2)
            return carry

        lax.fori_loop(0, nc // DN_BATCH, local_bwd, 0)
        for x_ref, w_ref, pre, d_sc, dx_ref, dw_ref in ((q_ref, wq_ref, _dn_pre_q, dloc[0], dq_ref, dwq_ref),
                                                       (k_ref, wk_ref, _dn_pre_k, dloc[1], dk_ref, dwk_ref),
                                                       (v_ref, wv_ref, _silu, dloc[3], dv_ref, dwv_ref)):
            x_in = x_ref[...].astype(F32)
            _, vjp = jax.vjp(pre, _conv(x_in, w_ref[...]))
            (dc,) = vjp(d_sc[...])
            dx, dws = _conv_bwd(x_in, w_ref[...], dc)
            dx_ref[...] = dx.astype(dx_ref.dtype)
            for j, dw in enumerate(dws):
                dw_ref[j:j + 1, :] = dw

    hblk = pl.BlockSpec((t, DN_DH), lambda h: (0, h))
    wout = pl.BlockSpec((4, DN_DH), lambda h: (0, h))
    return pl.pallas_call(
        body, grid=(DN_HEADS,),
        in_specs=proj_specs + w_specs + [gb_spec, pl.BlockSpec((1, DN_DH), lambda h: (0, 0)), hblk,
                                         pl.BlockSpec((1, nc, DN_DH, DN_DH), lambda h: (h, 0, 0, 0)), hblk] + loc_specs,
        out_specs=[hblk] * 4 + [wout] * 3 + [gb_spec, pl.BlockSpec((1, 1, DN_DH), lambda h: (h, 0, 0))],
        out_shape=[jax.ShapeDtypeStruct((t, BW), BF16)] * 4 + [jax.ShapeDtypeStruct((4, BW), F32)] * 3
        + [jax.ShapeDtypeStruct((DN_HEADS, t, 2), F32), jax.ShapeDtypeStruct((DN_HEADS, 1, DN_DH), F32)],
        scratch_shapes=[pltpu.VMEM((t, DN_DH), F32)] * 4 + _dn_loc_scratch(t, nc), name=name,
    )(proj, proj, proj, proj, conv_w, conv_w, conv_w, gb, ng, o, states, dy, *loc_saved)


MERGE_TM, MERGE_TN = 512, 512


def _merge_specs(t):
    tm, tn = min(MERGE_TM, t), MERGE_TN
    y_spec = pl.BlockSpec((tm, BW), lambda i, j: (i, 0))
    w_spec = pl.BlockSpec((3, BW, tn), lambda i, j: (0, 0, j))
    gate_specs = [pl.BlockSpec((tm, tn), functools.partial(lambda i, j, o: (i, o + j), o=(C_GATE + n * D_MODEL) // tn))
                  for n in range(3)]
    return tm, tn, [y_spec] * 3 + [w_spec] + gate_specs


def _merge_fwd(ys, wb, proj, name):
    t = proj.shape[0]
    tm, tn, in_specs = _merge_specs(t)

    def body(y0, y1, y2, w_ref, g0, g1, g2, o_ref):
        acc = jnp.zeros((tm, tn), F32)
        for n, (y, g) in enumerate(((y0, g0), (y1, g1), (y2, g2))):
            acc = acc + _dot(y[...].astype(BF16), w_ref[n], NN) * _sigmoid(g[...].astype(F32))
        o_ref[...] = acc.astype(o_ref.dtype)

    return pl.pallas_call(
        body, grid=(t // tm, D_MODEL // tn), in_specs=in_specs,
        out_specs=pl.BlockSpec((tm, tn), lambda i, j: (i, j)),
        out_shape=jax.ShapeDtypeStruct((t, D_MODEL), BF16), name=name,
    )(*ys, wb, proj, proj, proj)


def _merge_bwd(ys, wb, proj, dmerged, name):
    t = proj.shape[0]
    tm, tn, in_specs = _merge_specs(t)

    def body(y0, y1, y2, w_ref, g0, g1, g2, dm_ref, dg_ref, dt_ref):
        dm = dm_ref[...]
        for n, (y, g) in enumerate(((y0, g0), (y1, g1), (y2, g2))):
            tn_ = _dot(y[...].astype(BF16), w_ref[n], NN)
            sg = _sigmoid(g[...].astype(F32))
            dg_ref[n] = (dm * tn_ * sg * (1.0 - sg)).astype(dg_ref.dtype)
            dt_ref[n] = (dm * sg).astype(dt_ref.dtype)

    o3 = pl.BlockSpec((3, tm, tn), lambda i, j: (0, i, j))
    return pl.pallas_call(
        body, grid=(t // tm, D_MODEL // tn), in_specs=in_specs + [pl.BlockSpec((tm, tn), lambda i, j: (i, j))],
        out_specs=[o3, o3], out_shape=[jax.ShapeDtypeStruct((3, t, D_MODEL), BF16)] * 2, name=name,
    )(*ys, wb, proj, proj, proj, dmerged)


def _prep_rows(sp):
    z4, z112 = jnp.zeros((1, 4), F32), jnp.zeros((1, 112), F32)
    bias_row = jnp.concatenate([sp['b_fox_f'], z4, sp['dn_dt_bias'], z112], axis=1)
    alog_row = jnp.concatenate([jnp.zeros((1, 12), F32), sp['dn_a_log'], z112], axis=1)
    return bias_row, alog_row


def _sc_fwd_tile(sb, sc, sv, w):
    sb, sc, sv = sb.astype(F32), sc.astype(F32), sv.astype(F32)
    return (sb * _conv(sc * sv, w),)


def _ffn_act(ug, uv):
    return _silu(ug) * uv


W_IN_COLS = [(C_FQ, 0, 1536), (C_SB, 1544, 1536), (C_DQ, 3080, 1536), (C_DZ, 4624, 512), (C_GATE, 5136, 3072),
             (C_SMALL, 1536, 8), (C_SMALL + 8, 4616, 8)]
W_IN_SHARD = IN_ORIG // N_DEV


def _w_in_segments():
    out = []
    for d, o, w in W_IN_COLS:
        end = o + w
        while o < end:
            k = o // W_IN_SHARD
            n = min(end, (k + 1) * W_IN_SHARD) - o
            out.append((d, k, o - k * W_IN_SHARD, n))
            o, d = o + n, d + n
    return out


def _w_in_assemble(g, name):
    tm = 256

    def body(g_ref, o_ref):
        for d, k, s, n in _w_in_segments():
            o_ref[:, d:d + n] = g_ref[k, :, s:s + n]
        o_ref[:, IN_ORIG:IN_P] = jnp.zeros((tm, IN_P - IN_ORIG), o_ref.dtype)

    return pl.pallas_call(
        body, grid=(D_MODEL // tm,),
        in_specs=[pl.BlockSpec((N_DEV, tm, W_IN_SHARD), lambda i: (0, i, 0))],
        out_specs=pl.BlockSpec((tm, IN_P), lambda i: (i, 0)),
        out_shape=jax.ShapeDtypeStruct((D_MODEL, IN_P), g.dtype), name=name,
    )(g)


W_IN_SPLIT_TM = 128


def _w_in_split_by_device(g, name):
    tm = W_IN_SPLIT_TM

    def body(g_ref, own_ref, send_ref):
        me = 4 * lax.axis_index("x") + 2 * lax.axis_index("y") + lax.axis_index("c")
        for d, k, s, n in _w_in_segments():
            val = g_ref[:, d:d + n]
            send_ref[k, :, s:s + n] = val.astype(send_ref.dtype)

            @pl.when(me == k)
            def _():
                own_ref[:, s:s + n] = val

    return pl.pallas_call(
        body, grid=(D_MODEL // tm,), in_specs=[pl.BlockSpec((tm, IN_P), lambda i: (i, 0))],
        out_specs=[pl.BlockSpec((tm, W_IN_SHARD), lambda i: (i, 0)), pl.BlockSpec((N_DEV, tm, W_IN_SHARD), lambda i: (0, i, 0))],
        out_shape=[jax.ShapeDtypeStruct((D_MODEL, W_IN_SHARD), F32), jax.ShapeDtypeStruct((N_DEV, D_MODEL, W_IN_SHARD), BF16)],
        name=name,
    )(g)


def _layer_fwd(x, p_i, w, sp, tag, late_weights=None):
    t = x.shape[0]
    sv = {'x': x}
    (hn,) = _ew(lambda a, g: (_rms(a, g),), [x], [sp['g_mix']], [(D_MODEL, BF16)], name=f'rms_mix_{tag}')
    proj = _mm(hn, w['w_in'], 'nn', [BF16], name=f'in_proj_{tag}')
    small = _mm(hn, w['w_in'][:, C_SMALL:], 'nn', [F32], name=f'in_small_{tag}')
    bias_row, alog_row = _prep_rows(sp)
    aux = _prep_fwd(small, bias_row, alog_row, f'prep_{tag}')
    cf = aux[:, :FOX_HEADS].T
    cfc, cfr = cf[:, :, None], cf[:, None, :]
    y_fox, lse = _att_fwd(proj, sp['fox_q_gain'], sp['fox_k_gain'], cfc, cfr, f'fox_fwd_{tag}')
    (y_sc,) = _cb(_sc_fwd_tile, [(proj, C_SB), (proj, C_SC), (proj, C_SV)], [(w['sc_conv_w'], 0)], [F32], [],
                  tc=256, nblk=2, name=f'sc_fwd_{tag}')
    gb = jnp.stack([aux[:, 12:16].T, aux[:, 8:12].T], axis=-1)
    y_dn, o_dn, states, *dn_loc = _dn_fwd(proj, w['dn_conv_w'], gb, sp['dn_norm_gain'], f'dn_fwd_{tag}')
    ys = (y_fox, y_sc, y_dn)
    if late_weights is not None:
        w = {**w, **late_weights(y_dn)}
    merged = _merge_fwd(ys, w['w_branch'], proj, f'merge_fwd_{tag}')
    x1 = _mm(merged, w['w_o'], 'nn', [F32], epi=lambda acc, r: (acc + r,), epi_args=(x,), name=f'o_proj_{tag}')
    (hf,) = _ew(lambda a, g: (_rms(a, g),), [x1], [sp['g_ffn']], [(D_MODEL, BF16)], name=f'rms_ffn_{tag}')
    up = _mm(hf, w['w_up'], 'nt', [F32], name=f'up_proj_{tag}')
    (act,) = _cb(lambda ug, uv, wg, wv: (_ffn_act(_conv(ug, wg), _conv(uv, wv)),), [(up, 0), (up, D_FF)],
                 [(w['ffn_conv_w'], 0), (w['ffn_conv_w'], D_FF)], [BF16], [], tc=256, nblk=D_FF // 256, name=f'ffn_act_{tag}')
    x2 = _mm(act, w['w_down'], 'nn', [F32], epi=lambda acc, r: (acc + r,), epi_args=(x1,), name=f'down_proj_{tag}')
    (hp,) = _ew(lambda a, g: (_rms(a, g),), [x2], [sp['g_ple']], [(D_MODEL, BF16)], name=f'rms_ple_{tag}')
    gp = _mm(hp, w['w_ple_gate'], 'nn', [F32], name=f'ple_gate_{tag}')
    x3 = _mm(p_i, w['w_ple'], 'nn', [F32], epi=lambda acc, g, r: (r + _sigmoid(g) * acc,), epi_args=(gp, x2), name=f'ple_{tag}')
    sv.update(hn=hn, proj=proj, small=small, aux=aux, cfc=cfc, cfr=cfr, lse=lse, ys=ys, gb=gb, o_dn=o_dn,
              states=states, merged=merged, x1=x1, hf=hf, up=up, act=act, x2=x2, hp=hp, gp=gp, p=p_i,
              bias_row=bias_row, alog_row=alog_row, w=w, dn_loc=dn_loc)
    return x3, sv


def _rms_bwd(x, g, dh, dres, name):
    def fn(xv, dhv, dr, gv):
        _, vjp = jax.vjp(_rms, xv, gv)
        dx, dg = vjp(dhv)
        return dr + dx, dg
    return _ew(fn, [x, dh, dres], [g], [(D_MODEL, F32)], [(1, D_MODEL)], name=name)


def _layer_bwd(dx3, sv, sp, tag, early_grads=None, last_grad=None):
    t = dx3.shape[0]
    w = sv['w']
    g = {}
    def ple_epi(acc, gpv, d):
        s = _sigmoid(gpv)
        return d * acc * s * (1.0 - s), d * s
    dgp, de = _mm(sv['p'], w['w_ple'], 'nn', [BF16, BF16], epi=ple_epi, epi_args=(sv['gp'], dx3), name=f'ple_bwd_{tag}')
    g['w_ple'] = _mm(sv['p'], de, 'tn', [F32], name=f'd_w_ple_{tag}')
    g['w_ple_gate'] = _mm(sv['hp'], dgp, 'tn', [F32], name=f'd_w_ple_gate_{tag}')
    dhp = _mm(dgp, w['w_ple_gate'], 'nt', [F32], name=f'd_hp_{tag}')
    dx2, g['g_ple'] = _rms_bwd(sv['x2'], sp['g_ple'], dhp, dx3, f'rms_ple_bwd_{tag}')
    dact = _mm(dx2, w['w_down'], 'nt', [BF16], name=f'd_act_{tag}')
    g['w_down'] = _mm(sv['act'], dx2, 'tn', [F32], name=f'd_w_down_{tag}')

    def ffn_bwd_tile(ug, uv, da, wg, wv):
        cg, cv = _conv(ug, wg), _conv(uv, wv)
        _, vjp = jax.vjp(_ffn_act, cg, cv)
        dcg, dcv = vjp(da.astype(F32))
        dug, dwg = _conv_bwd(ug, wg, dcg)
        duv, dwv = _conv_bwd(uv, wv, dcv)
        return dug, duv, jnp.concatenate(dwg, axis=0), jnp.concatenate(dwv, axis=0)
    dupg, dupv, dwg, dwv = _cb(ffn_bwd_tile, [(sv['up'], 0), (sv['up'], D_FF), (dact, 0)],
                               [(w['ffn_conv_w'], 0), (w['ffn_conv_w'], D_FF)], [BF16, BF16], [3, 3], tc=256,
                               nblk=D_FF // 256, name=f'ffn_act_bwd_{tag}')
    dup = jnp.concatenate([dupg, dupv], axis=1)
    g['ffn_conv_w'] = jnp.concatenate([dwg, dwv], axis=1)
    g['w_up'] = _mm(dup, sv['hf'], 'tn', [F32], name=f'd_w_up_{tag}')
    dhf = _mm(dup, w['w_up'], 'nn', [F32], name=f'd_hf_{tag}')
    dx1, g['g_ffn'] = _rms_bwd(sv['x1'], sp['g_ffn'], dhf, dx2, f'rms_ffn_bwd_{tag}')
    dmerged = _mm(dx1, w['w_o'], 'nt', [F32], name=f'd_merged_{tag}')
    g['w_o'] = _mm(sv['merged'], dx1, 'tn', [F32], name=f'd_w_o_{tag}')
    dgate, dtn = _merge_bwd(sv['ys'], w['w_branch'], sv['proj'], dmerged, f'merge_bwd_{tag}')
    dys, dwb = [], []
    for n in range(3):
        dys.append(_mm(dtn[n], w['w_branch'][n], 'nt', [F32], name=f'd_y{n}_{tag}'))
        dwb.append(_mm(sv['ys'][n], dtn[n], 'tn', [F32], name=f'd_w_branch{n}_{tag}'))
    g['w_branch'] = jnp.stack(dwb)
    if early_grads is not None:
        sp = early_grads(g, sp)
    ddq, ddk, ddv, ddz, dwq, dwk, dwv_, dgb, dng = _dn_bwd(sv['proj'], w['dn_conv_w'], sv['gb'], sp['dn_norm_gain'],
                                                           sv['o_dn'], sv['states'], sv['dn_loc'], dys[2], f'dn_bwd_{tag}')
    g['dn_conv_w'] = jnp.concatenate([dwq, dwk, dwv_], axis=1)
    g['dn_norm_gain'] = jnp.sum(dng, axis=0)
    def sc_bwd_tile(sb, sc, svv, dy, wv):
        sb, sc, svv = sb.astype(F32), sc.astype(F32), svv.astype(F32)
        u = sc * svv
        dsb = dy * _conv(u, wv)
        du, dws = _conv_bwd(u, wv, dy * sb)
        return dsb, du * svv, du * sc, jnp.concatenate(dws, axis=0)
    dsb, dsc, dsv, g['sc_conv_w'] = _cb(sc_bwd_tile, [(sv['proj'], C_SB), (sv['proj'], C_SC), (sv['proj'], C_SV), (dys[1], 0)],
                                        [(w['sc_conv_w'], 0)], [BF16, BF16, BF16], [3], tc=256, nblk=2, name=f'sc_bwd_{tag}')
    dfq, dfk, dfv, dcfc, dcfr, dqg, dkg = _att_bwd(sv['proj'], sp['fox_q_gain'], sp['fox_k_gain'], sv['cfc'], sv['cfr'],
                                                   sv['lse'], sv['ys'][0], dys[0], f'fox_bwd_{tag}')
    g['fox_q_gain'] = jnp.sum(dqg, axis=0)
    g['fox_k_gain'] = jnp.sum(dkg, axis=0)
    dcf = (dcfc[:, :, 0] + dcfr[:, 0, :]).T
    daux = jnp.concatenate([dcf, dgb[:, :, 1].T, dgb[:, :, 0].T, jnp.zeros((t, 112), F32)], axis=1)
    dsmall, dbias, dalog = _prep_bwd(sv['small'], sv['bias_row'], sv['alog_row'], daux, f'prep_bwd_{tag}')
    g['b_fox_f'] = dbias[:, 0:8]
    g['dn_dt_bias'] = dbias[:, 12:16]
    g['dn_a_log'] = dalog[:, 12:16]
    dproj = jnp.concatenate([dfq, dfk, dfv, dsb, dsc, dsv, ddq, ddk, ddv, ddz, dgate[0], dgate[1], dgate[2], dsmall], axis=1)
    g['w_in'] = _mm(sv['hn'], dproj, 'tn', [F32], name=f'd_w_in_{tag}')
    if last_grad is not None:
        sp = last_grad(g, sp)
    dhn = _mm(dproj, w['w_in'], 'nt', [F32], epi=lambda acc, z: (acc + z,),
              epi_args=(sp.get('zero_row', jnp.zeros((1, D_MODEL), F32)),), name=f'd_hn_{tag}')
    dx, g['g_mix'] = _rms_bwd(sv['x'], sp['g_mix'], dhn, dx1, f'rms_mix_bwd_{tag}')
    return dx, g


def _loss_bwd(y, target, name):
    inv = 1.0 / y.shape[1]

    def fn(yv, tv):
        err = yv - tv
        return err * inv, jnp.zeros((8, 128), F32) + 0.5 * inv * jnp.sum(err * err)
    return _ew(fn, [y, target], [], [(y.shape[1], F32)], [(8, 128)], name=name)


PACK_W = 1024
FULL_SHAPE = {'w_in': (D_MODEL, IN_ORIG), 'w_branch': (3, BW, D_MODEL), 'w_o': (D_MODEL, D_MODEL), 'w_up': (2 * D_FF, D_MODEL),
              'w_down': (D_FF, D_MODEL), 'w_ple_gate': (D_MODEL, D_MODEL), 'w_ple': (PLE_DIM, D_MODEL),
              'sc_conv_w': (3, BW), 'dn_conv_w': (4, 3 * BW), 'ffn_conv_w': (3, 2 * D_FF)}
SMALL_SHAPE = {'g_mix': D_MODEL, 'b_fox_f': FOX_HEADS, 'fox_q_gain': FOX_DH, 'fox_k_gain': FOX_DH, 'dn_a_log': DN_HEADS,
               'dn_dt_bias': DN_HEADS, 'dn_norm_gain': DN_DH, 'g_ffn': D_MODEL, 'g_ple': D_MODEL}


def _shard_shape(name):
    s = list(FULL_SHAPE[name])
    s[SHARDED[name]] //= N_DEV
    return tuple(s)


def _full_from_gathered(g, name):
    sh, ax = _shard_shape(name), SHARDED[name]
    blocks = jnp.moveaxis(g, 0, ax)
    return blocks.reshape(sh[:ax] + (N_DEV * sh[ax],) + sh[ax + 1:])


def _by_dest(full, name):
    sh, ax = _shard_shape(name), SHARDED[name]
    return jnp.moveaxis(full.reshape(sh[:ax] + (N_DEV, sh[ax]) + sh[ax + 1:]), ax, 0)


def _flat_pack(arrs):
    flat = jnp.concatenate([a.reshape(-1).astype(F32) for a in arrs])
    rows = -(-flat.shape[0] // (8 * PACK_W)) * 8
    return jnp.pad(flat, (0, rows * PACK_W - flat.shape[0])).reshape(rows, PACK_W)


def _flat_unpack(pack, shapes):
    flat, out, off = pack.reshape(-1), [], 0
    for s in shapes:
        n = 1
        for d in s:
            n *= d
        out.append(flat[off:off + n].reshape(s))
        off += n
    return out


SMALL_SHAPES = [(DEPTH, SMALL_SHAPE[n]) for n in SMALL]
SMALL_LOSS_AT = sum(DEPTH * SMALL_SHAPE[n] for n in SMALL)
CONV_SHARD_SHAPES = [(DEPTH,) + _shard_shape(n) for n in CONVW]
CONV_FULL_SHAPES = [(DEPTH,) + FULL_SHAPE[n] for n in CONVW]


def _pick_tm(m, width):
    best = None
    for tm in range(16, m + 1, 16):
        if m % tm == 0 and tm * width * 4 <= (1 << 20):
            best = tm
    return best if best is not None else m


HBM_SPEC = pl.BlockSpec(memory_space=pltpu.HBM)


def _place():
    x, y, c = lax.axis_index("x"), lax.axis_index("y"), lax.axis_index("c")
    return x, y, c, [(1 - x, y), (x, 1 - y), (1 - x, 1 - y)]


def _all_gather(arrs, name):
    n = len(arrs)

    def body(*refs):
        ins, outs = refs[:n], refs[n:2 * n]
        send_sems, recv_sems = refs[2 * n:]
        x, y, c, chips = _place()
        me, sibling = (x, y, c), (x, y, 1 - c)

        def block(a, p):
            return outs[a].at[4 * p[0] + 2 * p[1] + p[2]]

        def copy(a, k, blk, to, src=None):
            return pltpu.make_async_remote_copy(src_ref=block(a, blk) if src is None else src, dst_ref=block(a, blk),
                                                send_sem=send_sems.at[a, k], recv_sem=recv_sems.at[a, k],
                                                device_id=to, device_id_type=MESH)

        first, passed = [], []
        for a in range(n):
            first.append(copy(a, 0, me, sibling, src=ins[a]))
            first += [copy(a, 1 + j, me, (*chip, c), src=ins[a]) for j, chip in enumerate(chips)]
        for cp in first:
            cp.start()
        for j, chip in enumerate(chips):
            for a in range(n):
                copy(a, 1 + j, (*chip, c), me).wait_recv()
                fwd = copy(a, 4 + j, (*chip, c), sibling)
                fwd.start()
                passed.append(fwd)
        for a in range(n):
            copy(a, 0, sibling, me).wait_recv()
            for j, chip in enumerate(chips):
                copy(a, 4 + j, (*chip, 1 - c), me).wait_recv()
        for cp in first + passed:
            cp.wait_send()

    gathered = pl.pallas_call(
        body, in_specs=[HBM_SPEC] * n, out_specs=[HBM_SPEC] * n,
        out_shape=[jax.ShapeDtypeStruct((N_DEV,) + a.shape, a.dtype) for a in arrs],
        scratch_shapes=[pltpu.SemaphoreType.DMA((n, 7)), pltpu.SemaphoreType.DMA((n, 7))], name=name,
    )(*arrs)
    me = 4 * lax.axis_index("x") + 2 * lax.axis_index("y") + lax.axis_index("c")
    return [lax.dynamic_update_index_in_dim(g, a, me, 0) for g, a in zip(gathered, arrs)]


SEM_SPEC = pl.BlockSpec(memory_space=pltpu.SEMAPHORE)
DATAFLOW = pltpu.SideEffectType.DATAFLOW_SIDE_EFFECTING
ALL_PEERS = (1, 2, 3, 4, 5, 6, 7)
NEAR_PEERS = (1, 4, 2, 6)


def _direct_copies(src_refs, zone_refs, send_sems, recv_sems, per_dest, peers):
    x, y, c, _ = _place()
    me = 4 * x + 2 * y + c
    cps = []
    for a, (src, zone) in enumerate(zip(src_refs, zone_refs)):
        for r, bits in enumerate(peers):
            px = 1 - x if bits & 4 else x
            py = 1 - y if bits & 2 else y
            pc = 1 - c if bits & 1 else c
            cps.append(pltpu.make_async_remote_copy(
                src_ref=src.at[4 * px + 2 * py + pc] if per_dest else src, dst_ref=zone.at[me],
                send_sem=send_sems.at[a * len(peers) + r], recv_sem=recv_sems.at[a * len(peers) + r],
                device_id=(px, py, pc), device_id_type=MESH))
    return cps


def _direct_start(srcs, per_dest, after, name, peers=ALL_PEERS):
    n = len(srcs)
    zones = [lax.empty((N_DEV,) + (s.shape[1:] if per_dest else s.shape), s.dtype) for s in srcs]

    def body(*refs):
        src_refs, zone_refs = refs[:n], refs[n:2 * n]
        send_sems, recv_sems = refs[2 * n + 1:2 * n + 3]
        for cp in _direct_copies(src_refs, zone_refs, send_sems, recv_sems, per_dest, peers):
            cp.start()
        refs[-1][...] = jnp.zeros_like(refs[-1])

    sems = pltpu.SemaphoreType.DMA((n * len(peers),))
    res = pl.pallas_call(
        body, name=name, in_specs=[HBM_SPEC] * (2 * n + 1),
        out_shape=[sems, sems] + [pltpu.HBM(s.shape, s.dtype) for s in srcs] + [pltpu.HBM(z.shape, z.dtype) for z in zones]
        + [jax.ShapeDtypeStruct((8, 128), F32)],
        out_specs=[SEM_SPEC, SEM_SPEC] + [HBM_SPEC] * (2 * n) + [pl.BlockSpec(memory_space=pltpu.VMEM)],
        input_output_aliases={i: 2 + i for i in range(2 * n)},
        compiler_params=pltpu.CompilerParams(has_side_effects=DATAFLOW),
    )(*[pltpu.with_memory_space_constraint(s, pltpu.HBM) for s in srcs],
      *[pltpu.with_memory_space_constraint(z, pltpu.HBM) for z in zones], after)
    return (res[0], res[1], list(res[2:2 + n]), list(res[2 + n:2 + 2 * n]), per_dest, peers), res[-1][0:1, 0:1]


def _direct_wait(started, after, name):
    send_sems, recv_sems, srcs, zones, per_dest, peers = started
    n = len(srcs)

    def body(*refs):
        src_refs, zone_refs = refs[:n], refs[n:2 * n]
        for cp in _direct_copies(src_refs, zone_refs, refs[2 * n], refs[2 * n + 1], per_dest, peers):
            cp.wait_send()
            cp.wait_recv()

    res = pl.pallas_call(
        body, name=name, in_specs=[HBM_SPEC] * (2 * n) + [SEM_SPEC, SEM_SPEC, HBM_SPEC],
        out_shape=[pltpu.HBM(s.shape, s.dtype) for s in srcs] + [pltpu.HBM(z.shape, z.dtype) for z in zones],
        out_specs=[HBM_SPEC] * (2 * n), input_output_aliases={i: i for i in range(2 * n)},
        compiler_params=pltpu.CompilerParams(has_side_effects=DATAFLOW),
    )(*srcs, *zones, send_sems, recv_sems, after)
    return list(res[n:])


def _gather_finish(zones, name):
    n = len(zones)

    def body(*refs):
        zone_in, zone_out = refs[:n], refs[n:2 * n]
        send_sems, recv_sems = refs[2 * n:]
        x, y, c, chips = _place()
        cps = []
        for a in range(n):
            for j, (cx, cy) in enumerate(chips):
                blk = 4 * cx + 2 * cy + c
                cp = pltpu.make_async_remote_copy(src_ref=zone_in[a].at[blk], dst_ref=zone_out[a].at[blk],
                                                  send_sem=send_sems.at[a, j], recv_sem=recv_sems.at[a, j],
                                                  device_id=(x, y, 1 - c), device_id_type=MESH)
                cp.start()
                cps.append(cp)
        for cp in cps:
            cp.wait()

    return pl.pallas_call(
        body, in_specs=[HBM_SPEC] * n, out_specs=[HBM_SPEC] * n,
        out_shape=[jax.ShapeDtypeStruct(z.shape, z.dtype) for z in zones], input_output_aliases={a: a for a in range(n)},
        scratch_shapes=[pltpu.SemaphoreType.DMA((n, 3)), pltpu.SemaphoreType.DMA((n, 3))], name=name,
    )(*zones)


def _adamw(w, g, m, v):
    m = ADAM_B1 * m + (1.0 - ADAM_B1) * g
    v = ADAM_B2 * v + (1.0 - ADAM_B2) * jnp.square(g)
    m_hat = m / (1.0 - ADAM_B1 ** ADAM_STEP)
    v_hat = v / (1.0 - ADAM_B2 ** ADAM_STEP)
    delta = -ADAM_LR * (m_hat / (jnp.sqrt(v_hat) + ADAM_EPS) + ADAM_WD * w)
    return delta, m, v


def _adamw_layer(w, m, v, own, parts, layer, prev, name):
    _, rows, c = w.shape
    tm = _pick_tm(rows, c)

    def body(w_ref, m_ref, v_ref, own_ref, parts_ref, *rest):
        g = own_ref[...]
        for j in range(parts_ref.shape[0]):
            g = g + parts_ref[j].astype(F32)
        delta, m2, v2 = _adamw(w_ref[...], g, m_ref[...], v_ref[...])
        g_ref, d_ref, m2_ref, v2_ref = rest[-4:]
        g_ref[...], d_ref[...], m2_ref[...], v2_ref[...] = g, delta, m2, v2

    wspec = pl.BlockSpec((None, tm, c), lambda i: (layer, i, 0))
    prev = list(prev) if prev is not None else []
    return pl.pallas_call(
        body, grid=(rows // tm,),
        in_specs=[wspec] * 3 + [pl.BlockSpec((tm, c), lambda i: (i, 0)), pl.BlockSpec((parts.shape[0], tm, c), lambda i: (0, i, 0))]
        + [HBM_SPEC] * len(prev),
        out_specs=[wspec] * 4, out_shape=[jax.ShapeDtypeStruct(w.shape, F32)] * 4,
        input_output_aliases={5 + k: k for k in range(len(prev))}, name=name,
    )(w, m, v, own, parts, *prev)


W_IN_ADAMW_TM = 128


def _adamw_w_in(w_t, m_t, v_t, grads, name):
    tm = W_IN_ADAMW_TM
    ng = [p.shape[0] for _, p in grads]

    def body(w_ref, m_ref, v_ref, *rest):
        outs = rest[2 * DEPTH:]
        for layer in range(DEPTH):
            own_ref, parts_ref = rest[2 * layer], rest[2 * layer + 1]
            g = own_ref[...]
            for j in range(ng[layer]):
                g = g + parts_ref[j].astype(F32)
            delta, m2, v2 = _adamw(w_ref[:, layer, :].T, g, m_ref[:, layer, :].T, v_ref[:, layer, :].T)
            for o_ref, val in zip(outs, (g, delta, m2, v2)):
                o_ref[:, layer, :] = val.T

    tspec = pl.BlockSpec((W_IN_SHARD, DEPTH, tm), lambda i: (0, 0, i))
    gspecs = []
    for _, parts in grads:
        gspecs += [pl.BlockSpec((tm, W_IN_SHARD), lambda i: (i, 0)), pl.BlockSpec((parts.shape[0], tm, W_IN_SHARD), lambda i: (0, i, 0))]
    return pl.pallas_call(
        body, grid=(D_MODEL // tm,), in_specs=[tspec] * 3 + gspecs, out_specs=[tspec] * 4,
        out_shape=[jax.ShapeDtypeStruct(w_t.shape, F32)] * 4, name=name,
    )(w_t, m_t, v_t, *[x for pair in grads for x in pair])


def _adamw_call(w, m, v, gparts, name):
    def fn(wv, mv, vv, *gs):
        g = gs[0].astype(F32)
        for gp in gs[1:]:
            g = g + gp.astype(F32)
        delta, m2, v2 = _adamw(wv, g, mv, vv)
        return g, delta, m2, v2
    rows, c = w.shape
    return _ew(fn, [w, m, v] + list(gparts), [], [(c, F32)] * 4, tm=_pick_tm(rows, c), name=name)


def kernel(x, p, g_mix, w_in, b_fox_f, fox_q_gain, fox_k_gain, sc_conv_w, dn_conv_w, dn_a_log, dn_dt_bias,
           dn_norm_gain, w_branch, w_o, g_ffn, w_up, ffn_conv_w, w_down, g_ple, w_ple_gate, w_ple, loss_target,
           m_g_mix, m_w_in, m_b_fox_f, m_fox_q_gain, m_fox_k_gain, m_sc_conv_w, m_dn_conv_w, m_dn_a_log,
           m_dn_dt_bias, m_dn_norm_gain, m_w_branch, m_w_o, m_g_ffn, m_w_up, m_ffn_conv_w, m_w_down, m_g_ple,
           m_w_ple_gate, m_w_ple, v_g_mix, v_w_in, v_b_fox_f, v_fox_q_gain, v_fox_k_gain, v_sc_conv_w, v_dn_conv_w,
           v_dn_a_log, v_dn_dt_bias, v_dn_norm_gain, v_w_branch, v_w_o, v_g_ffn, v_w_up, v_ffn_conv_w, v_w_down,
           v_g_ple, v_w_ple_gate, v_w_ple):
    return _step(x, p, g_mix, w_in, b_fox_f, fox_q_gain, fox_k_gain, sc_conv_w, dn_conv_w, dn_a_log, dn_dt_bias,
                 dn_norm_gain, w_branch, w_o, g_ffn, w_up, ffn_conv_w, w_down, g_ple, w_ple_gate, w_ple, loss_target,
                 m_g_mix, m_w_in, m_b_fox_f, m_fox_q_gain, m_fox_k_gain, m_sc_conv_w, m_dn_conv_w, m_dn_a_log,
                 m_dn_dt_bias, m_dn_norm_gain, m_w_branch, m_w_o, m_g_ffn, m_w_up, m_ffn_conv_w, m_w_down, m_g_ple,
                 m_w_ple_gate, m_w_ple, v_g_mix, v_w_in, v_b_fox_f, v_fox_q_gain, v_fox_k_gain, v_sc_conv_w,
                 v_dn_conv_w, v_dn_a_log, v_dn_dt_bias, v_dn_norm_gain, v_w_branch, v_w_o, v_g_ffn, v_w_up,
                 v_ffn_conv_w, v_w_down, v_g_ple, v_w_ple_gate, v_w_ple)


def _step(*args):
    names = ['x', 'p'] + WEIGHTS + ['loss_target'] + ['m_' + n for n in WEIGHTS] + ['v_' + n for n in WEIGHTS]
    assert len(args) == len(names)
    a = dict(zip(names, args))
    for n, perm in HELD_TRANSPOSED.items():
        for k in (n, 'm_' + n, 'v_' + n):
            a[k] = jnp.transpose(a[k], perm)
    x, target = a['x'][0], a['loss_target'][0]
    p = a['p'][:, 0]
    dev = 4 * lax.axis_index("x") + 2 * lax.axis_index("y") + lax.axis_index("c")

    LATE = [n for n in BIG if n != 'w_in']
    sps = [{n: a[n][layer][None, :] for n in SMALL} for layer in range(DEPTH)]
    shards = [{n: a[n][layer].astype(BF16) for n in BIG} for layer in range(DEPTH)]

    def by_device(full, n):
        by_dest = _by_dest(full, n)
        return lax.dynamic_index_in_dim(by_dest, dev, axis=0, keepdims=False), by_dest.astype(BF16)

    def blank(zone):
        return lax.dynamic_update_index_in_dim(zone, jnp.zeros(zone.shape[1:], zone.dtype), dev, 0)

    w_in_all, conv_all = _all_gather([shards[0]['w_in'], _flat_pack([a[n] for n in CONVW])], 'gather_w_in_l0')
    conv_by_dev = [_flat_unpack(conv_all[d], CONV_SHARD_SHAPES) for d in range(N_DEV)]
    conv_full = {n: jnp.concatenate([conv_by_dev[d][i] for d in range(N_DEV)], axis=2) for i, n in enumerate(CONVW)}
    late0_started, zero = _direct_start([shards[0][n] for n in LATE], False, w_in_all, 'weights_rest_l0_start', NEAR_PEERS)
    w1_started, zero = _direct_start([shards[1][n] for n in BIG], False, late0_started[2][0], 'weights_l1_start', NEAR_PEERS)
    sps[0]['g_mix'] = sps[0]['g_mix'] + zero

    def late_weights0(after):
        zones = _gather_finish(_direct_wait(late0_started, after, 'weights_rest_l0_wait'), 'weights_rest_l0_finish')
        return {n: _full_from_gathered(lax.dynamic_update_index_in_dim(z, shards[0][n], dev, 0), n) for n, z in zip(LATE, zones)}

    w0 = {n: conv_full[n][0] for n in CONVW}
    w0['w_in'] = _w_in_assemble(w_in_all, 'w_in_assemble_l0')
    h, sv0 = _layer_fwd(x, p[0], w0, sps[0], 'l0', late_weights0)
    zones = [lax.dynamic_update_index_in_dim(z, shards[1][n], dev, 0)
             for n, z in zip(BIG, _gather_finish(_direct_wait(w1_started, h, 'weights_l1_wait'), 'weights_l1_finish'))]
    w1 = {n: _full_from_gathered(z, n) for n, z in zip(BIG, zones) if n != 'w_in'}
    w1.update({n: conv_full[n][1] for n in CONVW})
    w1['w_in'] = _w_in_assemble(zones[list(BIG).index('w_in')], 'w_in_assemble_l1')
    h, sv1 = _layer_fwd(h, p[1], w1, sps[1], 'l1')
    dh, loss_part = _loss_bwd(h, target, 'loss')
    grads = [None] * DEPTH
    dh, grads[1] = _layer_bwd(dh, sv1, sps[1], 'l1')
    owns1, sends1 = {}, {}
    for n in BIG:
        owns1[n], sends1[n] = _w_in_split_by_device(grads[1][n], 'w_in_grad_split_l1') if n == 'w_in' else by_device(grads[1][n], n)
    g1_started, zero = _direct_start([sends1[n] for n in BIG], True, dh, 'grads_l1_start')
    early = {}

    def early_grads0(g, sp):
        owns, sends = zip(*[by_device(g[n], n) for n in LATE])
        early['started'], zero = _direct_start(list(sends), True, g['w_branch'], 'grads_rest_l0_start')
        early['owns'] = dict(zip(LATE, owns))
        return {**sp, 'dn_norm_gain': sp['dn_norm_gain'] + zero}

    def last_grad0(g, sp):
        early['own_in'], send = _w_in_split_by_device(g['w_in'], 'w_in_grad_split_l0')
        early['in_started'], zero = _direct_start([send], True, g['w_in'], 'grads_w_in_l0_start')
        return {**sp, 'g_mix': sp['g_mix'] + zero, 'zero_row': jnp.zeros((1, D_MODEL), F32) + zero}

    dh, grads[0] = _layer_bwd(dh, sv0, {**sps[0], 'g_ple': sps[0]['g_ple'] + zero}, 'l0', early_grads0, last_grad0)
    grad_x = dh[None]
    own_in0, in0_started = early['own_in'], early['in_started']
    sent = in0_started[2][0]
    zones1 = dict(zip(BIG, _direct_wait(g1_started, sent, 'grads_l1_wait')))
    zones0 = dict(zip(LATE, _direct_wait(early['started'], sent, 'grads_rest_l0_wait')))

    def adamw(n, layer, own, zone, prev):
        rows = (-1, a[n].shape[-1])
        own = own.reshape(rows)
        view = lambda t: t.reshape((DEPTH,) + own.shape)
        return _adamw_layer(view(a[n]), view(a['m_' + n]), view(a['v_' + n]), own, blank(zone).reshape((N_DEV,) + own.shape),
                            layer, prev, f'adamw_{n}_l{layer}')

    out, done = {}, []
    for n in LATE:
        res = adamw(n, 0, early['owns'][n], zones0[n], adamw(n, 1, owns1[n], zones1[n], None))
        out[n] = [r.reshape(a[n].shape) for r in res]
        done.append(res[0][0, 0:1, 0:1])
    (zone_in0,) = _direct_wait(in0_started, jnp.concatenate(done + [dh[0:1, 0:1]], axis=1), 'grads_w_in_l0_wait')
    res = _adamw_w_in(*[jnp.transpose(a[k], (2, 0, 1)) for k in ('w_in', 'm_w_in', 'v_w_in')],
                      [(own_in0, blank(zone_in0)), (owns1['w_in'], blank(zones1['w_in']))], 'adamw_w_in')
    out['w_in'] = [jnp.transpose(r, (1, 2, 0)) for r in res]
    last_update = res[0]

    small_pack = _flat_pack([jnp.stack([grads[layer][n].reshape(-1) for layer in range(DEPTH)]) for n in SMALL] + [loss_part[0, 0]])
    conv_pack = _flat_pack([jnp.stack([grads[layer][n] for layer in range(DEPTH)]) for n in CONVW])
    small_started, _ = _direct_start([small_pack, conv_pack], False, small_pack, 'small_grads_start')
    small_all, conv_all = [lax.dynamic_update_index_in_dim(z, s, dev, 0) for z, s in
                           zip(_direct_wait(small_started, last_update, 'small_grads_wait'), (small_pack, conv_pack))]
    res = _adamw_call(_flat_pack([a[n] for n in SMALL]), _flat_pack([a['m_' + n] for n in SMALL]),
                      _flat_pack([a['v_' + n] for n in SMALL]), [small_all[d] for d in range(N_DEV)], 'adamw_replicated')
    loss = res[0].reshape(-1)[SMALL_LOSS_AT]
    for k, r in enumerate(res):
        for n, val in zip(SMALL, _flat_unpack(r, SMALL_SHAPES)):
            out.setdefault(n, [None] * 4)[k] = val
    (conv_sum,) = _ew(lambda *gs: (functools.reduce(lambda s, t: s + t, gs),), [conv_all[d] for d in range(N_DEV)], [],
                      [(PACK_W, F32)], tm=conv_pack.shape[0], name='conv_grad_sum')
    conv_own = [lax.dynamic_slice_in_dim(g, dev * (g.shape[2] // N_DEV), g.shape[2] // N_DEV, axis=2)
                for g in _flat_unpack(conv_sum, CONV_FULL_SHAPES)]
    res = _adamw_call(_flat_pack([a[n] for n in CONVW]), _flat_pack([a['m_' + n] for n in CONVW]),
                      _flat_pack([a['v_' + n] for n in CONVW]), [_flat_pack(conv_own)], 'adamw_conv')
    for k, r in enumerate(res):
        for n, val in zip(CONVW, _flat_unpack(r, CONV_SHARD_SHAPES)):
            out.setdefault(n, [None] * 4)[k] = val

    for n, perm in HELD_TRANSPOSED.items():
        out[n] = [jnp.transpose(r, perm) for r in out[n]]
    outs = [loss, grad_x]
    for k in range(4):
        outs += [out[n][k] for n in WEIGHTS]
    return tuple(outs)
```
